```python
import jax, jax.numpy as jnp
from jax import lax
import numpy as np

D_MODEL = 1024
BATCH = 16
SEQ = 2048
DEPTH = 2

N_META = 16
POOL_WINDOWS = (2, 4, 8, 16)
POOL_GROUP = 128
POOL_WIDTH = POOL_GROUP * len(POOL_WINDOWS)
N_HEADS = 16
QK_NOPE = 64
QK_ROPE = 32
V_DIM = 64
Q_RANK = 256
KV_RANK = 128
QK_DIM = QK_NOPE + QK_ROPE
ATT_WIDTH = N_HEADS * V_DIM
SM_SCALE = QK_DIM ** -0.5
ROPE_THETA = 10000.0
Q_BLOCK = 128
D_FF = -(-8 * D_MODEL // (3 * 256)) * 256
NORM_EPS = 1e-6
MASK_VALUE = -1e30
IN_SIZES = (POOL_WIDTH, Q_RANK, KV_RANK, QK_ROPE, D_MODEL, D_MODEL)
D_IN = POOL_WIDTH + Q_RANK + KV_RANK + QK_ROPE + 2 * D_MODEL
IN_OFFSETS = (POOL_WIDTH,
              POOL_WIDTH + Q_RANK,
              POOL_WIDTH + Q_RANK + KV_RANK,
              POOL_WIDTH + Q_RANK + KV_RANK + QK_ROPE,
              POOL_WIDTH + Q_RANK + KV_RANK + QK_ROPE + D_MODEL)

kernel_name = "hybrid_pool_mla_gated_block"


def rmsnorm(x, g):
    xf = x.astype(jnp.float32)
    y = xf * lax.rsqrt(jnp.mean(xf * xf, axis=-1, keepdims=True) + NORM_EPS)
    return (y * g.astype(jnp.float32)).astype(x.dtype)


def rope_tables(length):
    inv = 1.0 / (ROPE_THETA ** (jnp.arange(0, QK_ROPE, 2, dtype=jnp.float32) / QK_ROPE))
    ang = jnp.arange(length, dtype=jnp.float32)[:, None] * inv[None, :]
    return jnp.cos(ang), jnp.sin(ang)


def apply_rope(x, cos, sin):
    xf = x.astype(jnp.float32)
    x1, x2 = jnp.split(xf, 2, axis=-1)
    out = jnp.concatenate([x1 * cos - x2 * sin, x1 * sin + x2 * cos], axis=-1)
    return out.astype(x.dtype)


def pool_mixer(u, pool_w, pool_scale):
    B, L, _ = u.shape
    cs = jnp.cumsum(u.astype(jnp.float32), axis=1)
    cs0 = jnp.concatenate([jnp.zeros((B, 1, POOL_WIDTH), jnp.float32), cs], axis=1)
    t = jnp.arange(L, dtype=jnp.float32)[:, None]
    groups = []
    for g, w in enumerate(POOL_WINDOWS):
        c = cs0[:, :, g * POOL_GROUP:(g + 1) * POOL_GROUP]
        prev = jnp.pad(c[:, :L + 1 - w], ((0, 0), (w, 0), (0, 0)))
        wsum = (c - prev)[:, 1:]
        count = jnp.minimum(t + 1.0, float(w))
        ug = u[:, :, g * POOL_GROUP:(g + 1) * POOL_GROUP].astype(jnp.float32)
        groups.append(wsum / count - ug)
    y = jnp.stack(groups, axis=2).astype(u.dtype)
    y = jnp.einsum('blgc,gcd->blgd', y, pool_w).reshape(B, L, POOL_WIDTH)
    return y * pool_scale


def mla_attention(q_nope, q_rope, k_nope, k_rope, v):
    L = q_nope.shape[1]
    outs = []
    for start in range(0, L, Q_BLOCK):
        end = min(start + Q_BLOCK, L)
        s = (jnp.einsum('bqhd,bkhd->bhqk', q_nope[:, start:end], k_nope[:, :end])
             + jnp.einsum('bqhr,bkr->bhqk', q_rope[:, start:end], k_rope[:, :end]))
        s = s.astype(jnp.float32) * SM_SCALE
        mask = jnp.arange(end)[None, :] <= jnp.arange(start, end)[:, None]
        s = jnp.where(mask[None, None], s, MASK_VALUE)
        p = jax.nn.softmax(s, axis=-1).astype(v.dtype)
        outs.append(jnp.einsum('bhqk,bkhd->bqhd', p, v[:, :end]))
    return jnp.concatenate(outs, axis=1)


def hybrid_layer(h, cos, sin, g_mix, w_in, pool_w, pool_scale, q_norm_g, kv_norm_g, w_uq, w_ukv,
                 w_pa, w_pb, w_o, g_ffn, w_gate, w_up, w_down):
    B, L, _ = h.shape
    hn = rmsnorm(h, g_mix)
    z = hn @ w_in
    u, c_q, c_kv, k_rope, gate_a, gate_b = jnp.split(z, IN_OFFSETS, axis=-1)
    a = pool_mixer(u, pool_w, pool_scale)
    q = (rmsnorm(c_q, q_norm_g) @ w_uq).reshape(B, L, N_HEADS, QK_DIM)
    q_nope, q_rope = q[..., :QK_NOPE], q[..., QK_NOPE:]
    q_rope = apply_rope(q_rope, cos[:, None, :], sin[:, None, :])
    kv = (rmsnorm(c_kv, kv_norm_g) @ w_ukv).reshape(B, L, N_HEADS, QK_NOPE + V_DIM)
    k_nope, v = kv[..., :QK_NOPE], kv[..., QK_NOPE:]
    k_rope = apply_rope(k_rope, cos, sin)
    b = mla_attention(q_nope, q_rope, k_nope, k_rope, v).reshape(B, L, ATT_WIDTH)
    merged = jax.nn.sigmoid(gate_a) * (a @ w_pa) + jax.nn.sigmoid(gate_b) * (b @ w_pb)
    h = h + merged @ w_o
    hn = rmsnorm(h, g_ffn)
    h = h + (jax.nn.silu(hn @ w_gate) * (hn @ w_up)) @ w_down
    return h


def _fwd_setup_inputs(seed: int = 0) -> dict:
    key = jax.random.key(seed)
    ks = jax.random.split(key, 20)

    def w(k, shape, fan_in):
        return jax.random.normal(k, shape, jnp.float32) * (fan_in ** -0.5)

    def gain(k, shape):
        return 1.0 + 0.05 * jax.random.normal(k, shape, jnp.float32)

    return {
        "x": jax.random.normal(ks[0], (BATCH, SEQ, D_MODEL), jnp.float32),
        "meta_tokens": jax.random.normal(ks[1], (N_META, D_MODEL), jnp.float32),
        "norm_mix_g": gain(ks[2], (DEPTH, D_MODEL)),
        "w_in": w(ks[3], (DEPTH, D_MODEL, D_IN), D_MODEL),
        "pool_w": w(ks[4], (DEPTH, len(POOL_WINDOWS), POOL_GROUP, POOL_GROUP), POOL_GROUP),
        "pool_scale": gain(ks[5], (DEPTH, POOL_WIDTH)),
        "q_norm_g": gain(ks[6], (DEPTH, Q_RANK)),
        "kv_norm_g": gain(ks[7], (DEPTH, KV_RANK)),
        "w_uq": w(ks[8], (DEPTH, Q_RANK, N_HEADS * QK_DIM), Q_RANK),
        "w_ukv": w(ks[9], (DEPTH, KV_RANK, N_HEADS * (QK_NOPE + V_DIM)), KV_RANK),
        "w_pa": w(ks[10], (DEPTH, POOL_WIDTH, D_MODEL), POOL_WIDTH),
        "w_pb": w(ks[11], (DEPTH, ATT_WIDTH, D_MODEL), ATT_WIDTH),
        "w_o": w(ks[12], (DEPTH, D_MODEL, D_MODEL), D_MODEL),
        "norm_ffn_g": gain(ks[13], (DEPTH, D_MODEL)),
        "w_gate": w(ks[14], (DEPTH, D_MODEL, D_FF), D_MODEL),
        "w_up": w(ks[15], (DEPTH, D_MODEL, D_FF), D_MODEL),
        "w_down": w(ks[16], (DEPTH, D_FF, D_MODEL), D_FF),
        "final_norm_g": gain(ks[17], (D_MODEL,)),
    }


def _fwd_reference(x, meta_tokens, norm_mix_g, w_in, pool_w, pool_scale, q_norm_g, kv_norm_g, w_uq, w_ukv,
              w_pa, w_pb, w_o, norm_ffn_g, w_gate, w_up, w_down, final_norm_g):
    B = x.shape[0]
    meta = jnp.broadcast_to(meta_tokens.astype(x.dtype)[None], (B, N_META, D_MODEL))
    h = jnp.concatenate([meta, x], axis=1)
    cos, sin = rope_tables(h.shape[1])
    for i in range(DEPTH):
        h = hybrid_layer(h, cos, sin, norm_mix_g[i], w_in[i], pool_w[i], pool_scale[i], q_norm_g[i],
                         kv_norm_g[i], w_uq[i], w_ukv[i], w_pa[i], w_pb[i], w_o[i], norm_ffn_g[i],
                         w_gate[i], w_up[i], w_down[i])
    return rmsnorm(h, final_norm_g)[:, N_META:]


import jax as _jax
import jax.numpy as _jnp

TWIN_FORMAT = 'train_step'
FWD_PARAMS = ['x', 'meta_tokens', 'norm_mix_g', 'w_in', 'pool_w', 'pool_scale', 'q_norm_g', 'kv_norm_g', 'w_uq', 'w_ukv', 'w_pa', 'w_pb', 'w_o', 'norm_ffn_g', 'w_gate', 'w_up', 'w_down', 'final_norm_g']
TWIN_WEIGHTS = ['meta_tokens', 'norm_mix_g', 'w_in', 'pool_w', 'pool_scale', 'q_norm_g', 'kv_norm_g', 'w_uq', 'w_ukv', 'w_pa', 'w_pb', 'w_o', 'norm_ffn_g', 'w_gate', 'w_up', 'w_down', 'final_norm_g']
TWIN_DIFF_INPUT = 'x'
TWIN_INPUTS = ['x', 'meta_tokens', 'norm_mix_g', 'w_in', 'pool_w', 'pool_scale', 'q_norm_g', 'kv_norm_g', 'w_uq', 'w_ukv', 'w_pa', 'w_pb', 'w_o', 'norm_ffn_g', 'w_gate', 'w_up', 'w_down', 'final_norm_g', 'loss_target', 'm_meta_tokens', 'm_norm_mix_g', 'm_w_in', 'm_pool_w', 'm_pool_scale', 'm_q_norm_g', 'm_kv_norm_g', 'm_w_uq', 'm_w_ukv', 'm_w_pa', 'm_w_pb', 'm_w_o', 'm_norm_ffn_g', 'm_w_gate', 'm_w_up', 'm_w_down', 'm_final_norm_g', 'v_meta_tokens', 'v_norm_mix_g', 'v_w_in', 'v_pool_w', 'v_pool_scale', 'v_q_norm_g', 'v_kv_norm_g', 'v_w_uq', 'v_w_ukv', 'v_w_pa', 'v_w_pb', 'v_w_o', 'v_norm_ffn_g', 'v_w_gate', 'v_w_up', 'v_w_down', 'v_final_norm_g']
TWIN_OUTPUTS = ['loss', 'grad_x', 'grad_meta_tokens', 'grad_norm_mix_g', 'grad_w_in', 'grad_pool_w', 'grad_pool_scale', 'grad_q_norm_g', 'grad_kv_norm_g', 'grad_w_uq', 'grad_w_ukv', 'grad_w_pa', 'grad_w_pb', 'grad_w_o', 'grad_norm_ffn_g', 'grad_w_gate', 'grad_w_up', 'grad_w_down', 'grad_final_norm_g', 'delta_meta_tokens', 'delta_norm_mix_g', 'delta_w_in', 'delta_pool_w', 'delta_pool_scale', 'delta_q_norm_g', 'delta_kv_norm_g', 'delta_w_uq', 'delta_w_ukv', 'delta_w_pa', 'delta_w_pb', 'delta_w_o', 'delta_norm_ffn_g', 'delta_w_gate', 'delta_w_up', 'delta_w_down', 'delta_final_norm_g', 'new_m_meta_tokens', 'new_m_norm_mix_g', 'new_m_w_in', 'new_m_pool_w', 'new_m_pool_scale', 'new_m_q_norm_g', 'new_m_kv_norm_g', 'new_m_w_uq', 'new_m_w_ukv', 'new_m_w_pa', 'new_m_w_pb', 'new_m_w_o', 'new_m_norm_ffn_g', 'new_m_w_gate', 'new_m_w_up', 'new_m_w_down', 'new_m_final_norm_g', 'new_v_meta_tokens', 'new_v_norm_mix_g', 'new_v_w_in', 'new_v_pool_w', 'new_v_pool_scale', 'new_v_q_norm_g', 'new_v_kv_norm_g', 'new_v_w_uq', 'new_v_w_ukv', 'new_v_w_pa', 'new_v_w_pb', 'new_v_w_o', 'new_v_norm_ffn_g', 'new_v_w_gate', 'new_v_w_up', 'new_v_w_down', 'new_v_final_norm_g']
TWIN_LEAF_KINDS = {'loss': 'loss', 'grad_x': 'grad_x', 'grad_meta_tokens': 'grad_w', 'grad_norm_mix_g': 'grad_w', 'grad_w_in': 'grad_w', 'grad_pool_w': 'grad_w', 'grad_pool_scale': 'grad_w', 'grad_q_norm_g': 'grad_w', 'grad_kv_norm_g': 'grad_w', 'grad_w_uq': 'grad_w', 'grad_w_ukv': 'grad_w', 'grad_w_pa': 'grad_w', 'grad_w_pb': 'grad_w', 'grad_w_o': 'grad_w', 'grad_norm_ffn_g': 'grad_w', 'grad_w_gate': 'grad_w', 'grad_w_up': 'grad_w', 'grad_w_down': 'grad_w', 'grad_final_norm_g': 'grad_w', 'delta_meta_tokens': 'delta_w', 'delta_norm_mix_g': 'delta_w', 'delta_w_in': 'delta_w', 'delta_pool_w': 'delta_w', 'delta_pool_scale': 'delta_w', 'delta_q_norm_g': 'delta_w', 'delta_kv_norm_g': 'delta_w', 'delta_w_uq': 'delta_w', 'delta_w_ukv': 'delta_w', 'delta_w_pa': 'delta_w', 'delta_w_pb': 'delta_w', 'delta_w_o': 'delta_w', 'delta_norm_ffn_g': 'delta_w', 'delta_w_gate': 'delta_w', 'delta_w_up': 'delta_w', 'delta_w_down': 'delta_w', 'delta_final_norm_g': 'delta_w', 'new_m_meta_tokens': 'new_m', 'new_m_norm_mix_g': 'new_m', 'new_m_w_in': 'new_m', 'new_m_pool_w': 'new_m', 'new_m_pool_scale': 'new_m', 'new_m_q_norm_g': 'new_m', 'new_m_kv_norm_g': 'new_m', 'new_m_w_uq': 'new_m', 'new_m_w_ukv': 'new_m', 'new_m_w_pa': 'new_m', 'new_m_w_pb': 'new_m', 'new_m_w_o': 'new_m', 'new_m_norm_ffn_g': 'new_m', 'new_m_w_gate': 'new_m', 'new_m_w_up': 'new_m', 'new_m_w_down': 'new_m', 'new_m_final_norm_g': 'new_m', 'new_v_meta_tokens': 'new_v', 'new_v_norm_mix_g': 'new_v', 'new_v_w_in': 'new_v', 'new_v_pool_w': 'new_v', 'new_v_pool_scale': 'new_v', 'new_v_q_norm_g': 'new_v', 'new_v_kv_norm_g': 'new_v', 'new_v_w_uq': 'new_v', 'new_v_w_ukv': 'new_v', 'new_v_w_pa': 'new_v', 'new_v_w_pb': 'new_v', 'new_v_w_o': 'new_v', 'new_v_norm_ffn_g': 'new_v', 'new_v_w_gate': 'new_v', 'new_v_w_up': 'new_v', 'new_v_w_down': 'new_v', 'new_v_final_norm_g': 'new_v'}


def _forward(args):
    return _fwd_reference(*[args[k] for k in FWD_PARAMS])


def _output_shape():
    out = _jax.eval_shape(lambda: _forward(_fwd_setup_inputs(0)))
    return out.shape, out.dtype

N_MICROBATCH = 1
ADAM_LR = 0.001
ADAM_B1 = 0.9
ADAM_B2 = 0.999
ADAM_EPS = 1e-08
ADAM_WD = 0.01
ADAM_STEP = 10
PER_EXAMPLE_BATCH_AXIS = {'x': 0, 'loss_target': 0}
SHARED_INPUTS = []
_WEIGHT_DTYPES = {'meta_tokens': _jnp.float32, 'norm_mix_g': _jnp.float32, 'w_in': _jnp.float32, 'pool_w': _jnp.float32, 'pool_scale': _jnp.float32, 'q_norm_g': _jnp.float32, 'kv_norm_g': _jnp.float32, 'w_uq': _jnp.float32, 'w_ukv': _jnp.float32, 'w_pa': _jnp.float32, 'w_pb': _jnp.float32, 'w_o': _jnp.float32, 'norm_ffn_g': _jnp.float32, 'w_gate': _jnp.float32, 'w_up': _jnp.float32, 'w_down': _jnp.float32, 'final_norm_g': _jnp.float32}
MOMENT_SCALE = {'meta_tokens': 4.451836e-03, 'norm_mix_g': 9.036186e-02, 'w_in': 5.399955e-02, 'pool_w': 1.176807e-01, 'pool_scale': 1.216031e-01, 'q_norm_g': 3.269853e-02, 'kv_norm_g': 6.458536e-02, 'w_uq': 1.284945e-02, 'w_ukv': 1.525491e-02, 'w_pa': 8.343570e-02, 'w_pb': 1.696449e-02, 'w_o': 8.361031e-02, 'norm_ffn_g': 1.231590e-01, 'w_gate': 5.198301e-02, 'w_up': 5.045017e-02, 'w_down': 8.395460e-02, 'final_norm_g': 3.209680e+01}


def _to_microbatches(a, axis):
    t = _jnp.moveaxis(a, axis, 0)
    t = t.reshape((N_MICROBATCH, t.shape[0] // N_MICROBATCH) + t.shape[1:])
    return _jnp.moveaxis(t, 1, axis + 1)


def setup_inputs(seed: int = 0) -> dict:
    inp = _fwd_setup_inputs(seed)
    key = _jax.random.fold_in(_jax.random.key(seed), 7919)
    shape, _ = _output_shape()
    out = dict(inp)
    out["loss_target"] = _jax.random.normal(_jax.random.fold_in(key, 0), shape, _jnp.float32)
    for i, name in enumerate(TWIN_WEIGHTS):
        w = inp[name].astype(_jnp.float32)
        if MOMENT_SCALE is None:
            s = _jnp.sqrt(_jnp.mean(_jnp.square(w)) + 1e-30)
        else:
            s = MOMENT_SCALE[name]
        km, kv = _jax.random.split(_jax.random.fold_in(key, i + 1))
        out[name] = w
        out["m_" + name] = s * _jax.random.normal(km, w.shape, _jnp.float32)
        out["v_" + name] = (s * s) * _jax.random.uniform(kv, w.shape, _jnp.float32, 0.5, 1.5)
    if N_MICROBATCH > 1:
        for name, axis in PER_EXAMPLE_BATCH_AXIS.items():
            out[name] = _to_microbatches(out[name], axis)
    return {'x': out['x'], 'meta_tokens': out['meta_tokens'], 'norm_mix_g': out['norm_mix_g'], 'w_in': out['w_in'], 'pool_w': out['pool_w'], 'pool_scale': out['pool_scale'], 'q_norm_g': out['q_norm_g'], 'kv_norm_g': out['kv_norm_g'], 'w_uq': out['w_uq'], 'w_ukv': out['w_ukv'], 'w_pa': out['w_pa'], 'w_pb': out['w_pb'], 'w_o': out['w_o'], 'norm_ffn_g': out['norm_ffn_g'], 'w_gate': out['w_gate'], 'w_up': out['w_up'], 'w_down': out['w_down'], 'final_norm_g': out['final_norm_g'], 'loss_target': out['loss_target'], 'm_meta_tokens': out['m_meta_tokens'], 'm_norm_mix_g': out['m_norm_mix_g'], 'm_w_in': out['m_w_in'], 'm_pool_w': out['m_pool_w'], 'm_pool_scale': out['m_pool_scale'], 'm_q_norm_g': out['m_q_norm_g'], 'm_kv_norm_g': out['m_kv_norm_g'], 'm_w_uq': out['m_w_uq'], 'm_w_ukv': out['m_w_ukv'], 'm_w_pa': out['m_w_pa'], 'm_w_pb': out['m_w_pb'], 'm_w_o': out['m_w_o'], 'm_norm_ffn_g': out['m_norm_ffn_g'], 'm_w_gate': out['m_w_gate'], 'm_w_up': out['m_w_up'], 'm_w_down': out['m_w_down'], 'm_final_norm_g': out['m_final_norm_g'], 'v_meta_tokens': out['v_meta_tokens'], 'v_norm_mix_g': out['v_norm_mix_g'], 'v_w_in': out['v_w_in'], 'v_pool_w': out['v_pool_w'], 'v_pool_scale': out['v_pool_scale'], 'v_q_norm_g': out['v_q_norm_g'], 'v_kv_norm_g': out['v_kv_norm_g'], 'v_w_uq': out['v_w_uq'], 'v_w_ukv': out['v_w_ukv'], 'v_w_pa': out['v_w_pa'], 'v_w_pb': out['v_w_pb'], 'v_w_o': out['v_w_o'], 'v_norm_ffn_g': out['v_norm_ffn_g'], 'v_w_gate': out['v_w_gate'], 'v_w_up': out['v_w_up'], 'v_w_down': out['v_w_down'], 'v_final_norm_g': out['v_final_norm_g']}


def _loss(weights, diff, rest, loss_target):
    with _jax.named_scope("forward"):
        args = {**rest, TWIN_DIFF_INPUT: diff, **{k: w.astype(_WEIGHT_DTYPES[k]) for k, w in weights.items()}}
        y = _forward(args)
    with _jax.named_scope("loss_head"):
        err = _jnp.square(y.astype(_jnp.float32) - loss_target)
        return 0.5 * _jnp.sum(_jnp.mean(err, axis=-1)) if err.ndim else 0.5 * err


def _adamw(w, g, m, v):
    m = ADAM_B1 * m + (1.0 - ADAM_B1) * g
    v = ADAM_B2 * v + (1.0 - ADAM_B2) * _jnp.square(g)
    m_hat = m / (1.0 - ADAM_B1 ** ADAM_STEP)
    v_hat = v / (1.0 - ADAM_B2 ** ADAM_STEP)
    delta = -ADAM_LR * (m_hat / (_jnp.sqrt(v_hat) + ADAM_EPS) + ADAM_WD * w)
    return delta, m, v


def reference(x, meta_tokens, norm_mix_g, w_in, pool_w, pool_scale, q_norm_g, kv_norm_g, w_uq, w_ukv, w_pa, w_pb, w_o, norm_ffn_g, w_gate, w_up, w_down, final_norm_g, loss_target, m_meta_tokens, m_norm_mix_g, m_w_in, m_pool_w, m_pool_scale, m_q_norm_g, m_kv_norm_g, m_w_uq, m_w_ukv, m_w_pa, m_w_pb, m_w_o, m_norm_ffn_g, m_w_gate, m_w_up, m_w_down, m_final_norm_g, v_meta_tokens, v_norm_mix_g, v_w_in, v_pool_w, v_pool_scale, v_q_norm_g, v_kv_norm_g, v_w_uq, v_w_ukv, v_w_pa, v_w_pb, v_w_o, v_norm_ffn_g, v_w_gate, v_w_up, v_w_down, v_final_norm_g):
    given = dict(x=x, meta_tokens=meta_tokens, norm_mix_g=norm_mix_g, w_in=w_in, pool_w=pool_w, pool_scale=pool_scale, q_norm_g=q_norm_g, kv_norm_g=kv_norm_g, w_uq=w_uq, w_ukv=w_ukv, w_pa=w_pa, w_pb=w_pb, w_o=w_o, norm_ffn_g=norm_ffn_g, w_gate=w_gate, w_up=w_up, w_down=w_down, final_norm_g=final_norm_g, loss_target=loss_target, m_meta_tokens=m_meta_tokens, m_norm_mix_g=m_norm_mix_g, m_w_in=m_w_in, m_pool_w=m_pool_w, m_pool_scale=m_pool_scale, m_q_norm_g=m_q_norm_g, m_kv_norm_g=m_kv_norm_g, m_w_uq=m_w_uq, m_w_ukv=m_w_ukv, m_w_pa=m_w_pa, m_w_pb=m_w_pb, m_w_o=m_w_o, m_norm_ffn_g=m_norm_ffn_g, m_w_gate=m_w_gate, m_w_up=m_w_up, m_w_down=m_w_down, m_final_norm_g=m_final_norm_g, v_meta_tokens=v_meta_tokens, v_norm_mix_g=v_norm_mix_g, v_w_in=v_w_in, v_pool_w=v_pool_w, v_pool_scale=v_pool_scale, v_q_norm_g=v_q_norm_g, v_kv_norm_g=v_kv_norm_g, v_w_uq=v_w_uq, v_w_ukv=v_w_ukv, v_w_pa=v_w_pa, v_w_pb=v_w_pb, v_w_o=v_w_o, v_norm_ffn_g=v_norm_ffn_g, v_w_gate=v_w_gate, v_w_up=v_w_up, v_w_down=v_w_down, v_final_norm_g=v_final_norm_g)
    weights = {n: given[n] for n in TWIN_WEIGHTS}
    shared = {n: given[n] for n in SHARED_INPUTS}
    per_example = {n: given[n] for n in ['x']}
    grad_fn = _jax.value_and_grad(_loss, argnums=(0, 1))

    def one_microbatch(ex, loss_target):
        ex = dict(ex)
        diff = ex.pop(TWIN_DIFF_INPUT)
        return grad_fn(weights, diff, {**shared, **ex}, loss_target)

    if N_MICROBATCH == 1:
        loss, (grad_w, grad_x) = one_microbatch(per_example, given["loss_target"])
    else:
        def body(carry, xs):
            loss_sum, grad_sum = carry
            l_k, (gw_k, gx_k) = one_microbatch(xs[0], xs[1])
            with _jax.named_scope("update"):
                return (loss_sum + l_k, _jax.tree.map(_jnp.add, grad_sum, gw_k)), gx_k

        init = (_jnp.zeros((), _jnp.float32), _jax.tree.map(_jnp.zeros_like, weights))
        (loss, grad_w), grad_x = _jax.lax.scan(body, init, (per_example, given["loss_target"]))
    with _jax.named_scope("update"):
        delta_w, new_m, new_v = {}, {}, {}
        for n in TWIN_WEIGHTS:
            delta_w[n], new_m[n], new_v[n] = _adamw(weights[n], grad_w[n], given["m_" + n], given["v_" + n])
    return (loss, grad_x, *[grad_w[n] for n in TWIN_WEIGHTS], *[delta_w[n] for n in TWIN_WEIGHTS],
            *[new_m[n] for n in TWIN_WEIGHTS], *[new_v[n] for n in TWIN_WEIGHTS])
```

```python
import functools

import jax
import jax.numpy as jnp
from jax import lax
from jax.experimental import pallas as pl
from jax.experimental.pallas import tpu as pltpu

F32 = jnp.float32
BF16 = jnp.bfloat16

N_META = 16
POOL_WINDOWS = (2, 4, 8, 16)
POOL_GROUP = 128
POOL_WIDTH = POOL_GROUP * len(POOL_WINDOWS)
QK_NOPE = 64
QK_ROPE = 32
V_DIM = 64
QK_DIM = QK_NOPE + QK_ROPE
Q_RANK = 256
KV_RANK = 128
HEAD_PAD = 128
SM_SCALE = QK_DIM ** -0.5
ROPE_THETA = 10000.0
NORM_EPS = 1e-6
MASK_VALUE = -1e30
Z_FIXED = POOL_WIDTH + Q_RANK + KV_RANK + HEAD_PAD

ADAM_LR = 0.001
ADAM_B1 = 0.9
ADAM_B2 = 0.999
ADAM_EPS = 1e-08
ADAM_WD = 0.01
ADAM_STEP = 10

N_CHIPS = 4
ATT_BLOCK = 256
PACK_COLS = 1024
VMEM_LIMIT = 60 * 1024 * 1024

MESH = pl.DeviceIdType.MESH

BIG_WEIGHTS = ("w_in", "w_uq", "w_ukv", "w_pa", "w_pb", "w_o", "w_gate", "w_up", "w_down")
SHARD_AXIS = {"w_in": 2, "w_uq": 2, "w_ukv": 2, "w_pa": 2, "w_pb": 1, "w_o": 1, "w_gate": 2, "w_up": 2, "w_down": 1}
SMALL_WEIGHTS = ("norm_mix_g", "pool_w", "pool_scale", "q_norm_g", "kv_norm_g", "norm_ffn_g", "final_norm_g")
WEIGHT_ORDER = ("meta_tokens", "norm_mix_g", "w_in", "pool_w", "pool_scale", "q_norm_g", "kv_norm_g", "w_uq", "w_ukv",
                "w_pa", "w_pb", "w_o", "norm_ffn_g", "w_gate", "w_up", "w_down", "final_norm_g")


def _round_up(n, m):
    return -(-n // m) * m


def _vmem_spec():
    return pl.BlockSpec(memory_space=pltpu.VMEM)


def _any_spec():
    return pl.BlockSpec(memory_space=pl.ANY)


def _row_block(tm, width, col_block=0):
    return pl.BlockSpec((tm, width), lambda i, cb=col_block: (i, cb))


def _params(sem, vmem=VMEM_LIMIT):
    return pltpu.CompilerParams(dimension_semantics=sem, vmem_limit_bytes=vmem)


def _token_tile(t, want):
    return want if t % want == 0 else ATT_BLOCK


def _dot(a, b):
    return jnp.dot(a, b, preferred_element_type=F32)


def _dot_nt(a, b):
    return lax.dot_general(a, b, (((1,), (1,)), ((), ())), preferred_element_type=F32)


def _dot_tn(a, b):
    return lax.dot_general(a, b, (((0,), (0,)), ((), ())), preferred_element_type=F32)


def _rms_fwd(x, g):
    r = lax.rsqrt(jnp.mean(x * x, axis=-1, keepdims=True) + NORM_EPS)
    xh = x * r
    return xh * g, xh, r


def _rms_bwd(dy, xh, r, g):
    gdy = dy * g
    dx = r * (gdy - xh * jnp.mean(xh * gdy, axis=-1, keepdims=True))
    return dx, dy * xh


def _rope_fwd(x, c, sa, sb):
    return x * c + pltpu.roll(x, 16, 1) * sa + pltpu.roll(x, HEAD_PAD - 16, 1) * sb


def _rope_bwd(d, c, sa, sb):
    return d * c + pltpu.roll(d * sa, HEAD_PAD - 16, 1) + pltpu.roll(d * sb, 16, 1)


def _in_proj_fwd(h, g, w_in_p):
    t, d = h.shape
    nz = w_in_p.shape[1]
    tm = _token_tile(t, 512)

    def body(h_ref, g_ref, w_ref, z_ref):
        hn, _, _ = _rms_fwd(h_ref[...], g_ref[...])
        z_ref[...] = _dot(hn.astype(BF16), w_ref[...]).astype(BF16)

    return pl.pallas_call(
        body, name="in_proj_fwd", grid=(t // tm,),
        in_specs=[_row_block(tm, d), _vmem_spec(), _vmem_spec()],
        out_specs=_row_block(tm, nz),
        out_shape=jax.ShapeDtypeStruct((t, nz), BF16),
        compiler_params=_params(("parallel",)),
    )(h, g, w_in_p)


def _window_sum(x, w, row, forward):
    n = x.shape[0]
    s = x
    k = 1
    while k < w:
        if forward:
            s = s + jnp.where(row >= k, pltpu.roll(s, k, 0), 0.0)
        else:
            s = s + jnp.where(row < n - k, pltpu.roll(s, n - k, 0), 0.0)
        k *= 2
    return s


def _pool_fwd(z3, pool_w, pool_scale):
    b, lp, _ = z3.shape

    def body(u_ref, pw_ref, sc_ref, a_ref):
        row = lax.broadcasted_iota(jnp.int32, (lp, POOL_GROUP), 0)
        pos = row.astype(F32)
        for gi, w in enumerate(POOL_WINDOWS):
            cols = slice(gi * POOL_GROUP, (gi + 1) * POOL_GROUP)
            u = u_ref[0, :, cols].astype(F32)
            y = _window_sum(u, w, row, True) / jnp.minimum(pos + 1.0, float(w)) - u
            yw = _dot(y.astype(BF16), pw_ref[gi])
            a_ref[0, :, cols] = (yw * sc_ref[:, cols]).astype(BF16)

    return pl.pallas_call(
        body, name="pool_fwd", grid=(b,),
        in_specs=[pl.BlockSpec((1, lp, POOL_WIDTH), lambda i: (i, 0, 0)), _vmem_spec(), _vmem_spec()],
        out_specs=pl.BlockSpec((1, lp, POOL_WIDTH), lambda i: (i, 0, 0)),
        out_shape=jax.ShapeDtypeStruct((b, lp, POOL_WIDTH), BF16),
        compiler_params=_params(("parallel",)),
    )(z3, pool_w, pool_scale)


def _qkv_fwd(z, g_q, g_kv, w_uq_p, w_kv_p, rope_c, rope_sa, rope_sb):
    t = z.shape[0]
    hw = w_uq_p.shape[1]
    heads = hw // HEAD_PAD
    tm = _token_tile(t, 512)

    def body(cq_ref, ckv_ref, kr_ref, gq_ref, gkv_ref, wq_ref, wkv_ref, c_ref, sa_ref, sb_ref, q_ref, k_ref, v_ref):
        c, sa, sb = c_ref[...], sa_ref[...], sb_ref[...]
        cqn, _, _ = _rms_fwd(cq_ref[...].astype(F32), gq_ref[...])
        qraw = _dot(cqn.astype(BF16), wq_ref[...])
        ckvn, _, _ = _rms_fwd(ckv_ref[...].astype(F32), gkv_ref[...])
        kvraw = _dot(ckvn.astype(BF16), wkv_ref[...])
        kr = kr_ref[...].astype(F32)
        for hd in range(heads):
            cols = slice(hd * HEAD_PAD, (hd + 1) * HEAD_PAD)
            q_ref[:, cols] = _rope_fwd(qraw[:, cols], c, sa, sb).astype(BF16)
            k_ref[:, cols] = _rope_fwd(kvraw[:, cols] + kr, c, sa, sb).astype(BF16)
        v_ref[...] = kvraw[:, hw:].astype(BF16)

    out = jax.ShapeDtypeStruct((t, hw), BF16)
    return pl.pallas_call(
        body, name="qkv_fwd", grid=(t // tm,),
        in_specs=[_row_block(tm, Q_RANK, POOL_WIDTH // Q_RANK),
                  _row_block(tm, KV_RANK, (POOL_WIDTH + Q_RANK) // KV_RANK),
                  _row_block(tm, HEAD_PAD, (POOL_WIDTH + Q_RANK + KV_RANK) // HEAD_PAD),
                  _vmem_spec(), _vmem_spec(), _vmem_spec(), _vmem_spec(),
                  _row_block(tm, HEAD_PAD), _row_block(tm, HEAD_PAD), _row_block(tm, HEAD_PAD)],
        out_specs=[_row_block(tm, hw)] * 3,
        out_shape=[out, out, out],
        compiler_params=_params(("parallel",)),
    )(z, z, z, g_q, g_kv, w_uq_p, w_kv_p, rope_c, rope_sa, rope_sb)


def _causal_mask(tb):
    row = lax.broadcasted_iota(jnp.int32, (tb, tb), 0)
    col = lax.broadcasted_iota(jnp.int32, (tb, tb), 1)
    return col <= row


def _attn_fwd(q3, k3, v3):
    b, lp, hw = q3.shape
    heads = hw // HEAD_PAD
    tb = ATT_BLOCK
    nblk = lp // tb

    def body(q_ref, k_ref, v_ref, o_ref, lse_ref):
        hd = pl.program_id(1)

        @pl.when(hd == 0)
        def _():
            lse_ref[...] = jnp.zeros_like(lse_ref)

        lane = lax.broadcasted_iota(jnp.int32, (tb, HEAD_PAD), 1)
        mask = _causal_mask(tb)

        def q_block(i, carry):
            r0 = pl.multiple_of(i * tb, tb)
            q = q_ref[0, pl.ds(r0, tb), :]

            def kv_block(j, state, diagonal):
                m, l, acc = state
                c0 = pl.multiple_of(j * tb, tb)
                s = _dot_nt(q, k_ref[0, pl.ds(c0, tb), :]) * SM_SCALE
                if diagonal:
                    s = jnp.where(mask, s, MASK_VALUE)
                m_new = jnp.maximum(m, jnp.max(s, axis=-1, keepdims=True))
                alpha = jnp.exp(m - m_new)
                p = jnp.exp(s - m_new)
                l = alpha * l + jnp.sum(p, axis=-1, keepdims=True)
                acc = alpha * acc + _dot(p.astype(BF16), v_ref[0, pl.ds(c0, tb), :])
                return m_new, l, acc

            init = (jnp.full((tb, 1), MASK_VALUE, F32), jnp.zeros((tb, 1), F32), jnp.zeros((tb, HEAD_PAD), F32))
            state = lax.fori_loop(0, i, lambda j, st: kv_block(j, st, False), init)
            m, l, acc = kv_block(i, state, True)
            o_ref[0, pl.ds(r0, tb), :] = (acc / l).astype(BF16)
            lse = m + jnp.log(l)
            lse_ref[0, pl.ds(r0, tb), :] = jnp.where(lane == hd, lse, lse_ref[0, pl.ds(r0, tb), :])
            return carry

        lax.fori_loop(0, nblk, q_block, 0)

    head_spec = pl.BlockSpec((1, lp, HEAD_PAD), lambda bi, hi: (bi, 0, hi))
    return pl.pallas_call(
        body, name="attn_fwd", grid=(b, heads),
        in_specs=[head_spec, head_spec, head_spec],
        out_specs=[head_spec, pl.BlockSpec((1, lp, HEAD_PAD), lambda bi, hi: (bi, 0, 0))],
        out_shape=[jax.ShapeDtypeStruct((b, lp, hw), BF16), jax.ShapeDtypeStruct((b, lp, HEAD_PAD), F32)],
        compiler_params=_params(("parallel", "arbitrary")),
    )(q3, k3, v3)


def _merge_fwd(h, z, a, o, w_pa, w_pb_p, w_o):
    t, d = h.shape
    hw = o.shape[1]
    tm = _token_tile(t, 512)
    gate_block = Z_FIXED // d

    def body(h_ref, ga_ref, gb_ref, a_ref, o_ref, wpa_ref, wpb_ref, wo_ref, h1_ref, pa_ref, pb_ref):
        pa = _dot(a_ref[...], wpa_ref[...])
        pb = _dot(o_ref[...], wpb_ref[...])
        merged = jax.nn.sigmoid(ga_ref[...].astype(F32)) * pa + jax.nn.sigmoid(gb_ref[...].astype(F32)) * pb
        h1_ref[...] = h_ref[...] + _dot(merged.astype(BF16), wo_ref[...])
        pa_ref[...] = pa.astype(BF16)
        pb_ref[...] = pb.astype(BF16)

    return pl.pallas_call(
        body, name="merge_fwd", grid=(t // tm,),
        in_specs=[_row_block(tm, d), _row_block(tm, d, gate_block), _row_block(tm, d, gate_block + 1),
                  _row_block(tm, POOL_WIDTH), _row_block(tm, hw), _vmem_spec(), _vmem_spec(), _vmem_spec()],
        out_specs=[_row_block(tm, d)] * 3,
        out_shape=[jax.ShapeDtypeStruct((t, d), F32), jax.ShapeDtypeStruct((t, d), BF16), jax.ShapeDtypeStruct((t, d), BF16)],
        compiler_params=_params(("parallel",)),
    )(h, z, z, a, o, w_pa, w_pb_p, w_o)


def _ffn_fwd(h, g, w_gate, w_up, w_down):
    t, d = h.shape
    f = w_gate.shape[1]
    tm = _token_tile(t, 256)

    def body(h_ref, g_ref, wg_ref, wu_ref, wd_ref, h2_ref, a_ref, b_ref):
        x = h_ref[...]
        hn, _, _ = _rms_fwd(x, g_ref[...])
        hn = hn.astype(BF16)
        ga = _dot(hn, wg_ref[...])
        up = _dot(hn, wu_ref[...])
        act = ga * jax.nn.sigmoid(ga) * up
        h2_ref[...] = x + _dot(act.astype(BF16), wd_ref[...])
        a_ref[...] = ga.astype(BF16)
        b_ref[...] = up.astype(BF16)

    return pl.pallas_call(
        body, name="ffn_fwd", grid=(t // tm,),
        in_specs=[_row_block(tm, d), _vmem_spec(), _vmem_spec(), _vmem_spec(), _vmem_spec()],
        out_specs=[_row_block(tm, d), _row_block(tm, f), _row_block(tm, f)],
        out_shape=[jax.ShapeDtypeStruct((t, d), F32), jax.ShapeDtypeStruct((t, f), BF16), jax.ShapeDtypeStruct((t, f), BF16)],
        compiler_params=_params(("parallel",)),
    )(h, g, w_gate, w_up, w_down)


def _loss_head(h, g, target, valid):
    t, d = h.shape
    tm = _token_tile(t, 512)

    def body(h_ref, g_ref, t_ref, valid_ref, dh_ref, loss_ref, dg_ref):
        @pl.when(pl.program_id(0) == 0)
        def _():
            loss_ref[...] = jnp.zeros_like(loss_ref)
            dg_ref[...] = jnp.zeros_like(dg_ref)

        gain = g_ref[...]
        y, xh, r = _rms_fwd(h_ref[...], gain)
        err = (y - t_ref[...]) * valid_ref[...]
        per_row = jnp.sum(err * err, axis=-1, keepdims=True) / d
        loss_ref[...] += 0.5 * jnp.sum(per_row, axis=0, keepdims=True)
        dx, dg_rows = _rms_bwd(err / d, xh, r, gain)
        dh_ref[...] = dx
        dg_ref[...] += jnp.sum(dg_rows, axis=0, keepdims=True)

    return pl.pallas_call(
        body, name="loss_head", grid=(t // tm,),
        in_specs=[_row_block(tm, d), _vmem_spec(), _row_block(tm, d), _row_block(tm, 1)],
        out_specs=[_row_block(tm, d), pl.BlockSpec((1, 1), lambda i: (0, 0)), pl.BlockSpec((1, d), lambda i: (0, 0))],
        out_shape=[jax.ShapeDtypeStruct((t, d), F32), jax.ShapeDtypeStruct((1, 1), F32), jax.ShapeDtypeStruct((1, d), F32)],
        compiler_params=_params(("arbitrary",)),
    )(h, g, target, valid)


def _weight_grad(x, y, name):
    t, k = x.shape
    n = y.shape[1]
    tm = _token_tile(t, 512)
    tn = n
    while k * tn * 4 > 8 * 1024 * 1024 and tn % 256 == 0:
        tn //= 2

    def body(x_ref, y_ref, o_ref):
        @pl.when(pl.program_id(1) == 0)
        def _():
            o_ref[...] = jnp.zeros_like(o_ref)

        o_ref[...] += _dot_tn(x_ref[...].astype(BF16), y_ref[...].astype(BF16))

    return pl.pallas_call(
        body, name=name, grid=(n // tn, t // tm),
        in_specs=[pl.BlockSpec((tm, k), lambda j, i: (i, 0)), pl.BlockSpec((tm, tn), lambda j, i: (i, j))],
        out_specs=pl.BlockSpec((k, tn), lambda j, i: (0, j)),
        out_shape=jax.ShapeDtypeStruct((k, n), F32),
        compiler_params=_params(("parallel", "arbitrary")),
    )(x, y)


def _ffn_bwd(h, dh2, a, b, g, w_gate_t, w_up_t, w_down_t):
    t, d = h.shape
    f = a.shape[1]
    tm = _token_tile(t, 256)

    def body(h_ref, dh2_ref, a_ref, b_ref, g_ref, wgt_ref, wut_ref, wdt_ref, dh_ref, hn_ref, act_ref, da_ref, db_ref, dg_ref):
        @pl.when(pl.program_id(0) == 0)
        def _():
            dg_ref[...] = jnp.zeros_like(dg_ref)

        gain = g_ref[...]
        hn, xh, r = _rms_fwd(h_ref[...], gain)
        hn_ref[...] = hn.astype(BF16)
        dh2 = dh2_ref[...]
        dact = _dot(dh2.astype(BF16), wdt_ref[...])
        ga = a_ref[...].astype(F32)
        up = b_ref[...].astype(F32)
        sg = jax.nn.sigmoid(ga)
        silu = ga * sg
        act_ref[...] = (silu * up).astype(BF16)
        da = (dact * up * (sg * (1.0 + ga * (1.0 - sg)))).astype(BF16)
        db = (dact * silu).astype(BF16)
        da_ref[...] = da
        db_ref[...] = db
        dhn = _dot(da, wgt_ref[...]) + _dot(db, wut_ref[...])
        dx, dg_rows = _rms_bwd(dhn, xh, r, gain)
        dh_ref[...] = dh2 + dx
        dg_ref[...] += jnp.sum(dg_rows, axis=0, keepdims=True)

    return pl.pallas_call(
        body, name="ffn_bwd", grid=(t // tm,),
        in_specs=[_row_block(tm, d), _row_block(tm, d), _row_block(tm, f), _row_block(tm, f),
                  _vmem_spec(), _vmem_spec(), _vmem_spec(), _vmem_spec()],
        out_specs=[_row_block(tm, d), _row_block(tm, d), _row_block(tm, f), _row_block(tm, f), _row_block(tm, f),
                   pl.BlockSpec((1, d), lambda i: (0, 0))],
        out_shape=[jax.ShapeDtypeStruct((t, d), F32), jax.ShapeDtypeStruct((t, d), BF16), jax.ShapeDtypeStruct((t, f), BF16),
                   jax.ShapeDtypeStruct((t, f), BF16), jax.ShapeDtypeStruct((t, f), BF16), jax.ShapeDtypeStruct((1, d), F32)],
        compiler_params=_params(("arbitrary",)),
    )(h, dh2, a, b, g, w_gate_t, w_up_t, w_down_t)


def _merge_bwd(dh1, z, pa, pb, w_o_t, w_pa_t, w_pb_pt):
    t, d = dh1.shape
    hw = w_pb_pt.shape[1]
    tm = _token_tile(t, 512)
    gate_block = Z_FIXED // d

    def body(dh_ref, ga_ref, gb_ref, pa_ref, pb_ref, wot_ref, wpat_ref, wpbt_ref,
             mg_ref, dpa_ref, dpb_ref, dga_ref, dgb_ref, da_ref, do_ref):
        dm = _dot(dh_ref[...].astype(BF16), wot_ref[...])
        sa = jax.nn.sigmoid(ga_ref[...].astype(F32))
        sb = jax.nn.sigmoid(gb_ref[...].astype(F32))
        pa = pa_ref[...].astype(F32)
        pb = pb_ref[...].astype(F32)
        mg_ref[...] = (sa * pa + sb * pb).astype(BF16)
        dpa = (dm * sa).astype(BF16)
        dpb = (dm * sb).astype(BF16)
        dpa_ref[...] = dpa
        dpb_ref[...] = dpb
        dga_ref[...] = (dm * pa * (sa * (1.0 - sa))).astype(BF16)
        dgb_ref[...] = (dm * pb * (sb * (1.0 - sb))).astype(BF16)
        da_ref[...] = _dot(dpa, wpat_ref[...]).astype(BF16)
        do_ref[...] = _dot(dpb, wpbt_ref[...]).astype(BF16)

    wide = jax.ShapeDtypeStruct((t, d), BF16)
    return pl.pallas_call(
        body, name="merge_bwd", grid=(t // tm,),
        in_specs=[_row_block(tm, d), _row_block(tm, d, gate_block), _row_block(tm, d, gate_block + 1),
                  _row_block(tm, d), _row_block(tm, d), _vmem_spec(), _vmem_spec(), _vmem_spec()],
        out_specs=[_row_block(tm, d)] * 5 + [_row_block(tm, POOL_WIDTH), _row_block(tm, hw)],
        out_shape=[wide] * 5 + [jax.ShapeDtypeStruct((t, POOL_WIDTH), BF16), jax.ShapeDtypeStruct((t, hw), BF16)],
        compiler_params=_params(("parallel",)),
    )(dh1, z, z, pa, pb, w_o_t, w_pa_t, w_pb_pt)


def _attn_bwd(q3, k3, v3, o3, do3, lse3):
    b, lp, hw = q3.shape
    heads = hw // HEAD_PAD
    tb = ATT_BLOCK
    nblk = lp // tb

    def body(q_ref, k_ref, v_ref, o_ref, do_ref, lse_ref, dq_ref, dk_ref, dv_ref, dq_acc, lse_col, delta_col):
        hd = pl.program_id(1)
        lane = lax.broadcasted_iota(jnp.int32, (lp, HEAD_PAD), 1)
        lse_col[...] = jnp.sum(jnp.where(lane == hd, lse_ref[0], 0.0), axis=-1, keepdims=True)
        delta_col[...] = jnp.sum(do_ref[0].astype(F32) * o_ref[0].astype(F32), axis=-1, keepdims=True)
        dq_acc[...] = jnp.zeros_like(dq_acc)
        mask = _causal_mask(tb)

        def kv_block(j, carry):
            c0 = pl.multiple_of(j * tb, tb)
            k = k_ref[0, pl.ds(c0, tb), :]
            v = v_ref[0, pl.ds(c0, tb), :]

            def q_block(i, state, diagonal):
                dk, dv = state
                r0 = pl.multiple_of(i * tb, tb)
                q = q_ref[0, pl.ds(r0, tb), :]
                do = do_ref[0, pl.ds(r0, tb), :]
                s = _dot_nt(q, k) * SM_SCALE
                if diagonal:
                    s = jnp.where(mask, s, MASK_VALUE)
                p = jnp.exp(s - lse_col[pl.ds(r0, tb), :])
                dp = _dot_nt(do, v)
                ds = (p * (dp - delta_col[pl.ds(r0, tb), :]) * SM_SCALE).astype(BF16)
                dv = dv + _dot_tn(p.astype(BF16), do)
                dk = dk + _dot_tn(ds, q)
                dq_acc[pl.ds(r0, tb), :] += _dot(ds, k)
                return dk, dv

            zero = jnp.zeros((tb, HEAD_PAD), F32)
            state = q_block(j, (zero, zero), True)
            dk, dv = lax.fori_loop(j + 1, nblk, lambda i, st: q_block(i, st, False), state)
            dk_ref[0, pl.ds(c0, tb), :] = dk.astype(BF16)
            dv_ref[0, pl.ds(c0, tb), :] = dv.astype(BF16)
            return carry

        lax.fori_loop(0, nblk, kv_block, 0)
        dq_ref[0] = dq_acc[...].astype(BF16)

    head_spec = pl.BlockSpec((1, lp, HEAD_PAD), lambda bi, hi: (bi, 0, hi))
    out = jax.ShapeDtypeStruct((b, lp, hw), BF16)
    return pl.pallas_call(
        body, name="attn_bwd", grid=(b, heads),
        in_specs=[head_spec] * 5 + [pl.BlockSpec((1, lp, HEAD_PAD), lambda bi, hi: (bi, 0, 0))],
        out_specs=[head_spec] * 3,
        out_shape=[out, out, out],
        scratch_shapes=[pltpu.VMEM((lp, HEAD_PAD), F32), pltpu.VMEM((lp, 1), F32), pltpu.VMEM((lp, 1), F32)],
        compiler_params=_params(("parallel", "parallel")),
    )(q3, k3, v3, o3, do3, lse3)


def _qkv_bwd(dq, dk, dv, z, g_q, g_kv, w_uq_pt, w_kv_pt, rope_c, rope_sa, rope_sb):
    t, hw = dq.shape
    heads = hw // HEAD_PAD
    tm = _token_tile(t, 512)

    def body(dq_ref, dk_ref, dv_ref, cq_ref, ckv_ref, gq_ref, gkv_ref, wqt_ref, wkvt_ref, c_ref, sa_ref, sb_ref,
             dqraw_ref, dkvraw_ref, cqn_ref, ckvn_ref, dcq_ref, dckv_ref, dkr_ref, dgq_ref, dgkv_ref):
        @pl.when(pl.program_id(0) == 0)
        def _():
            dgq_ref[...] = jnp.zeros_like(dgq_ref)
            dgkv_ref[...] = jnp.zeros_like(dgkv_ref)

        c, sa, sb = c_ref[...], sa_ref[...], sb_ref[...]
        dkr = jnp.zeros((tm, HEAD_PAD), F32)
        for hd in range(heads):
            cols = slice(hd * HEAD_PAD, (hd + 1) * HEAD_PAD)
            dqraw_ref[:, cols] = _rope_bwd(dq_ref[:, cols].astype(F32), c, sa, sb).astype(BF16)
            dkraw = _rope_bwd(dk_ref[:, cols].astype(F32), c, sa, sb)
            dkvraw_ref[:, cols] = dkraw.astype(BF16)
            dkr = dkr + dkraw
        dkvraw_ref[:, hw:] = dv_ref[...]
        lane = lax.broadcasted_iota(jnp.int32, (tm, HEAD_PAD), 1)
        dkr_ref[...] = jnp.where((lane >= QK_NOPE) & (lane < QK_DIM), dkr, 0.0).astype(BF16)

        gq = gq_ref[...]
        cqn, xh, r = _rms_fwd(cq_ref[...].astype(F32), gq)
        cqn_ref[...] = cqn.astype(BF16)
        dx, dg_rows = _rms_bwd(_dot(dqraw_ref[...], wqt_ref[...]), xh, r, gq)
        dcq_ref[...] = dx.astype(BF16)
        dgq_ref[...] += jnp.sum(dg_rows, axis=0, keepdims=True)

        gkv = gkv_ref[...]
        ckvn, xh, r = _rms_fwd(ckv_ref[...].astype(F32), gkv)
        ckvn_ref[...] = ckvn.astype(BF16)
        dx, dg_rows = _rms_bwd(_dot(dkvraw_ref[...], wkvt_ref[...]), xh, r, gkv)
        dckv_ref[...] = dx.astype(BF16)
        dgkv_ref[...] += jnp.sum(dg_rows, axis=0, keepdims=True)

    def shape(width, dtype=BF16):
        return jax.ShapeDtypeStruct((t, width), dtype)

    return pl.pallas_call(
        body, name="qkv_bwd", grid=(t // tm,),
        in_specs=[_row_block(tm, hw)] * 3
        + [_row_block(tm, Q_RANK, POOL_WIDTH // Q_RANK), _row_block(tm, KV_RANK, (POOL_WIDTH + Q_RANK) // KV_RANK)]
        + [_vmem_spec()] * 4 + [_row_block(tm, HEAD_PAD)] * 3,
        out_specs=[_row_block(tm, hw), _row_block(tm, 2 * hw), _row_block(tm, Q_RANK), _row_block(tm, KV_RANK),
                   _row_block(tm, Q_RANK), _row_block(tm, KV_RANK), _row_block(tm, HEAD_PAD),
                   pl.BlockSpec((1, Q_RANK), lambda i: (0, 0)), pl.BlockSpec((1, KV_RANK), lambda i: (0, 0))],
        out_shape=[shape(hw), shape(2 * hw), shape(Q_RANK), shape(KV_RANK), shape(Q_RANK), shape(KV_RANK), shape(HEAD_PAD),
                   jax.ShapeDtypeStruct((1, Q_RANK), F32), jax.ShapeDtypeStruct((1, KV_RANK), F32)],
        compiler_params=_params(("arbitrary",)),
    )(dq, dk, dv, z, z, g_q, g_kv, w_uq_pt, w_kv_pt, rope_c, rope_sa, rope_sb)


def _pool_bwd(z3, da3, pool_w, pool_w_t, pool_scale):
    b, lp, _ = z3.shape
    groups = len(POOL_WINDOWS)

    def body(u_ref, da_ref, pw_ref, pwt_ref, sc_ref, du_ref, dpw_ref, dsc_ref):
        @pl.when(pl.program_id(0) == 0)
        def _():
            dpw_ref[...] = jnp.zeros_like(dpw_ref)
            dsc_ref[...] = jnp.zeros_like(dsc_ref)

        row = lax.broadcasted_iota(jnp.int32, (lp, POOL_GROUP), 0)
        pos = row.astype(F32)
        for gi, w in enumerate(POOL_WINDOWS):
            cols = slice(gi * POOL_GROUP, (gi + 1) * POOL_GROUP)
            count = jnp.minimum(pos + 1.0, float(w))
            u = u_ref[0, :, cols].astype(F32)
            y = (_window_sum(u, w, row, True) / count - u).astype(BF16)
            yw = _dot(y, pw_ref[gi])
            da = da_ref[0, :, cols].astype(F32)
            dsc_ref[:, cols] += jnp.sum(da * yw, axis=0, keepdims=True)
            dyw = (da * sc_ref[:, cols]).astype(BF16)
            dpw_ref[gi] += _dot_tn(y, dyw)
            dy = _dot(dyw, pwt_ref[gi])
            du_ref[0, :, cols] = (_window_sum(dy / count, w, row, False) - dy).astype(BF16)

    return pl.pallas_call(
        body, name="pool_bwd", grid=(b,),
        in_specs=[pl.BlockSpec((1, lp, POOL_WIDTH), lambda i: (i, 0, 0)), pl.BlockSpec((1, lp, POOL_WIDTH), lambda i: (i, 0, 0)),
                  _vmem_spec(), _vmem_spec(), _vmem_spec()],
        out_specs=[pl.BlockSpec((1, lp, POOL_WIDTH), lambda i: (i, 0, 0)),
                   pl.BlockSpec((groups, POOL_GROUP, POOL_GROUP), lambda i: (0, 0, 0)),
                   pl.BlockSpec((1, POOL_WIDTH), lambda i: (0, 0))],
        out_shape=[jax.ShapeDtypeStruct((b, lp, POOL_WIDTH), BF16), jax.ShapeDtypeStruct((groups, POOL_GROUP, POOL_GROUP), F32),
                   jax.ShapeDtypeStruct((1, POOL_WIDTH), F32)],
        compiler_params=_params(("arbitrary",)),
    )(z3, da3, pool_w, pool_w_t, pool_scale)


def _in_proj_bwd(h, dh1, du, dcq, dckv, dkr, dga, dgb, g, w_in_pt):
    t, d = h.shape
    nz = w_in_pt.shape[0]
    tm = _token_tile(t, 512)
    widths = (POOL_WIDTH, Q_RANK, KV_RANK, HEAD_PAD, d, d)

    def body(h_ref, dh1_ref, du_ref, dcq_ref, dckv_ref, dkr_ref, dga_ref, dgb_ref, g_ref, wt_ref, dh_ref, hn_ref, dz_ref, dg_ref):
        @pl.when(pl.program_id(0) == 0)
        def _():
            dg_ref[...] = jnp.zeros_like(dg_ref)

        gain = g_ref[...]
        hn, xh, r = _rms_fwd(h_ref[...], gain)
        hn_ref[...] = hn.astype(BF16)
        dhn = jnp.zeros((tm, d), F32)
        start = 0
        for piece, width in zip((du_ref, dcq_ref, dckv_ref, dkr_ref, dga_ref, dgb_ref), widths):
            val = piece[...]
            dz_ref[:, start:start + width] = val
            dhn = dhn + _dot(val, wt_ref[start:start + width, :])
            start += width
        dx, dg_rows = _rms_bwd(dhn, xh, r, gain)
        dh_ref[...] = dh1_ref[...] + dx
        dg_ref[...] += jnp.sum(dg_rows, axis=0, keepdims=True)

    return pl.pallas_call(
        body, name="in_proj_bwd", grid=(t // tm,),
        in_specs=[_row_block(tm, d), _row_block(tm, d)] + [_row_block(tm, w) for w in widths] + [_vmem_spec(), _vmem_spec()],
        out_specs=[_row_block(tm, d), _row_block(tm, d), _row_block(tm, nz), pl.BlockSpec((1, d), lambda i: (0, 0))],
        out_shape=[jax.ShapeDtypeStruct((t, d), F32), jax.ShapeDtypeStruct((t, d), BF16), jax.ShapeDtypeStruct((t, nz), BF16),
                   jax.ShapeDtypeStruct((1, d), F32)],
        compiler_params=_params(("arbitrary",)),
    )(h, dh1, du, dcq, dckv, dkr, dga, dgb, g, w_in_pt)


def _pad_heads(w, heads, width):
    k = w.shape[0]
    w = w.reshape(k, heads, width)
    return jnp.pad(w, ((0, 0), (0, 0), (0, HEAD_PAD - width))).reshape(k, heads * HEAD_PAD)


def _unpad_heads(w, heads, width):
    k = w.shape[0]
    return w.reshape(k, heads, HEAD_PAD)[:, :, :width].reshape(k, heads * width)


def _layer_layouts(w, heads):
    d = w["w_in"].shape[0]
    o1, o2, o3, o4 = POOL_WIDTH, POOL_WIDTH + Q_RANK, POOL_WIDTH + Q_RANK + KV_RANK, POOL_WIDTH + Q_RANK + KV_RANK + QK_ROPE
    w_in = w["w_in"]
    rope_cols = jnp.pad(w_in[:, o3:o4], ((0, 0), (QK_NOPE, HEAD_PAD - QK_DIM)))
    w_in_p = jnp.concatenate([w_in[:, :o3], rope_cols, w_in[:, o4:]], axis=1)
    w_uq_p = _pad_heads(w["w_uq"], heads, QK_DIM)
    kv = w["w_ukv"].reshape(KV_RANK, heads, QK_NOPE + V_DIM)
    w_k = jnp.pad(kv[:, :, :QK_NOPE], ((0, 0), (0, 0), (0, HEAD_PAD - QK_NOPE))).reshape(KV_RANK, heads * HEAD_PAD)
    w_v = jnp.pad(kv[:, :, QK_NOPE:], ((0, 0), (0, 0), (0, HEAD_PAD - V_DIM))).reshape(KV_RANK, heads * HEAD_PAD)
    w_kv_p = jnp.concatenate([w_k, w_v], axis=1)
    w_pb_p = jnp.pad(w["w_pb"].reshape(heads, V_DIM, d), ((0, 0), (0, HEAD_PAD - V_DIM), (0, 0))).reshape(heads * HEAD_PAD, d)
    out = dict(w_in_p=w_in_p, w_uq_p=w_uq_p, w_kv_p=w_kv_p, w_pa=w["w_pa"], w_pb_p=w_pb_p, w_o=w["w_o"],
               w_gate=w["w_gate"], w_up=w["w_up"], w_down=w["w_down"])
    for name in tuple(out):
        out[name + "_t"] = out[name].T
    return out


def _layer_grad_layouts(g, heads, d):
    o3 = POOL_WIDTH + Q_RANK + KV_RANK
    gin = g["w_in_p"]
    w_in = jnp.concatenate([gin[:, :o3], gin[:, o3 + QK_NOPE:o3 + QK_DIM], gin[:, o3 + HEAD_PAD:]], axis=1)
    hw = heads * HEAD_PAD
    gk = g["w_kv_p"][:, :hw].reshape(KV_RANK, heads, HEAD_PAD)[:, :, :QK_NOPE]
    gv = g["w_kv_p"][:, hw:].reshape(KV_RANK, heads, HEAD_PAD)[:, :, :V_DIM]
    w_ukv = jnp.concatenate([gk, gv], axis=2).reshape(KV_RANK, heads * (QK_NOPE + V_DIM))
    w_pb = g["w_pb_p"].reshape(heads, HEAD_PAD, d)[:, :V_DIM].reshape(heads * V_DIM, d)
    return dict(w_in=w_in, w_uq=_unpad_heads(g["w_uq_p"], heads, QK_DIM), w_ukv=w_ukv, w_pa=g["w_pa"], w_pb=w_pb,
                w_o=g["w_o"], w_gate=g["w_gate"], w_up=g["w_up"], w_down=g["w_down"])


def _rope_tables(lp, b):
    inv = 1.0 / (ROPE_THETA ** (jnp.arange(0, QK_ROPE, 2, dtype=F32) / QK_ROPE))
    ang = jnp.arange(lp, dtype=F32)[:, None] * inv[None, :]
    cos, sin = jnp.cos(ang), jnp.sin(ang)
    half = QK_ROPE // 2
    ones = jnp.ones((lp, QK_NOPE), F32)
    zeros_lo = jnp.zeros((lp, QK_NOPE), F32)
    zeros_hi = jnp.zeros((lp, HEAD_PAD - QK_DIM), F32)
    zeros_half = jnp.zeros((lp, half), F32)
    c = jnp.concatenate([ones, cos, cos, zeros_hi], axis=1)
    sa = jnp.concatenate([zeros_lo, zeros_half, sin, zeros_hi], axis=1)
    sb = jnp.concatenate([zeros_lo, -sin, zeros_half, zeros_hi], axis=1)
    return tuple(jnp.tile(tab, (b, 1)) for tab in (c, sa, sb))


def _local_step(x, loss_target, meta_tokens, small, big):
    b, seq, d = x.shape
    depth = big["w_in"].shape[0]
    heads = big["w_uq"].shape[2] // QK_DIM
    lp = _round_up(N_META + seq, ATT_BLOCK)
    t = b * lp
    pad = lp - N_META - seq

    meta = jnp.broadcast_to(meta_tokens[None], (b, N_META, d))
    h = jnp.concatenate([meta, x, jnp.zeros((b, pad, d), F32)], axis=1).reshape(t, d)
    target = jnp.pad(loss_target, ((0, 0), (N_META, pad), (0, 0))).reshape(t, d)
    pos = jnp.arange(lp)
    valid = jnp.tile(((pos >= N_META) & (pos < N_META + seq)).astype(F32), b).reshape(t, 1)
    rope_c, rope_sa, rope_sb = _rope_tables(lp, b)

    layers = []
    for li in range(depth):
        lay = _layer_layouts({n: big[n][li] for n in BIG_WEIGHTS}, heads)
        lay["pool_w"] = small["pool_w"][li].astype(BF16)
        lay["pool_w_t"] = jnp.swapaxes(lay["pool_w"], 1, 2)
        lay["pool_scale"] = small["pool_scale"][li][None]
        for n in ("norm_mix_g", "q_norm_g", "kv_norm_g", "norm_ffn_g"):
            lay[n] = small[n][li][None]
        layers.append(lay)

    saved = []
    for lay in layers:
        z = _in_proj_fwd(h, lay["norm_mix_g"], lay["w_in_p"])
        a = _pool_fwd(z.reshape(b, lp, -1), lay["pool_w"], lay["pool_scale"]).reshape(t, POOL_WIDTH)
        q, k, v = _qkv_fwd(z, lay["q_norm_g"], lay["kv_norm_g"], lay["w_uq_p"], lay["w_kv_p"], rope_c, rope_sa, rope_sb)
        hw = q.shape[1]
        o3, lse = _attn_fwd(q.reshape(b, lp, hw), k.reshape(b, lp, hw), v.reshape(b, lp, hw))
        o = o3.reshape(t, hw)
        h1, pa, pb = _merge_fwd(h, z, a, o, lay["w_pa"], lay["w_pb_p"], lay["w_o"])
        h2, fa, fb = _ffn_fwd(h1, lay["norm_ffn_g"], lay["w_gate"], lay["w_up"], lay["w_down"])
        saved.append(dict(h=h, z=z, a=a, q=q, k=k, v=v, o=o, lse=lse, pa=pa, pb=pb, h1=h1, fa=fa, fb=fb))
        h = h2

    dh, loss, d_final = _loss_head(h, small["final_norm_g"][None], target, valid)

    g_small = {n: [] for n in SMALL_WEIGHTS if n != "final_norm_g"}
    g_big = {n: [] for n in BIG_WEIGHTS}
    for lay, sv in zip(reversed(layers), reversed(saved)):
        hw = sv["q"].shape[1]
        dh1, hn_f, act, dfa, dfb, dg_ffn = _ffn_bwd(sv["h1"], dh, sv["fa"], sv["fb"], lay["norm_ffn_g"],
                                                     lay["w_gate_t"], lay["w_up_t"], lay["w_down_t"])
        gl = dict(w_gate=_weight_grad(hn_f, dfa, "grad_w_gate"), w_up=_weight_grad(hn_f, dfb, "grad_w_up"),
                  w_down=_weight_grad(act, dh, "grad_w_down"))
        merged, dpa, dpb, dga, dgb, da, do = _merge_bwd(dh1, sv["z"], sv["pa"], sv["pb"], lay["w_o_t"], lay["w_pa_t"], lay["w_pb_p_t"])
        gl["w_o"] = _weight_grad(merged, dh1, "grad_w_o")
        gl["w_pa"] = _weight_grad(sv["a"], dpa, "grad_w_pa")
        gl["w_pb_p"] = _weight_grad(sv["o"], dpb, "grad_w_pb")
        shape3 = (b, lp, hw)
        dq3, dk3, dv3 = _attn_bwd(sv["q"].reshape(shape3), sv["k"].reshape(shape3), sv["v"].reshape(shape3),
                                  sv["o"].reshape(shape3), do.reshape(shape3), sv["lse"])
        dqraw, dkvraw, cqn, ckvn, dcq, dckv, dkr, dg_q, dg_kv = _qkv_bwd(
            dq3.reshape(t, hw), dk3.reshape(t, hw), dv3.reshape(t, hw), sv["z"], lay["q_norm_g"], lay["kv_norm_g"],
            lay["w_uq_p_t"], lay["w_kv_p_t"], rope_c, rope_sa, rope_sb)
        gl["w_uq_p"] = _weight_grad(cqn, dqraw, "grad_w_uq")
        gl["w_kv_p"] = _weight_grad(ckvn, dkvraw, "grad_w_ukv")
        du3, dpool_w, dpool_scale = _pool_bwd(sv["z"].reshape(b, lp, -1), da.reshape(b, lp, POOL_WIDTH),
                                              lay["pool_w"], lay["pool_w_t"], lay["pool_scale"])
        dh, hn_m, dz, dg_mix = _in_proj_bwd(sv["h"], dh1, du3.reshape(t, POOL_WIDTH), dcq, dckv, dkr, dga, dgb,
                                            lay["norm_mix_g"], lay["w_in_p_t"])
        gl["w_in_p"] = _weight_grad(hn_m, dz, "grad_w_in")
        for n, val in _layer_grad_layouts(gl, heads, d).items():
            g_big[n].insert(0, val)
        for n, val in (("norm_mix_g", dg_mix[0]), ("pool_w", dpool_w), ("pool_scale", dpool_scale[0]), ("q_norm_g", dg_q[0]),
                       ("kv_norm_g", dg_kv[0]), ("norm_ffn_g", dg_ffn[0])):
            g_small[n].insert(0, val)

    dh3 = dh.reshape(b, lp, d)
    grad_x = dh3[:, N_META:N_META + seq]
    d_meta_rows = dh3[:, :N_META]
    g_small = {n: jnp.stack(v) for n, v in g_small.items()}
    g_small["final_norm_g"] = d_final[0]
    g_big = {n: jnp.stack(v) for n, v in g_big.items()}
    return loss, grad_x, d_meta_rows, g_small, g_big


def _mesh_place():
    x, y, c = lax.axis_index("x"), lax.axis_index("y"), lax.axis_index("c")
    others = [(1 - x, y), (x, 1 - y), (1 - x, 1 - y)]
    return x, y, c, 2 * x + y, others


def _remote(src, dst, send_sems, recv_sems, k, device):
    return pltpu.make_async_remote_copy(src_ref=src, dst_ref=dst, send_sem=send_sems.at[k], recv_sem=recv_sems.at[k],
                                        device_id=device, device_id_type=MESH)


def _all_gather_shards(packed, meta_shard):
    _, rh, cols = packed.shape

    def body(p_ref, m_ref, g_ref, gm_ref, send_sems, recv_sems, local_sems):
        x, y, c, chip, others = _mesh_place()
        sibling = (x, y, 1 - c)
        own = pltpu.make_async_copy(p_ref, g_ref.at[chip], local_sems.at[0])
        own_meta = pltpu.make_async_copy(m_ref, gm_ref.at[chip], local_sems.at[1])
        own.start()
        own_meta.start()
        sends = []
        for r, (ox, oy) in enumerate(others):
            sends.append(_remote(p_ref.at[c], g_ref.at[chip, c], send_sems, recv_sems, r, (ox, oy, c)))
            sends.append(_remote(m_ref, gm_ref.at[chip], send_sems, recv_sems, 6 + r, (ox, oy, c)))
        for cp in sends:
            cp.start()
        for r, (ox, oy) in enumerate(others):
            src_chip = 2 * ox + oy
            _remote(p_ref.at[c], g_ref.at[src_chip, c], send_sems, recv_sems, r, (ox, oy, c)).wait_recv()
            passed = _remote(g_ref.at[src_chip, c], g_ref.at[src_chip, c], send_sems, recv_sems, 3 + r, sibling)
            passed.start()
            sends.append(passed)
        for r, (ox, oy) in enumerate(others):
            src_chip = 2 * ox + oy
            _remote(p_ref.at[c], g_ref.at[src_chip, 1 - c], send_sems, recv_sems, 3 + r, sibling).wait_recv()
            _remote(m_ref, gm_ref.at[src_chip], send_sems, recv_sems, 6 + r, (ox, oy, c)).wait_recv()
        for cp in sends:
            cp.wait_send()
        own.wait()
        own_meta.wait()

    return pl.pallas_call(
        body, name="all_gather_shards",
        in_specs=[_any_spec(), _any_spec()], out_specs=[_any_spec(), _any_spec()],
        out_shape=[jax.ShapeDtypeStruct((N_CHIPS, 2, rh, cols), packed.dtype),
                   jax.ShapeDtypeStruct((N_CHIPS,) + meta_shard.shape, meta_shard.dtype)],
        scratch_shapes=[pltpu.SemaphoreType.DMA((9,)), pltpu.SemaphoreType.DMA((9,)), pltpu.SemaphoreType.DMA((2,))],
    )(packed, meta_shard)


def _pair_exchange(give):
    def body(give_ref, got_ref, send_sems, recv_sems):
        x, y, c, _, _ = _mesh_place()
        cp = _remote(give_ref, got_ref, send_sems, recv_sems, 0, (x, y, 1 - c))
        cp.start()
        cp.wait()

    return pl.pallas_call(
        body, name="pair_exchange", in_specs=[_any_spec()], out_specs=_any_spec(),
        out_shape=jax.ShapeDtypeStruct(give.shape, give.dtype),
        scratch_shapes=[pltpu.SemaphoreType.DMA((1,)), pltpu.SemaphoreType.DMA((1,))],
    )(give)


def _chip_exchange(parts):
    def body(p_ref, got_ref, send_sems, recv_sems, local_sem):
        _, _, c, chip, others = _mesh_place()
        own = pltpu.make_async_copy(p_ref.at[chip], got_ref.at[chip], local_sem.at[0])
        own.start()
        sends = [_remote(p_ref.at[2 * ox + oy], got_ref.at[chip], send_sems, recv_sems, r, (ox, oy, c))
                 for r, (ox, oy) in enumerate(others)]
        for cp in sends:
            cp.start()
        for r, (ox, oy) in enumerate(others):
            _remote(p_ref.at[chip], got_ref.at[2 * ox + oy], send_sems, recv_sems, r, (ox, oy, c)).wait_recv()
        for cp in sends:
            cp.wait_send()
        own.wait()

    return pl.pallas_call(
        body, name="chip_exchange", in_specs=[_any_spec()], out_specs=_any_spec(),
        out_shape=jax.ShapeDtypeStruct(parts.shape, parts.dtype),
        scratch_shapes=[pltpu.SemaphoreType.DMA((3,)), pltpu.SemaphoreType.DMA((3,)), pltpu.SemaphoreType.DMA((1,))],
    )(parts)


def _pair_gather(half):
    def body(h_ref, out_ref, send_sems, recv_sems, local_sem):
        x, y, c, _, _ = _mesh_place()
        own = pltpu.make_async_copy(h_ref, out_ref.at[c], local_sem.at[0])
        own.start()
        cp = _remote(h_ref, out_ref.at[c], send_sems, recv_sems, 0, (x, y, 1 - c))
        cp.start()
        _remote(h_ref, out_ref.at[1 - c], send_sems, recv_sems, 0, (x, y, 1 - c)).wait_recv()
        cp.wait_send()
        own.wait()

    return pl.pallas_call(
        body, name="pair_gather", in_specs=[_any_spec()], out_specs=_any_spec(),
        out_shape=jax.ShapeDtypeStruct((2,) + half.shape, half.dtype),
        scratch_shapes=[pltpu.SemaphoreType.DMA((1,)), pltpu.SemaphoreType.DMA((1,)), pltpu.SemaphoreType.DMA((1,))],
    )(half)


def _row_tile(rows, limit=512):
    if rows <= limit:
        return rows
    for tr in range(limit, 7, -8):
        if rows % tr == 0:
            return tr
    return rows


def _pair_add(keep, got):
    n, rh, cols = keep.shape
    tr = _row_tile(rh)

    def body(k_ref, g_ref, o_ref):
        o_ref[...] = (k_ref[...] + g_ref[...].astype(F32)).astype(BF16)

    spec = pl.BlockSpec((1, tr, cols), lambda j, i: (j, i, 0))
    return pl.pallas_call(
        body, name="pair_add", grid=(n, rh // tr), in_specs=[spec, spec], out_specs=spec,
        out_shape=jax.ShapeDtypeStruct(keep.shape, BF16),
        compiler_params=_params(("parallel", "parallel")),
    )(keep, got)


def _chip_sum(parts):
    n, rh, cols = parts.shape
    tr = _row_tile(rh)

    def body(p_ref, o_ref):
        total = p_ref[0].astype(F32)
        for k in range(1, n):
            total = total + p_ref[k].astype(F32)
        o_ref[...] = total

    return pl.pallas_call(
        body, name="chip_sum", grid=(rh // tr,),
        in_specs=[pl.BlockSpec((n, tr, cols), lambda i: (0, i, 0))], out_specs=pl.BlockSpec((tr, cols), lambda i: (i, 0)),
        out_shape=jax.ShapeDtypeStruct((rh, cols), F32),
        compiler_params=_params(("parallel",)),
    )(parts)


def _reduce_scatter(grads, c):
    keep = lax.dynamic_index_in_dim(grads, c, axis=1, keepdims=False)
    give = lax.dynamic_index_in_dim(grads, 1 - c, axis=1, keepdims=False).astype(BF16)
    chip_partial = _pair_add(keep, _pair_exchange(give))
    return _pair_gather(_chip_sum(_chip_exchange(chip_partial)))


def _all_reduce_small(meta_rows, small):
    b, rm, cols = meta_rows.shape
    rows = rm + small.shape[0]

    def body(meta_ref, small_ref, out_ref, mine, pair_buf, chip_buf, send_sems, recv_sems):
        x, y, c, chip, others = _mesh_place()
        acc = meta_ref[0]
        for i in range(1, b):
            acc = acc + meta_ref[i]
        mine[0:rm, :] = acc
        mine[rm:rows, :] = small_ref[...]
        pair = _remote(mine, pair_buf, send_sems, recv_sems, 0, (x, y, 1 - c))
        pair.start()
        pair.wait()
        chip_buf[chip] = mine[...] + pair_buf[...]
        sends = [_remote(chip_buf.at[chip], chip_buf.at[chip], send_sems, recv_sems, 1 + r, (ox, oy, c))
                 for r, (ox, oy) in enumerate(others)]
        for cp in sends:
            cp.start()
        for r, (ox, oy) in enumerate(others):
            _remote(chip_buf.at[chip], chip_buf.at[2 * ox + oy], send_sems, recv_sems, 1 + r, (ox, oy, c)).wait_recv()
        for cp in sends:
            cp.wait_send()
        out_ref[...] = ((chip_buf[0] + chip_buf[1]) + chip_buf[2]) + chip_buf[3]

    return pl.pallas_call(
        body, name="all_reduce_small",
        in_specs=[_vmem_spec(), _vmem_spec()], out_specs=_vmem_spec(),
        out_shape=jax.ShapeDtypeStruct((rows, cols), F32),
        scratch_shapes=[pltpu.VMEM((rows, cols), F32), pltpu.VMEM((rows, cols), F32), pltpu.VMEM((N_CHIPS, rows, cols), F32),
                        pltpu.SemaphoreType.DMA((4,)), pltpu.SemaphoreType.DMA((4,))],
        compiler_params=pltpu.CompilerParams(vmem_limit_bytes=VMEM_LIMIT),
    )(meta_rows, small)


def _adamw(w, g, m, v):
    shape = w.shape
    cols = shape[-1]
    rows = w.size // cols
    tr = _row_tile(rows)

    def body(w_ref, g_ref, m_ref, v_ref, d_ref, m2_ref, v2_ref):
        grad = g_ref[...]
        m2 = ADAM_B1 * m_ref[...] + (1.0 - ADAM_B1) * grad
        v2 = ADAM_B2 * v_ref[...] + (1.0 - ADAM_B2) * jnp.square(grad)
        m_hat = m2 / (1.0 - ADAM_B1 ** ADAM_STEP)
        v_hat = v2 / (1.0 - ADAM_B2 ** ADAM_STEP)
        d_ref[...] = -ADAM_LR * (m_hat / (jnp.sqrt(v_hat) + ADAM_EPS) + ADAM_WD * w_ref[...])
        m2_ref[...] = m2
        v2_ref[...] = v2

    spec = pl.BlockSpec((tr, cols), lambda i: (i, 0))
    out = jax.ShapeDtypeStruct((rows, cols), F32)
    res = pl.pallas_call(
        body, name="adamw", grid=(rows // tr,), in_specs=[spec] * 4, out_specs=[spec] * 3, out_shape=[out] * 3,
        compiler_params=_params(("parallel",)),
    )(*(a.reshape(rows, cols) for a in (w, g, m, v)))
    return tuple(r.reshape(shape) for r in res)


def _pack_rows(arrays):
    flat = [a.reshape(-1, PACK_COLS) for a in arrays]
    counts = [f.shape[0] for f in flat]
    total = sum(counts)
    padded = _round_up(total, 32)
    if padded > total:
        flat.append(jnp.zeros((padded - total, PACK_COLS), flat[0].dtype))
    return jnp.concatenate(flat, axis=0), counts


def _unpack_rows(buffer, counts, shapes):
    out, start = [], 0
    for n, shape in zip(counts, shapes):
        out.append(buffer[..., start:start + n, :].reshape(buffer.shape[:-2] + tuple(shape)))
        start += n
    return out


def kernel(x, meta_tokens, norm_mix_g, w_in, pool_w, pool_scale, q_norm_g, kv_norm_g, w_uq, w_ukv, w_pa, w_pb, w_o, norm_ffn_g, w_gate, w_up, w_down, final_norm_g, loss_target, m_meta_tokens, m_norm_mix_g, m_w_in, m_pool_w, m_pool_scale, m_q_norm_g, m_kv_norm_g, m_w_uq, m_w_ukv, m_w_pa, m_w_pb, m_w_o, m_norm_ffn_g, m_w_gate, m_w_up, m_w_down, m_final_norm_g, v_meta_tokens, v_norm_mix_g, v_w_in, v_pool_w, v_pool_scale, v_q_norm_g, v_kv_norm_g, v_w_uq, v_w_ukv, v_w_pa, v_w_pb, v_w_o, v_norm_ffn_g, v_w_gate, v_w_up, v_w_down, v_final_norm_g):
    weights = dict(meta_tokens=meta_tokens, norm_mix_g=norm_mix_g, w_in=w_in, pool_w=pool_w, pool_scale=pool_scale,
                   q_norm_g=q_norm_g, kv_norm_g=kv_norm_g, w_uq=w_uq, w_ukv=w_ukv, w_pa=w_pa, w_pb=w_pb, w_o=w_o,
                   norm_ffn_g=norm_ffn_g, w_gate=w_gate, w_up=w_up, w_down=w_down, final_norm_g=final_norm_g)
    first = dict(meta_tokens=m_meta_tokens, norm_mix_g=m_norm_mix_g, w_in=m_w_in, pool_w=m_pool_w, pool_scale=m_pool_scale,
                 q_norm_g=m_q_norm_g, kv_norm_g=m_kv_norm_g, w_uq=m_w_uq, w_ukv=m_w_ukv, w_pa=m_w_pa, w_pb=m_w_pb, w_o=m_w_o,
                 norm_ffn_g=m_norm_ffn_g, w_gate=m_w_gate, w_up=m_w_up, w_down=m_w_down, final_norm_g=m_final_norm_g)
    second = dict(meta_tokens=v_meta_tokens, norm_mix_g=v_norm_mix_g, w_in=v_w_in, pool_w=v_pool_w, pool_scale=v_pool_scale,
                  q_norm_g=v_q_norm_g, kv_norm_g=v_kv_norm_g, w_uq=v_w_uq, w_ukv=v_w_ukv, w_pa=v_w_pa, w_pb=v_w_pb, w_o=v_w_o,
                  norm_ffn_g=v_norm_ffn_g, w_gate=v_w_gate, w_up=v_w_up, w_down=v_w_down, final_norm_g=v_final_norm_g)
    core = lax.axis_index("c")
    chip = 2 * lax.axis_index("x") + lax.axis_index("y")
    d = x.shape[-1]
    meta_cols = meta_tokens.shape[1]

    shard_shapes = [weights[n].shape for n in BIG_WEIGHTS]
    packed, counts = _pack_rows([weights[n].astype(BF16) for n in BIG_WEIGHTS])
    total_rows = packed.shape[0]
    gathered, meta_all = _all_gather_shards(packed.reshape(2, total_rows // 2, PACK_COLS), meta_tokens)
    per_chip = _unpack_rows(gathered.reshape(N_CHIPS, total_rows, PACK_COLS), counts, shard_shapes)
    big = {n: jnp.concatenate([per_chip[i][j] for j in range(N_CHIPS)], axis=SHARD_AXIS[n]) for i, n in enumerate(BIG_WEIGHTS)}
    meta_full = jnp.concatenate([meta_all[j] for j in range(N_CHIPS)], axis=1)
    small = {n: weights[n] for n in SMALL_WEIGHTS}

    loss, grad_x, d_meta_rows, g_small, g_big = _local_step(x, loss_target, meta_full, small, big)

    split = [jnp.stack(jnp.split(g_big[n], N_CHIPS, axis=SHARD_AXIS[n])).reshape(N_CHIPS, -1, PACK_COLS) for n in BIG_WEIGHTS]
    pad_rows = total_rows - sum(counts)
    if pad_rows:
        split.append(jnp.zeros((N_CHIPS, pad_rows, PACK_COLS), F32))
    partial = jnp.concatenate(split, axis=1).reshape(N_CHIPS, 2, total_rows // 2, PACK_COLS)
    reduced = _reduce_scatter(partial, core).reshape(total_rows, PACK_COLS)
    grads = dict(zip(BIG_WEIGHTS, _unpack_rows(reduced, counts, shard_shapes)))

    small_shapes = [weights[n].shape for n in SMALL_WEIGHTS]
    small_flat = jnp.concatenate([g_small[n].reshape(-1) for n in SMALL_WEIGHTS])
    small_len = small_flat.shape[0]
    small_rows = _round_up(-(-small_len // PACK_COLS), 8)
    small_pack = jnp.pad(small_flat, (0, small_rows * PACK_COLS - small_len)).reshape(small_rows, PACK_COLS)
    meta_rows = N_META * d // PACK_COLS
    summed = _all_reduce_small(d_meta_rows.reshape(-1, meta_rows, PACK_COLS), small_pack)
    grad_meta_full = summed[:meta_rows].reshape(N_META, d)
    grads["meta_tokens"] = lax.dynamic_slice_in_dim(grad_meta_full, chip * meta_cols, meta_cols, axis=1)
    small_sum = summed[meta_rows:].reshape(-1)
    start = 0
    for n, shape in zip(SMALL_WEIGHTS, small_shapes):
        size = 1
        for s in shape:
            size *= s
        grads[n] = small_sum[start:start + size].reshape(shape)
        start += size

    deltas, new_m, new_v = {}, {}, {}
    for n in WEIGHT_ORDER:
        deltas[n], new_m[n], new_v[n] = _adamw(weights[n], grads[n], first[n], second[n])

    total_loss = lax.psum(loss[0, 0], ("x", "y", "c"))
    return (total_loss, grad_x, *[grads[n] for n in WEIGHT_ORDER], *[deltas[n] for n in WEIGHT_ORDER],
            *[new_m[n] for n in WEIGHT_ORDER], *[new_v[n] for n in WEIGHT_ORDER])
```

```python
import functools

import jax
import jax.numpy as jnp
from jax import lax
from jax.experimental import pallas as pl
from jax.experimental.pallas import tpu as pltpu

F32 = jnp.float32
BF16 = jnp.bfloat16

N_META = 16
POOL_WINDOWS = (2, 4, 8, 16)
POOL_GROUP = 128
POOL_WIDTH = POOL_GROUP * len(POOL_WINDOWS)
QK_NOPE = 64
QK_ROPE = 32
V_DIM = 64
QK_DIM = QK_NOPE + QK_ROPE
Q_RANK = 256
KV_RANK = 128
HEAD_PAD = 128
SM_SCALE = QK_DIM ** -0.5
ROPE_THETA = 10000.0
NORM_EPS = 1e-6
MASK_VALUE = -1e30
Z_FIXED = POOL_WIDTH + Q_RANK + KV_RANK + HEAD_PAD

ADAM_LR = 0.001
ADAM_B1 = 0.9
ADAM_B2 = 0.999
ADAM_EPS = 1e-08
ADAM_WD = 0.01
ADAM_STEP = 10

N_CHIPS = 4
ATT_BLOCK = 256
ATT_FWD_HEADS = 4
ATT_BWD_HEADS = 2
PACK_COLS = 1024
PACK_TILE = 512
VMEM_LIMIT = 60 * 1024 * 1024

MESH = pl.DeviceIdType.MESH

BIG_WEIGHTS = ("w_in", "w_uq", "w_ukv", "w_pa", "w_pb", "w_o", "w_gate", "w_up", "w_down")
SHARD_AXIS = {"w_in": 2, "w_uq": 2, "w_ukv": 2, "w_pa": 2, "w_pb": 1, "w_o": 1, "w_gate": 2, "w_up": 2, "w_down": 1}
SMALL_WEIGHTS = ("norm_mix_g", "pool_w", "pool_scale", "q_norm_g", "kv_norm_g", "norm_ffn_g", "final_norm_g")
WEIGHT_ORDER = ("meta_tokens", "norm_mix_g", "w_in", "pool_w", "pool_scale", "q_norm_g", "kv_norm_g", "w_uq", "w_ukv",
                "w_pa", "w_pb", "w_o", "norm_ffn_g", "w_gate", "w_up", "w_down", "final_norm_g")


def _round_up(n, m):
    return -(-n // m) * m


def _vmem_spec():
    return pl.BlockSpec(memory_space=pltpu.VMEM)


def _any_spec():
    return pl.BlockSpec(memory_space=pl.ANY)


def _row_block(tm, width, col_block=0):
    return pl.BlockSpec((tm, width), lambda i, cb=col_block: (i, cb))


def _params(sem, vmem=VMEM_LIMIT):
    return pltpu.CompilerParams(dimension_semantics=sem, vmem_limit_bytes=vmem)


def _token_tile(t, want):
    return want if t % want == 0 else ATT_BLOCK


def _dot(a, b):
    return jnp.dot(a, b, preferred_element_type=F32)


def _dot_nt(a, b):
    return lax.dot_general(a, b, (((1,), (1,)), ((), ())), preferred_element_type=F32)


def _dot_tn(a, b):
    return lax.dot_general(a, b, (((0,), (0,)), ((), ())), preferred_element_type=F32)


def _rms_fwd(x, g):
    r = lax.rsqrt(jnp.mean(x * x, axis=-1, keepdims=True) + NORM_EPS)
    xh = x * r
    return xh * g, xh, r


def _rms_bwd(dy, xh, r, g):
    gdy = dy * g
    dx = r * (gdy - xh * jnp.mean(xh * gdy, axis=-1, keepdims=True))
    return dx, dy * xh


def _rope_fwd(x, c, sa, sb):
    return x * c + pltpu.roll(x, 16, 1) * sa + pltpu.roll(x, HEAD_PAD - 16, 1) * sb


def _rope_bwd(d, c, sa, sb):
    return d * c + pltpu.roll(d * sa, HEAD_PAD - 16, 1) + pltpu.roll(d * sb, 16, 1)


def _in_proj_fwd(h, g, w_in_p):
    t, d = h.shape
    nz = w_in_p.shape[1]
    tm = _token_tile(t, 512)

    def body(h_ref, g_ref, w_ref, z_ref):
        hn, _, _ = _rms_fwd(h_ref[...], g_ref[...])
        z_ref[...] = _dot(hn.astype(BF16), w_ref[...]).astype(BF16)

    return pl.pallas_call(
        body, name="in_proj_fwd", grid=(t // tm,),
        in_specs=[_row_block(tm, d), _vmem_spec(), _vmem_spec()],
        out_specs=_row_block(tm, nz),
        out_shape=jax.ShapeDtypeStruct((t, nz), BF16),
        compiler_params=_params(("parallel",)),
    )(h, g, w_in_p)


def _window_sum(x, w, row, forward):
    n = x.shape[0]
    s = x
    k = 1
    while k < w:
        if forward:
            s = s + jnp.where(row >= k, pltpu.roll(s, k, 0), 0.0)
        else:
            s = s + jnp.where(row < n - k, pltpu.roll(s, n - k, 0), 0.0)
        k *= 2
    return s


def _pool_fwd(z3, pool_w, pool_scale):
    b, lp, _ = z3.shape

    def body(u_ref, pw_ref, sc_ref, a_ref):
        row = lax.broadcasted_iota(jnp.int32, (lp, POOL_GROUP), 0)
        pos = row.astype(F32)
        for gi, w in enumerate(POOL_WINDOWS):
            cols = slice(gi * POOL_GROUP, (gi + 1) * POOL_GROUP)
            u = u_ref[0, :, cols].astype(F32)
            y = _window_sum(u, w, row, True) / jnp.minimum(pos + 1.0, float(w)) - u
            yw = _dot(y.astype(BF16), pw_ref[gi])
            a_ref[0, :, cols] = (yw * sc_ref[:, cols]).astype(BF16)

    return pl.pallas_call(
        body, name="pool_fwd", grid=(b,),
        in_specs=[pl.BlockSpec((1, lp, POOL_WIDTH), lambda i: (i, 0, 0)), _vmem_spec(), _vmem_spec()],
        out_specs=pl.BlockSpec((1, lp, POOL_WIDTH), lambda i: (i, 0, 0)),
        out_shape=jax.ShapeDtypeStruct((b, lp, POOL_WIDTH), BF16),
        compiler_params=_params(("parallel",)),
    )(z3, pool_w, pool_scale)


def _qkv_fwd(z, g_q, g_kv, w_uq_p, w_kv_p, rope_c, rope_sa, rope_sb):
    t = z.shape[0]
    hw = w_uq_p.shape[1]
    heads = hw // HEAD_PAD
    tm = _token_tile(t, 512)

    def body(cq_ref, ckv_ref, kr_ref, gq_ref, gkv_ref, wq_ref, wkv_ref, c_ref, sa_ref, sb_ref, q_ref, k_ref, v_ref):
        c, sa, sb = c_ref[...], sa_ref[...], sb_ref[...]
        cqn, _, _ = _rms_fwd(cq_ref[...].astype(F32), gq_ref[...])
        qraw = _dot(cqn.astype(BF16), wq_ref[...])
        ckvn, _, _ = _rms_fwd(ckv_ref[...].astype(F32), gkv_ref[...])
        kvraw = _dot(ckvn.astype(BF16), wkv_ref[...])
        kr = kr_ref[...].astype(F32)
        for hd in range(heads):
            cols = slice(hd * HEAD_PAD, (hd + 1) * HEAD_PAD)
            q_ref[:, cols] = _rope_fwd(qraw[:, cols], c, sa, sb).astype(BF16)
            k_ref[:, cols] = _rope_fwd(kvraw[:, cols] + kr, c, sa, sb).astype(BF16)
        v_ref[...] = kvraw[:, hw:].astype(BF16)

    out = jax.ShapeDtypeStruct((t, hw), BF16)
    return pl.pallas_call(
        body, name="qkv_fwd", grid=(t // tm,),
        in_specs=[_row_block(tm, Q_RANK, POOL_WIDTH // Q_RANK),
                  _row_block(tm, KV_RANK, (POOL_WIDTH + Q_RANK) // KV_RANK),
                  _row_block(tm, HEAD_PAD, (POOL_WIDTH + Q_RANK + KV_RANK) // HEAD_PAD),
                  _vmem_spec(), _vmem_spec(), _vmem_spec(), _vmem_spec(),
                  _row_block(tm, HEAD_PAD), _row_block(tm, HEAD_PAD), _row_block(tm, HEAD_PAD)],
        out_specs=[_row_block(tm, hw)] * 3,
        out_shape=[out, out, out],
        compiler_params=_params(("parallel",)),
    )(z, z, z, g_q, g_kv, w_uq_p, w_kv_p, rope_c, rope_sa, rope_sb)


def _heads_per_step(heads, want):
    while heads % want:
        want //= 2
    return want


def _causal_mask(tb):
    row = lax.broadcasted_iota(jnp.int32, (tb, tb), 0)
    col = lax.broadcasted_iota(jnp.int32, (tb, tb), 1)
    return col <= row


def _attn_fwd(q3, k3, v3):
    b, lp, hw = q3.shape
    heads = hw // HEAD_PAD
    tb = ATT_BLOCK
    nblk = lp // tb
    hpg = _heads_per_step(heads, ATT_FWD_HEADS)
    width = hpg * HEAD_PAD

    def body(q_ref, k_ref, v_ref, o_ref, lse_ref):
        group = pl.program_id(1)

        @pl.when(group == 0)
        def _():
            lse_ref[...] = jnp.zeros_like(lse_ref)

        lane = lax.broadcasted_iota(jnp.int32, (tb, HEAD_PAD), 1)
        mask = _causal_mask(tb)

        def q_block(i, carry):
            r0 = pl.multiple_of(i * tb, tb)

            def kv_block(j, states, diagonal):
                c0 = pl.multiple_of(j * tb, tb)
                out = []
                for hd, (m, l, acc) in enumerate(states):
                    cols = slice(hd * HEAD_PAD, (hd + 1) * HEAD_PAD)
                    s = _dot_nt(q_ref[0, pl.ds(r0, tb), cols], k_ref[0, pl.ds(c0, tb), cols]) * SM_SCALE
                    if diagonal:
                        s = jnp.where(mask, s, MASK_VALUE)
                    m_new = jnp.maximum(m, jnp.max(s, axis=-1, keepdims=True))
                    alpha = jnp.exp(m - m_new)
                    p = jnp.exp(s - m_new)
                    l = alpha * l + jnp.sum(p, axis=-1, keepdims=True)
                    acc = alpha * acc + _dot(p.astype(BF16), v_ref[0, pl.ds(c0, tb), cols])
                    out.append((m_new, l, acc))
                return tuple(out)

            init = tuple((jnp.full((tb, 1), MASK_VALUE, F32), jnp.zeros((tb, 1), F32), jnp.zeros((tb, HEAD_PAD), F32))
                         for _ in range(hpg))
            states = lax.fori_loop(0, i, lambda j, st: kv_block(j, st, False), init)
            states = kv_block(i, states, True)
            lse_rows = lse_ref[0, pl.ds(r0, tb), :]
            for hd, (m, l, acc) in enumerate(states):
                o_ref[0, pl.ds(r0, tb), hd * HEAD_PAD:(hd + 1) * HEAD_PAD] = (acc / l).astype(BF16)
                lse_rows = jnp.where(lane == group * hpg + hd, m + jnp.log(l), lse_rows)
            lse_ref[0, pl.ds(r0, tb), :] = lse_rows
            return carry

        lax.fori_loop(0, nblk, q_block, 0)

    head_spec = pl.BlockSpec((1, lp, width), lambda bi, hi: (bi, 0, hi))
    return pl.pallas_call(
        body, name="attn_fwd", grid=(b, heads // hpg),
        in_specs=[head_spec, head_spec, head_spec],
        out_specs=[head_spec, pl.BlockSpec((1, lp, HEAD_PAD), lambda bi, hi: (bi, 0, 0))],
        out_shape=[jax.ShapeDtypeStruct((b, lp, hw), BF16), jax.ShapeDtypeStruct((b, lp, HEAD_PAD), F32)],
        compiler_params=_params(("parallel", "arbitrary")),
    )(q3, k3, v3)


def _merge_fwd(h, z, a, o, w_pa, w_pb_p, w_o):
    t, d = h.shape
    hw = o.shape[1]
    tm = _token_tile(t, 512)
    gate_block = Z_FIXED // d

    def body(h_ref, ga_ref, gb_ref, a_ref, o_ref, wpa_ref, wpb_ref, wo_ref, h1_ref, pa_ref, pb_ref):
        pa = _dot(a_ref[...], wpa_ref[...])
        pb = _dot(o_ref[...], wpb_ref[...])
        merged = jax.nn.sigmoid(ga_ref[...].astype(F32)) * pa + jax.nn.sigmoid(gb_ref[...].astype(F32)) * pb
        h1_ref[...] = h_ref[...] + _dot(merged.astype(BF16), wo_ref[...])
        pa_ref[...] = pa.astype(BF16)
        pb_ref[...] = pb.astype(BF16)

    return pl.pallas_call(
        body, name="merge_fwd", grid=(t // tm,),
        in_specs=[_row_block(tm, d), _row_block(tm, d, gate_block), _row_block(tm, d, gate_block + 1),
                  _row_block(tm, POOL_WIDTH), _row_block(tm, hw), _vmem_spec(), _vmem_spec(), _vmem_spec()],
        out_specs=[_row_block(tm, d)] * 3,
        out_shape=[jax.ShapeDtypeStruct((t, d), F32), jax.ShapeDtypeStruct((t, d), BF16), jax.ShapeDtypeStruct((t, d), BF16)],
        compiler_params=_params(("parallel",)),
    )(h, z, z, a, o, w_pa, w_pb_p, w_o)


def _ffn_fwd(h, g, w_gate, w_up, w_down):
    t, d = h.shape
    f = w_gate.shape[1]
    tm = _token_tile(t, 256)

    def body(h_ref, g_ref, wg_ref, wu_ref, wd_ref, h2_ref, a_ref, b_ref):
        x = h_ref[...]
        hn, _, _ = _rms_fwd(x, g_ref[...])
        hn = hn.astype(BF16)
        ga = _dot(hn, wg_ref[...])
        up = _dot(hn, wu_ref[...])
        act = ga * jax.nn.sigmoid(ga) * up
        h2_ref[...] = x + _dot(act.astype(BF16), wd_ref[...])
        a_ref[...] = ga.astype(BF16)
        b_ref[...] = up.astype(BF16)

    return pl.pallas_call(
        body, name="ffn_fwd", grid=(t // tm,),
        in_specs=[_row_block(tm, d), _vmem_spec(), _vmem_spec(), _vmem_spec(), _vmem_spec()],
        out_specs=[_row_block(tm, d), _row_block(tm, f), _row_block(tm, f)],
        out_shape=[jax.ShapeDtypeStruct((t, d), F32), jax.ShapeDtypeStruct((t, f), BF16), jax.ShapeDtypeStruct((t, f), BF16)],
        compiler_params=_params(("parallel",)),
    )(h, g, w_gate, w_up, w_down)


def _loss_head(h, g, target, valid):
    t, d = h.shape
    tm = _token_tile(t, 512)

    def body(h_ref, g_ref, t_ref, valid_ref, dh_ref, loss_ref, dg_ref):
        @pl.when(pl.program_id(0) == 0)
        def _():
            loss_ref[...] = jnp.zeros_like(loss_ref)
            dg_ref[...] = jnp.zeros_like(dg_ref)

        gain = g_ref[...]
        y, xh, r = _rms_fwd(h_ref[...], gain)
        err = (y - t_ref[...]) * valid_ref[...]
        per_row = jnp.sum(err * err, axis=-1, keepdims=True) / d
        loss_ref[...] += 0.5 * jnp.sum(per_row, axis=0, keepdims=True)
        dx, dg_rows = _rms_bwd(err / d, xh, r, gain)
        dh_ref[...] = dx
        dg_ref[...] += jnp.sum(dg_rows, axis=0, keepdims=True)

    return pl.pallas_call(
        body, name="loss_head", grid=(t // tm,),
        in_specs=[_row_block(tm, d), _vmem_spec(), _row_block(tm, d), _row_block(tm, 1)],
        out_specs=[_row_block(tm, d), pl.BlockSpec((1, 1), lambda i: (0, 0)), pl.BlockSpec((1, d), lambda i: (0, 0))],
        out_shape=[jax.ShapeDtypeStruct((t, d), F32), jax.ShapeDtypeStruct((1, 1), F32), jax.ShapeDtypeStruct((1, d), F32)],
        compiler_params=_params(("arbitrary",)),
    )(h, g, target, valid)


def _weight_grad(x, y, name):
    t, k = x.shape
    n = y.shape[1]
    tm = _token_tile(t, 512)
    tn = n
    while k * tn * 4 > 8 * 1024 * 1024 and tn % 256 == 0:
        tn //= 2
    steps = t // tm

    def body(x_ref, y_ref, o_ref, acc):
        @pl.when(pl.program_id(1) == 0)
        def _():
            acc[...] = jnp.zeros_like(acc)

        acc[...] += _dot_tn(x_ref[...].astype(BF16), y_ref[...].astype(BF16))

        @pl.when(pl.program_id(1) == steps - 1)
        def _():
            o_ref[...] = acc[...].astype(BF16)

    return pl.pallas_call(
        body, name=name, grid=(n // tn, steps),
        in_specs=[pl.BlockSpec((tm, k), lambda j, i: (i, 0)), pl.BlockSpec((tm, tn), lambda j, i: (i, j))],
        out_specs=pl.BlockSpec((k, tn), lambda j, i: (0, j)),
        out_shape=jax.ShapeDtypeStruct((k, n), BF16),
        scratch_shapes=[pltpu.VMEM((k, tn), F32)],
        compiler_params=_params(("parallel", "arbitrary")),
    )(x, y)


def _ffn_bwd(h, dh2, a, b, g, w_gate_t, w_up_t, w_down_t):
    t, d = h.shape
    f = a.shape[1]
    tm = _token_tile(t, 256)

    def body(h_ref, dh2_ref, a_ref, b_ref, g_ref, wgt_ref, wut_ref, wdt_ref, dh_ref, hn_ref, act_ref, da_ref, db_ref, dg_ref):
        @pl.when(pl.program_id(0) == 0)
        def _():
            dg_ref[...] = jnp.zeros_like(dg_ref)

        gain = g_ref[...]
        hn, xh, r = _rms_fwd(h_ref[...], gain)
        hn_ref[...] = hn.astype(BF16)
        dh2 = dh2_ref[...]
        dact = _dot(dh2.astype(BF16), wdt_ref[...])
        ga = a_ref[...].astype(F32)
        up = b_ref[...].astype(F32)
        sg = jax.nn.sigmoid(ga)
        silu = ga * sg
        act_ref[...] = (silu * up).astype(BF16)
        da = (dact * up * (sg * (1.0 + ga * (1.0 - sg)))).astype(BF16)
        db = (dact * silu).astype(BF16)
        da_ref[...] = da
        db_ref[...] = db
        dhn = _dot(da, wgt_ref[...]) + _dot(db, wut_ref[...])
        dx, dg_rows = _rms_bwd(dhn, xh, r, gain)
        dh_ref[...] = dh2 + dx
        dg_ref[...] += jnp.sum(dg_rows, axis=0, keepdims=True)

    return pl.pallas_call(
        body, name="ffn_bwd", grid=(t // tm,),
        in_specs=[_row_block(tm, d), _row_block(tm, d), _row_block(tm, f), _row_block(tm, f),
                  _vmem_spec(), _vmem_spec(), _vmem_spec(), _vmem_spec()],
        out_specs=[_row_block(tm, d), _row_block(tm, d), _row_block(tm, f), _row_block(tm, f), _row_block(tm, f),
                   pl.BlockSpec((1, d), lambda i: (0, 0))],
        out_shape=[jax.ShapeDtypeStruct((t, d), F32), jax.ShapeDtypeStruct((t, d), BF16), jax.ShapeDtypeStruct((t, f), BF16),
                   jax.ShapeDtypeStruct((t, f), BF16), jax.ShapeDtypeStruct((t, f), BF16), jax.ShapeDtypeStruct((1, d), F32)],
        compiler_params=_params(("arbitrary",)),
    )(h, dh2, a, b, g, w_gate_t, w_up_t, w_down_t)


def _merge_bwd(dh1, z, pa, pb, w_o_t, w_pa_t, w_pb_pt):
    t, d = dh1.shape
    hw = w_pb_pt.shape[1]
    tm = _token_tile(t, 512)
    gate_block = Z_FIXED // d

    def body(dh_ref, ga_ref, gb_ref, pa_ref, pb_ref, wot_ref, wpat_ref, wpbt_ref,
             mg_ref, dpa_ref, dpb_ref, dga_ref, dgb_ref, da_ref, do_ref):
        dm = _dot(dh_ref[...].astype(BF16), wot_ref[...])
        sa = jax.nn.sigmoid(ga_ref[...].astype(F32))
        sb = jax.nn.sigmoid(gb_ref[...].astype(F32))
        pa = pa_ref[...].astype(F32)
        pb = pb_ref[...].astype(F32)
        mg_ref[...] = (sa * pa + sb * pb).astype(BF16)
        dpa = (dm * sa).astype(BF16)
        dpb = (dm * sb).astype(BF16)
        dpa_ref[...] = dpa
        dpb_ref[...] = dpb
        dga_ref[...] = (dm * pa * (sa * (1.0 - sa))).astype(BF16)
        dgb_ref[...] = (dm * pb * (sb * (1.0 - sb))).astype(BF16)
        da_ref[...] = _dot(dpa, wpat_ref[...]).astype(BF16)
        do_ref[...] = _dot(dpb, wpbt_ref[...]).astype(BF16)

    wide = jax.ShapeDtypeStruct((t, d), BF16)
    return pl.pallas_call(
        body, name="merge_bwd", grid=(t // tm,),
        in_specs=[_row_block(tm, d), _row_block(tm, d, gate_block), _row_block(tm, d, gate_block + 1),
                  _row_block(tm, d), _row_block(tm, d), _vmem_spec(), _vmem_spec(), _vmem_spec()],
        out_specs=[_row_block(tm, d)] * 5 + [_row_block(tm, POOL_WIDTH), _row_block(tm, hw)],
        out_shape=[wide] * 5 + [jax.ShapeDtypeStruct((t, POOL_WIDTH), BF16), jax.ShapeDtypeStruct((t, hw), BF16)],
        compiler_params=_params(("parallel",)),
    )(dh1, z, z, pa, pb, w_o_t, w_pa_t, w_pb_pt)


def _attn_bwd(q3, k3, v3, o3, do3, lse3):
    b, lp, hw = q3.shape
    heads = hw // HEAD_PAD
    tb = ATT_BLOCK
    nblk = lp // tb
    hpg = _heads_per_step(heads, ATT_BWD_HEADS)
    width = hpg * HEAD_PAD

    def body(q_ref, k_ref, v_ref, o_ref, do_ref, lse_ref, dq_ref, dk_ref, dv_ref, dq_acc, lse_col, delta_col):
        group = pl.program_id(1)
        lane = lax.broadcasted_iota(jnp.int32, (lp, HEAD_PAD), 1)
        for hd in range(hpg):
            cols = slice(hd * HEAD_PAD, (hd + 1) * HEAD_PAD)
            lse_col[hd] = jnp.sum(jnp.where(lane == group * hpg + hd, lse_ref[0], 0.0), axis=-1, keepdims=True)
            delta_col[hd] = jnp.sum(do_ref[0, :, cols].astype(F32) * o_ref[0, :, cols].astype(F32), axis=-1, keepdims=True)
        dq_acc[...] = jnp.zeros_like(dq_acc)
        mask = _causal_mask(tb)

        def kv_block(j, carry):
            c0 = pl.multiple_of(j * tb, tb)

            def q_block(i, states, diagonal):
                r0 = pl.multiple_of(i * tb, tb)
                out = []
                for hd, (dk, dv) in enumerate(states):
                    cols = slice(hd * HEAD_PAD, (hd + 1) * HEAD_PAD)
                    k = k_ref[0, pl.ds(c0, tb), cols]
                    q = q_ref[0, pl.ds(r0, tb), cols]
                    do = do_ref[0, pl.ds(r0, tb), cols]
                    s = _dot_nt(q, k) * SM_SCALE
                    if diagonal:
                        s = jnp.where(mask, s, MASK_VALUE)
                    p = jnp.exp(s - lse_col[hd, pl.ds(r0, tb), :])
                    dp = _dot_nt(do, v_ref[0, pl.ds(c0, tb), cols])
                    ds = (p * (dp - delta_col[hd, pl.ds(r0, tb), :]) * SM_SCALE).astype(BF16)
                    dv = dv + _dot_tn(p.astype(BF16), do)
                    dk = dk + _dot_tn(ds, q)
                    dq_acc[pl.ds(r0, tb), cols] += _dot(ds, k)
                    out.append((dk, dv))
                return tuple(out)

            zero = jnp.zeros((tb, HEAD_PAD), F32)
            states = q_block(j, tuple((zero, zero) for _ in range(hpg)), True)
            states = lax.fori_loop(j + 1, nblk, lambda i, st: q_block(i, st, False), states)
            for hd, (dk, dv) in enumerate(states):
                cols = slice(hd * HEAD_PAD, (hd + 1) * HEAD_PAD)
                dk_ref[0, pl.ds(c0, tb), cols] = dk.astype(BF16)
                dv_ref[0, pl.ds(c0, tb), cols] = dv.astype(BF16)
            return carry

        lax.fori_loop(0, nblk, kv_block, 0)
        dq_ref[0] = dq_acc[...].astype(BF16)

    head_spec = pl.BlockSpec((1, lp, width), lambda bi, hi: (bi, 0, hi))
    out = jax.ShapeDtypeStruct((b, lp, hw), BF16)
    return pl.pallas_call(
        body, name="attn_bwd", grid=(b, heads // hpg),
        in_specs=[head_spec] * 5 + [pl.BlockSpec((1, lp, HEAD_PAD), lambda bi, hi: (bi, 0, 0))],
        out_specs=[head_spec] * 3,
        out_shape=[out, out, out],
        scratch_shapes=[pltpu.VMEM((lp, width), F32), pltpu.VMEM((hpg, lp, 1), F32), pltpu.VMEM((hpg, lp, 1), F32)],
        compiler_params=_params(("parallel", "parallel")),
    )(q3, k3, v3, o3, do3, lse3)


def _qkv_bwd(dq, dk, dv, z, g_q, g_kv, w_uq_pt, w_kv_pt, rope_c, rope_sa, rope_sb):
    t, hw = dq.shape
    heads = hw // HEAD_PAD
    tm = _token_tile(t, 512)

    def body(dq_ref, dk_ref, dv_ref, cq_ref, ckv_ref, gq_ref, gkv_ref, wqt_ref, wkvt_ref, c_ref, sa_ref, sb_ref,
             dqraw_ref, dkvraw_ref, cqn_ref, ckvn_ref, dcq_ref, dckv_ref, dkr_ref, dgq_ref, dgkv_ref):
        @pl.when(pl.program_id(0) == 0)
        def _():
            dgq_ref[...] = jnp.zeros_like(dgq_ref)
            dgkv_ref[...] = jnp.zeros_like(dgkv_ref)

        c, sa, sb = c_ref[...], sa_ref[...], sb_ref[...]
        dkr = jnp.zeros((tm, HEAD_PAD), F32)
        for hd in range(heads):
            cols = slice(hd * HEAD_PAD, (hd + 1) * HEAD_PAD)
            dqraw_ref[:, cols] = _rope_bwd(dq_ref[:, cols].astype(F32), c, sa, sb).astype(BF16)
            dkraw = _rope_bwd(dk_ref[:, cols].astype(F32), c, sa, sb)
            dkvraw_ref[:, cols] = dkraw.astype(BF16)
            dkr = dkr + dkraw
        dkvraw_ref[:, hw:] = dv_ref[...]
        lane = lax.broadcasted_iota(jnp.int32, (tm, HEAD_PAD), 1)
        dkr_ref[...] = jnp.where((lane >= QK_NOPE) & (lane < QK_DIM), dkr, 0.0).astype(BF16)

        gq = gq_ref[...]
        cqn, xh, r = _rms_fwd(cq_ref[...].astype(F32), gq)
        cqn_ref[...] = cqn.astype(BF16)
        dx, dg_rows = _rms_bwd(_dot(dqraw_ref[...], wqt_ref[...]), xh, r, gq)
        dcq_ref[...] = dx.astype(BF16)
        dgq_ref[...] += jnp.sum(dg_rows, axis=0, keepdims=True)

        gkv = gkv_ref[...]
        ckvn, xh, r = _rms_fwd(ckv_ref[...].astype(F32), gkv)
        ckvn_ref[...] = ckvn.astype(BF16)
        dx, dg_rows = _rms_bwd(_dot(dkvraw_ref[...], wkvt_ref[...]), xh, r, gkv)
        dckv_ref[...] = dx.astype(BF16)
        dgkv_ref[...] += jnp.sum(dg_rows, axis=0, keepdims=True)

    def shape(width, dtype=BF16):
        return jax.ShapeDtypeStruct((t, width), dtype)

    return pl.pallas_call(
        body, name="qkv_bwd", grid=(t // tm,),
        in_specs=[_row_block(tm, hw)] * 3
        + [_row_block(tm, Q_RANK, POOL_WIDTH // Q_RANK), _row_block(tm, KV_RANK, (POOL_WIDTH + Q_RANK) // KV_RANK)]
        + [_vmem_spec()] * 4 + [_row_block(tm, HEAD_PAD)] * 3,
        out_specs=[_row_block(tm, hw), _row_block(tm, 2 * hw), _row_block(tm, Q_RANK), _row_block(tm, KV_RANK),
                   _row_block(tm, Q_RANK), _row_block(tm, KV_RANK), _row_block(tm, HEAD_PAD),
                   pl.BlockSpec((1, Q_RANK), lambda i: (0, 0)), pl.BlockSpec((1, KV_RANK), lambda i: (0, 0))],
        out_shape=[shape(hw), shape(2 * hw), shape(Q_RANK), shape(KV_RANK), shape(Q_RANK), shape(KV_RANK), shape(HEAD_PAD),
                   jax.ShapeDtypeStruct((1, Q_RANK), F32), jax.ShapeDtypeStruct((1, KV_RANK), F32)],
        compiler_params=_params(("arbitrary",)),
    )(dq, dk, dv, z, z, g_q, g_kv, w_uq_pt, w_kv_pt, rope_c, rope_sa, rope_sb)


def _pool_bwd(z3, da3, pool_w, pool_w_t, pool_scale):
    b, lp, _ = z3.shape
    groups = len(POOL_WINDOWS)

    def body(u_ref, da_ref, pw_ref, pwt_ref, sc_ref, du_ref, dpw_ref, dsc_ref):
        @pl.when(pl.program_id(0) == 0)
        def _():
            dpw_ref[...] = jnp.zeros_like(dpw_ref)
            dsc_ref[...] = jnp.zeros_like(dsc_ref)

        row = lax.broadcasted_iota(jnp.int32, (lp, POOL_GROUP), 0)
        pos = row.astype(F32)
        for gi, w in enumerate(POOL_WINDOWS):
            cols = slice(gi * POOL_GROUP, (gi + 1) * POOL_GROUP)
            count = jnp.minimum(pos + 1.0, float(w))
            u = u_ref[0, :, cols].astype(F32)
            y = (_window_sum(u, w, row, True) / count - u).astype(BF16)
            yw = _dot(y, pw_ref[gi])
            da = da_ref[0, :, cols].astype(F32)
            dsc_ref[:, cols] += jnp.sum(da * yw, axis=0, keepdims=True)
            dyw = (da * sc_ref[:, cols]).astype(BF16)
            dpw_ref[gi] += _dot_tn(y, dyw)
            dy = _dot(dyw, pwt_ref[gi])
            du_ref[0, :, cols] = (_window_sum(dy / count, w, row, False) - dy).astype(BF16)

    return pl.pallas_call(
        body, name="pool_bwd", grid=(b,),
        in_specs=[pl.BlockSpec((1, lp, POOL_WIDTH), lambda i: (i, 0, 0)), pl.BlockSpec((1, lp, POOL_WIDTH), lambda i: (i, 0, 0)),
                  _vmem_spec(), _vmem_spec(), _vmem_spec()],
        out_specs=[pl.BlockSpec((1, lp, POOL_WIDTH), lambda i: (i, 0, 0)),
                   pl.BlockSpec((groups, POOL_GROUP, POOL_GROUP), lambda i: (0, 0, 0)),
                   pl.BlockSpec((1, POOL_WIDTH), lambda i: (0, 0))],
        out_shape=[jax.ShapeDtypeStruct((b, lp, POOL_WIDTH), BF16), jax.ShapeDtypeStruct((groups, POOL_GROUP, POOL_GROUP), F32),
                   jax.ShapeDtypeStruct((1, POOL_WIDTH), F32)],
        compiler_params=_params(("arbitrary",)),
    )(z3, da3, pool_w, pool_w_t, pool_scale)


def _in_proj_bwd(h, dh1, du, dcq, dckv, dkr, dga, dgb, g, w_in_pt):
    t, d = h.shape
    nz = w_in_pt.shape[0]
    tm = _token_tile(t, 512)
    widths = (POOL_WIDTH, Q_RANK, KV_RANK, HEAD_PAD, d, d)

    def body(h_ref, dh1_ref, du_ref, dcq_ref, dckv_ref, dkr_ref, dga_ref, dgb_ref, g_ref, wt_ref, dh_ref, hn_ref, dz_ref, dg_ref):
        @pl.when(pl.program_id(0) == 0)
        def _():
            dg_ref[...] = jnp.zeros_like(dg_ref)

        gain = g_ref[...]
        hn, xh, r = _rms_fwd(h_ref[...], gain)
        hn_ref[...] = hn.astype(BF16)
        dhn = jnp.zeros((tm, d), F32)
        start = 0
        for piece, width in zip((du_ref, dcq_ref, dckv_ref, dkr_ref, dga_ref, dgb_ref), widths):
            val = piece[...]
            dz_ref[:, start:start + width] = val
            dhn = dhn + _dot(val, wt_ref[start:start + width, :])
            start += width
        dx, dg_rows = _rms_bwd(dhn, xh, r, gain)
        dh_ref[...] = dh1_ref[...] + dx
        dg_ref[...] += jnp.sum(dg_rows, axis=0, keepdims=True)

    return pl.pallas_call(
        body, name="in_proj_bwd", grid=(t // tm,),
        in_specs=[_row_block(tm, d), _row_block(tm, d)] + [_row_block(tm, w) for w in widths] + [_vmem_spec(), _vmem_spec()],
        out_specs=[_row_block(tm, d), _row_block(tm, d), _row_block(tm, nz), pl.BlockSpec((1, d), lambda i: (0, 0))],
        out_shape=[jax.ShapeDtypeStruct((t, d), F32), jax.ShapeDtypeStruct((t, d), BF16), jax.ShapeDtypeStruct((t, nz), BF16),
                   jax.ShapeDtypeStruct((1, d), F32)],
        compiler_params=_params(("arbitrary",)),
    )(h, dh1, du, dcq, dckv, dkr, dga, dgb, g, w_in_pt)


def _pad_heads(w, heads, width):
    k = w.shape[0]
    w = w.reshape(k, heads, width)
    return jnp.pad(w, ((0, 0), (0, 0), (0, HEAD_PAD - width))).reshape(k, heads * HEAD_PAD)


def _unpad_heads(w, heads, width):
    k = w.shape[0]
    return w.reshape(k, heads, HEAD_PAD)[:, :, :width].reshape(k, heads * width)


def _layer_layouts(w, heads):
    d = w["w_in"].shape[0]
    o1, o2, o3, o4 = POOL_WIDTH, POOL_WIDTH + Q_RANK, POOL_WIDTH + Q_RANK + KV_RANK, POOL_WIDTH + Q_RANK + KV_RANK + QK_ROPE
    w_in = w["w_in"]
    rope_cols = jnp.pad(w_in[:, o3:o4], ((0, 0), (QK_NOPE, HEAD_PAD - QK_DIM)))
    w_in_p = jnp.concatenate([w_in[:, :o3], rope_cols, w_in[:, o4:]], axis=1)
    w_uq_p = _pad_heads(w["w_uq"], heads, QK_DIM)
    kv = w["w_ukv"].reshape(KV_RANK, heads, QK_NOPE + V_DIM)
    w_k = jnp.pad(kv[:, :, :QK_NOPE], ((0, 0), (0, 0), (0, HEAD_PAD - QK_NOPE))).reshape(KV_RANK, heads * HEAD_PAD)
    w_v = jnp.pad(kv[:, :, QK_NOPE:], ((0, 0), (0, 0), (0, HEAD_PAD - V_DIM))).reshape(KV_RANK, heads * HEAD_PAD)
    w_kv_p = jnp.concatenate([w_k, w_v], axis=1)
    w_pb_p = jnp.pad(w["w_pb"].reshape(heads, V_DIM, d), ((0, 0), (0, HEAD_PAD - V_DIM), (0, 0))).reshape(heads * HEAD_PAD, d)
    out = dict(w_in_p=w_in_p, w_uq_p=w_uq_p, w_kv_p=w_kv_p, w_pa=w["w_pa"], w_pb_p=w_pb_p, w_o=w["w_o"],
               w_gate=w["w_gate"], w_up=w["w_up"], w_down=w["w_down"])
    for name in tuple(out):
        out[name + "_t"] = out[name].T
    return out


def _layer_grad_layouts(g, heads, d):
    o3 = POOL_WIDTH + Q_RANK + KV_RANK
    gin = g["w_in_p"]
    w_in = jnp.concatenate([gin[:, :o3], gin[:, o3 + QK_NOPE:o3 + QK_DIM], gin[:, o3 + HEAD_PAD:]], axis=1)
    hw = heads * HEAD_PAD
    gk = g["w_kv_p"][:, :hw].reshape(KV_RANK, heads, HEAD_PAD)[:, :, :QK_NOPE]
    gv = g["w_kv_p"][:, hw:].reshape(KV_RANK, heads, HEAD_PAD)[:, :, :V_DIM]
    w_ukv = jnp.concatenate([gk, gv], axis=2).reshape(KV_RANK, heads * (QK_NOPE + V_DIM))
    w_pb = g["w_pb_p"].reshape(heads, HEAD_PAD, d)[:, :V_DIM].reshape(heads * V_DIM, d)
    return dict(w_in=w_in, w_uq=_unpad_heads(g["w_uq_p"], heads, QK_DIM), w_ukv=w_ukv, w_pa=g["w_pa"], w_pb=w_pb,
                w_o=g["w_o"], w_gate=g["w_gate"], w_up=g["w_up"], w_down=g["w_down"])


def _rope_tables(lp, b):
    inv = 1.0 / (ROPE_THETA ** (jnp.arange(0, QK_ROPE, 2, dtype=F32) / QK_ROPE))
    ang = jnp.arange(lp, dtype=F32)[:, None] * inv[None, :]
    cos, sin = jnp.cos(ang), jnp.sin(ang)
    half = QK_ROPE // 2
    ones = jnp.ones((lp, QK_NOPE), F32)
    zeros_lo = jnp.zeros((lp, QK_NOPE), F32)
    zeros_hi = jnp.zeros((lp, HEAD_PAD - QK_DIM), F32)
    zeros_half = jnp.zeros((lp, half), F32)
    c = jnp.concatenate([ones, cos, cos, zeros_hi], axis=1)
    sa = jnp.concatenate([zeros_lo, zeros_half, sin, zeros_hi], axis=1)
    sb = jnp.concatenate([zeros_lo, -sin, zeros_half, zeros_hi], axis=1)
    return tuple(jnp.tile(tab, (b, 1)) for tab in (c, sa, sb))


def _local_step(x, loss_target, meta_tokens, small, big):
    b, seq, d = x.shape
    depth = big["w_in"].shape[0]
    heads = big["w_uq"].shape[2] // QK_DIM
    lp = _round_up(N_META + seq, ATT_BLOCK)
    t = b * lp
    pad = lp - N_META - seq

    meta = jnp.broadcast_to(meta_tokens[None], (b, N_META, d))
    h = jnp.concatenate([meta, x, jnp.zeros((b, pad, d), F32)], axis=1).reshape(t, d)
    target = jnp.pad(loss_target, ((0, 0), (N_META, pad), (0, 0))).reshape(t, d)
    pos = jnp.arange(lp)
    valid = jnp.tile(((pos >= N_META) & (pos < N_META + seq)).astype(F32), b).reshape(t, 1)
    rope_c, rope_sa, rope_sb = _rope_tables(lp, b)

    layers = []
    for li in range(depth):
        lay = _layer_layouts({n: big[n][li] for n in BIG_WEIGHTS}, heads)
        lay["pool_w"] = small["pool_w"][li].astype(BF16)
        lay["pool_w_t"] = jnp.swapaxes(lay["pool_w"], 1, 2)
        lay["pool_scale"] = small["pool_scale"][li][None]
        for n in ("norm_mix_g", "q_norm_g", "kv_norm_g", "norm_ffn_g"):
            lay[n] = small[n][li][None]
        layers.append(lay)

    saved = []
    for lay in layers:
        z = _in_proj_fwd(h, lay["norm_mix_g"], lay["w_in_p"])
        a = _pool_fwd(z.reshape(b, lp, -1), lay["pool_w"], lay["pool_scale"]).reshape(t, POOL_WIDTH)
        q, k, v = _qkv_fwd(z, lay["q_norm_g"], lay["kv_norm_g"], lay["w_uq_p"], lay["w_kv_p"], rope_c, rope_sa, rope_sb)
        hw = q.shape[1]
        o3, lse = _attn_fwd(q.reshape(b, lp, hw), k.reshape(b, lp, hw), v.reshape(b, lp, hw))
        o = o3.reshape(t, hw)
        h1, pa, pb = _merge_fwd(h, z, a, o, lay["w_pa"], lay["w_pb_p"], lay["w_o"])
        h2, fa, fb = _ffn_fwd(h1, lay["norm_ffn_g"], lay["w_gate"], lay["w_up"], lay["w_down"])
        saved.append(dict(h=h, z=z, a=a, q=q, k=k, v=v, o=o, lse=lse, pa=pa, pb=pb, h1=h1, fa=fa, fb=fb))
        h = h2

    dh, loss, d_final = _loss_head(h, small["final_norm_g"][None], target, valid)

    g_small = {n: [] for n in SMALL_WEIGHTS if n != "final_norm_g"}
    g_big = {n: [] for n in BIG_WEIGHTS}
    for lay, sv in zip(reversed(layers), reversed(saved)):
        hw = sv["q"].shape[1]
        dh1, hn_f, act, dfa, dfb, dg_ffn = _ffn_bwd(sv["h1"], dh, sv["fa"], sv["fb"], lay["norm_ffn_g"],
                                                     lay["w_gate_t"], lay["w_up_t"], lay["w_down_t"])
        gl = dict(w_gate=_weight_grad(hn_f, dfa, "grad_w_gate"), w_up=_weight_grad(hn_f, dfb, "grad_w_up"),
                  w_down=_weight_grad(act, dh, "grad_w_down"))
        merged, dpa, dpb, dga, dgb, da, do = _merge_bwd(dh1, sv["z"], sv["pa"], sv["pb"], lay["w_o_t"], lay["w_pa_t"], lay["w_pb_p_t"])
        gl["w_o"] = _weight_grad(merged, dh1, "grad_w_o")
        gl["w_pa"] = _weight_grad(sv["a"], dpa, "grad_w_pa")
        gl["w_pb_p"] = _weight_grad(sv["o"], dpb, "grad_w_pb")
        shape3 = (b, lp, hw)
        dq3, dk3, dv3 = _attn_bwd(sv["q"].reshape(shape3), sv["k"].reshape(shape3), sv["v"].reshape(shape3),
                                  sv["o"].reshape(shape3), do.reshape(shape3), sv["lse"])
        dqraw, dkvraw, cqn, ckvn, dcq, dckv, dkr, dg_q, dg_kv = _qkv_bwd(
            dq3.reshape(t, hw), dk3.reshape(t, hw), dv3.reshape(t, hw), sv["z"], lay["q_norm_g"], lay["kv_norm_g"],
            lay["w_uq_p_t"], lay["w_kv_p_t"], rope_c, rope_sa, rope_sb)
        gl["w_uq_p"] = _weight_grad(cqn, dqraw, "grad_w_uq")
        gl["w_kv_p"] = _weight_grad(ckvn, dkvraw, "grad_w_ukv")
        du3, dpool_w, dpool_scale = _pool_bwd(sv["z"].reshape(b, lp, -1), da.reshape(b, lp, POOL_WIDTH),
                                              lay["pool_w"], lay["pool_w_t"], lay["pool_scale"])
        dh, hn_m, dz, dg_mix = _in_proj_bwd(sv["h"], dh1, du3.reshape(t, POOL_WIDTH), dcq, dckv, dkr, dga, dgb,
                                            lay["norm_mix_g"], lay["w_in_p_t"])
        gl["w_in_p"] = _weight_grad(hn_m, dz, "grad_w_in")
        for n, val in _layer_grad_layouts(gl, heads, d).items():
            g_big[n].insert(0, val)
        for n, val in (("norm_mix_g", dg_mix[0]), ("pool_w", dpool_w), ("pool_scale", dpool_scale[0]), ("q_norm_g", dg_q[0]),
                       ("kv_norm_g", dg_kv[0]), ("norm_ffn_g", dg_ffn[0])):
            g_small[n].insert(0, val)

    dh3 = dh.reshape(b, lp, d)
    grad_x = dh3[:, N_META:N_META + seq]
    d_meta_rows = dh3[:, :N_META]
    g_small = {n: jnp.stack(v) for n, v in g_small.items()}
    g_small["final_norm_g"] = d_final[0]
    g_big = {n: jnp.stack(v) for n, v in g_big.items()}
    return loss, grad_x, d_meta_rows, g_small, g_big


def _mesh_place():
    x, y, c = lax.axis_index("x"), lax.axis_index("y"), lax.axis_index("c")
    others = [(1 - x, y), (x, 1 - y), (1 - x, 1 - y)]
    return x, y, c, 2 * x + y, others


def _remote(src, dst, send_sems, recv_sems, k, device):
    return pltpu.make_async_remote_copy(src_ref=src, dst_ref=dst, send_sem=send_sems.at[k], recv_sem=recv_sems.at[k],
                                        device_id=device, device_id_type=MESH)


def _all_gather_shards(packed, meta_shard):
    _, rh, cols = packed.shape

    def body(p_ref, m_ref, g_ref, gm_ref, send_sems, recv_sems):
        x, y, c, chip, others = _mesh_place()
        sibling = (x, y, 1 - c)
        sends = []
        for r, (ox, oy) in enumerate(others):
            sends.append(_remote(p_ref.at[c], g_ref.at[chip, c], send_sems, recv_sems, r, (ox, oy, c)))
            sends.append(_remote(m_ref, gm_ref.at[chip], send_sems, recv_sems, 6 + r, (ox, oy, c)))
        for cp in sends:
            cp.start()
        for r, (ox, oy) in enumerate(others):
            src_chip = 2 * ox + oy
            _remote(p_ref.at[c], g_ref.at[src_chip, c], send_sems, recv_sems, r, (ox, oy, c)).wait_recv()
            passed = _remote(g_ref.at[src_chip, c], g_ref.at[src_chip, c], send_sems, recv_sems, 3 + r, sibling)
            passed.start()
            sends.append(passed)
        for r, (ox, oy) in enumerate(others):
            src_chip = 2 * ox + oy
            _remote(p_ref.at[c], g_ref.at[src_chip, 1 - c], send_sems, recv_sems, 3 + r, sibling).wait_recv()
            _remote(m_ref, gm_ref.at[src_chip], send_sems, recv_sems, 6 + r, (ox, oy, c)).wait_recv()
        for cp in sends:
            cp.wait_send()

    gathered, meta_all = pl.pallas_call(
        body, name="all_gather_shards",
        in_specs=[_any_spec(), _any_spec()], out_specs=[_any_spec(), _any_spec()],
        out_shape=[jax.ShapeDtypeStruct((N_CHIPS, 2, rh, cols), packed.dtype),
                   jax.ShapeDtypeStruct((N_CHIPS,) + meta_shard.shape, meta_shard.dtype)],
        scratch_shapes=[pltpu.SemaphoreType.DMA((9,)), pltpu.SemaphoreType.DMA((9,))],
    )(packed, meta_shard)
    chip = 2 * lax.axis_index("x") + lax.axis_index("y")
    return (lax.dynamic_update_index_in_dim(gathered, packed, chip, 0),
            lax.dynamic_update_index_in_dim(meta_all, meta_shard, chip, 0))


def _pair_exchange(give):
    def body(give_ref, got_ref, send_sems, recv_sems):
        x, y, c, _, _ = _mesh_place()
        cp = _remote(give_ref, got_ref, send_sems, recv_sems, 0, (x, y, 1 - c))
        cp.start()
        cp.wait()

    return pl.pallas_call(
        body, name="pair_exchange", in_specs=[_any_spec()], out_specs=_any_spec(),
        out_shape=jax.ShapeDtypeStruct(give.shape, give.dtype),
        scratch_shapes=[pltpu.SemaphoreType.DMA((1,)), pltpu.SemaphoreType.DMA((1,))],
    )(give)


def _chip_exchange(parts):
    def body(p_ref, got_ref, send_sems, recv_sems):
        _, _, c, chip, others = _mesh_place()
        sends = [_remote(p_ref.at[2 * ox + oy], got_ref.at[chip], send_sems, recv_sems, r, (ox, oy, c))
                 for r, (ox, oy) in enumerate(others)]
        for cp in sends:
            cp.start()
        for r, (ox, oy) in enumerate(others):
            _remote(p_ref.at[chip], got_ref.at[2 * ox + oy], send_sems, recv_sems, r, (ox, oy, c)).wait_recv()
        for cp in sends:
            cp.wait_send()

    got = pl.pallas_call(
        body, name="chip_exchange", in_specs=[_any_spec()], out_specs=_any_spec(),
        out_shape=jax.ShapeDtypeStruct(parts.shape, parts.dtype),
        scratch_shapes=[pltpu.SemaphoreType.DMA((3,)), pltpu.SemaphoreType.DMA((3,))],
    )(parts)
    chip = 2 * lax.axis_index("x") + lax.axis_index("y")
    own = lax.dynamic_index_in_dim(parts, chip, 0, keepdims=False)
    return lax.dynamic_update_index_in_dim(got, own, chip, 0)


def _pair_gather(half):
    def body(h_ref, out_ref, send_sems, recv_sems):
        x, y, c, _, _ = _mesh_place()
        cp = _remote(h_ref, out_ref.at[c], send_sems, recv_sems, 0, (x, y, 1 - c))
        cp.start()
        _remote(h_ref, out_ref.at[1 - c], send_sems, recv_sems, 0, (x, y, 1 - c)).wait_recv()
        cp.wait_send()

    both = pl.pallas_call(
        body, name="pair_gather", in_specs=[_any_spec()], out_specs=_any_spec(),
        out_shape=jax.ShapeDtypeStruct((2,) + half.shape, half.dtype),
        scratch_shapes=[pltpu.SemaphoreType.DMA((1,)), pltpu.SemaphoreType.DMA((1,))],
    )(half)
    return lax.dynamic_update_index_in_dim(both, half, lax.axis_index("c"), 0)


def _row_tile(rows, limit=PACK_TILE):
    if rows <= limit:
        return rows
    for tr in range(limit, 7, -8):
        if rows % tr == 0:
            return tr
    return rows


def _pair_add(keep, got):
    n, rh, cols = keep.shape
    tr = _row_tile(rh)

    def body(k_ref, g_ref, o_ref):
        o_ref[...] = (k_ref[...].astype(F32) + g_ref[...].astype(F32)).astype(BF16)

    spec = pl.BlockSpec((1, tr, cols), lambda j, i: (j, i, 0))
    return pl.pallas_call(
        body, name="pair_add", grid=(n, rh // tr), in_specs=[spec, spec], out_specs=spec,
        out_shape=jax.ShapeDtypeStruct(keep.shape, BF16),
        compiler_params=_params(("parallel", "parallel")),
    )(keep, got)


def _chip_sum(parts):
    n, rh, cols = parts.shape
    tr = _row_tile(rh)

    def body(p_ref, o_ref):
        total = p_ref[0].astype(F32)
        for k in range(1, n):
            total = total + p_ref[k].astype(F32)
        o_ref[...] = total

    return pl.pallas_call(
        body, name="chip_sum", grid=(rh // tr,),
        in_specs=[pl.BlockSpec((n, tr, cols), lambda i: (0, i, 0))], out_specs=pl.BlockSpec((tr, cols), lambda i: (i, 0)),
        out_shape=jax.ShapeDtypeStruct((rh, cols), F32),
        compiler_params=_params(("parallel",)),
    )(parts)


def _reduce_scatter(grads, c):
    keep = lax.dynamic_index_in_dim(grads, c, axis=1, keepdims=False)
    give = lax.dynamic_index_in_dim(grads, 1 - c, axis=1, keepdims=False)
    chip_partial = _pair_add(keep, _pair_exchange(give))
    return _pair_gather(_chip_sum(_chip_exchange(chip_partial)))


def _all_reduce_small(meta_rows, small):
    b, rm, cols = meta_rows.shape
    rows = rm + small.shape[0]

    def body(meta_ref, small_ref, out_ref, mine, pair_buf, chip_buf, send_sems, recv_sems):
        x, y, c, chip, others = _mesh_place()
        acc = meta_ref[0]
        for i in range(1, b):
            acc = acc + meta_ref[i]
        mine[0:rm, :] = acc
        mine[rm:rows, :] = small_ref[...]
        pair = _remote(mine, pair_buf, send_sems, recv_sems, 0, (x, y, 1 - c))
        pair.start()
        pair.wait()
        chip_buf[chip] = mine[...] + pair_buf[...]
        sends = [_remote(chip_buf.at[chip], chip_buf.at[chip], send_sems, recv_sems, 1 + r, (ox, oy, c))
                 for r, (ox, oy) in enumerate(others)]
        for cp in sends:
            cp.start()
        for r, (ox, oy) in enumerate(others):
            _remote(chip_buf.at[chip], chip_buf.at[2 * ox + oy], send_sems, recv_sems, 1 + r, (ox, oy, c)).wait_recv()
        for cp in sends:
            cp.wait_send()
        out_ref[...] = ((chip_buf[0] + chip_buf[1]) + chip_buf[2]) + chip_buf[3]

    return pl.pallas_call(
        body, name="all_reduce_small",
        in_specs=[_vmem_spec(), _vmem_spec()], out_specs=_vmem_spec(),
        out_shape=jax.ShapeDtypeStruct((rows, cols), F32),
        scratch_shapes=[pltpu.VMEM((rows, cols), F32), pltpu.VMEM((rows, cols), F32), pltpu.VMEM((N_CHIPS, rows, cols), F32),
                        pltpu.SemaphoreType.DMA((4,)), pltpu.SemaphoreType.DMA((4,))],
        compiler_params=pltpu.CompilerParams(vmem_limit_bytes=VMEM_LIMIT),
    )(meta_rows, small)


def _adamw(w, g, m, v):
    shape = w.shape
    cols = shape[-1]
    rows = w.size // cols
    tr = _row_tile(rows)

    def body(w_ref, g_ref, m_ref, v_ref, d_ref, m2_ref, v2_ref):
        grad = g_ref[...]
        m2 = ADAM_B1 * m_ref[...] + (1.0 - ADAM_B1) * grad
        v2 = ADAM_B2 * v_ref[...] + (1.0 - ADAM_B2) * jnp.square(grad)
        m_hat = m2 / (1.0 - ADAM_B1 ** ADAM_STEP)
        v_hat = v2 / (1.0 - ADAM_B2 ** ADAM_STEP)
        d_ref[...] = -ADAM_LR * (m_hat / (jnp.sqrt(v_hat) + ADAM_EPS) + ADAM_WD * w_ref[...])
        m2_ref[...] = m2
        v2_ref[...] = v2

    spec = pl.BlockSpec((tr, cols), lambda i: (i, 0))
    out = jax.ShapeDtypeStruct((rows, cols), F32)
    res = pl.pallas_call(
        body, name="adamw", grid=(rows // tr,), in_specs=[spec] * 4, out_specs=[spec] * 3, out_shape=[out] * 3,
        compiler_params=_params(("parallel",)),
    )(*(a.reshape(rows, cols) for a in (w, g, m, v)))
    return tuple(r.reshape(shape) for r in res)


def _pack_rows(arrays):
    flat = [a.reshape(-1, PACK_COLS) for a in arrays]
    counts = [f.shape[0] for f in flat]
    total = sum(counts)
    half = -(-total // 2)
    tiles = -(-half // PACK_TILE)
    padded = 2 * tiles * _round_up(-(-half // tiles), 16)
    if padded > total:
        flat.append(jnp.zeros((padded - total, PACK_COLS), flat[0].dtype))
    return jnp.concatenate(flat, axis=0), counts


def _unpack_rows(buffer, counts, shapes):
    out, start = [], 0
    for n, shape in zip(counts, shapes):
        out.append(buffer[..., start:start + n, :].reshape(buffer.shape[:-2] + tuple(shape)))
        start += n
    return out


def kernel(x, meta_tokens, norm_mix_g, w_in, pool_w, pool_scale, q_norm_g, kv_norm_g, w_uq, w_ukv, w_pa, w_pb, w_o, norm_ffn_g, w_gate, w_up, w_down, final_norm_g, loss_target, m_meta_tokens, m_norm_mix_g, m_w_in, m_pool_w, m_pool_scale, m_q_norm_g, m_kv_norm_g, m_w_uq, m_w_ukv, m_w_pa, m_w_pb, m_w_o, m_norm_ffn_g, m_w_gate, m_w_up, m_w_down, m_final_norm_g, v_meta_tokens, v_norm_mix_g, v_w_in, v_pool_w, v_pool_scale, v_q_norm_g, v_kv_norm_g, v_w_uq, v_w_ukv, v_w_pa, v_w_pb, v_w_o, v_norm_ffn_g, v_w_gate, v_w_up, v_w_down, v_final_norm_g):
    weights = dict(meta_tokens=meta_tokens, norm_mix_g=norm_mix_g, w_in=w_in, pool_w=pool_w, pool_scale=pool_scale,
                   q_norm_g=q_norm_g, kv_norm_g=kv_norm_g, w_uq=w_uq, w_ukv=w_ukv, w_pa=w_pa, w_pb=w_pb, w_o=w_o,
                   norm_ffn_g=norm_ffn_g, w_gate=w_gate, w_up=w_up, w_down=w_down, final_norm_g=final_norm_g)
    first = dict(meta_tokens=m_meta_tokens, norm_mix_g=m_norm_mix_g, w_in=m_w_in, pool_w=m_pool_w, pool_scale=m_pool_scale,
                 q_norm_g=m_q_norm_g, kv_norm_g=m_kv_norm_g, w_uq=m_w_uq, w_ukv=m_w_ukv, w_pa=m_w_pa, w_pb=m_w_pb, w_o=m_w_o,
                 norm_ffn_g=m_norm_ffn_g, w_gate=m_w_gate, w_up=m_w_up, w_down=m_w_down, final_norm_g=m_final_norm_g)
    second = dict(meta_tokens=v_meta_tokens, norm_mix_g=v_norm_mix_g, w_in=v_w_in, pool_w=v_pool_w, pool_scale=v_pool_scale,
                  q_norm_g=v_q_norm_g, kv_norm_g=v_kv_norm_g, w_uq=v_w_uq, w_ukv=v_w_ukv, w_pa=v_w_pa, w_pb=v_w_pb, w_o=v_w_o,
                  norm_ffn_g=v_norm_ffn_g, w_gate=v_w_gate, w_up=v_w_up, w_down=v_w_down, final_norm_g=v_final_norm_g)
    core = lax.axis_index("c")
    chip = 2 * lax.axis_index("x") + lax.axis_index("y")
    d = x.shape[-1]
    meta_cols = meta_tokens.shape[1]

    shard_shapes = [weights[n].shape for n in BIG_WEIGHTS]
    packed, counts = _pack_rows([weights[n].astype(BF16) for n in BIG_WEIGHTS])
    total_rows = packed.shape[0]
    gathered, meta_all = _all_gather_shards(packed.reshape(2, total_rows // 2, PACK_COLS), meta_tokens)
    per_chip = _unpack_rows(gathered.reshape(N_CHIPS, total_rows, PACK_COLS), counts, shard_shapes)
    big = {n: jnp.concatenate([per_chip[i][j] for j in range(N_CHIPS)], axis=SHARD_AXIS[n]) for i, n in enumerate(BIG_WEIGHTS)}
    meta_full = jnp.concatenate([meta_all[j] for j in range(N_CHIPS)], axis=1)
    small = {n: weights[n] for n in SMALL_WEIGHTS}

    loss, grad_x, d_meta_rows, g_small, g_big = _local_step(x, loss_target, meta_full, small, big)

    split = [jnp.stack(jnp.split(g_big[n], N_CHIPS, axis=SHARD_AXIS[n])).reshape(N_CHIPS, -1, PACK_COLS) for n in BIG_WEIGHTS]
    pad_rows = total_rows - sum(counts)
    if pad_rows:
        split.append(jnp.zeros((N_CHIPS, pad_rows, PACK_COLS), BF16))
    partial = jnp.concatenate(split, axis=1).reshape(N_CHIPS, 2, total_rows // 2, PACK_COLS)
    reduced = _reduce_scatter(partial, core).reshape(total_rows, PACK_COLS)
    grads = dict(zip(BIG_WEIGHTS, _unpack_rows(reduced, counts, shard_shapes)))

    small_shapes = [weights[n].shape for n in SMALL_WEIGHTS]
    small_flat = jnp.concatenate([g_small[n].reshape(-1) for n in SMALL_WEIGHTS])
    small_len = small_flat.shape[0]
    small_rows = _round_up(-(-small_len // PACK_COLS), 8)
    small_pack = jnp.pad(small_flat, (0, small_rows * PACK_COLS - small_len)).reshape(small_rows, PACK_COLS)
    meta_rows = N_META * d // PACK_COLS
    summed = _all_reduce_small(d_meta_rows.reshape(-1, meta_rows, PACK_COLS), small_pack)
    grad_meta_full = summed[:meta_rows].reshape(N_META, d)
    grads["meta_tokens"] = lax.dynamic_slice_in_dim(grad_meta_full, chip * meta_cols, meta_cols, axis=1)
    small_sum = summed[meta_rows:].reshape(-1)
    start = 0
    for n, shape in zip(SMALL_WEIGHTS, small_shapes):
        size = 1
        for s in shape:
            size *= s
        grads[n] = small_sum[start:start + size].reshape(shape)
        start += size

    deltas, new_m, new_v = {}, {}, {}
    for n in WEIGHT_ORDER:
        deltas[n], new_m[n], new_v[n] = _adamw(weights[n], grads[n], first[n], second[n])

    total_loss = lax.psum(loss[0, 0], ("x", "y", "c"))
    return (total_loss, grad_x, *[grads[n] for n in WEIGHT_ORDER], *[deltas[n] for n in WEIGHT_ORDER],
            *[new_m[n] for n in WEIGHT_ORDER], *[new_v[n] for n in WEIGHT_ORDER])
```

```python
import functools

import jax
import jax.numpy as jnp
from jax import lax
from jax.experimental import pallas as pl
from jax.experimental.pallas import tpu as pltpu

F32 = jnp.float32
BF16 = jnp.bfloat16

N_META = 16
POOL_WINDOWS = (2, 4, 8, 16)
POOL_GROUP = 128
POOL_WIDTH = POOL_GROUP * len(POOL_WINDOWS)
QK_NOPE = 64
QK_ROPE = 32
V_DIM = 64
QK_DIM = QK_NOPE + QK_ROPE
Q_RANK = 256
KV_RANK = 128
HEAD_PAD = 128
SM_SCALE = QK_DIM ** -0.5
ROPE_THETA = 10000.0
NORM_EPS = 1e-6
MASK_VALUE = -1e30
Z_FIXED = POOL_WIDTH + Q_RANK + KV_RANK + HEAD_PAD

ADAM_LR = 0.001
ADAM_B1 = 0.9
ADAM_B2 = 0.999
ADAM_EPS = 1e-08
ADAM_WD = 0.01
ADAM_STEP = 10

N_CHIPS = 4
ATT_BLOCK = 256
SEQ_PAD = 128
ATT_FWD_HEADS = 4
ATT_BWD_HEADS = 4
PACK_COLS = 1024
PACK_TILE = 512
VMEM_LIMIT = 60 * 1024 * 1024

MESH = pl.DeviceIdType.MESH

BIG_WEIGHTS = ("w_in", "w_uq", "w_ukv", "w_pa", "w_pb", "w_o", "w_gate", "w_up", "w_down")
SHARD_AXIS = {"w_in": 2, "w_uq": 2, "w_ukv": 2, "w_pa": 2, "w_pb": 1, "w_o": 1, "w_gate": 2, "w_up": 2, "w_down": 1}
SMALL_WEIGHTS = ("norm_mix_g", "pool_w", "pool_scale", "q_norm_g", "kv_norm_g", "norm_ffn_g", "final_norm_g")
WEIGHT_ORDER = ("meta_tokens", "norm_mix_g", "w_in", "pool_w", "pool_scale", "q_norm_g", "kv_norm_g", "w_uq", "w_ukv",
                "w_pa", "w_pb", "w_o", "norm_ffn_g", "w_gate", "w_up", "w_down", "final_norm_g")


def _round_up(n, m):
    return -(-n // m) * m


def _vmem_spec():
    return pl.BlockSpec(memory_space=pltpu.VMEM)


def _any_spec():
    return pl.BlockSpec(memory_space=pl.ANY)


def _row_block(tm, width, col_block=0):
    return pl.BlockSpec((tm, width), lambda i, cb=col_block: (i, cb))


def _params(sem, vmem=VMEM_LIMIT):
    return pltpu.CompilerParams(dimension_semantics=sem, vmem_limit_bytes=vmem)


def _token_tile(t, want):
    best = SEQ_PAD
    for tm in range(32, min(t, 2 * want) + 1, 32):
        if t % tm == 0 and abs(tm - want) < abs(best - want):
            best = tm
    return best


def _dot(a, b):
    return jnp.dot(a, b, preferred_element_type=F32)


def _dot_nt(a, b):
    return lax.dot_general(a, b, (((1,), (1,)), ((), ())), preferred_element_type=F32)


def _dot_tn(a, b):
    return lax.dot_general(a, b, (((0,), (0,)), ((), ())), preferred_element_type=F32)


def _rms_fwd(x, g):
    r = lax.rsqrt(jnp.mean(x * x, axis=-1, keepdims=True) + NORM_EPS)
    xh = x * r
    return xh * g, xh, r


def _rms_bwd(dy, xh, r, g):
    gdy = dy * g
    dx = r * (gdy - xh * jnp.mean(xh * gdy, axis=-1, keepdims=True))
    return dx, dy * xh


def _rope_fwd(x, c, sa, sb):
    return x * c + pltpu.roll(x, 16, 1) * sa + pltpu.roll(x, HEAD_PAD - 16, 1) * sb


def _rope_bwd(d, c, sa, sb):
    return d * c + pltpu.roll(d * sa, HEAD_PAD - 16, 1) + pltpu.roll(d * sb, 16, 1)


def _in_proj_fwd(h, g, w_in_p):
    t, d = h.shape
    nz = w_in_p.shape[1]
    tm = _token_tile(t, 512)

    def body(h_ref, g_ref, w_ref, z_ref):
        hn, _, _ = _rms_fwd(h_ref[...], g_ref[...])
        z_ref[...] = _dot(hn.astype(BF16), w_ref[...]).astype(BF16)

    return pl.pallas_call(
        body, name="in_proj_fwd", grid=(t // tm,),
        in_specs=[_row_block(tm, d), _vmem_spec(), _vmem_spec()],
        out_specs=_row_block(tm, nz),
        out_shape=jax.ShapeDtypeStruct((t, nz), BF16),
        compiler_params=_params(("parallel",)),
    )(h, g, w_in_p)


def _window_sum(x, w, row, forward):
    n = x.shape[0]
    s = x
    k = 1
    while k < w:
        if forward:
            s = s + jnp.where(row >= k, pltpu.roll(s, k, 0), 0.0)
        else:
            s = s + jnp.where(row < n - k, pltpu.roll(s, n - k, 0), 0.0)
        k *= 2
    return s


def _pool_fwd(z3, pool_w, pool_scale):
    b, lp, _ = z3.shape

    def body(u_ref, pw_ref, sc_ref, a_ref):
        row = lax.broadcasted_iota(jnp.int32, (lp, POOL_GROUP), 0)
        pos = row.astype(F32)
        for gi, w in enumerate(POOL_WINDOWS):
            cols = slice(gi * POOL_GROUP, (gi + 1) * POOL_GROUP)
            u = u_ref[0, :, cols].astype(F32)
            y = _window_sum(u, w, row, True) / jnp.minimum(pos + 1.0, float(w)) - u
            yw = _dot(y.astype(BF16), pw_ref[gi])
            a_ref[0, :, cols] = (yw * sc_ref[:, cols]).astype(BF16)

    return pl.pallas_call(
        body, name="pool_fwd", grid=(b,),
        in_specs=[pl.BlockSpec((1, lp, POOL_WIDTH), lambda i: (i, 0, 0)), _vmem_spec(), _vmem_spec()],
        out_specs=pl.BlockSpec((1, lp, POOL_WIDTH), lambda i: (i, 0, 0)),
        out_shape=jax.ShapeDtypeStruct((b, lp, POOL_WIDTH), BF16),
        compiler_params=_params(("parallel",)),
    )(z3, pool_w, pool_scale)


def _qkv_fwd(z, g_q, g_kv, w_uq_p, w_kv_p, rope_c, rope_sa, rope_sb):
    t = z.shape[0]
    hw = w_uq_p.shape[1]
    heads = hw // HEAD_PAD
    tm = _token_tile(t, 512)

    def body(cq_ref, ckv_ref, kr_ref, gq_ref, gkv_ref, wq_ref, wkv_ref, c_ref, sa_ref, sb_ref, q_ref, k_ref, v_ref):
        c, sa, sb = c_ref[...], sa_ref[...], sb_ref[...]
        cqn, _, _ = _rms_fwd(cq_ref[...].astype(F32), gq_ref[...])
        qraw = _dot(cqn.astype(BF16), wq_ref[...])
        ckvn, _, _ = _rms_fwd(ckv_ref[...].astype(F32), gkv_ref[...])
        kvraw = _dot(ckvn.astype(BF16), wkv_ref[...])
        kr = kr_ref[...].astype(F32)
        for hd in range(heads):
            cols = slice(hd * HEAD_PAD, (hd + 1) * HEAD_PAD)
            q_ref[:, cols] = (_rope_fwd(qraw[:, cols], c, sa, sb) * SM_SCALE).astype(BF16)
            k_ref[:, cols] = _rope_fwd(kvraw[:, cols] + kr, c, sa, sb).astype(BF16)
        v_ref[...] = kvraw[:, hw:].astype(BF16)

    out = jax.ShapeDtypeStruct((t, hw), BF16)
    return pl.pallas_call(
        body, name="qkv_fwd", grid=(t // tm,),
        in_specs=[_row_block(tm, Q_RANK, POOL_WIDTH // Q_RANK),
                  _row_block(tm, KV_RANK, (POOL_WIDTH + Q_RANK) // KV_RANK),
                  _row_block(tm, HEAD_PAD, (POOL_WIDTH + Q_RANK + KV_RANK) // HEAD_PAD),
                  _vmem_spec(), _vmem_spec(), _vmem_spec(), _vmem_spec(),
                  _row_block(tm, HEAD_PAD), _row_block(tm, HEAD_PAD), _row_block(tm, HEAD_PAD)],
        out_specs=[_row_block(tm, hw)] * 3,
        out_shape=[out, out, out],
        compiler_params=_params(("parallel",)),
    )(z, z, z, g_q, g_kv, w_uq_p, w_kv_p, rope_c, rope_sa, rope_sb)


def _heads_per_step(heads, want):
    while heads % want:
        want //= 2
    return want


def _causal_mask(rows):
    row = lax.broadcasted_iota(jnp.int32, (rows, rows), 0)
    col = lax.broadcasted_iota(jnp.int32, (rows, rows), 1)
    return col <= row


def _attn_blocks(real_len):
    tail_start = (-(-real_len // ATT_BLOCK) - 1) * ATT_BLOCK
    return tail_start // ATT_BLOCK, tail_start, _round_up(real_len - tail_start, 32)


def _attn_fwd(q3, k3, v3, real_len):
    b, lp, hw = q3.shape
    heads = hw // HEAD_PAD
    tb = ATT_BLOCK
    nfull, tail_start, tail = _attn_blocks(real_len)
    done = tail_start + tail
    hpg = _heads_per_step(heads, ATT_FWD_HEADS)
    width = hpg * HEAD_PAD

    def body(q_ref, k_ref, v_ref, o_ref, lse_ref):
        group = pl.program_id(1)

        @pl.when(group == 0)
        def _():
            lse_ref[...] = jnp.zeros_like(lse_ref)

        def q_rows(r0, rows, whole_kv_blocks):
            def kv_step(c0, keys, states, mask):
                out = []
                for hd, (m, l, acc) in enumerate(states):
                    cols = slice(hd * HEAD_PAD, (hd + 1) * HEAD_PAD)
                    s = _dot_nt(q_ref[0, pl.ds(r0, rows), cols], k_ref[0, pl.ds(c0, keys), cols])
                    if mask is not None:
                        s = jnp.where(mask, s, MASK_VALUE)
                    m_new = jnp.maximum(m, jnp.max(s, axis=-1, keepdims=True))
                    alpha = jnp.exp(m - m_new)
                    p = jnp.exp(s - m_new)
                    l = alpha * l + jnp.sum(p, axis=-1, keepdims=True)
                    acc = alpha * acc + _dot(p.astype(BF16), v_ref[0, pl.ds(c0, keys), cols])
                    out.append((m_new, l, acc))
                return tuple(out)

            init = tuple((jnp.full((rows, 1), MASK_VALUE, F32), jnp.zeros((rows, 1), F32), jnp.zeros((rows, HEAD_PAD), F32))
                         for _ in range(hpg))
            states = lax.fori_loop(0, whole_kv_blocks, lambda j, st: kv_step(pl.multiple_of(j * tb, tb), tb, st, None), init)
            states = kv_step(r0, rows, states, _causal_mask(rows))
            lane = lax.broadcasted_iota(jnp.int32, (rows, HEAD_PAD), 1)
            lse_rows = lse_ref[0, pl.ds(r0, rows), :]
            for hd, (m, l, acc) in enumerate(states):
                o_ref[0, pl.ds(r0, rows), hd * HEAD_PAD:(hd + 1) * HEAD_PAD] = (acc / l).astype(BF16)
                lse_rows = jnp.where(lane == group * hpg + hd, m + jnp.log(l), lse_rows)
            lse_ref[0, pl.ds(r0, rows), :] = lse_rows

        def whole_block(i, carry):
            q_rows(pl.multiple_of(i * tb, tb), tb, i)
            return carry

        lax.fori_loop(0, nfull, whole_block, 0)
        q_rows(tail_start, tail, nfull)
        if done < lp:
            o_ref[0, done:lp, :] = jnp.zeros((lp - done, width), BF16)

    head_spec = pl.BlockSpec((1, lp, width), lambda bi, hi: (bi, 0, hi))
    return pl.pallas_call(
        body, name="attn_fwd", grid=(b, heads // hpg),
        in_specs=[head_spec, head_spec, head_spec],
        out_specs=[head_spec, pl.BlockSpec((1, lp, HEAD_PAD), lambda bi, hi: (bi, 0, 0))],
        out_shape=[jax.ShapeDtypeStruct((b, lp, hw), BF16), jax.ShapeDtypeStruct((b, lp, HEAD_PAD), F32)],
        compiler_params=_params(("parallel", "arbitrary")),
    )(q3, k3, v3)


def _merge_fwd(h, z, a, o, w_pa, w_pb_p, w_o):
    t, d = h.shape
    hw = o.shape[1]
    tm = _token_tile(t, 512)
    gate_block = Z_FIXED // d

    def body(h_ref, ga_ref, gb_ref, a_ref, o_ref, wpa_ref, wpb_ref, wo_ref, h1_ref, pa_ref, pb_ref):
        pa = _dot(a_ref[...], wpa_ref[...])
        pb = _dot(o_ref[...], wpb_ref[...])
        merged = jax.nn.sigmoid(ga_ref[...].astype(F32)) * pa + jax.nn.sigmoid(gb_ref[...].astype(F32)) * pb
        h1_ref[...] = h_ref[...] + _dot(merged.astype(BF16), wo_ref[...])
        pa_ref[...] = pa.astype(BF16)
        pb_ref[...] = pb.astype(BF16)

    return pl.pallas_call(
        body, name="merge_fwd", grid=(t // tm,),
        in_specs=[_row_block(tm, d), _row_block(tm, d, gate_block), _row_block(tm, d, gate_block + 1),
                  _row_block(tm, POOL_WIDTH), _row_block(tm, hw), _vmem_spec(), _vmem_spec(), _vmem_spec()],
        out_specs=[_row_block(tm, d)] * 3,
        out_shape=[jax.ShapeDtypeStruct((t, d), F32), jax.ShapeDtypeStruct((t, d), BF16), jax.ShapeDtypeStruct((t, d), BF16)],
        compiler_params=_params(("parallel",)),
    )(h, z, z, a, o, w_pa, w_pb_p, w_o)


def _ffn_fwd(h, g, w_gate, w_up, w_down):
    t, d = h.shape
    f = w_gate.shape[1]
    tm = _token_tile(t, 256)

    def body(h_ref, g_ref, wg_ref, wu_ref, wd_ref, h2_ref, a_ref, b_ref):
        x = h_ref[...]
        hn, _, _ = _rms_fwd(x, g_ref[...])
        hn = hn.astype(BF16)
        ga = _dot(hn, wg_ref[...])
        up = _dot(hn, wu_ref[...])
        act = ga * jax.nn.sigmoid(ga) * up
        h2_ref[...] = x + _dot(act.astype(BF16), wd_ref[...])
        a_ref[...] = ga.astype(BF16)
        b_ref[...] = up.astype(BF16)

    return pl.pallas_call(
        body, name="ffn_fwd", grid=(t // tm,),
        in_specs=[_row_block(tm, d), _vmem_spec(), _vmem_spec(), _vmem_spec(), _vmem_spec()],
        out_specs=[_row_block(tm, d), _row_block(tm, f), _row_block(tm, f)],
        out_shape=[jax.ShapeDtypeStruct((t, d), F32), jax.ShapeDtypeStruct((t, f), BF16), jax.ShapeDtypeStruct((t, f), BF16)],
        compiler_params=_params(("parallel",)),
    )(h, g, w_gate, w_up, w_down)


def _loss_head(h, g, target, valid):
    t, d = h.shape
    tm = _token_tile(t, 512)

    def body(h_ref, g_ref, t_ref, valid_ref, dh_ref, loss_ref, dg_ref):
        @pl.when(pl.program_id(0) == 0)
        def _():
            loss_ref[...] = jnp.zeros_like(loss_ref)
            dg_ref[...] = jnp.zeros_like(dg_ref)

        gain = g_ref[...]
        y, xh, r = _rms_fwd(h_ref[...], gain)
        err = (y - t_ref[...]) * valid_ref[...]
        per_row = jnp.sum(err * err, axis=-1, keepdims=True) / d
        loss_ref[...] += 0.5 * jnp.sum(per_row, axis=0, keepdims=True)
        dx, dg_rows = _rms_bwd(err / d, xh, r, gain)
        dh_ref[...] = dx
        dg_ref[...] += jnp.sum(dg_rows, axis=0, keepdims=True)

    return pl.pallas_call(
        body, name="loss_head", grid=(t // tm,),
        in_specs=[_row_block(tm, d), _vmem_spec(), _row_block(tm, d), _row_block(tm, 1)],
        out_specs=[_row_block(tm, d), pl.BlockSpec((1, 1), lambda i: (0, 0)), pl.BlockSpec((1, d), lambda i: (0, 0))],
        out_shape=[jax.ShapeDtypeStruct((t, d), F32), jax.ShapeDtypeStruct((1, 1), F32), jax.ShapeDtypeStruct((1, d), F32)],
        compiler_params=_params(("arbitrary",)),
    )(h, g, target, valid)


def _weight_grad(x, y, name):
    t, k = x.shape
    n = y.shape[1]
    tm = _token_tile(t, 512)
    tn = n
    while k * tn * 4 > 8 * 1024 * 1024 and tn % 256 == 0:
        tn //= 2
    steps = t // tm

    def body(x_ref, y_ref, o_ref, acc):
        @pl.when(pl.program_id(1) == 0)
        def _():
            acc[...] = jnp.zeros_like(acc)

        acc[...] += _dot_tn(x_ref[...].astype(BF16), y_ref[...].astype(BF16))

        @pl.when(pl.program_id(1) == steps - 1)
        def _():
            o_ref[...] = acc[...].astype(BF16)

    return pl.pallas_call(
        body, name=name, grid=(n // tn, steps),
        in_specs=[pl.BlockSpec((tm, k), lambda j, i: (i, 0)), pl.BlockSpec((tm, tn), lambda j, i: (i, j))],
        out_specs=pl.BlockSpec((k, tn), lambda j, i: (0, j)),
        out_shape=jax.ShapeDtypeStruct((k, n), BF16),
        scratch_shapes=[pltpu.VMEM((k, tn), F32)],
        compiler_params=_params(("parallel", "arbitrary")),
    )(x, y)


def _ffn_bwd(h, dh2, a, b, g, w_gate, w_up, w_down):
    t, d = h.shape
    f = a.shape[1]
    tm = _token_tile(t, 256)

    def body(h_ref, dh2_ref, a_ref, b_ref, g_ref, wg_ref, wu_ref, wd_ref, dh_ref, hn_ref, act_ref, da_ref, db_ref, dg_ref):
        @pl.when(pl.program_id(0) == 0)
        def _():
            dg_ref[...] = jnp.zeros_like(dg_ref)

        gain = g_ref[...]
        hn, xh, r = _rms_fwd(h_ref[...], gain)
        hn_ref[...] = hn.astype(BF16)
        dh2 = dh2_ref[...]
        dact = _dot_nt(dh2.astype(BF16), wd_ref[...])
        ga = a_ref[...].astype(F32)
        up = b_ref[...].astype(F32)
        sg = jax.nn.sigmoid(ga)
        silu = ga * sg
        act_ref[...] = (silu * up).astype(BF16)
        da = (dact * up * (sg * (1.0 + ga * (1.0 - sg)))).astype(BF16)
        db = (dact * silu).astype(BF16)
        da_ref[...] = da
        db_ref[...] = db
        dhn = _dot_nt(da, wg_ref[...]) + _dot_nt(db, wu_ref[...])
        dx, dg_rows = _rms_bwd(dhn, xh, r, gain)
        dh_ref[...] = dh2 + dx
        dg_ref[...] += jnp.sum(dg_rows, axis=0, keepdims=True)

    return pl.pallas_call(
        body, name="ffn_bwd", grid=(t // tm,),
        in_specs=[_row_block(tm, d), _row_block(tm, d), _row_block(tm, f), _row_block(tm, f),
                  _vmem_spec(), _vmem_spec(), _vmem_spec(), _vmem_spec()],
        out_specs=[_row_block(tm, d), _row_block(tm, d), _row_block(tm, f), _row_block(tm, f), _row_block(tm, f),
                   pl.BlockSpec((1, d), lambda i: (0, 0))],
        out_shape=[jax.ShapeDtypeStruct((t, d), F32), jax.ShapeDtypeStruct((t, d), BF16), jax.ShapeDtypeStruct((t, f), BF16),
                   jax.ShapeDtypeStruct((t, f), BF16), jax.ShapeDtypeStruct((t, f), BF16), jax.ShapeDtypeStruct((1, d), F32)],
        compiler_params=_params(("arbitrary",)),
    )(h, dh2, a, b, g, w_gate, w_up, w_down)


def _merge_bwd(dh1, z, pa, pb, w_o, w_pa, w_pb_p):
    t, d = dh1.shape
    hw = w_pb_p.shape[0]
    tm = _token_tile(t, 512)
    gate_block = Z_FIXED // d

    def body(dh_ref, ga_ref, gb_ref, pa_ref, pb_ref, wo_ref, wpa_ref, wpb_ref,
             mg_ref, dpa_ref, dpb_ref, dga_ref, dgb_ref, da_ref, do_ref):
        dm = _dot_nt(dh_ref[...].astype(BF16), wo_ref[...])
        sa = jax.nn.sigmoid(ga_ref[...].astype(F32))
        sb = jax.nn.sigmoid(gb_ref[...].astype(F32))
        pa = pa_ref[...].astype(F32)
        pb = pb_ref[...].astype(F32)
        mg_ref[...] = (sa * pa + sb * pb).astype(BF16)
        dpa = (dm * sa).astype(BF16)
        dpb = (dm * sb).astype(BF16)
        dpa_ref[...] = dpa
        dpb_ref[...] = dpb
        dga_ref[...] = (dm * pa * (sa * (1.0 - sa))).astype(BF16)
        dgb_ref[...] = (dm * pb * (sb * (1.0 - sb))).astype(BF16)
        da_ref[...] = _dot_nt(dpa, wpa_ref[...]).astype(BF16)
        do_ref[...] = _dot_nt(dpb, wpb_ref[...]).astype(BF16)

    wide = jax.ShapeDtypeStruct((t, d), BF16)
    return pl.pallas_call(
        body, name="merge_bwd", grid=(t // tm,),
        in_specs=[_row_block(tm, d), _row_block(tm, d, gate_block), _row_block(tm, d, gate_block + 1),
                  _row_block(tm, d), _row_block(tm, d), _vmem_spec(), _vmem_spec(), _vmem_spec()],
        out_specs=[_row_block(tm, d)] * 5 + [_row_block(tm, POOL_WIDTH), _row_block(tm, hw)],
        out_shape=[wide] * 5 + [jax.ShapeDtypeStruct((t, POOL_WIDTH), BF16), jax.ShapeDtypeStruct((t, hw), BF16)],
        compiler_params=_params(("parallel",)),
    )(dh1, z, z, pa, pb, w_o, w_pa, w_pb_p)


def _attn_bwd(q3, k3, v3, o3, do3, lse3, real_len):
    b, lp, hw = q3.shape
    heads = hw // HEAD_PAD
    tb = ATT_BLOCK
    nfull, tail_start, tail = _attn_blocks(real_len)
    done = tail_start + tail
    hpg = _heads_per_step(heads, ATT_BWD_HEADS)
    width = hpg * HEAD_PAD

    def body(q_ref, k_ref, v_ref, o_ref, do_ref, lse_ref, dq_ref, dk_ref, dv_ref, dq_acc, lse_col, delta_col):
        group = pl.program_id(1)
        lane = lax.broadcasted_iota(jnp.int32, (lp, HEAD_PAD), 1)
        for hd in range(hpg):
            cols = slice(hd * HEAD_PAD, (hd + 1) * HEAD_PAD)
            lse_col[hd] = jnp.sum(jnp.where(lane == group * hpg + hd, lse_ref[0], 0.0), axis=-1, keepdims=True)
            delta_col[hd] = jnp.sum(do_ref[0, :, cols].astype(F32) * o_ref[0, :, cols].astype(F32), axis=-1, keepdims=True)
        dq_acc[...] = jnp.zeros_like(dq_acc)

        def kv_rows(c0, keys, whole_q_blocks_from):
            def q_step(r0, rows, states, mask):
                out = []
                for hd, (dk, dv) in enumerate(states):
                    cols = slice(hd * HEAD_PAD, (hd + 1) * HEAD_PAD)
                    k = k_ref[0, pl.ds(c0, keys), cols]
                    q = q_ref[0, pl.ds(r0, rows), cols]
                    do = do_ref[0, pl.ds(r0, rows), cols]
                    s = _dot_nt(q, k)
                    if mask is not None:
                        s = jnp.where(mask, s, MASK_VALUE)
                    p = jnp.exp(s - lse_col[hd, pl.ds(r0, rows), :])
                    dp = _dot_nt(do, v_ref[0, pl.ds(c0, keys), cols])
                    ds = (p * (dp - delta_col[hd, pl.ds(r0, rows), :])).astype(BF16)
                    dv = dv + _dot_tn(p.astype(BF16), do)
                    dk = dk + _dot_tn(ds, q)
                    dq_acc[pl.ds(r0, rows), cols] += _dot(ds, k)
                    out.append((dk, dv))
                return tuple(out)

            zero = jnp.zeros((keys, HEAD_PAD), F32)
            states = q_step(c0, keys, tuple((zero, zero) for _ in range(hpg)), _causal_mask(keys))
            if whole_q_blocks_from is not None:
                states = lax.fori_loop(whole_q_blocks_from, nfull,
                                       lambda i, st: q_step(pl.multiple_of(i * tb, tb), tb, st, None), states)
                states = q_step(tail_start, tail, states, None)
            for hd, (dk, dv) in enumerate(states):
                cols = slice(hd * HEAD_PAD, (hd + 1) * HEAD_PAD)
                dk_ref[0, pl.ds(c0, keys), cols] = dk.astype(BF16)
                dv_ref[0, pl.ds(c0, keys), cols] = dv.astype(BF16)

        def whole_block(j, carry):
            kv_rows(pl.multiple_of(j * tb, tb), tb, j + 1)
            return carry

        lax.fori_loop(0, nfull, whole_block, 0)
        kv_rows(tail_start, tail, None)
        if done < lp:
            dk_ref[0, done:lp, :] = jnp.zeros((lp - done, width), BF16)
            dv_ref[0, done:lp, :] = jnp.zeros((lp - done, width), BF16)
        dq_ref[0] = dq_acc[...].astype(BF16)

    head_spec = pl.BlockSpec((1, lp, width), lambda bi, hi: (bi, 0, hi))
    out = jax.ShapeDtypeStruct((b, lp, hw), BF16)
    return pl.pallas_call(
        body, name="attn_bwd", grid=(b, heads // hpg),
        in_specs=[head_spec] * 5 + [pl.BlockSpec((1, lp, HEAD_PAD), lambda bi, hi: (bi, 0, 0))],
        out_specs=[head_spec] * 3,
        out_shape=[out, out, out],
        scratch_shapes=[pltpu.VMEM((lp, width), F32), pltpu.VMEM((hpg, lp, 1), F32), pltpu.VMEM((hpg, lp, 1), F32)],
        compiler_params=_params(("parallel", "parallel")),
    )(q3, k3, v3, o3, do3, lse3)


def _qkv_bwd(dq, dk, dv, z, g_q, g_kv, w_uq_p, w_kv_p, rope_c, rope_sa, rope_sb):
    t, hw = dq.shape
    heads = hw // HEAD_PAD
    tm = _token_tile(t, 512)

    def body(dq_ref, dk_ref, dv_ref, cq_ref, ckv_ref, gq_ref, gkv_ref, wq_ref, wkv_ref, c_ref, sa_ref, sb_ref,
             dqraw_ref, dkvraw_ref, cqn_ref, ckvn_ref, dcq_ref, dckv_ref, dkr_ref, dgq_ref, dgkv_ref):
        @pl.when(pl.program_id(0) == 0)
        def _():
            dgq_ref[...] = jnp.zeros_like(dgq_ref)
            dgkv_ref[...] = jnp.zeros_like(dgkv_ref)

        c, sa, sb = c_ref[...], sa_ref[...], sb_ref[...]
        dkr = jnp.zeros((tm, HEAD_PAD), F32)
        for hd in range(heads):
            cols = slice(hd * HEAD_PAD, (hd + 1) * HEAD_PAD)
            dqraw_ref[:, cols] = _rope_bwd(dq_ref[:, cols].astype(F32) * SM_SCALE, c, sa, sb).astype(BF16)
            dkraw = _rope_bwd(dk_ref[:, cols].astype(F32), c, sa, sb)
            dkvraw_ref[:, cols] = dkraw.astype(BF16)
            dkr = dkr + dkraw
        dkvraw_ref[:, hw:] = dv_ref[...]
        lane = lax.broadcasted_iota(jnp.int32, (tm, HEAD_PAD), 1)
        dkr_ref[...] = jnp.where((lane >= QK_NOPE) & (lane < QK_DIM), dkr, 0.0).astype(BF16)

        gq = gq_ref[...]
        cqn, xh, r = _rms_fwd(cq_ref[...].astype(F32), gq)
        cqn_ref[...] = cqn.astype(BF16)
        dx, dg_rows = _rms_bwd(_dot_nt(dqraw_ref[...], wq_ref[...]), xh, r, gq)
        dcq_ref[...] = dx.astype(BF16)
        dgq_ref[...] += jnp.sum(dg_rows, axis=0, keepdims=True)

        gkv = gkv_ref[...]
        ckvn, xh, r = _rms_fwd(ckv_ref[...].astype(F32), gkv)
        ckvn_ref[...] = ckvn.astype(BF16)
        dx, dg_rows = _rms_bwd(_dot_nt(dkvraw_ref[...], wkv_ref[...]), xh, r, gkv)
        dckv_ref[...] = dx.astype(BF16)
        dgkv_ref[...] += jnp.sum(dg_rows, axis=0, keepdims=True)

    def shape(width, dtype=BF16):
        return jax.ShapeDtypeStruct((t, width), dtype)

    return pl.pallas_call(
        body, name="qkv_bwd", grid=(t // tm,),
        in_specs=[_row_block(tm, hw)] * 3
        + [_row_block(tm, Q_RANK, POOL_WIDTH // Q_RANK), _row_block(tm, KV_RANK, (POOL_WIDTH + Q_RANK) // KV_RANK)]
        + [_vmem_spec()] * 4 + [_row_block(tm, HEAD_PAD)] * 3,
        out_specs=[_row_block(tm, hw), _row_block(tm, 2 * hw), _row_block(tm, Q_RANK), _row_block(tm, KV_RANK),
                   _row_block(tm, Q_RANK), _row_block(tm, KV_RANK), _row_block(tm, HEAD_PAD),
                   pl.BlockSpec((1, Q_RANK), lambda i: (0, 0)), pl.BlockSpec((1, KV_RANK), lambda i: (0, 0))],
        out_shape=[shape(hw), shape(2 * hw), shape(Q_RANK), shape(KV_RANK), shape(Q_RANK), shape(KV_RANK), shape(HEAD_PAD),
                   jax.ShapeDtypeStruct((1, Q_RANK), F32), jax.ShapeDtypeStruct((1, KV_RANK), F32)],
        compiler_params=_params(("arbitrary",)),
    )(dq, dk, dv, z, z, g_q, g_kv, w_uq_p, w_kv_p, rope_c, rope_sa, rope_sb)


def _pool_bwd(z3, da3, pool_w, pool_scale):
    b, lp, _ = z3.shape
    groups = len(POOL_WINDOWS)

    def body(u_ref, da_ref, pw_ref, sc_ref, du_ref, dpw_ref, dsc_ref):
        @pl.when(pl.program_id(0) == 0)
        def _():
            dpw_ref[...] = jnp.zeros_like(dpw_ref)
            dsc_ref[...] = jnp.zeros_like(dsc_ref)

        row = lax.broadcasted_iota(jnp.int32, (lp, POOL_GROUP), 0)
        pos = row.astype(F32)
        for gi, w in enumerate(POOL_WINDOWS):
            cols = slice(gi * POOL_GROUP, (gi + 1) * POOL_GROUP)
            count = jnp.minimum(pos + 1.0, float(w))
            u = u_ref[0, :, cols].astype(F32)
            y = (_window_sum(u, w, row, True) / count - u).astype(BF16)
            yw = _dot(y, pw_ref[gi])
            da = da_ref[0, :, cols].astype(F32)
            dsc_ref[:, cols] += jnp.sum(da * yw, axis=0, keepdims=True)
            dyw = (da * sc_ref[:, cols]).astype(BF16)
            dpw_ref[gi] += _dot_tn(y, dyw)
            dy = _dot_nt(dyw, pw_ref[gi])
            du_ref[0, :, cols] = (_window_sum(dy / count, w, row, False) - dy).astype(BF16)

    return pl.pallas_call(
        body, name="pool_bwd", grid=(b,),
        in_specs=[pl.BlockSpec((1, lp, POOL_WIDTH), lambda i: (i, 0, 0)), pl.BlockSpec((1, lp, POOL_WIDTH), lambda i: (i, 0, 0)),
                  _vmem_spec(), _vmem_spec()],
        out_specs=[pl.BlockSpec((1, lp, POOL_WIDTH), lambda i: (i, 0, 0)),
                   pl.BlockSpec((groups, POOL_GROUP, POOL_GROUP), lambda i: (0, 0, 0)),
                   pl.BlockSpec((1, POOL_WIDTH), lambda i: (0, 0))],
        out_shape=[jax.ShapeDtypeStruct((b, lp, POOL_WIDTH), BF16), jax.ShapeDtypeStruct((groups, POOL_GROUP, POOL_GROUP), F32),
                   jax.ShapeDtypeStruct((1, POOL_WIDTH), F32)],
        compiler_params=_params(("arbitrary",)),
    )(z3, da3, pool_w, pool_scale)


def _in_proj_bwd(h, dh1, du, dcq, dckv, dkr, dga, dgb, g, w_in_p):
    t, d = h.shape
    nz = w_in_p.shape[1]
    tm = _token_tile(t, 512)
    widths = (POOL_WIDTH, Q_RANK, KV_RANK, HEAD_PAD, d, d)

    def body(h_ref, dh1_ref, du_ref, dcq_ref, dckv_ref, dkr_ref, dga_ref, dgb_ref, g_ref, w_ref, dh_ref, hn_ref, dz_ref, dg_ref):
        @pl.when(pl.program_id(0) == 0)
        def _():
            dg_ref[...] = jnp.zeros_like(dg_ref)

        gain = g_ref[...]
        hn, xh, r = _rms_fwd(h_ref[...], gain)
        hn_ref[...] = hn.astype(BF16)
        dhn = jnp.zeros((tm, d), F32)
        start = 0
        for piece, width in zip((du_ref, dcq_ref, dckv_ref, dkr_ref, dga_ref, dgb_ref), widths):
            val = piece[...]
            dz_ref[:, start:start + width] = val
            dhn = dhn + _dot_nt(val, w_ref[:, start:start + width])
            start += width
        dx, dg_rows = _rms_bwd(dhn, xh, r, gain)
        dh_ref[...] = dh1_ref[...] + dx
        dg_ref[...] += jnp.sum(dg_rows, axis=0, keepdims=True)

    return pl.pallas_call(
        body, name="in_proj_bwd", grid=(t // tm,),
        in_specs=[_row_block(tm, d), _row_block(tm, d)] + [_row_block(tm, w) for w in widths] + [_vmem_spec(), _vmem_spec()],
        out_specs=[_row_block(tm, d), _row_block(tm, d), _row_block(tm, nz), pl.BlockSpec((1, d), lambda i: (0, 0))],
        out_shape=[jax.ShapeDtypeStruct((t, d), F32), jax.ShapeDtypeStruct((t, d), BF16), jax.ShapeDtypeStruct((t, nz), BF16),
                   jax.ShapeDtypeStruct((1, d), F32)],
        compiler_params=_params(("arbitrary",)),
    )(h, dh1, du, dcq, dckv, dkr, dga, dgb, g, w_in_p)


def _pad_heads(w, heads, width):
    k = w.shape[0]
    w = w.reshape(k, heads, width)
    return jnp.pad(w, ((0, 0), (0, 0), (0, HEAD_PAD - width))).reshape(k, heads * HEAD_PAD)


def _unpad_heads(w, heads, width):
    k = w.shape[0]
    return w.reshape(k, heads, HEAD_PAD)[:, :, :width].reshape(k, heads * width)


def _layer_layouts(w, heads):
    d = w["w_in"].shape[0]
    o1, o2, o3, o4 = POOL_WIDTH, POOL_WIDTH + Q_RANK, POOL_WIDTH + Q_RANK + KV_RANK, POOL_WIDTH + Q_RANK + KV_RANK + QK_ROPE
    w_in = w["w_in"]
    rope_cols = jnp.pad(w_in[:, o3:o4], ((0, 0), (QK_NOPE, HEAD_PAD - QK_DIM)))
    w_in_p = jnp.concatenate([w_in[:, :o3], rope_cols, w_in[:, o4:]], axis=1)
    w_uq_p = _pad_heads(w["w_uq"], heads, QK_DIM)
    kv = w["w_ukv"].reshape(KV_RANK, heads, QK_NOPE + V_DIM)
    w_k = jnp.pad(kv[:, :, :QK_NOPE], ((0, 0), (0, 0), (0, HEAD_PAD - QK_NOPE))).reshape(KV_RANK, heads * HEAD_PAD)
    w_v = jnp.pad(kv[:, :, QK_NOPE:], ((0, 0), (0, 0), (0, HEAD_PAD - V_DIM))).reshape(KV_RANK, heads * HEAD_PAD)
    w_kv_p = jnp.concatenate([w_k, w_v], axis=1)
    w_pb_p = jnp.pad(w["w_pb"].reshape(heads, V_DIM, d), ((0, 0), (0, HEAD_PAD - V_DIM), (0, 0))).reshape(heads * HEAD_PAD, d)
    return dict(w_in_p=w_in_p, w_uq_p=w_uq_p, w_kv_p=w_kv_p, w_pa=w["w_pa"], w_pb_p=w_pb_p, w_o=w["w_o"],
                w_gate=w["w_gate"], w_up=w["w_up"], w_down=w["w_down"])


def _layer_grad_layouts(g, heads, d):
    o3 = POOL_WIDTH + Q_RANK + KV_RANK
    gin = g["w_in_p"]
    w_in = jnp.concatenate([gin[:, :o3], gin[:, o3 + QK_NOPE:o3 + QK_DIM], gin[:, o3 + HEAD_PAD:]], axis=1)
    hw = heads * HEAD_PAD
    gk = g["w_kv_p"][:, :hw].reshape(KV_RANK, heads, HEAD_PAD)[:, :, :QK_NOPE]
    gv = g["w_kv_p"][:, hw:].reshape(KV_RANK, heads, HEAD_PAD)[:, :, :V_DIM]
    w_ukv = jnp.concatenate([gk, gv], axis=2).reshape(KV_RANK, heads * (QK_NOPE + V_DIM))
    w_pb = g["w_pb_p"].reshape(heads, HEAD_PAD, d)[:, :V_DIM].reshape(heads * V_DIM, d)
    return dict(w_in=w_in, w_uq=_unpad_heads(g["w_uq_p"], heads, QK_DIM), w_ukv=w_ukv, w_pa=g["w_pa"], w_pb=w_pb,
                w_o=g["w_o"], w_gate=g["w_gate"], w_up=g["w_up"], w_down=g["w_down"])


def _rope_tables(lp, b):
    inv = 1.0 / (ROPE_THETA ** (jnp.arange(0, QK_ROPE, 2, dtype=F32) / QK_ROPE))
    ang = jnp.arange(lp, dtype=F32)[:, None] * inv[None, :]
    cos, sin = jnp.cos(ang), jnp.sin(ang)
    half = QK_ROPE // 2
    ones = jnp.ones((lp, QK_NOPE), F32)
    zeros_lo = jnp.zeros((lp, QK_NOPE), F32)
    zeros_hi = jnp.zeros((lp, HEAD_PAD - QK_DIM), F32)
    zeros_half = jnp.zeros((lp, half), F32)
    c = jnp.concatenate([ones, cos, cos, zeros_hi], axis=1)
    sa = jnp.concatenate([zeros_lo, zeros_half, sin, zeros_hi], axis=1)
    sb = jnp.concatenate([zeros_lo, -sin, zeros_half, zeros_hi], axis=1)
    return tuple(jnp.tile(tab, (b, 1)) for tab in (c, sa, sb))


def _local_step(x, loss_target, meta_tokens, small, big):
    b, seq, d = x.shape
    depth = len(big)
    heads = big[0]["w_uq"].shape[1] // QK_DIM
    real_len = N_META + seq
    lp = _round_up(real_len, SEQ_PAD)
    t = b * lp
    pad = lp - N_META - seq

    meta = jnp.broadcast_to(meta_tokens[None], (b, N_META, d))
    h = jnp.concatenate([meta, x, jnp.zeros((b, pad, d), F32)], axis=1).reshape(t, d)
    target = jnp.pad(loss_target, ((0, 0), (N_META, pad), (0, 0))).reshape(t, d)
    pos = jnp.arange(lp)
    valid = jnp.tile(((pos >= N_META) & (pos < N_META + seq)).astype(F32), b).reshape(t, 1)
    rope_c, rope_sa, rope_sb = _rope_tables(lp, b)

    layers = []
    for li in range(depth):
        lay = _layer_layouts(big[li], heads)
        lay["pool_w"] = small["pool_w"][li].astype(BF16)
        lay["pool_scale"] = small["pool_scale"][li][None]
        for n in ("norm_mix_g", "q_norm_g", "kv_norm_g", "norm_ffn_g"):
            lay[n] = small[n][li][None]
        layers.append(lay)

    saved = []
    for lay in layers:
        z = _in_proj_fwd(h, lay["norm_mix_g"], lay["w_in_p"])
        a = _pool_fwd(z.reshape(b, lp, -1), lay["pool_w"], lay["pool_scale"]).reshape(t, POOL_WIDTH)
        q, k, v = _qkv_fwd(z, lay["q_norm_g"], lay["kv_norm_g"], lay["w_uq_p"], lay["w_kv_p"], rope_c, rope_sa, rope_sb)
        hw = q.shape[1]
        o3, lse = _attn_fwd(q.reshape(b, lp, hw), k.reshape(b, lp, hw), v.reshape(b, lp, hw), real_len)
        o = o3.reshape(t, hw)
        h1, pa, pb = _merge_fwd(h, z, a, o, lay["w_pa"], lay["w_pb_p"], lay["w_o"])
        h2, fa, fb = _ffn_fwd(h1, lay["norm_ffn_g"], lay["w_gate"], lay["w_up"], lay["w_down"])
        saved.append(dict(h=h, z=z, a=a, q=q, k=k, v=v, o=o, lse=lse, pa=pa, pb=pb, h1=h1, fa=fa, fb=fb))
        h = h2

    dh, loss, d_final = _loss_head(h, small["final_norm_g"][None], target, valid)

    g_small = {n: [] for n in SMALL_WEIGHTS if n != "final_norm_g"}
    g_big = []
    for lay, sv in zip(reversed(layers), reversed(saved)):
        hw = sv["q"].shape[1]
        dh1, hn_f, act, dfa, dfb, dg_ffn = _ffn_bwd(sv["h1"], dh, sv["fa"], sv["fb"], lay["norm_ffn_g"],
                                                     lay["w_gate"], lay["w_up"], lay["w_down"])
        gl = dict(w_gate=_weight_grad(hn_f, dfa, "grad_w_gate"), w_up=_weight_grad(hn_f, dfb, "grad_w_up"),
                  w_down=_weight_grad(act, dh, "grad_w_down"))
        merged, dpa, dpb, dga, dgb, da, do = _merge_bwd(dh1, sv["z"], sv["pa"], sv["pb"], lay["w_o"], lay["w_pa"], lay["w_pb_p"])
        gl["w_o"] = _weight_grad(merged, dh1, "grad_w_o")
        gl["w_pa"] = _weight_grad(sv["a"], dpa, "grad_w_pa")
        gl["w_pb_p"] = _weight_grad(sv["o"], dpb, "grad_w_pb")
        shape3 = (b, lp, hw)
        dq3, dk3, dv3 = _attn_bwd(sv["q"].reshape(shape3), sv["k"].reshape(shape3), sv["v"].reshape(shape3),
                                  sv["o"].reshape(shape3), do.reshape(shape3), sv["lse"], real_len)
        dqraw, dkvraw, cqn, ckvn, dcq, dckv, dkr, dg_q, dg_kv = _qkv_bwd(
            dq3.reshape(t, hw), dk3.reshape(t, hw), dv3.reshape(t, hw), sv["z"], lay["q_norm_g"], lay["kv_norm_g"],
            lay["w_uq_p"], lay["w_kv_p"], rope_c, rope_sa, rope_sb)
        gl["w_uq_p"] = _weight_grad(cqn, dqraw, "grad_w_uq")
        gl["w_kv_p"] = _weight_grad(ckvn, dkvraw, "grad_w_ukv")
        du3, dpool_w, dpool_scale = _pool_bwd(sv["z"].reshape(b, lp, -1), da.reshape(b, lp, POOL_WIDTH),
                                              lay["pool_w"], lay["pool_scale"])
        dh, hn_m, dz, dg_mix = _in_proj_bwd(sv["h"], dh1, du3.reshape(t, POOL_WIDTH), dcq, dckv, dkr, dga, dgb,
                                            lay["norm_mix_g"], lay["w_in_p"])
        gl["w_in_p"] = _weight_grad(hn_m, dz, "grad_w_in")
        g_big.insert(0, _layer_grad_layouts(gl, heads, d))
        for n, val in (("norm_mix_g", dg_mix[0]), ("pool_w", dpool_w), ("pool_scale", dpool_scale[0]), ("q_norm_g", dg_q[0]),
                       ("kv_norm_g", dg_kv[0]), ("norm_ffn_g", dg_ffn[0])):
            g_small[n].insert(0, val)

    dh3 = dh.reshape(b, lp, d)
    grad_x = dh3[:, N_META:N_META + seq]
    d_meta_rows = dh3[:, :N_META]
    g_small = {n: jnp.stack(v) for n, v in g_small.items()}
    g_small["final_norm_g"] = d_final[0]
    return loss, grad_x, d_meta_rows, g_small, g_big


def _mesh_place():
    x, y, c = lax.axis_index("x"), lax.axis_index("y"), lax.axis_index("c")
    others = [(1 - x, y), (x, 1 - y), (1 - x, 1 - y)]
    return x, y, c, 2 * x + y, others


def _remote(src, dst, send_sems, recv_sems, k, device):
    return pltpu.make_async_remote_copy(src_ref=src, dst_ref=dst, send_sem=send_sems.at[k], recv_sem=recv_sems.at[k],
                                        device_id=device, device_id_type=MESH)


def _all_gather_shards(packed, meta_shard):
    _, rh, cols = packed.shape

    def body(p_ref, m_ref, g_ref, gm_ref, send_sems, recv_sems):
        x, y, c, chip, others = _mesh_place()
        sibling = (x, y, 1 - c)
        sends = []
        for r, (ox, oy) in enumerate(others):
            sends.append(_remote(p_ref.at[c], g_ref.at[chip, c], send_sems, recv_sems, r, (ox, oy, c)))
            sends.append(_remote(m_ref, gm_ref.at[chip], send_sems, recv_sems, 6 + r, (ox, oy, c)))
        for cp in sends:
            cp.start()
        for r, (ox, oy) in enumerate(others):
            src_chip = 2 * ox + oy
            _remote(p_ref.at[c], g_ref.at[src_chip, c], send_sems, recv_sems, r, (ox, oy, c)).wait_recv()
            passed = _remote(g_ref.at[src_chip, c], g_ref.at[src_chip, c], send_sems, recv_sems, 3 + r, sibling)
            passed.start()
            sends.append(passed)
        for r, (ox, oy) in enumerate(others):
            src_chip = 2 * ox + oy
            _remote(p_ref.at[c], g_ref.at[src_chip, 1 - c], send_sems, recv_sems, 3 + r, sibling).wait_recv()
            _remote(m_ref, gm_ref.at[src_chip], send_sems, recv_sems, 6 + r, (ox, oy, c)).wait_recv()
        for cp in sends:
            cp.wait_send()

    gathered, meta_all = pl.pallas_call(
        body, name="all_gather_shards",
        in_specs=[_any_spec(), _any_spec()], out_specs=[_any_spec(), _any_spec()],
        out_shape=[jax.ShapeDtypeStruct((N_CHIPS, 2, rh, cols), packed.dtype),
                   jax.ShapeDtypeStruct((N_CHIPS,) + meta_shard.shape, meta_shard.dtype)],
        scratch_shapes=[pltpu.SemaphoreType.DMA((9,)), pltpu.SemaphoreType.DMA((9,))],
    )(packed, meta_shard)
    chip = 2 * lax.axis_index("x") + lax.axis_index("y")
    return (lax.dynamic_update_index_in_dim(gathered, packed, chip, 0),
            lax.dynamic_update_index_in_dim(meta_all, meta_shard, chip, 0))


def _pair_exchange(give):
    def body(give_ref, got_ref, send_sems, recv_sems):
        x, y, c, _, _ = _mesh_place()
        cp = _remote(give_ref, got_ref, send_sems, recv_sems, 0, (x, y, 1 - c))
        cp.start()
        cp.wait()

    return pl.pallas_call(
        body, name="pair_exchange", in_specs=[_any_spec()], out_specs=_any_spec(),
        out_shape=jax.ShapeDtypeStruct(give.shape, give.dtype),
        scratch_shapes=[pltpu.SemaphoreType.DMA((1,)), pltpu.SemaphoreType.DMA((1,))],
    )(give)


def _chip_exchange(parts):
    def body(p_ref, got_ref, send_sems, recv_sems):
        _, _, c, chip, others = _mesh_place()
        sends = [_remote(p_ref.at[2 * ox + oy], got_ref.at[chip], send_sems, recv_sems, r, (ox, oy, c))
                 for r, (ox, oy) in enumerate(others)]
        for cp in sends:
            cp.start()
        for r, (ox, oy) in enumerate(others):
            _remote(p_ref.at[chip], got_ref.at[2 * ox + oy], send_sems, recv_sems, r, (ox, oy, c)).wait_recv()
        for cp in sends:
            cp.wait_send()

    got = pl.pallas_call(
        body, name="chip_exchange", in_specs=[_any_spec()], out_specs=_any_spec(),
        out_shape=jax.ShapeDtypeStruct(parts.shape, parts.dtype),
        scratch_shapes=[pltpu.SemaphoreType.DMA((3,)), pltpu.SemaphoreType.DMA((3,))],
    )(parts)
    chip = 2 * lax.axis_index("x") + lax.axis_index("y")
    own = lax.dynamic_index_in_dim(parts, chip, 0, keepdims=False)
    return lax.dynamic_update_index_in_dim(got, own, chip, 0)


def _pair_gather(half):
    def body(h_ref, out_ref, send_sems, recv_sems):
        x, y, c, _, _ = _mesh_place()
        cp = _remote(h_ref, out_ref.at[c], send_sems, recv_sems, 0, (x, y, 1 - c))
        cp.start()
        _remote(h_ref, out_ref.at[1 - c], send_sems, recv_sems, 0, (x, y, 1 - c)).wait_recv()
        cp.wait_send()

    both = pl.pallas_call(
        body, name="pair_gather", in_specs=[_any_spec()], out_specs=_any_spec(),
        out_shape=jax.ShapeDtypeStruct((2,) + half.shape, half.dtype),
        scratch_shapes=[pltpu.SemaphoreType.DMA((1,)), pltpu.SemaphoreType.DMA((1,))],
    )(half)
    return lax.dynamic_update_index_in_dim(both, half, lax.axis_index("c"), 0)


def _row_tile(rows, limit=PACK_TILE):
    if rows <= limit:
        return rows
    for tr in range(limit, 7, -8):
        if rows % tr == 0:
            return tr
    return rows


def _pair_add(keep, got):
    n, rh, cols = keep.shape
    tr = _row_tile(rh)

    def body(k_ref, g_ref, o_ref):
        o_ref[...] = (k_ref[...].astype(F32) + g_ref[...].astype(F32)).astype(BF16)

    spec = pl.BlockSpec((1, tr, cols), lambda j, i: (j, i, 0))
    return pl.pallas_call(
        body, name="pair_add", grid=(n, rh // tr), in_specs=[spec, spec], out_specs=spec,
        out_shape=jax.ShapeDtypeStruct(keep.shape, BF16),
        compiler_params=_params(("parallel", "parallel")),
    )(keep, got)


def _chip_sum(parts):
    n, rh, cols = parts.shape
    tr = _row_tile(rh)

    def body(p_ref, o_ref):
        total = p_ref[0].astype(F32)
        for k in range(1, n):
            total = total + p_ref[k].astype(F32)
        o_ref[...] = total

    return pl.pallas_call(
        body, name="chip_sum", grid=(rh // tr,),
        in_specs=[pl.BlockSpec((n, tr, cols), lambda i: (0, i, 0))], out_specs=pl.BlockSpec((tr, cols), lambda i: (i, 0)),
        out_shape=jax.ShapeDtypeStruct((rh, cols), F32),
        compiler_params=_params(("parallel",)),
    )(parts)


def _reduce_scatter(grads, c):
    keep = lax.dynamic_index_in_dim(grads, c, axis=1, keepdims=False)
    give = lax.dynamic_index_in_dim(grads, 1 - c, axis=1, keepdims=False)
    chip_partial = _pair_add(keep, _pair_exchange(give))
    return _pair_gather(_chip_sum(_chip_exchange(chip_partial)))


def _all_reduce_small(meta_rows, small):
    b, rm, cols = meta_rows.shape
    rows = rm + small.shape[0]

    def body(meta_ref, small_ref, out_ref, mine, pair_buf, chip_buf, send_sems, recv_sems):
        x, y, c, chip, others = _mesh_place()
        acc = meta_ref[0]
        for i in range(1, b):
            acc = acc + meta_ref[i]
        mine[0:rm, :] = acc
        mine[rm:rows, :] = small_ref[...]
        pair = _remote(mine, pair_buf, send_sems, recv_sems, 0, (x, y, 1 - c))
        pair.start()
        pair.wait()
        chip_buf[chip] = mine[...] + pair_buf[...]
        sends = [_remote(chip_buf.at[chip], chip_buf.at[chip], send_sems, recv_sems, 1 + r, (ox, oy, c))
                 for r, (ox, oy) in enumerate(others)]
        for cp in sends:
            cp.start()
        for r, (ox, oy) in enumerate(others):
            _remote(chip_buf.at[chip], chip_buf.at[2 * ox + oy], send_sems, recv_sems, 1 + r, (ox, oy, c)).wait_recv()
        for cp in sends:
            cp.wait_send()
        out_ref[...] = ((chip_buf[0] + chip_buf[1]) + chip_buf[2]) + chip_buf[3]

    return pl.pallas_call(
        body, name="all_reduce_small",
        in_specs=[_vmem_spec(), _vmem_spec()], out_specs=_vmem_spec(),
        out_shape=jax.ShapeDtypeStruct((rows, cols), F32),
        scratch_shapes=[pltpu.VMEM((rows, cols), F32), pltpu.VMEM((rows, cols), F32), pltpu.VMEM((N_CHIPS, rows, cols), F32),
                        pltpu.SemaphoreType.DMA((4,)), pltpu.SemaphoreType.DMA((4,))],
        compiler_params=pltpu.CompilerParams(vmem_limit_bytes=VMEM_LIMIT),
    )(meta_rows, small)


def _adamw(w, g, m, v):
    shape = w.shape
    cols = shape[-1]
    rows = w.size // cols
    tr = _row_tile(rows)

    def body(w_ref, g_ref, m_ref, v_ref, d_ref, m2_ref, v2_ref):
        grad = g_ref[...]
        m2 = ADAM_B1 * m_ref[...] + (1.0 - ADAM_B1) * grad
        v2 = ADAM_B2 * v_ref[...] + (1.0 - ADAM_B2) * jnp.square(grad)
        m_hat = m2 / (1.0 - ADAM_B1 ** ADAM_STEP)
        v_hat = v2 / (1.0 - ADAM_B2 ** ADAM_STEP)
        d_ref[...] = -ADAM_LR * (m_hat / (jnp.sqrt(v_hat) + ADAM_EPS) + ADAM_WD * w_ref[...])
        m2_ref[...] = m2
        v2_ref[...] = v2

    spec = pl.BlockSpec((tr, cols), lambda i: (i, 0))
    out = jax.ShapeDtypeStruct((rows, cols), F32)
    res = pl.pallas_call(
        body, name="adamw", grid=(rows // tr,), in_specs=[spec] * 4, out_specs=[spec] * 3, out_shape=[out] * 3,
        compiler_params=_params(("parallel",)),
    )(*(a.reshape(rows, cols) for a in (w, g, m, v)))
    return tuple(r.reshape(shape) for r in res)


def _pack_rows(arrays):
    flat = [a.reshape(-1, PACK_COLS) for a in arrays]
    counts = [f.shape[0] for f in flat]
    total = sum(counts)
    half = -(-total // 2)
    tiles = -(-half // PACK_TILE)
    padded = 2 * tiles * _round_up(-(-half // tiles), 16)
    if padded > total:
        flat.append(jnp.zeros((padded - total, PACK_COLS), flat[0].dtype))
    return jnp.concatenate(flat, axis=0), counts


def _unpack_rows(buffer, counts, shapes):
    out, start = [], 0
    for n, shape in zip(counts, shapes):
        out.append(buffer[..., start:start + n, :].reshape(buffer.shape[:-2] + tuple(shape)))
        start += n
    return out


def kernel(x, meta_tokens, norm_mix_g, w_in, pool_w, pool_scale, q_norm_g, kv_norm_g, w_uq, w_ukv, w_pa, w_pb, w_o, norm_ffn_g, w_gate, w_up, w_down, final_norm_g, loss_target, m_meta_tokens, m_norm_mix_g, m_w_in, m_pool_w, m_pool_scale, m_q_norm_g, m_kv_norm_g, m_w_uq, m_w_ukv, m_w_pa, m_w_pb, m_w_o, m_norm_ffn_g, m_w_gate, m_w_up, m_w_down, m_final_norm_g, v_meta_tokens, v_norm_mix_g, v_w_in, v_pool_w, v_pool_scale, v_q_norm_g, v_kv_norm_g, v_w_uq, v_w_ukv, v_w_pa, v_w_pb, v_w_o, v_norm_ffn_g, v_w_gate, v_w_up, v_w_down, v_final_norm_g):
    weights = dict(meta_tokens=meta_tokens, norm_mix_g=norm_mix_g, w_in=w_in, pool_w=pool_w, pool_scale=pool_scale,
                   q_norm_g=q_norm_g, kv_norm_g=kv_norm_g, w_uq=w_uq, w_ukv=w_ukv, w_pa=w_pa, w_pb=w_pb, w_o=w_o,
                   norm_ffn_g=norm_ffn_g, w_gate=w_gate, w_up=w_up, w_down=w_down, final_norm_g=final_norm_g)
    first = dict(meta_tokens=m_meta_tokens, norm_mix_g=m_norm_mix_g, w_in=m_w_in, pool_w=m_pool_w, pool_scale=m_pool_scale,
                 q_norm_g=m_q_norm_g, kv_norm_g=m_kv_norm_g, w_uq=m_w_uq, w_ukv=m_w_ukv, w_pa=m_w_pa, w_pb=m_w_pb, w_o=m_w_o,
                 norm_ffn_g=m_norm_ffn_g, w_gate=m_w_gate, w_up=m_w_up, w_down=m_w_down, final_norm_g=m_final_norm_g)
    second = dict(meta_tokens=v_meta_tokens, norm_mix_g=v_norm_mix_g, w_in=v_w_in, pool_w=v_pool_w, pool_scale=v_pool_scale,
                  q_norm_g=v_q_norm_g, kv_norm_g=v_kv_norm_g, w_uq=v_w_uq, w_ukv=v_w_ukv, w_pa=v_w_pa, w_pb=v_w_pb, w_o=v_w_o,
                  norm_ffn_g=v_norm_ffn_g, w_gate=v_w_gate, w_up=v_w_up, w_down=v_w_down, final_norm_g=v_final_norm_g)
    core = lax.axis_index("c")
    chip = 2 * lax.axis_index("x") + lax.axis_index("y")
    d = x.shape[-1]
    meta_cols = meta_tokens.shape[1]

    shard_shapes = [weights[n].shape for n in BIG_WEIGHTS]
    packed, counts = _pack_rows([weights[n].astype(BF16) for n in BIG_WEIGHTS])
    total_rows = packed.shape[0]
    gathered, meta_all = _all_gather_shards(packed.reshape(2, total_rows // 2, PACK_COLS), meta_tokens)
    per_chip = _unpack_rows(gathered.reshape(N_CHIPS, total_rows, PACK_COLS), counts, shard_shapes)
    depth = w_in.shape[0]
    big = [{n: jnp.concatenate([per_chip[i][j, li] for j in range(N_CHIPS)], axis=SHARD_AXIS[n] - 1)
            for i, n in enumerate(BIG_WEIGHTS)} for li in range(depth)]
    meta_full = jnp.concatenate([meta_all[j] for j in range(N_CHIPS)], axis=1)
    small = {n: weights[n] for n in SMALL_WEIGHTS}

    loss, grad_x, d_meta_rows, g_small, g_big = _local_step(x, loss_target, meta_full, small, big)

    pad_rows = total_rows - sum(counts)
    pieces = []
    for j in range(N_CHIPS):
        for n in BIG_WEIGHTS:
            for li in range(depth):
                pieces.append(jnp.split(g_big[li][n], N_CHIPS, axis=SHARD_AXIS[n] - 1)[j].reshape(-1, PACK_COLS))
        if pad_rows:
            pieces.append(jnp.zeros((pad_rows, PACK_COLS), BF16))
    partial = jnp.concatenate(pieces, axis=0).reshape(N_CHIPS, 2, total_rows // 2, PACK_COLS)
    reduced = _reduce_scatter(partial, core).reshape(total_rows, PACK_COLS)
    grads = dict(zip(BIG_WEIGHTS, _unpack_rows(reduced, counts, shard_shapes)))

    small_shapes = [weights[n].shape for n in SMALL_WEIGHTS]
    small_flat = jnp.concatenate([g_small[n].reshape(-1) for n in SMALL_WEIGHTS])
    small_len = small_flat.shape[0]
    small_rows = _round_up(-(-small_len // PACK_COLS), 8)
    small_pack = jnp.pad(small_flat, (0, small_rows * PACK_COLS - small_len)).reshape(small_rows, PACK_COLS)
    meta_rows = N_META * d // PACK_COLS
    summed = _all_reduce_small(d_meta_rows.reshape(-1, meta_rows, PACK_COLS), small_pack)
    grad_meta_full = summed[:meta_rows].reshape(N_META, d)
    grads["meta_tokens"] = lax.dynamic_slice_in_dim(grad_meta_full, chip * meta_cols, meta_cols, axis=1)
    small_sum = summed[meta_rows:].reshape(-1)
    start = 0
    for n, shape in zip(SMALL_WEIGHTS, small_shapes):
        size = 1
        for s in shape:
            size *= s
        grads[n] = small_sum[start:start + size].reshape(shape)
        start += size

    deltas, new_m, new_v = {}, {}, {}
    for n in WEIGHT_ORDER:
        deltas[n], new_m[n], new_v[n] = _adamw(weights[n], grads[n], first[n], second[n])

    total_loss = lax.psum(loss[0, 0], ("x", "y", "c"))
    return (total_loss, grad_x, *[grads[n] for n in WEIGHT_ORDER], *[deltas[n] for n in WEIGHT_ORDER],
            *[new_m[n] for n in WEIGHT_ORDER], *[new_v[n] for n in WEIGHT_ORDER])
```

```python
import functools

import jax
import jax.numpy as jnp
from jax import lax
from jax.experimental import pallas as pl
from jax.experimental.pallas import tpu as pltpu

F32 = jnp.float32
BF16 = jnp.bfloat16

N_META = 16
POOL_WINDOWS = (2, 4, 8, 16)
POOL_GROUP = 128
POOL_WIDTH = POOL_GROUP * len(POOL_WINDOWS)
QK_NOPE = 64
QK_ROPE = 32
V_DIM = 64
QK_DIM = QK_NOPE + QK_ROPE
Q_RANK = 256
KV_RANK = 128
HEAD_PAD = 128
SM_SCALE = QK_DIM ** -0.5
ROPE_THETA = 10000.0
NORM_EPS = 1e-6
MASK_VALUE = -1e30
Z_FIXED = POOL_WIDTH + Q_RANK + KV_RANK + HEAD_PAD

ADAM_LR = 0.001
ADAM_B1 = 0.9
ADAM_B2 = 0.999
ADAM_EPS = 1e-08
ADAM_WD = 0.01
ADAM_STEP = 10

N_CHIPS = 4
ATT_BLOCK = 256
SEQ_PAD = 128
ATT_FWD_HEADS = 4
ATT_BWD_HEADS = 4
PACK_COLS = 1024
PACK_TILE = 512
VMEM_LIMIT = 60 * 1024 * 1024

MESH = pl.DeviceIdType.MESH

BIG_WEIGHTS = ("w_in", "w_uq", "w_ukv", "w_pa", "w_pb", "w_o", "w_gate", "w_up", "w_down")
SHARD_AXIS = {"w_in": 2, "w_uq": 2, "w_ukv": 2, "w_pa": 2, "w_pb": 1, "w_o": 1, "w_gate": 2, "w_up": 2, "w_down": 1}
SMALL_WEIGHTS = ("norm_mix_g", "pool_w", "pool_scale", "q_norm_g", "kv_norm_g", "norm_ffn_g", "final_norm_g")
WEIGHT_ORDER = ("meta_tokens", "norm_mix_g", "w_in", "pool_w", "pool_scale", "q_norm_g", "kv_norm_g", "w_uq", "w_ukv",
                "w_pa", "w_pb", "w_o", "norm_ffn_g", "w_gate", "w_up", "w_down", "final_norm_g")


def _round_up(n, m):
    return -(-n // m) * m


def _vmem_spec():
    return pl.BlockSpec(memory_space=pltpu.VMEM)


def _any_spec():
    return pl.BlockSpec(memory_space=pl.ANY)


def _row_block(tm, width, col_block=0):
    return pl.BlockSpec((tm, width), lambda i, cb=col_block: (i, cb))


def _params(sem, vmem=VMEM_LIMIT):
    return pltpu.CompilerParams(dimension_semantics=sem, vmem_limit_bytes=vmem)


def _token_tile(t, want):
    best = SEQ_PAD
    for tm in range(32, min(t, 2 * want) + 1, 32):
        if t % tm == 0 and abs(tm - want) < abs(best - want):
            best = tm
    return best


def _dot(a, b):
    return jnp.dot(a, b, preferred_element_type=F32)


def _dot_nt(a, b):
    return lax.dot_general(a, b, (((1,), (1,)), ((), ())), preferred_element_type=F32)


def _dot_tn(a, b):
    return lax.dot_general(a, b, (((0,), (0,)), ((), ())), preferred_element_type=F32)


def _rms_fwd(x, g):
    r = lax.rsqrt(jnp.mean(x * x, axis=-1, keepdims=True) + NORM_EPS)
    xh = x * r
    return xh * g, xh, r


def _rms_bwd(dy, xh, r, g):
    gdy = dy * g
    dx = r * (gdy - xh * jnp.mean(xh * gdy, axis=-1, keepdims=True))
    return dx, dy * xh


def _rope_fwd(x, c, sa, sb):
    return x * c + pltpu.roll(x, 16, 1) * sa + pltpu.roll(x, HEAD_PAD - 16, 1) * sb


def _rope_bwd(d, c, sa, sb):
    return d * c + pltpu.roll(d * sa, HEAD_PAD - 16, 1) + pltpu.roll(d * sb, 16, 1)


def _in_proj_fwd(h, g, w_in_p):
    t, d = h.shape
    nz = w_in_p.shape[1]
    tm = _token_tile(t, 512)

    def body(h_ref, g_ref, w_ref, z_ref):
        hn, _, _ = _rms_fwd(h_ref[...], g_ref[...])
        z_ref[...] = _dot(hn.astype(BF16), w_ref[...]).astype(BF16)

    return pl.pallas_call(
        body, name="in_proj_fwd", grid=(t // tm,),
        in_specs=[_row_block(tm, d), _vmem_spec(), _vmem_spec()],
        out_specs=_row_block(tm, nz),
        out_shape=jax.ShapeDtypeStruct((t, nz), BF16),
        compiler_params=_params(("parallel",)),
    )(h, g, w_in_p)


def _window_sum(x, w, row, forward):
    n = x.shape[0]
    s = x
    k = 1
    while k < w:
        if forward:
            s = s + jnp.where(row >= k, pltpu.roll(s, k, 0), 0.0)
        else:
            s = s + jnp.where(row < n - k, pltpu.roll(s, n - k, 0), 0.0)
        k *= 2
    return s


def _pool_fwd(z3, pool_w, pool_scale):
    b, lp, _ = z3.shape

    def body(u_ref, pw_ref, sc_ref, a_ref):
        row = lax.broadcasted_iota(jnp.int32, (lp, POOL_GROUP), 0)
        pos = row.astype(F32)
        for gi, w in enumerate(POOL_WINDOWS):
            cols = slice(gi * POOL_GROUP, (gi + 1) * POOL_GROUP)
            u = u_ref[0, :, cols].astype(F32)
            y = _window_sum(u, w, row, True) / jnp.minimum(pos + 1.0, float(w)) - u
            yw = _dot(y.astype(BF16), pw_ref[gi])
            a_ref[0, :, cols] = (yw * sc_ref[:, cols]).astype(BF16)

    return pl.pallas_call(
        body, name="pool_fwd", grid=(b,),
        in_specs=[pl.BlockSpec((1, lp, POOL_WIDTH), lambda i: (i, 0, 0)), _vmem_spec(), _vmem_spec()],
        out_specs=pl.BlockSpec((1, lp, POOL_WIDTH), lambda i: (i, 0, 0)),
        out_shape=jax.ShapeDtypeStruct((b, lp, POOL_WIDTH), BF16),
        compiler_params=_params(("parallel",)),
    )(z3, pool_w, pool_scale)


def _qkv_fwd(z, g_q, g_kv, w_uq_p, w_kv_p, rope_c, rope_sa, rope_sb):
    t = z.shape[0]
    hw = w_uq_p.shape[1]
    heads = hw // HEAD_PAD
    tm = _token_tile(t, 512)

    def body(cq_ref, ckv_ref, kr_ref, gq_ref, gkv_ref, wq_ref, wkv_ref, c_ref, sa_ref, sb_ref, q_ref, k_ref, v_ref):
        c, sa, sb = c_ref[...], sa_ref[...], sb_ref[...]
        cqn, _, _ = _rms_fwd(cq_ref[...].astype(F32), gq_ref[...])
        qraw = _dot(cqn.astype(BF16), wq_ref[...])
        ckvn, _, _ = _rms_fwd(ckv_ref[...].astype(F32), gkv_ref[...])
        kvraw = _dot(ckvn.astype(BF16), wkv_ref[...])
        kr = kr_ref[...].astype(F32)
        for hd in range(heads):
            cols = slice(hd * HEAD_PAD, (hd + 1) * HEAD_PAD)
            q_ref[:, cols] = (_rope_fwd(qraw[:, cols], c, sa, sb) * SM_SCALE).astype(BF16)
            k_ref[:, cols] = _rope_fwd(kvraw[:, cols] + kr, c, sa, sb).astype(BF16)
        v_ref[...] = kvraw[:, hw:].astype(BF16)

    out = jax.ShapeDtypeStruct((t, hw), BF16)
    return pl.pallas_call(
        body, name="qkv_fwd", grid=(t // tm,),
        in_specs=[_row_block(tm, Q_RANK, POOL_WIDTH // Q_RANK),
                  _row_block(tm, KV_RANK, (POOL_WIDTH + Q_RANK) // KV_RANK),
                  _row_block(tm, HEAD_PAD, (POOL_WIDTH + Q_RANK + KV_RANK) // HEAD_PAD),
                  _vmem_spec(), _vmem_spec(), _vmem_spec(), _vmem_spec(),
                  _row_block(tm, HEAD_PAD), _row_block(tm, HEAD_PAD), _row_block(tm, HEAD_PAD)],
        out_specs=[_row_block(tm, hw)] * 3,
        out_shape=[out, out, out],
        compiler_params=_params(("parallel",)),
    )(z, z, z, g_q, g_kv, w_uq_p, w_kv_p, rope_c, rope_sa, rope_sb)


def _heads_per_step(heads, want):
    while heads % want:
        want //= 2
    return want


def _causal_mask(rows):
    row = lax.broadcasted_iota(jnp.int32, (rows, rows), 0)
    col = lax.broadcasted_iota(jnp.int32, (rows, rows), 1)
    return col <= row


def _attn_blocks(real_len):
    tail_start = (-(-real_len // ATT_BLOCK) - 1) * ATT_BLOCK
    return tail_start // ATT_BLOCK, tail_start, _round_up(real_len - tail_start, 32)


def _call_with_exchange(body, exchange, *, name, grid, in_specs, out_specs, out_shape, scratch_shapes, operands):
    if exchange is None:
        return pl.pallas_call(body, name=name, grid=grid, in_specs=in_specs, out_specs=out_specs, out_shape=out_shape,
                              scratch_shapes=scratch_shapes,
                              compiler_params=_params(("parallel",) + ("arbitrary",) * (len(grid) - 1)))(*operands)
    n_in, n_out, n_scratch = len(in_specs), len(out_specs), len(scratch_shapes)

    def riding(*refs):
        ins, src = refs[:n_in], refs[n_in]
        outs, dst = refs[n_in + 1:n_in + 1 + n_out], refs[n_in + 1 + n_out]
        scratch = refs[n_in + 2 + n_out:n_in + 2 + n_out + n_scratch]
        send_sems, recv_sems = refs[n_in + 2 + n_out + n_scratch:]
        steps = [pl.program_id(a) for a in range(len(grid))]

        @pl.when(functools.reduce(jnp.logical_and, [s == 0 for s in steps]))
        def _():
            exchange["start"](src, dst, send_sems, recv_sems)

        body(*ins, *outs, *scratch)

        @pl.when(functools.reduce(jnp.logical_and, [s == g - 1 for s, g in zip(steps, grid)]))
        def _():
            exchange["finish"](src, dst, send_sems, recv_sems)

    n = exchange["copies"]
    return pl.pallas_call(
        riding, name=name + "_" + exchange["name"], grid=grid,
        in_specs=list(in_specs) + [_any_spec()], out_specs=list(out_specs) + [_any_spec()],
        out_shape=list(out_shape) + [exchange["out_shape"]],
        scratch_shapes=list(scratch_shapes) + [pltpu.SemaphoreType.DMA((n,)), pltpu.SemaphoreType.DMA((n,))],
        compiler_params=_params(("arbitrary",) * len(grid)),
    )(*operands, exchange["operand"])


def _attn_fwd(q3, k3, v3, real_len, exchange=None):
    b, lp, hw = q3.shape
    heads = hw // HEAD_PAD
    tb = ATT_BLOCK
    nfull, tail_start, tail = _attn_blocks(real_len)
    done = tail_start + tail
    hpg = _heads_per_step(heads, ATT_FWD_HEADS)
    width = hpg * HEAD_PAD

    def body(q_ref, k_ref, v_ref, o_ref, lse_ref):
        group = pl.program_id(1)

        @pl.when(group == 0)
        def _():
            lse_ref[...] = jnp.zeros_like(lse_ref)

        def q_rows(r0, rows, whole_kv_blocks):
            def kv_step(c0, keys, states, mask):
                out = []
                for hd, (m, l, acc) in enumerate(states):
                    cols = slice(hd * HEAD_PAD, (hd + 1) * HEAD_PAD)
                    s = _dot_nt(q_ref[0, pl.ds(r0, rows), cols], k_ref[0, pl.ds(c0, keys), cols])
                    if mask is not None:
                        s = jnp.where(mask, s, MASK_VALUE)
                    m_new = jnp.maximum(m, jnp.max(s, axis=-1, keepdims=True))
                    alpha = jnp.exp(m - m_new)
                    p = jnp.exp(s - m_new)
                    l = alpha * l + jnp.sum(p, axis=-1, keepdims=True)
                    acc = alpha * acc + _dot(p.astype(BF16), v_ref[0, pl.ds(c0, keys), cols])
                    out.append((m_new, l, acc))
                return tuple(out)

            init = tuple((jnp.full((rows, 1), MASK_VALUE, F32), jnp.zeros((rows, 1), F32), jnp.zeros((rows, HEAD_PAD), F32))
                         for _ in range(hpg))
            states = lax.fori_loop(0, whole_kv_blocks, lambda j, st: kv_step(pl.multiple_of(j * tb, tb), tb, st, None), init)
            states = kv_step(r0, rows, states, _causal_mask(rows))
            lane = lax.broadcasted_iota(jnp.int32, (rows, HEAD_PAD), 1)
            lse_rows = lse_ref[0, pl.ds(r0, rows), :]
            for hd, (m, l, acc) in enumerate(states):
                o_ref[0, pl.ds(r0, rows), hd * HEAD_PAD:(hd + 1) * HEAD_PAD] = (acc / l).astype(BF16)
                lse_rows = jnp.where(lane == group * hpg + hd, m + jnp.log(l), lse_rows)
            lse_ref[0, pl.ds(r0, rows), :] = lse_rows

        def whole_block(i, carry):
            q_rows(pl.multiple_of(i * tb, tb), tb, i)
            return carry

        lax.fori_loop(0, nfull, whole_block, 0)
        q_rows(tail_start, tail, nfull)
        if done < lp:
            o_ref[0, done:lp, :] = jnp.zeros((lp - done, width), BF16)

    head_spec = pl.BlockSpec((1, lp, width), lambda bi, hi: (bi, 0, hi))
    return _call_with_exchange(
        body, exchange, name="attn_fwd", grid=(b, heads // hpg),
        in_specs=[head_spec, head_spec, head_spec],
        out_specs=[head_spec, pl.BlockSpec((1, lp, HEAD_PAD), lambda bi, hi: (bi, 0, 0))],
        out_shape=[jax.ShapeDtypeStruct((b, lp, hw), BF16), jax.ShapeDtypeStruct((b, lp, HEAD_PAD), F32)],
        scratch_shapes=[], operands=(q3, k3, v3))


def _merge_fwd(h, z, a, o, w_pa, w_pb_p, w_o):
    t, d = h.shape
    hw = o.shape[1]
    tm = _token_tile(t, 512)
    gate_block = Z_FIXED // d

    def body(h_ref, ga_ref, gb_ref, a_ref, o_ref, wpa_ref, wpb_ref, wo_ref, h1_ref, pa_ref, pb_ref):
        pa = _dot(a_ref[...], wpa_ref[...])
        pb = _dot(o_ref[...], wpb_ref[...])
        merged = jax.nn.sigmoid(ga_ref[...].astype(F32)) * pa + jax.nn.sigmoid(gb_ref[...].astype(F32)) * pb
        h1_ref[...] = h_ref[...] + _dot(merged.astype(BF16), wo_ref[...])
        pa_ref[...] = pa.astype(BF16)
        pb_ref[...] = pb.astype(BF16)

    return pl.pallas_call(
        body, name="merge_fwd", grid=(t // tm,),
        in_specs=[_row_block(tm, d), _row_block(tm, d, gate_block), _row_block(tm, d, gate_block + 1),
                  _row_block(tm, POOL_WIDTH), _row_block(tm, hw), _vmem_spec(), _vmem_spec(), _vmem_spec()],
        out_specs=[_row_block(tm, d)] * 3,
        out_shape=[jax.ShapeDtypeStruct((t, d), F32), jax.ShapeDtypeStruct((t, d), BF16), jax.ShapeDtypeStruct((t, d), BF16)],
        compiler_params=_params(("parallel",)),
    )(h, z, z, a, o, w_pa, w_pb_p, w_o)


def _ffn_fwd(h, g, w_gate, w_up, w_down):
    t, d = h.shape
    f = w_gate.shape[1]
    tm = _token_tile(t, 256)

    def body(h_ref, g_ref, wg_ref, wu_ref, wd_ref, h2_ref, a_ref, b_ref):
        x = h_ref[...]
        hn, _, _ = _rms_fwd(x, g_ref[...])
        hn = hn.astype(BF16)
        ga = _dot(hn, wg_ref[...])
        up = _dot(hn, wu_ref[...])
        act = ga * jax.nn.sigmoid(ga) * up
        h2_ref[...] = x + _dot(act.astype(BF16), wd_ref[...])
        a_ref[...] = ga.astype(BF16)
        b_ref[...] = up.astype(BF16)

    return pl.pallas_call(
        body, name="ffn_fwd", grid=(t // tm,),
        in_specs=[_row_block(tm, d), _vmem_spec(), _vmem_spec(), _vmem_spec(), _vmem_spec()],
        out_specs=[_row_block(tm, d), _row_block(tm, f), _row_block(tm, f)],
        out_shape=[jax.ShapeDtypeStruct((t, d), F32), jax.ShapeDtypeStruct((t, f), BF16), jax.ShapeDtypeStruct((t, f), BF16)],
        compiler_params=_params(("parallel",)),
    )(h, g, w_gate, w_up, w_down)


def _loss_head(h, g, target, valid):
    t, d = h.shape
    tm = _token_tile(t, 512)

    def body(h_ref, g_ref, t_ref, valid_ref, dh_ref, loss_ref, dg_ref):
        @pl.when(pl.program_id(0) == 0)
        def _():
            loss_ref[...] = jnp.zeros_like(loss_ref)
            dg_ref[...] = jnp.zeros_like(dg_ref)

        gain = g_ref[...]
        y, xh, r = _rms_fwd(h_ref[...], gain)
        err = (y - t_ref[...]) * valid_ref[...]
        per_row = jnp.sum(err * err, axis=-1, keepdims=True) / d
        loss_ref[...] += 0.5 * jnp.sum(per_row, axis=0, keepdims=True)
        dx, dg_rows = _rms_bwd(err / d, xh, r, gain)
        dh_ref[...] = dx
        dg_ref[...] += jnp.sum(dg_rows, axis=0, keepdims=True)

    return pl.pallas_call(
        body, name="loss_head", grid=(t // tm,),
        in_specs=[_row_block(tm, d), _vmem_spec(), _row_block(tm, d), _row_block(tm, 1)],
        out_specs=[_row_block(tm, d), pl.BlockSpec((1, 1), lambda i: (0, 0)), pl.BlockSpec((1, d), lambda i: (0, 0))],
        out_shape=[jax.ShapeDtypeStruct((t, d), F32), jax.ShapeDtypeStruct((1, 1), F32), jax.ShapeDtypeStruct((1, d), F32)],
        compiler_params=_params(("arbitrary",)),
    )(h, g, target, valid)


def _weight_grad(x, y, name):
    t, k = x.shape
    n = y.shape[1]
    tm = _token_tile(t, 512)
    tn = n
    while k * tn * 4 > 8 * 1024 * 1024 and tn % 256 == 0:
        tn //= 2
    steps = t // tm

    def body(x_ref, y_ref, o_ref, acc):
        @pl.when(pl.program_id(1) == 0)
        def _():
            acc[...] = jnp.zeros_like(acc)

        acc[...] += _dot_tn(x_ref[...].astype(BF16), y_ref[...].astype(BF16))

        @pl.when(pl.program_id(1) == steps - 1)
        def _():
            o_ref[...] = acc[...].astype(BF16)

    return pl.pallas_call(
        body, name=name, grid=(n // tn, steps),
        in_specs=[pl.BlockSpec((tm, k), lambda j, i: (i, 0)), pl.BlockSpec((tm, tn), lambda j, i: (i, j))],
        out_specs=pl.BlockSpec((k, tn), lambda j, i: (0, j)),
        out_shape=jax.ShapeDtypeStruct((k, n), BF16),
        scratch_shapes=[pltpu.VMEM((k, tn), F32)],
        compiler_params=_params(("parallel", "arbitrary")),
    )(x, y)


def _ffn_bwd(h, dh2, a, b, g, w_gate, w_up, w_down):
    t, d = h.shape
    f = a.shape[1]
    tm = _token_tile(t, 256)

    def body(h_ref, dh2_ref, a_ref, b_ref, g_ref, wg_ref, wu_ref, wd_ref, dh_ref, hn_ref, act_ref, da_ref, db_ref, dg_ref):
        @pl.when(pl.program_id(0) == 0)
        def _():
            dg_ref[...] = jnp.zeros_like(dg_ref)

        gain = g_ref[...]
        hn, xh, r = _rms_fwd(h_ref[...], gain)
        hn_ref[...] = hn.astype(BF16)
        dh2 = dh2_ref[...]
        dact = _dot_nt(dh2.astype(BF16), wd_ref[...])
        ga = a_ref[...].astype(F32)
        up = b_ref[...].astype(F32)
        sg = jax.nn.sigmoid(ga)
        silu = ga * sg
        act_ref[...] = (silu * up).astype(BF16)
        da = (dact * up * (sg * (1.0 + ga * (1.0 - sg)))).astype(BF16)
        db = (dact * silu).astype(BF16)
        da_ref[...] = da
        db_ref[...] = db
        dhn = _dot_nt(da, wg_ref[...]) + _dot_nt(db, wu_ref[...])
        dx, dg_rows = _rms_bwd(dhn, xh, r, gain)
        dh_ref[...] = dh2 + dx
        dg_ref[...] += jnp.sum(dg_rows, axis=0, keepdims=True)

    return pl.pallas_call(
        body, name="ffn_bwd", grid=(t // tm,),
        in_specs=[_row_block(tm, d), _row_block(tm, d), _row_block(tm, f), _row_block(tm, f),
                  _vmem_spec(), _vmem_spec(), _vmem_spec(), _vmem_spec()],
        out_specs=[_row_block(tm, d), _row_block(tm, d), _row_block(tm, f), _row_block(tm, f), _row_block(tm, f),
                   pl.BlockSpec((1, d), lambda i: (0, 0))],
        out_shape=[jax.ShapeDtypeStruct((t, d), F32), jax.ShapeDtypeStruct((t, d), BF16), jax.ShapeDtypeStruct((t, f), BF16),
                   jax.ShapeDtypeStruct((t, f), BF16), jax.ShapeDtypeStruct((t, f), BF16), jax.ShapeDtypeStruct((1, d), F32)],
        compiler_params=_params(("arbitrary",)),
    )(h, dh2, a, b, g, w_gate, w_up, w_down)


def _merge_bwd(dh1, z, pa, pb, w_o, w_pa, w_pb_p):
    t, d = dh1.shape
    hw = w_pb_p.shape[0]
    tm = _token_tile(t, 512)
    gate_block = Z_FIXED // d

    def body(dh_ref, ga_ref, gb_ref, pa_ref, pb_ref, wo_ref, wpa_ref, wpb_ref,
             mg_ref, dpa_ref, dpb_ref, dga_ref, dgb_ref, da_ref, do_ref):
        dm = _dot_nt(dh_ref[...].astype(BF16), wo_ref[...])
        sa = jax.nn.sigmoid(ga_ref[...].astype(F32))
        sb = jax.nn.sigmoid(gb_ref[...].astype(F32))
        pa = pa_ref[...].astype(F32)
        pb = pb_ref[...].astype(F32)
        mg_ref[...] = (sa * pa + sb * pb).astype(BF16)
        dpa = (dm * sa).astype(BF16)
        dpb = (dm * sb).astype(BF16)
        dpa_ref[...] = dpa
        dpb_ref[...] = dpb
        dga_ref[...] = (dm * pa * (sa * (1.0 - sa))).astype(BF16)
        dgb_ref[...] = (dm * pb * (sb * (1.0 - sb))).astype(BF16)
        da_ref[...] = _dot_nt(dpa, wpa_ref[...]).astype(BF16)
        do_ref[...] = _dot_nt(dpb, wpb_ref[...]).astype(BF16)

    wide = jax.ShapeDtypeStruct((t, d), BF16)
    return pl.pallas_call(
        body, name="merge_bwd", grid=(t // tm,),
        in_specs=[_row_block(tm, d), _row_block(tm, d, gate_block), _row_block(tm, d, gate_block + 1),
                  _row_block(tm, d), _row_block(tm, d), _vmem_spec(), _vmem_spec(), _vmem_spec()],
        out_specs=[_row_block(tm, d)] * 5 + [_row_block(tm, POOL_WIDTH), _row_block(tm, hw)],
        out_shape=[wide] * 5 + [jax.ShapeDtypeStruct((t, POOL_WIDTH), BF16), jax.ShapeDtypeStruct((t, hw), BF16)],
        compiler_params=_params(("parallel",)),
    )(dh1, z, z, pa, pb, w_o, w_pa, w_pb_p)


def _attn_bwd(q3, k3, v3, o3, do3, lse3, real_len, exchange=None):
    b, lp, hw = q3.shape
    heads = hw // HEAD_PAD
    tb = ATT_BLOCK
    nfull, tail_start, tail = _attn_blocks(real_len)
    done = tail_start + tail
    hpg = _heads_per_step(heads, ATT_BWD_HEADS)
    width = hpg * HEAD_PAD

    def body(q_ref, k_ref, v_ref, o_ref, do_ref, lse_ref, dq_ref, dk_ref, dv_ref, dq_acc, lse_col, delta_col):
        group = pl.program_id(1)
        lane = lax.broadcasted_iota(jnp.int32, (lp, HEAD_PAD), 1)
        for hd in range(hpg):
            cols = slice(hd * HEAD_PAD, (hd + 1) * HEAD_PAD)
            lse_col[hd] = jnp.sum(jnp.where(lane == group * hpg + hd, lse_ref[0], 0.0), axis=-1, keepdims=True)
            delta_col[hd] = jnp.sum(do_ref[0, :, cols].astype(F32) * o_ref[0, :, cols].astype(F32), axis=-1, keepdims=True)
        dq_acc[...] = jnp.zeros_like(dq_acc)

        def kv_rows(c0, keys, whole_q_blocks_from):
            def q_step(r0, rows, states, mask):
                out = []
                for hd, (dk, dv) in enumerate(states):
                    cols = slice(hd * HEAD_PAD, (hd + 1) * HEAD_PAD)
                    k = k_ref[0, pl.ds(c0, keys), cols]
                    q = q_ref[0, pl.ds(r0, rows), cols]
                    do = do_ref[0, pl.ds(r0, rows), cols]
                    s = _dot_nt(q, k)
                    if mask is not None:
                        s = jnp.where(mask, s, MASK_VALUE)
                    p = jnp.exp(s - lse_col[hd, pl.ds(r0, rows), :])
                    dp = _dot_nt(do, v_ref[0, pl.ds(c0, keys), cols])
                    ds = (p * (dp - delta_col[hd, pl.ds(r0, rows), :])).astype(BF16)
                    dv = dv + _dot_tn(p.astype(BF16), do)
                    dk = dk + _dot_tn(ds, q)
                    dq_acc[pl.ds(r0, rows), cols] += _dot(ds, k)
                    out.append((dk, dv))
                return tuple(out)

            zero = jnp.zeros((keys, HEAD_PAD), F32)
            states = q_step(c0, keys, tuple((zero, zero) for _ in range(hpg)), _causal_mask(keys))
            if whole_q_blocks_from is not None:
                states = lax.fori_loop(whole_q_blocks_from, nfull,
                                       lambda i, st: q_step(pl.multiple_of(i * tb, tb), tb, st, None), states)
                states = q_step(tail_start, tail, states, None)
            for hd, (dk, dv) in enumerate(states):
                cols = slice(hd * HEAD_PAD, (hd + 1) * HEAD_PAD)
                dk_ref[0, pl.ds(c0, keys), cols] = dk.astype(BF16)
                dv_ref[0, pl.ds(c0, keys), cols] = dv.astype(BF16)

        def whole_block(j, carry):
            kv_rows(pl.multiple_of(j * tb, tb), tb, j + 1)
            return carry

        lax.fori_loop(0, nfull, whole_block, 0)
        kv_rows(tail_start, tail, None)
        if done < lp:
            dk_ref[0, done:lp, :] = jnp.zeros((lp - done, width), BF16)
            dv_ref[0, done:lp, :] = jnp.zeros((lp - done, width), BF16)
        dq_ref[0] = dq_acc[...].astype(BF16)

    head_spec = pl.BlockSpec((1, lp, width), lambda bi, hi: (bi, 0, hi))
    out = jax.ShapeDtypeStruct((b, lp, hw), BF16)
    return _call_with_exchange(
        body, exchange, name="attn_bwd", grid=(b, heads // hpg),
        in_specs=[head_spec] * 5 + [pl.BlockSpec((1, lp, HEAD_PAD), lambda bi, hi: (bi, 0, 0))],
        out_specs=[head_spec] * 3,
        out_shape=[out, out, out],
        scratch_shapes=[pltpu.VMEM((lp, width), F32), pltpu.VMEM((hpg, lp, 1), F32), pltpu.VMEM((hpg, lp, 1), F32)],
        operands=(q3, k3, v3, o3, do3, lse3))


def _qkv_bwd(dq, dk, dv, z, g_q, g_kv, w_uq_p, w_kv_p, rope_c, rope_sa, rope_sb):
    t, hw = dq.shape
    heads = hw // HEAD_PAD
    tm = _token_tile(t, 512)

    def body(dq_ref, dk_ref, dv_ref, cq_ref, ckv_ref, gq_ref, gkv_ref, wq_ref, wkv_ref, c_ref, sa_ref, sb_ref,
             dqraw_ref, dkvraw_ref, cqn_ref, ckvn_ref, dcq_ref, dckv_ref, dkr_ref, dgq_ref, dgkv_ref):
        @pl.when(pl.program_id(0) == 0)
        def _():
            dgq_ref[...] = jnp.zeros_like(dgq_ref)
            dgkv_ref[...] = jnp.zeros_like(dgkv_ref)

        c, sa, sb = c_ref[...], sa_ref[...], sb_ref[...]
        dkr = jnp.zeros((tm, HEAD_PAD), F32)
        for hd in range(heads):
            cols = slice(hd * HEAD_PAD, (hd + 1) * HEAD_PAD)
            dqraw_ref[:, cols] = _rope_bwd(dq_ref[:, cols].astype(F32) * SM_SCALE, c, sa, sb).astype(BF16)
            dkraw = _rope_bwd(dk_ref[:, cols].astype(F32), c, sa, sb)
            dkvraw_ref[:, cols] = dkraw.astype(BF16)
            dkr = dkr + dkraw
        dkvraw_ref[:, hw:] = dv_ref[...]
        lane = lax.broadcasted_iota(jnp.int32, (tm, HEAD_PAD), 1)
        dkr_ref[...] = jnp.where((lane >= QK_NOPE) & (lane < QK_DIM), dkr, 0.0).astype(BF16)

        gq = gq_ref[...]
        cqn, xh, r = _rms_fwd(cq_ref[...].astype(F32), gq)
        cqn_ref[...] = cqn.astype(BF16)
        dx, dg_rows = _rms_bwd(_dot_nt(dqraw_ref[...], wq_ref[...]), xh, r, gq)
        dcq_ref[...] = dx.astype(BF16)
        dgq_ref[...] += jnp.sum(dg_rows, axis=0, keepdims=True)

        gkv = gkv_ref[...]
        ckvn, xh, r = _rms_fwd(ckv_ref[...].astype(F32), gkv)
        ckvn_ref[...] = ckvn.astype(BF16)
        dx, dg_rows = _rms_bwd(_dot_nt(dkvraw_ref[...], wkv_ref[...]), xh, r, gkv)
        dckv_ref[...] = dx.astype(BF16)
        dgkv_ref[...] += jnp.sum(dg_rows, axis=0, keepdims=True)

    def shape(width, dtype=BF16):
        return jax.ShapeDtypeStruct((t, width), dtype)

    return pl.pallas_call(
        body, name="qkv_bwd", grid=(t // tm,),
        in_specs=[_row_block(tm, hw)] * 3
        + [_row_block(tm, Q_RANK, POOL_WIDTH // Q_RANK), _row_block(tm, KV_RANK, (POOL_WIDTH + Q_RANK) // KV_RANK)]
        + [_vmem_spec()] * 4 + [_row_block(tm, HEAD_PAD)] * 3,
        out_specs=[_row_block(tm, hw), _row_block(tm, 2 * hw), _row_block(tm, Q_RANK), _row_block(tm, KV_RANK),
                   _row_block(tm, Q_RANK), _row_block(tm, KV_RANK), _row_block(tm, HEAD_PAD),
                   pl.BlockSpec((1, Q_RANK), lambda i: (0, 0)), pl.BlockSpec((1, KV_RANK), lambda i: (0, 0))],
        out_shape=[shape(hw), shape(2 * hw), shape(Q_RANK), shape(KV_RANK), shape(Q_RANK), shape(KV_RANK), shape(HEAD_PAD),
                   jax.ShapeDtypeStruct((1, Q_RANK), F32), jax.ShapeDtypeStruct((1, KV_RANK), F32)],
        compiler_params=_params(("arbitrary",)),
    )(dq, dk, dv, z, z, g_q, g_kv, w_uq_p, w_kv_p, rope_c, rope_sa, rope_sb)


def _pool_bwd(z3, da3, pool_w, pool_scale):
    b, lp, _ = z3.shape
    groups = len(POOL_WINDOWS)

    def body(u_ref, da_ref, pw_ref, sc_ref, du_ref, dpw_ref, dsc_ref):
        @pl.when(pl.program_id(0) == 0)
        def _():
            dpw_ref[...] = jnp.zeros_like(dpw_ref)
            dsc_ref[...] = jnp.zeros_like(dsc_ref)

        row = lax.broadcasted_iota(jnp.int32, (lp, POOL_GROUP), 0)
        pos = row.astype(F32)
        for gi, w in enumerate(POOL_WINDOWS):
            cols = slice(gi * POOL_GROUP, (gi + 1) * POOL_GROUP)
            count = jnp.minimum(pos + 1.0, float(w))
            u = u_ref[0, :, cols].astype(F32)
            y = (_window_sum(u, w, row, True) / count - u).astype(BF16)
            yw = _dot(y, pw_ref[gi])
            da = da_ref[0, :, cols].astype(F32)
            dsc_ref[:, cols] += jnp.sum(da * yw, axis=0, keepdims=True)
            dyw = (da * sc_ref[:, cols]).astype(BF16)
            dpw_ref[gi] += _dot_tn(y, dyw)
            dy = _dot_nt(dyw, pw_ref[gi])
            du_ref[0, :, cols] = (_window_sum(dy / count, w, row, False) - dy).astype(BF16)

    return pl.pallas_call(
        body, name="pool_bwd", grid=(b,),
        in_specs=[pl.BlockSpec((1, lp, POOL_WIDTH), lambda i: (i, 0, 0)), pl.BlockSpec((1, lp, POOL_WIDTH), lambda i: (i, 0, 0)),
                  _vmem_spec(), _vmem_spec()],
        out_specs=[pl.BlockSpec((1, lp, POOL_WIDTH), lambda i: (i, 0, 0)),
                   pl.BlockSpec((groups, POOL_GROUP, POOL_GROUP), lambda i: (0, 0, 0)),
                   pl.BlockSpec((1, POOL_WIDTH), lambda i: (0, 0))],
        out_shape=[jax.ShapeDtypeStruct((b, lp, POOL_WIDTH), BF16), jax.ShapeDtypeStruct((groups, POOL_GROUP, POOL_GROUP), F32),
                   jax.ShapeDtypeStruct((1, POOL_WIDTH), F32)],
        compiler_params=_params(("arbitrary",)),
    )(z3, da3, pool_w, pool_scale)


def _in_proj_bwd(h, dh1, du, dcq, dckv, dkr, dga, dgb, g, w_in_p):
    t, d = h.shape
    nz = w_in_p.shape[1]
    tm = _token_tile(t, 512)
    widths = (POOL_WIDTH, Q_RANK, KV_RANK, HEAD_PAD, d, d)

    def body(h_ref, dh1_ref, du_ref, dcq_ref, dckv_ref, dkr_ref, dga_ref, dgb_ref, g_ref, w_ref, dh_ref, hn_ref, dz_ref, dg_ref):
        @pl.when(pl.program_id(0) == 0)
        def _():
            dg_ref[...] = jnp.zeros_like(dg_ref)

        gain = g_ref[...]
        hn, xh, r = _rms_fwd(h_ref[...], gain)
        hn_ref[...] = hn.astype(BF16)
        dhn = jnp.zeros((tm, d), F32)
        start = 0
        for piece, width in zip((du_ref, dcq_ref, dckv_ref, dkr_ref, dga_ref, dgb_ref), widths):
            val = piece[...]
            dz_ref[:, start:start + width] = val
            dhn = dhn + _dot_nt(val, w_ref[:, start:start + width])
            start += width
        dx, dg_rows = _rms_bwd(dhn, xh, r, gain)
        dh_ref[...] = dh1_ref[...] + dx
        dg_ref[...] += jnp.sum(dg_rows, axis=0, keepdims=True)

    return pl.pallas_call(
        body, name="in_proj_bwd", grid=(t // tm,),
        in_specs=[_row_block(tm, d), _row_block(tm, d)] + [_row_block(tm, w) for w in widths] + [_vmem_spec(), _vmem_spec()],
        out_specs=[_row_block(tm, d), _row_block(tm, d), _row_block(tm, nz), pl.BlockSpec((1, d), lambda i: (0, 0))],
        out_shape=[jax.ShapeDtypeStruct((t, d), F32), jax.ShapeDtypeStruct((t, d), BF16), jax.ShapeDtypeStruct((t, nz), BF16),
                   jax.ShapeDtypeStruct((1, d), F32)],
        compiler_params=_params(("arbitrary",)),
    )(h, dh1, du, dcq, dckv, dkr, dga, dgb, g, w_in_p)


def _pad_heads(w, heads, width):
    k = w.shape[0]
    w = w.reshape(k, heads, width)
    return jnp.pad(w, ((0, 0), (0, 0), (0, HEAD_PAD - width))).reshape(k, heads * HEAD_PAD)


def _unpad_heads(w, heads, width):
    k = w.shape[0]
    return w.reshape(k, heads, HEAD_PAD)[:, :, :width].reshape(k, heads * width)


def _layer_layouts(w, heads):
    d = w["w_in"].shape[0]
    o1, o2, o3, o4 = POOL_WIDTH, POOL_WIDTH + Q_RANK, POOL_WIDTH + Q_RANK + KV_RANK, POOL_WIDTH + Q_RANK + KV_RANK + QK_ROPE
    w_in = w["w_in"]
    rope_cols = jnp.pad(w_in[:, o3:o4], ((0, 0), (QK_NOPE, HEAD_PAD - QK_DIM)))
    w_in_p = jnp.concatenate([w_in[:, :o3], rope_cols, w_in[:, o4:]], axis=1)
    w_uq_p = _pad_heads(w["w_uq"], heads, QK_DIM)
    kv = w["w_ukv"].reshape(KV_RANK, heads, QK_NOPE + V_DIM)
    w_k = jnp.pad(kv[:, :, :QK_NOPE], ((0, 0), (0, 0), (0, HEAD_PAD - QK_NOPE))).reshape(KV_RANK, heads * HEAD_PAD)
    w_v = jnp.pad(kv[:, :, QK_NOPE:], ((0, 0), (0, 0), (0, HEAD_PAD - V_DIM))).reshape(KV_RANK, heads * HEAD_PAD)
    w_kv_p = jnp.concatenate([w_k, w_v], axis=1)
    w_pb_p = jnp.pad(w["w_pb"].reshape(heads, V_DIM, d), ((0, 0), (0, HEAD_PAD - V_DIM), (0, 0))).reshape(heads * HEAD_PAD, d)
    return dict(w_in_p=w_in_p, w_uq_p=w_uq_p, w_kv_p=w_kv_p, w_pa=w["w_pa"], w_pb_p=w_pb_p, w_o=w["w_o"],
                w_gate=w["w_gate"], w_up=w["w_up"], w_down=w["w_down"])


def _layer_grad_layouts(g, heads, d):
    o3 = POOL_WIDTH + Q_RANK + KV_RANK
    gin = g["w_in_p"]
    w_in = jnp.concatenate([gin[:, :o3], gin[:, o3 + QK_NOPE:o3 + QK_DIM], gin[:, o3 + HEAD_PAD:]], axis=1)
    hw = heads * HEAD_PAD
    gk = g["w_kv_p"][:, :hw].reshape(KV_RANK, heads, HEAD_PAD)[:, :, :QK_NOPE]
    gv = g["w_kv_p"][:, hw:].reshape(KV_RANK, heads, HEAD_PAD)[:, :, :V_DIM]
    w_ukv = jnp.concatenate([gk, gv], axis=2).reshape(KV_RANK, heads * (QK_NOPE + V_DIM))
    w_pb = g["w_pb_p"].reshape(heads, HEAD_PAD, d)[:, :V_DIM].reshape(heads * V_DIM, d)
    return dict(w_in=w_in, w_uq=_unpad_heads(g["w_uq_p"], heads, QK_DIM), w_ukv=w_ukv, w_pa=g["w_pa"], w_pb=w_pb,
                w_o=g["w_o"], w_gate=g["w_gate"], w_up=g["w_up"], w_down=g["w_down"])


def _rope_tables(lp, b):
    inv = 1.0 / (ROPE_THETA ** (jnp.arange(0, QK_ROPE, 2, dtype=F32) / QK_ROPE))
    ang = jnp.arange(lp, dtype=F32)[:, None] * inv[None, :]
    cos, sin = jnp.cos(ang), jnp.sin(ang)
    half = QK_ROPE // 2
    ones = jnp.ones((lp, QK_NOPE), F32)
    zeros_lo = jnp.zeros((lp, QK_NOPE), F32)
    zeros_hi = jnp.zeros((lp, HEAD_PAD - QK_DIM), F32)
    zeros_half = jnp.zeros((lp, half), F32)
    c = jnp.concatenate([ones, cos, cos, zeros_hi], axis=1)
    sa = jnp.concatenate([zeros_lo, zeros_half, sin, zeros_hi], axis=1)
    sb = jnp.concatenate([zeros_lo, -sin, zeros_half, zeros_hi], axis=1)
    return tuple(jnp.tile(tab, (b, 1)) for tab in (c, sa, sb))


def _local_step(x, loss_target, meta_tokens, small, first_layer, packed_second, unpack, pack_grads):
    b, seq, d = x.shape
    depth = 2
    heads = first_layer["w_uq"].shape[1] // QK_DIM
    core = lax.axis_index("c")
    chip = 2 * lax.axis_index("x") + lax.axis_index("y")
    real_len = N_META + seq
    lp = _round_up(real_len, SEQ_PAD)
    t = b * lp
    pad = lp - N_META - seq

    meta = jnp.broadcast_to(meta_tokens[None], (b, N_META, d))
    h = jnp.concatenate([meta, x, jnp.zeros((b, pad, d), F32)], axis=1).reshape(t, d)
    target = jnp.pad(loss_target, ((0, 0), (N_META, pad), (0, 0))).reshape(t, d)
    pos = jnp.arange(lp)
    valid = jnp.tile(((pos >= N_META) & (pos < N_META + seq)).astype(F32), b).reshape(t, 1)
    rope_c, rope_sa, rope_sb = _rope_tables(lp, b)

    def layer_inputs(li, gathered):
        lay = _layer_layouts(gathered, heads)
        lay["pool_w"] = small["pool_w"][li].astype(BF16)
        lay["pool_scale"] = small["pool_scale"][li][None]
        for n in ("norm_mix_g", "q_norm_g", "kv_norm_g", "norm_ffn_g"):
            lay[n] = small[n][li][None]
        return lay

    layers, saved = [layer_inputs(0, first_layer)], []
    for li in range(depth):
        lay = layers[li]
        z = _in_proj_fwd(h, lay["norm_mix_g"], lay["w_in_p"])
        a = _pool_fwd(z.reshape(b, lp, -1), lay["pool_w"], lay["pool_scale"]).reshape(t, POOL_WIDTH)
        q, k, v = _qkv_fwd(z, lay["q_norm_g"], lay["kv_norm_g"], lay["w_uq_p"], lay["w_kv_p"], rope_c, rope_sa, rope_sb)
        hw = q.shape[1]
        qkv3 = (q.reshape(b, lp, hw), k.reshape(b, lp, hw), v.reshape(b, lp, hw))
        if li == 0:
            o3, lse, others = _attn_fwd(*qkv3, real_len, _gather_exchange(packed_second))
            layers.append(layer_inputs(1, unpack(lax.dynamic_update_index_in_dim(others, packed_second, chip, 0))))
        else:
            o3, lse = _attn_fwd(*qkv3, real_len)
        o = o3.reshape(t, hw)
        h1, pa, pb = _merge_fwd(h, z, a, o, lay["w_pa"], lay["w_pb_p"], lay["w_o"])
        h2, fa, fb = _ffn_fwd(h1, lay["norm_ffn_g"], lay["w_gate"], lay["w_up"], lay["w_down"])
        saved.append(dict(h=h, z=z, a=a, q=q, k=k, v=v, o=o, lse=lse, pa=pa, pb=pb, h1=h1, fa=fa, fb=fb))
        h = h2

    dh, loss, d_final = _loss_head(h, small["final_norm_g"][None], target, valid)

    g_small = {n: [] for n in SMALL_WEIGHTS if n != "final_norm_g"}
    packed_grads = {}
    for li in reversed(range(depth)):
        lay, sv = layers[li], saved[li]
        hw = sv["q"].shape[1]
        dh1, hn_f, act, dfa, dfb, dg_ffn = _ffn_bwd(sv["h1"], dh, sv["fa"], sv["fb"], lay["norm_ffn_g"],
                                                     lay["w_gate"], lay["w_up"], lay["w_down"])
        gl = dict(w_gate=_weight_grad(hn_f, dfa, "grad_w_gate"), w_up=_weight_grad(hn_f, dfb, "grad_w_up"),
                  w_down=_weight_grad(act, dh, "grad_w_down"))
        merged, dpa, dpb, dga, dgb, da, do = _merge_bwd(dh1, sv["z"], sv["pa"], sv["pb"], lay["w_o"], lay["w_pa"], lay["w_pb_p"])
        gl["w_o"] = _weight_grad(merged, dh1, "grad_w_o")
        gl["w_pa"] = _weight_grad(sv["a"], dpa, "grad_w_pa")
        gl["w_pb_p"] = _weight_grad(sv["o"], dpb, "grad_w_pb")
        shape3 = (b, lp, hw)
        attn_in = (sv["q"].reshape(shape3), sv["k"].reshape(shape3), sv["v"].reshape(shape3), sv["o"].reshape(shape3),
                   do.reshape(shape3), sv["lse"])
        if li == 0:
            second = packed_grads[1]
            dq3, dk3, dv3, from_others = _attn_bwd(*attn_in, real_len, _scatter_exchange(second))
            own = lax.dynamic_index_in_dim(lax.dynamic_index_in_dim(second, chip, 0, keepdims=False), core, 0, keepdims=False)
            second_parts = lax.dynamic_update_index_in_dim(from_others, own, 2 * chip + core, 0)
        else:
            dq3, dk3, dv3 = _attn_bwd(*attn_in, real_len)
        dqraw, dkvraw, cqn, ckvn, dcq, dckv, dkr, dg_q, dg_kv = _qkv_bwd(
            dq3.reshape(t, hw), dk3.reshape(t, hw), dv3.reshape(t, hw), sv["z"], lay["q_norm_g"], lay["kv_norm_g"],
            lay["w_uq_p"], lay["w_kv_p"], rope_c, rope_sa, rope_sb)
        gl["w_uq_p"] = _weight_grad(cqn, dqraw, "grad_w_uq")
        gl["w_kv_p"] = _weight_grad(ckvn, dkvraw, "grad_w_ukv")
        du3, dpool_w, dpool_scale = _pool_bwd(sv["z"].reshape(b, lp, -1), da.reshape(b, lp, POOL_WIDTH),
                                              lay["pool_w"], lay["pool_scale"])
        dh, hn_m, dz, dg_mix = _in_proj_bwd(sv["h"], dh1, du3.reshape(t, POOL_WIDTH), dcq, dckv, dkr, dga, dgb,
                                            lay["norm_mix_g"], lay["w_in_p"])
        gl["w_in_p"] = _weight_grad(hn_m, dz, "grad_w_in")
        packed_grads[li] = pack_grads(_layer_grad_layouts(gl, heads, d))
        for n, val in (("norm_mix_g", dg_mix[0]), ("pool_w", dpool_w), ("pool_scale", dpool_scale[0]), ("q_norm_g", dg_q[0]),
                       ("kv_norm_g", dg_kv[0]), ("norm_ffn_g", dg_ffn[0])):
            g_small[n].insert(0, val)

    dh3 = dh.reshape(b, lp, d)
    grad_x = dh3[:, N_META:N_META + seq]
    d_meta_rows = dh3[:, :N_META]
    g_small = {n: jnp.stack(v) for n, v in g_small.items()}
    g_small["final_norm_g"] = d_final[0]
    return loss, grad_x, d_meta_rows, g_small, packed_grads[0], second_parts


def _mesh_place():
    x, y, c = lax.axis_index("x"), lax.axis_index("y"), lax.axis_index("c")
    others = [(1 - x, y), (x, 1 - y), (1 - x, 1 - y)]
    return x, y, c, 2 * x + y, others


def _remote(src, dst, send_sems, recv_sems, k, device):
    return pltpu.make_async_remote_copy(src_ref=src, dst_ref=dst, send_sem=send_sems.at[k], recv_sem=recv_sems.at[k],
                                        device_id=device, device_id_type=MESH)


def _gather_exchange(packed):
    def copy(p_ref, g_ref, send_sems, recv_sems, r, slot):
        _, _, c, _, others = _mesh_place()
        ox, oy = others[r]
        return _remote(p_ref, g_ref.at[slot], send_sems, recv_sems, r, (ox, oy, c))

    def start(p_ref, g_ref, send_sems, recv_sems):
        chip = _mesh_place()[3]
        for r in range(3):
            copy(p_ref, g_ref, send_sems, recv_sems, r, chip).start()

    def finish(p_ref, g_ref, send_sems, recv_sems):
        _, _, _, chip, others = _mesh_place()
        for r, (ox, oy) in enumerate(others):
            copy(p_ref, g_ref, send_sems, recv_sems, r, 2 * ox + oy).wait_recv()
        for r in range(3):
            copy(p_ref, g_ref, send_sems, recv_sems, r, chip).wait_send()

    return dict(name="gather", operand=packed, copies=3, start=start, finish=finish,
                out_shape=jax.ShapeDtypeStruct((N_CHIPS,) + packed.shape, packed.dtype))


def _scatter_exchange(parts):
    flips = [(dx, dy, dc) for dx in (0, 1) for dy in (0, 1) for dc in (0, 1)][1:]

    def copy(p_ref, got_ref, send_sems, recv_sems, k, arriving):
        x, y, c, _, _ = _mesh_place()
        dx, dy, dc = flips[k]
        tx, ty, tc = (1 - x if dx else x), (1 - y if dy else y), (1 - c if dc else c)
        slot = 4 * tx + 2 * ty + tc if arriving else 4 * x + 2 * y + c
        return _remote(p_ref.at[2 * tx + ty, tc], got_ref.at[slot], send_sems, recv_sems, k, (tx, ty, tc))

    def start(p_ref, got_ref, send_sems, recv_sems):
        for k in range(len(flips)):
            copy(p_ref, got_ref, send_sems, recv_sems, k, False).start()

    def finish(p_ref, got_ref, send_sems, recv_sems):
        for k in range(len(flips)):
            copy(p_ref, got_ref, send_sems, recv_sems, k, True).wait_recv()
        for k in range(len(flips)):
            copy(p_ref, got_ref, send_sems, recv_sems, k, False).wait_send()

    return dict(name="scatter", operand=parts, copies=len(flips), start=start, finish=finish,
                out_shape=jax.ShapeDtypeStruct((2 * N_CHIPS,) + parts.shape[2:], parts.dtype))


def _all_gather_shards(packed, meta_shard):
    _, rh, cols = packed.shape

    def body(p_ref, m_ref, g_ref, gm_ref, send_sems, recv_sems):
        x, y, c, chip, others = _mesh_place()
        sibling = (x, y, 1 - c)
        sends = []
        for r, (ox, oy) in enumerate(others):
            sends.append(_remote(p_ref.at[c], g_ref.at[chip, c], send_sems, recv_sems, r, (ox, oy, c)))
            sends.append(_remote(m_ref, gm_ref.at[chip], send_sems, recv_sems, 6 + r, (ox, oy, c)))
        for cp in sends:
            cp.start()
        for r, (ox, oy) in enumerate(others):
            src_chip = 2 * ox + oy
            _remote(p_ref.at[c], g_ref.at[src_chip, c], send_sems, recv_sems, r, (ox, oy, c)).wait_recv()
            passed = _remote(g_ref.at[src_chip, c], g_ref.at[src_chip, c], send_sems, recv_sems, 3 + r, sibling)
            passed.start()
            sends.append(passed)
        for r, (ox, oy) in enumerate(others):
            src_chip = 2 * ox + oy
            _remote(p_ref.at[c], g_ref.at[src_chip, 1 - c], send_sems, recv_sems, 3 + r, sibling).wait_recv()
            _remote(m_ref, gm_ref.at[src_chip], send_sems, recv_sems, 6 + r, (ox, oy, c)).wait_recv()
        for cp in sends:
            cp.wait_send()

    gathered, meta_all = pl.pallas_call(
        body, name="all_gather_shards",
        in_specs=[_any_spec(), _any_spec()], out_specs=[_any_spec(), _any_spec()],
        out_shape=[jax.ShapeDtypeStruct((N_CHIPS, 2, rh, cols), packed.dtype),
                   jax.ShapeDtypeStruct((N_CHIPS,) + meta_shard.shape, meta_shard.dtype)],
        scratch_shapes=[pltpu.SemaphoreType.DMA((9,)), pltpu.SemaphoreType.DMA((9,))],
    )(packed, meta_shard)
    chip = 2 * lax.axis_index("x") + lax.axis_index("y")
    return (lax.dynamic_update_index_in_dim(gathered, packed, chip, 0),
            lax.dynamic_update_index_in_dim(meta_all, meta_shard, chip, 0))


def _pair_exchange(give):
    def body(give_ref, got_ref, send_sems, recv_sems):
        x, y, c, _, _ = _mesh_place()
        cp = _remote(give_ref, got_ref, send_sems, recv_sems, 0, (x, y, 1 - c))
        cp.start()
        cp.wait()

    return pl.pallas_call(
        body, name="pair_exchange", in_specs=[_any_spec()], out_specs=_any_spec(),
        out_shape=jax.ShapeDtypeStruct(give.shape, give.dtype),
        scratch_shapes=[pltpu.SemaphoreType.DMA((1,)), pltpu.SemaphoreType.DMA((1,))],
    )(give)


def _chip_exchange(parts):
    def body(p_ref, got_ref, send_sems, recv_sems):
        _, _, c, chip, others = _mesh_place()
        sends = [_remote(p_ref.at[2 * ox + oy], got_ref.at[chip], send_sems, recv_sems, r, (ox, oy, c))
                 for r, (ox, oy) in enumerate(others)]
        for cp in sends:
            cp.start()
        for r, (ox, oy) in enumerate(others):
            _remote(p_ref.at[chip], got_ref.at[2 * ox + oy], send_sems, recv_sems, r, (ox, oy, c)).wait_recv()
        for cp in sends:
            cp.wait_send()

    got = pl.pallas_call(
        body, name="chip_exchange", in_specs=[_any_spec()], out_specs=_any_spec(),
        out_shape=jax.ShapeDtypeStruct(parts.shape, parts.dtype),
        scratch_shapes=[pltpu.SemaphoreType.DMA((3,)), pltpu.SemaphoreType.DMA((3,))],
    )(parts)
    chip = 2 * lax.axis_index("x") + lax.axis_index("y")
    own = lax.dynamic_index_in_dim(parts, chip, 0, keepdims=False)
    return lax.dynamic_update_index_in_dim(got, own, chip, 0)


def _pair_gather(half):
    def body(h_ref, out_ref, send_sems, recv_sems):
        x, y, c, _, _ = _mesh_place()
        cp = _remote(h_ref, out_ref.at[c], send_sems, recv_sems, 0, (x, y, 1 - c))
        cp.start()
        _remote(h_ref, out_ref.at[1 - c], send_sems, recv_sems, 0, (x, y, 1 - c)).wait_recv()
        cp.wait_send()

    both = pl.pallas_call(
        body, name="pair_gather", in_specs=[_any_spec()], out_specs=_any_spec(),
        out_shape=jax.ShapeDtypeStruct((2,) + half.shape, half.dtype),
        scratch_shapes=[pltpu.SemaphoreType.DMA((1,)), pltpu.SemaphoreType.DMA((1,))],
    )(half)
    return lax.dynamic_update_index_in_dim(both, half, lax.axis_index("c"), 0)


def _row_tile(rows, limit=PACK_TILE):
    if rows <= limit:
        return rows
    for tr in range(limit, 7, -8):
        if rows % tr == 0:
            return tr
    return rows


def _pair_add(keep, got):
    n, rh, cols = keep.shape
    tr = _row_tile(rh)

    def body(k_ref, g_ref, o_ref):
        o_ref[...] = (k_ref[...].astype(F32) + g_ref[...].astype(F32)).astype(BF16)

    spec = pl.BlockSpec((1, tr, cols), lambda j, i: (j, i, 0))
    return pl.pallas_call(
        body, name="pair_add", grid=(n, rh // tr), in_specs=[spec, spec], out_specs=spec,
        out_shape=jax.ShapeDtypeStruct(keep.shape, BF16),
        compiler_params=_params(("parallel", "parallel")),
    )(keep, got)


def _chip_sum(parts):
    n, rh, cols = parts.shape
    tr = _row_tile(rh)

    def body(p_ref, o_ref):
        total = p_ref[0].astype(F32)
        for k in range(1, n):
            total = total + p_ref[k].astype(F32)
        o_ref[...] = total

    return pl.pallas_call(
        body, name="chip_sum", grid=(rh // tr,),
        in_specs=[pl.BlockSpec((n, tr, cols), lambda i: (0, i, 0))], out_specs=pl.BlockSpec((tr, cols), lambda i: (i, 0)),
        out_shape=jax.ShapeDtypeStruct((rh, cols), F32),
        compiler_params=_params(("parallel",)),
    )(parts)


def _reduce_scatter(grads, c):
    keep = lax.dynamic_index_in_dim(grads, c, axis=1, keepdims=False)
    give = lax.dynamic_index_in_dim(grads, 1 - c, axis=1, keepdims=False)
    chip_partial = _pair_add(keep, _pair_exchange(give))
    return _pair_gather(_chip_sum(_chip_exchange(chip_partial)))


def _all_reduce_small(meta_rows, small):
    b, rm, cols = meta_rows.shape
    rows = rm + small.shape[0]

    def body(meta_ref, small_ref, out_ref, mine, pair_buf, chip_buf, send_sems, recv_sems):
        x, y, c, chip, others = _mesh_place()
        acc = meta_ref[0]
        for i in range(1, b):
            acc = acc + meta_ref[i]
        mine[0:rm, :] = acc
        mine[rm:rows, :] = small_ref[...]
        pair = _remote(mine, pair_buf, send_sems, recv_sems, 0, (x, y, 1 - c))
        pair.start()
        pair.wait()
        chip_buf[chip] = mine[...] + pair_buf[...]
        sends = [_remote(chip_buf.at[chip], chip_buf.at[chip], send_sems, recv_sems, 1 + r, (ox, oy, c))
                 for r, (ox, oy) in enumerate(others)]
        for cp in sends:
            cp.start()
        for r, (ox, oy) in enumerate(others):
            _remote(chip_buf.at[chip], chip_buf.at[2 * ox + oy], send_sems, recv_sems, 1 + r, (ox, oy, c)).wait_recv()
        for cp in sends:
            cp.wait_send()
        out_ref[...] = ((chip_buf[0] + chip_buf[1]) + chip_buf[2]) + chip_buf[3]

    return pl.pallas_call(
        body, name="all_reduce_small",
        in_specs=[_vmem_spec(), _vmem_spec()], out_specs=_vmem_spec(),
        out_shape=jax.ShapeDtypeStruct((rows, cols), F32),
        scratch_shapes=[pltpu.VMEM((rows, cols), F32), pltpu.VMEM((rows, cols), F32), pltpu.VMEM((N_CHIPS, rows, cols), F32),
                        pltpu.SemaphoreType.DMA((4,)), pltpu.SemaphoreType.DMA((4,))],
        compiler_params=pltpu.CompilerParams(vmem_limit_bytes=VMEM_LIMIT),
    )(meta_rows, small)


def _adamw(w, g, m, v):
    shape = w.shape
    cols = shape[-1]
    rows = w.size // cols
    tr = _row_tile(rows)

    def body(w_ref, g_ref, m_ref, v_ref, d_ref, m2_ref, v2_ref):
        grad = g_ref[...]
        m2 = ADAM_B1 * m_ref[...] + (1.0 - ADAM_B1) * grad
        v2 = ADAM_B2 * v_ref[...] + (1.0 - ADAM_B2) * jnp.square(grad)
        m_hat = m2 / (1.0 - ADAM_B1 ** ADAM_STEP)
        v_hat = v2 / (1.0 - ADAM_B2 ** ADAM_STEP)
        d_ref[...] = -ADAM_LR * (m_hat / (jnp.sqrt(v_hat) + ADAM_EPS) + ADAM_WD * w_ref[...])
        m2_ref[...] = m2
        v2_ref[...] = v2

    spec = pl.BlockSpec((tr, cols), lambda i: (i, 0))
    out = jax.ShapeDtypeStruct((rows, cols), F32)
    res = pl.pallas_call(
        body, name="adamw", grid=(rows // tr,), in_specs=[spec] * 4, out_specs=[spec] * 3, out_shape=[out] * 3,
        compiler_params=_params(("parallel",)),
    )(*(a.reshape(rows, cols) for a in (w, g, m, v)))
    return tuple(r.reshape(shape) for r in res)


def _pack_rows(arrays):
    flat = [a.reshape(-1, PACK_COLS) for a in arrays]
    counts = [f.shape[0] for f in flat]
    total = sum(counts)
    half = -(-total // 2)
    tiles = -(-half // PACK_TILE)
    padded = 2 * tiles * _round_up(-(-half // tiles), 16)
    if padded > total:
        flat.append(jnp.zeros((padded - total, PACK_COLS), flat[0].dtype))
    return jnp.concatenate(flat, axis=0), counts


def _unpack_rows(buffer, counts, shapes):
    out, start = [], 0
    for n, shape in zip(counts, shapes):
        out.append(buffer[..., start:start + n, :].reshape(buffer.shape[:-2] + tuple(shape)))
        start += n
    return out


def kernel(x, meta_tokens, norm_mix_g, w_in, pool_w, pool_scale, q_norm_g, kv_norm_g, w_uq, w_ukv, w_pa, w_pb, w_o, norm_ffn_g, w_gate, w_up, w_down, final_norm_g, loss_target, m_meta_tokens, m_norm_mix_g, m_w_in, m_pool_w, m_pool_scale, m_q_norm_g, m_kv_norm_g, m_w_uq, m_w_ukv, m_w_pa, m_w_pb, m_w_o, m_norm_ffn_g, m_w_gate, m_w_up, m_w_down, m_final_norm_g, v_meta_tokens, v_norm_mix_g, v_w_in, v_pool_w, v_pool_scale, v_q_norm_g, v_kv_norm_g, v_w_uq, v_w_ukv, v_w_pa, v_w_pb, v_w_o, v_norm_ffn_g, v_w_gate, v_w_up, v_w_down, v_final_norm_g):
    weights = dict(meta_tokens=meta_tokens, norm_mix_g=norm_mix_g, w_in=w_in, pool_w=pool_w, pool_scale=pool_scale,
                   q_norm_g=q_norm_g, kv_norm_g=kv_norm_g, w_uq=w_uq, w_ukv=w_ukv, w_pa=w_pa, w_pb=w_pb, w_o=w_o,
                   norm_ffn_g=norm_ffn_g, w_gate=w_gate, w_up=w_up, w_down=w_down, final_norm_g=final_norm_g)
    first = dict(meta_tokens=m_meta_tokens, norm_mix_g=m_norm_mix_g, w_in=m_w_in, pool_w=m_pool_w, pool_scale=m_pool_scale,
                 q_norm_g=m_q_norm_g, kv_norm_g=m_kv_norm_g, w_uq=m_w_uq, w_ukv=m_w_ukv, w_pa=m_w_pa, w_pb=m_w_pb, w_o=m_w_o,
                 norm_ffn_g=m_norm_ffn_g, w_gate=m_w_gate, w_up=m_w_up, w_down=m_w_down, final_norm_g=m_final_norm_g)
    second = dict(meta_tokens=v_meta_tokens, norm_mix_g=v_norm_mix_g, w_in=v_w_in, pool_w=v_pool_w, pool_scale=v_pool_scale,
                  q_norm_g=v_q_norm_g, kv_norm_g=v_kv_norm_g, w_uq=v_w_uq, w_ukv=v_w_ukv, w_pa=v_w_pa, w_pb=v_w_pb, w_o=v_w_o,
                  norm_ffn_g=v_norm_ffn_g, w_gate=v_w_gate, w_up=v_w_up, w_down=v_w_down, final_norm_g=v_final_norm_g)
    core = lax.axis_index("c")
    chip = 2 * lax.axis_index("x") + lax.axis_index("y")
    d = x.shape[-1]
    meta_cols = meta_tokens.shape[1]

    shard_shapes = [weights[n].shape[1:] for n in BIG_WEIGHTS]
    packs = []
    for li in range(2):
        packed, counts = _pack_rows([weights[n][li].astype(BF16) for n in BIG_WEIGHTS])
        packs.append(packed)
    rows = packs[0].shape[0]
    pad_rows = rows - sum(counts)

    def unpack(per_chip_packed):
        per_chip = _unpack_rows(per_chip_packed, counts, shard_shapes)
        return {n: jnp.concatenate([per_chip[i][j] for j in range(N_CHIPS)], axis=SHARD_AXIS[n] - 1)
                for i, n in enumerate(BIG_WEIGHTS)}

    def pack_grads(layer_grads):
        pieces = []
        for j in range(N_CHIPS):
            for n in BIG_WEIGHTS:
                pieces.append(jnp.split(layer_grads[n], N_CHIPS, axis=SHARD_AXIS[n] - 1)[j].reshape(-1, PACK_COLS))
            if pad_rows:
                pieces.append(jnp.zeros((pad_rows, PACK_COLS), BF16))
        return jnp.concatenate(pieces, axis=0).reshape(N_CHIPS, 2, rows // 2, PACK_COLS)

    gathered, meta_all = _all_gather_shards(packs[0].reshape(2, rows // 2, PACK_COLS), meta_tokens)
    first_layer = unpack(gathered.reshape(N_CHIPS, rows, PACK_COLS))
    meta_full = jnp.concatenate([meta_all[j] for j in range(N_CHIPS)], axis=1)
    small = {n: weights[n] for n in SMALL_WEIGHTS}

    loss, grad_x, d_meta_rows, g_small, partial_first, parts_second = _local_step(
        x, loss_target, meta_full, small, first_layer, packs[1], unpack, pack_grads)

    reduced = [_reduce_scatter(partial_first, core), _pair_gather(_chip_sum(parts_second))]
    per_layer = [_unpack_rows(r.reshape(rows, PACK_COLS), counts, shard_shapes) for r in reduced]
    grads = {n: jnp.stack([per_layer[0][i], per_layer[1][i]]) for i, n in enumerate(BIG_WEIGHTS)}

    small_shapes = [weights[n].shape for n in SMALL_WEIGHTS]
    small_flat = jnp.concatenate([g_small[n].reshape(-1) for n in SMALL_WEIGHTS])
    small_len = small_flat.shape[0]
    small_rows = _round_up(-(-small_len // PACK_COLS), 8)
    small_pack = jnp.pad(small_flat, (0, small_rows * PACK_COLS - small_len)).reshape(small_rows, PACK_COLS)
    meta_rows = N_META * d // PACK_COLS
    summed = _all_reduce_small(d_meta_rows.reshape(-1, meta_rows, PACK_COLS), small_pack)
    grad_meta_full = summed[:meta_rows].reshape(N_META, d)
    grads["meta_tokens"] = lax.dynamic_slice_in_dim(grad_meta_full, chip * meta_cols, meta_cols, axis=1)
    small_sum = summed[meta_rows:].reshape(-1)
    start = 0
    for n, shape in zip(SMALL_WEIGHTS, small_shapes):
        size = 1
        for s in shape:
            size *= s
        grads[n] = small_sum[start:start + size].reshape(shape)
        start += size

    deltas, new_m, new_v = {}, {}, {}
    for n in WEIGHT_ORDER:
        deltas[n], new_m[n], new_v[n] = _adamw(weights[n], grads[n], first[n], second[n])

    total_loss = lax.psum(loss[0, 0], ("x", "y", "c"))
    return (total_loss, grad_x, *[grads[n] for n in WEIGHT_ORDER], *[deltas[n] for n in WEIGHT_ORDER],
            *[new_m[n] for n in WEIGHT_ORDER], *[new_v[n] for n in WEIGHT_ORDER])
```

```python
import functools

import jax
import jax.numpy as jnp
from jax import lax
from jax.experimental import pallas as pl
from jax.experimental.pallas import tpu as pltpu

F32 = jnp.float32
BF16 = jnp.bfloat16

N_META = 16
POOL_WINDOWS = (2, 4, 8, 16)
POOL_GROUP = 128
POOL_WIDTH = POOL_GROUP * len(POOL_WINDOWS)
QK_NOPE = 64
QK_ROPE = 32
V_DIM = 64
QK_DIM = QK_NOPE + QK_ROPE
Q_RANK = 256
KV_RANK = 128
HEAD_PAD = 128
SM_SCALE = QK_DIM ** -0.5
ROPE_THETA = 10000.0
NORM_EPS = 1e-6
MASK_VALUE = -1e30
Z_FIXED = POOL_WIDTH + Q_RANK + KV_RANK + HEAD_PAD

ADAM_LR = 0.001
ADAM_B1 = 0.9
ADAM_B2 = 0.999
ADAM_EPS = 1e-08
ADAM_WD = 0.01
ADAM_STEP = 10

N_CHIPS = 4
ATT_BLOCK = 256
SEQ_PAD = 128
ATT_FWD_HEADS = 4
ATT_BWD_HEADS = 4
PACK_COLS = 1024
PACK_TILE = 512
VMEM_LIMIT = 60 * 1024 * 1024

MESH = pl.DeviceIdType.MESH

BIG_WEIGHTS = ("w_in", "w_uq", "w_ukv", "w_pa", "w_pb", "w_o", "w_gate", "w_up", "w_down")
EARLY_WEIGHTS = ("w_in", "w_uq", "w_ukv")
LATE_WEIGHTS = ("w_pa", "w_pb", "w_o", "w_gate", "w_up", "w_down")
SHARD_AXIS = {"w_in": 2, "w_uq": 2, "w_ukv": 2, "w_pa": 2, "w_pb": 1, "w_o": 1, "w_gate": 2, "w_up": 2, "w_down": 1}
SMALL_WEIGHTS = ("norm_mix_g", "pool_w", "pool_scale", "q_norm_g", "kv_norm_g", "norm_ffn_g", "final_norm_g")
WEIGHT_ORDER = ("meta_tokens", "norm_mix_g", "w_in", "pool_w", "pool_scale", "q_norm_g", "kv_norm_g", "w_uq", "w_ukv",
                "w_pa", "w_pb", "w_o", "norm_ffn_g", "w_gate", "w_up", "w_down", "final_norm_g")


def _round_up(n, m):
    return -(-n // m) * m


def _vmem_spec():
    return pl.BlockSpec(memory_space=pltpu.VMEM)


def _any_spec():
    return pl.BlockSpec(memory_space=pl.ANY)


def _row_block(tm, width, col_block=0):
    return pl.BlockSpec((tm, width), lambda i, cb=col_block: (i, cb))


def _params(sem, vmem=VMEM_LIMIT):
    return pltpu.CompilerParams(dimension_semantics=sem, vmem_limit_bytes=vmem)


def _token_tile(t, want):
    best = SEQ_PAD
    for tm in range(32, min(t, 2 * want) + 1, 32):
        if t % tm == 0 and abs(tm - want) < abs(best - want):
            best = tm
    return best


def _dot(a, b):
    return jnp.dot(a, b, preferred_element_type=F32)


def _dot_nt(a, b):
    return lax.dot_general(a, b, (((1,), (1,)), ((), ())), preferred_element_type=F32)


def _dot_tn(a, b):
    return lax.dot_general(a, b, (((0,), (0,)), ((), ())), preferred_element_type=F32)


def _rms_fwd(x, g):
    r = lax.rsqrt(jnp.mean(x * x, axis=-1, keepdims=True) + NORM_EPS)
    xh = x * r
    return xh * g, xh, r


def _rms_bwd(dy, xh, r, g):
    gdy = dy * g
    dx = r * (gdy - xh * jnp.mean(xh * gdy, axis=-1, keepdims=True))
    return dx, dy * xh


def _rope_fwd(x, c, sa, sb):
    return x * c + pltpu.roll(x, 16, 1) * sa + pltpu.roll(x, HEAD_PAD - 16, 1) * sb


def _rope_bwd(d, c, sa, sb):
    return d * c + pltpu.roll(d * sa, HEAD_PAD - 16, 1) + pltpu.roll(d * sb, 16, 1)


def _in_proj_fwd(h, g, w_in_p):
    t, d = h.shape
    nz = w_in_p.shape[1]
    tm = _token_tile(t, 512)

    def body(h_ref, g_ref, w_ref, z_ref):
        hn, _, _ = _rms_fwd(h_ref[...], g_ref[...])
        z_ref[...] = _dot(hn.astype(BF16), w_ref[...]).astype(BF16)

    return pl.pallas_call(
        body, name="in_proj_fwd", grid=(t // tm,),
        in_specs=[_row_block(tm, d), _vmem_spec(), _vmem_spec()],
        out_specs=_row_block(tm, nz),
        out_shape=jax.ShapeDtypeStruct((t, nz), BF16),
        compiler_params=_params(("parallel",)),
    )(h, g, w_in_p)


def _window_sum(x, w, row, forward):
    n = x.shape[0]
    s = x
    k = 1
    while k < w:
        if forward:
            s = s + jnp.where(row >= k, pltpu.roll(s, k, 0), 0.0)
        else:
            s = s + jnp.where(row < n - k, pltpu.roll(s, n - k, 0), 0.0)
        k *= 2
    return s


def _pool_fwd(z3, pool_w, pool_scale):
    b, lp, _ = z3.shape

    def body(u_ref, pw_ref, sc_ref, a_ref):
        row = lax.broadcasted_iota(jnp.int32, (lp, POOL_GROUP), 0)
        pos = row.astype(F32)
        for gi, w in enumerate(POOL_WINDOWS):
            cols = slice(gi * POOL_GROUP, (gi + 1) * POOL_GROUP)
            u = u_ref[0, :, cols].astype(F32)
            y = _window_sum(u, w, row, True) / jnp.minimum(pos + 1.0, float(w)) - u
            yw = _dot(y.astype(BF16), pw_ref[gi])
            a_ref[0, :, cols] = (yw * sc_ref[:, cols]).astype(BF16)

    return pl.pallas_call(
        body, name="pool_fwd", grid=(b,),
        in_specs=[pl.BlockSpec((1, lp, POOL_WIDTH), lambda i: (i, 0, 0)), _vmem_spec(), _vmem_spec()],
        out_specs=pl.BlockSpec((1, lp, POOL_WIDTH), lambda i: (i, 0, 0)),
        out_shape=jax.ShapeDtypeStruct((b, lp, POOL_WIDTH), BF16),
        compiler_params=_params(("parallel",)),
    )(z3, pool_w, pool_scale)


def _qkv_fwd(z, g_q, g_kv, w_uq_p, w_kv_p, rope_c, rope_sa, rope_sb):
    t = z.shape[0]
    hw = w_uq_p.shape[1]
    heads = hw // HEAD_PAD
    tm = _token_tile(t, 512)

    def body(cq_ref, ckv_ref, kr_ref, gq_ref, gkv_ref, wq_ref, wkv_ref, c_ref, sa_ref, sb_ref, q_ref, k_ref, v_ref):
        c, sa, sb = c_ref[...], sa_ref[...], sb_ref[...]
        cqn, _, _ = _rms_fwd(cq_ref[...].astype(F32), gq_ref[...])
        qraw = _dot(cqn.astype(BF16), wq_ref[...])
        ckvn, _, _ = _rms_fwd(ckv_ref[...].astype(F32), gkv_ref[...])
        kvraw = _dot(ckvn.astype(BF16), wkv_ref[...])
        kr = kr_ref[...].astype(F32)
        for hd in range(heads):
            cols = slice(hd * HEAD_PAD, (hd + 1) * HEAD_PAD)
            q_ref[:, cols] = (_rope_fwd(qraw[:, cols], c, sa, sb) * SM_SCALE).astype(BF16)
            k_ref[:, cols] = _rope_fwd(kvraw[:, cols] + kr, c, sa, sb).astype(BF16)
        v_ref[...] = kvraw[:, hw:].astype(BF16)

    out = jax.ShapeDtypeStruct((t, hw), BF16)
    return pl.pallas_call(
        body, name="qkv_fwd", grid=(t // tm,),
        in_specs=[_row_block(tm, Q_RANK, POOL_WIDTH // Q_RANK),
                  _row_block(tm, KV_RANK, (POOL_WIDTH + Q_RANK) // KV_RANK),
                  _row_block(tm, HEAD_PAD, (POOL_WIDTH + Q_RANK + KV_RANK) // HEAD_PAD),
                  _vmem_spec(), _vmem_spec(), _vmem_spec(), _vmem_spec(),
                  _row_block(tm, HEAD_PAD), _row_block(tm, HEAD_PAD), _row_block(tm, HEAD_PAD)],
        out_specs=[_row_block(tm, hw)] * 3,
        out_shape=[out, out, out],
        compiler_params=_params(("parallel",)),
    )(z, z, z, g_q, g_kv, w_uq_p, w_kv_p, rope_c, rope_sa, rope_sb)


def _heads_per_step(heads, want):
    while heads % want:
        want //= 2
    return want


def _causal_mask(rows):
    row = lax.broadcasted_iota(jnp.int32, (rows, rows), 0)
    col = lax.broadcasted_iota(jnp.int32, (rows, rows), 1)
    return col <= row


def _attn_blocks(real_len):
    tail_start = (-(-real_len // ATT_BLOCK) - 1) * ATT_BLOCK
    return tail_start // ATT_BLOCK, tail_start, _round_up(real_len - tail_start, 32)


def _call_with_exchange(body, exchange, *, name, grid, in_specs, out_specs, out_shape, scratch_shapes, operands):
    if exchange is None:
        return pl.pallas_call(body, name=name, grid=grid, in_specs=in_specs, out_specs=out_specs, out_shape=out_shape,
                              scratch_shapes=scratch_shapes,
                              compiler_params=_params(("parallel",) + ("arbitrary",) * (len(grid) - 1)))(*operands)
    n_in, n_out, n_scratch = len(in_specs), len(out_specs), len(scratch_shapes)

    def riding(*refs):
        ins, src = refs[:n_in], refs[n_in]
        outs, dst = refs[n_in + 1:n_in + 1 + n_out], refs[n_in + 1 + n_out]
        scratch = refs[n_in + 2 + n_out:n_in + 2 + n_out + n_scratch]
        send_sems, recv_sems = refs[n_in + 2 + n_out + n_scratch:]
        steps = [pl.program_id(a) for a in range(len(grid))]

        @pl.when(functools.reduce(jnp.logical_and, [s == 0 for s in steps]))
        def _():
            exchange["start"](src, dst, send_sems, recv_sems)

        body(*ins, *outs, *scratch)

        @pl.when(functools.reduce(jnp.logical_and, [s == g - 1 for s, g in zip(steps, grid)]))
        def _():
            exchange["finish"](src, dst, send_sems, recv_sems)

    n = exchange["copies"]
    return pl.pallas_call(
        riding, name=name + "_" + exchange["name"], grid=grid,
        in_specs=list(in_specs) + [_any_spec()], out_specs=list(out_specs) + [_any_spec()],
        out_shape=list(out_shape) + [exchange["out_shape"]],
        scratch_shapes=list(scratch_shapes) + [pltpu.SemaphoreType.DMA((n,)), pltpu.SemaphoreType.DMA((n,))],
        compiler_params=_params(("arbitrary",) * len(grid)),
    )(*operands, exchange["operand"])


def _attn_fwd(q3, k3, v3, real_len, exchange=None):
    b, lp, hw = q3.shape
    heads = hw // HEAD_PAD
    tb = ATT_BLOCK
    nfull, tail_start, tail = _attn_blocks(real_len)
    done = tail_start + tail
    hpg = _heads_per_step(heads, ATT_FWD_HEADS)
    width = hpg * HEAD_PAD

    def body(q_ref, k_ref, v_ref, o_ref, lse_ref):
        group = pl.program_id(1)

        @pl.when(group == 0)
        def _():
            lse_ref[...] = jnp.zeros_like(lse_ref)

        def q_rows(r0, rows, whole_kv_blocks):
            def kv_step(c0, keys, states, mask):
                out = []
                for hd, (m, l, acc) in enumerate(states):
                    cols = slice(hd * HEAD_PAD, (hd + 1) * HEAD_PAD)
                    s = _dot_nt(q_ref[0, pl.ds(r0, rows), cols], k_ref[0, pl.ds(c0, keys), cols])
                    if mask is not None:
                        s = jnp.where(mask, s, MASK_VALUE)
                    m_new = jnp.maximum(m, jnp.max(s, axis=-1, keepdims=True))
                    alpha = jnp.exp(m - m_new)
                    p = jnp.exp(s - m_new)
                    l = alpha * l + jnp.sum(p, axis=-1, keepdims=True)
                    acc = alpha * acc + _dot(p.astype(BF16), v_ref[0, pl.ds(c0, keys), cols])
                    out.append((m_new, l, acc))
                return tuple(out)

            init = tuple((jnp.full((rows, 1), MASK_VALUE, F32), jnp.zeros((rows, 1), F32), jnp.zeros((rows, HEAD_PAD), F32))
                         for _ in range(hpg))
            states = lax.fori_loop(0, whole_kv_blocks, lambda j, st: kv_step(pl.multiple_of(j * tb, tb), tb, st, None), init)
            states = kv_step(r0, rows, states, _causal_mask(rows))
            lane = lax.broadcasted_iota(jnp.int32, (rows, HEAD_PAD), 1)
            lse_rows = lse_ref[0, pl.ds(r0, rows), :]
            for hd, (m, l, acc) in enumerate(states):
                o_ref[0, pl.ds(r0, rows), hd * HEAD_PAD:(hd + 1) * HEAD_PAD] = (acc / l).astype(BF16)
                lse_rows = jnp.where(lane == group * hpg + hd, m + jnp.log(l), lse_rows)
            lse_ref[0, pl.ds(r0, rows), :] = lse_rows

        def whole_block(i, carry):
            q_rows(pl.multiple_of(i * tb, tb), tb, i)
            return carry

        lax.fori_loop(0, nfull, whole_block, 0)
        q_rows(tail_start, tail, nfull)
        if done < lp:
            o_ref[0, done:lp, :] = jnp.zeros((lp - done, width), BF16)

    head_spec = pl.BlockSpec((1, lp, width), lambda bi, hi: (bi, 0, hi))
    return _call_with_exchange(
        body, exchange, name="attn_fwd", grid=(b, heads // hpg),
        in_specs=[head_spec, head_spec, head_spec],
        out_specs=[head_spec, pl.BlockSpec((1, lp, HEAD_PAD), lambda bi, hi: (bi, 0, 0))],
        out_shape=[jax.ShapeDtypeStruct((b, lp, hw), BF16), jax.ShapeDtypeStruct((b, lp, HEAD_PAD), F32)],
        scratch_shapes=[], operands=(q3, k3, v3))


def _merge_fwd(h, z, a, o, w_pa, w_pb_p, w_o):
    t, d = h.shape
    hw = o.shape[1]
    tm = _token_tile(t, 512)
    gate_block = Z_FIXED // d

    def body(h_ref, ga_ref, gb_ref, a_ref, o_ref, wpa_ref, wpb_ref, wo_ref, h1_ref, pa_ref, pb_ref):
        pa = _dot(a_ref[...], wpa_ref[...])
        pb = _dot(o_ref[...], wpb_ref[...])
        merged = jax.nn.sigmoid(ga_ref[...].astype(F32)) * pa + jax.nn.sigmoid(gb_ref[...].astype(F32)) * pb
        h1_ref[...] = h_ref[...] + _dot(merged.astype(BF16), wo_ref[...])
        pa_ref[...] = pa.astype(BF16)
        pb_ref[...] = pb.astype(BF16)

    return pl.pallas_call(
        body, name="merge_fwd", grid=(t // tm,),
        in_specs=[_row_block(tm, d), _row_block(tm, d, gate_block), _row_block(tm, d, gate_block + 1),
                  _row_block(tm, POOL_WIDTH), _row_block(tm, hw), _vmem_spec(), _vmem_spec(), _vmem_spec()],
        out_specs=[_row_block(tm, d)] * 3,
        out_shape=[jax.ShapeDtypeStruct((t, d), F32), jax.ShapeDtypeStruct((t, d), BF16), jax.ShapeDtypeStruct((t, d), BF16)],
        compiler_params=_params(("parallel",)),
    )(h, z, z, a, o, w_pa, w_pb_p, w_o)


def _ffn_fwd(h, g, w_gate, w_up, w_down):
    t, d = h.shape
    f = w_gate.shape[1]
    tm = _token_tile(t, 256)

    def body(h_ref, g_ref, wg_ref, wu_ref, wd_ref, h2_ref, a_ref, b_ref):
        x = h_ref[...]
        hn, _, _ = _rms_fwd(x, g_ref[...])
        hn = hn.astype(BF16)
        ga = _dot(hn, wg_ref[...])
        up = _dot(hn, wu_ref[...])
        act = ga * jax.nn.sigmoid(ga) * up
        h2_ref[...] = x + _dot(act.astype(BF16), wd_ref[...])
        a_ref[...] = ga.astype(BF16)
        b_ref[...] = up.astype(BF16)

    return pl.pallas_call(
        body, name="ffn_fwd", grid=(t // tm,),
        in_specs=[_row_block(tm, d), _vmem_spec(), _vmem_spec(), _vmem_spec(), _vmem_spec()],
        out_specs=[_row_block(tm, d), _row_block(tm, f), _row_block(tm, f)],
        out_shape=[jax.ShapeDtypeStruct((t, d), F32), jax.ShapeDtypeStruct((t, f), BF16), jax.ShapeDtypeStruct((t, f), BF16)],
        compiler_params=_params(("parallel",)),
    )(h, g, w_gate, w_up, w_down)


def _loss_head(h, g, target, valid):
    t, d = h.shape
    tm = _token_tile(t, 512)

    def body(h_ref, g_ref, t_ref, valid_ref, dh_ref, loss_ref, dg_ref):
        @pl.when(pl.program_id(0) == 0)
        def _():
            loss_ref[...] = jnp.zeros_like(loss_ref)
            dg_ref[...] = jnp.zeros_like(dg_ref)

        gain = g_ref[...]
        y, xh, r = _rms_fwd(h_ref[...], gain)
        err = (y - t_ref[...]) * valid_ref[...]
        per_row = jnp.sum(err * err, axis=-1, keepdims=True) / d
        loss_ref[...] += 0.5 * jnp.sum(per_row, axis=0, keepdims=True)
        dx, dg_rows = _rms_bwd(err / d, xh, r, gain)
        dh_ref[...] = dx
        dg_ref[...] += jnp.sum(dg_rows, axis=0, keepdims=True)

    return pl.pallas_call(
        body, name="loss_head", grid=(t // tm,),
        in_specs=[_row_block(tm, d), _vmem_spec(), _row_block(tm, d), _row_block(tm, 1)],
        out_specs=[_row_block(tm, d), pl.BlockSpec((1, 1), lambda i: (0, 0)), pl.BlockSpec((1, d), lambda i: (0, 0))],
        out_shape=[jax.ShapeDtypeStruct((t, d), F32), jax.ShapeDtypeStruct((1, 1), F32), jax.ShapeDtypeStruct((1, d), F32)],
        compiler_params=_params(("arbitrary",)),
    )(h, g, target, valid)


def _weight_grad(x, y, name):
    t, k = x.shape
    n = y.shape[1]
    tm = _token_tile(t, 512)
    tn = n
    while k * tn * 4 > 8 * 1024 * 1024 and tn % 256 == 0:
        tn //= 2
    steps = t // tm

    def body(x_ref, y_ref, o_ref, acc):
        @pl.when(pl.program_id(1) == 0)
        def _():
            acc[...] = jnp.zeros_like(acc)

        acc[...] += _dot_tn(x_ref[...].astype(BF16), y_ref[...].astype(BF16))

        @pl.when(pl.program_id(1) == steps - 1)
        def _():
            o_ref[...] = acc[...].astype(BF16)

    return pl.pallas_call(
        body, name=name, grid=(n // tn, steps),
        in_specs=[pl.BlockSpec((tm, k), lambda j, i: (i, 0)), pl.BlockSpec((tm, tn), lambda j, i: (i, j))],
        out_specs=pl.BlockSpec((k, tn), lambda j, i: (0, j)),
        out_shape=jax.ShapeDtypeStruct((k, n), BF16),
        scratch_shapes=[pltpu.VMEM((k, tn), F32)],
        compiler_params=_params(("parallel", "arbitrary")),
    )(x, y)


def _ffn_bwd(h, dh2, a, b, g, w_gate, w_up, w_down):
    t, d = h.shape
    f = a.shape[1]
    tm = _token_tile(t, 256)

    def body(h_ref, dh2_ref, a_ref, b_ref, g_ref, wg_ref, wu_ref, wd_ref, dh_ref, hn_ref, act_ref, da_ref, db_ref, dg_ref):
        @pl.when(pl.program_id(0) == 0)
        def _():
            dg_ref[...] = jnp.zeros_like(dg_ref)

        gain = g_ref[...]
        hn, xh, r = _rms_fwd(h_ref[...], gain)
        hn_ref[...] = hn.astype(BF16)
        dh2 = dh2_ref[...]
        dact = _dot_nt(dh2.astype(BF16), wd_ref[...])
        ga = a_ref[...].astype(F32)
        up = b_ref[...].astype(F32)
        sg = jax.nn.sigmoid(ga)
        silu = ga * sg
        act_ref[...] = (silu * up).astype(BF16)
        da = (dact * up * (sg * (1.0 + ga * (1.0 - sg)))).astype(BF16)
        db = (dact * silu).astype(BF16)
        da_ref[...] = da
        db_ref[...] = db
        dhn = _dot_nt(da, wg_ref[...]) + _dot_nt(db, wu_ref[...])
        dx, dg_rows = _rms_bwd(dhn, xh, r, gain)
        dh_ref[...] = dh2 + dx
        dg_ref[...] += jnp.sum(dg_rows, axis=0, keepdims=True)

    return pl.pallas_call(
        body, name="ffn_bwd", grid=(t // tm,),
        in_specs=[_row_block(tm, d), _row_block(tm, d), _row_block(tm, f), _row_block(tm, f),
                  _vmem_spec(), _vmem_spec(), _vmem_spec(), _vmem_spec()],
        out_specs=[_row_block(tm, d), _row_block(tm, d), _row_block(tm, f), _row_block(tm, f), _row_block(tm, f),
                   pl.BlockSpec((1, d), lambda i: (0, 0))],
        out_shape=[jax.ShapeDtypeStruct((t, d), F32), jax.ShapeDtypeStruct((t, d), BF16), jax.ShapeDtypeStruct((t, f), BF16),
                   jax.ShapeDtypeStruct((t, f), BF16), jax.ShapeDtypeStruct((t, f), BF16), jax.ShapeDtypeStruct((1, d), F32)],
        compiler_params=_params(("arbitrary",)),
    )(h, dh2, a, b, g, w_gate, w_up, w_down)


def _merge_bwd(dh1, z, pa, pb, w_o, w_pa, w_pb_p):
    t, d = dh1.shape
    hw = w_pb_p.shape[0]
    tm = _token_tile(t, 512)
    gate_block = Z_FIXED // d

    def body(dh_ref, ga_ref, gb_ref, pa_ref, pb_ref, wo_ref, wpa_ref, wpb_ref,
             mg_ref, dpa_ref, dpb_ref, dga_ref, dgb_ref, da_ref, do_ref):
        dm = _dot_nt(dh_ref[...].astype(BF16), wo_ref[...])
        sa = jax.nn.sigmoid(ga_ref[...].astype(F32))
        sb = jax.nn.sigmoid(gb_ref[...].astype(F32))
        pa = pa_ref[...].astype(F32)
        pb = pb_ref[...].astype(F32)
        mg_ref[...] = (sa * pa + sb * pb).astype(BF16)
        dpa = (dm * sa).astype(BF16)
        dpb = (dm * sb).astype(BF16)
        dpa_ref[...] = dpa
        dpb_ref[...] = dpb
        dga_ref[...] = (dm * pa * (sa * (1.0 - sa))).astype(BF16)
        dgb_ref[...] = (dm * pb * (sb * (1.0 - sb))).astype(BF16)
        da_ref[...] = _dot_nt(dpa, wpa_ref[...]).astype(BF16)
        do_ref[...] = _dot_nt(dpb, wpb_ref[...]).astype(BF16)

    wide = jax.ShapeDtypeStruct((t, d), BF16)
    return pl.pallas_call(
        body, name="merge_bwd", grid=(t // tm,),
        in_specs=[_row_block(tm, d), _row_block(tm, d, gate_block), _row_block(tm, d, gate_block + 1),
                  _row_block(tm, d), _row_block(tm, d), _vmem_spec(), _vmem_spec(), _vmem_spec()],
        out_specs=[_row_block(tm, d)] * 5 + [_row_block(tm, POOL_WIDTH), _row_block(tm, hw)],
        out_shape=[wide] * 5 + [jax.ShapeDtypeStruct((t, POOL_WIDTH), BF16), jax.ShapeDtypeStruct((t, hw), BF16)],
        compiler_params=_params(("parallel",)),
    )(dh1, z, z, pa, pb, w_o, w_pa, w_pb_p)


def _attn_bwd(q3, k3, v3, o3, do3, lse3, real_len, exchange=None):
    b, lp, hw = q3.shape
    heads = hw // HEAD_PAD
    tb = ATT_BLOCK
    nfull, tail_start, tail = _attn_blocks(real_len)
    done = tail_start + tail
    hpg = _heads_per_step(heads, ATT_BWD_HEADS)
    width = hpg * HEAD_PAD

    def body(q_ref, k_ref, v_ref, o_ref, do_ref, lse_ref, dq_ref, dk_ref, dv_ref, dq_acc, lse_col, delta_col):
        group = pl.program_id(1)
        lane = lax.broadcasted_iota(jnp.int32, (lp, HEAD_PAD), 1)
        for hd in range(hpg):
            cols = slice(hd * HEAD_PAD, (hd + 1) * HEAD_PAD)
            lse_col[hd] = jnp.sum(jnp.where(lane == group * hpg + hd, lse_ref[0], 0.0), axis=-1, keepdims=True)
            delta_col[hd] = jnp.sum(do_ref[0, :, cols].astype(F32) * o_ref[0, :, cols].astype(F32), axis=-1, keepdims=True)
        dq_acc[...] = jnp.zeros_like(dq_acc)

        def kv_rows(c0, keys, whole_q_blocks_from):
            def q_step(r0, rows, states, mask):
                out = []
                for hd, (dk, dv) in enumerate(states):
                    cols = slice(hd * HEAD_PAD, (hd + 1) * HEAD_PAD)
                    k = k_ref[0, pl.ds(c0, keys), cols]
                    q = q_ref[0, pl.ds(r0, rows), cols]
                    do = do_ref[0, pl.ds(r0, rows), cols]
                    s = _dot_nt(q, k)
                    if mask is not None:
                        s = jnp.where(mask, s, MASK_VALUE)
                    p = jnp.exp(s - lse_col[hd, pl.ds(r0, rows), :])
                    dp = _dot_nt(do, v_ref[0, pl.ds(c0, keys), cols])
                    ds = (p * (dp - delta_col[hd, pl.ds(r0, rows), :])).astype(BF16)
                    dv = dv + _dot_tn(p.astype(BF16), do)
                    dk = dk + _dot_tn(ds, q)
                    dq_acc[pl.ds(r0, rows), cols] += _dot(ds, k)
                    out.append((dk, dv))
                return tuple(out)

            zero = jnp.zeros((keys, HEAD_PAD), F32)
            states = q_step(c0, keys, tuple((zero, zero) for _ in range(hpg)), _causal_mask(keys))
            if whole_q_blocks_from is not None:
                states = lax.fori_loop(whole_q_blocks_from, nfull,
                                       lambda i, st: q_step(pl.multiple_of(i * tb, tb), tb, st, None), states)
                states = q_step(tail_start, tail, states, None)
            for hd, (dk, dv) in enumerate(states):
                cols = slice(hd * HEAD_PAD, (hd + 1) * HEAD_PAD)
                dk_ref[0, pl.ds(c0, keys), cols] = dk.astype(BF16)
                dv_ref[0, pl.ds(c0, keys), cols] = dv.astype(BF16)

        def whole_block(j, carry):
            kv_rows(pl.multiple_of(j * tb, tb), tb, j + 1)
            return carry

        lax.fori_loop(0, nfull, whole_block, 0)
        kv_rows(tail_start, tail, None)
        if done < lp:
            dk_ref[0, done:lp, :] = jnp.zeros((lp - done, width), BF16)
            dv_ref[0, done:lp, :] = jnp.zeros((lp - done, width), BF16)
        dq_ref[0] = dq_acc[...].astype(BF16)

    head_spec = pl.BlockSpec((1, lp, width), lambda bi, hi: (bi, 0, hi))
    out = jax.ShapeDtypeStruct((b, lp, hw), BF16)
    return _call_with_exchange(
        body, exchange, name="attn_bwd", grid=(b, heads // hpg),
        in_specs=[head_spec] * 5 + [pl.BlockSpec((1, lp, HEAD_PAD), lambda bi, hi: (bi, 0, 0))],
        out_specs=[head_spec] * 3,
        out_shape=[out, out, out],
        scratch_shapes=[pltpu.VMEM((lp, width), F32), pltpu.VMEM((hpg, lp, 1), F32), pltpu.VMEM((hpg, lp, 1), F32)],
        operands=(q3, k3, v3, o3, do3, lse3))


def _qkv_bwd(dq, dk, dv, z, g_q, g_kv, w_uq_p, w_kv_p, rope_c, rope_sa, rope_sb):
    t, hw = dq.shape
    heads = hw // HEAD_PAD
    tm = _token_tile(t, 512)

    def body(dq_ref, dk_ref, dv_ref, cq_ref, ckv_ref, gq_ref, gkv_ref, wq_ref, wkv_ref, c_ref, sa_ref, sb_ref,
             dqraw_ref, dkvraw_ref, cqn_ref, ckvn_ref, dcq_ref, dckv_ref, dkr_ref, dgq_ref, dgkv_ref):
        @pl.when(pl.program_id(0) == 0)
        def _():
            dgq_ref[...] = jnp.zeros_like(dgq_ref)
            dgkv_ref[...] = jnp.zeros_like(dgkv_ref)

        c, sa, sb = c_ref[...], sa_ref[...], sb_ref[...]
        dkr = jnp.zeros((tm, HEAD_PAD), F32)
        for hd in range(heads):
            cols = slice(hd * HEAD_PAD, (hd + 1) * HEAD_PAD)
            dqraw_ref[:, cols] = _rope_bwd(dq_ref[:, cols].astype(F32) * SM_SCALE, c, sa, sb).astype(BF16)
            dkraw = _rope_bwd(dk_ref[:, cols].astype(F32), c, sa, sb)
            dkvraw_ref[:, cols] = dkraw.astype(BF16)
            dkr = dkr + dkraw
        dkvraw_ref[:, hw:] = dv_ref[...]
        lane = lax.broadcasted_iota(jnp.int32, (tm, HEAD_PAD), 1)
        dkr_ref[...] = jnp.where((lane >= QK_NOPE) & (lane < QK_DIM), dkr, 0.0).astype(BF16)

        gq = gq_ref[...]
        cqn, xh, r = _rms_fwd(cq_ref[...].astype(F32), gq)
        cqn_ref[...] = cqn.astype(BF16)
        dx, dg_rows = _rms_bwd(_dot_nt(dqraw_ref[...], wq_ref[...]), xh, r, gq)
        dcq_ref[...] = dx.astype(BF16)
        dgq_ref[...] += jnp.sum(dg_rows, axis=0, keepdims=True)

        gkv = gkv_ref[...]
        ckvn, xh, r = _rms_fwd(ckv_ref[...].astype(F32), gkv)
        ckvn_ref[...] = ckvn.astype(BF16)
        dx, dg_rows = _rms_bwd(_dot_nt(dkvraw_ref[...], wkv_ref[...]), xh, r, gkv)
        dckv_ref[...] = dx.astype(BF16)
        dgkv_ref[...] += jnp.sum(dg_rows, axis=0, keepdims=True)

    def shape(width, dtype=BF16):
        return jax.ShapeDtypeStruct((t, width), dtype)

    return pl.pallas_call(
        body, name="qkv_bwd", grid=(t // tm,),
        in_specs=[_row_block(tm, hw)] * 3
        + [_row_block(tm, Q_RANK, POOL_WIDTH // Q_RANK), _row_block(tm, KV_RANK, (POOL_WIDTH + Q_RANK) // KV_RANK)]
        + [_vmem_spec()] * 4 + [_row_block(tm, HEAD_PAD)] * 3,
        out_specs=[_row_block(tm, hw), _row_block(tm, 2 * hw), _row_block(tm, Q_RANK), _row_block(tm, KV_RANK),
                   _row_block(tm, Q_RANK), _row_block(tm, KV_RANK), _row_block(tm, HEAD_PAD),
                   pl.BlockSpec((1, Q_RANK), lambda i: (0, 0)), pl.BlockSpec((1, KV_RANK), lambda i: (0, 0))],
        out_shape=[shape(hw), shape(2 * hw), shape(Q_RANK), shape(KV_RANK), shape(Q_RANK), shape(KV_RANK), shape(HEAD_PAD),
                   jax.ShapeDtypeStruct((1, Q_RANK), F32), jax.ShapeDtypeStruct((1, KV_RANK), F32)],
        compiler_params=_params(("arbitrary",)),
    )(dq, dk, dv, z, z, g_q, g_kv, w_uq_p, w_kv_p, rope_c, rope_sa, rope_sb)


def _pool_bwd(z3, da3, pool_w, pool_scale):
    b, lp, _ = z3.shape
    groups = len(POOL_WINDOWS)

    def body(u_ref, da_ref, pw_ref, sc_ref, du_ref, dpw_ref, dsc_ref):
        @pl.when(pl.program_id(0) == 0)
        def _():
            dpw_ref[...] = jnp.zeros_like(dpw_ref)
            dsc_ref[...] = jnp.zeros_like(dsc_ref)

        row = lax.broadcasted_iota(jnp.int32, (lp, POOL_GROUP), 0)
        pos = row.astype(F32)
        for gi, w in enumerate(POOL_WINDOWS):
            cols = slice(gi * POOL_GROUP, (gi + 1) * POOL_GROUP)
            count = jnp.minimum(pos + 1.0, float(w))
            u = u_ref[0, :, cols].astype(F32)
            y = (_window_sum(u, w, row, True) / count - u).astype(BF16)
            yw = _dot(y, pw_ref[gi])
            da = da_ref[0, :, cols].astype(F32)
            dsc_ref[:, cols] += jnp.sum(da * yw, axis=0, keepdims=True)
            dyw = (da * sc_ref[:, cols]).astype(BF16)
            dpw_ref[gi] += _dot_tn(y, dyw)
            dy = _dot_nt(dyw, pw_ref[gi])
            du_ref[0, :, cols] = (_window_sum(dy / count, w, row, False) - dy).astype(BF16)

    return pl.pallas_call(
        body, name="pool_bwd", grid=(b,),
        in_specs=[pl.BlockSpec((1, lp, POOL_WIDTH), lambda i: (i, 0, 0)), pl.BlockSpec((1, lp, POOL_WIDTH), lambda i: (i, 0, 0)),
                  _vmem_spec(), _vmem_spec()],
        out_specs=[pl.BlockSpec((1, lp, POOL_WIDTH), lambda i: (i, 0, 0)),
                   pl.BlockSpec((groups, POOL_GROUP, POOL_GROUP), lambda i: (0, 0, 0)),
                   pl.BlockSpec((1, POOL_WIDTH), lambda i: (0, 0))],
        out_shape=[jax.ShapeDtypeStruct((b, lp, POOL_WIDTH), BF16), jax.ShapeDtypeStruct((groups, POOL_GROUP, POOL_GROUP), F32),
                   jax.ShapeDtypeStruct((1, POOL_WIDTH), F32)],
        compiler_params=_params(("arbitrary",)),
    )(z3, da3, pool_w, pool_scale)


def _in_proj_bwd(h, dh1, du, dcq, dckv, dkr, dga, dgb, g, w_in_p):
    t, d = h.shape
    nz = w_in_p.shape[1]
    tm = _token_tile(t, 512)
    widths = (POOL_WIDTH, Q_RANK, KV_RANK, HEAD_PAD, d, d)

    def body(h_ref, dh1_ref, du_ref, dcq_ref, dckv_ref, dkr_ref, dga_ref, dgb_ref, g_ref, w_ref, dh_ref, hn_ref, dz_ref, dg_ref):
        @pl.when(pl.program_id(0) == 0)
        def _():
            dg_ref[...] = jnp.zeros_like(dg_ref)

        gain = g_ref[...]
        hn, xh, r = _rms_fwd(h_ref[...], gain)
        hn_ref[...] = hn.astype(BF16)
        dhn = jnp.zeros((tm, d), F32)
        start = 0
        for piece, width in zip((du_ref, dcq_ref, dckv_ref, dkr_ref, dga_ref, dgb_ref), widths):
            val = piece[...]
            dz_ref[:, start:start + width] = val
            dhn = dhn + _dot_nt(val, w_ref[:, start:start + width])
            start += width
        dx, dg_rows = _rms_bwd(dhn, xh, r, gain)
        dh_ref[...] = dh1_ref[...] + dx
        dg_ref[...] += jnp.sum(dg_rows, axis=0, keepdims=True)

    return pl.pallas_call(
        body, name="in_proj_bwd", grid=(t // tm,),
        in_specs=[_row_block(tm, d), _row_block(tm, d)] + [_row_block(tm, w) for w in widths] + [_vmem_spec(), _vmem_spec()],
        out_specs=[_row_block(tm, d), _row_block(tm, d), _row_block(tm, nz), pl.BlockSpec((1, d), lambda i: (0, 0))],
        out_shape=[jax.ShapeDtypeStruct((t, d), F32), jax.ShapeDtypeStruct((t, d), BF16), jax.ShapeDtypeStruct((t, nz), BF16),
                   jax.ShapeDtypeStruct((1, d), F32)],
        compiler_params=_params(("arbitrary",)),
    )(h, dh1, du, dcq, dckv, dkr, dga, dgb, g, w_in_p)


def _pad_heads(w, heads, width):
    k = w.shape[0]
    w = w.reshape(k, heads, width)
    return jnp.pad(w, ((0, 0), (0, 0), (0, HEAD_PAD - width))).reshape(k, heads * HEAD_PAD)


def _unpad_heads(w, heads, width):
    k = w.shape[0]
    return w.reshape(k, heads, HEAD_PAD)[:, :, :width].reshape(k, heads * width)


def _early_layouts(w, heads):
    o3, o4 = POOL_WIDTH + Q_RANK + KV_RANK, POOL_WIDTH + Q_RANK + KV_RANK + QK_ROPE
    w_in = w["w_in"]
    rope_cols = jnp.pad(w_in[:, o3:o4], ((0, 0), (QK_NOPE, HEAD_PAD - QK_DIM)))
    w_in_p = jnp.concatenate([w_in[:, :o3], rope_cols, w_in[:, o4:]], axis=1)
    w_uq_p = _pad_heads(w["w_uq"], heads, QK_DIM)
    kv = w["w_ukv"].reshape(KV_RANK, heads, QK_NOPE + V_DIM)
    w_k = jnp.pad(kv[:, :, :QK_NOPE], ((0, 0), (0, 0), (0, HEAD_PAD - QK_NOPE))).reshape(KV_RANK, heads * HEAD_PAD)
    w_v = jnp.pad(kv[:, :, QK_NOPE:], ((0, 0), (0, 0), (0, HEAD_PAD - V_DIM))).reshape(KV_RANK, heads * HEAD_PAD)
    return dict(w_in_p=w_in_p, w_uq_p=w_uq_p, w_kv_p=jnp.concatenate([w_k, w_v], axis=1))


def _late_layouts(w, heads):
    d = w["w_pb"].shape[1]
    w_pb_p = jnp.pad(w["w_pb"].reshape(heads, V_DIM, d), ((0, 0), (0, HEAD_PAD - V_DIM), (0, 0))).reshape(heads * HEAD_PAD, d)
    return dict(w_pa=w["w_pa"], w_pb_p=w_pb_p, w_o=w["w_o"], w_gate=w["w_gate"], w_up=w["w_up"], w_down=w["w_down"])


def _early_grad_layouts(g, heads):
    o3 = POOL_WIDTH + Q_RANK + KV_RANK
    gin = g["w_in_p"]
    w_in = jnp.concatenate([gin[:, :o3], gin[:, o3 + QK_NOPE:o3 + QK_DIM], gin[:, o3 + HEAD_PAD:]], axis=1)
    hw = heads * HEAD_PAD
    gk = g["w_kv_p"][:, :hw].reshape(KV_RANK, heads, HEAD_PAD)[:, :, :QK_NOPE]
    gv = g["w_kv_p"][:, hw:].reshape(KV_RANK, heads, HEAD_PAD)[:, :, :V_DIM]
    w_ukv = jnp.concatenate([gk, gv], axis=2).reshape(KV_RANK, heads * (QK_NOPE + V_DIM))
    return dict(w_in=w_in, w_uq=_unpad_heads(g["w_uq_p"], heads, QK_DIM), w_ukv=w_ukv)


def _late_grad_layouts(g, heads):
    d = g["w_pb_p"].shape[1]
    w_pb = g["w_pb_p"].reshape(heads, HEAD_PAD, d)[:, :V_DIM].reshape(heads * V_DIM, d)
    return dict(w_pa=g["w_pa"], w_pb=w_pb, w_o=g["w_o"], w_gate=g["w_gate"], w_up=g["w_up"], w_down=g["w_down"])


def _rope_tables(lp, b):
    inv = 1.0 / (ROPE_THETA ** (jnp.arange(0, QK_ROPE, 2, dtype=F32) / QK_ROPE))
    ang = jnp.arange(lp, dtype=F32)[:, None] * inv[None, :]
    cos, sin = jnp.cos(ang), jnp.sin(ang)
    half = QK_ROPE // 2
    ones = jnp.ones((lp, QK_NOPE), F32)
    zeros_lo = jnp.zeros((lp, QK_NOPE), F32)
    zeros_hi = jnp.zeros((lp, HEAD_PAD - QK_DIM), F32)
    zeros_half = jnp.zeros((lp, half), F32)
    c = jnp.concatenate([ones, cos, cos, zeros_hi], axis=1)
    sa = jnp.concatenate([zeros_lo, zeros_half, sin, zeros_hi], axis=1)
    sb = jnp.concatenate([zeros_lo, -sin, zeros_half, zeros_hi], axis=1)
    return tuple(jnp.tile(tab, (b, 1)) for tab in (c, sa, sb))


def _local_step(x, loss_target, meta_tokens, small, early_first, riding, early_first_codec):
    b, seq, d = x.shape
    depth = 2
    heads = early_first["w_uq"].shape[1] // QK_DIM
    core = lax.axis_index("c")
    chip = 2 * lax.axis_index("x") + lax.axis_index("y")
    real_len = N_META + seq
    lp = _round_up(real_len, SEQ_PAD)
    t = b * lp
    pad = lp - N_META - seq

    meta = jnp.broadcast_to(meta_tokens[None], (b, N_META, d))
    h = jnp.concatenate([meta, x, jnp.zeros((b, pad, d), F32)], axis=1).reshape(t, d)
    target = jnp.pad(loss_target, ((0, 0), (N_META, pad), (0, 0))).reshape(t, d)
    pos = jnp.arange(lp)
    valid = jnp.tile(((pos >= N_META) & (pos < N_META + seq)).astype(F32), b).reshape(t, 1)
    rope_c, rope_sa, rope_sb = _rope_tables(lp, b)

    layers = []
    for li in range(depth):
        lay = dict(pool_w=small["pool_w"][li].astype(BF16), pool_scale=small["pool_scale"][li][None])
        for n in ("norm_mix_g", "q_norm_g", "kv_norm_g", "norm_ffn_g"):
            lay[n] = small[n][li][None]
        layers.append(lay)
    layers[0].update(_early_layouts(early_first, heads))

    saved = []
    for li in range(depth):
        lay = layers[li]
        z = _in_proj_fwd(h, lay["norm_mix_g"], lay["w_in_p"])
        a = _pool_fwd(z.reshape(b, lp, -1), lay["pool_w"], lay["pool_scale"]).reshape(t, POOL_WIDTH)
        q, k, v = _qkv_fwd(z, lay["q_norm_g"], lay["kv_norm_g"], lay["w_uq_p"], lay["w_kv_p"], rope_c, rope_sa, rope_sb)
        hw = q.shape[1]
        packed = riding[li]["packed"]
        o3, lse, others = _attn_fwd(q.reshape(b, lp, hw), k.reshape(b, lp, hw), v.reshape(b, lp, hw), real_len,
                                    _gather_exchange(packed))
        arrived = riding[li]["unpack"](lax.dynamic_update_index_in_dim(others, packed, chip, 0))
        lay.update(_late_layouts({n: arrived[n, li] for n in LATE_WEIGHTS}, heads))
        if li + 1 < depth:
            layers[li + 1].update(_early_layouts({n: arrived[n, li + 1] for n in EARLY_WEIGHTS}, heads))
        o = o3.reshape(t, hw)
        h1, pa, pb = _merge_fwd(h, z, a, o, lay["w_pa"], lay["w_pb_p"], lay["w_o"])
        h2, fa, fb = _ffn_fwd(h1, lay["norm_ffn_g"], lay["w_gate"], lay["w_up"], lay["w_down"])
        saved.append(dict(h=h, z=z, a=a, q=q, k=k, v=v, o=o, lse=lse, pa=pa, pb=pb, h1=h1, fa=fa, fb=fb))
        h = h2

    dh, loss, d_final = _loss_head(h, small["final_norm_g"][None], target, valid)

    g_small = {n: [] for n in SMALL_WEIGHTS if n != "final_norm_g"}
    early_grads, parts = {}, [None] * depth
    for li in reversed(range(depth)):
        lay, sv = layers[li], saved[li]
        hw = sv["q"].shape[1]
        dh1, hn_f, act, dfa, dfb, dg_ffn = _ffn_bwd(sv["h1"], dh, sv["fa"], sv["fb"], lay["norm_ffn_g"],
                                                     lay["w_gate"], lay["w_up"], lay["w_down"])
        gl = dict(w_gate=_weight_grad(hn_f, dfa, "grad_w_gate"), w_up=_weight_grad(hn_f, dfb, "grad_w_up"),
                  w_down=_weight_grad(act, dh, "grad_w_down"))
        merged, dpa, dpb, dga, dgb, da, do = _merge_bwd(dh1, sv["z"], sv["pa"], sv["pb"], lay["w_o"], lay["w_pa"], lay["w_pb_p"])
        gl["w_o"] = _weight_grad(merged, dh1, "grad_w_o")
        gl["w_pa"] = _weight_grad(sv["a"], dpa, "grad_w_pa")
        gl["w_pb_p"] = _weight_grad(sv["o"], dpb, "grad_w_pb")
        to_send = {(n, li): g for n, g in _late_grad_layouts(gl, heads).items()}
        if li + 1 < depth:
            to_send.update({(n, li + 1): g for n, g in early_grads[li + 1].items()})
        sending = riding[li]["pack_grads"](to_send)
        shape3 = (b, lp, hw)
        dq3, dk3, dv3, from_others = _attn_bwd(
            sv["q"].reshape(shape3), sv["k"].reshape(shape3), sv["v"].reshape(shape3), sv["o"].reshape(shape3),
            do.reshape(shape3), sv["lse"], real_len, _scatter_exchange(sending))
        own = lax.dynamic_index_in_dim(lax.dynamic_index_in_dim(sending, chip, 0, keepdims=False), core, 0, keepdims=False)
        parts[li] = lax.dynamic_update_index_in_dim(from_others, own, 2 * chip + core, 0)
        dqraw, dkvraw, cqn, ckvn, dcq, dckv, dkr, dg_q, dg_kv = _qkv_bwd(
            dq3.reshape(t, hw), dk3.reshape(t, hw), dv3.reshape(t, hw), sv["z"], lay["q_norm_g"], lay["kv_norm_g"],
            lay["w_uq_p"], lay["w_kv_p"], rope_c, rope_sa, rope_sb)
        gl["w_uq_p"] = _weight_grad(cqn, dqraw, "grad_w_uq")
        gl["w_kv_p"] = _weight_grad(ckvn, dkvraw, "grad_w_ukv")
        du3, dpool_w, dpool_scale = _pool_bwd(sv["z"].reshape(b, lp, -1), da.reshape(b, lp, POOL_WIDTH),
                                              lay["pool_w"], lay["pool_scale"])
        dh, hn_m, dz, dg_mix = _in_proj_bwd(sv["h"], dh1, du3.reshape(t, POOL_WIDTH), dcq, dckv, dkr, dga, dgb,
                                            lay["norm_mix_g"], lay["w_in_p"])
        gl["w_in_p"] = _weight_grad(hn_m, dz, "grad_w_in")
        early_grads[li] = _early_grad_layouts(gl, heads)
        for n, val in (("norm_mix_g", dg_mix[0]), ("pool_w", dpool_w), ("pool_scale", dpool_scale[0]), ("q_norm_g", dg_q[0]),
                       ("kv_norm_g", dg_kv[0]), ("norm_ffn_g", dg_ffn[0])):
            g_small[n].insert(0, val)

    dh3 = dh.reshape(b, lp, d)
    grad_x = dh3[:, N_META:N_META + seq]
    d_meta_rows = dh3[:, :N_META]
    g_small = {n: jnp.stack(v) for n, v in g_small.items()}
    g_small["final_norm_g"] = d_final[0]
    early_first_partial = early_first_codec["pack_grads"]({(n, 0): g for n, g in early_grads[0].items()})
    return loss, grad_x, d_meta_rows, g_small, early_first_partial, parts


def _mesh_place():
    x, y, c = lax.axis_index("x"), lax.axis_index("y"), lax.axis_index("c")
    others = [(1 - x, y), (x, 1 - y), (1 - x, 1 - y)]
    return x, y, c, 2 * x + y, others


def _remote(src, dst, send_sems, recv_sems, k, device):
    return pltpu.make_async_remote_copy(src_ref=src, dst_ref=dst, send_sem=send_sems.at[k], recv_sem=recv_sems.at[k],
                                        device_id=device, device_id_type=MESH)


def _gather_exchange(packed):
    def copy(p_ref, g_ref, send_sems, recv_sems, r, slot):
        _, _, c, _, others = _mesh_place()
        ox, oy = others[r]
        return _remote(p_ref, g_ref.at[slot], send_sems, recv_sems, r, (ox, oy, c))

    def start(p_ref, g_ref, send_sems, recv_sems):
        chip = _mesh_place()[3]
        for r in range(3):
            copy(p_ref, g_ref, send_sems, recv_sems, r, chip).start()

    def finish(p_ref, g_ref, send_sems, recv_sems):
        _, _, _, chip, others = _mesh_place()
        for r, (ox, oy) in enumerate(others):
            copy(p_ref, g_ref, send_sems, recv_sems, r, 2 * ox + oy).wait_recv()
        for r in range(3):
            copy(p_ref, g_ref, send_sems, recv_sems, r, chip).wait_send()

    return dict(name="gather", operand=packed, copies=3, start=start, finish=finish,
                out_shape=jax.ShapeDtypeStruct((N_CHIPS,) + packed.shape, packed.dtype))


def _scatter_exchange(parts):
    flips = [(dx, dy, dc) for dx in (0, 1) for dy in (0, 1) for dc in (0, 1)][1:]

    def copy(p_ref, got_ref, send_sems, recv_sems, k, arriving):
        x, y, c, _, _ = _mesh_place()
        dx, dy, dc = flips[k]
        tx, ty, tc = (1 - x if dx else x), (1 - y if dy else y), (1 - c if dc else c)
        slot = 4 * tx + 2 * ty + tc if arriving else 4 * x + 2 * y + c
        return _remote(p_ref.at[2 * tx + ty, tc], got_ref.at[slot], send_sems, recv_sems, k, (tx, ty, tc))

    def start(p_ref, got_ref, send_sems, recv_sems):
        for k in range(len(flips)):
            copy(p_ref, got_ref, send_sems, recv_sems, k, False).start()

    def finish(p_ref, got_ref, send_sems, recv_sems):
        for k in range(len(flips)):
            copy(p_ref, got_ref, send_sems, recv_sems, k, True).wait_recv()
        for k in range(len(flips)):
            copy(p_ref, got_ref, send_sems, recv_sems, k, False).wait_send()

    return dict(name="scatter", operand=parts, copies=len(flips), start=start, finish=finish,
                out_shape=jax.ShapeDtypeStruct((2 * N_CHIPS,) + parts.shape[2:], parts.dtype))


def _all_gather_shards(packed, meta_shard):
    _, rh, cols = packed.shape

    def body(p_ref, m_ref, g_ref, gm_ref, send_sems, recv_sems):
        x, y, c, chip, others = _mesh_place()
        sibling = (x, y, 1 - c)
        sends = []
        for r, (ox, oy) in enumerate(others):
            sends.append(_remote(p_ref.at[c], g_ref.at[chip, c], send_sems, recv_sems, r, (ox, oy, c)))
            sends.append(_remote(m_ref, gm_ref.at[chip], send_sems, recv_sems, 6 + r, (ox, oy, c)))
        for cp in sends:
            cp.start()
        for r, (ox, oy) in enumerate(others):
            src_chip = 2 * ox + oy
            _remote(p_ref.at[c], g_ref.at[src_chip, c], send_sems, recv_sems, r, (ox, oy, c)).wait_recv()
            passed = _remote(g_ref.at[src_chip, c], g_ref.at[src_chip, c], send_sems, recv_sems, 3 + r, sibling)
            passed.start()
            sends.append(passed)
        for r, (ox, oy) in enumerate(others):
            src_chip = 2 * ox + oy
            _remote(p_ref.at[c], g_ref.at[src_chip, 1 - c], send_sems, recv_sems, 3 + r, sibling).wait_recv()
            _remote(m_ref, gm_ref.at[src_chip], send_sems, recv_sems, 6 + r, (ox, oy, c)).wait_recv()
        for cp in sends:
            cp.wait_send()

    gathered, meta_all = pl.pallas_call(
        body, name="all_gather_shards",
        in_specs=[_any_spec(), _any_spec()], out_specs=[_any_spec(), _any_spec()],
        out_shape=[jax.ShapeDtypeStruct((N_CHIPS, 2, rh, cols), packed.dtype),
                   jax.ShapeDtypeStruct((N_CHIPS,) + meta_shard.shape, meta_shard.dtype)],
        scratch_shapes=[pltpu.SemaphoreType.DMA((9,)), pltpu.SemaphoreType.DMA((9,))],
    )(packed, meta_shard)
    chip = 2 * lax.axis_index("x") + lax.axis_index("y")
    return (lax.dynamic_update_index_in_dim(gathered, packed, chip, 0),
            lax.dynamic_update_index_in_dim(meta_all, meta_shard, chip, 0))


def _pair_exchange(give):
    def body(give_ref, got_ref, send_sems, recv_sems):
        x, y, c, _, _ = _mesh_place()
        cp = _remote(give_ref, got_ref, send_sems, recv_sems, 0, (x, y, 1 - c))
        cp.start()
        cp.wait()

    return pl.pallas_call(
        body, name="pair_exchange", in_specs=[_any_spec()], out_specs=_any_spec(),
        out_shape=jax.ShapeDtypeStruct(give.shape, give.dtype),
        scratch_shapes=[pltpu.SemaphoreType.DMA((1,)), pltpu.SemaphoreType.DMA((1,))],
    )(give)


def _chip_exchange(parts):
    def body(p_ref, got_ref, send_sems, recv_sems):
        _, _, c, chip, others = _mesh_place()
        sends = [_remote(p_ref.at[2 * ox + oy], got_ref.at[chip], send_sems, recv_sems, r, (ox, oy, c))
                 for r, (ox, oy) in enumerate(others)]
        for cp in sends:
            cp.start()
        for r, (ox, oy) in enumerate(others):
            _remote(p_ref.at[chip], got_ref.at[2 * ox + oy], send_sems, recv_sems, r, (ox, oy, c)).wait_recv()
        for cp in sends:
            cp.wait_send()

    got = pl.pallas_call(
        body, name="chip_exchange", in_specs=[_any_spec()], out_specs=_any_spec(),
        out_shape=jax.ShapeDtypeStruct(parts.shape, parts.dtype),
        scratch_shapes=[pltpu.SemaphoreType.DMA((3,)), pltpu.SemaphoreType.DMA((3,))],
    )(parts)
    chip = 2 * lax.axis_index("x") + lax.axis_index("y")
    own = lax.dynamic_index_in_dim(parts, chip, 0, keepdims=False)
    return lax.dynamic_update_index_in_dim(got, own, chip, 0)


def _pair_gather(half):
    def body(h_ref, out_ref, send_sems, recv_sems):
        x, y, c, _, _ = _mesh_place()
        cp = _remote(h_ref, out_ref.at[c], send_sems, recv_sems, 0, (x, y, 1 - c))
        cp.start()
        _remote(h_ref, out_ref.at[1 - c], send_sems, recv_sems, 0, (x, y, 1 - c)).wait_recv()
        cp.wait_send()

    both = pl.pallas_call(
        body, name="pair_gather", in_specs=[_any_spec()], out_specs=_any_spec(),
        out_shape=jax.ShapeDtypeStruct((2,) + half.shape, half.dtype),
        scratch_shapes=[pltpu.SemaphoreType.DMA((1,)), pltpu.SemaphoreType.DMA((1,))],
    )(half)
    return lax.dynamic_update_index_in_dim(both, half, lax.axis_index("c"), 0)


def _row_tile(rows, limit=PACK_TILE):
    if rows <= limit:
        return rows
    for tr in range(limit, 7, -8):
        if rows % tr == 0:
            return tr
    return rows


def _pair_add(keep, got):
    n, rh, cols = keep.shape
    tr = _row_tile(rh)

    def body(k_ref, g_ref, o_ref):
        o_ref[...] = (k_ref[...].astype(F32) + g_ref[...].astype(F32)).astype(BF16)

    spec = pl.BlockSpec((1, tr, cols), lambda j, i: (j, i, 0))
    return pl.pallas_call(
        body, name="pair_add", grid=(n, rh // tr), in_specs=[spec, spec], out_specs=spec,
        out_shape=jax.ShapeDtypeStruct(keep.shape, BF16),
        compiler_params=_params(("parallel", "parallel")),
    )(keep, got)


def _chip_sum(parts):
    n, rh, cols = parts.shape
    tr = _row_tile(rh)

    def body(p_ref, o_ref):
        total = p_ref[0].astype(F32)
        for k in range(1, n):
            total = total + p_ref[k].astype(F32)
        o_ref[...] = total

    return pl.pallas_call(
        body, name="chip_sum", grid=(rh // tr,),
        in_specs=[pl.BlockSpec((n, tr, cols), lambda i: (0, i, 0))], out_specs=pl.BlockSpec((tr, cols), lambda i: (i, 0)),
        out_shape=jax.ShapeDtypeStruct((rh, cols), F32),
        compiler_params=_params(("parallel",)),
    )(parts)


def _reduce_scatter(grads, c):
    keep = lax.dynamic_index_in_dim(grads, c, axis=1, keepdims=False)
    give = lax.dynamic_index_in_dim(grads, 1 - c, axis=1, keepdims=False)
    chip_partial = _pair_add(keep, _pair_exchange(give))
    return _pair_gather(_chip_sum(_chip_exchange(chip_partial)))


def _all_reduce_small(meta_rows, small):
    b, rm, cols = meta_rows.shape
    rows = rm + small.shape[0]

    def body(meta_ref, small_ref, out_ref, mine, pair_buf, chip_buf, send_sems, recv_sems):
        x, y, c, chip, others = _mesh_place()
        acc = meta_ref[0]
        for i in range(1, b):
            acc = acc + meta_ref[i]
        mine[0:rm, :] = acc
        mine[rm:rows, :] = small_ref[...]
        pair = _remote(mine, pair_buf, send_sems, recv_sems, 0, (x, y, 1 - c))
        pair.start()
        pair.wait()
        chip_buf[chip] = mine[...] + pair_buf[...]
        sends = [_remote(chip_buf.at[chip], chip_buf.at[chip], send_sems, recv_sems, 1 + r, (ox, oy, c))
                 for r, (ox, oy) in enumerate(others)]
        for cp in sends:
            cp.start()
        for r, (ox, oy) in enumerate(others):
            _remote(chip_buf.at[chip], chip_buf.at[2 * ox + oy], send_sems, recv_sems, 1 + r, (ox, oy, c)).wait_recv()
        for cp in sends:
            cp.wait_send()
        out_ref[...] = ((chip_buf[0] + chip_buf[1]) + chip_buf[2]) + chip_buf[3]

    return pl.pallas_call(
        body, name="all_reduce_small",
        in_specs=[_vmem_spec(), _vmem_spec()], out_specs=_vmem_spec(),
        out_shape=jax.ShapeDtypeStruct((rows, cols), F32),
        scratch_shapes=[pltpu.VMEM((rows, cols), F32), pltpu.VMEM((rows, cols), F32), pltpu.VMEM((N_CHIPS, rows, cols), F32),
                        pltpu.SemaphoreType.DMA((4,)), pltpu.SemaphoreType.DMA((4,))],
        compiler_params=pltpu.CompilerParams(vmem_limit_bytes=VMEM_LIMIT),
    )(meta_rows, small)


def _adamw(w, g, m, v):
    shape = w.shape
    cols = shape[-1]
    rows = w.size // cols
    tr = _row_tile(rows)

    def body(w_ref, g_ref, m_ref, v_ref, d_ref, m2_ref, v2_ref):
        grad = g_ref[...]
        m2 = ADAM_B1 * m_ref[...] + (1.0 - ADAM_B1) * grad
        v2 = ADAM_B2 * v_ref[...] + (1.0 - ADAM_B2) * jnp.square(grad)
        m_hat = m2 / (1.0 - ADAM_B1 ** ADAM_STEP)
        v_hat = v2 / (1.0 - ADAM_B2 ** ADAM_STEP)
        d_ref[...] = -ADAM_LR * (m_hat / (jnp.sqrt(v_hat) + ADAM_EPS) + ADAM_WD * w_ref[...])
        m2_ref[...] = m2
        v2_ref[...] = v2

    spec = pl.BlockSpec((tr, cols), lambda i: (i, 0))
    out = jax.ShapeDtypeStruct((rows, cols), F32)
    res = pl.pallas_call(
        body, name="adamw", grid=(rows // tr,), in_specs=[spec] * 4, out_specs=[spec] * 3, out_shape=[out] * 3,
        compiler_params=_params(("parallel",)),
    )(*(a.reshape(rows, cols) for a in (w, g, m, v)))
    return tuple(r.reshape(shape) for r in res)


def _pack_rows(arrays):
    flat = [a.reshape(-1, PACK_COLS) for a in arrays]
    counts = [f.shape[0] for f in flat]
    total = sum(counts)
    half = -(-total // 2)
    tiles = -(-half // PACK_TILE)
    padded = 2 * tiles * _round_up(-(-half // tiles), 16)
    if padded > total:
        flat.append(jnp.zeros((padded - total, PACK_COLS), flat[0].dtype))
    return jnp.concatenate(flat, axis=0), counts


def _unpack_rows(buffer, counts, shapes):
    out, start = [], 0
    for n, shape in zip(counts, shapes):
        out.append(buffer[..., start:start + n, :].reshape(buffer.shape[:-2] + tuple(shape)))
        start += n
    return out


def _group_codec(entries, weights):
    shapes = [weights[n].shape[1:] for n, _ in entries]
    by_rows = [SHARD_AXIS[n] == 1 for n, _ in entries]
    packed, counts = _pack_rows([weights[n][li].astype(BF16) for n, li in entries])
    rows = packed.shape[0]
    pad_rows = rows - sum(counts)

    def unpack(per_chip_packed):
        out = {}
        for entry, (s0, s1), rowwise, blk in zip(entries, shapes, by_rows, _unpack_rows(per_chip_packed, counts, shapes)):
            out[entry] = blk.reshape(N_CHIPS * s0, s1) if rowwise else jnp.transpose(blk, (1, 0, 2)).reshape(s0, N_CHIPS * s1)
        return out

    def pack_grads(whole):
        pieces = []
        for entry, (s0, s1), rowwise in zip(entries, shapes, by_rows):
            g = whole[entry]
            by_chip = g.reshape(N_CHIPS, s0, s1) if rowwise else jnp.transpose(g.reshape(s0, N_CHIPS, s1), (1, 0, 2))
            pieces.append(by_chip.reshape(N_CHIPS, -1, PACK_COLS))
        if pad_rows:
            pieces.append(jnp.zeros((N_CHIPS, pad_rows, PACK_COLS), BF16))
        return jnp.concatenate(pieces, axis=1).reshape(N_CHIPS, 2, rows // 2, PACK_COLS)

    def unpack_reduced(reduced):
        return dict(zip(entries, _unpack_rows(reduced.reshape(rows, PACK_COLS), counts, shapes)))

    return dict(packed=packed, unpack=unpack, pack_grads=pack_grads, unpack_reduced=unpack_reduced)


def kernel(x, meta_tokens, norm_mix_g, w_in, pool_w, pool_scale, q_norm_g, kv_norm_g, w_uq, w_ukv, w_pa, w_pb, w_o, norm_ffn_g, w_gate, w_up, w_down, final_norm_g, loss_target, m_meta_tokens, m_norm_mix_g, m_w_in, m_pool_w, m_pool_scale, m_q_norm_g, m_kv_norm_g, m_w_uq, m_w_ukv, m_w_pa, m_w_pb, m_w_o, m_norm_ffn_g, m_w_gate, m_w_up, m_w_down, m_final_norm_g, v_meta_tokens, v_norm_mix_g, v_w_in, v_pool_w, v_pool_scale, v_q_norm_g, v_kv_norm_g, v_w_uq, v_w_ukv, v_w_pa, v_w_pb, v_w_o, v_norm_ffn_g, v_w_gate, v_w_up, v_w_down, v_final_norm_g):
    weights = dict(meta_tokens=meta_tokens, norm_mix_g=norm_mix_g, w_in=w_in, pool_w=pool_w, pool_scale=pool_scale,
                   q_norm_g=q_norm_g, kv_norm_g=kv_norm_g, w_uq=w_uq, w_ukv=w_ukv, w_pa=w_pa, w_pb=w_pb, w_o=w_o,
                   norm_ffn_g=norm_ffn_g, w_gate=w_gate, w_up=w_up, w_down=w_down, final_norm_g=final_norm_g)
    first = dict(meta_tokens=m_meta_tokens, norm_mix_g=m_norm_mix_g, w_in=m_w_in, pool_w=m_pool_w, pool_scale=m_pool_scale,
                 q_norm_g=m_q_norm_g, kv_norm_g=m_kv_norm_g, w_uq=m_w_uq, w_ukv=m_w_ukv, w_pa=m_w_pa, w_pb=m_w_pb, w_o=m_w_o,
                 norm_ffn_g=m_norm_ffn_g, w_gate=m_w_gate, w_up=m_w_up, w_down=m_w_down, final_norm_g=m_final_norm_g)
    second = dict(meta_tokens=v_meta_tokens, norm_mix_g=v_norm_mix_g, w_in=v_w_in, pool_w=v_pool_w, pool_scale=v_pool_scale,
                  q_norm_g=v_q_norm_g, kv_norm_g=v_kv_norm_g, w_uq=v_w_uq, w_ukv=v_w_ukv, w_pa=v_w_pa, w_pb=v_w_pb, w_o=v_w_o,
                  norm_ffn_g=v_norm_ffn_g, w_gate=v_w_gate, w_up=v_w_up, w_down=v_w_down, final_norm_g=v_final_norm_g)
    core = lax.axis_index("c")
    chip = 2 * lax.axis_index("x") + lax.axis_index("y")
    d = x.shape[-1]
    meta_cols = meta_tokens.shape[1]

    early_first = _group_codec([(n, 0) for n in EARLY_WEIGHTS], weights)
    riding = [_group_codec([(n, 0) for n in LATE_WEIGHTS] + [(n, 1) for n in EARLY_WEIGHTS], weights),
              _group_codec([(n, 1) for n in LATE_WEIGHTS], weights)]
    gathered, meta_all = _all_gather_shards(early_first["packed"].reshape(2, -1, PACK_COLS), meta_tokens)
    early_weights = early_first["unpack"](gathered.reshape(N_CHIPS, -1, PACK_COLS))
    meta_full = jnp.concatenate([meta_all[j] for j in range(N_CHIPS)], axis=1)
    small = {n: weights[n] for n in SMALL_WEIGHTS}

    loss, grad_x, d_meta_rows, g_small, early_partial, parts = _local_step(
        x, loss_target, meta_full, small, {n: early_weights[n, 0] for n in EARLY_WEIGHTS}, riding, early_first)

    shards = early_first["unpack_reduced"](_reduce_scatter(early_partial, core))
    for codec, from_all in zip(riding, parts):
        shards.update(codec["unpack_reduced"](_pair_gather(_chip_sum(from_all))))
    grads = {n: jnp.stack([shards[n, 0], shards[n, 1]]) for n in BIG_WEIGHTS}

    small_shapes = [weights[n].shape for n in SMALL_WEIGHTS]
    small_flat = jnp.concatenate([g_small[n].reshape(-1) for n in SMALL_WEIGHTS])
    small_len = small_flat.shape[0]
    small_rows = _round_up(-(-small_len // PACK_COLS), 8)
    small_pack = jnp.pad(small_flat, (0, small_rows * PACK_COLS - small_len)).reshape(small_rows, PACK_COLS)
    meta_rows = N_META * d // PACK_COLS
    summed = _all_reduce_small(d_meta_rows.reshape(-1, meta_rows, PACK_COLS), small_pack)
    grad_meta_full = summed[:meta_rows].reshape(N_META, d)
    grads["meta_tokens"] = lax.dynamic_slice_in_dim(grad_meta_full, chip * meta_cols, meta_cols, axis=1)
    small_sum = summed[meta_rows:].reshape(-1)
    start = 0
    for n, shape in zip(SMALL_WEIGHTS, small_shapes):
        size = 1
        for s in shape:
            size *= s
        grads[n] = small_sum[start:start + size].reshape(shape)
        start += size

    deltas, new_m, new_v = {}, {}, {}
    for n in WEIGHT_ORDER:
        deltas[n], new_m[n], new_v[n] = _adamw(weights[n], grads[n], first[n], second[n])

    total_loss = lax.psum(loss[0, 0], ("x", "y", "c"))
    return (total_loss, grad_x, *[grads[n] for n in WEIGHT_ORDER], *[deltas[n] for n in WEIGHT_ORDER],
            *[new_m[n] for n in WEIGHT_ORDER], *[new_v[n] for n in WEIGHT_ORDER])
```

```python
import functools

import jax
import jax.numpy as jnp
from jax import lax
from jax.experimental import pallas as pl
from jax.experimental.pallas import tpu as pltpu

F32 = jnp.float32
BF16 = jnp.bfloat16

N_META = 16
POOL_WINDOWS = (2, 4, 8, 16)
POOL_GROUP = 128
POOL_WIDTH = POOL_GROUP * len(POOL_WINDOWS)
QK_NOPE = 64
QK_ROPE = 32
V_DIM = 64
QK_DIM = QK_NOPE + QK_ROPE
Q_RANK = 256
KV_RANK = 128
HEAD_PAD = 128
SM_SCALE = QK_DIM ** -0.5
ROPE_THETA = 10000.0
NORM_EPS = 1e-6
MASK_VALUE = -1e30
Z_FIXED = POOL_WIDTH + Q_RANK + KV_RANK + HEAD_PAD

ADAM_LR = 0.001
ADAM_B1 = 0.9
ADAM_B2 = 0.999
ADAM_EPS = 1e-08
ADAM_WD = 0.01
ADAM_STEP = 10

N_CHIPS = 4
ATT_BLOCK = 256
SEQ_PAD = 128
ATT_FWD_HEADS = 4
ATT_BWD_HEADS = 4
PACK_COLS = 1024
PACK_TILE = 512
VMEM_LIMIT = 60 * 1024 * 1024

MESH = pl.DeviceIdType.MESH

BIG_WEIGHTS = ("w_in", "w_uq", "w_ukv", "w_pa", "w_pb", "w_o", "w_gate", "w_up", "w_down")
EARLY_WEIGHTS = ("w_in", "w_uq", "w_ukv")
LATE_WEIGHTS = ("w_pa", "w_pb", "w_o", "w_gate", "w_up", "w_down")
SHARD_AXIS = {"w_in": 2, "w_uq": 2, "w_ukv": 2, "w_pa": 2, "w_pb": 1, "w_o": 1, "w_gate": 2, "w_up": 2, "w_down": 1}
SMALL_WEIGHTS = ("norm_mix_g", "pool_w", "pool_scale", "q_norm_g", "kv_norm_g", "norm_ffn_g", "final_norm_g")
WEIGHT_ORDER = ("meta_tokens", "norm_mix_g", "w_in", "pool_w", "pool_scale", "q_norm_g", "kv_norm_g", "w_uq", "w_ukv",
                "w_pa", "w_pb", "w_o", "norm_ffn_g", "w_gate", "w_up", "w_down", "final_norm_g")


def _round_up(n, m):
    return -(-n // m) * m


def _vmem_spec():
    return pl.BlockSpec(memory_space=pltpu.VMEM)


def _any_spec():
    return pl.BlockSpec(memory_space=pl.ANY)


def _row_block(tm, width, col_block=0):
    return pl.BlockSpec((tm, width), lambda i, cb=col_block: (i, cb))


def _params(sem, vmem=VMEM_LIMIT):
    return pltpu.CompilerParams(dimension_semantics=sem, vmem_limit_bytes=vmem)


def _token_tile(t, want):
    best = SEQ_PAD
    for tm in range(32, min(t, 2 * want) + 1, 32):
        if t % tm == 0 and abs(tm - want) < abs(best - want):
            best = tm
    return best


def _dot(a, b):
    return jnp.dot(a, b, preferred_element_type=F32)


def _dot_nt(a, b):
    return lax.dot_general(a, b, (((1,), (1,)), ((), ())), preferred_element_type=F32)


def _dot_tn(a, b):
    return lax.dot_general(a, b, (((0,), (0,)), ((), ())), preferred_element_type=F32)


def _rms_fwd(x, g):
    r = lax.rsqrt(jnp.mean(x * x, axis=-1, keepdims=True) + NORM_EPS)
    xh = x * r
    return xh * g, xh, r


def _rms_bwd(dy, xh, r, g):
    gdy = dy * g
    dx = r * (gdy - xh * jnp.mean(xh * gdy, axis=-1, keepdims=True))
    return dx, dy * xh


def _rope_fwd(x, c, sa, sb):
    return x * c + pltpu.roll(x, 16, 1) * sa + pltpu.roll(x, HEAD_PAD - 16, 1) * sb


def _rope_bwd(d, c, sa, sb):
    return d * c + pltpu.roll(d * sa, HEAD_PAD - 16, 1) + pltpu.roll(d * sb, 16, 1)


def _in_proj_fwd(h, g, w_in_p):
    t, d = h.shape
    nz = w_in_p.shape[1]
    tm = _token_tile(t, 512)

    def body(h_ref, g_ref, w_ref, z_ref):
        hn, _, _ = _rms_fwd(h_ref[...], g_ref[...])
        z_ref[...] = _dot(hn.astype(BF16), w_ref[...]).astype(BF16)

    return pl.pallas_call(
        body, name="in_proj_fwd", grid=(t // tm,),
        in_specs=[_row_block(tm, d), _vmem_spec(), _vmem_spec()],
        out_specs=_row_block(tm, nz),
        out_shape=jax.ShapeDtypeStruct((t, nz), BF16),
        compiler_params=_params(("parallel",)),
    )(h, g, w_in_p)


def _window_sum(x, w, row, forward):
    n = x.shape[0]
    s = x
    k = 1
    while k < w:
        if forward:
            s = s + jnp.where(row >= k, pltpu.roll(s, k, 0), 0.0)
        else:
            s = s + jnp.where(row < n - k, pltpu.roll(s, n - k, 0), 0.0)
        k *= 2
    return s


def _pool_fwd(z3, pool_w, pool_scale):
    b, lp, _ = z3.shape

    def body(u_ref, pw_ref, sc_ref, a_ref):
        row = lax.broadcasted_iota(jnp.int32, (lp, POOL_GROUP), 0)
        pos = row.astype(F32)
        for gi, w in enumerate(POOL_WINDOWS):
            cols = slice(gi * POOL_GROUP, (gi + 1) * POOL_GROUP)
            u = u_ref[0, :, cols].astype(F32)
            y = _window_sum(u, w, row, True) / jnp.minimum(pos + 1.0, float(w)) - u
            yw = _dot(y.astype(BF16), pw_ref[gi])
            a_ref[0, :, cols] = (yw * sc_ref[:, cols]).astype(BF16)

    return pl.pallas_call(
        body, name="pool_fwd", grid=(b,),
        in_specs=[pl.BlockSpec((1, lp, POOL_WIDTH), lambda i: (i, 0, 0)), _vmem_spec(), _vmem_spec()],
        out_specs=pl.BlockSpec((1, lp, POOL_WIDTH), lambda i: (i, 0, 0)),
        out_shape=jax.ShapeDtypeStruct((b, lp, POOL_WIDTH), BF16),
        compiler_params=_params(("parallel",)),
    )(z3, pool_w, pool_scale)


def _qkv_fwd(z, g_q, g_kv, w_uq_p, w_kv_p, rope_c, rope_sa, rope_sb):
    t = z.shape[0]
    hw = w_uq_p.shape[1]
    heads = hw // HEAD_PAD
    tm = _token_tile(t, 512)

    def body(cq_ref, ckv_ref, kr_ref, gq_ref, gkv_ref, wq_ref, wkv_ref, c_ref, sa_ref, sb_ref, q_ref, k_ref, v_ref):
        c, sa, sb = c_ref[...], sa_ref[...], sb_ref[...]
        cqn, _, _ = _rms_fwd(cq_ref[...].astype(F32), gq_ref[...])
        qraw = _dot(cqn.astype(BF16), wq_ref[...])
        ckvn, _, _ = _rms_fwd(ckv_ref[...].astype(F32), gkv_ref[...])
        kvraw = _dot(ckvn.astype(BF16), wkv_ref[...])
        kr = kr_ref[...].astype(F32)
        for hd in range(heads):
            cols = slice(hd * HEAD_PAD, (hd + 1) * HEAD_PAD)
            q_ref[:, cols] = (_rope_fwd(qraw[:, cols], c, sa, sb) * SM_SCALE).astype(BF16)
            k_ref[:, cols] = _rope_fwd(kvraw[:, cols] + kr, c, sa, sb).astype(BF16)
        v_ref[...] = kvraw[:, hw:].astype(BF16)

    out = jax.ShapeDtypeStruct((t, hw), BF16)
    return pl.pallas_call(
        body, name="qkv_fwd", grid=(t // tm,),
        in_specs=[_row_block(tm, Q_RANK, POOL_WIDTH // Q_RANK),
                  _row_block(tm, KV_RANK, (POOL_WIDTH + Q_RANK) // KV_RANK),
                  _row_block(tm, HEAD_PAD, (POOL_WIDTH + Q_RANK + KV_RANK) // HEAD_PAD),
                  _vmem_spec(), _vmem_spec(), _vmem_spec(), _vmem_spec(),
                  _row_block(tm, HEAD_PAD), _row_block(tm, HEAD_PAD), _row_block(tm, HEAD_PAD)],
        out_specs=[_row_block(tm, hw)] * 3,
        out_shape=[out, out, out],
        compiler_params=_params(("parallel",)),
    )(z, z, z, g_q, g_kv, w_uq_p, w_kv_p, rope_c, rope_sa, rope_sb)


def _heads_per_step(heads, want):
    while heads % want:
        want //= 2
    return want


def _causal_mask(rows):
    row = lax.broadcasted_iota(jnp.int32, (rows, rows), 0)
    col = lax.broadcasted_iota(jnp.int32, (rows, rows), 1)
    return col <= row


def _attn_blocks(real_len):
    tail_start = (-(-real_len // ATT_BLOCK) - 1) * ATT_BLOCK
    return tail_start // ATT_BLOCK, tail_start, _round_up(real_len - tail_start, SEQ_PAD)


def _call_with_exchange(body, exchange, *, name, grid, in_specs, out_specs, out_shape, scratch_shapes, operands):
    if exchange is None:
        return pl.pallas_call(body, name=name, grid=grid, in_specs=in_specs, out_specs=out_specs, out_shape=out_shape,
                              scratch_shapes=scratch_shapes,
                              compiler_params=_params(("parallel",) + ("arbitrary",) * (len(grid) - 1)))(*operands)
    n_in, n_out, n_scratch = len(in_specs), len(out_specs), len(scratch_shapes)

    def riding(*refs):
        ins, src = refs[:n_in], refs[n_in]
        outs, dst = refs[n_in + 1:n_in + 1 + n_out], refs[n_in + 1 + n_out]
        scratch = refs[n_in + 2 + n_out:n_in + 2 + n_out + n_scratch]
        send_sems, recv_sems = refs[n_in + 2 + n_out + n_scratch:]
        steps = [pl.program_id(a) for a in range(len(grid))]

        @pl.when(functools.reduce(jnp.logical_and, [s == 0 for s in steps]))
        def _():
            exchange["start"](src, dst, send_sems, recv_sems)

        body(*ins, *outs, *scratch)

        @pl.when(functools.reduce(jnp.logical_and, [s == g - 1 for s, g in zip(steps, grid)]))
        def _():
            exchange["finish"](src, dst, send_sems, recv_sems)

    n = exchange["copies"]
    return pl.pallas_call(
        riding, name=name + "_" + exchange["name"], grid=grid,
        in_specs=list(in_specs) + [_any_spec()], out_specs=list(out_specs) + [_any_spec()],
        out_shape=list(out_shape) + [exchange["out_shape"]],
        scratch_shapes=list(scratch_shapes) + [pltpu.SemaphoreType.DMA((n,)), pltpu.SemaphoreType.DMA((n,))],
        compiler_params=_params(("arbitrary",) * len(grid)),
    )(*operands, exchange["operand"])


def _attn_fwd(q3, k3, v3, real_len, exchange=None):
    b, lp, hw = q3.shape
    heads = hw // HEAD_PAD
    tb = ATT_BLOCK
    nfull, tail_start, tail = _attn_blocks(real_len)
    done = tail_start + tail
    hpg = _heads_per_step(heads, ATT_FWD_HEADS)
    width = hpg * HEAD_PAD

    def body(q_ref, k_ref, v_ref, o_ref, lse_ref):
        group = pl.program_id(1)

        @pl.when(group == 0)
        def _():
            lse_ref[...] = jnp.zeros_like(lse_ref)

        def q_rows(r0, rows, whole_kv_blocks):
            def kv_step(c0, keys, states, mask):
                out = []
                for hd, (m, l, acc) in enumerate(states):
                    cols = slice(hd * HEAD_PAD, (hd + 1) * HEAD_PAD)
                    s = _dot_nt(q_ref[0, pl.ds(r0, rows), cols], k_ref[0, pl.ds(c0, keys), cols])
                    if mask is not None:
                        s = jnp.where(mask, s, MASK_VALUE)
                    m_new = jnp.maximum(m, jnp.max(s, axis=-1, keepdims=True))
                    alpha = jnp.exp(m - m_new)
                    p = jnp.exp(s - m_new)
                    l = alpha * l + jnp.sum(p, axis=-1, keepdims=True)
                    acc = alpha * acc + _dot(p.astype(BF16), v_ref[0, pl.ds(c0, keys), cols])
                    out.append((m_new, l, acc))
                return tuple(out)

            init = tuple((jnp.full((rows, 1), MASK_VALUE, F32), jnp.zeros((rows, 1), F32), jnp.zeros((rows, HEAD_PAD), F32))
                         for _ in range(hpg))
            states = lax.fori_loop(0, whole_kv_blocks, lambda j, st: kv_step(pl.multiple_of(j * tb, tb), tb, st, None), init)
            states = kv_step(r0, rows, states, _causal_mask(rows))
            lane = lax.broadcasted_iota(jnp.int32, (rows, HEAD_PAD), 1)
            lse_rows = lse_ref[0, pl.ds(r0, rows), :]
            for hd, (m, l, acc) in enumerate(states):
                o_ref[0, pl.ds(r0, rows), hd * HEAD_PAD:(hd + 1) * HEAD_PAD] = (acc / l).astype(BF16)
                lse_rows = jnp.where(lane == group * hpg + hd, m + jnp.log(l), lse_rows)
            lse_ref[0, pl.ds(r0, rows), :] = lse_rows

        def whole_block(i, carry):
            q_rows(pl.multiple_of(i * tb, tb), tb, i)
            return carry

        lax.fori_loop(0, nfull, whole_block, 0)
        q_rows(tail_start, tail, nfull)
        if done < lp:
            o_ref[0, done:lp, :] = jnp.zeros((lp - done, width), BF16)

    head_spec = pl.BlockSpec((1, lp, width), lambda bi, hi: (bi, 0, hi))
    return _call_with_exchange(
        body, exchange, name="attn_fwd", grid=(b, heads // hpg),
        in_specs=[head_spec, head_spec, head_spec],
        out_specs=[head_spec, pl.BlockSpec((1, lp, HEAD_PAD), lambda bi, hi: (bi, 0, 0))],
        out_shape=[jax.ShapeDtypeStruct((b, lp, hw), BF16), jax.ShapeDtypeStruct((b, lp, HEAD_PAD), F32)],
        scratch_shapes=[], operands=(q3, k3, v3))


def _merge_fwd(h, z, a, o, w_pa, w_pb_p, w_o):
    t, d = h.shape
    hw = o.shape[1]
    tm = _token_tile(t, 512)
    gate_block = Z_FIXED // d

    def body(h_ref, ga_ref, gb_ref, a_ref, o_ref, wpa_ref, wpb_ref, wo_ref, h1_ref, pa_ref, pb_ref):
        pa = _dot(a_ref[...], wpa_ref[...])
        pb = _dot(o_ref[...], wpb_ref[...])
        merged = jax.nn.sigmoid(ga_ref[...].astype(F32)) * pa + jax.nn.sigmoid(gb_ref[...].astype(F32)) * pb
        h1_ref[...] = h_ref[...] + _dot(merged.astype(BF16), wo_ref[...])
        pa_ref[...] = pa.astype(BF16)
        pb_ref[...] = pb.astype(BF16)

    return pl.pallas_call(
        body, name="merge_fwd", grid=(t // tm,),
        in_specs=[_row_block(tm, d), _row_block(tm, d, gate_block), _row_block(tm, d, gate_block + 1),
                  _row_block(tm, POOL_WIDTH), _row_block(tm, hw), _vmem_spec(), _vmem_spec(), _vmem_spec()],
        out_specs=[_row_block(tm, d)] * 3,
        out_shape=[jax.ShapeDtypeStruct((t, d), F32), jax.ShapeDtypeStruct((t, d), BF16), jax.ShapeDtypeStruct((t, d), BF16)],
        compiler_params=_params(("parallel",)),
    )(h, z, z, a, o, w_pa, w_pb_p, w_o)


def _ffn_fwd(h, g, w_gate, w_up, w_down):
    t, d = h.shape
    f = w_gate.shape[1]
    tm = _token_tile(t, 256)

    def body(h_ref, g_ref, wg_ref, wu_ref, wd_ref, h2_ref, a_ref, b_ref):
        x = h_ref[...]
        hn, _, _ = _rms_fwd(x, g_ref[...])
        hn = hn.astype(BF16)
        ga = _dot(hn, wg_ref[...])
        up = _dot(hn, wu_ref[...])
        act = ga * jax.nn.sigmoid(ga) * up
        h2_ref[...] = x + _dot(act.astype(BF16), wd_ref[...])
        a_ref[...] = ga.astype(BF16)
        b_ref[...] = up.astype(BF16)

    return pl.pallas_call(
        body, name="ffn_fwd", grid=(t // tm,),
        in_specs=[_row_block(tm, d), _vmem_spec(), _vmem_spec(), _vmem_spec(), _vmem_spec()],
        out_specs=[_row_block(tm, d), _row_block(tm, f), _row_block(tm, f)],
        out_shape=[jax.ShapeDtypeStruct((t, d), F32), jax.ShapeDtypeStruct((t, f), BF16), jax.ShapeDtypeStruct((t, f), BF16)],
        compiler_params=_params(("parallel",)),
    )(h, g, w_gate, w_up, w_down)


def _loss_head(h, g, target, valid):
    t, d = h.shape
    tm = _token_tile(t, 512)

    def body(h_ref, g_ref, t_ref, valid_ref, dh_ref, loss_ref, dg_ref):
        @pl.when(pl.program_id(0) == 0)
        def _():
            loss_ref[...] = jnp.zeros_like(loss_ref)
            dg_ref[...] = jnp.zeros_like(dg_ref)

        gain = g_ref[...]
        y, xh, r = _rms_fwd(h_ref[...], gain)
        err = (y - t_ref[...]) * valid_ref[...]
        per_row = jnp.sum(err * err, axis=-1, keepdims=True) / d
        loss_ref[...] += 0.5 * jnp.sum(per_row, axis=0, keepdims=True)
        dx, dg_rows = _rms_bwd(err / d, xh, r, gain)
        dh_ref[...] = dx
        dg_ref[...] += jnp.sum(dg_rows, axis=0, keepdims=True)

    return pl.pallas_call(
        body, name="loss_head", grid=(t // tm,),
        in_specs=[_row_block(tm, d), _vmem_spec(), _row_block(tm, d), _row_block(tm, 1)],
        out_specs=[_row_block(tm, d), pl.BlockSpec((1, 1), lambda i: (0, 0)), pl.BlockSpec((1, d), lambda i: (0, 0))],
        out_shape=[jax.ShapeDtypeStruct((t, d), F32), jax.ShapeDtypeStruct((1, 1), F32), jax.ShapeDtypeStruct((1, d), F32)],
        compiler_params=_params(("arbitrary",)),
    )(h, g, target, valid)


def _weight_grad(x, y, name):
    t, k = x.shape
    n = y.shape[1]
    tm = _token_tile(t, 512)
    tn = n
    while k * tn * 4 > 8 * 1024 * 1024 and tn % 256 == 0:
        tn //= 2
    steps = t // tm

    def body(x_ref, y_ref, o_ref, acc):
        @pl.when(pl.program_id(1) == 0)
        def _():
            acc[...] = jnp.zeros_like(acc)

        acc[...] += _dot_tn(x_ref[...].astype(BF16), y_ref[...].astype(BF16))

        @pl.when(pl.program_id(1) == steps - 1)
        def _():
            o_ref[...] = acc[...].astype(BF16)

    return pl.pallas_call(
        body, name=name, grid=(n // tn, steps),
        in_specs=[pl.BlockSpec((tm, k), lambda j, i: (i, 0)), pl.BlockSpec((tm, tn), lambda j, i: (i, j))],
        out_specs=pl.BlockSpec((k, tn), lambda j, i: (0, j)),
        out_shape=jax.ShapeDtypeStruct((k, n), BF16),
        scratch_shapes=[pltpu.VMEM((k, tn), F32)],
        compiler_params=_params(("parallel", "arbitrary")),
    )(x, y)


def _ffn_bwd(h, dh2, a, b, g, w_gate, w_up, w_down):
    t, d = h.shape
    f = a.shape[1]
    tm = _token_tile(t, 256)

    def body(h_ref, dh2_ref, a_ref, b_ref, g_ref, wg_ref, wu_ref, wd_ref, dh_ref, hn_ref, act_ref, da_ref, db_ref, dg_ref):
        @pl.when(pl.program_id(0) == 0)
        def _():
            dg_ref[...] = jnp.zeros_like(dg_ref)

        gain = g_ref[...]
        hn, xh, r = _rms_fwd(h_ref[...], gain)
        hn_ref[...] = hn.astype(BF16)
        dh2 = dh2_ref[...]
        dact = _dot_nt(dh2.astype(BF16), wd_ref[...])
        ga = a_ref[...].astype(F32)
        up = b_ref[...].astype(F32)
        sg = jax.nn.sigmoid(ga)
        silu = ga * sg
        act_ref[...] = (silu * up).astype(BF16)
        da = (dact * up * (sg * (1.0 + ga * (1.0 - sg)))).astype(BF16)
        db = (dact * silu).astype(BF16)
        da_ref[...] = da
        db_ref[...] = db
        dhn = _dot_nt(da, wg_ref[...]) + _dot_nt(db, wu_ref[...])
        dx, dg_rows = _rms_bwd(dhn, xh, r, gain)
        dh_ref[...] = dh2 + dx
        dg_ref[...] += jnp.sum(dg_rows, axis=0, keepdims=True)

    return pl.pallas_call(
        body, name="ffn_bwd", grid=(t // tm,),
        in_specs=[_row_block(tm, d), _row_block(tm, d), _row_block(tm, f), _row_block(tm, f),
                  _vmem_spec(), _vmem_spec(), _vmem_spec(), _vmem_spec()],
        out_specs=[_row_block(tm, d), _row_block(tm, d), _row_block(tm, f), _row_block(tm, f), _row_block(tm, f),
                   pl.BlockSpec((1, d), lambda i: (0, 0))],
        out_shape=[jax.ShapeDtypeStruct((t, d), F32), jax.ShapeDtypeStruct((t, d), BF16), jax.ShapeDtypeStruct((t, f), BF16),
                   jax.ShapeDtypeStruct((t, f), BF16), jax.ShapeDtypeStruct((t, f), BF16), jax.ShapeDtypeStruct((1, d), F32)],
        compiler_params=_params(("arbitrary",)),
    )(h, dh2, a, b, g, w_gate, w_up, w_down)


def _merge_bwd(dh1, z, pa, pb, w_o, w_pa, w_pb_p):
    t, d = dh1.shape
    hw = w_pb_p.shape[0]
    tm = _token_tile(t, 512)
    gate_block = Z_FIXED // d

    def body(dh_ref, ga_ref, gb_ref, pa_ref, pb_ref, wo_ref, wpa_ref, wpb_ref,
             mg_ref, dpa_ref, dpb_ref, dga_ref, dgb_ref, da_ref, do_ref):
        dm = _dot_nt(dh_ref[...].astype(BF16), wo_ref[...])
        sa = jax.nn.sigmoid(ga_ref[...].astype(F32))
        sb = jax.nn.sigmoid(gb_ref[...].astype(F32))
        pa = pa_ref[...].astype(F32)
        pb = pb_ref[...].astype(F32)
        mg_ref[...] = (sa * pa + sb * pb).astype(BF16)
        dpa = (dm * sa).astype(BF16)
        dpb = (dm * sb).astype(BF16)
        dpa_ref[...] = dpa
        dpb_ref[...] = dpb
        dga_ref[...] = (dm * pa * (sa * (1.0 - sa))).astype(BF16)
        dgb_ref[...] = (dm * pb * (sb * (1.0 - sb))).astype(BF16)
        da_ref[...] = _dot_nt(dpa, wpa_ref[...]).astype(BF16)
        do_ref[...] = _dot_nt(dpb, wpb_ref[...]).astype(BF16)

    wide = jax.ShapeDtypeStruct((t, d), BF16)
    return pl.pallas_call(
        body, name="merge_bwd", grid=(t // tm,),
        in_specs=[_row_block(tm, d), _row_block(tm, d, gate_block), _row_block(tm, d, gate_block + 1),
                  _row_block(tm, d), _row_block(tm, d), _vmem_spec(), _vmem_spec(), _vmem_spec()],
        out_specs=[_row_block(tm, d)] * 5 + [_row_block(tm, POOL_WIDTH), _row_block(tm, hw)],
        out_shape=[wide] * 5 + [jax.ShapeDtypeStruct((t, POOL_WIDTH), BF16), jax.ShapeDtypeStruct((t, hw), BF16)],
        compiler_params=_params(("parallel",)),
    )(dh1, z, z, pa, pb, w_o, w_pa, w_pb_p)


def _attn_bwd(q3, k3, v3, o3, do3, lse3, real_len, exchange=None):
    b, lp, hw = q3.shape
    heads = hw // HEAD_PAD
    tb = ATT_BLOCK
    nfull, tail_start, tail = _attn_blocks(real_len)
    done = tail_start + tail
    hpg = _heads_per_step(heads, ATT_BWD_HEADS)
    width = hpg * HEAD_PAD

    def body(q_ref, k_ref, v_ref, o_ref, do_ref, lse_ref, dq_ref, dk_ref, dv_ref, dqt_acc, lse_row, delta_row):
        group = pl.program_id(1)
        lse_t = jnp.transpose(lse_ref[0])
        head_of_row = lax.broadcasted_iota(jnp.int32, (HEAD_PAD, lp), 0)
        for hd in range(hpg):
            cols = slice(hd * HEAD_PAD, (hd + 1) * HEAD_PAD)
            lse_row[hd] = jnp.sum(jnp.where(head_of_row == group * hpg + hd, lse_t, 0.0), axis=0, keepdims=True)
            prod = do_ref[0, :, cols].astype(F32) * o_ref[0, :, cols].astype(F32)
            delta_row[hd] = jnp.sum(jnp.transpose(prod), axis=0, keepdims=True)
        dqt_acc[...] = jnp.zeros_like(dqt_acc)

        def kv_rows(c0, keys, whole_q_blocks_from):
            k_t = [jnp.transpose(k_ref[0, pl.ds(c0, keys), hd * HEAD_PAD:(hd + 1) * HEAD_PAD].astype(F32)).astype(BF16)
                   for hd in range(hpg)]

            def q_step(r0, rows, states, mask):
                out = []
                for hd, (dk, dv) in enumerate(states):
                    cols = slice(hd * HEAD_PAD, (hd + 1) * HEAD_PAD)
                    q = q_ref[0, pl.ds(r0, rows), cols]
                    do = do_ref[0, pl.ds(r0, rows), cols]
                    s_t = _dot_nt(k_ref[0, pl.ds(c0, keys), cols], q)
                    if mask is not None:
                        s_t = jnp.where(mask, s_t, MASK_VALUE)
                    p_t = jnp.exp(s_t - lse_row[hd, :, pl.ds(r0, rows)])
                    dp_t = _dot_nt(v_ref[0, pl.ds(c0, keys), cols], do)
                    ds_t = (p_t * (dp_t - delta_row[hd, :, pl.ds(r0, rows)])).astype(BF16)
                    dv = dv + _dot(p_t.astype(BF16), do)
                    dk = dk + _dot(ds_t, q)
                    dqt_acc[cols, pl.ds(r0, rows)] += _dot(k_t[hd], ds_t)
                    out.append((dk, dv))
                return tuple(out)

            zero = jnp.zeros((keys, HEAD_PAD), F32)
            key_pos = lax.broadcasted_iota(jnp.int32, (keys, keys), 0)
            query_pos = lax.broadcasted_iota(jnp.int32, (keys, keys), 1)
            states = q_step(c0, keys, tuple((zero, zero) for _ in range(hpg)), key_pos <= query_pos)
            if whole_q_blocks_from is not None:
                states = lax.fori_loop(whole_q_blocks_from, nfull,
                                       lambda i, st: q_step(pl.multiple_of(i * tb, tb), tb, st, None), states)
                states = q_step(tail_start, tail, states, None)
            for hd, (dk, dv) in enumerate(states):
                cols = slice(hd * HEAD_PAD, (hd + 1) * HEAD_PAD)
                dk_ref[0, pl.ds(c0, keys), cols] = dk.astype(BF16)
                dv_ref[0, pl.ds(c0, keys), cols] = dv.astype(BF16)

        def whole_block(j, carry):
            kv_rows(pl.multiple_of(j * tb, tb), tb, j + 1)
            return carry

        lax.fori_loop(0, nfull, whole_block, 0)
        kv_rows(tail_start, tail, None)
        if done < lp:
            dk_ref[0, done:lp, :] = jnp.zeros((lp - done, width), BF16)
            dv_ref[0, done:lp, :] = jnp.zeros((lp - done, width), BF16)
        for hd in range(hpg):
            cols = slice(hd * HEAD_PAD, (hd + 1) * HEAD_PAD)
            dq_ref[0, :, cols] = jnp.transpose(dqt_acc[cols, :]).astype(BF16)

    head_spec = pl.BlockSpec((1, lp, width), lambda bi, hi: (bi, 0, hi))
    out = jax.ShapeDtypeStruct((b, lp, hw), BF16)
    return _call_with_exchange(
        body, exchange, name="attn_bwd", grid=(b, heads // hpg),
        in_specs=[head_spec] * 5 + [pl.BlockSpec((1, lp, HEAD_PAD), lambda bi, hi: (bi, 0, 0))],
        out_specs=[head_spec] * 3,
        out_shape=[out, out, out],
        scratch_shapes=[pltpu.VMEM((width, lp), F32), pltpu.VMEM((hpg, 1, lp), F32), pltpu.VMEM((hpg, 1, lp), F32)],
        operands=(q3, k3, v3, o3, do3, lse3))


def _qkv_bwd(dq, dk, dv, z, g_q, g_kv, w_uq_p, w_kv_p, rope_c, rope_sa, rope_sb):
    t, hw = dq.shape
    heads = hw // HEAD_PAD
    tm = _token_tile(t, 512)

    def body(dq_ref, dk_ref, dv_ref, cq_ref, ckv_ref, gq_ref, gkv_ref, wq_ref, wkv_ref, c_ref, sa_ref, sb_ref,
             dqraw_ref, dkvraw_ref, cqn_ref, ckvn_ref, dcq_ref, dckv_ref, dkr_ref, dgq_ref, dgkv_ref):
        @pl.when(pl.program_id(0) == 0)
        def _():
            dgq_ref[...] = jnp.zeros_like(dgq_ref)
            dgkv_ref[...] = jnp.zeros_like(dgkv_ref)

        c, sa, sb = c_ref[...], sa_ref[...], sb_ref[...]
        dkr = jnp.zeros((tm, HEAD_PAD), F32)
        for hd in range(heads):
            cols = slice(hd * HEAD_PAD, (hd + 1) * HEAD_PAD)
            dqraw_ref[:, cols] = _rope_bwd(dq_ref[:, cols].astype(F32) * SM_SCALE, c, sa, sb).astype(BF16)
            dkraw = _rope_bwd(dk_ref[:, cols].astype(F32), c, sa, sb)
            dkvraw_ref[:, cols] = dkraw.astype(BF16)
            dkr = dkr + dkraw
        dkvraw_ref[:, hw:] = dv_ref[...]
        lane = lax.broadcasted_iota(jnp.int32, (tm, HEAD_PAD), 1)
        dkr_ref[...] = jnp.where((lane >= QK_NOPE) & (lane < QK_DIM), dkr, 0.0).astype(BF16)

        gq = gq_ref[...]
        cqn, xh, r = _rms_fwd(cq_ref[...].astype(F32), gq)
        cqn_ref[...] = cqn.astype(BF16)
        dx, dg_rows = _rms_bwd(_dot_nt(dqraw_ref[...], wq_ref[...]), xh, r, gq)
        dcq_ref[...] = dx.astype(BF16)
        dgq_ref[...] += jnp.sum(dg_rows, axis=0, keepdims=True)

        gkv = gkv_ref[...]
        ckvn, xh, r = _rms_fwd(ckv_ref[...].astype(F32), gkv)
        ckvn_ref[...] = ckvn.astype(BF16)
        dx, dg_rows = _rms_bwd(_dot_nt(dkvraw_ref[...], wkv_ref[...]), xh, r, gkv)
        dckv_ref[...] = dx.astype(BF16)
        dgkv_ref[...] += jnp.sum(dg_rows, axis=0, keepdims=True)

    def shape(width, dtype=BF16):
        return jax.ShapeDtypeStruct((t, width), dtype)

    return pl.pallas_call(
        body, name="qkv_bwd", grid=(t // tm,),
        in_specs=[_row_block(tm, hw)] * 3
        + [_row_block(tm, Q_RANK, POOL_WIDTH // Q_RANK), _row_block(tm, KV_RANK, (POOL_WIDTH + Q_RANK) // KV_RANK)]
        + [_vmem_spec()] * 4 + [_row_block(tm, HEAD_PAD)] * 3,
        out_specs=[_row_block(tm, hw), _row_block(tm, 2 * hw), _row_block(tm, Q_RANK), _row_block(tm, KV_RANK),
                   _row_block(tm, Q_RANK), _row_block(tm, KV_RANK), _row_block(tm, HEAD_PAD),
                   pl.BlockSpec((1, Q_RANK), lambda i: (0, 0)), pl.BlockSpec((1, KV_RANK), lambda i: (0, 0))],
        out_shape=[shape(hw), shape(2 * hw), shape(Q_RANK), shape(KV_RANK), shape(Q_RANK), shape(KV_RANK), shape(HEAD_PAD),
                   jax.ShapeDtypeStruct((1, Q_RANK), F32), jax.ShapeDtypeStruct((1, KV_RANK), F32)],
        compiler_params=_params(("arbitrary",)),
    )(dq, dk, dv, z, z, g_q, g_kv, w_uq_p, w_kv_p, rope_c, rope_sa, rope_sb)


def _pool_bwd(z3, da3, pool_w, pool_scale):
    b, lp, _ = z3.shape
    groups = len(POOL_WINDOWS)

    def body(u_ref, da_ref, pw_ref, sc_ref, du_ref, dpw_ref, dsc_ref):
        @pl.when(pl.program_id(0) == 0)
        def _():
            dpw_ref[...] = jnp.zeros_like(dpw_ref)
            dsc_ref[...] = jnp.zeros_like(dsc_ref)

        row = lax.broadcasted_iota(jnp.int32, (lp, POOL_GROUP), 0)
        pos = row.astype(F32)
        for gi, w in enumerate(POOL_WINDOWS):
            cols = slice(gi * POOL_GROUP, (gi + 1) * POOL_GROUP)
            count = jnp.minimum(pos + 1.0, float(w))
            u = u_ref[0, :, cols].astype(F32)
            y = (_window_sum(u, w, row, True) / count - u).astype(BF16)
            yw = _dot(y, pw_ref[gi])
            da = da_ref[0, :, cols].astype(F32)
            dsc_ref[:, cols] += jnp.sum(da * yw, axis=0, keepdims=True)
            dyw = (da * sc_ref[:, cols]).astype(BF16)
            dpw_ref[gi] += _dot_tn(y, dyw)
            dy = _dot_nt(dyw, pw_ref[gi])
            du_ref[0, :, cols] = (_window_sum(dy / count, w, row, False) - dy).astype(BF16)

    return pl.pallas_call(
        body, name="pool_bwd", grid=(b,),
        in_specs=[pl.BlockSpec((1, lp, POOL_WIDTH), lambda i: (i, 0, 0)), pl.BlockSpec((1, lp, POOL_WIDTH), lambda i: (i, 0, 0)),
                  _vmem_spec(), _vmem_spec()],
        out_specs=[pl.BlockSpec((1, lp, POOL_WIDTH), lambda i: (i, 0, 0)),
                   pl.BlockSpec((groups, POOL_GROUP, POOL_GROUP), lambda i: (0, 0, 0)),
                   pl.BlockSpec((1, POOL_WIDTH), lambda i: (0, 0))],
        out_shape=[jax.ShapeDtypeStruct((b, lp, POOL_WIDTH), BF16), jax.ShapeDtypeStruct((groups, POOL_GROUP, POOL_GROUP), F32),
                   jax.ShapeDtypeStruct((1, POOL_WIDTH), F32)],
        compiler_params=_params(("arbitrary",)),
    )(z3, da3, pool_w, pool_scale)


def _in_proj_bwd(h, dh1, du, dcq, dckv, dkr, dga, dgb, g, w_in_p):
    t, d = h.shape
    nz = w_in_p.shape[1]
    tm = _token_tile(t, 512)
    widths = (POOL_WIDTH, Q_RANK, KV_RANK, HEAD_PAD, d, d)

    def body(h_ref, dh1_ref, du_ref, dcq_ref, dckv_ref, dkr_ref, dga_ref, dgb_ref, g_ref, w_ref, dh_ref, hn_ref, dz_ref, dg_ref):
        @pl.when(pl.program_id(0) == 0)
        def _():
            dg_ref[...] = jnp.zeros_like(dg_ref)

        gain = g_ref[...]
        hn, xh, r = _rms_fwd(h_ref[...], gain)
        hn_ref[...] = hn.astype(BF16)
        dhn = jnp.zeros((tm, d), F32)
        start = 0
        for piece, width in zip((du_ref, dcq_ref, dckv_ref, dkr_ref, dga_ref, dgb_ref), widths):
            val = piece[...]
            dz_ref[:, start:start + width] = val
            dhn = dhn + _dot_nt(val, w_ref[:, start:start + width])
            start += width
        dx, dg_rows = _rms_bwd(dhn, xh, r, gain)
        dh_ref[...] = dh1_ref[...] + dx
        dg_ref[...] += jnp.sum(dg_rows, axis=0, keepdims=True)

    return pl.pallas_call(
        body, name="in_proj_bwd", grid=(t // tm,),
        in_specs=[_row_block(tm, d), _row_block(tm, d)] + [_row_block(tm, w) for w in widths] + [_vmem_spec(), _vmem_spec()],
        out_specs=[_row_block(tm, d), _row_block(tm, d), _row_block(tm, nz), pl.BlockSpec((1, d), lambda i: (0, 0))],
        out_shape=[jax.ShapeDtypeStruct((t, d), F32), jax.ShapeDtypeStruct((t, d), BF16), jax.ShapeDtypeStruct((t, nz), BF16),
                   jax.ShapeDtypeStruct((1, d), F32)],
        compiler_params=_params(("arbitrary",)),
    )(h, dh1, du, dcq, dckv, dkr, dga, dgb, g, w_in_p)


def _pad_heads(w, heads, width):
    k = w.shape[0]
    w = w.reshape(k, heads, width)
    return jnp.pad(w, ((0, 0), (0, 0), (0, HEAD_PAD - width))).reshape(k, heads * HEAD_PAD)


def _unpad_heads(w, heads, width):
    k = w.shape[0]
    return w.reshape(k, heads, HEAD_PAD)[:, :, :width].reshape(k, heads * width)


def _early_layouts(w, heads):
    o3, o4 = POOL_WIDTH + Q_RANK + KV_RANK, POOL_WIDTH + Q_RANK + KV_RANK + QK_ROPE
    w_in = w["w_in"]
    rope_cols = jnp.pad(w_in[:, o3:o4], ((0, 0), (QK_NOPE, HEAD_PAD - QK_DIM)))
    w_in_p = jnp.concatenate([w_in[:, :o3], rope_cols, w_in[:, o4:]], axis=1)
    w_uq_p = _pad_heads(w["w_uq"], heads, QK_DIM)
    kv = w["w_ukv"].reshape(KV_RANK, heads, QK_NOPE + V_DIM)
    w_k = jnp.pad(kv[:, :, :QK_NOPE], ((0, 0), (0, 0), (0, HEAD_PAD - QK_NOPE))).reshape(KV_RANK, heads * HEAD_PAD)
    w_v = jnp.pad(kv[:, :, QK_NOPE:], ((0, 0), (0, 0), (0, HEAD_PAD - V_DIM))).reshape(KV_RANK, heads * HEAD_PAD)
    return dict(w_in_p=w_in_p, w_uq_p=w_uq_p, w_kv_p=jnp.concatenate([w_k, w_v], axis=1))


def _late_layouts(w, heads):
    d = w["w_pb"].shape[1]
    w_pb_p = jnp.pad(w["w_pb"].reshape(heads, V_DIM, d), ((0, 0), (0, HEAD_PAD - V_DIM), (0, 0))).reshape(heads * HEAD_PAD, d)
    return dict(w_pa=w["w_pa"], w_pb_p=w_pb_p, w_o=w["w_o"], w_gate=w["w_gate"], w_up=w["w_up"], w_down=w["w_down"])


def _early_grad_layouts(g, heads):
    o3 = POOL_WIDTH + Q_RANK + KV_RANK
    gin = g["w_in_p"]
    w_in = jnp.concatenate([gin[:, :o3], gin[:, o3 + QK_NOPE:o3 + QK_DIM], gin[:, o3 + HEAD_PAD:]], axis=1)
    hw = heads * HEAD_PAD
    gk = g["w_kv_p"][:, :hw].reshape(KV_RANK, heads, HEAD_PAD)[:, :, :QK_NOPE]
    gv = g["w_kv_p"][:, hw:].reshape(KV_RANK, heads, HEAD_PAD)[:, :, :V_DIM]
    w_ukv = jnp.concatenate([gk, gv], axis=2).reshape(KV_RANK, heads * (QK_NOPE + V_DIM))
    return dict(w_in=w_in, w_uq=_unpad_heads(g["w_uq_p"], heads, QK_DIM), w_ukv=w_ukv)


def _late_grad_layouts(g, heads):
    d = g["w_pb_p"].shape[1]
    w_pb = g["w_pb_p"].reshape(heads, HEAD_PAD, d)[:, :V_DIM].reshape(heads * V_DIM, d)
    return dict(w_pa=g["w_pa"], w_pb=w_pb, w_o=g["w_o"], w_gate=g["w_gate"], w_up=g["w_up"], w_down=g["w_down"])


def _rope_tables(lp, b):
    inv = 1.0 / (ROPE_THETA ** (jnp.arange(0, QK_ROPE, 2, dtype=F32) / QK_ROPE))
    ang = jnp.arange(lp, dtype=F32)[:, None] * inv[None, :]
    cos, sin = jnp.cos(ang), jnp.sin(ang)
    half = QK_ROPE // 2
    ones = jnp.ones((lp, QK_NOPE), F32)
    zeros_lo = jnp.zeros((lp, QK_NOPE), F32)
    zeros_hi = jnp.zeros((lp, HEAD_PAD - QK_DIM), F32)
    zeros_half = jnp.zeros((lp, half), F32)
    c = jnp.concatenate([ones, cos, cos, zeros_hi], axis=1)
    sa = jnp.concatenate([zeros_lo, zeros_half, sin, zeros_hi], axis=1)
    sb = jnp.concatenate([zeros_lo, -sin, zeros_half, zeros_hi], axis=1)
    return tuple(jnp.tile(tab, (b, 1)) for tab in (c, sa, sb))


def _local_step(x, loss_target, meta_tokens, small, early_first, riding, early_first_codec):
    b, seq, d = x.shape
    depth = 2
    heads = early_first["w_uq"].shape[1] // QK_DIM
    core = lax.axis_index("c")
    chip = 2 * lax.axis_index("x") + lax.axis_index("y")
    real_len = N_META + seq
    lp = _round_up(real_len, SEQ_PAD)
    t = b * lp
    pad = lp - N_META - seq

    meta = jnp.broadcast_to(meta_tokens[None], (b, N_META, d))
    h = jnp.concatenate([meta, x, jnp.zeros((b, pad, d), F32)], axis=1).reshape(t, d)
    target = jnp.pad(loss_target, ((0, 0), (N_META, pad), (0, 0))).reshape(t, d)
    pos = jnp.arange(lp)
    valid = jnp.tile(((pos >= N_META) & (pos < N_META + seq)).astype(F32), b).reshape(t, 1)
    rope_c, rope_sa, rope_sb = _rope_tables(lp, b)

    layers = []
    for li in range(depth):
        lay = dict(pool_w=small["pool_w"][li].astype(BF16), pool_scale=small["pool_scale"][li][None])
        for n in ("norm_mix_g", "q_norm_g", "kv_norm_g", "norm_ffn_g"):
            lay[n] = small[n][li][None]
        layers.append(lay)
    layers[0].update(_early_layouts(early_first, heads))

    saved = []
    for li in range(depth):
        lay = layers[li]
        z = _in_proj_fwd(h, lay["norm_mix_g"], lay["w_in_p"])
        a = _pool_fwd(z.reshape(b, lp, -1), lay["pool_w"], lay["pool_scale"]).reshape(t, POOL_WIDTH)
        q, k, v = _qkv_fwd(z, lay["q_norm_g"], lay["kv_norm_g"], lay["w_uq_p"], lay["w_kv_p"], rope_c, rope_sa, rope_sb)
        hw = q.shape[1]
        packed = riding[li]["packed"]
        o3, lse, others = _attn_fwd(q.reshape(b, lp, hw), k.reshape(b, lp, hw), v.reshape(b, lp, hw), real_len,
                                    _gather_exchange(packed))
        arrived = riding[li]["unpack"](lax.dynamic_update_index_in_dim(others, packed, chip, 0))
        lay.update(_late_layouts({n: arrived[n, li] for n in LATE_WEIGHTS}, heads))
        if li + 1 < depth:
            layers[li + 1].update(_early_layouts({n: arrived[n, li + 1] for n in EARLY_WEIGHTS}, heads))
        o = o3.reshape(t, hw)
        h1, pa, pb = _merge_fwd(h, z, a, o, lay["w_pa"], lay["w_pb_p"], lay["w_o"])
        h2, fa, fb = _ffn_fwd(h1, lay["norm_ffn_g"], lay["w_gate"], lay["w_up"], lay["w_down"])
        saved.append(dict(h=h, z=z, a=a, q=q, k=k, v=v, o=o, lse=lse, pa=pa, pb=pb, h1=h1, fa=fa, fb=fb))
        h = h2

    dh, loss, d_final = _loss_head(h, small["final_norm_g"][None], target, valid)

    g_small = {n: [] for n in SMALL_WEIGHTS if n != "final_norm_g"}
    early_grads, parts = {}, [None] * depth
    for li in reversed(range(depth)):
        lay, sv = layers[li], saved[li]
        hw = sv["q"].shape[1]
        dh1, hn_f, act, dfa, dfb, dg_ffn = _ffn_bwd(sv["h1"], dh, sv["fa"], sv["fb"], lay["norm_ffn_g"],
                                                     lay["w_gate"], lay["w_up"], lay["w_down"])
        gl = dict(w_gate=_weight_grad(hn_f, dfa, "grad_w_gate"), w_up=_weight_grad(hn_f, dfb, "grad_w_up"),
                  w_down=_weight_grad(act, dh, "grad_w_down"))
        merged, dpa, dpb, dga, dgb, da, do = _merge_bwd(dh1, sv["z"], sv["pa"], sv["pb"], lay["w_o"], lay["w_pa"], lay["w_pb_p"])
        gl["w_o"] = _weight_grad(merged, dh1, "grad_w_o")
        gl["w_pa"] = _weight_grad(sv["a"], dpa, "grad_w_pa")
        gl["w_pb_p"] = _weight_grad(sv["o"], dpb, "grad_w_pb")
        to_send = {(n, li): g for n, g in _late_grad_layouts(gl, heads).items()}
        if li + 1 < depth:
            to_send.update({(n, li + 1): g for n, g in early_grads[li + 1].items()})
        sending = riding[li]["pack_grads"](to_send)
        shape3 = (b, lp, hw)
        dq3, dk3, dv3, from_others = _attn_bwd(
            sv["q"].reshape(shape3), sv["k"].reshape(shape3), sv["v"].reshape(shape3), sv["o"].reshape(shape3),
            do.reshape(shape3), sv["lse"], real_len, _scatter_exchange(sending))
        own = lax.dynamic_index_in_dim(lax.dynamic_index_in_dim(sending, chip, 0, keepdims=False), core, 0, keepdims=False)
        parts[li] = lax.dynamic_update_index_in_dim(from_others, own, 2 * chip + core, 0)
        dqraw, dkvraw, cqn, ckvn, dcq, dckv, dkr, dg_q, dg_kv = _qkv_bwd(
            dq3.reshape(t, hw), dk3.reshape(t, hw), dv3.reshape(t, hw), sv["z"], lay["q_norm_g"], lay["kv_norm_g"],
            lay["w_uq_p"], lay["w_kv_p"], rope_c, rope_sa, rope_sb)
        gl["w_uq_p"] = _weight_grad(cqn, dqraw, "grad_w_uq")
        gl["w_kv_p"] = _weight_grad(ckvn, dkvraw, "grad_w_ukv")
        du3, dpool_w, dpool_scale = _pool_bwd(sv["z"].reshape(b, lp, -1), da.reshape(b, lp, POOL_WIDTH),
                                              lay["pool_w"], lay["pool_scale"])
        dh, hn_m, dz, dg_mix = _in_proj_bwd(sv["h"], dh1, du3.reshape(t, POOL_WIDTH), dcq, dckv, dkr, dga, dgb,
                                            lay["norm_mix_g"], lay["w_in_p"])
        gl["w_in_p"] = _weight_grad(hn_m, dz, "grad_w_in")
        early_grads[li] = _early_grad_layouts(gl, heads)
        for n, val in (("norm_mix_g", dg_mix[0]), ("pool_w", dpool_w), ("pool_scale", dpool_scale[0]), ("q_norm_g", dg_q[0]),
                       ("kv_norm_g", dg_kv[0]), ("norm_ffn_g", dg_ffn[0])):
            g_small[n].insert(0, val)

    dh3 = dh.reshape(b, lp, d)
    grad_x = dh3[:, N_META:N_META + seq]
    d_meta_rows = dh3[:, :N_META]
    g_small = {n: jnp.stack(v) for n, v in g_small.items()}
    g_small["final_norm_g"] = d_final[0]
    early_first_partial = early_first_codec["pack_grads"]({(n, 0): g for n, g in early_grads[0].items()})
    return loss, grad_x, d_meta_rows, g_small, early_first_partial, parts


def _mesh_place():
    x, y, c = lax.axis_index("x"), lax.axis_index("y"), lax.axis_index("c")
    others = [(1 - x, y), (x, 1 - y), (1 - x, 1 - y)]
    return x, y, c, 2 * x + y, others


def _remote(src, dst, send_sems, recv_sems, k, device):
    return pltpu.make_async_remote_copy(src_ref=src, dst_ref=dst, send_sem=send_sems.at[k], recv_sem=recv_sems.at[k],
                                        device_id=device, device_id_type=MESH)


def _gather_exchange(packed):
    def copy(p_ref, g_ref, send_sems, recv_sems, r, slot):
        _, _, c, _, others = _mesh_place()
        ox, oy = others[r]
        return _remote(p_ref, g_ref.at[slot], send_sems, recv_sems, r, (ox, oy, c))

    def start(p_ref, g_ref, send_sems, recv_sems):
        chip = _mesh_place()[3]
        for r in range(3):
            copy(p_ref, g_ref, send_sems, recv_sems, r, chip).start()

    def finish(p_ref, g_ref, send_sems, recv_sems):
        _, _, _, chip, others = _mesh_place()
        for r, (ox, oy) in enumerate(others):
            copy(p_ref, g_ref, send_sems, recv_sems, r, 2 * ox + oy).wait_recv()
        for r in range(3):
            copy(p_ref, g_ref, send_sems, recv_sems, r, chip).wait_send()

    return dict(name="gather", operand=packed, copies=3, start=start, finish=finish,
                out_shape=jax.ShapeDtypeStruct((N_CHIPS,) + packed.shape, packed.dtype))


def _scatter_exchange(parts):
    flips = [(dx, dy, dc) for dx in (0, 1) for dy in (0, 1) for dc in (0, 1)][1:]

    def copy(p_ref, got_ref, send_sems, recv_sems, k, arriving):
        x, y, c, _, _ = _mesh_place()
        dx, dy, dc = flips[k]
        tx, ty, tc = (1 - x if dx else x), (1 - y if dy else y), (1 - c if dc else c)
        slot = 4 * tx + 2 * ty + tc if arriving else 4 * x + 2 * y + c
        return _remote(p_ref.at[2 * tx + ty, tc], got_ref.at[slot], send_sems, recv_sems, k, (tx, ty, tc))

    def start(p_ref, got_ref, send_sems, recv_sems):
        for k in range(len(flips)):
            copy(p_ref, got_ref, send_sems, recv_sems, k, False).start()

    def finish(p_ref, got_ref, send_sems, recv_sems):
        for k in range(len(flips)):
            copy(p_ref, got_ref, send_sems, recv_sems, k, True).wait_recv()
        for k in range(len(flips)):
            copy(p_ref, got_ref, send_sems, recv_sems, k, False).wait_send()

    return dict(name="scatter", operand=parts, copies=len(flips), start=start, finish=finish,
                out_shape=jax.ShapeDtypeStruct((2 * N_CHIPS,) + parts.shape[2:], parts.dtype))


def _all_gather_shards(packed, meta_shard):
    _, rh, cols = packed.shape

    def body(p_ref, m_ref, g_ref, gm_ref, send_sems, recv_sems):
        x, y, c, chip, others = _mesh_place()
        sibling = (x, y, 1 - c)
        sends = []
        for r, (ox, oy) in enumerate(others):
            sends.append(_remote(p_ref.at[c], g_ref.at[chip, c], send_sems, recv_sems, r, (ox, oy, c)))
            sends.append(_remote(m_ref, gm_ref.at[chip], send_sems, recv_sems, 6 + r, (ox, oy, c)))
        for cp in sends:
            cp.start()
        for r, (ox, oy) in enumerate(others):
            src_chip = 2 * ox + oy
            _remote(p_ref.at[c], g_ref.at[src_chip, c], send_sems, recv_sems, r, (ox, oy, c)).wait_recv()
            passed = _remote(g_ref.at[src_chip, c], g_ref.at[src_chip, c], send_sems, recv_sems, 3 + r, sibling)
            passed.start()
            sends.append(passed)
        for r, (ox, oy) in enumerate(others):
            src_chip = 2 * ox + oy
            _remote(p_ref.at[c], g_ref.at[src_chip, 1 - c], send_sems, recv_sems, 3 + r, sibling).wait_recv()
            _remote(m_ref, gm_ref.at[src_chip], send_sems, recv_sems, 6 + r, (ox, oy, c)).wait_recv()
        for cp in sends:
            cp.wait_send()

    gathered, meta_all = pl.pallas_call(
        body, name="all_gather_shards",
        in_specs=[_any_spec(), _any_spec()], out_specs=[_any_spec(), _any_spec()],
        out_shape=[jax.ShapeDtypeStruct((N_CHIPS, 2, rh, cols), packed.dtype),
                   jax.ShapeDtypeStruct((N_CHIPS,) + meta_shard.shape, meta_shard.dtype)],
        scratch_shapes=[pltpu.SemaphoreType.DMA((9,)), pltpu.SemaphoreType.DMA((9,))],
    )(packed, meta_shard)
    chip = 2 * lax.axis_index("x") + lax.axis_index("y")
    return (lax.dynamic_update_index_in_dim(gathered, packed, chip, 0),
            lax.dynamic_update_index_in_dim(meta_all, meta_shard, chip, 0))


def _pair_exchange(give):
    def body(give_ref, got_ref, send_sems, recv_sems):
        x, y, c, _, _ = _mesh_place()
        cp = _remote(give_ref, got_ref, send_sems, recv_sems, 0, (x, y, 1 - c))
        cp.start()
        cp.wait()

    return pl.pallas_call(
        body, name="pair_exchange", in_specs=[_any_spec()], out_specs=_any_spec(),
        out_shape=jax.ShapeDtypeStruct(give.shape, give.dtype),
        scratch_shapes=[pltpu.SemaphoreType.DMA((1,)), pltpu.SemaphoreType.DMA((1,))],
    )(give)


def _chip_exchange(parts):
    def body(p_ref, got_ref, send_sems, recv_sems):
        _, _, c, chip, others = _mesh_place()
        sends = [_remote(p_ref.at[2 * ox + oy], got_ref.at[chip], send_sems, recv_sems, r, (ox, oy, c))
                 for r, (ox, oy) in enumerate(others)]
        for cp in sends:
            cp.start()
        for r, (ox, oy) in enumerate(others):
            _remote(p_ref.at[chip], got_ref.at[2 * ox + oy], send_sems, recv_sems, r, (ox, oy, c)).wait_recv()
        for cp in sends:
            cp.wait_send()

    got = pl.pallas_call(
        body, name="chip_exchange", in_specs=[_any_spec()], out_specs=_any_spec(),
        out_shape=jax.ShapeDtypeStruct(parts.shape, parts.dtype),
        scratch_shapes=[pltpu.SemaphoreType.DMA((3,)), pltpu.SemaphoreType.DMA((3,))],
    )(parts)
    chip = 2 * lax.axis_index("x") + lax.axis_index("y")
    own = lax.dynamic_index_in_dim(parts, chip, 0, keepdims=False)
    return lax.dynamic_update_index_in_dim(got, own, chip, 0)


def _pair_gather(half):
    def body(h_ref, out_ref, send_sems, recv_sems):
        x, y, c, _, _ = _mesh_place()
        cp = _remote(h_ref, out_ref.at[c], send_sems, recv_sems, 0, (x, y, 1 - c))
        cp.start()
        _remote(h_ref, out_ref.at[1 - c], send_sems, recv_sems, 0, (x, y, 1 - c)).wait_recv()
        cp.wait_send()

    both = pl.pallas_call(
        body, name="pair_gather", in_specs=[_any_spec()], out_specs=_any_spec(),
        out_shape=jax.ShapeDtypeStruct((2,) + half.shape, half.dtype),
        scratch_shapes=[pltpu.SemaphoreType.DMA((1,)), pltpu.SemaphoreType.DMA((1,))],
    )(half)
    return lax.dynamic_update_index_in_dim(both, half, lax.axis_index("c"), 0)


def _row_tile(rows, limit=PACK_TILE):
    if rows <= limit:
        return rows
    for tr in range(limit, 7, -8):
        if rows % tr == 0:
            return tr
    return rows


def _pair_add(keep, got):
    n, rh, cols = keep.shape
    tr = _row_tile(rh)

    def body(k_ref, g_ref, o_ref):
        o_ref[...] = (k_ref[...].astype(F32) + g_ref[...].astype(F32)).astype(BF16)

    spec = pl.BlockSpec((1, tr, cols), lambda j, i: (j, i, 0))
    return pl.pallas_call(
        body, name="pair_add", grid=(n, rh // tr), in_specs=[spec, spec], out_specs=spec,
        out_shape=jax.ShapeDtypeStruct(keep.shape, BF16),
        compiler_params=_params(("parallel", "parallel")),
    )(keep, got)


def _chip_sum(parts):
    n, rh, cols = parts.shape
    tr = _row_tile(rh)

    def body(p_ref, o_ref):
        total = p_ref[0].astype(F32)
        for k in range(1, n):
            total = total + p_ref[k].astype(F32)
        o_ref[...] = total

    return pl.pallas_call(
        body, name="chip_sum", grid=(rh // tr,),
        in_specs=[pl.BlockSpec((n, tr, cols), lambda i: (0, i, 0))], out_specs=pl.BlockSpec((tr, cols), lambda i: (i, 0)),
        out_shape=jax.ShapeDtypeStruct((rh, cols), F32),
        compiler_params=_params(("parallel",)),
    )(parts)


def _reduce_scatter(grads, c):
    keep = lax.dynamic_index_in_dim(grads, c, axis=1, keepdims=False)
    give = lax.dynamic_index_in_dim(grads, 1 - c, axis=1, keepdims=False)
    chip_partial = _pair_add(keep, _pair_exchange(give))
    return _pair_gather(_chip_sum(_chip_exchange(chip_partial)))


def _all_reduce_small(meta_rows, small):
    b, rm, cols = meta_rows.shape
    rows = rm + small.shape[0]

    def body(meta_ref, small_ref, out_ref, mine, pair_buf, chip_buf, send_sems, recv_sems):
        x, y, c, chip, others = _mesh_place()
        acc = meta_ref[0]
        for i in range(1, b):
            acc = acc + meta_ref[i]
        mine[0:rm, :] = acc
        mine[rm:rows, :] = small_ref[...]
        pair = _remote(mine, pair_buf, send_sems, recv_sems, 0, (x, y, 1 - c))
        pair.start()
        pair.wait()
        chip_buf[chip] = mine[...] + pair_buf[...]
        sends = [_remote(chip_buf.at[chip], chip_buf.at[chip], send_sems, recv_sems, 1 + r, (ox, oy, c))
                 for r, (ox, oy) in enumerate(others)]
        for cp in sends:
            cp.start()
        for r, (ox, oy) in enumerate(others):
            _remote(chip_buf.at[chip], chip_buf.at[2 * ox + oy], send_sems, recv_sems, 1 + r, (ox, oy, c)).wait_recv()
        for cp in sends:
            cp.wait_send()
        out_ref[...] = ((chip_buf[0] + chip_buf[1]) + chip_buf[2]) + chip_buf[3]

    return pl.pallas_call(
        body, name="all_reduce_small",
        in_specs=[_vmem_spec(), _vmem_spec()], out_specs=_vmem_spec(),
        out_shape=jax.ShapeDtypeStruct((rows, cols), F32),
        scratch_shapes=[pltpu.VMEM((rows, cols), F32), pltpu.VMEM((rows, cols), F32), pltpu.VMEM((N_CHIPS, rows, cols), F32),
                        pltpu.SemaphoreType.DMA((4,)), pltpu.SemaphoreType.DMA((4,))],
        compiler_params=pltpu.CompilerParams(vmem_limit_bytes=VMEM_LIMIT),
    )(meta_rows, small)


def _adamw(w, g, m, v):
    shape = w.shape
    cols = shape[-1]
    rows = w.size // cols
    tr = _row_tile(rows)

    def body(w_ref, g_ref, m_ref, v_ref, d_ref, m2_ref, v2_ref):
        grad = g_ref[...]
        m2 = ADAM_B1 * m_ref[...] + (1.0 - ADAM_B1) * grad
        v2 = ADAM_B2 * v_ref[...] + (1.0 - ADAM_B2) * jnp.square(grad)
        m_hat = m2 / (1.0 - ADAM_B1 ** ADAM_STEP)
        v_hat = v2 / (1.0 - ADAM_B2 ** ADAM_STEP)
        d_ref[...] = -ADAM_LR * (m_hat / (jnp.sqrt(v_hat) + ADAM_EPS) + ADAM_WD * w_ref[...])
        m2_ref[...] = m2
        v2_ref[...] = v2

    spec = pl.BlockSpec((tr, cols), lambda i: (i, 0))
    out = jax.ShapeDtypeStruct((rows, cols), F32)
    res = pl.pallas_call(
        body, name="adamw", grid=(rows // tr,), in_specs=[spec] * 4, out_specs=[spec] * 3, out_shape=[out] * 3,
        compiler_params=_params(("parallel",)),
    )(*(a.reshape(rows, cols) for a in (w, g, m, v)))
    return tuple(r.reshape(shape) for r in res)


def _pack_rows(arrays):
    flat = [a.reshape(-1, PACK_COLS) for a in arrays]
    counts = [f.shape[0] for f in flat]
    total = sum(counts)
    half = -(-total // 2)
    tiles = -(-half // PACK_TILE)
    padded = 2 * tiles * _round_up(-(-half // tiles), 16)
    if padded > total:
        flat.append(jnp.zeros((padded - total, PACK_COLS), flat[0].dtype))
    return jnp.concatenate(flat, axis=0), counts


def _unpack_rows(buffer, counts, shapes):
    out, start = [], 0
    for n, shape in zip(counts, shapes):
        out.append(buffer[..., start:start + n, :].reshape(buffer.shape[:-2] + tuple(shape)))
        start += n
    return out


def _group_codec(entries, weights):
    shapes = [weights[n].shape[1:] for n, _ in entries]
    by_rows = [SHARD_AXIS[n] == 1 for n, _ in entries]
    packed, counts = _pack_rows([weights[n][li].astype(BF16) for n, li in entries])
    rows = packed.shape[0]
    pad_rows = rows - sum(counts)

    def unpack(per_chip_packed):
        out = {}
        for entry, (s0, s1), rowwise, blk in zip(entries, shapes, by_rows, _unpack_rows(per_chip_packed, counts, shapes)):
            out[entry] = blk.reshape(N_CHIPS * s0, s1) if rowwise else jnp.transpose(blk, (1, 0, 2)).reshape(s0, N_CHIPS * s1)
        return out

    def pack_grads(whole):
        pieces = []
        for entry, (s0, s1), rowwise in zip(entries, shapes, by_rows):
            g = whole[entry]
            by_chip = g.reshape(N_CHIPS, s0, s1) if rowwise else jnp.transpose(g.reshape(s0, N_CHIPS, s1), (1, 0, 2))
            pieces.append(by_chip.reshape(N_CHIPS, -1, PACK_COLS))
        if pad_rows:
            pieces.append(jnp.zeros((N_CHIPS, pad_rows, PACK_COLS), BF16))
        return jnp.concatenate(pieces, axis=1).reshape(N_CHIPS, 2, rows // 2, PACK_COLS)

    def unpack_reduced(reduced):
        return dict(zip(entries, _unpack_rows(reduced.reshape(rows, PACK_COLS), counts, shapes)))

    return dict(packed=packed, unpack=unpack, pack_grads=pack_grads, unpack_reduced=unpack_reduced)


def kernel(x, meta_tokens, norm_mix_g, w_in, pool_w, pool_scale, q_norm_g, kv_norm_g, w_uq, w_ukv, w_pa, w_pb, w_o, norm_ffn_g, w_gate, w_up, w_down, final_norm_g, loss_target, m_meta_tokens, m_norm_mix_g, m_w_in, m_pool_w, m_pool_scale, m_q_norm_g, m_kv_norm_g, m_w_uq, m_w_ukv, m_w_pa, m_w_pb, m_w_o, m_norm_ffn_g, m_w_gate, m_w_up, m_w_down, m_final_norm_g, v_meta_tokens, v_norm_mix_g, v_w_in, v_pool_w, v_pool_scale, v_q_norm_g, v_kv_norm_g, v_w_uq, v_w_ukv, v_w_pa, v_w_pb, v_w_o, v_norm_ffn_g, v_w_gate, v_w_up, v_w_down, v_final_norm_g):
    weights = dict(meta_tokens=meta_tokens, norm_mix_g=norm_mix_g, w_in=w_in, pool_w=pool_w, pool_scale=pool_scale,
                   q_norm_g=q_norm_g, kv_norm_g=kv_norm_g, w_uq=w_uq, w_ukv=w_ukv, w_pa=w_pa, w_pb=w_pb, w_o=w_o,
                   norm_ffn_g=norm_ffn_g, w_gate=w_gate, w_up=w_up, w_down=w_down, final_norm_g=final_norm_g)
    first = dict(meta_tokens=m_meta_tokens, norm_mix_g=m_norm_mix_g, w_in=m_w_in, pool_w=m_pool_w, pool_scale=m_pool_scale,
                 q_norm_g=m_q_norm_g, kv_norm_g=m_kv_norm_g, w_uq=m_w_uq, w_ukv=m_w_ukv, w_pa=m_w_pa, w_pb=m_w_pb, w_o=m_w_o,
                 norm_ffn_g=m_norm_ffn_g, w_gate=m_w_gate, w_up=m_w_up, w_down=m_w_down, final_norm_g=m_final_norm_g)
    second = dict(meta_tokens=v_meta_tokens, norm_mix_g=v_norm_mix_g, w_in=v_w_in, pool_w=v_pool_w, pool_scale=v_pool_scale,
                  q_norm_g=v_q_norm_g, kv_norm_g=v_kv_norm_g, w_uq=v_w_uq, w_ukv=v_w_ukv, w_pa=v_w_pa, w_pb=v_w_pb, w_o=v_w_o,
                  norm_ffn_g=v_norm_ffn_g, w_gate=v_w_gate, w_up=v_w_up, w_down=v_w_down, final_norm_g=v_final_norm_g)
    core = lax.axis_index("c")
    chip = 2 * lax.axis_index("x") + lax.axis_index("y")
    d = x.shape[-1]
    meta_cols = meta_tokens.shape[1]

    early_first = _group_codec([(n, 0) for n in EARLY_WEIGHTS], weights)
    riding = [_group_codec([(n, 0) for n in LATE_WEIGHTS] + [(n, 1) for n in EARLY_WEIGHTS], weights),
              _group_codec([(n, 1) for n in LATE_WEIGHTS], weights)]
    gathered, meta_all = _all_gather_shards(early_first["packed"].reshape(2, -1, PACK_COLS), meta_tokens)
    early_weights = early_first["unpack"](gathered.reshape(N_CHIPS, -1, PACK_COLS))
    meta_full = jnp.concatenate([meta_all[j] for j in range(N_CHIPS)], axis=1)
    small = {n: weights[n] for n in SMALL_WEIGHTS}

    loss, grad_x, d_meta_rows, g_small, early_partial, parts = _local_step(
        x, loss_target, meta_full, small, {n: early_weights[n, 0] for n in EARLY_WEIGHTS}, riding, early_first)

    shards = early_first["unpack_reduced"](_reduce_scatter(early_partial, core))
    for codec, from_all in zip(riding, parts):
        shards.update(codec["unpack_reduced"](_pair_gather(_chip_sum(from_all))))
    grads = {n: jnp.stack([shards[n, 0], shards[n, 1]]) for n in BIG_WEIGHTS}

    small_shapes = [weights[n].shape for n in SMALL_WEIGHTS]
    small_flat = jnp.concatenate([g_small[n].reshape(-1) for n in SMALL_WEIGHTS])
    small_len = small_flat.shape[0]
    small_rows = _round_up(-(-small_len // PACK_COLS), 8)
    small_pack = jnp.pad(small_flat, (0, small_rows * PACK_COLS - small_len)).reshape(small_rows, PACK_COLS)
    meta_rows = N_META * d // PACK_COLS
    summed = _all_reduce_small(d_meta_rows.reshape(-1, meta_rows, PACK_COLS), small_pack)
    grad_meta_full = summed[:meta_rows].reshape(N_META, d)
    grads["meta_tokens"] = lax.dynamic_slice_in_dim(grad_meta_full, chip * meta_cols, meta_cols, axis=1)
    small_sum = summed[meta_rows:].reshape(-1)
    start = 0
    for n, shape in zip(SMALL_WEIGHTS, small_shapes):
        size = 1
        for s in shape:
            size *= s
        grads[n] = small_sum[start:start + size].reshape(shape)
        start += size

    deltas, new_m, new_v = {}, {}, {}
    for n in WEIGHT_ORDER:
        deltas[n], new_m[n], new_v[n] = _adamw(weights[n], grads[n], first[n], second[n])

    total_loss = lax.psum(loss[0, 0], ("x", "y", "c"))
    return (total_loss, grad_x, *[grads[n] for n in WEIGHT_ORDER], *[deltas[n] for n in WEIGHT_ORDER],
            *[new_m[n] for n in WEIGHT_ORDER], *[new_v[n] for n in WEIGHT_ORDER])
```

```python
import functools

import jax
import jax.numpy as jnp
from jax import lax
from jax.experimental import pallas as pl
from jax.experimental.pallas import tpu as pltpu

F32 = jnp.float32
BF16 = jnp.bfloat16

N_META = 16
POOL_WINDOWS = (2, 4, 8, 16)
POOL_GROUP = 128
POOL_WIDTH = POOL_GROUP * len(POOL_WINDOWS)
QK_NOPE = 64
QK_ROPE = 32
V_DIM = 64
QK_DIM = QK_NOPE + QK_ROPE
Q_RANK = 256
KV_RANK = 128
HEAD_PAD = 128
SM_SCALE = QK_DIM ** -0.5
ROPE_THETA = 10000.0
NORM_EPS = 1e-6
MASK_VALUE = -1e30
Z_FIXED = POOL_WIDTH + Q_RANK + KV_RANK + HEAD_PAD

ADAM_LR = 0.001
ADAM_B1 = 0.9
ADAM_B2 = 0.999
ADAM_EPS = 1e-08
ADAM_WD = 0.01
ADAM_STEP = 10

N_CHIPS = 4
ATT_BLOCK = 256
SEQ_PAD = 128
ATT_FWD_HEADS = 4
ATT_BWD_HEADS = 4
PACK_COLS = 1024
PACK_TILE = 512
VMEM_LIMIT = 60 * 1024 * 1024

MESH = pl.DeviceIdType.MESH

BIG_WEIGHTS = ("w_in", "w_uq", "w_ukv", "w_pa", "w_pb", "w_o", "w_gate", "w_up", "w_down")
EARLY_WEIGHTS = ("w_in", "w_uq", "w_ukv")
LATE_WEIGHTS = ("w_pa", "w_pb", "w_o", "w_gate", "w_up", "w_down")
TRANSPOSED_WEIGHTS = ("w_in", "w_gate", "w_up")
SHARD_AXIS = {"w_in": 2, "w_uq": 2, "w_ukv": 2, "w_pa": 2, "w_pb": 1, "w_o": 1, "w_gate": 2, "w_up": 2, "w_down": 1}
SMALL_WEIGHTS = ("norm_mix_g", "pool_w", "pool_scale", "q_norm_g", "kv_norm_g", "norm_ffn_g", "final_norm_g")
WEIGHT_ORDER = ("meta_tokens", "norm_mix_g", "w_in", "pool_w", "pool_scale", "q_norm_g", "kv_norm_g", "w_uq", "w_ukv",
                "w_pa", "w_pb", "w_o", "norm_ffn_g", "w_gate", "w_up", "w_down", "final_norm_g")


def _round_up(n, m):
    return -(-n // m) * m


def _vmem_spec():
    return pl.BlockSpec(memory_space=pltpu.VMEM)


def _any_spec():
    return pl.BlockSpec(memory_space=pl.ANY)


def _row_block(tm, width, col_block=0):
    return pl.BlockSpec((tm, width), lambda i, cb=col_block: (i, cb))


def _params(sem, vmem=VMEM_LIMIT):
    return pltpu.CompilerParams(dimension_semantics=sem, vmem_limit_bytes=vmem)


def _token_tile(t, want):
    best = SEQ_PAD
    for tm in range(32, min(t, 2 * want) + 1, 32):
        if t % tm == 0 and abs(tm - want) < abs(best - want):
            best = tm
    return best


def _dot(a, b):
    return jnp.dot(a, b, preferred_element_type=F32)


def _dot_nt(a, b):
    return lax.dot_general(a, b, (((1,), (1,)), ((), ())), preferred_element_type=F32)


def _dot_tn(a, b):
    return lax.dot_general(a, b, (((0,), (0,)), ((), ())), preferred_element_type=F32)


def _rms_fwd(x, g):
    r = lax.rsqrt(jnp.mean(x * x, axis=-1, keepdims=True) + NORM_EPS)
    xh = x * r
    return xh * g, xh, r


def _rms_bwd(dy, xh, r, g):
    gdy = dy * g
    dx = r * (gdy - xh * jnp.mean(xh * gdy, axis=-1, keepdims=True))
    return dx, dy * xh


def _rope_fwd(x, c, sa, sb):
    return x * c + pltpu.roll(x, 16, 1) * sa + pltpu.roll(x, HEAD_PAD - 16, 1) * sb


def _rope_bwd(d, c, sa, sb):
    return d * c + pltpu.roll(d * sa, HEAD_PAD - 16, 1) + pltpu.roll(d * sb, 16, 1)


def _in_proj_fwd(h, g, w_in_pt):
    t, d = h.shape
    nz = w_in_pt.shape[0]
    tm = _token_tile(t, 512)

    def body(h_ref, g_ref, w_ref, z_ref):
        hn, _, _ = _rms_fwd(h_ref[...], g_ref[...])
        z_ref[...] = _dot_nt(hn.astype(BF16), w_ref[...]).astype(BF16)

    return pl.pallas_call(
        body, name="in_proj_fwd", grid=(t // tm,),
        in_specs=[_row_block(tm, d), _vmem_spec(), _vmem_spec()],
        out_specs=_row_block(tm, nz),
        out_shape=jax.ShapeDtypeStruct((t, nz), BF16),
        compiler_params=_params(("parallel",)),
    )(h, g, w_in_pt)


def _window_sum(x, w, row, forward):
    n = x.shape[0]
    s = x
    k = 1
    while k < w:
        if forward:
            s = s + jnp.where(row >= k, pltpu.roll(s, k, 0), 0.0)
        else:
            s = s + jnp.where(row < n - k, pltpu.roll(s, n - k, 0), 0.0)
        k *= 2
    return s


def _pool_fwd(z3, pool_w, pool_scale):
    b, lp, _ = z3.shape

    def body(u_ref, pw_ref, sc_ref, a_ref):
        row = lax.broadcasted_iota(jnp.int32, (lp, POOL_GROUP), 0)
        pos = row.astype(F32)
        for gi, w in enumerate(POOL_WINDOWS):
            cols = slice(gi * POOL_GROUP, (gi + 1) * POOL_GROUP)
            u = u_ref[0, :, cols].astype(F32)
            y = _window_sum(u, w, row, True) / jnp.minimum(pos + 1.0, float(w)) - u
            yw = _dot(y.astype(BF16), pw_ref[gi])
            a_ref[0, :, cols] = (yw * sc_ref[:, cols]).astype(BF16)

    return pl.pallas_call(
        body, name="pool_fwd", grid=(b,),
        in_specs=[pl.BlockSpec((1, lp, POOL_WIDTH), lambda i: (i, 0, 0)), _vmem_spec(), _vmem_spec()],
        out_specs=pl.BlockSpec((1, lp, POOL_WIDTH), lambda i: (i, 0, 0)),
        out_shape=jax.ShapeDtypeStruct((b, lp, POOL_WIDTH), BF16),
        compiler_params=_params(("parallel",)),
    )(z3, pool_w, pool_scale)


def _qkv_fwd(z, g_q, g_kv, w_uq_p, w_kv_p, rope_c, rope_sa, rope_sb):
    t = z.shape[0]
    hw = w_uq_p.shape[1]
    heads = hw // HEAD_PAD
    tm = _token_tile(t, 512)

    def body(cq_ref, ckv_ref, kr_ref, gq_ref, gkv_ref, wq_ref, wkv_ref, c_ref, sa_ref, sb_ref, q_ref, k_ref, v_ref):
        c, sa, sb = c_ref[...], sa_ref[...], sb_ref[...]
        cqn, _, _ = _rms_fwd(cq_ref[...].astype(F32), gq_ref[...])
        qraw = _dot(cqn.astype(BF16), wq_ref[...])
        ckvn, _, _ = _rms_fwd(ckv_ref[...].astype(F32), gkv_ref[...])
        kvraw = _dot(ckvn.astype(BF16), wkv_ref[...])
        kr = kr_ref[...].astype(F32)
        for hd in range(heads):
            cols = slice(hd * HEAD_PAD, (hd + 1) * HEAD_PAD)
            q_ref[:, cols] = (_rope_fwd(qraw[:, cols], c, sa, sb) * SM_SCALE).astype(BF16)
            k_ref[:, cols] = _rope_fwd(kvraw[:, cols] + kr, c, sa, sb).astype(BF16)
        v_ref[...] = kvraw[:, hw:].astype(BF16)

    out = jax.ShapeDtypeStruct((t, hw), BF16)
    return pl.pallas_call(
        body, name="qkv_fwd", grid=(t // tm,),
        in_specs=[_row_block(tm, Q_RANK, POOL_WIDTH // Q_RANK),
                  _row_block(tm, KV_RANK, (POOL_WIDTH + Q_RANK) // KV_RANK),
                  _row_block(tm, HEAD_PAD, (POOL_WIDTH + Q_RANK + KV_RANK) // HEAD_PAD),
                  _vmem_spec(), _vmem_spec(), _vmem_spec(), _vmem_spec(),
                  _row_block(tm, HEAD_PAD), _row_block(tm, HEAD_PAD), _row_block(tm, HEAD_PAD)],
        out_specs=[_row_block(tm, hw)] * 3,
        out_shape=[out, out, out],
        compiler_params=_params(("parallel",)),
    )(z, z, z, g_q, g_kv, w_uq_p, w_kv_p, rope_c, rope_sa, rope_sb)


def _heads_per_step(heads, want):
    while heads % want:
        want //= 2
    return want


def _causal_mask(rows):
    row = lax.broadcasted_iota(jnp.int32, (rows, rows), 0)
    col = lax.broadcasted_iota(jnp.int32, (rows, rows), 1)
    return col <= row


def _attn_blocks(real_len):
    tail_start = (-(-real_len // ATT_BLOCK) - 1) * ATT_BLOCK
    return tail_start // ATT_BLOCK, tail_start, _round_up(real_len - tail_start, SEQ_PAD)


def _call_with_exchange(body, exchange, *, name, grid, in_specs, out_specs, out_shape, scratch_shapes, operands):
    if exchange is None:
        return pl.pallas_call(body, name=name, grid=grid, in_specs=in_specs, out_specs=out_specs, out_shape=out_shape,
                              scratch_shapes=scratch_shapes,
                              compiler_params=_params(("parallel",) + ("arbitrary",) * (len(grid) - 1)))(*operands)
    n_in, n_out, n_scratch = len(in_specs), len(out_specs), len(scratch_shapes)

    def riding(*refs):
        ins, src = refs[:n_in], refs[n_in]
        outs, dst = refs[n_in + 1:n_in + 1 + n_out], refs[n_in + 1 + n_out]
        scratch = refs[n_in + 2 + n_out:n_in + 2 + n_out + n_scratch]
        send_sems, recv_sems = refs[n_in + 2 + n_out + n_scratch:]
        steps = [pl.program_id(a) for a in range(len(grid))]

        @pl.when(functools.reduce(jnp.logical_and, [s == 0 for s in steps]))
        def _():
            exchange["start"](src, dst, send_sems, recv_sems)

        body(*ins, *outs, *scratch)

        @pl.when(functools.reduce(jnp.logical_and, [s == g - 1 for s, g in zip(steps, grid)]))
        def _():
            exchange["finish"](src, dst, send_sems, recv_sems)

    n = exchange["copies"]
    return pl.pallas_call(
        riding, name=name + "_" + exchange["name"], grid=grid,
        in_specs=list(in_specs) + [_any_spec()], out_specs=list(out_specs) + [_any_spec()],
        out_shape=list(out_shape) + [exchange["out_shape"]],
        scratch_shapes=list(scratch_shapes) + [pltpu.SemaphoreType.DMA((n,)), pltpu.SemaphoreType.DMA((n,))],
        compiler_params=_params(("arbitrary",) * len(grid)),
    )(*operands, exchange["operand"])


def _attn_fwd(q3, k3, v3, real_len, exchange=None):
    b, lp, hw = q3.shape
    heads = hw // HEAD_PAD
    tb = ATT_BLOCK
    nfull, tail_start, tail = _attn_blocks(real_len)
    done = tail_start + tail
    hpg = _heads_per_step(heads, ATT_FWD_HEADS)
    width = hpg * HEAD_PAD

    def body(q_ref, k_ref, v_ref, o_ref, lse_ref):
        group = pl.program_id(1)

        @pl.when(group == 0)
        def _():
            lse_ref[...] = jnp.zeros_like(lse_ref)

        def q_rows(r0, rows, whole_kv_blocks):
            def kv_step(c0, keys, states, mask):
                out = []
                for hd, (m, l, acc) in enumerate(states):
                    cols = slice(hd * HEAD_PAD, (hd + 1) * HEAD_PAD)
                    s = _dot_nt(q_ref[0, pl.ds(r0, rows), cols], k_ref[0, pl.ds(c0, keys), cols])
                    if mask is not None:
                        s = jnp.where(mask, s, MASK_VALUE)
                    m_new = jnp.maximum(m, jnp.max(s, axis=-1, keepdims=True))
                    alpha = jnp.exp(m - m_new)
                    p = jnp.exp(s - m_new)
                    l = alpha * l + jnp.sum(p, axis=-1, keepdims=True)
                    acc = alpha * acc + _dot(p.astype(BF16), v_ref[0, pl.ds(c0, keys), cols])
                    out.append((m_new, l, acc))
                return tuple(out)

            init = tuple((jnp.full((rows, 1), MASK_VALUE, F32), jnp.zeros((rows, 1), F32), jnp.zeros((rows, HEAD_PAD), F32))
                         for _ in range(hpg))
            states = lax.fori_loop(0, whole_kv_blocks, lambda j, st: kv_step(pl.multiple_of(j * tb, tb), tb, st, None), init)
            states = kv_step(r0, rows, states, _causal_mask(rows))
            lane = lax.broadcasted_iota(jnp.int32, (rows, HEAD_PAD), 1)
            lse_rows = lse_ref[0, pl.ds(r0, rows), :]
            for hd, (m, l, acc) in enumerate(states):
                o_ref[0, pl.ds(r0, rows), hd * HEAD_PAD:(hd + 1) * HEAD_PAD] = (acc / l).astype(BF16)
                lse_rows = jnp.where(lane == group * hpg + hd, m + jnp.log(l), lse_rows)
            lse_ref[0, pl.ds(r0, rows), :] = lse_rows

        def whole_block(i, carry):
            q_rows(pl.multiple_of(i * tb, tb), tb, i)
            return carry

        lax.fori_loop(0, nfull, whole_block, 0)
        q_rows(tail_start, tail, nfull)
        if done < lp:
            o_ref[0, done:lp, :] = jnp.zeros((lp - done, width), BF16)

    head_spec = pl.BlockSpec((1, lp, width), lambda bi, hi: (bi, 0, hi))
    return _call_with_exchange(
        body, exchange, name="attn_fwd", grid=(b, heads // hpg),
        in_specs=[head_spec, head_spec, head_spec],
        out_specs=[head_spec, pl.BlockSpec((1, lp, HEAD_PAD), lambda bi, hi: (bi, 0, 0))],
        out_shape=[jax.ShapeDtypeStruct((b, lp, hw), BF16), jax.ShapeDtypeStruct((b, lp, HEAD_PAD), F32)],
        scratch_shapes=[], operands=(q3, k3, v3))


def _merge_fwd(h, z, a, o, w_pa, w_pb_p, w_o):
    t, d = h.shape
    hw = o.shape[1]
    tm = _token_tile(t, 512)
    gate_block = Z_FIXED // d

    def body(h_ref, ga_ref, gb_ref, a_ref, o_ref, wpa_ref, wpb_ref, wo_ref, h1_ref, pa_ref, pb_ref):
        pa = _dot(a_ref[...], wpa_ref[...])
        pb = _dot(o_ref[...], wpb_ref[...])
        merged = jax.nn.sigmoid(ga_ref[...].astype(F32)) * pa + jax.nn.sigmoid(gb_ref[...].astype(F32)) * pb
        h1_ref[...] = h_ref[...] + _dot(merged.astype(BF16), wo_ref[...])
        pa_ref[...] = pa.astype(BF16)
        pb_ref[...] = pb.astype(BF16)

    return pl.pallas_call(
        body, name="merge_fwd", grid=(t // tm,),
        in_specs=[_row_block(tm, d), _row_block(tm, d, gate_block), _row_block(tm, d, gate_block + 1),
                  _row_block(tm, POOL_WIDTH), _row_block(tm, hw), _vmem_spec(), _vmem_spec(), _vmem_spec()],
        out_specs=[_row_block(tm, d)] * 3,
        out_shape=[jax.ShapeDtypeStruct((t, d), F32), jax.ShapeDtypeStruct((t, d), BF16), jax.ShapeDtypeStruct((t, d), BF16)],
        compiler_params=_params(("parallel",)),
    )(h, z, z, a, o, w_pa, w_pb_p, w_o)


def _ffn_fwd(h, g, w_gate_t, w_up_t, w_down):
    t, d = h.shape
    f = w_gate_t.shape[0]
    tm = _token_tile(t, 256)

    def body(h_ref, g_ref, wg_ref, wu_ref, wd_ref, h2_ref, a_ref, b_ref):
        x = h_ref[...]
        hn, _, _ = _rms_fwd(x, g_ref[...])
        hn = hn.astype(BF16)
        ga = _dot_nt(hn, wg_ref[...])
        up = _dot_nt(hn, wu_ref[...])
        act = ga * jax.nn.sigmoid(ga) * up
        h2_ref[...] = x + _dot(act.astype(BF16), wd_ref[...])
        a_ref[...] = ga.astype(BF16)
        b_ref[...] = up.astype(BF16)

    return pl.pallas_call(
        body, name="ffn_fwd", grid=(t // tm,),
        in_specs=[_row_block(tm, d), _vmem_spec(), _vmem_spec(), _vmem_spec(), _vmem_spec()],
        out_specs=[_row_block(tm, d), _row_block(tm, f), _row_block(tm, f)],
        out_shape=[jax.ShapeDtypeStruct((t, d), F32), jax.ShapeDtypeStruct((t, f), BF16), jax.ShapeDtypeStruct((t, f), BF16)],
        compiler_params=_params(("parallel",)),
    )(h, g, w_gate_t, w_up_t, w_down)


def _loss_head(h, g, target, valid):
    t, d = h.shape
    tm = _token_tile(t, 512)

    def body(h_ref, g_ref, t_ref, valid_ref, dh_ref, loss_ref, dg_ref):
        @pl.when(pl.program_id(0) == 0)
        def _():
            loss_ref[...] = jnp.zeros_like(loss_ref)
            dg_ref[...] = jnp.zeros_like(dg_ref)

        gain = g_ref[...]
        y, xh, r = _rms_fwd(h_ref[...], gain)
        err = (y - t_ref[...]) * valid_ref[...]
        per_row = jnp.sum(err * err, axis=-1, keepdims=True) / d
        loss_ref[...] += 0.5 * jnp.sum(per_row, axis=0, keepdims=True)
        dx, dg_rows = _rms_bwd(err / d, xh, r, gain)
        dh_ref[...] = dx
        dg_ref[...] += jnp.sum(dg_rows, axis=0, keepdims=True)

    return pl.pallas_call(
        body, name="loss_head", grid=(t // tm,),
        in_specs=[_row_block(tm, d), _vmem_spec(), _row_block(tm, d), _row_block(tm, 1)],
        out_specs=[_row_block(tm, d), pl.BlockSpec((1, 1), lambda i: (0, 0)), pl.BlockSpec((1, d), lambda i: (0, 0))],
        out_shape=[jax.ShapeDtypeStruct((t, d), F32), jax.ShapeDtypeStruct((1, 1), F32), jax.ShapeDtypeStruct((1, d), F32)],
        compiler_params=_params(("arbitrary",)),
    )(h, g, target, valid)


def _weight_grad(x, y, name):
    t, k = x.shape
    n = y.shape[1]
    tm = _token_tile(t, 512)
    tn = n
    while k * tn * 4 > 8 * 1024 * 1024 and tn % 256 == 0:
        tn //= 2
    steps = t // tm

    def body(x_ref, y_ref, o_ref, acc):
        @pl.when(pl.program_id(1) == 0)
        def _():
            acc[...] = jnp.zeros_like(acc)

        acc[...] += _dot_tn(x_ref[...].astype(BF16), y_ref[...].astype(BF16))

        @pl.when(pl.program_id(1) == steps - 1)
        def _():
            o_ref[...] = acc[...].astype(BF16)

    return pl.pallas_call(
        body, name=name, grid=(n // tn, steps),
        in_specs=[pl.BlockSpec((tm, k), lambda j, i: (i, 0)), pl.BlockSpec((tm, tn), lambda j, i: (i, j))],
        out_specs=pl.BlockSpec((k, tn), lambda j, i: (0, j)),
        out_shape=jax.ShapeDtypeStruct((k, n), BF16),
        scratch_shapes=[pltpu.VMEM((k, tn), F32)],
        compiler_params=_params(("parallel", "arbitrary")),
    )(x, y)


def _ffn_bwd(h, dh2, a, b, g, w_gate_t, w_up_t, w_down):
    t, d = h.shape
    f = a.shape[1]
    tm = _token_tile(t, 256)

    def body(h_ref, dh2_ref, a_ref, b_ref, g_ref, wg_ref, wu_ref, wd_ref, dh_ref, hn_ref, act_ref, da_ref, db_ref, dg_ref):
        @pl.when(pl.program_id(0) == 0)
        def _():
            dg_ref[...] = jnp.zeros_like(dg_ref)

        gain = g_ref[...]
        hn, xh, r = _rms_fwd(h_ref[...], gain)
        hn_ref[...] = hn.astype(BF16)
        dh2 = dh2_ref[...]
        dact = _dot_nt(dh2.astype(BF16), wd_ref[...])
        ga = a_ref[...].astype(F32)
        up = b_ref[...].astype(F32)
        sg = jax.nn.sigmoid(ga)
        silu = ga * sg
        act_ref[...] = (silu * up).astype(BF16)
        da = (dact * up * (sg * (1.0 + ga * (1.0 - sg)))).astype(BF16)
        db = (dact * silu).astype(BF16)
        da_ref[...] = da
        db_ref[...] = db
        dhn = _dot(da, wg_ref[...]) + _dot(db, wu_ref[...])
        dx, dg_rows = _rms_bwd(dhn, xh, r, gain)
        dh_ref[...] = dh2 + dx
        dg_ref[...] += jnp.sum(dg_rows, axis=0, keepdims=True)

    return pl.pallas_call(
        body, name="ffn_bwd", grid=(t // tm,),
        in_specs=[_row_block(tm, d), _row_block(tm, d), _row_block(tm, f), _row_block(tm, f),
                  _vmem_spec(), _vmem_spec(), _vmem_spec(), _vmem_spec()],
        out_specs=[_row_block(tm, d), _row_block(tm, d), _row_block(tm, f), _row_block(tm, f), _row_block(tm, f),
                   pl.BlockSpec((1, d), lambda i: (0, 0))],
        out_shape=[jax.ShapeDtypeStruct((t, d), F32), jax.ShapeDtypeStruct((t, d), BF16), jax.ShapeDtypeStruct((t, f), BF16),
                   jax.ShapeDtypeStruct((t, f), BF16), jax.ShapeDtypeStruct((t, f), BF16), jax.ShapeDtypeStruct((1, d), F32)],
        compiler_params=_params(("arbitrary",)),
    )(h, dh2, a, b, g, w_gate_t, w_up_t, w_down)


def _merge_bwd(dh1, z, pa, pb, w_o, w_pa, w_pb_p):
    t, d = dh1.shape
    hw = w_pb_p.shape[0]
    tm = _token_tile(t, 512)
    gate_block = Z_FIXED // d

    def body(dh_ref, ga_ref, gb_ref, pa_ref, pb_ref, wo_ref, wpa_ref, wpb_ref,
             mg_ref, dpa_ref, dpb_ref, dga_ref, dgb_ref, da_ref, do_ref):
        dm = _dot_nt(dh_ref[...].astype(BF16), wo_ref[...])
        sa = jax.nn.sigmoid(ga_ref[...].astype(F32))
        sb = jax.nn.sigmoid(gb_ref[...].astype(F32))
        pa = pa_ref[...].astype(F32)
        pb = pb_ref[...].astype(F32)
        mg_ref[...] = (sa * pa + sb * pb).astype(BF16)
        dpa = (dm * sa).astype(BF16)
        dpb = (dm * sb).astype(BF16)
        dpa_ref[...] = dpa
        dpb_ref[...] = dpb
        dga_ref[...] = (dm * pa * (sa * (1.0 - sa))).astype(BF16)
        dgb_ref[...] = (dm * pb * (sb * (1.0 - sb))).astype(BF16)
        da_ref[...] = _dot_nt(dpa, wpa_ref[...]).astype(BF16)
        do_ref[...] = _dot_nt(dpb, wpb_ref[...]).astype(BF16)

    wide = jax.ShapeDtypeStruct((t, d), BF16)
    return pl.pallas_call(
        body, name="merge_bwd", grid=(t // tm,),
        in_specs=[_row_block(tm, d), _row_block(tm, d, gate_block), _row_block(tm, d, gate_block + 1),
                  _row_block(tm, d), _row_block(tm, d), _vmem_spec(), _vmem_spec(), _vmem_spec()],
        out_specs=[_row_block(tm, d)] * 5 + [_row_block(tm, POOL_WIDTH), _row_block(tm, hw)],
        out_shape=[wide] * 5 + [jax.ShapeDtypeStruct((t, POOL_WIDTH), BF16), jax.ShapeDtypeStruct((t, hw), BF16)],
        compiler_params=_params(("parallel",)),
    )(dh1, z, z, pa, pb, w_o, w_pa, w_pb_p)


def _attn_bwd(q3, k3, v3, o3, do3, lse3, real_len, exchange=None):
    b, lp, hw = q3.shape
    heads = hw // HEAD_PAD
    tb = ATT_BLOCK
    nfull, tail_start, tail = _attn_blocks(real_len)
    done = tail_start + tail
    hpg = _heads_per_step(heads, ATT_BWD_HEADS)
    width = hpg * HEAD_PAD

    def body(q_ref, k_ref, v_ref, o_ref, do_ref, lse_ref, dq_ref, dk_ref, dv_ref, dqt_acc, lse_row, delta_row):
        group = pl.program_id(1)
        lse_t = jnp.transpose(lse_ref[0])
        head_of_row = lax.broadcasted_iota(jnp.int32, (HEAD_PAD, lp), 0)
        for hd in range(hpg):
            cols = slice(hd * HEAD_PAD, (hd + 1) * HEAD_PAD)
            lse_row[hd] = jnp.sum(jnp.where(head_of_row == group * hpg + hd, lse_t, 0.0), axis=0, keepdims=True)
            prod = do_ref[0, :, cols].astype(F32) * o_ref[0, :, cols].astype(F32)
            delta_row[hd] = jnp.sum(jnp.transpose(prod), axis=0, keepdims=True)
        dqt_acc[...] = jnp.zeros_like(dqt_acc)

        def kv_rows(c0, keys, whole_q_blocks_from):
            k_t = [jnp.transpose(k_ref[0, pl.ds(c0, keys), hd * HEAD_PAD:(hd + 1) * HEAD_PAD].astype(F32)).astype(BF16)
                   for hd in range(hpg)]

            def q_step(r0, rows, states, mask):
                out = []
                for hd, (dk, dv) in enumerate(states):
                    cols = slice(hd * HEAD_PAD, (hd + 1) * HEAD_PAD)
                    q = q_ref[0, pl.ds(r0, rows), cols]
                    do = do_ref[0, pl.ds(r0, rows), cols]
                    s_t = _dot_nt(k_ref[0, pl.ds(c0, keys), cols], q)
                    if mask is not None:
                        s_t = jnp.where(mask, s_t, MASK_VALUE)
                    p_t = jnp.exp(s_t - lse_row[hd, :, pl.ds(r0, rows)])
                    dp_t = _dot_nt(v_ref[0, pl.ds(c0, keys), cols], do)
                    ds_t = (p_t * (dp_t - delta_row[hd, :, pl.ds(r0, rows)])).astype(BF16)
                    dv = dv + _dot(p_t.astype(BF16), do)
                    dk = dk + _dot(ds_t, q)
                    dqt_acc[cols, pl.ds(r0, rows)] += _dot(k_t[hd], ds_t)
                    out.append((dk, dv))
                return tuple(out)

            zero = jnp.zeros((keys, HEAD_PAD), F32)
            key_pos = lax.broadcasted_iota(jnp.int32, (keys, keys), 0)
            query_pos = lax.broadcasted_iota(jnp.int32, (keys, keys), 1)
            states = q_step(c0, keys, tuple((zero, zero) for _ in range(hpg)), key_pos <= query_pos)
            if whole_q_blocks_from is not None:
                states = lax.fori_loop(whole_q_blocks_from, nfull,
                                       lambda i, st: q_step(pl.multiple_of(i * tb, tb), tb, st, None), states)
                states = q_step(tail_start, tail, states, None)
            for hd, (dk, dv) in enumerate(states):
                cols = slice(hd * HEAD_PAD, (hd + 1) * HEAD_PAD)
                dk_ref[0, pl.ds(c0, keys), cols] = dk.astype(BF16)
                dv_ref[0, pl.ds(c0, keys), cols] = dv.astype(BF16)

        def whole_block(j, carry):
            kv_rows(pl.multiple_of(j * tb, tb), tb, j + 1)
            return carry

        lax.fori_loop(0, nfull, whole_block, 0)
        kv_rows(tail_start, tail, None)
        if done < lp:
            dk_ref[0, done:lp, :] = jnp.zeros((lp - done, width), BF16)
            dv_ref[0, done:lp, :] = jnp.zeros((lp - done, width), BF16)
        for hd in range(hpg):
            cols = slice(hd * HEAD_PAD, (hd + 1) * HEAD_PAD)
            dq_ref[0, :, cols] = jnp.transpose(dqt_acc[cols, :]).astype(BF16)

    head_spec = pl.BlockSpec((1, lp, width), lambda bi, hi: (bi, 0, hi))
    out = jax.ShapeDtypeStruct((b, lp, hw), BF16)
    return _call_with_exchange(
        body, exchange, name="attn_bwd", grid=(b, heads // hpg),
        in_specs=[head_spec] * 5 + [pl.BlockSpec((1, lp, HEAD_PAD), lambda bi, hi: (bi, 0, 0))],
        out_specs=[head_spec] * 3,
        out_shape=[out, out, out],
        scratch_shapes=[pltpu.VMEM((width, lp), F32), pltpu.VMEM((hpg, 1, lp), F32), pltpu.VMEM((hpg, 1, lp), F32)],
        operands=(q3, k3, v3, o3, do3, lse3))


def _qkv_bwd(dq, dk, dv, z, g_q, g_kv, w_uq_p, w_kv_p, rope_c, rope_sa, rope_sb):
    t, hw = dq.shape
    heads = hw // HEAD_PAD
    tm = _token_tile(t, 512)

    def body(dq_ref, dk_ref, dv_ref, cq_ref, ckv_ref, gq_ref, gkv_ref, wq_ref, wkv_ref, c_ref, sa_ref, sb_ref,
             dqraw_ref, dkvraw_ref, cqn_ref, ckvn_ref, dcq_ref, dckv_ref, dkr_ref, dgq_ref, dgkv_ref):
        @pl.when(pl.program_id(0) == 0)
        def _():
            dgq_ref[...] = jnp.zeros_like(dgq_ref)
            dgkv_ref[...] = jnp.zeros_like(dgkv_ref)

        c, sa, sb = c_ref[...], sa_ref[...], sb_ref[...]
        dkr = jnp.zeros((tm, HEAD_PAD), F32)
        for hd in range(heads):
            cols = slice(hd * HEAD_PAD, (hd + 1) * HEAD_PAD)
            dqraw_ref[:, cols] = _rope_bwd(dq_ref[:, cols].astype(F32) * SM_SCALE, c, sa, sb).astype(BF16)
            dkraw = _rope_bwd(dk_ref[:, cols].astype(F32), c, sa, sb)
            dkvraw_ref[:, cols] = dkraw.astype(BF16)
            dkr = dkr + dkraw
        dkvraw_ref[:, hw:] = dv_ref[...]
        lane = lax.broadcasted_iota(jnp.int32, (tm, HEAD_PAD), 1)
        dkr_ref[...] = jnp.where((lane >= QK_NOPE) & (lane < QK_DIM), dkr, 0.0).astype(BF16)

        gq = gq_ref[...]
        cqn, xh, r = _rms_fwd(cq_ref[...].astype(F32), gq)
        cqn_ref[...] = cqn.astype(BF16)
        dx, dg_rows = _rms_bwd(_dot_nt(dqraw_ref[...], wq_ref[...]), xh, r, gq)
        dcq_ref[...] = dx.astype(BF16)
        dgq_ref[...] += jnp.sum(dg_rows, axis=0, keepdims=True)

        gkv = gkv_ref[...]
        ckvn, xh, r = _rms_fwd(ckv_ref[...].astype(F32), gkv)
        ckvn_ref[...] = ckvn.astype(BF16)
        dx, dg_rows = _rms_bwd(_dot_nt(dkvraw_ref[...], wkv_ref[...]), xh, r, gkv)
        dckv_ref[...] = dx.astype(BF16)
        dgkv_ref[...] += jnp.sum(dg_rows, axis=0, keepdims=True)

    def shape(width, dtype=BF16):
        return jax.ShapeDtypeStruct((t, width), dtype)

    return pl.pallas_call(
        body, name="qkv_bwd", grid=(t // tm,),
        in_specs=[_row_block(tm, hw)] * 3
        + [_row_block(tm, Q_RANK, POOL_WIDTH // Q_RANK), _row_block(tm, KV_RANK, (POOL_WIDTH + Q_RANK) // KV_RANK)]
        + [_vmem_spec()] * 4 + [_row_block(tm, HEAD_PAD)] * 3,
        out_specs=[_row_block(tm, hw), _row_block(tm, 2 * hw), _row_block(tm, Q_RANK), _row_block(tm, KV_RANK),
                   _row_block(tm, Q_RANK), _row_block(tm, KV_RANK), _row_block(tm, HEAD_PAD),
                   pl.BlockSpec((1, Q_RANK), lambda i: (0, 0)), pl.BlockSpec((1, KV_RANK), lambda i: (0, 0))],
        out_shape=[shape(hw), shape(2 * hw), shape(Q_RANK), shape(KV_RANK), shape(Q_RANK), shape(KV_RANK), shape(HEAD_PAD),
                   jax.ShapeDtypeStruct((1, Q_RANK), F32), jax.ShapeDtypeStruct((1, KV_RANK), F32)],
        compiler_params=_params(("arbitrary",)),
    )(dq, dk, dv, z, z, g_q, g_kv, w_uq_p, w_kv_p, rope_c, rope_sa, rope_sb)


def _pool_bwd(z3, da3, pool_w, pool_scale):
    b, lp, _ = z3.shape
    groups = len(POOL_WINDOWS)

    def body(u_ref, da_ref, pw_ref, sc_ref, du_ref, dpw_ref, dsc_ref):
        @pl.when(pl.program_id(0) == 0)
        def _():
            dpw_ref[...] = jnp.zeros_like(dpw_ref)
            dsc_ref[...] = jnp.zeros_like(dsc_ref)

        row = lax.broadcasted_iota(jnp.int32, (lp, POOL_GROUP), 0)
        pos = row.astype(F32)
        for gi, w in enumerate(POOL_WINDOWS):
            cols = slice(gi * POOL_GROUP, (gi + 1) * POOL_GROUP)
            count = jnp.minimum(pos + 1.0, float(w))
            u = u_ref[0, :, cols].astype(F32)
            y = (_window_sum(u, w, row, True) / count - u).astype(BF16)
            yw = _dot(y, pw_ref[gi])
            da = da_ref[0, :, cols].astype(F32)
            dsc_ref[:, cols] += jnp.sum(da * yw, axis=0, keepdims=True)
            dyw = (da * sc_ref[:, cols]).astype(BF16)
            dpw_ref[gi] += _dot_tn(y, dyw)
            dy = _dot_nt(dyw, pw_ref[gi])
            du_ref[0, :, cols] = (_window_sum(dy / count, w, row, False) - dy).astype(BF16)

    return pl.pallas_call(
        body, name="pool_bwd", grid=(b,),
        in_specs=[pl.BlockSpec((1, lp, POOL_WIDTH), lambda i: (i, 0, 0)), pl.BlockSpec((1, lp, POOL_WIDTH), lambda i: (i, 0, 0)),
                  _vmem_spec(), _vmem_spec()],
        out_specs=[pl.BlockSpec((1, lp, POOL_WIDTH), lambda i: (i, 0, 0)),
                   pl.BlockSpec((groups, POOL_GROUP, POOL_GROUP), lambda i: (0, 0, 0)),
                   pl.BlockSpec((1, POOL_WIDTH), lambda i: (0, 0))],
        out_shape=[jax.ShapeDtypeStruct((b, lp, POOL_WIDTH), BF16), jax.ShapeDtypeStruct((groups, POOL_GROUP, POOL_GROUP), F32),
                   jax.ShapeDtypeStruct((1, POOL_WIDTH), F32)],
        compiler_params=_params(("arbitrary",)),
    )(z3, da3, pool_w, pool_scale)


def _in_proj_bwd(h, dh1, du, dcq, dckv, dkr, dga, dgb, g, w_in_pt):
    t, d = h.shape
    nz = w_in_pt.shape[0]
    tm = _token_tile(t, 512)
    widths = (POOL_WIDTH, Q_RANK, KV_RANK, HEAD_PAD, d, d)

    def body(h_ref, dh1_ref, du_ref, dcq_ref, dckv_ref, dkr_ref, dga_ref, dgb_ref, g_ref, w_ref, dh_ref, hn_ref, dz_ref, dg_ref):
        @pl.when(pl.program_id(0) == 0)
        def _():
            dg_ref[...] = jnp.zeros_like(dg_ref)

        gain = g_ref[...]
        hn, xh, r = _rms_fwd(h_ref[...], gain)
        hn_ref[...] = hn.astype(BF16)
        dhn = jnp.zeros((tm, d), F32)
        start = 0
        for piece, width in zip((du_ref, dcq_ref, dckv_ref, dkr_ref, dga_ref, dgb_ref), widths):
            val = piece[...]
            dz_ref[:, start:start + width] = val
            dhn = dhn + _dot(val, w_ref[start:start + width, :])
            start += width
        dx, dg_rows = _rms_bwd(dhn, xh, r, gain)
        dh_ref[...] = dh1_ref[...] + dx
        dg_ref[...] += jnp.sum(dg_rows, axis=0, keepdims=True)

    return pl.pallas_call(
        body, name="in_proj_bwd", grid=(t // tm,),
        in_specs=[_row_block(tm, d), _row_block(tm, d)] + [_row_block(tm, w) for w in widths] + [_vmem_spec(), _vmem_spec()],
        out_specs=[_row_block(tm, d), _row_block(tm, d), _row_block(tm, nz), pl.BlockSpec((1, d), lambda i: (0, 0))],
        out_shape=[jax.ShapeDtypeStruct((t, d), F32), jax.ShapeDtypeStruct((t, d), BF16), jax.ShapeDtypeStruct((t, nz), BF16),
                   jax.ShapeDtypeStruct((1, d), F32)],
        compiler_params=_params(("arbitrary",)),
    )(h, dh1, du, dcq, dckv, dkr, dga, dgb, g, w_in_pt)


def _pad_heads(w, heads, width):
    k = w.shape[0]
    w = w.reshape(k, heads, width)
    return jnp.pad(w, ((0, 0), (0, 0), (0, HEAD_PAD - width))).reshape(k, heads * HEAD_PAD)


def _unpad_heads(w, heads, width):
    k = w.shape[0]
    return w.reshape(k, heads, HEAD_PAD)[:, :, :width].reshape(k, heads * width)


def _early_layouts(w, heads):
    o3, o4 = POOL_WIDTH + Q_RANK + KV_RANK, POOL_WIDTH + Q_RANK + KV_RANK + QK_ROPE
    w_in_t = w["w_in"]
    rope_rows = jnp.pad(w_in_t[o3:o4], ((QK_NOPE, HEAD_PAD - QK_DIM), (0, 0)))
    w_in_pt = jnp.concatenate([w_in_t[:o3], rope_rows, w_in_t[o4:]], axis=0)
    w_uq_p = _pad_heads(w["w_uq"], heads, QK_DIM)
    kv = w["w_ukv"].reshape(KV_RANK, heads, QK_NOPE + V_DIM)
    w_k = jnp.pad(kv[:, :, :QK_NOPE], ((0, 0), (0, 0), (0, HEAD_PAD - QK_NOPE))).reshape(KV_RANK, heads * HEAD_PAD)
    w_v = jnp.pad(kv[:, :, QK_NOPE:], ((0, 0), (0, 0), (0, HEAD_PAD - V_DIM))).reshape(KV_RANK, heads * HEAD_PAD)
    return dict(w_in_pt=w_in_pt, w_uq_p=w_uq_p, w_kv_p=jnp.concatenate([w_k, w_v], axis=1))


def _late_layouts(w, heads):
    d = w["w_pb"].shape[1]
    w_pb_p = jnp.pad(w["w_pb"].reshape(heads, V_DIM, d), ((0, 0), (0, HEAD_PAD - V_DIM), (0, 0))).reshape(heads * HEAD_PAD, d)
    return dict(w_pa=w["w_pa"], w_pb_p=w_pb_p, w_o=w["w_o"], w_gate_t=w["w_gate"], w_up_t=w["w_up"], w_down=w["w_down"])


def _early_grad_layouts(g, heads):
    o3 = POOL_WIDTH + Q_RANK + KV_RANK
    gin = g["w_in_pt"]
    w_in = jnp.concatenate([gin[:o3], gin[o3 + QK_NOPE:o3 + QK_DIM], gin[o3 + HEAD_PAD:]], axis=0)
    hw = heads * HEAD_PAD
    gk = g["w_kv_p"][:, :hw].reshape(KV_RANK, heads, HEAD_PAD)[:, :, :QK_NOPE]
    gv = g["w_kv_p"][:, hw:].reshape(KV_RANK, heads, HEAD_PAD)[:, :, :V_DIM]
    w_ukv = jnp.concatenate([gk, gv], axis=2).reshape(KV_RANK, heads * (QK_NOPE + V_DIM))
    return dict(w_in=w_in, w_uq=_unpad_heads(g["w_uq_p"], heads, QK_DIM), w_ukv=w_ukv)


def _late_grad_layouts(g, heads):
    d = g["w_pb_p"].shape[1]
    w_pb = g["w_pb_p"].reshape(heads, HEAD_PAD, d)[:, :V_DIM].reshape(heads * V_DIM, d)
    return dict(w_pa=g["w_pa"], w_pb=w_pb, w_o=g["w_o"], w_gate=g["w_gate_t"], w_up=g["w_up_t"], w_down=g["w_down"])


def _rope_tables(lp, b):
    inv = 1.0 / (ROPE_THETA ** (jnp.arange(0, QK_ROPE, 2, dtype=F32) / QK_ROPE))
    ang = jnp.arange(lp, dtype=F32)[:, None] * inv[None, :]
    cos, sin = jnp.cos(ang), jnp.sin(ang)
    half = QK_ROPE // 2
    ones = jnp.ones((lp, QK_NOPE), F32)
    zeros_lo = jnp.zeros((lp, QK_NOPE), F32)
    zeros_hi = jnp.zeros((lp, HEAD_PAD - QK_DIM), F32)
    zeros_half = jnp.zeros((lp, half), F32)
    c = jnp.concatenate([ones, cos, cos, zeros_hi], axis=1)
    sa = jnp.concatenate([zeros_lo, zeros_half, sin, zeros_hi], axis=1)
    sb = jnp.concatenate([zeros_lo, -sin, zeros_half, zeros_hi], axis=1)
    return tuple(jnp.tile(tab, (b, 1)) for tab in (c, sa, sb))


def _local_step(x, loss_target, meta_tokens, small, early_first, riding, early_first_codec):
    b, seq, d = x.shape
    depth = 2
    heads = early_first["w_uq"].shape[1] // QK_DIM
    core = lax.axis_index("c")
    chip = 2 * lax.axis_index("x") + lax.axis_index("y")
    real_len = N_META + seq
    lp = _round_up(real_len, SEQ_PAD)
    t = b * lp
    pad = lp - N_META - seq

    meta = jnp.broadcast_to(meta_tokens[None], (b, N_META, d))
    h = jnp.concatenate([meta, x, jnp.zeros((b, pad, d), F32)], axis=1).reshape(t, d)
    target = jnp.pad(loss_target, ((0, 0), (N_META, pad), (0, 0))).reshape(t, d)
    pos = jnp.arange(lp)
    valid = jnp.tile(((pos >= N_META) & (pos < N_META + seq)).astype(F32), b).reshape(t, 1)
    rope_c, rope_sa, rope_sb = _rope_tables(lp, b)

    layers = []
    for li in range(depth):
        lay = dict(pool_w=small["pool_w"][li].astype(BF16), pool_scale=small["pool_scale"][li][None])
        for n in ("norm_mix_g", "q_norm_g", "kv_norm_g", "norm_ffn_g"):
            lay[n] = small[n][li][None]
        layers.append(lay)
    layers[0].update(_early_layouts(early_first, heads))

    saved = []
    for li in range(depth):
        lay = layers[li]
        z = _in_proj_fwd(h, lay["norm_mix_g"], lay["w_in_pt"])
        a = _pool_fwd(z.reshape(b, lp, -1), lay["pool_w"], lay["pool_scale"]).reshape(t, POOL_WIDTH)
        q, k, v = _qkv_fwd(z, lay["q_norm_g"], lay["kv_norm_g"], lay["w_uq_p"], lay["w_kv_p"], rope_c, rope_sa, rope_sb)
        hw = q.shape[1]
        packed = riding[li]["packed"]
        o3, lse, others = _attn_fwd(q.reshape(b, lp, hw), k.reshape(b, lp, hw), v.reshape(b, lp, hw), real_len,
                                    _gather_exchange(packed))
        arrived = riding[li]["unpack"](lax.dynamic_update_index_in_dim(others, packed, chip, 0))
        lay.update(_late_layouts({n: arrived[n, li] for n in LATE_WEIGHTS}, heads))
        if li + 1 < depth:
            layers[li + 1].update(_early_layouts({n: arrived[n, li + 1] for n in EARLY_WEIGHTS}, heads))
        o = o3.reshape(t, hw)
        h1, pa, pb = _merge_fwd(h, z, a, o, lay["w_pa"], lay["w_pb_p"], lay["w_o"])
        h2, fa, fb = _ffn_fwd(h1, lay["norm_ffn_g"], lay["w_gate_t"], lay["w_up_t"], lay["w_down"])
        saved.append(dict(h=h, z=z, a=a, q=q, k=k, v=v, o=o, lse=lse, pa=pa, pb=pb, h1=h1, fa=fa, fb=fb))
        h = h2

    dh, loss, d_final = _loss_head(h, small["final_norm_g"][None], target, valid)

    g_small = {n: [] for n in SMALL_WEIGHTS if n != "final_norm_g"}
    early_grads, parts = {}, [None] * depth
    for li in reversed(range(depth)):
        lay, sv = layers[li], saved[li]
        hw = sv["q"].shape[1]
        dh1, hn_f, act, dfa, dfb, dg_ffn = _ffn_bwd(sv["h1"], dh, sv["fa"], sv["fb"], lay["norm_ffn_g"],
                                                     lay["w_gate_t"], lay["w_up_t"], lay["w_down"])
        gl = dict(w_gate_t=_weight_grad(dfa, hn_f, "grad_w_gate"), w_up_t=_weight_grad(dfb, hn_f, "grad_w_up"),
                  w_down=_weight_grad(act, dh, "grad_w_down"))
        merged, dpa, dpb, dga, dgb, da, do = _merge_bwd(dh1, sv["z"], sv["pa"], sv["pb"], lay["w_o"], lay["w_pa"], lay["w_pb_p"])
        gl["w_o"] = _weight_grad(merged, dh1, "grad_w_o")
        gl["w_pa"] = _weight_grad(sv["a"], dpa, "grad_w_pa")
        gl["w_pb_p"] = _weight_grad(sv["o"], dpb, "grad_w_pb")
        to_send = {(n, li): g for n, g in _late_grad_layouts(gl, heads).items()}
        if li + 1 < depth:
            to_send.update({(n, li + 1): g for n, g in early_grads[li + 1].items()})
        sending = riding[li]["pack_grads"](to_send)
        shape3 = (b, lp, hw)
        dq3, dk3, dv3, from_others = _attn_bwd(
            sv["q"].reshape(shape3), sv["k"].reshape(shape3), sv["v"].reshape(shape3), sv["o"].reshape(shape3),
            do.reshape(shape3), sv["lse"], real_len, _scatter_exchange(sending))
        own = lax.dynamic_index_in_dim(lax.dynamic_index_in_dim(sending, chip, 0, keepdims=False), core, 0, keepdims=False)
        parts[li] = lax.dynamic_update_index_in_dim(from_others, own, 2 * chip + core, 0)
        dqraw, dkvraw, cqn, ckvn, dcq, dckv, dkr, dg_q, dg_kv = _qkv_bwd(
            dq3.reshape(t, hw), dk3.reshape(t, hw), dv3.reshape(t, hw), sv["z"], lay["q_norm_g"], lay["kv_norm_g"],
            lay["w_uq_p"], lay["w_kv_p"], rope_c, rope_sa, rope_sb)
        gl["w_uq_p"] = _weight_grad(cqn, dqraw, "grad_w_uq")
        gl["w_kv_p"] = _weight_grad(ckvn, dkvraw, "grad_w_ukv")
        du3, dpool_w, dpool_scale = _pool_bwd(sv["z"].reshape(b, lp, -1), da.reshape(b, lp, POOL_WIDTH),
                                              lay["pool_w"], lay["pool_scale"])
        dh, hn_m, dz, dg_mix = _in_proj_bwd(sv["h"], dh1, du3.reshape(t, POOL_WIDTH), dcq, dckv, dkr, dga, dgb,
                                            lay["norm_mix_g"], lay["w_in_pt"])
        gl["w_in_pt"] = _weight_grad(dz, hn_m, "grad_w_in")
        early_grads[li] = _early_grad_layouts(gl, heads)
        for n, val in (("norm_mix_g", dg_mix[0]), ("pool_w", dpool_w), ("pool_scale", dpool_scale[0]), ("q_norm_g", dg_q[0]),
                       ("kv_norm_g", dg_kv[0]), ("norm_ffn_g", dg_ffn[0])):
            g_small[n].insert(0, val)

    dh3 = dh.reshape(b, lp, d)
    grad_x = dh3[:, N_META:N_META + seq]
    d_meta_rows = dh3[:, :N_META]
    g_small = {n: jnp.stack(v) for n, v in g_small.items()}
    g_small["final_norm_g"] = d_final[0]
    early_first_partial = early_first_codec["pack_grads"]({(n, 0): g for n, g in early_grads[0].items()})
    return loss, grad_x, d_meta_rows, g_small, early_first_partial, parts


def _mesh_place():
    x, y, c = lax.axis_index("x"), lax.axis_index("y"), lax.axis_index("c")
    others = [(1 - x, y), (x, 1 - y), (1 - x, 1 - y)]
    return x, y, c, 2 * x + y, others


def _remote(src, dst, send_sems, recv_sems, k, device):
    return pltpu.make_async_remote_copy(src_ref=src, dst_ref=dst, send_sem=send_sems.at[k], recv_sem=recv_sems.at[k],
                                        device_id=device, device_id_type=MESH)


def _gather_exchange(packed):
    def copy(p_ref, g_ref, send_sems, recv_sems, r, slot):
        _, _, c, _, others = _mesh_place()
        ox, oy = others[r]
        return _remote(p_ref, g_ref.at[slot], send_sems, recv_sems, r, (ox, oy, c))

    def start(p_ref, g_ref, send_sems, recv_sems):
        chip = _mesh_place()[3]
        for r in range(3):
            copy(p_ref, g_ref, send_sems, recv_sems, r, chip).start()

    def finish(p_ref, g_ref, send_sems, recv_sems):
        _, _, _, chip, others = _mesh_place()
        for r, (ox, oy) in enumerate(others):
            copy(p_ref, g_ref, send_sems, recv_sems, r, 2 * ox + oy).wait_recv()
        for r in range(3):
            copy(p_ref, g_ref, send_sems, recv_sems, r, chip).wait_send()

    return dict(name="gather", operand=packed, copies=3, start=start, finish=finish,
                out_shape=jax.ShapeDtypeStruct((N_CHIPS,) + packed.shape, packed.dtype))


def _scatter_exchange(parts):
    flips = [(dx, dy, dc) for dx in (0, 1) for dy in (0, 1) for dc in (0, 1)][1:]

    def copy(p_ref, got_ref, send_sems, recv_sems, k, arriving):
        x, y, c, _, _ = _mesh_place()
        dx, dy, dc = flips[k]
        tx, ty, tc = (1 - x if dx else x), (1 - y if dy else y), (1 - c if dc else c)
        slot = 4 * tx + 2 * ty + tc if arriving else 4 * x + 2 * y + c
        return _remote(p_ref.at[2 * tx + ty, tc], got_ref.at[slot], send_sems, recv_sems, k, (tx, ty, tc))

    def start(p_ref, got_ref, send_sems, recv_sems):
        for k in range(len(flips)):
            copy(p_ref, got_ref, send_sems, recv_sems, k, False).start()

    def finish(p_ref, got_ref, send_sems, recv_sems):
        for k in range(len(flips)):
            copy(p_ref, got_ref, send_sems, recv_sems, k, True).wait_recv()
        for k in range(len(flips)):
            copy(p_ref, got_ref, send_sems, recv_sems, k, False).wait_send()

    return dict(name="scatter", operand=parts, copies=len(flips), start=start, finish=finish,
                out_shape=jax.ShapeDtypeStruct((2 * N_CHIPS,) + parts.shape[2:], parts.dtype))


def _all_gather_shards(packed, meta_shard):
    _, rh, cols = packed.shape

    def body(p_ref, m_ref, g_ref, gm_ref, send_sems, recv_sems):
        x, y, c, chip, others = _mesh_place()
        sibling = (x, y, 1 - c)
        sends = []
        for r, (ox, oy) in enumerate(others):
            sends.append(_remote(p_ref.at[c], g_ref.at[chip, c], send_sems, recv_sems, r, (ox, oy, c)))
            sends.append(_remote(m_ref, gm_ref.at[chip], send_sems, recv_sems, 6 + r, (ox, oy, c)))
        for cp in sends:
            cp.start()
        for r, (ox, oy) in enumerate(others):
            src_chip = 2 * ox + oy
            _remote(p_ref.at[c], g_ref.at[src_chip, c], send_sems, recv_sems, r, (ox, oy, c)).wait_recv()
            passed = _remote(g_ref.at[src_chip, c], g_ref.at[src_chip, c], send_sems, recv_sems, 3 + r, sibling)
            passed.start()
            sends.append(passed)
        for r, (ox, oy) in enumerate(others):
            src_chip = 2 * ox + oy
            _remote(p_ref.at[c], g_ref.at[src_chip, 1 - c], send_sems, recv_sems, 3 + r, sibling).wait_recv()
            _remote(m_ref, gm_ref.at[src_chip], send_sems, recv_sems, 6 + r, (ox, oy, c)).wait_recv()
        for cp in sends:
            cp.wait_send()

    gathered, meta_all = pl.pallas_call(
        body, name="all_gather_shards",
        in_specs=[_any_spec(), _any_spec()], out_specs=[_any_spec(), _any_spec()],
        out_shape=[jax.ShapeDtypeStruct((N_CHIPS, 2, rh, cols), packed.dtype),
                   jax.ShapeDtypeStruct((N_CHIPS,) + meta_shard.shape, meta_shard.dtype)],
        scratch_shapes=[pltpu.SemaphoreType.DMA((9,)), pltpu.SemaphoreType.DMA((9,))],
    )(packed, meta_shard)
    chip = 2 * lax.axis_index("x") + lax.axis_index("y")
    return (lax.dynamic_update_index_in_dim(gathered, packed, chip, 0),
            lax.dynamic_update_index_in_dim(meta_all, meta_shard, chip, 0))


def _pair_exchange(give):
    def body(give_ref, got_ref, send_sems, recv_sems):
        x, y, c, _, _ = _mesh_place()
        cp = _remote(give_ref, got_ref, send_sems, recv_sems, 0, (x, y, 1 - c))
        cp.start()
        cp.wait()

    return pl.pallas_call(
        body, name="pair_exchange", in_specs=[_any_spec()], out_specs=_any_spec(),
        out_shape=jax.ShapeDtypeStruct(give.shape, give.dtype),
        scratch_shapes=[pltpu.SemaphoreType.DMA((1,)), pltpu.SemaphoreType.DMA((1,))],
    )(give)


def _chip_exchange(parts):
    def body(p_ref, got_ref, send_sems, recv_sems):
        _, _, c, chip, others = _mesh_place()
        sends = [_remote(p_ref.at[2 * ox + oy], got_ref.at[chip], send_sems, recv_sems, r, (ox, oy, c))
                 for r, (ox, oy) in enumerate(others)]
        for cp in sends:
            cp.start()
        for r, (ox, oy) in enumerate(others):
            _remote(p_ref.at[chip], got_ref.at[2 * ox + oy], send_sems, recv_sems, r, (ox, oy, c)).wait_recv()
        for cp in sends:
            cp.wait_send()

    got = pl.pallas_call(
        body, name="chip_exchange", in_specs=[_any_spec()], out_specs=_any_spec(),
        out_shape=jax.ShapeDtypeStruct(parts.shape, parts.dtype),
        scratch_shapes=[pltpu.SemaphoreType.DMA((3,)), pltpu.SemaphoreType.DMA((3,))],
    )(parts)
    chip = 2 * lax.axis_index("x") + lax.axis_index("y")
    own = lax.dynamic_index_in_dim(parts, chip, 0, keepdims=False)
    return lax.dynamic_update_index_in_dim(got, own, chip, 0)


def _pair_gather(half):
    def body(h_ref, out_ref, send_sems, recv_sems):
        x, y, c, _, _ = _mesh_place()
        cp = _remote(h_ref, out_ref.at[c], send_sems, recv_sems, 0, (x, y, 1 - c))
        cp.start()
        _remote(h_ref, out_ref.at[1 - c], send_sems, recv_sems, 0, (x, y, 1 - c)).wait_recv()
        cp.wait_send()

    both = pl.pallas_call(
        body, name="pair_gather", in_specs=[_any_spec()], out_specs=_any_spec(),
        out_shape=jax.ShapeDtypeStruct((2,) + half.shape, half.dtype),
        scratch_shapes=[pltpu.SemaphoreType.DMA((1,)), pltpu.SemaphoreType.DMA((1,))],
    )(half)
    return lax.dynamic_update_index_in_dim(both, half, lax.axis_index("c"), 0)


def _row_tile(rows, limit=PACK_TILE):
    if rows <= limit:
        return rows
    for tr in range(limit, 7, -8):
        if rows % tr == 0:
            return tr
    return rows


def _pair_add(keep, got):
    n, rh, cols = keep.shape
    tr = _row_tile(rh)

    def body(k_ref, g_ref, o_ref):
        o_ref[...] = (k_ref[...].astype(F32) + g_ref[...].astype(F32)).astype(BF16)

    spec = pl.BlockSpec((1, tr, cols), lambda j, i: (j, i, 0))
    return pl.pallas_call(
        body, name="pair_add", grid=(n, rh // tr), in_specs=[spec, spec], out_specs=spec,
        out_shape=jax.ShapeDtypeStruct(keep.shape, BF16),
        compiler_params=_params(("parallel", "parallel")),
    )(keep, got)


def _chip_sum(parts):
    n, rh, cols = parts.shape
    tr = _row_tile(rh)

    def body(p_ref, o_ref):
        total = p_ref[0].astype(F32)
        for k in range(1, n):
            total = total + p_ref[k].astype(F32)
        o_ref[...] = total

    return pl.pallas_call(
        body, name="chip_sum", grid=(rh // tr,),
        in_specs=[pl.BlockSpec((n, tr, cols), lambda i: (0, i, 0))], out_specs=pl.BlockSpec((tr, cols), lambda i: (i, 0)),
        out_shape=jax.ShapeDtypeStruct((rh, cols), F32),
        compiler_params=_params(("parallel",)),
    )(parts)


def _reduce_scatter(grads, c):
    keep = lax.dynamic_index_in_dim(grads, c, axis=1, keepdims=False)
    give = lax.dynamic_index_in_dim(grads, 1 - c, axis=1, keepdims=False)
    chip_partial = _pair_add(keep, _pair_exchange(give))
    return _pair_gather(_chip_sum(_chip_exchange(chip_partial)))


def _all_reduce_small(meta_rows, small):
    b, rm, cols = meta_rows.shape
    rows = rm + small.shape[0]

    def body(meta_ref, small_ref, out_ref, mine, pair_buf, chip_buf, send_sems, recv_sems):
        x, y, c, chip, others = _mesh_place()
        acc = meta_ref[0]
        for i in range(1, b):
            acc = acc + meta_ref[i]
        mine[0:rm, :] = acc
        mine[rm:rows, :] = small_ref[...]
        pair = _remote(mine, pair_buf, send_sems, recv_sems, 0, (x, y, 1 - c))
        pair.start()
        pair.wait()
        chip_buf[chip] = mine[...] + pair_buf[...]
        sends = [_remote(chip_buf.at[chip], chip_buf.at[chip], send_sems, recv_sems, 1 + r, (ox, oy, c))
                 for r, (ox, oy) in enumerate(others)]
        for cp in sends:
            cp.start()
        for r, (ox, oy) in enumerate(others):
            _remote(chip_buf.at[chip], chip_buf.at[2 * ox + oy], send_sems, recv_sems, 1 + r, (ox, oy, c)).wait_recv()
        for cp in sends:
            cp.wait_send()
        out_ref[...] = ((chip_buf[0] + chip_buf[1]) + chip_buf[2]) + chip_buf[3]

    return pl.pallas_call(
        body, name="all_reduce_small",
        in_specs=[_vmem_spec(), _vmem_spec()], out_specs=_vmem_spec(),
        out_shape=jax.ShapeDtypeStruct((rows, cols), F32),
        scratch_shapes=[pltpu.VMEM((rows, cols), F32), pltpu.VMEM((rows, cols), F32), pltpu.VMEM((N_CHIPS, rows, cols), F32),
                        pltpu.SemaphoreType.DMA((4,)), pltpu.SemaphoreType.DMA((4,))],
        compiler_params=pltpu.CompilerParams(vmem_limit_bytes=VMEM_LIMIT),
    )(meta_rows, small)


def _adamw(w, g, m, v):
    shape = w.shape
    cols = shape[-1]
    rows = w.size // cols
    tr = _row_tile(rows)

    def body(w_ref, g_ref, m_ref, v_ref, d_ref, m2_ref, v2_ref):
        grad = g_ref[...]
        m2 = ADAM_B1 * m_ref[...] + (1.0 - ADAM_B1) * grad
        v2 = ADAM_B2 * v_ref[...] + (1.0 - ADAM_B2) * jnp.square(grad)
        m_hat = m2 / (1.0 - ADAM_B1 ** ADAM_STEP)
        v_hat = v2 / (1.0 - ADAM_B2 ** ADAM_STEP)
        d_ref[...] = -ADAM_LR * (m_hat / (jnp.sqrt(v_hat) + ADAM_EPS) + ADAM_WD * w_ref[...])
        m2_ref[...] = m2
        v2_ref[...] = v2

    spec = pl.BlockSpec((tr, cols), lambda i: (i, 0))
    out = jax.ShapeDtypeStruct((rows, cols), F32)
    res = pl.pallas_call(
        body, name="adamw", grid=(rows // tr,), in_specs=[spec] * 4, out_specs=[spec] * 3, out_shape=[out] * 3,
        compiler_params=_params(("parallel",)),
    )(*(a.reshape(rows, cols) for a in (w, g, m, v)))
    return tuple(r.reshape(shape) for r in res)


def _pack_rows(arrays):
    flat = [a.reshape(-1, PACK_COLS) for a in arrays]
    counts = [f.shape[0] for f in flat]
    total = sum(counts)
    half = -(-total // 2)
    tiles = -(-half // PACK_TILE)
    padded = 2 * tiles * _round_up(-(-half // tiles), 16)
    if padded > total:
        flat.append(jnp.zeros((padded - total, PACK_COLS), flat[0].dtype))
    return jnp.concatenate(flat, axis=0), counts


def _unpack_rows(buffer, counts, shapes):
    out, start = [], 0
    for n, shape in zip(counts, shapes):
        out.append(buffer[..., start:start + n, :].reshape(buffer.shape[:-2] + tuple(shape)))
        start += n
    return out


def _group_codec(entries, weights):
    turned = [n in TRANSPOSED_WEIGHTS for n, _ in entries]
    shapes = [weights[n].shape[1:][::-1] if t else weights[n].shape[1:] for (n, _), t in zip(entries, turned)]
    by_rows = [t or SHARD_AXIS[n] == 1 for (n, _), t in zip(entries, turned)]
    packed, counts = _pack_rows([(weights[n][li].T if t else weights[n][li]).astype(BF16) for (n, li), t in zip(entries, turned)])
    rows = packed.shape[0]
    pad_rows = rows - sum(counts)

    def unpack(per_chip_packed):
        out = {}
        for entry, (s0, s1), rowwise, blk in zip(entries, shapes, by_rows, _unpack_rows(per_chip_packed, counts, shapes)):
            out[entry] = blk.reshape(N_CHIPS * s0, s1) if rowwise else jnp.transpose(blk, (1, 0, 2)).reshape(s0, N_CHIPS * s1)
        return out

    def pack_grads(whole):
        pieces = []
        for entry, (s0, s1), rowwise in zip(entries, shapes, by_rows):
            g = whole[entry]
            by_chip = g.reshape(N_CHIPS, s0, s1) if rowwise else jnp.transpose(g.reshape(s0, N_CHIPS, s1), (1, 0, 2))
            pieces.append(by_chip.reshape(N_CHIPS, -1, PACK_COLS))
        if pad_rows:
            pieces.append(jnp.zeros((N_CHIPS, pad_rows, PACK_COLS), BF16))
        return jnp.concatenate(pieces, axis=1).reshape(N_CHIPS, 2, rows // 2, PACK_COLS)

    def unpack_reduced(reduced):
        shards = _unpack_rows(reduced.reshape(rows, PACK_COLS), counts, shapes)
        return {entry: s.T if t else s for entry, s, t in zip(entries, shards, turned)}

    return dict(packed=packed, unpack=unpack, pack_grads=pack_grads, unpack_reduced=unpack_reduced)


def kernel(x, meta_tokens, norm_mix_g, w_in, pool_w, pool_scale, q_norm_g, kv_norm_g, w_uq, w_ukv, w_pa, w_pb, w_o, norm_ffn_g, w_gate, w_up, w_down, final_norm_g, loss_target, m_meta_tokens, m_norm_mix_g, m_w_in, m_pool_w, m_pool_scale, m_q_norm_g, m_kv_norm_g, m_w_uq, m_w_ukv, m_w_pa, m_w_pb, m_w_o, m_norm_ffn_g, m_w_gate, m_w_up, m_w_down, m_final_norm_g, v_meta_tokens, v_norm_mix_g, v_w_in, v_pool_w, v_pool_scale, v_q_norm_g, v_kv_norm_g, v_w_uq, v_w_ukv, v_w_pa, v_w_pb, v_w_o, v_norm_ffn_g, v_w_gate, v_w_up, v_w_down, v_final_norm_g):
    weights = dict(meta_tokens=meta_tokens, norm_mix_g=norm_mix_g, w_in=w_in, pool_w=pool_w, pool_scale=pool_scale,
                   q_norm_g=q_norm_g, kv_norm_g=kv_norm_g, w_uq=w_uq, w_ukv=w_ukv, w_pa=w_pa, w_pb=w_pb, w_o=w_o,
                   norm_ffn_g=norm_ffn_g, w_gate=w_gate, w_up=w_up, w_down=w_down, final_norm_g=final_norm_g)
    first = dict(meta_tokens=m_meta_tokens, norm_mix_g=m_norm_mix_g, w_in=m_w_in, pool_w=m_pool_w, pool_scale=m_pool_scale,
                 q_norm_g=m_q_norm_g, kv_norm_g=m_kv_norm_g, w_uq=m_w_uq, w_ukv=m_w_ukv, w_pa=m_w_pa, w_pb=m_w_pb, w_o=m_w_o,
                 norm_ffn_g=m_norm_ffn_g, w_gate=m_w_gate, w_up=m_w_up, w_down=m_w_down, final_norm_g=m_final_norm_g)
    second = dict(meta_tokens=v_meta_tokens, norm_mix_g=v_norm_mix_g, w_in=v_w_in, pool_w=v_pool_w, pool_scale=v_pool_scale,
                  q_norm_g=v_q_norm_g, kv_norm_g=v_kv_norm_g, w_uq=v_w_uq, w_ukv=v_w_ukv, w_pa=v_w_pa, w_pb=v_w_pb, w_o=v_w_o,
                  norm_ffn_g=v_norm_ffn_g, w_gate=v_w_gate, w_up=v_w_up, w_down=v_w_down, final_norm_g=v_final_norm_g)
    core = lax.axis_index("c")
    chip = 2 * lax.axis_index("x") + lax.axis_index("y")
    d = x.shape[-1]
    meta_cols = meta_tokens.shape[1]

    early_first = _group_codec([(n, 0) for n in EARLY_WEIGHTS], weights)
    riding = [_group_codec([(n, 0) for n in LATE_WEIGHTS] + [(n, 1) for n in EARLY_WEIGHTS], weights),
              _group_codec([(n, 1) for n in LATE_WEIGHTS], weights)]
    gathered, meta_all = _all_gather_shards(early_first["packed"].reshape(2, -1, PACK_COLS), meta_tokens)
    early_weights = early_first["unpack"](gathered.reshape(N_CHIPS, -1, PACK_COLS))
    meta_full = jnp.concatenate([meta_all[j] for j in range(N_CHIPS)], axis=1)
    small = {n: weights[n] for n in SMALL_WEIGHTS}

    loss, grad_x, d_meta_rows, g_small, early_partial, parts = _local_step(
        x, loss_target, meta_full, small, {n: early_weights[n, 0] for n in EARLY_WEIGHTS}, riding, early_first)

    shards = early_first["unpack_reduced"](_reduce_scatter(early_partial, core))
    for codec, from_all in zip(riding, parts):
        shards.update(codec["unpack_reduced"](_pair_gather(_chip_sum(from_all))))
    grads = {n: jnp.stack([shards[n, 0], shards[n, 1]]) for n in BIG_WEIGHTS}

    small_shapes = [weights[n].shape for n in SMALL_WEIGHTS]
    small_flat = jnp.concatenate([g_small[n].reshape(-1) for n in SMALL_WEIGHTS])
    small_len = small_flat.shape[0]
    small_rows = _round_up(-(-small_len // PACK_COLS), 8)
    small_pack = jnp.pad(small_flat, (0, small_rows * PACK_COLS - small_len)).reshape(small_rows, PACK_COLS)
    meta_rows = N_META * d // PACK_COLS
    summed = _all_reduce_small(d_meta_rows.reshape(-1, meta_rows, PACK_COLS), small_pack)
    grad_meta_full = summed[:meta_rows].reshape(N_META, d)
    grads["meta_tokens"] = lax.dynamic_slice_in_dim(grad_meta_full, chip * meta_cols, meta_cols, axis=1)
    small_sum = summed[meta_rows:].reshape(-1)
    start = 0
    for n, shape in zip(SMALL_WEIGHTS, small_shapes):
        size = 1
        for s in shape:
            size *= s
        grads[n] = small_sum[start:start + size].reshape(shape)
        start += size

    deltas, new_m, new_v = {}, {}, {}
    for n in WEIGHT_ORDER:
        deltas[n], new_m[n], new_v[n] = _adamw(weights[n], grads[n], first[n], second[n])

    total_loss = lax.psum(loss[0, 0], ("x", "y", "c"))
    return (total_loss, grad_x, *[grads[n] for n in WEIGHT_ORDER], *[deltas[n] for n in WEIGHT_ORDER],
            *[new_m[n] for n in WEIGHT_ORDER], *[new_v[n] for n in WEIGHT_ORDER])
```

```python
import functools

import jax
import jax.numpy as jnp
from jax import lax
from jax.experimental import pallas as pl
from jax.experimental.pallas import tpu as pltpu

F32 = jnp.float32
BF16 = jnp.bfloat16

N_META = 16
POOL_WINDOWS = (2, 4, 8, 16)
POOL_GROUP = 128
POOL_WIDTH = POOL_GROUP * len(POOL_WINDOWS)
QK_NOPE = 64
QK_ROPE = 32
V_DIM = 64
QK_DIM = QK_NOPE + QK_ROPE
Q_RANK = 256
KV_RANK = 128
HEAD_PAD = 128
SM_SCALE = QK_DIM ** -0.5
ROPE_THETA = 10000.0
NORM_EPS = 1e-6
MASK_VALUE = -1e30
Z_FIXED = POOL_WIDTH + Q_RANK + KV_RANK + HEAD_PAD

ADAM_LR = 0.001
ADAM_B1 = 0.9
ADAM_B2 = 0.999
ADAM_EPS = 1e-08
ADAM_WD = 0.01
ADAM_STEP = 10

N_CHIPS = 4
ATT_BLOCK = 256
SEQ_PAD = 128
ATT_Q_ROWS = 256
ATT_FWD_HEADS = 8
ATT_BWD_HEADS = 4
PACK_COLS = 1024
PACK_TILE = 512
VMEM_LIMIT = 60 * 1024 * 1024

MESH = pl.DeviceIdType.MESH

BIG_WEIGHTS = ("w_in", "w_uq", "w_ukv", "w_pa", "w_pb", "w_o", "w_gate", "w_up", "w_down")
EARLY_WEIGHTS = ("w_in", "w_uq", "w_ukv")
LATE_WEIGHTS = ("w_pa", "w_pb", "w_o", "w_gate", "w_up", "w_down")
TRANSPOSED_WEIGHTS = ("w_in", "w_gate", "w_up")
SHARD_AXIS = {"w_in": 2, "w_uq": 2, "w_ukv": 2, "w_pa": 2, "w_pb": 1, "w_o": 1, "w_gate": 2, "w_up": 2, "w_down": 1}
SMALL_WEIGHTS = ("norm_mix_g", "pool_w", "pool_scale", "q_norm_g", "kv_norm_g", "norm_ffn_g", "final_norm_g")
WEIGHT_ORDER = ("meta_tokens", "norm_mix_g", "w_in", "pool_w", "pool_scale", "q_norm_g", "kv_norm_g", "w_uq", "w_ukv",
                "w_pa", "w_pb", "w_o", "norm_ffn_g", "w_gate", "w_up", "w_down", "final_norm_g")


def _round_up(n, m):
    return -(-n // m) * m


def _vmem_spec():
    return pl.BlockSpec(memory_space=pltpu.VMEM)


def _any_spec():
    return pl.BlockSpec(memory_space=pl.ANY)


def _row_block(tm, width, col_block=0):
    return pl.BlockSpec((tm, width), lambda i, cb=col_block: (i, cb))


def _params(sem, vmem=VMEM_LIMIT):
    return pltpu.CompilerParams(dimension_semantics=sem, vmem_limit_bytes=vmem)


def _token_tile(t, want):
    best = SEQ_PAD
    for tm in range(32, min(t, 2 * want) + 1, 32):
        if t % tm == 0 and abs(tm - want) < abs(best - want):
            best = tm
    return best


def _dot(a, b):
    return jnp.dot(a, b, preferred_element_type=F32)


def _dot_nt(a, b):
    return lax.dot_general(a, b, (((1,), (1,)), ((), ())), preferred_element_type=F32)


def _dot_tn(a, b):
    return lax.dot_general(a, b, (((0,), (0,)), ((), ())), preferred_element_type=F32)


def _rms_fwd(x, g):
    r = lax.rsqrt(jnp.mean(x * x, axis=-1, keepdims=True) + NORM_EPS)
    xh = x * r
    return xh * g, xh, r


def _rms_bwd(dy, xh, r, g):
    gdy = dy * g
    dx = r * (gdy - xh * jnp.mean(xh * gdy, axis=-1, keepdims=True))
    return dx, dy * xh


def _rope_fwd(x, c, sa, sb):
    return x * c + pltpu.roll(x, 16, 1) * sa + pltpu.roll(x, HEAD_PAD - 16, 1) * sb


def _rope_bwd(d, c, sa, sb):
    return d * c + pltpu.roll(d * sa, HEAD_PAD - 16, 1) + pltpu.roll(d * sb, 16, 1)


def _in_proj_fwd(h, g, w_in_pt):
    t, d = h.shape
    nz = w_in_pt.shape[0]
    tm = _token_tile(t, 512)

    def body(h_ref, g_ref, w_ref, z_ref):
        hn, _, _ = _rms_fwd(h_ref[...], g_ref[...])
        z_ref[...] = _dot_nt(hn.astype(BF16), w_ref[...]).astype(BF16)

    return pl.pallas_call(
        body, name="in_proj_fwd", grid=(t // tm,),
        in_specs=[_row_block(tm, d), _vmem_spec(), _vmem_spec()],
        out_specs=_row_block(tm, nz),
        out_shape=jax.ShapeDtypeStruct((t, nz), BF16),
        compiler_params=_params(("parallel",)),
    )(h, g, w_in_pt)


def _window_sum(x, w, row, forward):
    n = x.shape[0]
    s = x
    k = 1
    while k < w:
        if forward:
            s = s + jnp.where(row >= k, pltpu.roll(s, k, 0), 0.0)
        else:
            s = s + jnp.where(row < n - k, pltpu.roll(s, n - k, 0), 0.0)
        k *= 2
    return s


def _pool_fwd(z3, pool_w, pool_scale):
    b, lp, _ = z3.shape

    def body(u_ref, pw_ref, sc_ref, a_ref):
        row = lax.broadcasted_iota(jnp.int32, (lp, POOL_GROUP), 0)
        pos = row.astype(F32)
        for gi, w in enumerate(POOL_WINDOWS):
            cols = slice(gi * POOL_GROUP, (gi + 1) * POOL_GROUP)
            u = u_ref[0, :, cols].astype(F32)
            y = _window_sum(u, w, row, True) / jnp.minimum(pos + 1.0, float(w)) - u
            yw = _dot(y.astype(BF16), pw_ref[gi])
            a_ref[0, :, cols] = (yw * sc_ref[:, cols]).astype(BF16)

    return pl.pallas_call(
        body, name="pool_fwd", grid=(b,),
        in_specs=[pl.BlockSpec((1, lp, POOL_WIDTH), lambda i: (i, 0, 0)), _vmem_spec(), _vmem_spec()],
        out_specs=pl.BlockSpec((1, lp, POOL_WIDTH), lambda i: (i, 0, 0)),
        out_shape=jax.ShapeDtypeStruct((b, lp, POOL_WIDTH), BF16),
        compiler_params=_params(("parallel",)),
    )(z3, pool_w, pool_scale)


def _qkv_fwd(z, g_q, g_kv, w_uq_p, w_kv_p, rope_c, rope_sa, rope_sb):
    t = z.shape[0]
    hw = w_uq_p.shape[1]
    heads = hw // HEAD_PAD
    tm = _token_tile(t, 512)

    def body(cq_ref, ckv_ref, kr_ref, gq_ref, gkv_ref, wq_ref, wkv_ref, c_ref, sa_ref, sb_ref, q_ref, k_ref, v_ref):
        c, sa, sb = c_ref[...], sa_ref[...], sb_ref[...]
        cqn, _, _ = _rms_fwd(cq_ref[...].astype(F32), gq_ref[...])
        qraw = _dot(cqn.astype(BF16), wq_ref[...])
        ckvn, _, _ = _rms_fwd(ckv_ref[...].astype(F32), gkv_ref[...])
        kvraw = _dot(ckvn.astype(BF16), wkv_ref[...])
        kr = _rope_fwd(kr_ref[...].astype(F32), c, sa, sb)
        for hd in range(heads):
            cols = slice(hd * HEAD_PAD, (hd + 1) * HEAD_PAD)
            q_ref[:, cols] = (_rope_fwd(qraw[:, cols], c, sa, sb) * SM_SCALE).astype(BF16)
            k_ref[:, cols] = (kvraw[:, cols] + kr).astype(BF16)
        v_ref[...] = kvraw[:, hw:].astype(BF16)

    out = jax.ShapeDtypeStruct((t, hw), BF16)
    return pl.pallas_call(
        body, name="qkv_fwd", grid=(t // tm,),
        in_specs=[_row_block(tm, Q_RANK, POOL_WIDTH // Q_RANK),
                  _row_block(tm, KV_RANK, (POOL_WIDTH + Q_RANK) // KV_RANK),
                  _row_block(tm, HEAD_PAD, (POOL_WIDTH + Q_RANK + KV_RANK) // HEAD_PAD),
                  _vmem_spec(), _vmem_spec(), _vmem_spec(), _vmem_spec(),
                  _row_block(tm, HEAD_PAD), _row_block(tm, HEAD_PAD), _row_block(tm, HEAD_PAD)],
        out_specs=[_row_block(tm, hw)] * 3,
        out_shape=[out, out, out],
        compiler_params=_params(("parallel",)),
    )(z, z, z, g_q, g_kv, w_uq_p, w_kv_p, rope_c, rope_sa, rope_sb)


def _heads_per_step(heads, want):
    while heads % want:
        want //= 2
    return want


def _causal_mask(rows):
    row = lax.broadcasted_iota(jnp.int32, (rows, rows), 0)
    col = lax.broadcasted_iota(jnp.int32, (rows, rows), 1)
    return col <= row


def _attn_blocks(real_len):
    tail_start = (-(-real_len // ATT_BLOCK) - 1) * ATT_BLOCK
    return tail_start // ATT_BLOCK, tail_start, _round_up(real_len - tail_start, SEQ_PAD)


def _call_with_exchange(body, exchange, *, name, grid, in_specs, out_specs, out_shape, scratch_shapes, operands):
    if exchange is None:
        return pl.pallas_call(body, name=name, grid=grid, in_specs=in_specs, out_specs=out_specs, out_shape=out_shape,
                              scratch_shapes=scratch_shapes,
                              compiler_params=_params(("parallel",) + ("arbitrary",) * (len(grid) - 1)))(*operands)
    n_in, n_out, n_scratch = len(in_specs), len(out_specs), len(scratch_shapes)

    def riding(*refs):
        ins, src = refs[:n_in], refs[n_in]
        outs, dst = refs[n_in + 1:n_in + 1 + n_out], refs[n_in + 1 + n_out]
        scratch = refs[n_in + 2 + n_out:n_in + 2 + n_out + n_scratch]
        send_sems, recv_sems = refs[n_in + 2 + n_out + n_scratch:]
        steps = [pl.program_id(a) for a in range(len(grid))]

        @pl.when(functools.reduce(jnp.logical_and, [s == 0 for s in steps]))
        def _():
            exchange["start"](src, dst, send_sems, recv_sems)

        body(*ins, *outs, *scratch)

        @pl.when(functools.reduce(jnp.logical_and, [s == g - 1 for s, g in zip(steps, grid)]))
        def _():
            exchange["finish"](src, dst, send_sems, recv_sems)

    n = exchange["copies"]
    return pl.pallas_call(
        riding, name=name + "_" + exchange["name"], grid=grid,
        in_specs=list(in_specs) + [_any_spec()], out_specs=list(out_specs) + [_any_spec()],
        out_shape=list(out_shape) + [exchange["out_shape"]],
        scratch_shapes=list(scratch_shapes) + [pltpu.SemaphoreType.DMA((n,)), pltpu.SemaphoreType.DMA((n,))],
        compiler_params=_params(("arbitrary",) * len(grid)),
    )(*operands, exchange["operand"])


def _attn_fwd(q3, k3, v3, real_len, exchange=None):
    b, lp, hw = q3.shape
    heads = hw // HEAD_PAD
    tb = ATT_BLOCK
    nfull, tail_start, tail = _attn_blocks(real_len)
    done = tail_start + tail
    hpg = _heads_per_step(heads, ATT_FWD_HEADS)
    width = hpg * HEAD_PAD

    def body(q_ref, k_ref, v_ref, o_ref, lse_ref):
        group = pl.program_id(1)

        @pl.when(group == 0)
        def _():
            lse_ref[...] = jnp.zeros_like(lse_ref)

        def q_rows(r0, rows, whole_kv_blocks, back):
            def kv_step(c0, keys, states, mask):
                out = []
                for hd, (m, l, acc) in enumerate(states):
                    cols = slice(hd * HEAD_PAD, (hd + 1) * HEAD_PAD)
                    s = _dot_nt(q_ref[0, pl.ds(r0, rows), cols], k_ref[0, pl.ds(c0, keys), cols])
                    if mask is not None:
                        s = jnp.where(mask, s, MASK_VALUE)
                    m_new = jnp.maximum(m, jnp.max(s, axis=-1, keepdims=True))
                    alpha = jnp.exp(m - m_new)
                    p = jnp.exp(s - m_new)
                    l = alpha * l + jnp.sum(p, axis=-1, keepdims=True)
                    acc = alpha * acc + _dot(p.astype(BF16), v_ref[0, pl.ds(c0, keys), cols])
                    out.append((m_new, l, acc))
                return tuple(out)

            init = tuple((jnp.full((rows, 1), MASK_VALUE, F32), jnp.zeros((rows, 1), F32), jnp.zeros((rows, HEAD_PAD), F32))
                         for _ in range(hpg))
            states = lax.fori_loop(0, whole_kv_blocks, lambda j, st: kv_step(pl.multiple_of(j * tb, tb), tb, st, None), init)
            query = lax.broadcasted_iota(jnp.int32, (rows, back + rows), 0)
            key = lax.broadcasted_iota(jnp.int32, (rows, back + rows), 1)
            states = kv_step(pl.multiple_of(r0 - back, SEQ_PAD), back + rows, states, key <= query + back)
            lane = lax.broadcasted_iota(jnp.int32, (rows, HEAD_PAD), 1)
            lse_rows = lse_ref[0, pl.ds(r0, rows), :]
            for hd, (m, l, acc) in enumerate(states):
                o_ref[0, pl.ds(r0, rows), hd * HEAD_PAD:(hd + 1) * HEAD_PAD] = (acc / l).astype(BF16)
                lse_rows = jnp.where(lane == group * hpg + hd, m + jnp.log(l), lse_rows)
            lse_ref[0, pl.ds(r0, rows), :] = lse_rows

        def whole_block(i, carry):
            for back in range(0, tb, ATT_Q_ROWS):
                q_rows(pl.multiple_of(i * tb + back, ATT_Q_ROWS), ATT_Q_ROWS, i, back)
            return carry

        lax.fori_loop(0, nfull, whole_block, 0)
        for back in range(0, tail, ATT_Q_ROWS):
            q_rows(tail_start + back, min(ATT_Q_ROWS, tail - back), nfull, back)
        if done < lp:
            o_ref[0, done:lp, :] = jnp.zeros((lp - done, width), BF16)

    head_spec = pl.BlockSpec((1, lp, width), lambda bi, hi: (bi, 0, hi))
    return _call_with_exchange(
        body, exchange, name="attn_fwd", grid=(b, heads // hpg),
        in_specs=[head_spec, head_spec, head_spec],
        out_specs=[head_spec, pl.BlockSpec((1, lp, HEAD_PAD), lambda bi, hi: (bi, 0, 0))],
        out_shape=[jax.ShapeDtypeStruct((b, lp, hw), BF16), jax.ShapeDtypeStruct((b, lp, HEAD_PAD), F32)],
        scratch_shapes=[], operands=(q3, k3, v3))


def _merge_fwd(h, z, a, o, w_pa, w_pb_p, w_o):
    t, d = h.shape
    hw = o.shape[1]
    tm = _token_tile(t, 512)
    gate_block = Z_FIXED // d

    def body(h_ref, ga_ref, gb_ref, a_ref, o_ref, wpa_ref, wpb_ref, wo_ref, h1_ref, pa_ref, pb_ref):
        pa = _dot(a_ref[...], wpa_ref[...])
        pb = _dot(o_ref[...], wpb_ref[...])
        merged = jax.nn.sigmoid(ga_ref[...].astype(F32)) * pa + jax.nn.sigmoid(gb_ref[...].astype(F32)) * pb
        h1_ref[...] = h_ref[...] + _dot(merged.astype(BF16), wo_ref[...])
        pa_ref[...] = pa.astype(BF16)
        pb_ref[...] = pb.astype(BF16)

    return pl.pallas_call(
        body, name="merge_fwd", grid=(t // tm,),
        in_specs=[_row_block(tm, d), _row_block(tm, d, gate_block), _row_block(tm, d, gate_block + 1),
                  _row_block(tm, POOL_WIDTH), _row_block(tm, hw), _vmem_spec(), _vmem_spec(), _vmem_spec()],
        out_specs=[_row_block(tm, d)] * 3,
        out_shape=[jax.ShapeDtypeStruct((t, d), F32), jax.ShapeDtypeStruct((t, d), BF16), jax.ShapeDtypeStruct((t, d), BF16)],
        compiler_params=_params(("parallel",)),
    )(h, z, z, a, o, w_pa, w_pb_p, w_o)


def _ffn_fwd(h, g, w_gate_t, w_up_t, w_down):
    t, d = h.shape
    f = w_gate_t.shape[0]
    tm = _token_tile(t, 256)

    def body(h_ref, g_ref, wg_ref, wu_ref, wd_ref, h2_ref, a_ref, b_ref):
        x = h_ref[...]
        hn, _, _ = _rms_fwd(x, g_ref[...])
        hn = hn.astype(BF16)
        ga = _dot_nt(hn, wg_ref[...])
        up = _dot_nt(hn, wu_ref[...])
        act = ga * jax.nn.sigmoid(ga) * up
        h2_ref[...] = x + _dot(act.astype(BF16), wd_ref[...])
        a_ref[...] = ga.astype(BF16)
        b_ref[...] = up.astype(BF16)

    return pl.pallas_call(
        body, name="ffn_fwd", grid=(t // tm,),
        in_specs=[_row_block(tm, d), _vmem_spec(), _vmem_spec(), _vmem_spec(), _vmem_spec()],
        out_specs=[_row_block(tm, d), _row_block(tm, f), _row_block(tm, f)],
        out_shape=[jax.ShapeDtypeStruct((t, d), F32), jax.ShapeDtypeStruct((t, f), BF16), jax.ShapeDtypeStruct((t, f), BF16)],
        compiler_params=_params(("parallel",)),
    )(h, g, w_gate_t, w_up_t, w_down)


def _loss_head(h, g, target, valid):
    t, d = h.shape
    tm = _token_tile(t, 512)

    def body(h_ref, g_ref, t_ref, valid_ref, dh_ref, loss_ref, dg_ref):
        @pl.when(pl.program_id(0) == 0)
        def _():
            loss_ref[...] = jnp.zeros_like(loss_ref)
            dg_ref[...] = jnp.zeros_like(dg_ref)

        gain = g_ref[...]
        y, xh, r = _rms_fwd(h_ref[...], gain)
        err = (y - t_ref[...]) * valid_ref[...]
        per_row = jnp.sum(err * err, axis=-1, keepdims=True) / d
        loss_ref[...] += 0.5 * jnp.sum(per_row, axis=0, keepdims=True)
        dx, dg_rows = _rms_bwd(err / d, xh, r, gain)
        dh_ref[...] = dx
        dg_ref[...] += jnp.sum(dg_rows, axis=0, keepdims=True)

    return pl.pallas_call(
        body, name="loss_head", grid=(t // tm,),
        in_specs=[_row_block(tm, d), _vmem_spec(), _row_block(tm, d), _row_block(tm, 1)],
        out_specs=[_row_block(tm, d), pl.BlockSpec((1, 1), lambda i: (0, 0)), pl.BlockSpec((1, d), lambda i: (0, 0))],
        out_shape=[jax.ShapeDtypeStruct((t, d), F32), jax.ShapeDtypeStruct((1, 1), F32), jax.ShapeDtypeStruct((1, d), F32)],
        compiler_params=_params(("arbitrary",)),
    )(h, g, target, valid)


def _weight_grad(x, y, name):
    t, k = x.shape
    n = y.shape[1]
    tm = _token_tile(t, 512)
    tn = n
    while k * tn * 4 > 8 * 1024 * 1024 and tn % 256 == 0:
        tn //= 2
    steps = t // tm

    def body(x_ref, y_ref, o_ref, acc):
        @pl.when(pl.program_id(1) == 0)
        def _():
            acc[...] = jnp.zeros_like(acc)

        acc[...] += _dot_tn(x_ref[...].astype(BF16), y_ref[...].astype(BF16))

        @pl.when(pl.program_id(1) == steps - 1)
        def _():
            o_ref[...] = acc[...].astype(BF16)

    return pl.pallas_call(
        body, name=name, grid=(n // tn, steps),
        in_specs=[pl.BlockSpec((tm, k), lambda j, i: (i, 0)), pl.BlockSpec((tm, tn), lambda j, i: (i, j))],
        out_specs=pl.BlockSpec((k, tn), lambda j, i: (0, j)),
        out_shape=jax.ShapeDtypeStruct((k, n), BF16),
        scratch_shapes=[pltpu.VMEM((k, tn), F32)],
        compiler_params=_params(("parallel", "arbitrary")),
    )(x, y)


def _ffn_bwd(h, dh2, a, b, g, w_gate_t, w_up_t, w_down):
    t, d = h.shape
    f = a.shape[1]
    tm = _token_tile(t, 256)

    def body(h_ref, dh2_ref, a_ref, b_ref, g_ref, wg_ref, wu_ref, wd_ref, dh_ref, hn_ref, act_ref, da_ref, db_ref, dg_ref):
        @pl.when(pl.program_id(0) == 0)
        def _():
            dg_ref[...] = jnp.zeros_like(dg_ref)

        gain = g_ref[...]
        hn, xh, r = _rms_fwd(h_ref[...], gain)
        hn_ref[...] = hn.astype(BF16)
        dh2 = dh2_ref[...]
        dact = _dot_nt(dh2.astype(BF16), wd_ref[...])
        ga = a_ref[...].astype(F32)
        up = b_ref[...].astype(F32)
        sg = jax.nn.sigmoid(ga)
        silu = ga * sg
        act_ref[...] = (silu * up).astype(BF16)
        da = (dact * up * (sg * (1.0 + ga * (1.0 - sg)))).astype(BF16)
        db = (dact * silu).astype(BF16)
        da_ref[...] = da
        db_ref[...] = db
        dhn = _dot(da, wg_ref[...]) + _dot(db, wu_ref[...])
        dx, dg_rows = _rms_bwd(dhn, xh, r, gain)
        dh_ref[...] = dh2 + dx
        dg_ref[...] += jnp.sum(dg_rows, axis=0, keepdims=True)

    return pl.pallas_call(
        body, name="ffn_bwd", grid=(t // tm,),
        in_specs=[_row_block(tm, d), _row_block(tm, d), _row_block(tm, f), _row_block(tm, f),
                  _vmem_spec(), _vmem_spec(), _vmem_spec(), _vmem_spec()],
        out_specs=[_row_block(tm, d), _row_block(tm, d), _row_block(tm, f), _row_block(tm, f), _row_block(tm, f),
                   pl.BlockSpec((1, d), lambda i: (0, 0))],
        out_shape=[jax.ShapeDtypeStruct((t, d), F32), jax.ShapeDtypeStruct((t, d), BF16), jax.ShapeDtypeStruct((t, f), BF16),
                   jax.ShapeDtypeStruct((t, f), BF16), jax.ShapeDtypeStruct((t, f), BF16), jax.ShapeDtypeStruct((1, d), F32)],
        compiler_params=_params(("arbitrary",)),
    )(h, dh2, a, b, g, w_gate_t, w_up_t, w_down)


def _merge_bwd(dh1, z, pa, pb, w_o, w_pa, w_pb_p):
    t, d = dh1.shape
    hw = w_pb_p.shape[0]
    tm = _token_tile(t, 512)
    gate_block = Z_FIXED // d

    def body(dh_ref, ga_ref, gb_ref, pa_ref, pb_ref, wo_ref, wpa_ref, wpb_ref,
             mg_ref, dpa_ref, dpb_ref, dga_ref, dgb_ref, da_ref, do_ref):
        dm = _dot_nt(dh_ref[...].astype(BF16), wo_ref[...])
        sa = jax.nn.sigmoid(ga_ref[...].astype(F32))
        sb = jax.nn.sigmoid(gb_ref[...].astype(F32))
        pa = pa_ref[...].astype(F32)
        pb = pb_ref[...].astype(F32)
        mg_ref[...] = (sa * pa + sb * pb).astype(BF16)
        dpa = (dm * sa).astype(BF16)
        dpb = (dm * sb).astype(BF16)
        dpa_ref[...] = dpa
        dpb_ref[...] = dpb
        dga_ref[...] = (dm * pa * (sa * (1.0 - sa))).astype(BF16)
        dgb_ref[...] = (dm * pb * (sb * (1.0 - sb))).astype(BF16)
        da_ref[...] = _dot_nt(dpa, wpa_ref[...]).astype(BF16)
        do_ref[...] = _dot_nt(dpb, wpb_ref[...]).astype(BF16)

    wide = jax.ShapeDtypeStruct((t, d), BF16)
    return pl.pallas_call(
        body, name="merge_bwd", grid=(t // tm,),
        in_specs=[_row_block(tm, d), _row_block(tm, d, gate_block), _row_block(tm, d, gate_block + 1),
                  _row_block(tm, d), _row_block(tm, d), _vmem_spec(), _vmem_spec(), _vmem_spec()],
        out_specs=[_row_block(tm, d)] * 5 + [_row_block(tm, POOL_WIDTH), _row_block(tm, hw)],
        out_shape=[wide] * 5 + [jax.ShapeDtypeStruct((t, POOL_WIDTH), BF16), jax.ShapeDtypeStruct((t, hw), BF16)],
        compiler_params=_params(("parallel",)),
    )(dh1, z, z, pa, pb, w_o, w_pa, w_pb_p)


def _attn_bwd(q3, k3, v3, o3, do3, lse3, real_len, exchange=None):
    b, lp, hw = q3.shape
    heads = hw // HEAD_PAD
    tb = ATT_BLOCK
    nfull, tail_start, tail = _attn_blocks(real_len)
    done = tail_start + tail
    hpg = _heads_per_step(heads, ATT_BWD_HEADS)
    width = hpg * HEAD_PAD

    def body(q_ref, k_ref, v_ref, o_ref, do_ref, lse_ref, dq_ref, dk_ref, dv_ref, dqt_acc, lse_row, delta_row):
        group = pl.program_id(1)
        lse_t = jnp.transpose(lse_ref[0])
        head_of_row = lax.broadcasted_iota(jnp.int32, (HEAD_PAD, lp), 0)
        for hd in range(hpg):
            cols = slice(hd * HEAD_PAD, (hd + 1) * HEAD_PAD)
            lse_row[hd] = jnp.sum(jnp.where(head_of_row == group * hpg + hd, lse_t, 0.0), axis=0, keepdims=True)
            prod = do_ref[0, :, cols].astype(F32) * o_ref[0, :, cols].astype(F32)
            delta_row[hd] = jnp.sum(jnp.transpose(prod), axis=0, keepdims=True)
        dqt_acc[...] = jnp.zeros_like(dqt_acc)

        def kv_rows(c0, keys, whole_q_blocks_from):
            k_t = [jnp.transpose(k_ref[0, pl.ds(c0, keys), hd * HEAD_PAD:(hd + 1) * HEAD_PAD].astype(F32)).astype(BF16)
                   for hd in range(hpg)]

            def q_step(r0, rows, states, mask):
                out = []
                for hd, (dk, dv) in enumerate(states):
                    cols = slice(hd * HEAD_PAD, (hd + 1) * HEAD_PAD)
                    q = q_ref[0, pl.ds(r0, rows), cols]
                    do = do_ref[0, pl.ds(r0, rows), cols]
                    s_t = _dot_nt(k_ref[0, pl.ds(c0, keys), cols], q)
                    if mask is not None:
                        s_t = jnp.where(mask, s_t, MASK_VALUE)
                    p_t = jnp.exp(s_t - lse_row[hd, :, pl.ds(r0, rows)])
                    dp_t = _dot_nt(v_ref[0, pl.ds(c0, keys), cols], do)
                    ds_t = (p_t * (dp_t - delta_row[hd, :, pl.ds(r0, rows)])).astype(BF16)
                    dv = dv + _dot(p_t.astype(BF16), do)
                    dk = dk + _dot(ds_t, q)
                    dqt_acc[cols, pl.ds(r0, rows)] += _dot(k_t[hd], ds_t)
                    out.append((dk, dv))
                return tuple(out)

            zero = jnp.zeros((keys, HEAD_PAD), F32)
            key_pos = lax.broadcasted_iota(jnp.int32, (keys, keys), 0)
            query_pos = lax.broadcasted_iota(jnp.int32, (keys, keys), 1)
            states = q_step(c0, keys, tuple((zero, zero) for _ in range(hpg)), key_pos <= query_pos)
            if whole_q_blocks_from is not None:
                states = lax.fori_loop(whole_q_blocks_from, nfull,
                                       lambda i, st: q_step(pl.multiple_of(i * tb, tb), tb, st, None), states)
                states = q_step(tail_start, tail, states, None)
            for hd, (dk, dv) in enumerate(states):
                cols = slice(hd * HEAD_PAD, (hd + 1) * HEAD_PAD)
                dk_ref[0, pl.ds(c0, keys), cols] = dk.astype(BF16)
                dv_ref[0, pl.ds(c0, keys), cols] = dv.astype(BF16)

        def whole_block(j, carry):
            kv_rows(pl.multiple_of(j * tb, tb), tb, j + 1)
            return carry

        lax.fori_loop(0, nfull, whole_block, 0)
        kv_rows(tail_start, tail, None)
        if done < lp:
            dk_ref[0, done:lp, :] = jnp.zeros((lp - done, width), BF16)
            dv_ref[0, done:lp, :] = jnp.zeros((lp - done, width), BF16)
        for hd in range(hpg):
            cols = slice(hd * HEAD_PAD, (hd + 1) * HEAD_PAD)
            dq_ref[0, :, cols] = jnp.transpose(dqt_acc[cols, :]).astype(BF16)

    head_spec = pl.BlockSpec((1, lp, width), lambda bi, hi: (bi, 0, hi))
    out = jax.ShapeDtypeStruct((b, lp, hw), BF16)
    return _call_with_exchange(
        body, exchange, name="attn_bwd", grid=(b, heads // hpg),
        in_specs=[head_spec] * 5 + [pl.BlockSpec((1, lp, HEAD_PAD), lambda bi, hi: (bi, 0, 0))],
        out_specs=[head_spec] * 3,
        out_shape=[out, out, out],
        scratch_shapes=[pltpu.VMEM((width, lp), F32), pltpu.VMEM((hpg, 1, lp), F32), pltpu.VMEM((hpg, 1, lp), F32)],
        operands=(q3, k3, v3, o3, do3, lse3))


def _qkv_bwd(dq, dk, dv, z, g_q, g_kv, w_uq_p, w_kv_p, rope_c, rope_sa, rope_sb):
    t, hw = dq.shape
    heads = hw // HEAD_PAD
    tm = _token_tile(t, 512)

    def body(dq_ref, dk_ref, dv_ref, cq_ref, ckv_ref, gq_ref, gkv_ref, wq_ref, wkv_ref, c_ref, sa_ref, sb_ref,
             dqraw_ref, dkvraw_ref, cqn_ref, ckvn_ref, dcq_ref, dckv_ref, dkr_ref, dgq_ref, dgkv_ref):
        @pl.when(pl.program_id(0) == 0)
        def _():
            dgq_ref[...] = jnp.zeros_like(dgq_ref)
            dgkv_ref[...] = jnp.zeros_like(dgkv_ref)

        c, sa, sb = c_ref[...], sa_ref[...], sb_ref[...]
        dkr = jnp.zeros((tm, HEAD_PAD), F32)
        for hd in range(heads):
            cols = slice(hd * HEAD_PAD, (hd + 1) * HEAD_PAD)
            dqraw_ref[:, cols] = _rope_bwd(dq_ref[:, cols].astype(F32) * SM_SCALE, c, sa, sb).astype(BF16)
            dkvraw_ref[:, cols] = dk_ref[:, cols]
            dkr = dkr + dk_ref[:, cols].astype(F32)
        dkvraw_ref[:, hw:] = dv_ref[...]
        lane = lax.broadcasted_iota(jnp.int32, (tm, HEAD_PAD), 1)
        dkr_ref[...] = jnp.where((lane >= QK_NOPE) & (lane < QK_DIM), _rope_bwd(dkr, c, sa, sb), 0.0).astype(BF16)

        gq = gq_ref[...]
        cqn, xh, r = _rms_fwd(cq_ref[...].astype(F32), gq)
        cqn_ref[...] = cqn.astype(BF16)
        dx, dg_rows = _rms_bwd(_dot_nt(dqraw_ref[...], wq_ref[...]), xh, r, gq)
        dcq_ref[...] = dx.astype(BF16)
        dgq_ref[...] += jnp.sum(dg_rows, axis=0, keepdims=True)

        gkv = gkv_ref[...]
        ckvn, xh, r = _rms_fwd(ckv_ref[...].astype(F32), gkv)
        ckvn_ref[...] = ckvn.astype(BF16)
        dx, dg_rows = _rms_bwd(_dot_nt(dkvraw_ref[...], wkv_ref[...]), xh, r, gkv)
        dckv_ref[...] = dx.astype(BF16)
        dgkv_ref[...] += jnp.sum(dg_rows, axis=0, keepdims=True)

    def shape(width, dtype=BF16):
        return jax.ShapeDtypeStruct((t, width), dtype)

    return pl.pallas_call(
        body, name="qkv_bwd", grid=(t // tm,),
        in_specs=[_row_block(tm, hw)] * 3
        + [_row_block(tm, Q_RANK, POOL_WIDTH // Q_RANK), _row_block(tm, KV_RANK, (POOL_WIDTH + Q_RANK) // KV_RANK)]
        + [_vmem_spec()] * 4 + [_row_block(tm, HEAD_PAD)] * 3,
        out_specs=[_row_block(tm, hw), _row_block(tm, 2 * hw), _row_block(tm, Q_RANK), _row_block(tm, KV_RANK),
                   _row_block(tm, Q_RANK), _row_block(tm, KV_RANK), _row_block(tm, HEAD_PAD),
                   pl.BlockSpec((1, Q_RANK), lambda i: (0, 0)), pl.BlockSpec((1, KV_RANK), lambda i: (0, 0))],
        out_shape=[shape(hw), shape(2 * hw), shape(Q_RANK), shape(KV_RANK), shape(Q_RANK), shape(KV_RANK), shape(HEAD_PAD),
                   jax.ShapeDtypeStruct((1, Q_RANK), F32), jax.ShapeDtypeStruct((1, KV_RANK), F32)],
        compiler_params=_params(("arbitrary",)),
    )(dq, dk, dv, z, z, g_q, g_kv, w_uq_p, w_kv_p, rope_c, rope_sa, rope_sb)


def _pool_bwd(z3, da3, pool_w, pool_scale):
    b, lp, _ = z3.shape
    groups = len(POOL_WINDOWS)

    def body(u_ref, da_ref, pw_ref, sc_ref, du_ref, dpw_ref, dsc_ref):
        @pl.when(pl.program_id(0) == 0)
        def _():
            dpw_ref[...] = jnp.zeros_like(dpw_ref)
            dsc_ref[...] = jnp.zeros_like(dsc_ref)

        row = lax.broadcasted_iota(jnp.int32, (lp, POOL_GROUP), 0)
        pos = row.astype(F32)
        for gi, w in enumerate(POOL_WINDOWS):
            cols = slice(gi * POOL_GROUP, (gi + 1) * POOL_GROUP)
            count = jnp.minimum(pos + 1.0, float(w))
            u = u_ref[0, :, cols].astype(F32)
            y = (_window_sum(u, w, row, True) / count - u).astype(BF16)
            yw = _dot(y, pw_ref[gi])
            da = da_ref[0, :, cols].astype(F32)
            dsc_ref[:, cols] += jnp.sum(da * yw, axis=0, keepdims=True)
            dyw = (da * sc_ref[:, cols]).astype(BF16)
            dpw_ref[gi] += _dot_tn(y, dyw)
            dy = _dot_nt(dyw, pw_ref[gi])
            du_ref[0, :, cols] = (_window_sum(dy / count, w, row, False) - dy).astype(BF16)

    return pl.pallas_call(
        body, name="pool_bwd", grid=(b,),
        in_specs=[pl.BlockSpec((1, lp, POOL_WIDTH), lambda i: (i, 0, 0)), pl.BlockSpec((1, lp, POOL_WIDTH), lambda i: (i, 0, 0)),
                  _vmem_spec(), _vmem_spec()],
        out_specs=[pl.BlockSpec((1, lp, POOL_WIDTH), lambda i: (i, 0, 0)),
                   pl.BlockSpec((groups, POOL_GROUP, POOL_GROUP), lambda i: (0, 0, 0)),
                   pl.BlockSpec((1, POOL_WIDTH), lambda i: (0, 0))],
        out_shape=[jax.ShapeDtypeStruct((b, lp, POOL_WIDTH), BF16), jax.ShapeDtypeStruct((groups, POOL_GROUP, POOL_GROUP), F32),
                   jax.ShapeDtypeStruct((1, POOL_WIDTH), F32)],
        compiler_params=_params(("arbitrary",)),
    )(z3, da3, pool_w, pool_scale)


def _in_proj_bwd(h, dh1, du, dcq, dckv, dkr, dga, dgb, g, w_in_pt):
    t, d = h.shape
    nz = w_in_pt.shape[0]
    tm = _token_tile(t, 512)
    widths = (POOL_WIDTH, Q_RANK, KV_RANK, HEAD_PAD, d, d)

    def body(h_ref, dh1_ref, du_ref, dcq_ref, dckv_ref, dkr_ref, dga_ref, dgb_ref, g_ref, w_ref, dh_ref, hn_ref, dz_ref, dg_ref):
        @pl.when(pl.program_id(0) == 0)
        def _():
            dg_ref[...] = jnp.zeros_like(dg_ref)

        gain = g_ref[...]
        hn, xh, r = _rms_fwd(h_ref[...], gain)
        hn_ref[...] = hn.astype(BF16)
        dhn = jnp.zeros((tm, d), F32)
        start = 0
        for piece, width in zip((du_ref, dcq_ref, dckv_ref, dkr_ref, dga_ref, dgb_ref), widths):
            val = piece[...]
            dz_ref[:, start:start + width] = val
            dhn = dhn + _dot(val, w_ref[start:start + width, :])
            start += width
        dx, dg_rows = _rms_bwd(dhn, xh, r, gain)
        dh_ref[...] = dh1_ref[...] + dx
        dg_ref[...] += jnp.sum(dg_rows, axis=0, keepdims=True)

    return pl.pallas_call(
        body, name="in_proj_bwd", grid=(t // tm,),
        in_specs=[_row_block(tm, d), _row_block(tm, d)] + [_row_block(tm, w) for w in widths] + [_vmem_spec(), _vmem_spec()],
        out_specs=[_row_block(tm, d), _row_block(tm, d), _row_block(tm, nz), pl.BlockSpec((1, d), lambda i: (0, 0))],
        out_shape=[jax.ShapeDtypeStruct((t, d), F32), jax.ShapeDtypeStruct((t, d), BF16), jax.ShapeDtypeStruct((t, nz), BF16),
                   jax.ShapeDtypeStruct((1, d), F32)],
        compiler_params=_params(("arbitrary",)),
    )(h, dh1, du, dcq, dckv, dkr, dga, dgb, g, w_in_pt)


def _pad_heads(w, heads, width):
    k = w.shape[0]
    w = w.reshape(k, heads, width)
    return jnp.pad(w, ((0, 0), (0, 0), (0, HEAD_PAD - width))).reshape(k, heads * HEAD_PAD)


def _unpad_heads(w, heads, width):
    k = w.shape[0]
    return w.reshape(k, heads, HEAD_PAD)[:, :, :width].reshape(k, heads * width)


def _early_layouts(w, heads):
    o3, o4 = POOL_WIDTH + Q_RANK + KV_RANK, POOL_WIDTH + Q_RANK + KV_RANK + QK_ROPE
    w_in_t = w["w_in"]
    rope_rows = jnp.pad(w_in_t[o3:o4], ((QK_NOPE, HEAD_PAD - QK_DIM), (0, 0)))
    w_in_pt = jnp.concatenate([w_in_t[:o3], rope_rows, w_in_t[o4:]], axis=0)
    w_uq_p = _pad_heads(w["w_uq"], heads, QK_DIM)
    kv = w["w_ukv"].reshape(KV_RANK, heads, QK_NOPE + V_DIM)
    w_k = jnp.pad(kv[:, :, :QK_NOPE], ((0, 0), (0, 0), (0, HEAD_PAD - QK_NOPE))).reshape(KV_RANK, heads * HEAD_PAD)
    w_v = jnp.pad(kv[:, :, QK_NOPE:], ((0, 0), (0, 0), (0, HEAD_PAD - V_DIM))).reshape(KV_RANK, heads * HEAD_PAD)
    return dict(w_in_pt=w_in_pt, w_uq_p=w_uq_p, w_kv_p=jnp.concatenate([w_k, w_v], axis=1))


def _late_layouts(w, heads):
    d = w["w_pb"].shape[1]
    w_pb_p = jnp.pad(w["w_pb"].reshape(heads, V_DIM, d), ((0, 0), (0, HEAD_PAD - V_DIM), (0, 0))).reshape(heads * HEAD_PAD, d)
    return dict(w_pa=w["w_pa"], w_pb_p=w_pb_p, w_o=w["w_o"], w_gate_t=w["w_gate"], w_up_t=w["w_up"], w_down=w["w_down"])


def _early_grad_layouts(g, heads):
    o3 = POOL_WIDTH + Q_RANK + KV_RANK
    gin = g["w_in_pt"]
    w_in = jnp.concatenate([gin[:o3], gin[o3 + QK_NOPE:o3 + QK_DIM], gin[o3 + HEAD_PAD:]], axis=0)
    hw = heads * HEAD_PAD
    gk = g["w_kv_p"][:, :hw].reshape(KV_RANK, heads, HEAD_PAD)[:, :, :QK_NOPE]
    gv = g["w_kv_p"][:, hw:].reshape(KV_RANK, heads, HEAD_PAD)[:, :, :V_DIM]
    w_ukv = jnp.concatenate([gk, gv], axis=2).reshape(KV_RANK, heads * (QK_NOPE + V_DIM))
    return dict(w_in=w_in, w_uq=_unpad_heads(g["w_uq_p"], heads, QK_DIM), w_ukv=w_ukv)


def _late_grad_layouts(g, heads):
    d = g["w_pb_p"].shape[1]
    w_pb = g["w_pb_p"].reshape(heads, HEAD_PAD, d)[:, :V_DIM].reshape(heads * V_DIM, d)
    return dict(w_pa=g["w_pa"], w_pb=w_pb, w_o=g["w_o"], w_gate=g["w_gate_t"], w_up=g["w_up_t"], w_down=g["w_down"])


def _rope_tables(lp, b):
    inv = 1.0 / (ROPE_THETA ** (jnp.arange(0, QK_ROPE, 2, dtype=F32) / QK_ROPE))
    ang = jnp.arange(lp, dtype=F32)[:, None] * inv[None, :]
    cos, sin = jnp.cos(ang), jnp.sin(ang)
    half = QK_ROPE // 2
    ones = jnp.ones((lp, QK_NOPE), F32)
    zeros_lo = jnp.zeros((lp, QK_NOPE), F32)
    zeros_hi = jnp.zeros((lp, HEAD_PAD - QK_DIM), F32)
    zeros_half = jnp.zeros((lp, half), F32)
    c = jnp.concatenate([ones, cos, cos, zeros_hi], axis=1)
    sa = jnp.concatenate([zeros_lo, zeros_half, sin, zeros_hi], axis=1)
    sb = jnp.concatenate([zeros_lo, -sin, zeros_half, zeros_hi], axis=1)
    return tuple(jnp.tile(tab, (b, 1)) for tab in (c, sa, sb))


def _local_step(x, loss_target, meta_tokens, small, early_first, riding, early_first_codec):
    b, seq, d = x.shape
    depth = 2
    heads = early_first["w_uq"].shape[1] // QK_DIM
    core = lax.axis_index("c")
    chip = 2 * lax.axis_index("x") + lax.axis_index("y")
    real_len = N_META + seq
    lp = _round_up(real_len, SEQ_PAD)
    t = b * lp
    pad = lp - N_META - seq

    meta = jnp.broadcast_to(meta_tokens[None], (b, N_META, d))
    h = jnp.concatenate([meta, x, jnp.zeros((b, pad, d), F32)], axis=1).reshape(t, d)
    target = jnp.pad(loss_target, ((0, 0), (N_META, pad), (0, 0))).reshape(t, d)
    pos = jnp.arange(lp)
    valid = jnp.tile(((pos >= N_META) & (pos < N_META + seq)).astype(F32), b).reshape(t, 1)
    rope_c, rope_sa, rope_sb = _rope_tables(lp, b)

    layers = []
    for li in range(depth):
        lay = dict(pool_w=small["pool_w"][li].astype(BF16), pool_scale=small["pool_scale"][li][None])
        for n in ("norm_mix_g", "q_norm_g", "kv_norm_g", "norm_ffn_g"):
            lay[n] = small[n][li][None]
        layers.append(lay)
    layers[0].update(_early_layouts(early_first, heads))

    saved = []
    for li in range(depth):
        lay = layers[li]
        z = _in_proj_fwd(h, lay["norm_mix_g"], lay["w_in_pt"])
        a = _pool_fwd(z.reshape(b, lp, -1), lay["pool_w"], lay["pool_scale"]).reshape(t, POOL_WIDTH)
        q, k, v = _qkv_fwd(z, lay["q_norm_g"], lay["kv_norm_g"], lay["w_uq_p"], lay["w_kv_p"], rope_c, rope_sa, rope_sb)
        hw = q.shape[1]
        packed = riding[li]["packed"]
        o3, lse, others = _attn_fwd(q.reshape(b, lp, hw), k.reshape(b, lp, hw), v.reshape(b, lp, hw), real_len,
                                    _gather_exchange(packed))
        arrived = riding[li]["unpack"](lax.dynamic_update_index_in_dim(others, packed, chip, 0))
        lay.update(_late_layouts({n: arrived[n, li] for n in LATE_WEIGHTS}, heads))
        if li + 1 < depth:
            layers[li + 1].update(_early_layouts({n: arrived[n, li + 1] for n in EARLY_WEIGHTS}, heads))
        o = o3.reshape(t, hw)
        h1, pa, pb = _merge_fwd(h, z, a, o, lay["w_pa"], lay["w_pb_p"], lay["w_o"])
        h2, fa, fb = _ffn_fwd(h1, lay["norm_ffn_g"], lay["w_gate_t"], lay["w_up_t"], lay["w_down"])
        saved.append(dict(h=h, z=z, a=a, q=q, k=k, v=v, o=o, lse=lse, pa=pa, pb=pb, h1=h1, fa=fa, fb=fb))
        h = h2

    dh, loss, d_final = _loss_head(h, small["final_norm_g"][None], target, valid)

    g_small = {n: [] for n in SMALL_WEIGHTS if n != "final_norm_g"}
    early_grads, parts = {}, [None] * depth
    for li in reversed(range(depth)):
        lay, sv = layers[li], saved[li]
        hw = sv["q"].shape[1]
        dh1, hn_f, act, dfa, dfb, dg_ffn = _ffn_bwd(sv["h1"], dh, sv["fa"], sv["fb"], lay["norm_ffn_g"],
                                                     lay["w_gate_t"], lay["w_up_t"], lay["w_down"])
        gl = dict(w_gate_t=_weight_grad(dfa, hn_f, "grad_w_gate"), w_up_t=_weight_grad(dfb, hn_f, "grad_w_up"),
                  w_down=_weight_grad(act, dh, "grad_w_down"))
        merged, dpa, dpb, dga, dgb, da, do = _merge_bwd(dh1, sv["z"], sv["pa"], sv["pb"], lay["w_o"], lay["w_pa"], lay["w_pb_p"])
        gl["w_o"] = _weight_grad(merged, dh1, "grad_w_o")
        gl["w_pa"] = _weight_grad(sv["a"], dpa, "grad_w_pa")
        gl["w_pb_p"] = _weight_grad(sv["o"], dpb, "grad_w_pb")
        to_send = {(n, li): g for n, g in _late_grad_layouts(gl, heads).items()}
        if li + 1 < depth:
            to_send.update({(n, li + 1): g for n, g in early_grads[li + 1].items()})
        sending = riding[li]["pack_grads"](to_send)
        shape3 = (b, lp, hw)
        dq3, dk3, dv3, from_others = _attn_bwd(
            sv["q"].reshape(shape3), sv["k"].reshape(shape3), sv["v"].reshape(shape3), sv["o"].reshape(shape3),
            do.reshape(shape3), sv["lse"], real_len, _scatter_exchange(sending))
        own = lax.dynamic_index_in_dim(lax.dynamic_index_in_dim(sending, chip, 0, keepdims=False), core, 0, keepdims=False)
        parts[li] = lax.dynamic_update_index_in_dim(from_others, own, 2 * chip + core, 0)
        dqraw, dkvraw, cqn, ckvn, dcq, dckv, dkr, dg_q, dg_kv = _qkv_bwd(
            dq3.reshape(t, hw), dk3.reshape(t, hw), dv3.reshape(t, hw), sv["z"], lay["q_norm_g"], lay["kv_norm_g"],
            lay["w_uq_p"], lay["w_kv_p"], rope_c, rope_sa, rope_sb)
        gl["w_uq_p"] = _weight_grad(cqn, dqraw, "grad_w_uq")
        gl["w_kv_p"] = _weight_grad(ckvn, dkvraw, "grad_w_ukv")
        du3, dpool_w, dpool_scale = _pool_bwd(sv["z"].reshape(b, lp, -1), da.reshape(b, lp, POOL_WIDTH),
                                              lay["pool_w"], lay["pool_scale"])
        dh, hn_m, dz, dg_mix = _in_proj_bwd(sv["h"], dh1, du3.reshape(t, POOL_WIDTH), dcq, dckv, dkr, dga, dgb,
                                            lay["norm_mix_g"], lay["w_in_pt"])
        gl["w_in_pt"] = _weight_grad(dz, hn_m, "grad_w_in")
        early_grads[li] = _early_grad_layouts(gl, heads)
        for n, val in (("norm_mix_g", dg_mix[0]), ("pool_w", dpool_w), ("pool_scale", dpool_scale[0]), ("q_norm_g", dg_q[0]),
                       ("kv_norm_g", dg_kv[0]), ("norm_ffn_g", dg_ffn[0])):
            g_small[n].insert(0, val)

    dh3 = dh.reshape(b, lp, d)
    grad_x = dh3[:, N_META:N_META + seq]
    d_meta_rows = dh3[:, :N_META]
    g_small = {n: jnp.stack(v) for n, v in g_small.items()}
    g_small["final_norm_g"] = d_final[0]
    early_first_partial = early_first_codec["pack_grads"]({(n, 0): g for n, g in early_grads[0].items()})
    return loss, grad_x, d_meta_rows, g_small, early_first_partial, parts


def _mesh_place():
    x, y, c = lax.axis_index("x"), lax.axis_index("y"), lax.axis_index("c")
    others = [(1 - x, y), (x, 1 - y), (1 - x, 1 - y)]
    return x, y, c, 2 * x + y, others


def _remote(src, dst, send_sems, recv_sems, k, device):
    return pltpu.make_async_remote_copy(src_ref=src, dst_ref=dst, send_sem=send_sems.at[k], recv_sem=recv_sems.at[k],
                                        device_id=device, device_id_type=MESH)


def _gather_exchange(packed):
    def copy(p_ref, g_ref, send_sems, recv_sems, r, slot):
        _, _, c, _, others = _mesh_place()
        ox, oy = others[r]
        return _remote(p_ref, g_ref.at[slot], send_sems, recv_sems, r, (ox, oy, c))

    def start(p_ref, g_ref, send_sems, recv_sems):
        chip = _mesh_place()[3]
        for r in range(3):
            copy(p_ref, g_ref, send_sems, recv_sems, r, chip).start()

    def finish(p_ref, g_ref, send_sems, recv_sems):
        _, _, _, chip, others = _mesh_place()
        for r, (ox, oy) in enumerate(others):
            copy(p_ref, g_ref, send_sems, recv_sems, r, 2 * ox + oy).wait_recv()
        for r in range(3):
            copy(p_ref, g_ref, send_sems, recv_sems, r, chip).wait_send()

    return dict(name="gather", operand=packed, copies=3, start=start, finish=finish,
                out_shape=jax.ShapeDtypeStruct((N_CHIPS,) + packed.shape, packed.dtype))


def _scatter_exchange(parts):
    flips = [(dx, dy, dc) for dx in (0, 1) for dy in (0, 1) for dc in (0, 1)][1:]

    def copy(p_ref, got_ref, send_sems, recv_sems, k, arriving):
        x, y, c, _, _ = _mesh_place()
        dx, dy, dc = flips[k]
        tx, ty, tc = (1 - x if dx else x), (1 - y if dy else y), (1 - c if dc else c)
        slot = 4 * tx + 2 * ty + tc if arriving else 4 * x + 2 * y + c
        return _remote(p_ref.at[2 * tx + ty, tc], got_ref.at[slot], send_sems, recv_sems, k, (tx, ty, tc))

    def start(p_ref, got_ref, send_sems, recv_sems):
        for k in range(len(flips)):
            copy(p_ref, got_ref, send_sems, recv_sems, k, False).start()

    def finish(p_ref, got_ref, send_sems, recv_sems):
        for k in range(len(flips)):
            copy(p_ref, got_ref, send_sems, recv_sems, k, True).wait_recv()
        for k in range(len(flips)):
            copy(p_ref, got_ref, send_sems, recv_sems, k, False).wait_send()

    return dict(name="scatter", operand=parts, copies=len(flips), start=start, finish=finish,
                out_shape=jax.ShapeDtypeStruct((2 * N_CHIPS,) + parts.shape[2:], parts.dtype))


def _all_gather_shards(packed, meta_shard):
    _, rh, cols = packed.shape

    def body(p_ref, m_ref, g_ref, gm_ref, send_sems, recv_sems):
        x, y, c, chip, others = _mesh_place()
        sibling = (x, y, 1 - c)
        sends = []
        for r, (ox, oy) in enumerate(others):
            sends.append(_remote(p_ref.at[c], g_ref.at[chip, c], send_sems, recv_sems, r, (ox, oy, c)))
            sends.append(_remote(m_ref, gm_ref.at[chip], send_sems, recv_sems, 6 + r, (ox, oy, c)))
        for cp in sends:
            cp.start()
        for r, (ox, oy) in enumerate(others):
            src_chip = 2 * ox + oy
            _remote(p_ref.at[c], g_ref.at[src_chip, c], send_sems, recv_sems, r, (ox, oy, c)).wait_recv()
            passed = _remote(g_ref.at[src_chip, c], g_ref.at[src_chip, c], send_sems, recv_sems, 3 + r, sibling)
            passed.start()
            sends.append(passed)
        for r, (ox, oy) in enumerate(others):
            src_chip = 2 * ox + oy
            _remote(p_ref.at[c], g_ref.at[src_chip, 1 - c], send_sems, recv_sems, 3 + r, sibling).wait_recv()
            _remote(m_ref, gm_ref.at[src_chip], send_sems, recv_sems, 6 + r, (ox, oy, c)).wait_recv()
        for cp in sends:
            cp.wait_send()

    gathered, meta_all = pl.pallas_call(
        body, name="all_gather_shards",
        in_specs=[_any_spec(), _any_spec()], out_specs=[_any_spec(), _any_spec()],
        out_shape=[jax.ShapeDtypeStruct((N_CHIPS, 2, rh, cols), packed.dtype),
                   jax.ShapeDtypeStruct((N_CHIPS,) + meta_shard.shape, meta_shard.dtype)],
        scratch_shapes=[pltpu.SemaphoreType.DMA((9,)), pltpu.SemaphoreType.DMA((9,))],
    )(packed, meta_shard)
    chip = 2 * lax.axis_index("x") + lax.axis_index("y")
    return (lax.dynamic_update_index_in_dim(gathered, packed, chip, 0),
            lax.dynamic_update_index_in_dim(meta_all, meta_shard, chip, 0))


def _pair_exchange(give):
    def body(give_ref, got_ref, send_sems, recv_sems):
        x, y, c, _, _ = _mesh_place()
        cp = _remote(give_ref, got_ref, send_sems, recv_sems, 0, (x, y, 1 - c))
        cp.start()
        cp.wait()

    return pl.pallas_call(
        body, name="pair_exchange", in_specs=[_any_spec()], out_specs=_any_spec(),
        out_shape=jax.ShapeDtypeStruct(give.shape, give.dtype),
        scratch_shapes=[pltpu.SemaphoreType.DMA((1,)), pltpu.SemaphoreType.DMA((1,))],
    )(give)


def _chip_exchange(parts):
    def body(p_ref, got_ref, send_sems, recv_sems):
        _, _, c, chip, others = _mesh_place()
        sends = [_remote(p_ref.at[2 * ox + oy], got_ref.at[chip], send_sems, recv_sems, r, (ox, oy, c))
                 for r, (ox, oy) in enumerate(others)]
        for cp in sends:
            cp.start()
        for r, (ox, oy) in enumerate(others):
            _remote(p_ref.at[chip], got_ref.at[2 * ox + oy], send_sems, recv_sems, r, (ox, oy, c)).wait_recv()
        for cp in sends:
            cp.wait_send()

    got = pl.pallas_call(
        body, name="chip_exchange", in_specs=[_any_spec()], out_specs=_any_spec(),
        out_shape=jax.ShapeDtypeStruct(parts.shape, parts.dtype),
        scratch_shapes=[pltpu.SemaphoreType.DMA((3,)), pltpu.SemaphoreType.DMA((3,))],
    )(parts)
    chip = 2 * lax.axis_index("x") + lax.axis_index("y")
    own = lax.dynamic_index_in_dim(parts, chip, 0, keepdims=False)
    return lax.dynamic_update_index_in_dim(got, own, chip, 0)


def _pair_gather(half):
    def body(h_ref, out_ref, send_sems, recv_sems):
        x, y, c, _, _ = _mesh_place()
        cp = _remote(h_ref, out_ref.at[c], send_sems, recv_sems, 0, (x, y, 1 - c))
        cp.start()
        _remote(h_ref, out_ref.at[1 - c], send_sems, recv_sems, 0, (x, y, 1 - c)).wait_recv()
        cp.wait_send()

    both = pl.pallas_call(
        body, name="pair_gather", in_specs=[_any_spec()], out_specs=_any_spec(),
        out_shape=jax.ShapeDtypeStruct((2,) + half.shape, half.dtype),
        scratch_shapes=[pltpu.SemaphoreType.DMA((1,)), pltpu.SemaphoreType.DMA((1,))],
    )(half)
    return lax.dynamic_update_index_in_dim(both, half, lax.axis_index("c"), 0)


def _row_tile(rows, limit=PACK_TILE):
    if rows <= limit:
        return rows
    for tr in range(limit, 7, -8):
        if rows % tr == 0:
            return tr
    return rows


def _pair_add(keep, got):
    n, rh, cols = keep.shape
    tr = _row_tile(rh)

    def body(k_ref, g_ref, o_ref):
        o_ref[...] = (k_ref[...].astype(F32) + g_ref[...].astype(F32)).astype(BF16)

    spec = pl.BlockSpec((1, tr, cols), lambda j, i: (j, i, 0))
    return pl.pallas_call(
        body, name="pair_add", grid=(n, rh // tr), in_specs=[spec, spec], out_specs=spec,
        out_shape=jax.ShapeDtypeStruct(keep.shape, BF16),
        compiler_params=_params(("parallel", "parallel")),
    )(keep, got)


def _chip_sum(parts):
    n, rh, cols = parts.shape
    tr = _row_tile(rh)

    def body(p_ref, o_ref):
        total = p_ref[0].astype(F32)
        for k in range(1, n):
            total = total + p_ref[k].astype(F32)
        o_ref[...] = total

    return pl.pallas_call(
        body, name="chip_sum", grid=(rh // tr,),
        in_specs=[pl.BlockSpec((n, tr, cols), lambda i: (0, i, 0))], out_specs=pl.BlockSpec((tr, cols), lambda i: (i, 0)),
        out_shape=jax.ShapeDtypeStruct((rh, cols), F32),
        compiler_params=_params(("parallel",)),
    )(parts)


def _reduce_scatter(grads, c):
    keep = lax.dynamic_index_in_dim(grads, c, axis=1, keepdims=False)
    give = lax.dynamic_index_in_dim(grads, 1 - c, axis=1, keepdims=False)
    chip_partial = _pair_add(keep, _pair_exchange(give))
    return _pair_gather(_chip_sum(_chip_exchange(chip_partial)))


def _all_reduce_small(meta_rows, small):
    b, rm, cols = meta_rows.shape
    rows = rm + small.shape[0]

    def body(meta_ref, small_ref, out_ref, mine, pair_buf, chip_buf, send_sems, recv_sems):
        x, y, c, chip, others = _mesh_place()
        acc = meta_ref[0]
        for i in range(1, b):
            acc = acc + meta_ref[i]
        mine[0:rm, :] = acc
        mine[rm:rows, :] = small_ref[...]
        pair = _remote(mine, pair_buf, send_sems, recv_sems, 0, (x, y, 1 - c))
        pair.start()
        pair.wait()
        chip_buf[chip] = mine[...] + pair_buf[...]
        sends = [_remote(chip_buf.at[chip], chip_buf.at[chip], send_sems, recv_sems, 1 + r, (ox, oy, c))
                 for r, (ox, oy) in enumerate(others)]
        for cp in sends:
            cp.start()
        for r, (ox, oy) in enumerate(others):
            _remote(chip_buf.at[chip], chip_buf.at[2 * ox + oy], send_sems, recv_sems, 1 + r, (ox, oy, c)).wait_recv()
        for cp in sends:
            cp.wait_send()
        out_ref[...] = ((chip_buf[0] + chip_buf[1]) + chip_buf[2]) + chip_buf[3]

    return pl.pallas_call(
        body, name="all_reduce_small",
        in_specs=[_vmem_spec(), _vmem_spec()], out_specs=_vmem_spec(),
        out_shape=jax.ShapeDtypeStruct((rows, cols), F32),
        scratch_shapes=[pltpu.VMEM((rows, cols), F32), pltpu.VMEM((rows, cols), F32), pltpu.VMEM((N_CHIPS, rows, cols), F32),
                        pltpu.SemaphoreType.DMA((4,)), pltpu.SemaphoreType.DMA((4,))],
        compiler_params=pltpu.CompilerParams(vmem_limit_bytes=VMEM_LIMIT),
    )(meta_rows, small)


def _adamw(w, g, m, v):
    shape = w.shape
    cols = shape[-1]
    rows = w.size // cols
    tr = _row_tile(rows)

    def body(w_ref, g_ref, m_ref, v_ref, d_ref, m2_ref, v2_ref):
        grad = g_ref[...]
        m2 = ADAM_B1 * m_ref[...] + (1.0 - ADAM_B1) * grad
        v2 = ADAM_B2 * v_ref[...] + (1.0 - ADAM_B2) * jnp.square(grad)
        m_hat = m2 / (1.0 - ADAM_B1 ** ADAM_STEP)
        v_hat = v2 / (1.0 - ADAM_B2 ** ADAM_STEP)
        d_ref[...] = -ADAM_LR * (m_hat / (jnp.sqrt(v_hat) + ADAM_EPS) + ADAM_WD * w_ref[...])
        m2_ref[...] = m2
        v2_ref[...] = v2

    spec = pl.BlockSpec((tr, cols), lambda i: (i, 0))
    out = jax.ShapeDtypeStruct((rows, cols), F32)
    res = pl.pallas_call(
        body, name="adamw", grid=(rows // tr,), in_specs=[spec] * 4, out_specs=[spec] * 3, out_shape=[out] * 3,
        compiler_params=_params(("parallel",)),
    )(*(a.reshape(rows, cols) for a in (w, g, m, v)))
    return tuple(r.reshape(shape) for r in res)


def _pack_rows(arrays):
    flat = [a.reshape(-1, PACK_COLS) for a in arrays]
    counts = [f.shape[0] for f in flat]
    total = sum(counts)
    half = -(-total // 2)
    tiles = -(-half // PACK_TILE)
    padded = 2 * tiles * _round_up(-(-half // tiles), 16)
    if padded > total:
        flat.append(jnp.zeros((padded - total, PACK_COLS), flat[0].dtype))
    return jnp.concatenate(flat, axis=0), counts


def _unpack_rows(buffer, counts, shapes):
    out, start = [], 0
    for n, shape in zip(counts, shapes):
        out.append(buffer[..., start:start + n, :].reshape(buffer.shape[:-2] + tuple(shape)))
        start += n
    return out


def _group_codec(entries, weights):
    turned = [n in TRANSPOSED_WEIGHTS for n, _ in entries]
    shapes = [weights[n].shape[1:][::-1] if t else weights[n].shape[1:] for (n, _), t in zip(entries, turned)]
    by_rows = [t or SHARD_AXIS[n] == 1 for (n, _), t in zip(entries, turned)]
    packed, counts = _pack_rows([(weights[n][li].T if t else weights[n][li]).astype(BF16) for (n, li), t in zip(entries, turned)])
    rows = packed.shape[0]
    pad_rows = rows - sum(counts)

    def unpack(per_chip_packed):
        out = {}
        for entry, (s0, s1), rowwise, blk in zip(entries, shapes, by_rows, _unpack_rows(per_chip_packed, counts, shapes)):
            out[entry] = blk.reshape(N_CHIPS * s0, s1) if rowwise else jnp.transpose(blk, (1, 0, 2)).reshape(s0, N_CHIPS * s1)
        return out

    def pack_grads(whole):
        pieces = []
        for entry, (s0, s1), rowwise in zip(entries, shapes, by_rows):
            g = whole[entry]
            by_chip = g.reshape(N_CHIPS, s0, s1) if rowwise else jnp.transpose(g.reshape(s0, N_CHIPS, s1), (1, 0, 2))
            pieces.append(by_chip.reshape(N_CHIPS, -1, PACK_COLS))
        if pad_rows:
            pieces.append(jnp.zeros((N_CHIPS, pad_rows, PACK_COLS), BF16))
        return jnp.concatenate(pieces, axis=1).reshape(N_CHIPS, 2, rows // 2, PACK_COLS)

    def unpack_reduced(reduced):
        shards = _unpack_rows(reduced.reshape(rows, PACK_COLS), counts, shapes)
        return {entry: s.T if t else s for entry, s, t in zip(entries, shards, turned)}

    return dict(packed=packed, unpack=unpack, pack_grads=pack_grads, unpack_reduced=unpack_reduced)


def kernel(x, meta_tokens, norm_mix_g, w_in, pool_w, pool_scale, q_norm_g, kv_norm_g, w_uq, w_ukv, w_pa, w_pb, w_o, norm_ffn_g, w_gate, w_up, w_down, final_norm_g, loss_target, m_meta_tokens, m_norm_mix_g, m_w_in, m_pool_w, m_pool_scale, m_q_norm_g, m_kv_norm_g, m_w_uq, m_w_ukv, m_w_pa, m_w_pb, m_w_o, m_norm_ffn_g, m_w_gate, m_w_up, m_w_down, m_final_norm_g, v_meta_tokens, v_norm_mix_g, v_w_in, v_pool_w, v_pool_scale, v_q_norm_g, v_kv_norm_g, v_w_uq, v_w_ukv, v_w_pa, v_w_pb, v_w_o, v_norm_ffn_g, v_w_gate, v_w_up, v_w_down, v_final_norm_g):
    weights = dict(meta_tokens=meta_tokens, norm_mix_g=norm_mix_g, w_in=w_in, pool_w=pool_w, pool_scale=pool_scale,
                   q_norm_g=q_norm_g, kv_norm_g=kv_norm_g, w_uq=w_uq, w_ukv=w_ukv, w_pa=w_pa, w_pb=w_pb, w_o=w_o,
                   norm_ffn_g=norm_ffn_g, w_gate=w_gate, w_up=w_up, w_down=w_down, final_norm_g=final_norm_g)
    first = dict(meta_tokens=m_meta_tokens, norm_mix_g=m_norm_mix_g, w_in=m_w_in, pool_w=m_pool_w, pool_scale=m_pool_scale,
                 q_norm_g=m_q_norm_g, kv_norm_g=m_kv_norm_g, w_uq=m_w_uq, w_ukv=m_w_ukv, w_pa=m_w_pa, w_pb=m_w_pb, w_o=m_w_o,
                 norm_ffn_g=m_norm_ffn_g, w_gate=m_w_gate, w_up=m_w_up, w_down=m_w_down, final_norm_g=m_final_norm_g)
    second = dict(meta_tokens=v_meta_tokens, norm_mix_g=v_norm_mix_g, w_in=v_w_in, pool_w=v_pool_w, pool_scale=v_pool_scale,
                  q_norm_g=v_q_norm_g, kv_norm_g=v_kv_norm_g, w_uq=v_w_uq, w_ukv=v_w_ukv, w_pa=v_w_pa, w_pb=v_w_pb, w_o=v_w_o,
                  norm_ffn_g=v_norm_ffn_g, w_gate=v_w_gate, w_up=v_w_up, w_down=v_w_down, final_norm_g=v_final_norm_g)
    core = lax.axis_index("c")
    chip = 2 * lax.axis_index("x") + lax.axis_index("y")
    d = x.shape[-1]
    meta_cols = meta_tokens.shape[1]

    early_first = _group_codec([(n, 0) for n in EARLY_WEIGHTS], weights)
    riding = [_group_codec([(n, 0) for n in LATE_WEIGHTS] + [(n, 1) for n in EARLY_WEIGHTS], weights),
              _group_codec([(n, 1) for n in LATE_WEIGHTS], weights)]
    gathered, meta_all = _all_gather_shards(early_first["packed"].reshape(2, -1, PACK_COLS), meta_tokens)
    early_weights = early_first["unpack"](gathered.reshape(N_CHIPS, -1, PACK_COLS))
    meta_full = jnp.concatenate([meta_all[j] for j in range(N_CHIPS)], axis=1)
    small = {n: weights[n] for n in SMALL_WEIGHTS}

    loss, grad_x, d_meta_rows, g_small, early_partial, parts = _local_step(
        x, loss_target, meta_full, small, {n: early_weights[n, 0] for n in EARLY_WEIGHTS}, riding, early_first)

    shards = early_first["unpack_reduced"](_reduce_scatter(early_partial, core))
    for codec, from_all in zip(riding, parts):
        shards.update(codec["unpack_reduced"](_pair_gather(_chip_sum(from_all))))
    grads = {n: jnp.stack([shards[n, 0], shards[n, 1]]) for n in BIG_WEIGHTS}

    small_shapes = [weights[n].shape for n in SMALL_WEIGHTS]
    small_flat = jnp.concatenate([g_small[n].reshape(-1) for n in SMALL_WEIGHTS])
    small_len = small_flat.shape[0]
    small_rows = _round_up(-(-small_len // PACK_COLS), 8)
    small_pack = jnp.pad(small_flat, (0, small_rows * PACK_COLS - small_len)).reshape(small_rows, PACK_COLS)
    meta_rows = N_META * d // PACK_COLS
    summed = _all_reduce_small(d_meta_rows.reshape(-1, meta_rows, PACK_COLS), small_pack)
    grad_meta_full = summed[:meta_rows].reshape(N_META, d)
    grads["meta_tokens"] = lax.dynamic_slice_in_dim(grad_meta_full, chip * meta_cols, meta_cols, axis=1)
    small_sum = summed[meta_rows:].reshape(-1)
    start = 0
    for n, shape in zip(SMALL_WEIGHTS, small_shapes):
        size = 1
        for s in shape:
            size *= s
        grads[n] = small_sum[start:start + size].reshape(shape)
        start += size

    deltas, new_m, new_v = {}, {}, {}
    for n in WEIGHT_ORDER:
        deltas[n], new_m[n], new_v[n] = _adamw(weights[n], grads[n], first[n], second[n])

    total_loss = lax.psum(loss[0, 0], ("x", "y", "c"))
    return (total_loss, grad_x, *[grads[n] for n in WEIGHT_ORDER], *[deltas[n] for n in WEIGHT_ORDER],
            *[new_m[n] for n in WEIGHT_ORDER], *[new_v[n] for n in WEIGHT_ORDER])
```

```python
import functools

import jax
import jax.numpy as jnp
from jax import lax
from jax.experimental import pallas as pl
from jax.experimental.pallas import tpu as pltpu

F32 = jnp.float32
BF16 = jnp.bfloat16

N_META = 16
POOL_WINDOWS = (2, 4, 8, 16)
POOL_GROUP = 128
POOL_WIDTH = POOL_GROUP * len(POOL_WINDOWS)
QK_NOPE = 64
QK_ROPE = 32
V_DIM = 64
QK_DIM = QK_NOPE + QK_ROPE
Q_RANK = 256
KV_RANK = 128
HEAD_PAD = 128
SM_SCALE = QK_DIM ** -0.5
ROPE_THETA = 10000.0
NORM_EPS = 1e-6
MASK_VALUE = -1e30
Z_FIXED = POOL_WIDTH + Q_RANK + KV_RANK + HEAD_PAD

ADAM_LR = 0.001
ADAM_B1 = 0.9
ADAM_B2 = 0.999
ADAM_EPS = 1e-08
ADAM_WD = 0.01
ADAM_STEP = 10

N_CHIPS = 4
ATT_BLOCK = 256
SEQ_PAD = 128
ATT_Q_ROWS = 256
ATT_FWD_HEADS = 8
ATT_BWD_HEADS = 4
PACK_COLS = 1024
PACK_TILE = 512
VMEM_LIMIT = 60 * 1024 * 1024
MXU_DEPTH = 256

MESH = pl.DeviceIdType.MESH

BIG_WEIGHTS = ("w_in", "w_uq", "w_ukv", "w_pa", "w_pb", "w_o", "w_gate", "w_up", "w_down")
EARLY_WEIGHTS = ("w_in", "w_uq", "w_ukv")
LATE_WEIGHTS = ("w_pa", "w_pb", "w_o", "w_gate", "w_up", "w_down")
TRANSPOSED_WEIGHTS = ("w_in", "w_gate", "w_up")
SHARD_AXIS = {"w_in": 2, "w_uq": 2, "w_ukv": 2, "w_pa": 2, "w_pb": 1, "w_o": 1, "w_gate": 2, "w_up": 2, "w_down": 1}
SMALL_WEIGHTS = ("norm_mix_g", "pool_w", "pool_scale", "q_norm_g", "kv_norm_g", "norm_ffn_g", "final_norm_g")
WEIGHT_ORDER = ("meta_tokens", "norm_mix_g", "w_in", "pool_w", "pool_scale", "q_norm_g", "kv_norm_g", "w_uq", "w_ukv",
                "w_pa", "w_pb", "w_o", "norm_ffn_g", "w_gate", "w_up", "w_down", "final_norm_g")


def _round_up(n, m):
    return -(-n // m) * m


def _vmem_spec():
    return pl.BlockSpec(memory_space=pltpu.VMEM)


def _any_spec():
    return pl.BlockSpec(memory_space=pl.ANY)


def _row_block(tm, width, col_block=0):
    return pl.BlockSpec((tm, width), lambda i, cb=col_block: (i, cb))


def _params(sem, vmem=VMEM_LIMIT):
    return pltpu.CompilerParams(dimension_semantics=sem, vmem_limit_bytes=vmem)


def _token_tile(t, want):
    best = SEQ_PAD
    for tm in range(32, min(t, 2 * want) + 1, 32):
        if t % tm == 0 and abs(tm - want) < abs(best - want):
            best = tm
    return best


def _dot(a, b):
    return jnp.dot(a, b, preferred_element_type=F32)


def _dot_nt(a, b):
    return lax.dot_general(a, b, (((1,), (1,)), ((), ())), preferred_element_type=F32)


def _dot_tn(a, b):
    return lax.dot_general(a, b, (((0,), (0,)), ((), ())), preferred_element_type=F32)


def _rms_fwd(x, g):
    r = lax.rsqrt(jnp.mean(x * x, axis=-1, keepdims=True) + NORM_EPS)
    xh = x * r
    return xh * g, xh, r


def _rms_bwd(dy, xh, r, g):
    gdy = dy * g
    dx = r * (gdy - xh * jnp.mean(xh * gdy, axis=-1, keepdims=True))
    return dx, dy * xh


def _rope_fwd(x, c, sa, sb):
    return x * c + pltpu.roll(x, 16, 1) * sa + pltpu.roll(x, HEAD_PAD - 16, 1) * sb


def _rope_bwd(d, c, sa, sb):
    return d * c + pltpu.roll(d * sa, HEAD_PAD - 16, 1) + pltpu.roll(d * sb, 16, 1)


def _in_proj_fwd(h, g, w_in_pt):
    t, d = h.shape
    nz = w_in_pt.shape[0]
    tm = _token_tile(t, 512)

    def body(h_ref, g_ref, w_ref, z_ref):
        hn, _, _ = _rms_fwd(h_ref[...], g_ref[...])
        z_ref[...] = _dot_nt(hn.astype(BF16), w_ref[...]).astype(BF16)

    return pl.pallas_call(
        body, name="in_proj_fwd", grid=(t // tm,),
        in_specs=[_row_block(tm, d), _vmem_spec(), _vmem_spec()],
        out_specs=_row_block(tm, nz),
        out_shape=jax.ShapeDtypeStruct((t, nz), BF16),
        compiler_params=_params(("parallel",)),
    )(h, g, w_in_pt)


def _window_sum(x, w, row, forward):
    n = x.shape[0]
    s = x
    k = 1
    while k < w:
        if forward:
            s = s + jnp.where(row >= k, pltpu.roll(s, k, 0), 0.0)
        else:
            s = s + jnp.where(row < n - k, pltpu.roll(s, n - k, 0), 0.0)
        k *= 2
    return s


def _pool_fwd(z3, pool_w, pool_scale):
    b, lp, _ = z3.shape

    def body(u_ref, pw_ref, sc_ref, a_ref):
        row = lax.broadcasted_iota(jnp.int32, (lp, POOL_GROUP), 0)
        pos = row.astype(F32)
        for gi, w in enumerate(POOL_WINDOWS):
            cols = slice(gi * POOL_GROUP, (gi + 1) * POOL_GROUP)
            u = u_ref[0, :, cols].astype(F32)
            y = _window_sum(u, w, row, True) / jnp.minimum(pos + 1.0, float(w)) - u
            yw = _dot(y.astype(BF16), pw_ref[gi])
            a_ref[0, :, cols] = (yw * sc_ref[:, cols]).astype(BF16)

    return pl.pallas_call(
        body, name="pool_fwd", grid=(b,),
        in_specs=[pl.BlockSpec((1, lp, POOL_WIDTH), lambda i: (i, 0, 0)), _vmem_spec(), _vmem_spec()],
        out_specs=pl.BlockSpec((1, lp, POOL_WIDTH), lambda i: (i, 0, 0)),
        out_shape=jax.ShapeDtypeStruct((b, lp, POOL_WIDTH), BF16),
        compiler_params=_params(("parallel",)),
    )(z3, pool_w, pool_scale)


def _qkv_fwd(z, g_q, g_kv, w_uq_p, w_kv_p, rope_c, rope_sa, rope_sb):
    t = z.shape[0]
    hw = w_uq_p.shape[1]
    heads = hw // HEAD_PAD
    tm = _token_tile(t, 512)

    def body(cq_ref, ckv_ref, kr_ref, gq_ref, gkv_ref, wq_ref, wkv_ref, c_ref, sa_ref, sb_ref, q_ref, k_ref, v_ref):
        c, sa, sb = c_ref[...], sa_ref[...], sb_ref[...]
        cqn, _, _ = _rms_fwd(cq_ref[...].astype(F32), gq_ref[...])
        qraw = _dot(cqn.astype(BF16), wq_ref[...])
        ckvn, _, _ = _rms_fwd(ckv_ref[...].astype(F32), gkv_ref[...])
        kvraw = _dot(ckvn.astype(BF16), wkv_ref[...])
        kr = _rope_fwd(kr_ref[...].astype(F32), c, sa, sb)
        for hd in range(heads):
            cols = slice(hd * HEAD_PAD, (hd + 1) * HEAD_PAD)
            q_ref[:, cols] = (_rope_fwd(qraw[:, cols], c, sa, sb) * SM_SCALE).astype(BF16)
            k_ref[:, cols] = (kvraw[:, cols] + kr).astype(BF16)
        v_ref[...] = kvraw[:, hw:].astype(BF16)

    out = jax.ShapeDtypeStruct((t, hw), BF16)
    return pl.pallas_call(
        body, name="qkv_fwd", grid=(t // tm,),
        in_specs=[_row_block(tm, Q_RANK, POOL_WIDTH // Q_RANK),
                  _row_block(tm, KV_RANK, (POOL_WIDTH + Q_RANK) // KV_RANK),
                  _row_block(tm, HEAD_PAD, (POOL_WIDTH + Q_RANK + KV_RANK) // HEAD_PAD),
                  _vmem_spec(), _vmem_spec(), _vmem_spec(), _vmem_spec(),
                  _row_block(tm, HEAD_PAD), _row_block(tm, HEAD_PAD), _row_block(tm, HEAD_PAD)],
        out_specs=[_row_block(tm, hw)] * 3,
        out_shape=[out, out, out],
        compiler_params=_params(("parallel",)),
    )(z, z, z, g_q, g_kv, w_uq_p, w_kv_p, rope_c, rope_sa, rope_sb)


def _heads_per_step(heads, want):
    while heads % want:
        want //= 2
    return want


def _causal_mask(rows):
    row = lax.broadcasted_iota(jnp.int32, (rows, rows), 0)
    col = lax.broadcasted_iota(jnp.int32, (rows, rows), 1)
    return col <= row


def _attn_blocks(real_len):
    tail_start = (-(-real_len // ATT_BLOCK) - 1) * ATT_BLOCK
    return tail_start // ATT_BLOCK, tail_start, _round_up(real_len - tail_start, SEQ_PAD)


def _call_with_exchange(body, exchange, *, name, grid, in_specs, out_specs, out_shape, scratch_shapes, operands):
    if exchange is None:
        return pl.pallas_call(body, name=name, grid=grid, in_specs=in_specs, out_specs=out_specs, out_shape=out_shape,
                              scratch_shapes=scratch_shapes,
                              compiler_params=_params(("parallel",) + ("arbitrary",) * (len(grid) - 1)))(*operands)
    n_in, n_out, n_scratch = len(in_specs), len(out_specs), len(scratch_shapes)

    def riding(*refs):
        ins, src = refs[:n_in], refs[n_in]
        outs, dst = refs[n_in + 1:n_in + 1 + n_out], refs[n_in + 1 + n_out]
        scratch = refs[n_in + 2 + n_out:n_in + 2 + n_out + n_scratch]
        send_sems, recv_sems = refs[n_in + 2 + n_out + n_scratch:]
        steps = [pl.program_id(a) for a in range(len(grid))]

        @pl.when(functools.reduce(jnp.logical_and, [s == 0 for s in steps]))
        def _():
            exchange["start"](src, dst, send_sems, recv_sems)

        body(*ins, *outs, *scratch)

        @pl.when(functools.reduce(jnp.logical_and, [s == g - 1 for s, g in zip(steps, grid)]))
        def _():
            exchange["finish"](src, dst, send_sems, recv_sems)

    n = exchange["copies"]
    return pl.pallas_call(
        riding, name=name + "_" + exchange["name"], grid=grid,
        in_specs=list(in_specs) + [_any_spec()], out_specs=list(out_specs) + [_any_spec()],
        out_shape=list(out_shape) + [exchange["out_shape"]],
        scratch_shapes=list(scratch_shapes) + [pltpu.SemaphoreType.DMA((n,)), pltpu.SemaphoreType.DMA((n,))],
        compiler_params=_params(("arbitrary",) * len(grid)),
    )(*operands, exchange["operand"])


def _attn_fwd(q3, k3, v3, real_len, exchange=None):
    b, lp, hw = q3.shape
    heads = hw // HEAD_PAD
    tb = ATT_BLOCK
    nfull, tail_start, tail = _attn_blocks(real_len)
    done = tail_start + tail
    hpg = _heads_per_step(heads, ATT_FWD_HEADS)
    width = hpg * HEAD_PAD

    def body(q_ref, k_ref, v_ref, o_ref, lse_ref):
        group = pl.program_id(1)

        @pl.when(group == 0)
        def _():
            lse_ref[...] = jnp.zeros_like(lse_ref)

        def q_rows(r0, rows, whole_kv_blocks, back):
            def kv_step(c0, keys, states, mask):
                out = []
                for hd, (m, l, acc) in enumerate(states):
                    cols = slice(hd * HEAD_PAD, (hd + 1) * HEAD_PAD)
                    s = _dot_nt(q_ref[0, pl.ds(r0, rows), cols], k_ref[0, pl.ds(c0, keys), cols])
                    if mask is not None:
                        s = jnp.where(mask, s, MASK_VALUE)
                    m_new = jnp.maximum(m, jnp.max(s, axis=-1, keepdims=True))
                    alpha = jnp.exp(m - m_new)
                    p = jnp.exp(s - m_new)
                    l = alpha * l + jnp.sum(p, axis=-1, keepdims=True)
                    acc = alpha * acc + _dot(p.astype(BF16), v_ref[0, pl.ds(c0, keys), cols])
                    out.append((m_new, l, acc))
                return tuple(out)

            init = tuple((jnp.full((rows, 1), MASK_VALUE, F32), jnp.zeros((rows, 1), F32), jnp.zeros((rows, HEAD_PAD), F32))
                         for _ in range(hpg))
            states = lax.fori_loop(0, whole_kv_blocks, lambda j, st: kv_step(pl.multiple_of(j * tb, tb), tb, st, None), init)
            query = lax.broadcasted_iota(jnp.int32, (rows, back + rows), 0)
            key = lax.broadcasted_iota(jnp.int32, (rows, back + rows), 1)
            states = kv_step(pl.multiple_of(r0 - back, SEQ_PAD), back + rows, states, key <= query + back)
            lane = lax.broadcasted_iota(jnp.int32, (rows, HEAD_PAD), 1)
            lse_rows = lse_ref[0, pl.ds(r0, rows), :]
            for hd, (m, l, acc) in enumerate(states):
                o_ref[0, pl.ds(r0, rows), hd * HEAD_PAD:(hd + 1) * HEAD_PAD] = (acc / l).astype(BF16)
                lse_rows = jnp.where(lane == group * hpg + hd, m + jnp.log(l), lse_rows)
            lse_ref[0, pl.ds(r0, rows), :] = lse_rows

        def whole_block(i, carry):
            for back in range(0, tb, ATT_Q_ROWS):
                q_rows(pl.multiple_of(i * tb + back, ATT_Q_ROWS), ATT_Q_ROWS, i, back)
            return carry

        lax.fori_loop(0, nfull, whole_block, 0)
        for back in range(0, tail, ATT_Q_ROWS):
            q_rows(tail_start + back, min(ATT_Q_ROWS, tail - back), nfull, back)
        if done < lp:
            o_ref[0, done:lp, :] = jnp.zeros((lp - done, width), BF16)

    head_spec = pl.BlockSpec((1, lp, width), lambda bi, hi: (bi, 0, hi))
    return _call_with_exchange(
        body, exchange, name="attn_fwd", grid=(b, heads // hpg),
        in_specs=[head_spec, head_spec, head_spec],
        out_specs=[head_spec, pl.BlockSpec((1, lp, HEAD_PAD), lambda bi, hi: (bi, 0, 0))],
        out_shape=[jax.ShapeDtypeStruct((b, lp, hw), BF16), jax.ShapeDtypeStruct((b, lp, HEAD_PAD), F32)],
        scratch_shapes=[], operands=(q3, k3, v3))


def _merge_fwd(h, z, a, o, w_pa, w_pb_p, w_o):
    t, d = h.shape
    hw = o.shape[1]
    tm = _token_tile(t, 512)
    gate_block = Z_FIXED // d

    def body(h_ref, ga_ref, gb_ref, a_ref, o_ref, wpa_ref, wpb_ref, wo_ref, h1_ref, pa_ref, pb_ref):
        pa = _dot(a_ref[...], wpa_ref[...])
        pb = _dot(o_ref[...], wpb_ref[...])
        merged = jax.nn.sigmoid(ga_ref[...].astype(F32)) * pa + jax.nn.sigmoid(gb_ref[...].astype(F32)) * pb
        h1_ref[...] = h_ref[...] + _dot(merged.astype(BF16), wo_ref[...])
        pa_ref[...] = pa.astype(BF16)
        pb_ref[...] = pb.astype(BF16)

    return pl.pallas_call(
        body, name="merge_fwd", grid=(t // tm,),
        in_specs=[_row_block(tm, d), _row_block(tm, d, gate_block), _row_block(tm, d, gate_block + 1),
                  _row_block(tm, POOL_WIDTH), _row_block(tm, hw), _vmem_spec(), _vmem_spec(), _vmem_spec()],
        out_specs=[_row_block(tm, d)] * 3,
        out_shape=[jax.ShapeDtypeStruct((t, d), F32), jax.ShapeDtypeStruct((t, d), BF16), jax.ShapeDtypeStruct((t, d), BF16)],
        compiler_params=_params(("parallel",)),
    )(h, z, z, a, o, w_pa, w_pb_p, w_o)


def _ffn_fwd(h, g, w_gate_t, w_up_t, w_down):
    t, d = h.shape
    f = w_gate_t.shape[0]
    tm = _token_tile(t, 256)

    def body(h_ref, g_ref, wg_ref, wu_ref, wd_ref, h2_ref, a_ref, b_ref):
        x = h_ref[...]
        hn, _, _ = _rms_fwd(x, g_ref[...])
        hn = hn.astype(BF16)
        ga = _dot_nt(hn, wg_ref[...])
        up = _dot_nt(hn, wu_ref[...])
        act = ga * jax.nn.sigmoid(ga) * up
        h2_ref[...] = x + _dot(act.astype(BF16), wd_ref[...])
        a_ref[...] = ga.astype(BF16)
        b_ref[...] = up.astype(BF16)

    return pl.pallas_call(
        body, name="ffn_fwd", grid=(t // tm,),
        in_specs=[_row_block(tm, d), _vmem_spec(), _vmem_spec(), _vmem_spec(), _vmem_spec()],
        out_specs=[_row_block(tm, d), _row_block(tm, f), _row_block(tm, f)],
        out_shape=[jax.ShapeDtypeStruct((t, d), F32), jax.ShapeDtypeStruct((t, f), BF16), jax.ShapeDtypeStruct((t, f), BF16)],
        compiler_params=_params(("parallel",)),
    )(h, g, w_gate_t, w_up_t, w_down)


def _loss_head(h, g, target, valid):
    t, d = h.shape
    tm = _token_tile(t, 512)

    def body(h_ref, g_ref, t_ref, valid_ref, dh_ref, loss_ref, dg_ref):
        @pl.when(pl.program_id(0) == 0)
        def _():
            loss_ref[...] = jnp.zeros_like(loss_ref)
            dg_ref[...] = jnp.zeros_like(dg_ref)

        gain = g_ref[...]
        y, xh, r = _rms_fwd(h_ref[...], gain)
        err = (y - t_ref[...]) * valid_ref[...]
        per_row = jnp.sum(err * err, axis=-1, keepdims=True) / d
        loss_ref[...] += 0.5 * jnp.sum(per_row, axis=0, keepdims=True)
        dx, dg_rows = _rms_bwd(err / d, xh, r, gain)
        dh_ref[...] = dx
        dg_ref[...] += jnp.sum(dg_rows, axis=0, keepdims=True)

    return pl.pallas_call(
        body, name="loss_head", grid=(t // tm,),
        in_specs=[_row_block(tm, d), _vmem_spec(), _row_block(tm, d), _row_block(tm, 1)],
        out_specs=[_row_block(tm, d), pl.BlockSpec((1, 1), lambda i: (0, 0)), pl.BlockSpec((1, d), lambda i: (0, 0))],
        out_shape=[jax.ShapeDtypeStruct((t, d), F32), jax.ShapeDtypeStruct((1, 1), F32), jax.ShapeDtypeStruct((1, d), F32)],
        compiler_params=_params(("arbitrary",)),
    )(h, g, target, valid)


def _weight_grad(x, y, name):
    t, k = x.shape
    n = y.shape[1]
    tm = _token_tile(t, 8 * MXU_DEPTH)
    tn = n
    while k * tn * 4 > 8 * 1024 * 1024 and tn % 256 == 0:
        tn //= 2
    steps = t // tm

    def body(x_ref, y_ref, o_ref, acc):
        @pl.when(pl.program_id(1) == 0)
        def _():
            acc[...] = jnp.zeros_like(acc)

        acc[...] += _dot_tn(x_ref[...].astype(BF16), y_ref[...].astype(BF16))

        @pl.when(pl.program_id(1) == steps - 1)
        def _():
            o_ref[...] = acc[...].astype(BF16)

    return pl.pallas_call(
        body, name=name, grid=(n // tn, steps),
        in_specs=[pl.BlockSpec((tm, k), lambda j, i: (i, 0)), pl.BlockSpec((tm, tn), lambda j, i: (i, j))],
        out_specs=pl.BlockSpec((k, tn), lambda j, i: (0, j)),
        out_shape=jax.ShapeDtypeStruct((k, n), BF16),
        scratch_shapes=[pltpu.VMEM((k, tn), F32)],
        compiler_params=_params(("parallel", "arbitrary")),
    )(x, y)


def _ffn_bwd(h, dh2, a, b, g, w_gate_t, w_up_t, w_down):
    t, d = h.shape
    f = a.shape[1]
    tm = _token_tile(t, 256)

    def body(h_ref, dh2_ref, a_ref, b_ref, g_ref, wg_ref, wu_ref, wd_ref, dh_ref, hn_ref, act_ref, da_ref, db_ref, dg_ref):
        @pl.when(pl.program_id(0) == 0)
        def _():
            dg_ref[...] = jnp.zeros_like(dg_ref)

        gain = g_ref[...]
        hn, xh, r = _rms_fwd(h_ref[...], gain)
        hn_ref[...] = hn.astype(BF16)
        dh2 = dh2_ref[...]
        dact = _dot_nt(dh2.astype(BF16), wd_ref[...])
        ga = a_ref[...].astype(F32)
        up = b_ref[...].astype(F32)
        sg = jax.nn.sigmoid(ga)
        silu = ga * sg
        act_ref[...] = (silu * up).astype(BF16)
        da = (dact * up * (sg * (1.0 + ga * (1.0 - sg)))).astype(BF16)
        db = (dact * silu).astype(BF16)
        da_ref[...] = da
        db_ref[...] = db
        dhn = _dot(da, wg_ref[...]) + _dot(db, wu_ref[...])
        dx, dg_rows = _rms_bwd(dhn, xh, r, gain)
        dh_ref[...] = dh2 + dx
        dg_ref[...] += jnp.sum(dg_rows, axis=0, keepdims=True)

    return pl.pallas_call(
        body, name="ffn_bwd", grid=(t // tm,),
        in_specs=[_row_block(tm, d), _row_block(tm, d), _row_block(tm, f), _row_block(tm, f),
                  _vmem_spec(), _vmem_spec(), _vmem_spec(), _vmem_spec()],
        out_specs=[_row_block(tm, d), _row_block(tm, d), _row_block(tm, f), _row_block(tm, f), _row_block(tm, f),
                   pl.BlockSpec((1, d), lambda i: (0, 0))],
        out_shape=[jax.ShapeDtypeStruct((t, d), F32), jax.ShapeDtypeStruct((t, d), BF16), jax.ShapeDtypeStruct((t, f), BF16),
                   jax.ShapeDtypeStruct((t, f), BF16), jax.ShapeDtypeStruct((t, f), BF16), jax.ShapeDtypeStruct((1, d), F32)],
        compiler_params=_params(("arbitrary",)),
    )(h, dh2, a, b, g, w_gate_t, w_up_t, w_down)


def _merge_bwd(dh1, z, pa, pb, w_o, w_pa, w_pb_p):
    t, d = dh1.shape
    hw = w_pb_p.shape[0]
    tm = _token_tile(t, 512)
    gate_block = Z_FIXED // d

    def body(dh_ref, ga_ref, gb_ref, pa_ref, pb_ref, wo_ref, wpa_ref, wpb_ref,
             mg_ref, dpa_ref, dpb_ref, dga_ref, dgb_ref, da_ref, do_ref):
        dm = _dot_nt(dh_ref[...].astype(BF16), wo_ref[...])
        sa = jax.nn.sigmoid(ga_ref[...].astype(F32))
        sb = jax.nn.sigmoid(gb_ref[...].astype(F32))
        pa = pa_ref[...].astype(F32)
        pb = pb_ref[...].astype(F32)
        mg_ref[...] = (sa * pa + sb * pb).astype(BF16)
        dpa = (dm * sa).astype(BF16)
        dpb = (dm * sb).astype(BF16)
        dpa_ref[...] = dpa
        dpb_ref[...] = dpb
        dga_ref[...] = (dm * pa * (sa * (1.0 - sa))).astype(BF16)
        dgb_ref[...] = (dm * pb * (sb * (1.0 - sb))).astype(BF16)
        da_ref[...] = _dot_nt(dpa, wpa_ref[...]).astype(BF16)
        do_ref[...] = _dot_nt(dpb, wpb_ref[...]).astype(BF16)

    wide = jax.ShapeDtypeStruct((t, d), BF16)
    return pl.pallas_call(
        body, name="merge_bwd", grid=(t // tm,),
        in_specs=[_row_block(tm, d), _row_block(tm, d, gate_block), _row_block(tm, d, gate_block + 1),
                  _row_block(tm, d), _row_block(tm, d), _vmem_spec(), _vmem_spec(), _vmem_spec()],
        out_specs=[_row_block(tm, d)] * 5 + [_row_block(tm, POOL_WIDTH), _row_block(tm, hw)],
        out_shape=[wide] * 5 + [jax.ShapeDtypeStruct((t, POOL_WIDTH), BF16), jax.ShapeDtypeStruct((t, hw), BF16)],
        compiler_params=_params(("parallel",)),
    )(dh1, z, z, pa, pb, w_o, w_pa, w_pb_p)


def _attn_bwd(q3, k3, v3, o3, do3, lse3, real_len, exchange=None):
    b, lp, hw = q3.shape
    heads = hw // HEAD_PAD
    tb = ATT_BLOCK
    nfull, tail_start, tail = _attn_blocks(real_len)
    done = tail_start + tail
    hpg = _heads_per_step(heads, ATT_BWD_HEADS)
    width = hpg * HEAD_PAD

    def body(q_ref, k_ref, v_ref, o_ref, do_ref, lse_ref, dq_ref, dk_ref, dv_ref, dqt_acc, lse_row, delta_row):
        group = pl.program_id(1)
        lse_t = jnp.transpose(lse_ref[0])
        head_of_row = lax.broadcasted_iota(jnp.int32, (HEAD_PAD, lp), 0)
        for hd in range(hpg):
            cols = slice(hd * HEAD_PAD, (hd + 1) * HEAD_PAD)
            lse_row[hd] = jnp.sum(jnp.where(head_of_row == group * hpg + hd, lse_t, 0.0), axis=0, keepdims=True)
            prod = do_ref[0, :, cols].astype(F32) * o_ref[0, :, cols].astype(F32)
            delta_row[hd] = jnp.sum(jnp.transpose(prod), axis=0, keepdims=True)
        dqt_acc[...] = jnp.zeros_like(dqt_acc)

        def kv_rows(c0, keys, whole_q_blocks_from):
            k_t = [jnp.transpose(k_ref[0, pl.ds(c0, keys), hd * HEAD_PAD:(hd + 1) * HEAD_PAD].astype(F32)).astype(BF16)
                   for hd in range(hpg)]

            def q_step(r0, rows, states, mask):
                out = []
                for hd, (dk, dv) in enumerate(states):
                    cols = slice(hd * HEAD_PAD, (hd + 1) * HEAD_PAD)
                    q = q_ref[0, pl.ds(r0, rows), cols]
                    do = do_ref[0, pl.ds(r0, rows), cols]
                    s_t = _dot_nt(k_ref[0, pl.ds(c0, keys), cols], q)
                    if mask is not None:
                        s_t = jnp.where(mask, s_t, MASK_VALUE)
                    p_t = jnp.exp(s_t - lse_row[hd, :, pl.ds(r0, rows)])
                    dp_t = _dot_nt(v_ref[0, pl.ds(c0, keys), cols], do)
                    ds_t = (p_t * (dp_t - delta_row[hd, :, pl.ds(r0, rows)])).astype(BF16)
                    dv = dv + _dot(p_t.astype(BF16), do)
                    dk = dk + _dot(ds_t, q)
                    dqt_acc[cols, pl.ds(r0, rows)] += _dot(k_t[hd], ds_t)
                    out.append((dk, dv))
                return tuple(out)

            zero = jnp.zeros((keys, HEAD_PAD), F32)
            key_pos = lax.broadcasted_iota(jnp.int32, (keys, keys), 0)
            query_pos = lax.broadcasted_iota(jnp.int32, (keys, keys), 1)
            states = q_step(c0, keys, tuple((zero, zero) for _ in range(hpg)), key_pos <= query_pos)
            if whole_q_blocks_from is not None:
                states = lax.fori_loop(whole_q_blocks_from, nfull,
                                       lambda i, st: q_step(pl.multiple_of(i * tb, tb), tb, st, None), states)
                states = q_step(tail_start, tail, states, None)
            for hd, (dk, dv) in enumerate(states):
                cols = slice(hd * HEAD_PAD, (hd + 1) * HEAD_PAD)
                dk_ref[0, pl.ds(c0, keys), cols] = dk.astype(BF16)
                dv_ref[0, pl.ds(c0, keys), cols] = dv.astype(BF16)

        def whole_block(j, carry):
            kv_rows(pl.multiple_of(j * tb, tb), tb, j + 1)
            return carry

        lax.fori_loop(0, nfull, whole_block, 0)
        kv_rows(tail_start, tail, None)
        if done < lp:
            dk_ref[0, done:lp, :] = jnp.zeros((lp - done, width), BF16)
            dv_ref[0, done:lp, :] = jnp.zeros((lp - done, width), BF16)
        for hd in range(hpg):
            cols = slice(hd * HEAD_PAD, (hd + 1) * HEAD_PAD)
            dq_ref[0, :, cols] = jnp.transpose(dqt_acc[cols, :]).astype(BF16)

    head_spec = pl.BlockSpec((1, lp, width), lambda bi, hi: (bi, 0, hi))
    out = jax.ShapeDtypeStruct((b, lp, hw), BF16)
    return _call_with_exchange(
        body, exchange, name="attn_bwd", grid=(b, heads // hpg),
        in_specs=[head_spec] * 5 + [pl.BlockSpec((1, lp, HEAD_PAD), lambda bi, hi: (bi, 0, 0))],
        out_specs=[head_spec] * 3,
        out_shape=[out, out, out],
        scratch_shapes=[pltpu.VMEM((width, lp), F32), pltpu.VMEM((hpg, 1, lp), F32), pltpu.VMEM((hpg, 1, lp), F32)],
        operands=(q3, k3, v3, o3, do3, lse3))


def _qkv_bwd(dq, dk, dv, z, g_q, g_kv, w_uq_p, w_kv_p, rope_c, rope_sa, rope_sb):
    t, hw = dq.shape
    heads = hw // HEAD_PAD
    tm = _token_tile(t, 512)

    def body(dq_ref, dk_ref, dv_ref, cq_ref, ckv_ref, gq_ref, gkv_ref, wq_ref, wkv_ref, c_ref, sa_ref, sb_ref,
             dqraw_ref, dkvraw_ref, cqn_ref, ckvn_ref, dcq_ref, dckv_ref, dkr_ref, dgq_ref, dgkv_ref):
        @pl.when(pl.program_id(0) == 0)
        def _():
            dgq_ref[...] = jnp.zeros_like(dgq_ref)
            dgkv_ref[...] = jnp.zeros_like(dgkv_ref)

        c, sa, sb = c_ref[...], sa_ref[...], sb_ref[...]
        dkr = jnp.zeros((tm, HEAD_PAD), F32)
        for hd in range(heads):
            cols = slice(hd * HEAD_PAD, (hd + 1) * HEAD_PAD)
            dqraw_ref[:, cols] = _rope_bwd(dq_ref[:, cols].astype(F32) * SM_SCALE, c, sa, sb).astype(BF16)
            dkvraw_ref[:, cols] = dk_ref[:, cols]
            dkr = dkr + dk_ref[:, cols].astype(F32)
        dkvraw_ref[:, hw:] = dv_ref[...]
        lane = lax.broadcasted_iota(jnp.int32, (tm, HEAD_PAD), 1)
        dkr_ref[...] = jnp.where((lane >= QK_NOPE) & (lane < QK_DIM), _rope_bwd(dkr, c, sa, sb), 0.0).astype(BF16)

        gq = gq_ref[...]
        cqn, xh, r = _rms_fwd(cq_ref[...].astype(F32), gq)
        cqn_ref[...] = cqn.astype(BF16)
        dx, dg_rows = _rms_bwd(_dot_nt(dqraw_ref[...], wq_ref[...]), xh, r, gq)
        dcq_ref[...] = dx.astype(BF16)
        dgq_ref[...] += jnp.sum(dg_rows, axis=0, keepdims=True)

        gkv = gkv_ref[...]
        ckvn, xh, r = _rms_fwd(ckv_ref[...].astype(F32), gkv)
        ckvn_ref[...] = ckvn.astype(BF16)
        dx, dg_rows = _rms_bwd(_dot_nt(dkvraw_ref[...], wkv_ref[...]), xh, r, gkv)
        dckv_ref[...] = dx.astype(BF16)
        dgkv_ref[...] += jnp.sum(dg_rows, axis=0, keepdims=True)

    def shape(width, dtype=BF16):
        return jax.ShapeDtypeStruct((t, width), dtype)

    return pl.pallas_call(
        body, name="qkv_bwd", grid=(t // tm,),
        in_specs=[_row_block(tm, hw)] * 3
        + [_row_block(tm, Q_RANK, POOL_WIDTH // Q_RANK), _row_block(tm, KV_RANK, (POOL_WIDTH + Q_RANK) // KV_RANK)]
        + [_vmem_spec()] * 4 + [_row_block(tm, HEAD_PAD)] * 3,
        out_specs=[_row_block(tm, hw), _row_block(tm, 2 * hw), _row_block(tm, Q_RANK), _row_block(tm, KV_RANK),
                   _row_block(tm, Q_RANK), _row_block(tm, KV_RANK), _row_block(tm, HEAD_PAD),
                   pl.BlockSpec((1, Q_RANK), lambda i: (0, 0)), pl.BlockSpec((1, KV_RANK), lambda i: (0, 0))],
        out_shape=[shape(hw), shape(2 * hw), shape(Q_RANK), shape(KV_RANK), shape(Q_RANK), shape(KV_RANK), shape(HEAD_PAD),
                   jax.ShapeDtypeStruct((1, Q_RANK), F32), jax.ShapeDtypeStruct((1, KV_RANK), F32)],
        compiler_params=_params(("arbitrary",)),
    )(dq, dk, dv, z, z, g_q, g_kv, w_uq_p, w_kv_p, rope_c, rope_sa, rope_sb)


def _pool_bwd(z3, da3, pool_w, pool_scale):
    b, lp, _ = z3.shape
    groups = len(POOL_WINDOWS)

    def body(u_ref, da_ref, pw_ref, sc_ref, du_ref, dpw_ref, dsc_ref):
        @pl.when(pl.program_id(0) == 0)
        def _():
            dpw_ref[...] = jnp.zeros_like(dpw_ref)
            dsc_ref[...] = jnp.zeros_like(dsc_ref)

        row = lax.broadcasted_iota(jnp.int32, (lp, POOL_GROUP), 0)
        pos = row.astype(F32)
        for gi, w in enumerate(POOL_WINDOWS):
            cols = slice(gi * POOL_GROUP, (gi + 1) * POOL_GROUP)
            count = jnp.minimum(pos + 1.0, float(w))
            u = u_ref[0, :, cols].astype(F32)
            y = (_window_sum(u, w, row, True) / count - u).astype(BF16)
            yw = _dot(y, pw_ref[gi])
            da = da_ref[0, :, cols].astype(F32)
            dsc_ref[:, cols] += jnp.sum(da * yw, axis=0, keepdims=True)
            dyw = (da * sc_ref[:, cols]).astype(BF16)
            dpw_ref[gi] += _dot_tn(y, dyw)
            dy = _dot_nt(dyw, pw_ref[gi])
            du_ref[0, :, cols] = (_window_sum(dy / count, w, row, False) - dy).astype(BF16)

    return pl.pallas_call(
        body, name="pool_bwd", grid=(b,),
        in_specs=[pl.BlockSpec((1, lp, POOL_WIDTH), lambda i: (i, 0, 0)), pl.BlockSpec((1, lp, POOL_WIDTH), lambda i: (i, 0, 0)),
                  _vmem_spec(), _vmem_spec()],
        out_specs=[pl.BlockSpec((1, lp, POOL_WIDTH), lambda i: (i, 0, 0)),
                   pl.BlockSpec((groups, POOL_GROUP, POOL_GROUP), lambda i: (0, 0, 0)),
                   pl.BlockSpec((1, POOL_WIDTH), lambda i: (0, 0))],
        out_shape=[jax.ShapeDtypeStruct((b, lp, POOL_WIDTH), BF16), jax.ShapeDtypeStruct((groups, POOL_GROUP, POOL_GROUP), F32),
                   jax.ShapeDtypeStruct((1, POOL_WIDTH), F32)],
        compiler_params=_params(("arbitrary",)),
    )(z3, da3, pool_w, pool_scale)


def _in_proj_bwd(h, dh1, du, dcq, dckv, dkr, dga, dgb, g, w_in_pt):
    t, d = h.shape
    nz = w_in_pt.shape[0]
    tm = _token_tile(t, 512)
    widths = (POOL_WIDTH, Q_RANK, KV_RANK, HEAD_PAD, d, d)

    def body(h_ref, dh1_ref, du_ref, dcq_ref, dckv_ref, dkr_ref, dga_ref, dgb_ref, g_ref, w_ref, dh_ref, hn_ref, dz_ref, dg_ref):
        @pl.when(pl.program_id(0) == 0)
        def _():
            dg_ref[...] = jnp.zeros_like(dg_ref)

        gain = g_ref[...]
        hn, xh, r = _rms_fwd(h_ref[...], gain)
        hn_ref[...] = hn.astype(BF16)
        dhn = jnp.zeros((tm, d), F32)
        start = 0
        for piece, width in zip((du_ref, dcq_ref, dckv_ref, dkr_ref, dga_ref, dgb_ref), widths):
            val = piece[...]
            dz_ref[:, start:start + width] = val
            dhn = dhn + _dot(val, w_ref[start:start + width, :])
            start += width
        dx, dg_rows = _rms_bwd(dhn, xh, r, gain)
        dh_ref[...] = dh1_ref[...] + dx
        dg_ref[...] += jnp.sum(dg_rows, axis=0, keepdims=True)

    return pl.pallas_call(
        body, name="in_proj_bwd", grid=(t // tm,),
        in_specs=[_row_block(tm, d), _row_block(tm, d)] + [_row_block(tm, w) for w in widths] + [_vmem_spec(), _vmem_spec()],
        out_specs=[_row_block(tm, d), _row_block(tm, d), _row_block(tm, nz), pl.BlockSpec((1, d), lambda i: (0, 0))],
        out_shape=[jax.ShapeDtypeStruct((t, d), F32), jax.ShapeDtypeStruct((t, d), BF16), jax.ShapeDtypeStruct((t, nz), BF16),
                   jax.ShapeDtypeStruct((1, d), F32)],
        compiler_params=_params(("arbitrary",)),
    )(h, dh1, du, dcq, dckv, dkr, dga, dgb, g, w_in_pt)


def _pad_heads(w, heads, width):
    k = w.shape[0]
    w = w.reshape(k, heads, width)
    return jnp.pad(w, ((0, 0), (0, 0), (0, HEAD_PAD - width))).reshape(k, heads * HEAD_PAD)


def _unpad_heads(w, heads, width):
    k = w.shape[0]
    return w.reshape(k, heads, HEAD_PAD)[:, :, :width].reshape(k, heads * width)


def _early_layouts(w, heads):
    o3, o4 = POOL_WIDTH + Q_RANK + KV_RANK, POOL_WIDTH + Q_RANK + KV_RANK + QK_ROPE
    w_in_t = w["w_in"]
    rope_rows = jnp.pad(w_in_t[o3:o4], ((QK_NOPE, HEAD_PAD - QK_DIM), (0, 0)))
    w_in_pt = jnp.concatenate([w_in_t[:o3], rope_rows, w_in_t[o4:]], axis=0)
    w_uq_p = _pad_heads(w["w_uq"], heads, QK_DIM)
    kv = w["w_ukv"].reshape(KV_RANK, heads, QK_NOPE + V_DIM)
    w_k = jnp.pad(kv[:, :, :QK_NOPE], ((0, 0), (0, 0), (0, HEAD_PAD - QK_NOPE))).reshape(KV_RANK, heads * HEAD_PAD)
    w_v = jnp.pad(kv[:, :, QK_NOPE:], ((0, 0), (0, 0), (0, HEAD_PAD - V_DIM))).reshape(KV_RANK, heads * HEAD_PAD)
    return dict(w_in_pt=w_in_pt, w_uq_p=w_uq_p, w_kv_p=jnp.concatenate([w_k, w_v], axis=1))


def _late_layouts(w, heads):
    d = w["w_pb"].shape[1]
    w_pb_p = jnp.pad(w["w_pb"].reshape(heads, V_DIM, d), ((0, 0), (0, HEAD_PAD - V_DIM), (0, 0))).reshape(heads * HEAD_PAD, d)
    return dict(w_pa=w["w_pa"], w_pb_p=w_pb_p, w_o=w["w_o"], w_gate_t=w["w_gate"], w_up_t=w["w_up"], w_down=w["w_down"])


def _early_grad_layouts(g, heads):
    o3 = POOL_WIDTH + Q_RANK + KV_RANK
    gin = g["w_in_pt"]
    w_in = jnp.concatenate([gin[:o3], gin[o3 + QK_NOPE:o3 + QK_DIM], gin[o3 + HEAD_PAD:]], axis=0)
    hw = heads * HEAD_PAD
    gk = g["w_kv_p"][:, :hw].reshape(KV_RANK, heads, HEAD_PAD)[:, :, :QK_NOPE]
    gv = g["w_kv_p"][:, hw:].reshape(KV_RANK, heads, HEAD_PAD)[:, :, :V_DIM]
    w_ukv = jnp.concatenate([gk, gv], axis=2).reshape(KV_RANK, heads * (QK_NOPE + V_DIM))
    return dict(w_in=w_in, w_uq=_unpad_heads(g["w_uq_p"], heads, QK_DIM), w_ukv=w_ukv)


def _late_grad_layouts(g, heads):
    d = g["w_pb_p"].shape[1]
    w_pb = g["w_pb_p"].reshape(heads, HEAD_PAD, d)[:, :V_DIM].reshape(heads * V_DIM, d)
    return dict(w_pa=g["w_pa"], w_pb=w_pb, w_o=g["w_o"], w_gate=g["w_gate_t"], w_up=g["w_up_t"], w_down=g["w_down"])


def _rope_tables(lp, b):
    inv = 1.0 / (ROPE_THETA ** (jnp.arange(0, QK_ROPE, 2, dtype=F32) / QK_ROPE))
    ang = jnp.arange(lp, dtype=F32)[:, None] * inv[None, :]
    cos, sin = jnp.cos(ang), jnp.sin(ang)
    half = QK_ROPE // 2
    ones = jnp.ones((lp, QK_NOPE), F32)
    zeros_lo = jnp.zeros((lp, QK_NOPE), F32)
    zeros_hi = jnp.zeros((lp, HEAD_PAD - QK_DIM), F32)
    zeros_half = jnp.zeros((lp, half), F32)
    c = jnp.concatenate([ones, cos, cos, zeros_hi], axis=1)
    sa = jnp.concatenate([zeros_lo, zeros_half, sin, zeros_hi], axis=1)
    sb = jnp.concatenate([zeros_lo, -sin, zeros_half, zeros_hi], axis=1)
    return tuple(jnp.tile(tab, (b, 1)) for tab in (c, sa, sb))


def _local_step(x, loss_target, meta_tokens, small, early_first, riding, early_first_codec):
    b, seq, d = x.shape
    depth = 2
    heads = early_first["w_uq"].shape[1] // QK_DIM
    core = lax.axis_index("c")
    chip = 2 * lax.axis_index("x") + lax.axis_index("y")
    real_len = N_META + seq
    lp = _round_up(real_len, SEQ_PAD)
    t = b * lp
    pad = lp - N_META - seq

    meta = jnp.broadcast_to(meta_tokens[None], (b, N_META, d))
    h = jnp.concatenate([meta, x, jnp.zeros((b, pad, d), F32)], axis=1).reshape(t, d)
    target = jnp.pad(loss_target, ((0, 0), (N_META, pad), (0, 0))).reshape(t, d)
    pos = jnp.arange(lp)
    valid = jnp.tile(((pos >= N_META) & (pos < N_META + seq)).astype(F32), b).reshape(t, 1)
    rope_c, rope_sa, rope_sb = _rope_tables(lp, b)

    layers = []
    for li in range(depth):
        lay = dict(pool_w=small["pool_w"][li].astype(BF16), pool_scale=small["pool_scale"][li][None])
        for n in ("norm_mix_g", "q_norm_g", "kv_norm_g", "norm_ffn_g"):
            lay[n] = small[n][li][None]
        layers.append(lay)
    layers[0].update(_early_layouts(early_first, heads))

    saved = []
    for li in range(depth):
        lay = layers[li]
        z = _in_proj_fwd(h, lay["norm_mix_g"], lay["w_in_pt"])
        a = _pool_fwd(z.reshape(b, lp, -1), lay["pool_w"], lay["pool_scale"]).reshape(t, POOL_WIDTH)
        q, k, v = _qkv_fwd(z, lay["q_norm_g"], lay["kv_norm_g"], lay["w_uq_p"], lay["w_kv_p"], rope_c, rope_sa, rope_sb)
        hw = q.shape[1]
        packed = riding[li]["packed"]
        o3, lse, others = _attn_fwd(q.reshape(b, lp, hw), k.reshape(b, lp, hw), v.reshape(b, lp, hw), real_len,
                                    _gather_exchange(packed))
        arrived = riding[li]["unpack"](lax.dynamic_update_index_in_dim(others, packed, chip, 0))
        lay.update(_late_layouts({n: arrived[n, li] for n in LATE_WEIGHTS}, heads))
        if li + 1 < depth:
            layers[li + 1].update(_early_layouts({n: arrived[n, li + 1] for n in EARLY_WEIGHTS}, heads))
        o = o3.reshape(t, hw)
        h1, pa, pb = _merge_fwd(h, z, a, o, lay["w_pa"], lay["w_pb_p"], lay["w_o"])
        h2, fa, fb = _ffn_fwd(h1, lay["norm_ffn_g"], lay["w_gate_t"], lay["w_up_t"], lay["w_down"])
        saved.append(dict(h=h, z=z, a=a, q=q, k=k, v=v, o=o, lse=lse, pa=pa, pb=pb, h1=h1, fa=fa, fb=fb))
        h = h2

    dh, loss, d_final = _loss_head(h, small["final_norm_g"][None], target, valid)

    g_small = {n: [] for n in SMALL_WEIGHTS if n != "final_norm_g"}
    early_grads, parts = {}, [None] * depth
    for li in reversed(range(depth)):
        lay, sv = layers[li], saved[li]
        hw = sv["q"].shape[1]
        dh1, hn_f, act, dfa, dfb, dg_ffn = _ffn_bwd(sv["h1"], dh, sv["fa"], sv["fb"], lay["norm_ffn_g"],
                                                     lay["w_gate_t"], lay["w_up_t"], lay["w_down"])
        gl = dict(w_gate_t=_weight_grad(dfa, hn_f, "grad_w_gate"), w_up_t=_weight_grad(dfb, hn_f, "grad_w_up"),
                  w_down=_weight_grad(act, dh, "grad_w_down"))
        merged, dpa, dpb, dga, dgb, da, do = _merge_bwd(dh1, sv["z"], sv["pa"], sv["pb"], lay["w_o"], lay["w_pa"], lay["w_pb_p"])
        gl["w_o"] = _weight_grad(merged, dh1, "grad_w_o")
        gl["w_pa"] = _weight_grad(sv["a"], dpa, "grad_w_pa")
        gl["w_pb_p"] = _weight_grad(sv["o"], dpb, "grad_w_pb")
        to_send = {(n, li): g for n, g in _late_grad_layouts(gl, heads).items()}
        if li + 1 < depth:
            to_send.update({(n, li + 1): g for n, g in early_grads[li + 1].items()})
        sending = riding[li]["pack_grads"](to_send)
        shape3 = (b, lp, hw)
        dq3, dk3, dv3, from_others = _attn_bwd(
            sv["q"].reshape(shape3), sv["k"].reshape(shape3), sv["v"].reshape(shape3), sv["o"].reshape(shape3),
            do.reshape(shape3), sv["lse"], real_len, _scatter_exchange(sending))
        own = lax.dynamic_index_in_dim(lax.dynamic_index_in_dim(sending, chip, 0, keepdims=False), core, 0, keepdims=False)
        parts[li] = lax.dynamic_update_index_in_dim(from_others, own, 2 * chip + core, 0)
        dqraw, dkvraw, cqn, ckvn, dcq, dckv, dkr, dg_q, dg_kv = _qkv_bwd(
            dq3.reshape(t, hw), dk3.reshape(t, hw), dv3.reshape(t, hw), sv["z"], lay["q_norm_g"], lay["kv_norm_g"],
            lay["w_uq_p"], lay["w_kv_p"], rope_c, rope_sa, rope_sb)
        gl["w_uq_p"] = _weight_grad(cqn, dqraw, "grad_w_uq")
        gl["w_kv_p"] = _weight_grad(ckvn, dkvraw, "grad_w_ukv")
        du3, dpool_w, dpool_scale = _pool_bwd(sv["z"].reshape(b, lp, -1), da.reshape(b, lp, POOL_WIDTH),
                                              lay["pool_w"], lay["pool_scale"])
        dh, hn_m, dz, dg_mix = _in_proj_bwd(sv["h"], dh1, du3.reshape(t, POOL_WIDTH), dcq, dckv, dkr, dga, dgb,
                                            lay["norm_mix_g"], lay["w_in_pt"])
        gl["w_in_pt"] = _weight_grad(dz, hn_m, "grad_w_in")
        early_grads[li] = _early_grad_layouts(gl, heads)
        for n, val in (("norm_mix_g", dg_mix[0]), ("pool_w", dpool_w), ("pool_scale", dpool_scale[0]), ("q_norm_g", dg_q[0]),
                       ("kv_norm_g", dg_kv[0]), ("norm_ffn_g", dg_ffn[0])):
            g_small[n].insert(0, val)

    dh3 = dh.reshape(b, lp, d)
    grad_x = dh3[:, N_META:N_META + seq]
    d_meta_rows = dh3[:, :N_META]
    g_small = {n: jnp.stack(v) for n, v in g_small.items()}
    g_small["final_norm_g"] = d_final[0]
    early_first_partial = early_first_codec["pack_grads"]({(n, 0): g for n, g in early_grads[0].items()})
    return loss, grad_x, d_meta_rows, g_small, early_first_partial, parts


def _mesh_place():
    x, y, c = lax.axis_index("x"), lax.axis_index("y"), lax.axis_index("c")
    others = [(1 - x, y), (x, 1 - y), (1 - x, 1 - y)]
    return x, y, c, 2 * x + y, others


def _remote(src, dst, send_sems, recv_sems, k, device):
    return pltpu.make_async_remote_copy(src_ref=src, dst_ref=dst, send_sem=send_sems.at[k], recv_sem=recv_sems.at[k],
                                        device_id=device, device_id_type=MESH)


def _gather_exchange(packed):
    def copy(p_ref, g_ref, send_sems, recv_sems, r, slot):
        _, _, c, _, others = _mesh_place()
        ox, oy = others[r]
        return _remote(p_ref, g_ref.at[slot], send_sems, recv_sems, r, (ox, oy, c))

    def start(p_ref, g_ref, send_sems, recv_sems):
        chip = _mesh_place()[3]
        for r in range(3):
            copy(p_ref, g_ref, send_sems, recv_sems, r, chip).start()

    def finish(p_ref, g_ref, send_sems, recv_sems):
        _, _, _, chip, others = _mesh_place()
        for r, (ox, oy) in enumerate(others):
            copy(p_ref, g_ref, send_sems, recv_sems, r, 2 * ox + oy).wait_recv()
        for r in range(3):
            copy(p_ref, g_ref, send_sems, recv_sems, r, chip).wait_send()

    return dict(name="gather", operand=packed, copies=3, start=start, finish=finish,
                out_shape=jax.ShapeDtypeStruct((N_CHIPS,) + packed.shape, packed.dtype))


def _scatter_exchange(parts):
    flips = [(dx, dy, dc) for dx in (0, 1) for dy in (0, 1) for dc in (0, 1)][1:]

    def copy(p_ref, got_ref, send_sems, recv_sems, k, arriving):
        x, y, c, _, _ = _mesh_place()
        dx, dy, dc = flips[k]
        tx, ty, tc = (1 - x if dx else x), (1 - y if dy else y), (1 - c if dc else c)
        slot = 4 * tx + 2 * ty + tc if arriving else 4 * x + 2 * y + c
        return _remote(p_ref.at[2 * tx + ty, tc], got_ref.at[slot], send_sems, recv_sems, k, (tx, ty, tc))

    def start(p_ref, got_ref, send_sems, recv_sems):
        for k in range(len(flips)):
            copy(p_ref, got_ref, send_sems, recv_sems, k, False).start()

    def finish(p_ref, got_ref, send_sems, recv_sems):
        for k in range(len(flips)):
            copy(p_ref, got_ref, send_sems, recv_sems, k, True).wait_recv()
        for k in range(len(flips)):
            copy(p_ref, got_ref, send_sems, recv_sems, k, False).wait_send()

    return dict(name="scatter", operand=parts, copies=len(flips), start=start, finish=finish,
                out_shape=jax.ShapeDtypeStruct((2 * N_CHIPS,) + parts.shape[2:], parts.dtype))


def _all_gather_shards(packed, meta_shard):
    _, rh, cols = packed.shape

    def body(p_ref, m_ref, g_ref, gm_ref, send_sems, recv_sems):
        x, y, c, chip, others = _mesh_place()
        sibling = (x, y, 1 - c)
        sends = []
        for r, (ox, oy) in enumerate(others):
            sends.append(_remote(p_ref.at[c], g_ref.at[chip, c], send_sems, recv_sems, r, (ox, oy, c)))
            sends.append(_remote(m_ref, gm_ref.at[chip], send_sems, recv_sems, 6 + r, (ox, oy, c)))
        for cp in sends:
            cp.start()
        for r, (ox, oy) in enumerate(others):
            src_chip = 2 * ox + oy
            _remote(p_ref.at[c], g_ref.at[src_chip, c], send_sems, recv_sems, r, (ox, oy, c)).wait_recv()
            passed = _remote(g_ref.at[src_chip, c], g_ref.at[src_chip, c], send_sems, recv_sems, 3 + r, sibling)
            passed.start()
            sends.append(passed)
        for r, (ox, oy) in enumerate(others):
            src_chip = 2 * ox + oy
            _remote(p_ref.at[c], g_ref.at[src_chip, 1 - c], send_sems, recv_sems, 3 + r, sibling).wait_recv()
            _remote(m_ref, gm_ref.at[src_chip], send_sems, recv_sems, 6 + r, (ox, oy, c)).wait_recv()
        for cp in sends:
            cp.wait_send()

    gathered, meta_all = pl.pallas_call(
        body, name="all_gather_shards",
        in_specs=[_any_spec(), _any_spec()], out_specs=[_any_spec(), _any_spec()],
        out_shape=[jax.ShapeDtypeStruct((N_CHIPS, 2, rh, cols), packed.dtype),
                   jax.ShapeDtypeStruct((N_CHIPS,) + meta_shard.shape, meta_shard.dtype)],
        scratch_shapes=[pltpu.SemaphoreType.DMA((9,)), pltpu.SemaphoreType.DMA((9,))],
    )(packed, meta_shard)
    chip = 2 * lax.axis_index("x") + lax.axis_index("y")
    return (lax.dynamic_update_index_in_dim(gathered, packed, chip, 0),
            lax.dynamic_update_index_in_dim(meta_all, meta_shard, chip, 0))


def _pair_exchange(give):
    def body(give_ref, got_ref, send_sems, recv_sems):
        x, y, c, _, _ = _mesh_place()
        cp = _remote(give_ref, got_ref, send_sems, recv_sems, 0, (x, y, 1 - c))
        cp.start()
        cp.wait()

    return pl.pallas_call(
        body, name="pair_exchange", in_specs=[_any_spec()], out_specs=_any_spec(),
        out_shape=jax.ShapeDtypeStruct(give.shape, give.dtype),
        scratch_shapes=[pltpu.SemaphoreType.DMA((1,)), pltpu.SemaphoreType.DMA((1,))],
    )(give)


def _chip_exchange(parts):
    def body(p_ref, got_ref, send_sems, recv_sems):
        _, _, c, chip, others = _mesh_place()
        sends = [_remote(p_ref.at[2 * ox + oy], got_ref.at[chip], send_sems, recv_sems, r, (ox, oy, c))
                 for r, (ox, oy) in enumerate(others)]
        for cp in sends:
            cp.start()
        for r, (ox, oy) in enumerate(others):
            _remote(p_ref.at[chip], got_ref.at[2 * ox + oy], send_sems, recv_sems, r, (ox, oy, c)).wait_recv()
        for cp in sends:
            cp.wait_send()

    got = pl.pallas_call(
        body, name="chip_exchange", in_specs=[_any_spec()], out_specs=_any_spec(),
        out_shape=jax.ShapeDtypeStruct(parts.shape, parts.dtype),
        scratch_shapes=[pltpu.SemaphoreType.DMA((3,)), pltpu.SemaphoreType.DMA((3,))],
    )(parts)
    chip = 2 * lax.axis_index("x") + lax.axis_index("y")
    own = lax.dynamic_index_in_dim(parts, chip, 0, keepdims=False)
    return lax.dynamic_update_index_in_dim(got, own, chip, 0)


def _pair_gather(half):
    def body(h_ref, out_ref, send_sems, recv_sems):
        x, y, c, _, _ = _mesh_place()
        cp = _remote(h_ref, out_ref.at[c], send_sems, recv_sems, 0, (x, y, 1 - c))
        cp.start()
        _remote(h_ref, out_ref.at[1 - c], send_sems, recv_sems, 0, (x, y, 1 - c)).wait_recv()
        cp.wait_send()

    both = pl.pallas_call(
        body, name="pair_gather", in_specs=[_any_spec()], out_specs=_any_spec(),
        out_shape=jax.ShapeDtypeStruct((2,) + half.shape, half.dtype),
        scratch_shapes=[pltpu.SemaphoreType.DMA((1,)), pltpu.SemaphoreType.DMA((1,))],
    )(half)
    return lax.dynamic_update_index_in_dim(both, half, lax.axis_index("c"), 0)


def _row_tile(rows, limit=PACK_TILE):
    if rows <= limit:
        return rows
    for tr in range(limit, 7, -8):
        if rows % tr == 0:
            return tr
    return rows


def _pair_add(keep, got):
    n, rh, cols = keep.shape
    tr = _row_tile(rh)

    def body(k_ref, g_ref, o_ref):
        o_ref[...] = (k_ref[...].astype(F32) + g_ref[...].astype(F32)).astype(BF16)

    spec = pl.BlockSpec((1, tr, cols), lambda j, i: (j, i, 0))
    return pl.pallas_call(
        body, name="pair_add", grid=(n, rh // tr), in_specs=[spec, spec], out_specs=spec,
        out_shape=jax.ShapeDtypeStruct(keep.shape, BF16),
        compiler_params=_params(("parallel", "parallel")),
    )(keep, got)


def _chip_sum(parts):
    n, rh, cols = parts.shape
    tr = _row_tile(rh)

    def body(p_ref, o_ref):
        total = p_ref[0].astype(F32)
        for k in range(1, n):
            total = total + p_ref[k].astype(F32)
        o_ref[...] = total

    return pl.pallas_call(
        body, name="chip_sum", grid=(rh // tr,),
        in_specs=[pl.BlockSpec((n, tr, cols), lambda i: (0, i, 0))], out_specs=pl.BlockSpec((tr, cols), lambda i: (i, 0)),
        out_shape=jax.ShapeDtypeStruct((rh, cols), F32),
        compiler_params=_params(("parallel",)),
    )(parts)


def _reduce_scatter(grads, c):
    keep = lax.dynamic_index_in_dim(grads, c, axis=1, keepdims=False)
    give = lax.dynamic_index_in_dim(grads, 1 - c, axis=1, keepdims=False)
    chip_partial = _pair_add(keep, _pair_exchange(give))
    return _pair_gather(_chip_sum(_chip_exchange(chip_partial)))


def _all_reduce_small(meta_rows, small):
    b, rm, cols = meta_rows.shape
    rows = rm + small.shape[0]

    def body(meta_ref, small_ref, out_ref, mine, pair_buf, chip_buf, send_sems, recv_sems):
        x, y, c, chip, others = _mesh_place()
        acc = meta_ref[0]
        for i in range(1, b):
            acc = acc + meta_ref[i]
        mine[0:rm, :] = acc
        mine[rm:rows, :] = small_ref[...]
        pair = _remote(mine, pair_buf, send_sems, recv_sems, 0, (x, y, 1 - c))
        pair.start()
        pair.wait()
        chip_buf[chip] = mine[...] + pair_buf[...]
        sends = [_remote(chip_buf.at[chip], chip_buf.at[chip], send_sems, recv_sems, 1 + r, (ox, oy, c))
                 for r, (ox, oy) in enumerate(others)]
        for cp in sends:
            cp.start()
        for r, (ox, oy) in enumerate(others):
            _remote(chip_buf.at[chip], chip_buf.at[2 * ox + oy], send_sems, recv_sems, 1 + r, (ox, oy, c)).wait_recv()
        for cp in sends:
            cp.wait_send()
        out_ref[...] = ((chip_buf[0] + chip_buf[1]) + chip_buf[2]) + chip_buf[3]

    return pl.pallas_call(
        body, name="all_reduce_small",
        in_specs=[_vmem_spec(), _vmem_spec()], out_specs=_vmem_spec(),
        out_shape=jax.ShapeDtypeStruct((rows, cols), F32),
        scratch_shapes=[pltpu.VMEM((rows, cols), F32), pltpu.VMEM((rows, cols), F32), pltpu.VMEM((N_CHIPS, rows, cols), F32),
                        pltpu.SemaphoreType.DMA((4,)), pltpu.SemaphoreType.DMA((4,))],
        compiler_params=pltpu.CompilerParams(vmem_limit_bytes=VMEM_LIMIT),
    )(meta_rows, small)


def _adamw(w, g, m, v):
    shape = w.shape
    cols = shape[-1]
    rows = w.size // cols
    tr = _row_tile(rows)

    def body(w_ref, g_ref, m_ref, v_ref, d_ref, m2_ref, v2_ref):
        grad = g_ref[...]
        m2 = ADAM_B1 * m_ref[...] + (1.0 - ADAM_B1) * grad
        v2 = ADAM_B2 * v_ref[...] + (1.0 - ADAM_B2) * jnp.square(grad)
        m_hat = m2 / (1.0 - ADAM_B1 ** ADAM_STEP)
        v_hat = v2 / (1.0 - ADAM_B2 ** ADAM_STEP)
        d_ref[...] = -ADAM_LR * (m_hat / (jnp.sqrt(v_hat) + ADAM_EPS) + ADAM_WD * w_ref[...])
        m2_ref[...] = m2
        v2_ref[...] = v2

    spec = pl.BlockSpec((tr, cols), lambda i: (i, 0))
    out = jax.ShapeDtypeStruct((rows, cols), F32)
    res = pl.pallas_call(
        body, name="adamw", grid=(rows // tr,), in_specs=[spec] * 4, out_specs=[spec] * 3, out_shape=[out] * 3,
        compiler_params=_params(("parallel",)),
    )(*(a.reshape(rows, cols) for a in (w, g, m, v)))
    return tuple(r.reshape(shape) for r in res)


def _pack_rows(arrays):
    flat = [a.reshape(-1, PACK_COLS) for a in arrays]
    counts = [f.shape[0] for f in flat]
    total = sum(counts)
    half = -(-total // 2)
    tiles = -(-half // PACK_TILE)
    padded = 2 * tiles * _round_up(-(-half // tiles), 16)
    if padded > total:
        flat.append(jnp.zeros((padded - total, PACK_COLS), flat[0].dtype))
    return jnp.concatenate(flat, axis=0), counts


def _unpack_rows(buffer, counts, shapes):
    out, start = [], 0
    for n, shape in zip(counts, shapes):
        out.append(buffer[..., start:start + n, :].reshape(buffer.shape[:-2] + tuple(shape)))
        start += n
    return out


def _group_codec(entries, weights):
    turned = [n in TRANSPOSED_WEIGHTS for n, _ in entries]
    shapes = [weights[n].shape[1:][::-1] if t else weights[n].shape[1:] for (n, _), t in zip(entries, turned)]
    by_rows = [t or SHARD_AXIS[n] == 1 for (n, _), t in zip(entries, turned)]
    packed, counts = _pack_rows([(weights[n][li].T if t else weights[n][li]).astype(BF16) for (n, li), t in zip(entries, turned)])
    rows = packed.shape[0]
    pad_rows = rows - sum(counts)

    def unpack(per_chip_packed):
        out = {}
        for entry, (s0, s1), rowwise, blk in zip(entries, shapes, by_rows, _unpack_rows(per_chip_packed, counts, shapes)):
            out[entry] = blk.reshape(N_CHIPS * s0, s1) if rowwise else jnp.transpose(blk, (1, 0, 2)).reshape(s0, N_CHIPS * s1)
        return out

    def pack_grads(whole):
        pieces = []
        for entry, (s0, s1), rowwise in zip(entries, shapes, by_rows):
            g = whole[entry]
            by_chip = g.reshape(N_CHIPS, s0, s1) if rowwise else jnp.transpose(g.reshape(s0, N_CHIPS, s1), (1, 0, 2))
            pieces.append(by_chip.reshape(N_CHIPS, -1, PACK_COLS))
        if pad_rows:
            pieces.append(jnp.zeros((N_CHIPS, pad_rows, PACK_COLS), BF16))
        return jnp.concatenate(pieces, axis=1).reshape(N_CHIPS, 2, rows // 2, PACK_COLS)

    def unpack_reduced(reduced):
        shards = _unpack_rows(reduced.reshape(rows, PACK_COLS), counts, shapes)
        return {entry: s.T if t else s for entry, s, t in zip(entries, shards, turned)}

    return dict(packed=packed, unpack=unpack, pack_grads=pack_grads, unpack_reduced=unpack_reduced)


def kernel(x, meta_tokens, norm_mix_g, w_in, pool_w, pool_scale, q_norm_g, kv_norm_g, w_uq, w_ukv, w_pa, w_pb, w_o, norm_ffn_g, w_gate, w_up, w_down, final_norm_g, loss_target, m_meta_tokens, m_norm_mix_g, m_w_in, m_pool_w, m_pool_scale, m_q_norm_g, m_kv_norm_g, m_w_uq, m_w_ukv, m_w_pa, m_w_pb, m_w_o, m_norm_ffn_g, m_w_gate, m_w_up, m_w_down, m_final_norm_g, v_meta_tokens, v_norm_mix_g, v_w_in, v_pool_w, v_pool_scale, v_q_norm_g, v_kv_norm_g, v_w_uq, v_w_ukv, v_w_pa, v_w_pb, v_w_o, v_norm_ffn_g, v_w_gate, v_w_up, v_w_down, v_final_norm_g):
    weights = dict(meta_tokens=meta_tokens, norm_mix_g=norm_mix_g, w_in=w_in, pool_w=pool_w, pool_scale=pool_scale,
                   q_norm_g=q_norm_g, kv_norm_g=kv_norm_g, w_uq=w_uq, w_ukv=w_ukv, w_pa=w_pa, w_pb=w_pb, w_o=w_o,
                   norm_ffn_g=norm_ffn_g, w_gate=w_gate, w_up=w_up, w_down=w_down, final_norm_g=final_norm_g)
    first = dict(meta_tokens=m_meta_tokens, norm_mix_g=m_norm_mix_g, w_in=m_w_in, pool_w=m_pool_w, pool_scale=m_pool_scale,
                 q_norm_g=m_q_norm_g, kv_norm_g=m_kv_norm_g, w_uq=m_w_uq, w_ukv=m_w_ukv, w_pa=m_w_pa, w_pb=m_w_pb, w_o=m_w_o,
                 norm_ffn_g=m_norm_ffn_g, w_gate=m_w_gate, w_up=m_w_up, w_down=m_w_down, final_norm_g=m_final_norm_g)
    second = dict(meta_tokens=v_meta_tokens, norm_mix_g=v_norm_mix_g, w_in=v_w_in, pool_w=v_pool_w, pool_scale=v_pool_scale,
                  q_norm_g=v_q_norm_g, kv_norm_g=v_kv_norm_g, w_uq=v_w_uq, w_ukv=v_w_ukv, w_pa=v_w_pa, w_pb=v_w_pb, w_o=v_w_o,
                  norm_ffn_g=v_norm_ffn_g, w_gate=v_w_gate, w_up=v_w_up, w_down=v_w_down, final_norm_g=v_final_norm_g)
    core = lax.axis_index("c")
    chip = 2 * lax.axis_index("x") + lax.axis_index("y")
    d = x.shape[-1]
    meta_cols = meta_tokens.shape[1]

    early_first = _group_codec([(n, 0) for n in EARLY_WEIGHTS], weights)
    riding = [_group_codec([(n, 0) for n in LATE_WEIGHTS] + [(n, 1) for n in EARLY_WEIGHTS], weights),
              _group_codec([(n, 1) for n in LATE_WEIGHTS], weights)]
    gathered, meta_all = _all_gather_shards(early_first["packed"].reshape(2, -1, PACK_COLS), meta_tokens)
    early_weights = early_first["unpack"](gathered.reshape(N_CHIPS, -1, PACK_COLS))
    meta_full = jnp.concatenate([meta_all[j] for j in range(N_CHIPS)], axis=1)
    small = {n: weights[n] for n in SMALL_WEIGHTS}

    loss, grad_x, d_meta_rows, g_small, early_partial, parts = _local_step(
        x, loss_target, meta_full, small, {n: early_weights[n, 0] for n in EARLY_WEIGHTS}, riding, early_first)

    shards = early_first["unpack_reduced"](_reduce_scatter(early_partial, core))
    for codec, from_all in zip(riding, parts):
        shards.update(codec["unpack_reduced"](_pair_gather(_chip_sum(from_all))))
    grads = {n: jnp.stack([shards[n, 0], shards[n, 1]]) for n in BIG_WEIGHTS}

    small_shapes = [weights[n].shape for n in SMALL_WEIGHTS]
    small_flat = jnp.concatenate([g_small[n].reshape(-1) for n in SMALL_WEIGHTS])
    small_len = small_flat.shape[0]
    small_rows = _round_up(-(-small_len // PACK_COLS), 8)
    small_pack = jnp.pad(small_flat, (0, small_rows * PACK_COLS - small_len)).reshape(small_rows, PACK_COLS)
    meta_rows = N_META * d // PACK_COLS
    summed = _all_reduce_small(d_meta_rows.reshape(-1, meta_rows, PACK_COLS), small_pack)
    grad_meta_full = summed[:meta_rows].reshape(N_META, d)
    grads["meta_tokens"] = lax.dynamic_slice_in_dim(grad_meta_full, chip * meta_cols, meta_cols, axis=1)
    small_sum = summed[meta_rows:].reshape(-1)
    start = 0
    for n, shape in zip(SMALL_WEIGHTS, small_shapes):
        size = 1
        for s in shape:
            size *= s
        grads[n] = small_sum[start:start + size].reshape(shape)
        start += size

    deltas, new_m, new_v = {}, {}, {}
    for n in WEIGHT_ORDER:
        deltas[n], new_m[n], new_v[n] = _adamw(weights[n], grads[n], first[n], second[n])

    total_loss = lax.psum(loss[0, 0], ("x", "y", "c"))
    return (total_loss, grad_x, *[grads[n] for n in WEIGHT_ORDER], *[deltas[n] for n in WEIGHT_ORDER],
            *[new_m[n] for n in WEIGHT_ORDER], *[new_v[n] for n in WEIGHT_ORDER])
```

```python
import functools

import jax
import jax.numpy as jnp
from jax import lax
from jax.experimental import pallas as pl
from jax.experimental.pallas import tpu as pltpu

F32 = jnp.float32
BF16 = jnp.bfloat16

N_META = 16
POOL_WINDOWS = (2, 4, 8, 16)
POOL_GROUP = 128
POOL_WIDTH = POOL_GROUP * len(POOL_WINDOWS)
QK_NOPE = 64
QK_ROPE = 32
V_DIM = 64
QK_DIM = QK_NOPE + QK_ROPE
Q_RANK = 256
KV_RANK = 128
HEAD_PAD = 128
SM_SCALE = QK_DIM ** -0.5
ROPE_THETA = 10000.0
NORM_EPS = 1e-6
MASK_VALUE = -1e30
Z_FIXED = POOL_WIDTH + Q_RANK + KV_RANK + HEAD_PAD

ADAM_LR = 0.001
ADAM_B1 = 0.9
ADAM_B2 = 0.999
ADAM_EPS = 1e-08
ADAM_WD = 0.01
ADAM_STEP = 10

N_CHIPS = 4
ATT_BLOCK = 256
SEQ_PAD = 128
ATT_Q_ROWS = 256
ATT_FWD_HEADS = 8
ATT_BWD_HEADS = 4
PACK_COLS = 1024
PACK_TILE = 512
VMEM_LIMIT = 60 * 1024 * 1024
MXU_DEPTH = 256
ACC_BYTES = 8 * 1024 * 1024

MESH = pl.DeviceIdType.MESH

BIG_WEIGHTS = ("w_in", "w_uq", "w_ukv", "w_pa", "w_pb", "w_o", "w_gate", "w_up", "w_down")
EARLY_WEIGHTS = ("w_in", "w_uq", "w_ukv")
LATE_WEIGHTS = ("w_pa", "w_pb", "w_o", "w_gate", "w_up", "w_down")
TRANSPOSED_WEIGHTS = ("w_in", "w_gate", "w_up")
SHARD_AXIS = {"w_in": 2, "w_uq": 2, "w_ukv": 2, "w_pa": 2, "w_pb": 1, "w_o": 1, "w_gate": 2, "w_up": 2, "w_down": 1}
SMALL_WEIGHTS = ("norm_mix_g", "pool_w", "pool_scale", "q_norm_g", "kv_norm_g", "norm_ffn_g", "final_norm_g")
WEIGHT_ORDER = ("meta_tokens", "norm_mix_g", "w_in", "pool_w", "pool_scale", "q_norm_g", "kv_norm_g", "w_uq", "w_ukv",
                "w_pa", "w_pb", "w_o", "norm_ffn_g", "w_gate", "w_up", "w_down", "final_norm_g")


def _round_up(n, m):
    return -(-n // m) * m


def _vmem_spec():
    return pl.BlockSpec(memory_space=pltpu.VMEM)


def _any_spec():
    return pl.BlockSpec(memory_space=pl.ANY)


def _row_block(tm, width, col_block=0):
    return pl.BlockSpec((tm, width), lambda i, cb=col_block: (i, cb))


def _params(sem, vmem=VMEM_LIMIT):
    return pltpu.CompilerParams(dimension_semantics=sem, vmem_limit_bytes=vmem)


def _token_tile(t, want):
    best = SEQ_PAD
    for tm in range(32, min(t, 2 * want) + 1, 32):
        if t % tm == 0 and abs(tm - want) < abs(best - want):
            best = tm
    return best


def _dot(a, b):
    return jnp.dot(a, b, preferred_element_type=F32)


def _dot_nt(a, b):
    return lax.dot_general(a, b, (((1,), (1,)), ((), ())), preferred_element_type=F32)


def _dot_tn(a, b):
    return lax.dot_general(a, b, (((0,), (0,)), ((), ())), preferred_element_type=F32)


def _rms_fwd(x, g):
    r = lax.rsqrt(jnp.mean(x * x, axis=-1, keepdims=True) + NORM_EPS)
    xh = x * r
    return xh * g, xh, r


def _rms_bwd(dy, xh, r, g):
    gdy = dy * g
    dx = r * (gdy - xh * jnp.mean(xh * gdy, axis=-1, keepdims=True))
    return dx, dy * xh


def _rope_fwd(x, c, sa, sb):
    return x * c + pltpu.roll(x, 16, 1) * sa + pltpu.roll(x, HEAD_PAD - 16, 1) * sb


def _rope_bwd(d, c, sa, sb):
    return d * c + pltpu.roll(d * sa, HEAD_PAD - 16, 1) + pltpu.roll(d * sb, 16, 1)


def _in_proj_fwd(h, g, w_in_pt):
    t, d = h.shape
    nz = w_in_pt.shape[0]
    tm = _token_tile(t, 512)

    def body(h_ref, g_ref, w_ref, z_ref):
        hn, _, _ = _rms_fwd(h_ref[...], g_ref[...])
        z_ref[...] = _dot_nt(hn.astype(BF16), w_ref[...]).astype(BF16)

    return pl.pallas_call(
        body, name="in_proj_fwd", grid=(t // tm,),
        in_specs=[_row_block(tm, d), _vmem_spec(), _vmem_spec()],
        out_specs=_row_block(tm, nz),
        out_shape=jax.ShapeDtypeStruct((t, nz), BF16),
        compiler_params=_params(("parallel",)),
    )(h, g, w_in_pt)


def _window_sum(x, w, row, forward):
    n = x.shape[0]
    s = x
    k = 1
    while k < w:
        if forward:
            s = s + jnp.where(row >= k, pltpu.roll(s, k, 0), 0.0)
        else:
            s = s + jnp.where(row < n - k, pltpu.roll(s, n - k, 0), 0.0)
        k *= 2
    return s


def _pool_fwd(z3, pool_w, pool_scale):
    b, lp, _ = z3.shape

    def body(u_ref, pw_ref, sc_ref, a_ref):
        row = lax.broadcasted_iota(jnp.int32, (lp, POOL_GROUP), 0)
        pos = row.astype(F32)
        for gi, w in enumerate(POOL_WINDOWS):
            cols = slice(gi * POOL_GROUP, (gi + 1) * POOL_GROUP)
            u = u_ref[0, :, cols].astype(F32)
            y = _window_sum(u, w, row, True) / jnp.minimum(pos + 1.0, float(w)) - u
            yw = _dot(y.astype(BF16), pw_ref[gi])
            a_ref[0, :, cols] = (yw * sc_ref[:, cols]).astype(BF16)

    return pl.pallas_call(
        body, name="pool_fwd", grid=(b,),
        in_specs=[pl.BlockSpec((1, lp, POOL_WIDTH), lambda i: (i, 0, 0)), _vmem_spec(), _vmem_spec()],
        out_specs=pl.BlockSpec((1, lp, POOL_WIDTH), lambda i: (i, 0, 0)),
        out_shape=jax.ShapeDtypeStruct((b, lp, POOL_WIDTH), BF16),
        compiler_params=_params(("parallel",)),
    )(z3, pool_w, pool_scale)


def _qkv_fwd(z, g_q, g_kv, w_uq_p, w_kv_p, rope_c, rope_sa, rope_sb):
    t = z.shape[0]
    hw = w_uq_p.shape[1]
    heads = hw // HEAD_PAD
    tm = _token_tile(t, 512)

    def body(cq_ref, ckv_ref, kr_ref, gq_ref, gkv_ref, wq_ref, wkv_ref, c_ref, sa_ref, sb_ref, q_ref, k_ref, v_ref):
        c, sa, sb = c_ref[...], sa_ref[...], sb_ref[...]
        cqn, _, _ = _rms_fwd(cq_ref[...].astype(F32), gq_ref[...])
        qraw = _dot(cqn.astype(BF16), wq_ref[...])
        ckvn, _, _ = _rms_fwd(ckv_ref[...].astype(F32), gkv_ref[...])
        kvraw = _dot(ckvn.astype(BF16), wkv_ref[...])
        kr = _rope_fwd(kr_ref[...].astype(F32), c, sa, sb)
        for hd in range(heads):
            cols = slice(hd * HEAD_PAD, (hd + 1) * HEAD_PAD)
            q_ref[:, cols] = (_rope_fwd(qraw[:, cols], c, sa, sb) * SM_SCALE).astype(BF16)
            k_ref[:, cols] = (kvraw[:, cols] + kr).astype(BF16)
        v_ref[...] = kvraw[:, hw:].astype(BF16)

    out = jax.ShapeDtypeStruct((t, hw), BF16)
    return pl.pallas_call(
        body, name="qkv_fwd", grid=(t // tm,),
        in_specs=[_row_block(tm, Q_RANK, POOL_WIDTH // Q_RANK),
                  _row_block(tm, KV_RANK, (POOL_WIDTH + Q_RANK) // KV_RANK),
                  _row_block(tm, HEAD_PAD, (POOL_WIDTH + Q_RANK + KV_RANK) // HEAD_PAD),
                  _vmem_spec(), _vmem_spec(), _vmem_spec(), _vmem_spec(),
                  _row_block(tm, HEAD_PAD), _row_block(tm, HEAD_PAD), _row_block(tm, HEAD_PAD)],
        out_specs=[_row_block(tm, hw)] * 3,
        out_shape=[out, out, out],
        compiler_params=_params(("parallel",)),
    )(z, z, z, g_q, g_kv, w_uq_p, w_kv_p, rope_c, rope_sa, rope_sb)


def _heads_per_step(heads, want):
    while heads % want:
        want //= 2
    return want


def _causal_mask(rows):
    row = lax.broadcasted_iota(jnp.int32, (rows, rows), 0)
    col = lax.broadcasted_iota(jnp.int32, (rows, rows), 1)
    return col <= row


def _attn_blocks(real_len):
    tail_start = (-(-real_len // ATT_BLOCK) - 1) * ATT_BLOCK
    return tail_start // ATT_BLOCK, tail_start, _round_up(real_len - tail_start, SEQ_PAD)


def _call_with_exchange(body, exchange, *, name, grid, in_specs, out_specs, out_shape, scratch_shapes, operands):
    if exchange is None:
        return pl.pallas_call(body, name=name, grid=grid, in_specs=in_specs, out_specs=out_specs, out_shape=out_shape,
                              scratch_shapes=scratch_shapes,
                              compiler_params=_params(("parallel",) + ("arbitrary",) * (len(grid) - 1)))(*operands)
    n_in, n_out, n_scratch = len(in_specs), len(out_specs), len(scratch_shapes)

    def riding(*refs):
        ins, src = refs[:n_in], refs[n_in]
        outs, dst = refs[n_in + 1:n_in + 1 + n_out], refs[n_in + 1 + n_out]
        scratch = refs[n_in + 2 + n_out:n_in + 2 + n_out + n_scratch]
        send_sems, recv_sems = refs[n_in + 2 + n_out + n_scratch:]
        steps = [pl.program_id(a) for a in range(len(grid))]

        @pl.when(functools.reduce(jnp.logical_and, [s == 0 for s in steps]))
        def _():
            exchange["start"](src, dst, send_sems, recv_sems)

        body(*ins, *outs, *scratch)

        @pl.when(functools.reduce(jnp.logical_and, [s == g - 1 for s, g in zip(steps, grid)]))
        def _():
            exchange["finish"](src, dst, send_sems, recv_sems)

    n = exchange["copies"]
    return pl.pallas_call(
        riding, name=name + "_" + exchange["name"], grid=grid,
        in_specs=list(in_specs) + [_any_spec()], out_specs=list(out_specs) + [_any_spec()],
        out_shape=list(out_shape) + [exchange["out_shape"]],
        scratch_shapes=list(scratch_shapes) + [pltpu.SemaphoreType.DMA((n,)), pltpu.SemaphoreType.DMA((n,))],
        compiler_params=_params(("arbitrary",) * len(grid)),
    )(*operands, exchange["operand"])


def _attn_fwd(q3, k3, v3, real_len, exchange=None):
    b, lp, hw = q3.shape
    heads = hw // HEAD_PAD
    tb = ATT_BLOCK
    nfull, tail_start, tail = _attn_blocks(real_len)
    done = tail_start + tail
    hpg = _heads_per_step(heads, ATT_FWD_HEADS)
    width = hpg * HEAD_PAD

    def body(q_ref, k_ref, v_ref, o_ref, lse_ref):
        group = pl.program_id(1)

        @pl.when(group == 0)
        def _():
            lse_ref[...] = jnp.zeros_like(lse_ref)

        def q_rows(r0, rows, whole_kv_blocks, back):
            def kv_step(c0, keys, states, mask):
                out = []
                for hd, (m, l, acc) in enumerate(states):
                    cols = slice(hd * HEAD_PAD, (hd + 1) * HEAD_PAD)
                    s = _dot_nt(q_ref[0, pl.ds(r0, rows), cols], k_ref[0, pl.ds(c0, keys), cols])
                    if mask is not None:
                        s = jnp.where(mask, s, MASK_VALUE)
                    m_new = jnp.maximum(m, jnp.max(s, axis=-1, keepdims=True))
                    alpha = jnp.exp(m - m_new)
                    p = jnp.exp(s - m_new)
                    l = alpha * l + jnp.sum(p, axis=-1, keepdims=True)
                    acc = alpha * acc + _dot(p.astype(BF16), v_ref[0, pl.ds(c0, keys), cols])
                    out.append((m_new, l, acc))
                return tuple(out)

            init = tuple((jnp.full((rows, 1), MASK_VALUE, F32), jnp.zeros((rows, 1), F32), jnp.zeros((rows, HEAD_PAD), F32))
                         for _ in range(hpg))
            states = lax.fori_loop(0, whole_kv_blocks, lambda j, st: kv_step(pl.multiple_of(j * tb, tb), tb, st, None), init)
            query = lax.broadcasted_iota(jnp.int32, (rows, back + rows), 0)
            key = lax.broadcasted_iota(jnp.int32, (rows, back + rows), 1)
            states = kv_step(pl.multiple_of(r0 - back, SEQ_PAD), back + rows, states, key <= query + back)
            lane = lax.broadcasted_iota(jnp.int32, (rows, HEAD_PAD), 1)
            lse_rows = lse_ref[0, pl.ds(r0, rows), :]
            for hd, (m, l, acc) in enumerate(states):
                o_ref[0, pl.ds(r0, rows), hd * HEAD_PAD:(hd + 1) * HEAD_PAD] = (acc / l).astype(BF16)
                lse_rows = jnp.where(lane == group * hpg + hd, m + jnp.log(l), lse_rows)
            lse_ref[0, pl.ds(r0, rows), :] = lse_rows

        def whole_block(i, carry):
            for back in range(0, tb, ATT_Q_ROWS):
                q_rows(pl.multiple_of(i * tb + back, ATT_Q_ROWS), ATT_Q_ROWS, i, back)
            return carry

        lax.fori_loop(0, nfull, whole_block, 0)
        for back in range(0, tail, ATT_Q_ROWS):
            q_rows(tail_start + back, min(ATT_Q_ROWS, tail - back), nfull, back)
        if done < lp:
            o_ref[0, done:lp, :] = jnp.zeros((lp - done, width), BF16)

    head_spec = pl.BlockSpec((1, lp, width), lambda bi, hi: (bi, 0, hi))
    return _call_with_exchange(
        body, exchange, name="attn_fwd", grid=(b, heads // hpg),
        in_specs=[head_spec, head_spec, head_spec],
        out_specs=[head_spec, pl.BlockSpec((1, lp, HEAD_PAD), lambda bi, hi: (bi, 0, 0))],
        out_shape=[jax.ShapeDtypeStruct((b, lp, hw), BF16), jax.ShapeDtypeStruct((b, lp, HEAD_PAD), F32)],
        scratch_shapes=[], operands=(q3, k3, v3))


def _merge_fwd(h, z, a, o, w_pa, w_pb_p, w_o):
    t, d = h.shape
    hw = o.shape[1]
    tm = _token_tile(t, 512)
    gate_block = Z_FIXED // d

    def body(h_ref, ga_ref, gb_ref, a_ref, o_ref, wpa_ref, wpb_ref, wo_ref, h1_ref, pa_ref, pb_ref):
        pa = _dot(a_ref[...], wpa_ref[...])
        pb = _dot(o_ref[...], wpb_ref[...])
        merged = jax.nn.sigmoid(ga_ref[...].astype(F32)) * pa + jax.nn.sigmoid(gb_ref[...].astype(F32)) * pb
        h1_ref[...] = h_ref[...] + _dot(merged.astype(BF16), wo_ref[...])
        pa_ref[...] = pa.astype(BF16)
        pb_ref[...] = pb.astype(BF16)

    return pl.pallas_call(
        body, name="merge_fwd", grid=(t // tm,),
        in_specs=[_row_block(tm, d), _row_block(tm, d, gate_block), _row_block(tm, d, gate_block + 1),
                  _row_block(tm, POOL_WIDTH), _row_block(tm, hw), _vmem_spec(), _vmem_spec(), _vmem_spec()],
        out_specs=[_row_block(tm, d)] * 3,
        out_shape=[jax.ShapeDtypeStruct((t, d), F32), jax.ShapeDtypeStruct((t, d), BF16), jax.ShapeDtypeStruct((t, d), BF16)],
        compiler_params=_params(("parallel",)),
    )(h, z, z, a, o, w_pa, w_pb_p, w_o)


def _ffn_fwd(h, g, w_gate_t, w_up_t, w_down):
    t, d = h.shape
    f = w_gate_t.shape[0]
    tm = _token_tile(t, 256)

    def body(h_ref, g_ref, wg_ref, wu_ref, wd_ref, h2_ref, a_ref, b_ref):
        x = h_ref[...]
        hn, _, _ = _rms_fwd(x, g_ref[...])
        hn = hn.astype(BF16)
        ga = _dot_nt(hn, wg_ref[...])
        up = _dot_nt(hn, wu_ref[...])
        act = ga * jax.nn.sigmoid(ga) * up
        h2_ref[...] = x + _dot(act.astype(BF16), wd_ref[...])
        a_ref[...] = ga.astype(BF16)
        b_ref[...] = up.astype(BF16)

    return pl.pallas_call(
        body, name="ffn_fwd", grid=(t // tm,),
        in_specs=[_row_block(tm, d), _vmem_spec(), _vmem_spec(), _vmem_spec(), _vmem_spec()],
        out_specs=[_row_block(tm, d), _row_block(tm, f), _row_block(tm, f)],
        out_shape=[jax.ShapeDtypeStruct((t, d), F32), jax.ShapeDtypeStruct((t, f), BF16), jax.ShapeDtypeStruct((t, f), BF16)],
        compiler_params=_params(("parallel",)),
    )(h, g, w_gate_t, w_up_t, w_down)


def _loss_head(h, g, target, valid):
    t, d = h.shape
    tm = _token_tile(t, 512)

    def body(h_ref, g_ref, t_ref, valid_ref, dh_ref, loss_ref, dg_ref):
        @pl.when(pl.program_id(0) == 0)
        def _():
            loss_ref[...] = jnp.zeros_like(loss_ref)
            dg_ref[...] = jnp.zeros_like(dg_ref)

        gain = g_ref[...]
        y, xh, r = _rms_fwd(h_ref[...], gain)
        err = (y - t_ref[...]) * valid_ref[...]
        per_row = jnp.sum(err * err, axis=-1, keepdims=True) / d
        loss_ref[...] += 0.5 * jnp.sum(per_row, axis=0, keepdims=True)
        dx, dg_rows = _rms_bwd(err / d, xh, r, gain)
        dh_ref[...] = dx
        dg_ref[...] += jnp.sum(dg_rows, axis=0, keepdims=True)

    return pl.pallas_call(
        body, name="loss_head", grid=(t // tm,),
        in_specs=[_row_block(tm, d), _vmem_spec(), _row_block(tm, d), _row_block(tm, 1)],
        out_specs=[_row_block(tm, d), pl.BlockSpec((1, 1), lambda i: (0, 0)), pl.BlockSpec((1, d), lambda i: (0, 0))],
        out_shape=[jax.ShapeDtypeStruct((t, d), F32), jax.ShapeDtypeStruct((1, 1), F32), jax.ShapeDtypeStruct((1, d), F32)],
        compiler_params=_params(("arbitrary",)),
    )(h, g, target, valid)


def _weight_grad(x, y, name):
    t, k = x.shape
    n = y.shape[1]
    tm = _token_tile(t, 8 * MXU_DEPTH)
    tk, tn = k, n
    while tk * tn * 4 > ACC_BYTES and max(tk, tn) % 256 == 0:
        if tk > tn:
            tk //= 2
        else:
            tn //= 2
    steps = t // tm

    def body(x_ref, y_ref, o_ref, acc):
        @pl.when(pl.program_id(2) == 0)
        def _():
            acc[...] = jnp.zeros_like(acc)

        acc[...] += _dot_tn(x_ref[...].astype(BF16), y_ref[...].astype(BF16))

        @pl.when(pl.program_id(2) == steps - 1)
        def _():
            o_ref[...] = acc[...].astype(BF16)

    return pl.pallas_call(
        body, name=name, grid=(k // tk, n // tn, steps),
        in_specs=[pl.BlockSpec((tm, tk), lambda a, b, i: (i, a)), pl.BlockSpec((tm, tn), lambda a, b, i: (i, b))],
        out_specs=pl.BlockSpec((tk, tn), lambda a, b, i: (a, b)),
        out_shape=jax.ShapeDtypeStruct((k, n), BF16),
        scratch_shapes=[pltpu.VMEM((tk, tn), F32)],
        compiler_params=_params(("parallel", "parallel", "arbitrary")),
    )(x, y)


def _ffn_bwd(h, dh2, a, b, g, w_gate_t, w_up_t, w_down):
    t, d = h.shape
    f = a.shape[1]
    tm = _token_tile(t, 256)

    def body(h_ref, dh2_ref, a_ref, b_ref, g_ref, wg_ref, wu_ref, wd_ref, dh_ref, hn_ref, act_ref, da_ref, db_ref, dg_ref):
        @pl.when(pl.program_id(0) == 0)
        def _():
            dg_ref[...] = jnp.zeros_like(dg_ref)

        gain = g_ref[...]
        hn, xh, r = _rms_fwd(h_ref[...], gain)
        hn_ref[...] = hn.astype(BF16)
        dh2 = dh2_ref[...]
        dact = _dot_nt(dh2.astype(BF16), wd_ref[...])
        ga = a_ref[...].astype(F32)
        up = b_ref[...].astype(F32)
        sg = jax.nn.sigmoid(ga)
        silu = ga * sg
        act_ref[...] = (silu * up).astype(BF16)
        da = (dact * up * (sg * (1.0 + ga * (1.0 - sg)))).astype(BF16)
        db = (dact * silu).astype(BF16)
        da_ref[...] = da
        db_ref[...] = db
        dhn = _dot(da, wg_ref[...]) + _dot(db, wu_ref[...])
        dx, dg_rows = _rms_bwd(dhn, xh, r, gain)
        dh_ref[...] = dh2 + dx
        dg_ref[...] += jnp.sum(dg_rows, axis=0, keepdims=True)

    return pl.pallas_call(
        body, name="ffn_bwd", grid=(t // tm,),
        in_specs=[_row_block(tm, d), _row_block(tm, d), _row_block(tm, f), _row_block(tm, f),
                  _vmem_spec(), _vmem_spec(), _vmem_spec(), _vmem_spec()],
        out_specs=[_row_block(tm, d), _row_block(tm, d), _row_block(tm, f), _row_block(tm, f), _row_block(tm, f),
                   pl.BlockSpec((1, d), lambda i: (0, 0))],
        out_shape=[jax.ShapeDtypeStruct((t, d), F32), jax.ShapeDtypeStruct((t, d), BF16), jax.ShapeDtypeStruct((t, f), BF16),
                   jax.ShapeDtypeStruct((t, f), BF16), jax.ShapeDtypeStruct((t, f), BF16), jax.ShapeDtypeStruct((1, d), F32)],
        compiler_params=_params(("arbitrary",)),
    )(h, dh2, a, b, g, w_gate_t, w_up_t, w_down)


def _merge_bwd(dh1, z, pa, pb, w_o, w_pa, w_pb_p):
    t, d = dh1.shape
    hw = w_pb_p.shape[0]
    tm = _token_tile(t, 512)
    gate_block = Z_FIXED // d

    def body(dh_ref, ga_ref, gb_ref, pa_ref, pb_ref, wo_ref, wpa_ref, wpb_ref,
             mg_ref, dpa_ref, dpb_ref, dga_ref, dgb_ref, da_ref, do_ref):
        dm = _dot_nt(dh_ref[...].astype(BF16), wo_ref[...])
        sa = jax.nn.sigmoid(ga_ref[...].astype(F32))
        sb = jax.nn.sigmoid(gb_ref[...].astype(F32))
        pa = pa_ref[...].astype(F32)
        pb = pb_ref[...].astype(F32)
        mg_ref[...] = (sa * pa + sb * pb).astype(BF16)
        dpa = (dm * sa).astype(BF16)
        dpb = (dm * sb).astype(BF16)
        dpa_ref[...] = dpa
        dpb_ref[...] = dpb
        dga_ref[...] = (dm * pa * (sa * (1.0 - sa))).astype(BF16)
        dgb_ref[...] = (dm * pb * (sb * (1.0 - sb))).astype(BF16)
        da_ref[...] = _dot_nt(dpa, wpa_ref[...]).astype(BF16)
        do_ref[...] = _dot_nt(dpb, wpb_ref[...]).astype(BF16)

    wide = jax.ShapeDtypeStruct((t, d), BF16)
    return pl.pallas_call(
        body, name="merge_bwd", grid=(t // tm,),
        in_specs=[_row_block(tm, d), _row_block(tm, d, gate_block), _row_block(tm, d, gate_block + 1),
                  _row_block(tm, d), _row_block(tm, d), _vmem_spec(), _vmem_spec(), _vmem_spec()],
        out_specs=[_row_block(tm, d)] * 5 + [_row_block(tm, POOL_WIDTH), _row_block(tm, hw)],
        out_shape=[wide] * 5 + [jax.ShapeDtypeStruct((t, POOL_WIDTH), BF16), jax.ShapeDtypeStruct((t, hw), BF16)],
        compiler_params=_params(("parallel",)),
    )(dh1, z, z, pa, pb, w_o, w_pa, w_pb_p)


def _attn_bwd(q3, k3, v3, o3, do3, lse3, real_len, exchange=None):
    b, lp, hw = q3.shape
    heads = hw // HEAD_PAD
    tb = ATT_BLOCK
    nfull, tail_start, tail = _attn_blocks(real_len)
    done = tail_start + tail
    hpg = _heads_per_step(heads, ATT_BWD_HEADS)
    width = hpg * HEAD_PAD

    def body(q_ref, k_ref, v_ref, o_ref, do_ref, lse_ref, dq_ref, dk_ref, dv_ref, dqt_acc, lse_row, delta_row):
        group = pl.program_id(1)
        lse_t = jnp.transpose(lse_ref[0])
        head_of_row = lax.broadcasted_iota(jnp.int32, (HEAD_PAD, lp), 0)
        for hd in range(hpg):
            cols = slice(hd * HEAD_PAD, (hd + 1) * HEAD_PAD)
            lse_row[hd] = jnp.sum(jnp.where(head_of_row == group * hpg + hd, lse_t, 0.0), axis=0, keepdims=True)
            prod = do_ref[0, :, cols].astype(F32) * o_ref[0, :, cols].astype(F32)
            delta_row[hd] = jnp.sum(jnp.transpose(prod), axis=0, keepdims=True)
        dqt_acc[...] = jnp.zeros_like(dqt_acc)

        def kv_rows(c0, keys, whole_q_blocks_from):
            k_t = [jnp.transpose(k_ref[0, pl.ds(c0, keys), hd * HEAD_PAD:(hd + 1) * HEAD_PAD].astype(F32)).astype(BF16)
                   for hd in range(hpg)]

            def q_step(r0, rows, states, mask):
                out = []
                for hd, (dk, dv) in enumerate(states):
                    cols = slice(hd * HEAD_PAD, (hd + 1) * HEAD_PAD)
                    q = q_ref[0, pl.ds(r0, rows), cols]
                    do = do_ref[0, pl.ds(r0, rows), cols]
                    s_t = _dot_nt(k_ref[0, pl.ds(c0, keys), cols], q)
                    if mask is not None:
                        s_t = jnp.where(mask, s_t, MASK_VALUE)
                    p_t = jnp.exp(s_t - lse_row[hd, :, pl.ds(r0, rows)])
                    dp_t = _dot_nt(v_ref[0, pl.ds(c0, keys), cols], do)
                    ds_t = (p_t * (dp_t - delta_row[hd, :, pl.ds(r0, rows)])).astype(BF16)
                    dv = dv + _dot(p_t.astype(BF16), do)
                    dk = dk + _dot(ds_t, q)
                    dqt_acc[cols, pl.ds(r0, rows)] += _dot(k_t[hd], ds_t)
                    out.append((dk, dv))
                return tuple(out)

            zero = jnp.zeros((keys, HEAD_PAD), F32)
            key_pos = lax.broadcasted_iota(jnp.int32, (keys, keys), 0)
            query_pos = lax.broadcasted_iota(jnp.int32, (keys, keys), 1)
            states = q_step(c0, keys, tuple((zero, zero) for _ in range(hpg)), key_pos <= query_pos)
            if whole_q_blocks_from is not None:
                states = lax.fori_loop(whole_q_blocks_from, nfull,
                                       lambda i, st: q_step(pl.multiple_of(i * tb, tb), tb, st, None), states)
                states = q_step(tail_start, tail, states, None)
            for hd, (dk, dv) in enumerate(states):
                cols = slice(hd * HEAD_PAD, (hd + 1) * HEAD_PAD)
                dk_ref[0, pl.ds(c0, keys), cols] = dk.astype(BF16)
                dv_ref[0, pl.ds(c0, keys), cols] = dv.astype(BF16)

        def whole_block(j, carry):
            kv_rows(pl.multiple_of(j * tb, tb), tb, j + 1)
            return carry

        lax.fori_loop(0, nfull, whole_block, 0)
        kv_rows(tail_start, tail, None)
        if done < lp:
            dk_ref[0, done:lp, :] = jnp.zeros((lp - done, width), BF16)
            dv_ref[0, done:lp, :] = jnp.zeros((lp - done, width), BF16)
        for hd in range(hpg):
            cols = slice(hd * HEAD_PAD, (hd + 1) * HEAD_PAD)
            dq_ref[0, :, cols] = jnp.transpose(dqt_acc[cols, :]).astype(BF16)

    head_spec = pl.BlockSpec((1, lp, width), lambda bi, hi: (bi, 0, hi))
    out = jax.ShapeDtypeStruct((b, lp, hw), BF16)
    return _call_with_exchange(
        body, exchange, name="attn_bwd", grid=(b, heads // hpg),
        in_specs=[head_spec] * 5 + [pl.BlockSpec((1, lp, HEAD_PAD), lambda bi, hi: (bi, 0, 0))],
        out_specs=[head_spec] * 3,
        out_shape=[out, out, out],
        scratch_shapes=[pltpu.VMEM((width, lp), F32), pltpu.VMEM((hpg, 1, lp), F32), pltpu.VMEM((hpg, 1, lp), F32)],
        operands=(q3, k3, v3, o3, do3, lse3))


def _qkv_bwd(dq, dk, dv, z, g_q, g_kv, w_uq_p, w_kv_p, rope_c, rope_sa, rope_sb):
    t, hw = dq.shape
    heads = hw // HEAD_PAD
    tm = _token_tile(t, 512)

    def body(dq_ref, dk_ref, dv_ref, cq_ref, ckv_ref, gq_ref, gkv_ref, wq_ref, wkv_ref, c_ref, sa_ref, sb_ref,
             dqraw_ref, dkvraw_ref, cqn_ref, ckvn_ref, dcq_ref, dckv_ref, dkr_ref, dgq_ref, dgkv_ref):
        @pl.when(pl.program_id(0) == 0)
        def _():
            dgq_ref[...] = jnp.zeros_like(dgq_ref)
            dgkv_ref[...] = jnp.zeros_like(dgkv_ref)

        c, sa, sb = c_ref[...], sa_ref[...], sb_ref[...]
        dkr = jnp.zeros((tm, HEAD_PAD), F32)
        for hd in range(heads):
            cols = slice(hd * HEAD_PAD, (hd + 1) * HEAD_PAD)
            dqraw_ref[:, cols] = _rope_bwd(dq_ref[:, cols].astype(F32) * SM_SCALE, c, sa, sb).astype(BF16)
            dkvraw_ref[:, cols] = dk_ref[:, cols]
            dkr = dkr + dk_ref[:, cols].astype(F32)
        dkvraw_ref[:, hw:] = dv_ref[...]
        lane = lax.broadcasted_iota(jnp.int32, (tm, HEAD_PAD), 1)
        dkr_ref[...] = jnp.where((lane >= QK_NOPE) & (lane < QK_DIM), _rope_bwd(dkr, c, sa, sb), 0.0).astype(BF16)

        gq = gq_ref[...]
        cqn, xh, r = _rms_fwd(cq_ref[...].astype(F32), gq)
        cqn_ref[...] = cqn.astype(BF16)
        dx, dg_rows = _rms_bwd(_dot_nt(dqraw_ref[...], wq_ref[...]), xh, r, gq)
        dcq_ref[...] = dx.astype(BF16)
        dgq_ref[...] += jnp.sum(dg_rows, axis=0, keepdims=True)

        gkv = gkv_ref[...]
        ckvn, xh, r = _rms_fwd(ckv_ref[...].astype(F32), gkv)
        ckvn_ref[...] = ckvn.astype(BF16)
        dx, dg_rows = _rms_bwd(_dot_nt(dkvraw_ref[...], wkv_ref[...]), xh, r, gkv)
        dckv_ref[...] = dx.astype(BF16)
        dgkv_ref[...] += jnp.sum(dg_rows, axis=0, keepdims=True)

    def shape(width, dtype=BF16):
        return jax.ShapeDtypeStruct((t, width), dtype)

    return pl.pallas_call(
        body, name="qkv_bwd", grid=(t // tm,),
        in_specs=[_row_block(tm, hw)] * 3
        + [_row_block(tm, Q_RANK, POOL_WIDTH // Q_RANK), _row_block(tm, KV_RANK, (POOL_WIDTH + Q_RANK) // KV_RANK)]
        + [_vmem_spec()] * 4 + [_row_block(tm, HEAD_PAD)] * 3,
        out_specs=[_row_block(tm, hw), _row_block(tm, 2 * hw), _row_block(tm, Q_RANK), _row_block(tm, KV_RANK),
                   _row_block(tm, Q_RANK), _row_block(tm, KV_RANK), _row_block(tm, HEAD_PAD),
                   pl.BlockSpec((1, Q_RANK), lambda i: (0, 0)), pl.BlockSpec((1, KV_RANK), lambda i: (0, 0))],
        out_shape=[shape(hw), shape(2 * hw), shape(Q_RANK), shape(KV_RANK), shape(Q_RANK), shape(KV_RANK), shape(HEAD_PAD),
                   jax.ShapeDtypeStruct((1, Q_RANK), F32), jax.ShapeDtypeStruct((1, KV_RANK), F32)],
        compiler_params=_params(("arbitrary",)),
    )(dq, dk, dv, z, z, g_q, g_kv, w_uq_p, w_kv_p, rope_c, rope_sa, rope_sb)


def _pool_bwd(z3, da3, pool_w, pool_scale):
    b, lp, _ = z3.shape
    groups = len(POOL_WINDOWS)

    def body(u_ref, da_ref, pw_ref, sc_ref, du_ref, dpw_ref, dsc_ref):
        @pl.when(pl.program_id(0) == 0)
        def _():
            dpw_ref[...] = jnp.zeros_like(dpw_ref)
            dsc_ref[...] = jnp.zeros_like(dsc_ref)

        row = lax.broadcasted_iota(jnp.int32, (lp, POOL_GROUP), 0)
        pos = row.astype(F32)
        for gi, w in enumerate(POOL_WINDOWS):
            cols = slice(gi * POOL_GROUP, (gi + 1) * POOL_GROUP)
            count = jnp.minimum(pos + 1.0, float(w))
            u = u_ref[0, :, cols].astype(F32)
            y = (_window_sum(u, w, row, True) / count - u).astype(BF16)
            yw = _dot(y, pw_ref[gi])
            da = da_ref[0, :, cols].astype(F32)
            dsc_ref[:, cols] += jnp.sum(da * yw, axis=0, keepdims=True)
            dyw = (da * sc_ref[:, cols]).astype(BF16)
            dpw_ref[gi] += _dot_tn(y, dyw)
            dy = _dot_nt(dyw, pw_ref[gi])
            du_ref[0, :, cols] = (_window_sum(dy / count, w, row, False) - dy).astype(BF16)

    return pl.pallas_call(
        body, name="pool_bwd", grid=(b,),
        in_specs=[pl.BlockSpec((1, lp, POOL_WIDTH), lambda i: (i, 0, 0)), pl.BlockSpec((1, lp, POOL_WIDTH), lambda i: (i, 0, 0)),
                  _vmem_spec(), _vmem_spec()],
        out_specs=[pl.BlockSpec((1, lp, POOL_WIDTH), lambda i: (i, 0, 0)),
                   pl.BlockSpec((groups, POOL_GROUP, POOL_GROUP), lambda i: (0, 0, 0)),
                   pl.BlockSpec((1, POOL_WIDTH), lambda i: (0, 0))],
        out_shape=[jax.ShapeDtypeStruct((b, lp, POOL_WIDTH), BF16), jax.ShapeDtypeStruct((groups, POOL_GROUP, POOL_GROUP), F32),
                   jax.ShapeDtypeStruct((1, POOL_WIDTH), F32)],
        compiler_params=_params(("arbitrary",)),
    )(z3, da3, pool_w, pool_scale)


def _in_proj_bwd(h, dh1, du, dcq, dckv, dkr, dga, dgb, g, w_in_pt):
    t, d = h.shape
    nz = w_in_pt.shape[0]
    tm = _token_tile(t, 512)
    widths = (POOL_WIDTH, Q_RANK, KV_RANK, HEAD_PAD, d, d)

    def body(h_ref, dh1_ref, du_ref, dcq_ref, dckv_ref, dkr_ref, dga_ref, dgb_ref, g_ref, w_ref, dh_ref, hn_ref, dz_ref, dg_ref):
        @pl.when(pl.program_id(0) == 0)
        def _():
            dg_ref[...] = jnp.zeros_like(dg_ref)

        gain = g_ref[...]
        hn, xh, r = _rms_fwd(h_ref[...], gain)
        hn_ref[...] = hn.astype(BF16)
        dhn = jnp.zeros((tm, d), F32)
        start = 0
        for piece, width in zip((du_ref, dcq_ref, dckv_ref, dkr_ref, dga_ref, dgb_ref), widths):
            val = piece[...]
            dz_ref[:, start:start + width] = val
            dhn = dhn + _dot(val, w_ref[start:start + width, :])
            start += width
        dx, dg_rows = _rms_bwd(dhn, xh, r, gain)
        dh_ref[...] = dh1_ref[...] + dx
        dg_ref[...] += jnp.sum(dg_rows, axis=0, keepdims=True)

    return pl.pallas_call(
        body, name="in_proj_bwd", grid=(t // tm,),
        in_specs=[_row_block(tm, d), _row_block(tm, d)] + [_row_block(tm, w) for w in widths] + [_vmem_spec(), _vmem_spec()],
        out_specs=[_row_block(tm, d), _row_block(tm, d), _row_block(tm, nz), pl.BlockSpec((1, d), lambda i: (0, 0))],
        out_shape=[jax.ShapeDtypeStruct((t, d), F32), jax.ShapeDtypeStruct((t, d), BF16), jax.ShapeDtypeStruct((t, nz), BF16),
                   jax.ShapeDtypeStruct((1, d), F32)],
        compiler_params=_params(("arbitrary",)),
    )(h, dh1, du, dcq, dckv, dkr, dga, dgb, g, w_in_pt)


def _pad_heads(w, heads, width):
    k = w.shape[0]
    w = w.reshape(k, heads, width)
    return jnp.pad(w, ((0, 0), (0, 0), (0, HEAD_PAD - width))).reshape(k, heads * HEAD_PAD)


def _unpad_heads(w, heads, width):
    k = w.shape[0]
    return w.reshape(k, heads, HEAD_PAD)[:, :, :width].reshape(k, heads * width)


def _early_layouts(w, heads):
    o3, o4 = POOL_WIDTH + Q_RANK + KV_RANK, POOL_WIDTH + Q_RANK + KV_RANK + QK_ROPE
    w_in_t = w["w_in"]
    rope_rows = jnp.pad(w_in_t[o3:o4], ((QK_NOPE, HEAD_PAD - QK_DIM), (0, 0)))
    w_in_pt = jnp.concatenate([w_in_t[:o3], rope_rows, w_in_t[o4:]], axis=0)
    w_uq_p = _pad_heads(w["w_uq"], heads, QK_DIM)
    kv = w["w_ukv"].reshape(KV_RANK, heads, QK_NOPE + V_DIM)
    w_k = jnp.pad(kv[:, :, :QK_NOPE], ((0, 0), (0, 0), (0, HEAD_PAD - QK_NOPE))).reshape(KV_RANK, heads * HEAD_PAD)
    w_v = jnp.pad(kv[:, :, QK_NOPE:], ((0, 0), (0, 0), (0, HEAD_PAD - V_DIM))).reshape(KV_RANK, heads * HEAD_PAD)
    return dict(w_in_pt=w_in_pt, w_uq_p=w_uq_p, w_kv_p=jnp.concatenate([w_k, w_v], axis=1))


def _late_layouts(w, heads):
    d = w["w_pb"].shape[1]
    w_pb_p = jnp.pad(w["w_pb"].reshape(heads, V_DIM, d), ((0, 0), (0, HEAD_PAD - V_DIM), (0, 0))).reshape(heads * HEAD_PAD, d)
    return dict(w_pa=w["w_pa"], w_pb_p=w_pb_p, w_o=w["w_o"], w_gate_t=w["w_gate"], w_up_t=w["w_up"], w_down=w["w_down"])


def _early_grad_layouts(g, heads):
    o3 = POOL_WIDTH + Q_RANK + KV_RANK
    gin = g["w_in_pt"]
    w_in = jnp.concatenate([gin[:o3], gin[o3 + QK_NOPE:o3 + QK_DIM], gin[o3 + HEAD_PAD:]], axis=0)
    hw = heads * HEAD_PAD
    gk = g["w_kv_p"][:, :hw].reshape(KV_RANK, heads, HEAD_PAD)[:, :, :QK_NOPE]
    gv = g["w_kv_p"][:, hw:].reshape(KV_RANK, heads, HEAD_PAD)[:, :, :V_DIM]
    w_ukv = jnp.concatenate([gk, gv], axis=2).reshape(KV_RANK, heads * (QK_NOPE + V_DIM))
    return dict(w_in=w_in, w_uq=_unpad_heads(g["w_uq_p"], heads, QK_DIM), w_ukv=w_ukv)


def _late_grad_layouts(g, heads):
    d = g["w_pb_p"].shape[1]
    w_pb = g["w_pb_p"].reshape(heads, HEAD_PAD, d)[:, :V_DIM].reshape(heads * V_DIM, d)
    return dict(w_pa=g["w_pa"], w_pb=w_pb, w_o=g["w_o"], w_gate=g["w_gate_t"], w_up=g["w_up_t"], w_down=g["w_down"])


def _rope_tables(lp, b):
    inv = 1.0 / (ROPE_THETA ** (jnp.arange(0, QK_ROPE, 2, dtype=F32) / QK_ROPE))
    ang = jnp.arange(lp, dtype=F32)[:, None] * inv[None, :]
    cos, sin = jnp.cos(ang), jnp.sin(ang)
    half = QK_ROPE // 2
    ones = jnp.ones((lp, QK_NOPE), F32)
    zeros_lo = jnp.zeros((lp, QK_NOPE), F32)
    zeros_hi = jnp.zeros((lp, HEAD_PAD - QK_DIM), F32)
    zeros_half = jnp.zeros((lp, half), F32)
    c = jnp.concatenate([ones, cos, cos, zeros_hi], axis=1)
    sa = jnp.concatenate([zeros_lo, zeros_half, sin, zeros_hi], axis=1)
    sb = jnp.concatenate([zeros_lo, -sin, zeros_half, zeros_hi], axis=1)
    return tuple(jnp.tile(tab, (b, 1)) for tab in (c, sa, sb))


def _local_step(x, loss_target, meta_tokens, small, early_first, riding, early_first_codec):
    b, seq, d = x.shape
    depth = 2
    heads = early_first["w_uq"].shape[1] // QK_DIM
    core = lax.axis_index("c")
    chip = 2 * lax.axis_index("x") + lax.axis_index("y")
    real_len = N_META + seq
    lp = _round_up(real_len, SEQ_PAD)
    t = b * lp
    pad = lp - N_META - seq

    meta = jnp.broadcast_to(meta_tokens[None], (b, N_META, d))
    h = jnp.concatenate([meta, x, jnp.zeros((b, pad, d), F32)], axis=1).reshape(t, d)
    target = jnp.pad(loss_target, ((0, 0), (N_META, pad), (0, 0))).reshape(t, d)
    pos = jnp.arange(lp)
    valid = jnp.tile(((pos >= N_META) & (pos < N_META + seq)).astype(F32), b).reshape(t, 1)
    rope_c, rope_sa, rope_sb = _rope_tables(lp, b)

    layers = []
    for li in range(depth):
        lay = dict(pool_w=small["pool_w"][li].astype(BF16), pool_scale=small["pool_scale"][li][None])
        for n in ("norm_mix_g", "q_norm_g", "kv_norm_g", "norm_ffn_g"):
            lay[n] = small[n][li][None]
        layers.append(lay)
    layers[0].update(_early_layouts(early_first, heads))

    saved = []
    for li in range(depth):
        lay = layers[li]
        z = _in_proj_fwd(h, lay["norm_mix_g"], lay["w_in_pt"])
        a = _pool_fwd(z.reshape(b, lp, -1), lay["pool_w"], lay["pool_scale"]).reshape(t, POOL_WIDTH)
        q, k, v = _qkv_fwd(z, lay["q_norm_g"], lay["kv_norm_g"], lay["w_uq_p"], lay["w_kv_p"], rope_c, rope_sa, rope_sb)
        hw = q.shape[1]
        packed = riding[li]["packed"]
        o3, lse, others = _attn_fwd(q.reshape(b, lp, hw), k.reshape(b, lp, hw), v.reshape(b, lp, hw), real_len,
                                    _gather_exchange(packed))
        arrived = riding[li]["unpack"](lax.dynamic_update_index_in_dim(others, packed, chip, 0))
        lay.update(_late_layouts({n: arrived[n, li] for n in LATE_WEIGHTS}, heads))
        if li + 1 < depth:
            layers[li + 1].update(_early_layouts({n: arrived[n, li + 1] for n in EARLY_WEIGHTS}, heads))
        o = o3.reshape(t, hw)
        h1, pa, pb = _merge_fwd(h, z, a, o, lay["w_pa"], lay["w_pb_p"], lay["w_o"])
        h2, fa, fb = _ffn_fwd(h1, lay["norm_ffn_g"], lay["w_gate_t"], lay["w_up_t"], lay["w_down"])
        saved.append(dict(h=h, z=z, a=a, q=q, k=k, v=v, o=o, lse=lse, pa=pa, pb=pb, h1=h1, fa=fa, fb=fb))
        h = h2

    dh, loss, d_final = _loss_head(h, small["final_norm_g"][None], target, valid)

    g_small = {n: [] for n in SMALL_WEIGHTS if n != "final_norm_g"}
    early_grads, parts = {}, [None] * depth
    for li in reversed(range(depth)):
        lay, sv = layers[li], saved[li]
        hw = sv["q"].shape[1]
        dh1, hn_f, act, dfa, dfb, dg_ffn = _ffn_bwd(sv["h1"], dh, sv["fa"], sv["fb"], lay["norm_ffn_g"],
                                                     lay["w_gate_t"], lay["w_up_t"], lay["w_down"])
        gl = dict(w_gate_t=_weight_grad(dfa, hn_f, "grad_w_gate"), w_up_t=_weight_grad(dfb, hn_f, "grad_w_up"),
                  w_down=_weight_grad(act, dh, "grad_w_down"))
        merged, dpa, dpb, dga, dgb, da, do = _merge_bwd(dh1, sv["z"], sv["pa"], sv["pb"], lay["w_o"], lay["w_pa"], lay["w_pb_p"])
        gl["w_o"] = _weight_grad(merged, dh1, "grad_w_o")
        gl["w_pa"] = _weight_grad(sv["a"], dpa, "grad_w_pa")
        gl["w_pb_p"] = _weight_grad(sv["o"], dpb, "grad_w_pb")
        to_send = {(n, li): g for n, g in _late_grad_layouts(gl, heads).items()}
        if li + 1 < depth:
            to_send.update({(n, li + 1): g for n, g in early_grads[li + 1].items()})
        sending = riding[li]["pack_grads"](to_send)
        shape3 = (b, lp, hw)
        dq3, dk3, dv3, from_others = _attn_bwd(
            sv["q"].reshape(shape3), sv["k"].reshape(shape3), sv["v"].reshape(shape3), sv["o"].reshape(shape3),
            do.reshape(shape3), sv["lse"], real_len, _scatter_exchange(sending))
        own = lax.dynamic_index_in_dim(lax.dynamic_index_in_dim(sending, chip, 0, keepdims=False), core, 0, keepdims=False)
        parts[li] = lax.dynamic_update_index_in_dim(from_others, own, 2 * chip + core, 0)
        dqraw, dkvraw, cqn, ckvn, dcq, dckv, dkr, dg_q, dg_kv = _qkv_bwd(
            dq3.reshape(t, hw), dk3.reshape(t, hw), dv3.reshape(t, hw), sv["z"], lay["q_norm_g"], lay["kv_norm_g"],
            lay["w_uq_p"], lay["w_kv_p"], rope_c, rope_sa, rope_sb)
        gl["w_uq_p"] = _weight_grad(cqn, dqraw, "grad_w_uq")
        gl["w_kv_p"] = _weight_grad(ckvn, dkvraw, "grad_w_ukv")
        du3, dpool_w, dpool_scale = _pool_bwd(sv["z"].reshape(b, lp, -1), da.reshape(b, lp, POOL_WIDTH),
                                              lay["pool_w"], lay["pool_scale"])
        dh, hn_m, dz, dg_mix = _in_proj_bwd(sv["h"], dh1, du3.reshape(t, POOL_WIDTH), dcq, dckv, dkr, dga, dgb,
                                            lay["norm_mix_g"], lay["w_in_pt"])
        gl["w_in_pt"] = _weight_grad(dz, hn_m, "grad_w_in")
        early_grads[li] = _early_grad_layouts(gl, heads)
        for n, val in (("norm_mix_g", dg_mix[0]), ("pool_w", dpool_w), ("pool_scale", dpool_scale[0]), ("q_norm_g", dg_q[0]),
                       ("kv_norm_g", dg_kv[0]), ("norm_ffn_g", dg_ffn[0])):
            g_small[n].insert(0, val)

    dh3 = dh.reshape(b, lp, d)
    grad_x = dh3[:, N_META:N_META + seq]
    d_meta_rows = dh3[:, :N_META]
    g_small = {n: jnp.stack(v) for n, v in g_small.items()}
    g_small["final_norm_g"] = d_final[0]
    early_first_partial = early_first_codec["pack_grads"]({(n, 0): g for n, g in early_grads[0].items()})
    return loss, grad_x, d_meta_rows, g_small, early_first_partial, parts


def _mesh_place():
    x, y, c = lax.axis_index("x"), lax.axis_index("y"), lax.axis_index("c")
    others = [(1 - x, y), (x, 1 - y), (1 - x, 1 - y)]
    return x, y, c, 2 * x + y, others


def _remote(src, dst, send_sems, recv_sems, k, device):
    return pltpu.make_async_remote_copy(src_ref=src, dst_ref=dst, send_sem=send_sems.at[k], recv_sem=recv_sems.at[k],
                                        device_id=device, device_id_type=MESH)


def _gather_exchange(packed):
    def copy(p_ref, g_ref, send_sems, recv_sems, r, slot):
        _, _, c, _, others = _mesh_place()
        ox, oy = others[r]
        return _remote(p_ref, g_ref.at[slot], send_sems, recv_sems, r, (ox, oy, c))

    def start(p_ref, g_ref, send_sems, recv_sems):
        chip = _mesh_place()[3]
        for r in range(3):
            copy(p_ref, g_ref, send_sems, recv_sems, r, chip).start()

    def finish(p_ref, g_ref, send_sems, recv_sems):
        _, _, _, chip, others = _mesh_place()
        for r, (ox, oy) in enumerate(others):
            copy(p_ref, g_ref, send_sems, recv_sems, r, 2 * ox + oy).wait_recv()
        for r in range(3):
            copy(p_ref, g_ref, send_sems, recv_sems, r, chip).wait_send()

    return dict(name="gather", operand=packed, copies=3, start=start, finish=finish,
                out_shape=jax.ShapeDtypeStruct((N_CHIPS,) + packed.shape, packed.dtype))


def _scatter_exchange(parts):
    flips = [(dx, dy, dc) for dx in (0, 1) for dy in (0, 1) for dc in (0, 1)][1:]

    def copy(p_ref, got_ref, send_sems, recv_sems, k, arriving):
        x, y, c, _, _ = _mesh_place()
        dx, dy, dc = flips[k]
        tx, ty, tc = (1 - x if dx else x), (1 - y if dy else y), (1 - c if dc else c)
        slot = 4 * tx + 2 * ty + tc if arriving else 4 * x + 2 * y + c
        return _remote(p_ref.at[2 * tx + ty, tc], got_ref.at[slot], send_sems, recv_sems, k, (tx, ty, tc))

    def start(p_ref, got_ref, send_sems, recv_sems):
        for k in range(len(flips)):
            copy(p_ref, got_ref, send_sems, recv_sems, k, False).start()

    def finish(p_ref, got_ref, send_sems, recv_sems):
        for k in range(len(flips)):
            copy(p_ref, got_ref, send_sems, recv_sems, k, True).wait_recv()
        for k in range(len(flips)):
            copy(p_ref, got_ref, send_sems, recv_sems, k, False).wait_send()

    return dict(name="scatter", operand=parts, copies=len(flips), start=start, finish=finish,
                out_shape=jax.ShapeDtypeStruct((2 * N_CHIPS,) + parts.shape[2:], parts.dtype))


def _all_gather_shards(packed, meta_shard):
    _, rh, cols = packed.shape

    def body(p_ref, m_ref, g_ref, gm_ref, send_sems, recv_sems):
        x, y, c, chip, others = _mesh_place()
        sibling = (x, y, 1 - c)
        sends = []
        for r, (ox, oy) in enumerate(others):
            sends.append(_remote(p_ref.at[c], g_ref.at[chip, c], send_sems, recv_sems, r, (ox, oy, c)))
            sends.append(_remote(m_ref, gm_ref.at[chip], send_sems, recv_sems, 6 + r, (ox, oy, c)))
        for cp in sends:
            cp.start()
        for r, (ox, oy) in enumerate(others):
            src_chip = 2 * ox + oy
            _remote(p_ref.at[c], g_ref.at[src_chip, c], send_sems, recv_sems, r, (ox, oy, c)).wait_recv()
            passed = _remote(g_ref.at[src_chip, c], g_ref.at[src_chip, c], send_sems, recv_sems, 3 + r, sibling)
            passed.start()
            sends.append(passed)
        for r, (ox, oy) in enumerate(others):
            src_chip = 2 * ox + oy
            _remote(p_ref.at[c], g_ref.at[src_chip, 1 - c], send_sems, recv_sems, 3 + r, sibling).wait_recv()
            _remote(m_ref, gm_ref.at[src_chip], send_sems, recv_sems, 6 + r, (ox, oy, c)).wait_recv()
        for cp in sends:
            cp.wait_send()

    gathered, meta_all = pl.pallas_call(
        body, name="all_gather_shards",
        in_specs=[_any_spec(), _any_spec()], out_specs=[_any_spec(), _any_spec()],
        out_shape=[jax.ShapeDtypeStruct((N_CHIPS, 2, rh, cols), packed.dtype),
                   jax.ShapeDtypeStruct((N_CHIPS,) + meta_shard.shape, meta_shard.dtype)],
        scratch_shapes=[pltpu.SemaphoreType.DMA((9,)), pltpu.SemaphoreType.DMA((9,))],
    )(packed, meta_shard)
    chip = 2 * lax.axis_index("x") + lax.axis_index("y")
    return (lax.dynamic_update_index_in_dim(gathered, packed, chip, 0),
            lax.dynamic_update_index_in_dim(meta_all, meta_shard, chip, 0))


def _pair_exchange(give):
    def body(give_ref, got_ref, send_sems, recv_sems):
        x, y, c, _, _ = _mesh_place()
        cp = _remote(give_ref, got_ref, send_sems, recv_sems, 0, (x, y, 1 - c))
        cp.start()
        cp.wait()

    return pl.pallas_call(
        body, name="pair_exchange", in_specs=[_any_spec()], out_specs=_any_spec(),
        out_shape=jax.ShapeDtypeStruct(give.shape, give.dtype),
        scratch_shapes=[pltpu.SemaphoreType.DMA((1,)), pltpu.SemaphoreType.DMA((1,))],
    )(give)


def _chip_exchange(parts):
    def body(p_ref, got_ref, send_sems, recv_sems):
        _, _, c, chip, others = _mesh_place()
        sends = [_remote(p_ref.at[2 * ox + oy], got_ref.at[chip], send_sems, recv_sems, r, (ox, oy, c))
                 for r, (ox, oy) in enumerate(others)]
        for cp in sends:
            cp.start()
        for r, (ox, oy) in enumerate(others):
            _remote(p_ref.at[chip], got_ref.at[2 * ox + oy], send_sems, recv_sems, r, (ox, oy, c)).wait_recv()
        for cp in sends:
            cp.wait_send()

    got = pl.pallas_call(
        body, name="chip_exchange", in_specs=[_any_spec()], out_specs=_any_spec(),
        out_shape=jax.ShapeDtypeStruct(parts.shape, parts.dtype),
        scratch_shapes=[pltpu.SemaphoreType.DMA((3,)), pltpu.SemaphoreType.DMA((3,))],
    )(parts)
    chip = 2 * lax.axis_index("x") + lax.axis_index("y")
    own = lax.dynamic_index_in_dim(parts, chip, 0, keepdims=False)
    return lax.dynamic_update_index_in_dim(got, own, chip, 0)


def _pair_gather(half):
    def body(h_ref, out_ref, send_sems, recv_sems):
        x, y, c, _, _ = _mesh_place()
        cp = _remote(h_ref, out_ref.at[c], send_sems, recv_sems, 0, (x, y, 1 - c))
        cp.start()
        _remote(h_ref, out_ref.at[1 - c], send_sems, recv_sems, 0, (x, y, 1 - c)).wait_recv()
        cp.wait_send()

    both = pl.pallas_call(
        body, name="pair_gather", in_specs=[_any_spec()], out_specs=_any_spec(),
        out_shape=jax.ShapeDtypeStruct((2,) + half.shape, half.dtype),
        scratch_shapes=[pltpu.SemaphoreType.DMA((1,)), pltpu.SemaphoreType.DMA((1,))],
    )(half)
    return lax.dynamic_update_index_in_dim(both, half, lax.axis_index("c"), 0)


def _row_tile(rows, limit=PACK_TILE):
    if rows <= limit:
        return rows
    for tr in range(limit, 7, -8):
        if rows % tr == 0:
            return tr
    return rows


def _pair_add(keep, got):
    n, rh, cols = keep.shape
    tr = _row_tile(rh)

    def body(k_ref, g_ref, o_ref):
        o_ref[...] = (k_ref[...].astype(F32) + g_ref[...].astype(F32)).astype(BF16)

    spec = pl.BlockSpec((1, tr, cols), lambda j, i: (j, i, 0))
    return pl.pallas_call(
        body, name="pair_add", grid=(n, rh // tr), in_specs=[spec, spec], out_specs=spec,
        out_shape=jax.ShapeDtypeStruct(keep.shape, BF16),
        compiler_params=_params(("parallel", "parallel")),
    )(keep, got)


def _chip_sum(parts):
    n, rh, cols = parts.shape
    tr = _row_tile(rh)

    def body(p_ref, o_ref):
        total = p_ref[0].astype(F32)
        for k in range(1, n):
            total = total + p_ref[k].astype(F32)
        o_ref[...] = total

    return pl.pallas_call(
        body, name="chip_sum", grid=(rh // tr,),
        in_specs=[pl.BlockSpec((n, tr, cols), lambda i: (0, i, 0))], out_specs=pl.BlockSpec((tr, cols), lambda i: (i, 0)),
        out_shape=jax.ShapeDtypeStruct((rh, cols), F32),
        compiler_params=_params(("parallel",)),
    )(parts)


def _reduce_scatter(grads, c):
    keep = lax.dynamic_index_in_dim(grads, c, axis=1, keepdims=False)
    give = lax.dynamic_index_in_dim(grads, 1 - c, axis=1, keepdims=False)
    chip_partial = _pair_add(keep, _pair_exchange(give))
    return _pair_gather(_chip_sum(_chip_exchange(chip_partial)))


def _all_reduce_small(meta_rows, small):
    b, rm, cols = meta_rows.shape
    rows = rm + small.shape[0]

    def body(meta_ref, small_ref, out_ref, mine, pair_buf, chip_buf, send_sems, recv_sems):
        x, y, c, chip, others = _mesh_place()
        acc = meta_ref[0]
        for i in range(1, b):
            acc = acc + meta_ref[i]
        mine[0:rm, :] = acc
        mine[rm:rows, :] = small_ref[...]
        pair = _remote(mine, pair_buf, send_sems, recv_sems, 0, (x, y, 1 - c))
        pair.start()
        pair.wait()
        chip_buf[chip] = mine[...] + pair_buf[...]
        sends = [_remote(chip_buf.at[chip], chip_buf.at[chip], send_sems, recv_sems, 1 + r, (ox, oy, c))
                 for r, (ox, oy) in enumerate(others)]
        for cp in sends:
            cp.start()
        for r, (ox, oy) in enumerate(others):
            _remote(chip_buf.at[chip], chip_buf.at[2 * ox + oy], send_sems, recv_sems, 1 + r, (ox, oy, c)).wait_recv()
        for cp in sends:
            cp.wait_send()
        out_ref[...] = ((chip_buf[0] + chip_buf[1]) + chip_buf[2]) + chip_buf[3]

    return pl.pallas_call(
        body, name="all_reduce_small",
        in_specs=[_vmem_spec(), _vmem_spec()], out_specs=_vmem_spec(),
        out_shape=jax.ShapeDtypeStruct((rows, cols), F32),
        scratch_shapes=[pltpu.VMEM((rows, cols), F32), pltpu.VMEM((rows, cols), F32), pltpu.VMEM((N_CHIPS, rows, cols), F32),
                        pltpu.SemaphoreType.DMA((4,)), pltpu.SemaphoreType.DMA((4,))],
        compiler_params=pltpu.CompilerParams(vmem_limit_bytes=VMEM_LIMIT),
    )(meta_rows, small)


def _adamw(w, g, m, v):
    shape = w.shape
    cols = shape[-1]
    rows = w.size // cols
    tr = _row_tile(rows)

    def body(w_ref, g_ref, m_ref, v_ref, d_ref, m2_ref, v2_ref):
        grad = g_ref[...]
        m2 = ADAM_B1 * m_ref[...] + (1.0 - ADAM_B1) * grad
        v2 = ADAM_B2 * v_ref[...] + (1.0 - ADAM_B2) * jnp.square(grad)
        m_hat = m2 / (1.0 - ADAM_B1 ** ADAM_STEP)
        v_hat = v2 / (1.0 - ADAM_B2 ** ADAM_STEP)
        d_ref[...] = -ADAM_LR * (m_hat / (jnp.sqrt(v_hat) + ADAM_EPS) + ADAM_WD * w_ref[...])
        m2_ref[...] = m2
        v2_ref[...] = v2

    spec = pl.BlockSpec((tr, cols), lambda i: (i, 0))
    out = jax.ShapeDtypeStruct((rows, cols), F32)
    res = pl.pallas_call(
        body, name="adamw", grid=(rows // tr,), in_specs=[spec] * 4, out_specs=[spec] * 3, out_shape=[out] * 3,
        compiler_params=_params(("parallel",)),
    )(*(a.reshape(rows, cols) for a in (w, g, m, v)))
    return tuple(r.reshape(shape) for r in res)


def _pack_rows(arrays):
    flat = [a.reshape(-1, PACK_COLS) for a in arrays]
    counts = [f.shape[0] for f in flat]
    total = sum(counts)
    half = -(-total // 2)
    tiles = -(-half // PACK_TILE)
    padded = 2 * tiles * _round_up(-(-half // tiles), 16)
    if padded > total:
        flat.append(jnp.zeros((padded - total, PACK_COLS), flat[0].dtype))
    return jnp.concatenate(flat, axis=0), counts


def _unpack_rows(buffer, counts, shapes):
    out, start = [], 0
    for n, shape in zip(counts, shapes):
        out.append(buffer[..., start:start + n, :].reshape(buffer.shape[:-2] + tuple(shape)))
        start += n
    return out


def _group_codec(entries, weights):
    turned = [n in TRANSPOSED_WEIGHTS for n, _ in entries]
    shapes = [weights[n].shape[1:][::-1] if t else weights[n].shape[1:] for (n, _), t in zip(entries, turned)]
    by_rows = [t or SHARD_AXIS[n] == 1 for (n, _), t in zip(entries, turned)]
    packed, counts = _pack_rows([(weights[n][li].T if t else weights[n][li]).astype(BF16) for (n, li), t in zip(entries, turned)])
    rows = packed.shape[0]
    pad_rows = rows - sum(counts)

    def unpack(per_chip_packed):
        out = {}
        for entry, (s0, s1), rowwise, blk in zip(entries, shapes, by_rows, _unpack_rows(per_chip_packed, counts, shapes)):
            out[entry] = blk.reshape(N_CHIPS * s0, s1) if rowwise else jnp.transpose(blk, (1, 0, 2)).reshape(s0, N_CHIPS * s1)
        return out

    def pack_grads(whole):
        pieces = []
        for entry, (s0, s1), rowwise in zip(entries, shapes, by_rows):
            g = whole[entry]
            by_chip = g.reshape(N_CHIPS, s0, s1) if rowwise else jnp.transpose(g.reshape(s0, N_CHIPS, s1), (1, 0, 2))
            pieces.append(by_chip.reshape(N_CHIPS, -1, PACK_COLS))
        if pad_rows:
            pieces.append(jnp.zeros((N_CHIPS, pad_rows, PACK_COLS), BF16))
        return jnp.concatenate(pieces, axis=1).reshape(N_CHIPS, 2, rows // 2, PACK_COLS)

    def unpack_reduced(reduced):
        shards = _unpack_rows(reduced.reshape(rows, PACK_COLS), counts, shapes)
        return {entry: s.T if t else s for entry, s, t in zip(entries, shards, turned)}

    return dict(packed=packed, unpack=unpack, pack_grads=pack_grads, unpack_reduced=unpack_reduced)


def kernel(x, meta_tokens, norm_mix_g, w_in, pool_w, pool_scale, q_norm_g, kv_norm_g, w_uq, w_ukv, w_pa, w_pb, w_o, norm_ffn_g, w_gate, w_up, w_down, final_norm_g, loss_target, m_meta_tokens, m_norm_mix_g, m_w_in, m_pool_w, m_pool_scale, m_q_norm_g, m_kv_norm_g, m_w_uq, m_w_ukv, m_w_pa, m_w_pb, m_w_o, m_norm_ffn_g, m_w_gate, m_w_up, m_w_down, m_final_norm_g, v_meta_tokens, v_norm_mix_g, v_w_in, v_pool_w, v_pool_scale, v_q_norm_g, v_kv_norm_g, v_w_uq, v_w_ukv, v_w_pa, v_w_pb, v_w_o, v_norm_ffn_g, v_w_gate, v_w_up, v_w_down, v_final_norm_g):
    weights = dict(meta_tokens=meta_tokens, norm_mix_g=norm_mix_g, w_in=w_in, pool_w=pool_w, pool_scale=pool_scale,
                   q_norm_g=q_norm_g, kv_norm_g=kv_norm_g, w_uq=w_uq, w_ukv=w_ukv, w_pa=w_pa, w_pb=w_pb, w_o=w_o,
                   norm_ffn_g=norm_ffn_g, w_gate=w_gate, w_up=w_up, w_down=w_down, final_norm_g=final_norm_g)
    first = dict(meta_tokens=m_meta_tokens, norm_mix_g=m_norm_mix_g, w_in=m_w_in, pool_w=m_pool_w, pool_scale=m_pool_scale,
                 q_norm_g=m_q_norm_g, kv_norm_g=m_kv_norm_g, w_uq=m_w_uq, w_ukv=m_w_ukv, w_pa=m_w_pa, w_pb=m_w_pb, w_o=m_w_o,
                 norm_ffn_g=m_norm_ffn_g, w_gate=m_w_gate, w_up=m_w_up, w_down=m_w_down, final_norm_g=m_final_norm_g)
    second = dict(meta_tokens=v_meta_tokens, norm_mix_g=v_norm_mix_g, w_in=v_w_in, pool_w=v_pool_w, pool_scale=v_pool_scale,
                  q_norm_g=v_q_norm_g, kv_norm_g=v_kv_norm_g, w_uq=v_w_uq, w_ukv=v_w_ukv, w_pa=v_w_pa, w_pb=v_w_pb, w_o=v_w_o,
                  norm_ffn_g=v_norm_ffn_g, w_gate=v_w_gate, w_up=v_w_up, w_down=v_w_down, final_norm_g=v_final_norm_g)
    core = lax.axis_index("c")
    chip = 2 * lax.axis_index("x") + lax.axis_index("y")
    d = x.shape[-1]
    meta_cols = meta_tokens.shape[1]

    early_first = _group_codec([(n, 0) for n in EARLY_WEIGHTS], weights)
    riding = [_group_codec([(n, 0) for n in LATE_WEIGHTS] + [(n, 1) for n in EARLY_WEIGHTS], weights),
              _group_codec([(n, 1) for n in LATE_WEIGHTS], weights)]
    gathered, meta_all = _all_gather_shards(early_first["packed"].reshape(2, -1, PACK_COLS), meta_tokens)
    early_weights = early_first["unpack"](gathered.reshape(N_CHIPS, -1, PACK_COLS))
    meta_full = jnp.concatenate([meta_all[j] for j in range(N_CHIPS)], axis=1)
    small = {n: weights[n] for n in SMALL_WEIGHTS}

    loss, grad_x, d_meta_rows, g_small, early_partial, parts = _local_step(
        x, loss_target, meta_full, small, {n: early_weights[n, 0] for n in EARLY_WEIGHTS}, riding, early_first)

    shards = early_first["unpack_reduced"](_reduce_scatter(early_partial, core))
    for codec, from_all in zip(riding, parts):
        shards.update(codec["unpack_reduced"](_pair_gather(_chip_sum(from_all))))
    grads = {n: jnp.stack([shards[n, 0], shards[n, 1]]) for n in BIG_WEIGHTS}

    small_shapes = [weights[n].shape for n in SMALL_WEIGHTS]
    small_flat = jnp.concatenate([g_small[n].reshape(-1) for n in SMALL_WEIGHTS])
    small_len = small_flat.shape[0]
    small_rows = _round_up(-(-small_len // PACK_COLS), 8)
    small_pack = jnp.pad(small_flat, (0, small_rows * PACK_COLS - small_len)).reshape(small_rows, PACK_COLS)
    meta_rows = N_META * d // PACK_COLS
    summed = _all_reduce_small(d_meta_rows.reshape(-1, meta_rows, PACK_COLS), small_pack)
    grad_meta_full = summed[:meta_rows].reshape(N_META, d)
    grads["meta_tokens"] = lax.dynamic_slice_in_dim(grad_meta_full, chip * meta_cols, meta_cols, axis=1)
    small_sum = summed[meta_rows:].reshape(-1)
    start = 0
    for n, shape in zip(SMALL_WEIGHTS, small_shapes):
        size = 1
        for s in shape:
            size *= s
        grads[n] = small_sum[start:start + size].reshape(shape)
        start += size

    deltas, new_m, new_v = {}, {}, {}
    for n in WEIGHT_ORDER:
        deltas[n], new_m[n], new_v[n] = _adamw(weights[n], grads[n], first[n], second[n])

    total_loss = lax.psum(loss[0, 0], ("x", "y", "c"))
    return (total_loss, grad_x, *[grads[n] for n in WEIGHT_ORDER], *[deltas[n] for n in WEIGHT_ORDER],
            *[new_m[n] for n in WEIGHT_ORDER], *[new_v[n] for n in WEIGHT_ORDER])
```

```python
import functools

import jax
import jax.numpy as jnp
from jax import lax
from jax.experimental import pallas as pl
from jax.experimental.pallas import tpu as pltpu

F32 = jnp.float32
BF16 = jnp.bfloat16

N_META = 16
POOL_WINDOWS = (2, 4, 8, 16)
POOL_GROUP = 128
POOL_WIDTH = POOL_GROUP * len(POOL_WINDOWS)
QK_NOPE = 64
QK_ROPE = 32
V_DIM = 64
QK_DIM = QK_NOPE + QK_ROPE
Q_RANK = 256
KV_RANK = 128
HEAD_PAD = 128
SM_SCALE = QK_DIM ** -0.5
ROPE_THETA = 10000.0
NORM_EPS = 1e-6
MASK_VALUE = -1e30
Z_FIXED = POOL_WIDTH + Q_RANK + KV_RANK + HEAD_PAD

ADAM_LR = 0.001
ADAM_B1 = 0.9
ADAM_B2 = 0.999
ADAM_EPS = 1e-08
ADAM_WD = 0.01
ADAM_STEP = 10

N_CHIPS = 4
ATT_BLOCK = 256
SEQ_PAD = 128
ATT_Q_ROWS = 256
ATT_FWD_HEADS = 8
ATT_BWD_HEADS = 4
PACK_COLS = 1024
PACK_TILE = 512
VMEM_LIMIT = 60 * 1024 * 1024
MXU_DEPTH = 256
ACC_BYTES = 8 * 1024 * 1024

MESH = pl.DeviceIdType.MESH

BIG_WEIGHTS = ("w_in", "w_uq", "w_ukv", "w_pa", "w_pb", "w_o", "w_gate", "w_up", "w_down")
EARLY_WEIGHTS = ("w_in", "w_uq", "w_ukv")
LATE_WEIGHTS = ("w_pa", "w_pb", "w_o", "w_gate", "w_up", "w_down")
TRANSPOSED_WEIGHTS = ("w_in", "w_gate", "w_up")
SHARD_AXIS = {"w_in": 2, "w_uq": 2, "w_ukv": 2, "w_pa": 2, "w_pb": 1, "w_o": 1, "w_gate": 2, "w_up": 2, "w_down": 1}
SMALL_WEIGHTS = ("norm_mix_g", "pool_w", "pool_scale", "q_norm_g", "kv_norm_g", "norm_ffn_g", "final_norm_g")
WEIGHT_ORDER = ("meta_tokens", "norm_mix_g", "w_in", "pool_w", "pool_scale", "q_norm_g", "kv_norm_g", "w_uq", "w_ukv",
                "w_pa", "w_pb", "w_o", "norm_ffn_g", "w_gate", "w_up", "w_down", "final_norm_g")


def _round_up(n, m):
    return -(-n // m) * m


def _vmem_spec():
    return pl.BlockSpec(memory_space=pltpu.VMEM)


def _any_spec():
    return pl.BlockSpec(memory_space=pl.ANY)


def _row_block(tm, width, col_block=0):
    return pl.BlockSpec((tm, width), lambda i, cb=col_block: (i, cb))


def _params(sem, vmem=VMEM_LIMIT):
    return pltpu.CompilerParams(dimension_semantics=sem, vmem_limit_bytes=vmem)


def _token_tile(t, want):
    best = SEQ_PAD
    for tm in range(32, min(t, 2 * want) + 1, 32):
        if t % tm == 0 and abs(tm - want) < abs(best - want):
            best = tm
    return best


def _dot(a, b):
    return jnp.dot(a, b, preferred_element_type=F32)


def _dot_nt(a, b):
    return lax.dot_general(a, b, (((1,), (1,)), ((), ())), preferred_element_type=F32)


def _dot_tn(a, b):
    return lax.dot_general(a, b, (((0,), (0,)), ((), ())), preferred_element_type=F32)


def _rms_fwd(x, g):
    r = lax.rsqrt(jnp.mean(x * x, axis=-1, keepdims=True) + NORM_EPS)
    xh = x * r
    return xh * g, xh, r


def _rms_bwd(dy, xh, r, g):
    gdy = dy * g
    dx = r * (gdy - xh * jnp.mean(xh * gdy, axis=-1, keepdims=True))
    return dx, dy * xh


def _rope_fwd(x, c, sa, sb):
    return x * c + pltpu.roll(x, 16, 1) * sa + pltpu.roll(x, HEAD_PAD - 16, 1) * sb


def _rope_bwd(d, c, sa, sb):
    return d * c + pltpu.roll(d * sa, HEAD_PAD - 16, 1) + pltpu.roll(d * sb, 16, 1)


def _in_proj_fwd(h, g, w_in_pt):
    t, d = h.shape
    nz = w_in_pt.shape[0]
    tm = _token_tile(t, 512)

    def body(h_ref, g_ref, w_ref, z_ref):
        hn, _, _ = _rms_fwd(h_ref[...], g_ref[...])
        z_ref[...] = _dot_nt(hn.astype(BF16), w_ref[...]).astype(BF16)

    return pl.pallas_call(
        body, name="in_proj_fwd", grid=(t // tm,),
        in_specs=[_row_block(tm, d), _vmem_spec(), _vmem_spec()],
        out_specs=_row_block(tm, nz),
        out_shape=jax.ShapeDtypeStruct((t, nz), BF16),
        compiler_params=_params(("parallel",)),
    )(h, g, w_in_pt)


def _window_sum(x, w, row, forward):
    n = x.shape[0]
    s = x
    k = 1
    while k < w:
        if forward:
            s = s + jnp.where(row >= k, pltpu.roll(s, k, 0), 0.0)
        else:
            s = s + jnp.where(row < n - k, pltpu.roll(s, n - k, 0), 0.0)
        k *= 2
    return s


def _pool_fwd(z3, pool_w, pool_scale):
    b, lp, _ = z3.shape

    def body(u_ref, pw_ref, sc_ref, a_ref):
        row = lax.broadcasted_iota(jnp.int32, (lp, POOL_GROUP), 0)
        pos = row.astype(F32)
        for gi, w in enumerate(POOL_WINDOWS):
            cols = slice(gi * POOL_GROUP, (gi + 1) * POOL_GROUP)
            u = u_ref[0, :, cols].astype(F32)
            y = _window_sum(u, w, row, True) / jnp.minimum(pos + 1.0, float(w)) - u
            yw = _dot(y.astype(BF16), pw_ref[gi])
            a_ref[0, :, cols] = (yw * sc_ref[:, cols]).astype(BF16)

    return pl.pallas_call(
        body, name="pool_fwd", grid=(b,),
        in_specs=[pl.BlockSpec((1, lp, POOL_WIDTH), lambda i: (i, 0, 0)), _vmem_spec(), _vmem_spec()],
        out_specs=pl.BlockSpec((1, lp, POOL_WIDTH), lambda i: (i, 0, 0)),
        out_shape=jax.ShapeDtypeStruct((b, lp, POOL_WIDTH), BF16),
        compiler_params=_params(("parallel",)),
    )(z3, pool_w, pool_scale)


def _qkv_fwd(z, g_q, g_kv, w_uq_p, w_kv_p, rope_c, rope_sa, rope_sb):
    t = z.shape[0]
    hw = w_uq_p.shape[1]
    heads = hw // HEAD_PAD
    tm = _token_tile(t, 512)

    def body(cq_ref, ckv_ref, kr_ref, gq_ref, gkv_ref, wq_ref, wkv_ref, c_ref, sa_ref, sb_ref, q_ref, k_ref, v_ref):
        c, sa, sb = c_ref[...], sa_ref[...], sb_ref[...]
        cqn, _, _ = _rms_fwd(cq_ref[...].astype(F32), gq_ref[...])
        qraw = _dot(cqn.astype(BF16), wq_ref[...])
        ckvn, _, _ = _rms_fwd(ckv_ref[...].astype(F32), gkv_ref[...])
        kvraw = _dot(ckvn.astype(BF16), wkv_ref[...])
        kr = _rope_fwd(kr_ref[...].astype(F32), c, sa, sb)
        for hd in range(heads):
            cols = slice(hd * HEAD_PAD, (hd + 1) * HEAD_PAD)
            q_ref[:, cols] = (_rope_fwd(qraw[:, cols], c, sa, sb) * SM_SCALE).astype(BF16)
            k_ref[:, cols] = (kvraw[:, cols] + kr).astype(BF16)
        lane = lax.broadcasted_iota(jnp.int32, (tm, hw), 1)
        v_ref[...] = jnp.where((lane & (HEAD_PAD - 1)) == V_DIM, 1.0, kvraw[:, hw:]).astype(BF16)

    out = jax.ShapeDtypeStruct((t, hw), BF16)
    return pl.pallas_call(
        body, name="qkv_fwd", grid=(t // tm,),
        in_specs=[_row_block(tm, Q_RANK, POOL_WIDTH // Q_RANK),
                  _row_block(tm, KV_RANK, (POOL_WIDTH + Q_RANK) // KV_RANK),
                  _row_block(tm, HEAD_PAD, (POOL_WIDTH + Q_RANK + KV_RANK) // HEAD_PAD),
                  _vmem_spec(), _vmem_spec(), _vmem_spec(), _vmem_spec(),
                  _row_block(tm, HEAD_PAD), _row_block(tm, HEAD_PAD), _row_block(tm, HEAD_PAD)],
        out_specs=[_row_block(tm, hw)] * 3,
        out_shape=[out, out, out],
        compiler_params=_params(("parallel",)),
    )(z, z, z, g_q, g_kv, w_uq_p, w_kv_p, rope_c, rope_sa, rope_sb)


def _heads_per_step(heads, want):
    while heads % want:
        want //= 2
    return want


def _causal_mask(rows):
    row = lax.broadcasted_iota(jnp.int32, (rows, rows), 0)
    col = lax.broadcasted_iota(jnp.int32, (rows, rows), 1)
    return col <= row


def _attn_blocks(real_len):
    tail_start = (-(-real_len // ATT_BLOCK) - 1) * ATT_BLOCK
    return tail_start // ATT_BLOCK, tail_start, _round_up(real_len - tail_start, SEQ_PAD)


def _call_with_exchange(body, exchange, *, name, grid, in_specs, out_specs, out_shape, scratch_shapes, operands):
    if exchange is None:
        return pl.pallas_call(body, name=name, grid=grid, in_specs=in_specs, out_specs=out_specs, out_shape=out_shape,
                              scratch_shapes=scratch_shapes,
                              compiler_params=_params(("parallel",) + ("arbitrary",) * (len(grid) - 1)))(*operands)
    n_in, n_out, n_scratch = len(in_specs), len(out_specs), len(scratch_shapes)

    def riding(*refs):
        ins, src = refs[:n_in], refs[n_in]
        outs, dst = refs[n_in + 1:n_in + 1 + n_out], refs[n_in + 1 + n_out]
        scratch = refs[n_in + 2 + n_out:n_in + 2 + n_out + n_scratch]
        send_sems, recv_sems = refs[n_in + 2 + n_out + n_scratch:]
        steps = [pl.program_id(a) for a in range(len(grid))]

        @pl.when(functools.reduce(jnp.logical_and, [s == 0 for s in steps]))
        def _():
            exchange["start"](src, dst, send_sems, recv_sems)

        body(*ins, *outs, *scratch)

        @pl.when(functools.reduce(jnp.logical_and, [s == g - 1 for s, g in zip(steps, grid)]))
        def _():
            exchange["finish"](src, dst, send_sems, recv_sems)

    n = exchange["copies"]
    return pl.pallas_call(
        riding, name=name + "_" + exchange["name"], grid=grid,
        in_specs=list(in_specs) + [_any_spec()], out_specs=list(out_specs) + [_any_spec()],
        out_shape=list(out_shape) + [exchange["out_shape"]],
        scratch_shapes=list(scratch_shapes) + [pltpu.SemaphoreType.DMA((n,)), pltpu.SemaphoreType.DMA((n,))],
        compiler_params=_params(("arbitrary",) * len(grid)),
    )(*operands, exchange["operand"])


def _attn_fwd(q3, k3, v3, real_len, exchange=None):
    b, lp, hw = q3.shape
    heads = hw // HEAD_PAD
    tb = ATT_BLOCK
    nfull, tail_start, tail = _attn_blocks(real_len)
    done = tail_start + tail
    hpg = _heads_per_step(heads, ATT_FWD_HEADS)
    width = hpg * HEAD_PAD

    def body(q_ref, k_ref, v_ref, o_ref, lse_ref):
        group = pl.program_id(1)

        @pl.when(group == 0)
        def _():
            lse_ref[...] = jnp.zeros_like(lse_ref)

        def q_rows(r0, rows, whole_kv_blocks, back):
            def kv_step(c0, keys, states, mask):
                out = []
                for hd, (m, acc) in enumerate(states):
                    cols = slice(hd * HEAD_PAD, (hd + 1) * HEAD_PAD)
                    s = _dot_nt(q_ref[0, pl.ds(r0, rows), cols], k_ref[0, pl.ds(c0, keys), cols])
                    if mask is not None:
                        s = jnp.where(mask, s, MASK_VALUE)
                    m_new = jnp.maximum(m, jnp.max(s, axis=-1, keepdims=True))
                    p = jnp.exp((s - m_new).astype(BF16))
                    acc = jnp.exp(m - m_new) * acc + _dot(p, v_ref[0, pl.ds(c0, keys), cols])
                    out.append((m_new, acc))
                return tuple(out)

            init = tuple((jnp.full((rows, 1), MASK_VALUE, F32), jnp.zeros((rows, HEAD_PAD), F32)) for _ in range(hpg))
            states = lax.fori_loop(0, whole_kv_blocks, lambda j, st: kv_step(pl.multiple_of(j * tb, tb), tb, st, None), init)
            query = lax.broadcasted_iota(jnp.int32, (rows, back + rows), 0)
            key = lax.broadcasted_iota(jnp.int32, (rows, back + rows), 1)
            states = kv_step(pl.multiple_of(r0 - back, SEQ_PAD), back + rows, states, key <= query + back)
            lane = lax.broadcasted_iota(jnp.int32, (rows, HEAD_PAD), 1)
            lse_rows = lse_ref[0, pl.ds(r0, rows), :]
            for hd, (m, acc) in enumerate(states):
                l = jnp.sum(jnp.where(lane == V_DIM, acc, 0.0), axis=-1, keepdims=True)
                o_ref[0, pl.ds(r0, rows), hd * HEAD_PAD:(hd + 1) * HEAD_PAD] = (acc / l).astype(BF16)
                lse_rows = jnp.where(lane == group * hpg + hd, m + jnp.log(l), lse_rows)
            lse_ref[0, pl.ds(r0, rows), :] = lse_rows

        def whole_block(i, carry):
            for back in range(0, tb, ATT_Q_ROWS):
                q_rows(pl.multiple_of(i * tb + back, ATT_Q_ROWS), ATT_Q_ROWS, i, back)
            return carry

        lax.fori_loop(0, nfull, whole_block, 0)
        for back in range(0, tail, ATT_Q_ROWS):
            q_rows(tail_start + back, min(ATT_Q_ROWS, tail - back), nfull, back)
        if done < lp:
            o_ref[0, done:lp, :] = jnp.zeros((lp - done, width), BF16)

    head_spec = pl.BlockSpec((1, lp, width), lambda bi, hi: (bi, 0, hi))
    return _call_with_exchange(
        body, exchange, name="attn_fwd", grid=(b, heads // hpg),
        in_specs=[head_spec, head_spec, head_spec],
        out_specs=[head_spec, pl.BlockSpec((1, lp, HEAD_PAD), lambda bi, hi: (bi, 0, 0))],
        out_shape=[jax.ShapeDtypeStruct((b, lp, hw), BF16), jax.ShapeDtypeStruct((b, lp, HEAD_PAD), F32)],
        scratch_shapes=[], operands=(q3, k3, v3))


def _merge_fwd(h, z, a, o, w_pa, w_pb_p, w_o):
    t, d = h.shape
    hw = o.shape[1]
    tm = _token_tile(t, 512)
    gate_block = Z_FIXED // d

    def body(h_ref, ga_ref, gb_ref, a_ref, o_ref, wpa_ref, wpb_ref, wo_ref, h1_ref, pa_ref, pb_ref):
        pa = _dot(a_ref[...], wpa_ref[...])
        pb = _dot(o_ref[...], wpb_ref[...])
        merged = jax.nn.sigmoid(ga_ref[...].astype(F32)) * pa + jax.nn.sigmoid(gb_ref[...].astype(F32)) * pb
        h1_ref[...] = h_ref[...] + _dot(merged.astype(BF16), wo_ref[...])
        pa_ref[...] = pa.astype(BF16)
        pb_ref[...] = pb.astype(BF16)

    return pl.pallas_call(
        body, name="merge_fwd", grid=(t // tm,),
        in_specs=[_row_block(tm, d), _row_block(tm, d, gate_block), _row_block(tm, d, gate_block + 1),
                  _row_block(tm, POOL_WIDTH), _row_block(tm, hw), _vmem_spec(), _vmem_spec(), _vmem_spec()],
        out_specs=[_row_block(tm, d)] * 3,
        out_shape=[jax.ShapeDtypeStruct((t, d), F32), jax.ShapeDtypeStruct((t, d), BF16), jax.ShapeDtypeStruct((t, d), BF16)],
        compiler_params=_params(("parallel",)),
    )(h, z, z, a, o, w_pa, w_pb_p, w_o)


def _ffn_fwd(h, g, w_gate_t, w_up_t, w_down):
    t, d = h.shape
    f = w_gate_t.shape[0]
    tm = _token_tile(t, 256)

    def body(h_ref, g_ref, wg_ref, wu_ref, wd_ref, h2_ref, a_ref, b_ref):
        x = h_ref[...]
        hn, _, _ = _rms_fwd(x, g_ref[...])
        hn = hn.astype(BF16)
        ga = _dot_nt(hn, wg_ref[...])
        up = _dot_nt(hn, wu_ref[...])
        act = ga * jax.nn.sigmoid(ga) * up
        h2_ref[...] = x + _dot(act.astype(BF16), wd_ref[...])
        a_ref[...] = ga.astype(BF16)
        b_ref[...] = up.astype(BF16)

    return pl.pallas_call(
        body, name="ffn_fwd", grid=(t // tm,),
        in_specs=[_row_block(tm, d), _vmem_spec(), _vmem_spec(), _vmem_spec(), _vmem_spec()],
        out_specs=[_row_block(tm, d), _row_block(tm, f), _row_block(tm, f)],
        out_shape=[jax.ShapeDtypeStruct((t, d), F32), jax.ShapeDtypeStruct((t, f), BF16), jax.ShapeDtypeStruct((t, f), BF16)],
        compiler_params=_params(("parallel",)),
    )(h, g, w_gate_t, w_up_t, w_down)


def _loss_head(h, g, target, valid):
    t, d = h.shape
    tm = _token_tile(t, 512)

    def body(h_ref, g_ref, t_ref, valid_ref, dh_ref, loss_ref, dg_ref):
        @pl.when(pl.program_id(0) == 0)
        def _():
            loss_ref[...] = jnp.zeros_like(loss_ref)
            dg_ref[...] = jnp.zeros_like(dg_ref)

        gain = g_ref[...]
        y, xh, r = _rms_fwd(h_ref[...], gain)
        err = (y - t_ref[...]) * valid_ref[...]
        per_row = jnp.sum(err * err, axis=-1, keepdims=True) / d
        loss_ref[...] += 0.5 * jnp.sum(per_row, axis=0, keepdims=True)
        dx, dg_rows = _rms_bwd(err / d, xh, r, gain)
        dh_ref[...] = dx
        dg_ref[...] += jnp.sum(dg_rows, axis=0, keepdims=True)

    return pl.pallas_call(
        body, name="loss_head", grid=(t // tm,),
        in_specs=[_row_block(tm, d), _vmem_spec(), _row_block(tm, d), _row_block(tm, 1)],
        out_specs=[_row_block(tm, d), pl.BlockSpec((1, 1), lambda i: (0, 0)), pl.BlockSpec((1, d), lambda i: (0, 0))],
        out_shape=[jax.ShapeDtypeStruct((t, d), F32), jax.ShapeDtypeStruct((1, 1), F32), jax.ShapeDtypeStruct((1, d), F32)],
        compiler_params=_params(("arbitrary",)),
    )(h, g, target, valid)


def _weight_grad(x, y, name):
    t, k = x.shape
    n = y.shape[1]
    tm = _token_tile(t, 8 * MXU_DEPTH)
    tk, tn = k, n
    while tk * tn * 4 > ACC_BYTES and max(tk, tn) % 256 == 0:
        if tk > tn:
            tk //= 2
        else:
            tn //= 2
    steps = t // tm

    def body(x_ref, y_ref, o_ref, acc):
        @pl.when(pl.program_id(2) == 0)
        def _():
            acc[...] = jnp.zeros_like(acc)

        acc[...] += _dot_tn(x_ref[...].astype(BF16), y_ref[...].astype(BF16))

        @pl.when(pl.program_id(2) == steps - 1)
        def _():
            o_ref[...] = acc[...].astype(BF16)

    return pl.pallas_call(
        body, name=name, grid=(k // tk, n // tn, steps),
        in_specs=[pl.BlockSpec((tm, tk), lambda a, b, i: (i, a)), pl.BlockSpec((tm, tn), lambda a, b, i: (i, b))],
        out_specs=pl.BlockSpec((tk, tn), lambda a, b, i: (a, b)),
        out_shape=jax.ShapeDtypeStruct((k, n), BF16),
        scratch_shapes=[pltpu.VMEM((tk, tn), F32)],
        compiler_params=_params(("parallel", "parallel", "arbitrary")),
    )(x, y)


def _ffn_bwd(h, dh2, a, b, g, w_gate_t, w_up_t, w_down):
    t, d = h.shape
    f = a.shape[1]
    tm = _token_tile(t, 256)

    def body(h_ref, dh2_ref, a_ref, b_ref, g_ref, wg_ref, wu_ref, wd_ref, dh_ref, hn_ref, act_ref, da_ref, db_ref, dg_ref):
        @pl.when(pl.program_id(0) == 0)
        def _():
            dg_ref[...] = jnp.zeros_like(dg_ref)

        gain = g_ref[...]
        hn, xh, r = _rms_fwd(h_ref[...], gain)
        hn_ref[...] = hn.astype(BF16)
        dh2 = dh2_ref[...]
        dact = _dot_nt(dh2.astype(BF16), wd_ref[...])
        ga = a_ref[...].astype(F32)
        up = b_ref[...].astype(F32)
        sg = jax.nn.sigmoid(ga)
        silu = ga * sg
        act_ref[...] = (silu * up).astype(BF16)
        da = (dact * up * (sg * (1.0 + ga * (1.0 - sg)))).astype(BF16)
        db = (dact * silu).astype(BF16)
        da_ref[...] = da
        db_ref[...] = db
        dhn = _dot(da, wg_ref[...]) + _dot(db, wu_ref[...])
        dx, dg_rows = _rms_bwd(dhn, xh, r, gain)
        dh_ref[...] = dh2 + dx
        dg_ref[...] += jnp.sum(dg_rows, axis=0, keepdims=True)

    return pl.pallas_call(
        body, name="ffn_bwd", grid=(t // tm,),
        in_specs=[_row_block(tm, d), _row_block(tm, d), _row_block(tm, f), _row_block(tm, f),
                  _vmem_spec(), _vmem_spec(), _vmem_spec(), _vmem_spec()],
        out_specs=[_row_block(tm, d), _row_block(tm, d), _row_block(tm, f), _row_block(tm, f), _row_block(tm, f),
                   pl.BlockSpec((1, d), lambda i: (0, 0))],
        out_shape=[jax.ShapeDtypeStruct((t, d), F32), jax.ShapeDtypeStruct((t, d), BF16), jax.ShapeDtypeStruct((t, f), BF16),
                   jax.ShapeDtypeStruct((t, f), BF16), jax.ShapeDtypeStruct((t, f), BF16), jax.ShapeDtypeStruct((1, d), F32)],
        compiler_params=_params(("arbitrary",)),
    )(h, dh2, a, b, g, w_gate_t, w_up_t, w_down)


def _merge_bwd(dh1, z, pa, pb, w_o, w_pa, w_pb_p):
    t, d = dh1.shape
    hw = w_pb_p.shape[0]
    tm = _token_tile(t, 512)
    gate_block = Z_FIXED // d

    def body(dh_ref, ga_ref, gb_ref, pa_ref, pb_ref, wo_ref, wpa_ref, wpb_ref,
             mg_ref, dpa_ref, dpb_ref, dga_ref, dgb_ref, da_ref, do_ref):
        dm = _dot_nt(dh_ref[...].astype(BF16), wo_ref[...])
        sa = jax.nn.sigmoid(ga_ref[...].astype(F32))
        sb = jax.nn.sigmoid(gb_ref[...].astype(F32))
        pa = pa_ref[...].astype(F32)
        pb = pb_ref[...].astype(F32)
        mg_ref[...] = (sa * pa + sb * pb).astype(BF16)
        dpa = (dm * sa).astype(BF16)
        dpb = (dm * sb).astype(BF16)
        dpa_ref[...] = dpa
        dpb_ref[...] = dpb
        dga_ref[...] = (dm * pa * (sa * (1.0 - sa))).astype(BF16)
        dgb_ref[...] = (dm * pb * (sb * (1.0 - sb))).astype(BF16)
        da_ref[...] = _dot_nt(dpa, wpa_ref[...]).astype(BF16)
        do_ref[...] = _dot_nt(dpb, wpb_ref[...]).astype(BF16)

    wide = jax.ShapeDtypeStruct((t, d), BF16)
    return pl.pallas_call(
        body, name="merge_bwd", grid=(t // tm,),
        in_specs=[_row_block(tm, d), _row_block(tm, d, gate_block), _row_block(tm, d, gate_block + 1),
                  _row_block(tm, d), _row_block(tm, d), _vmem_spec(), _vmem_spec(), _vmem_spec()],
        out_specs=[_row_block(tm, d)] * 5 + [_row_block(tm, POOL_WIDTH), _row_block(tm, hw)],
        out_shape=[wide] * 5 + [jax.ShapeDtypeStruct((t, POOL_WIDTH), BF16), jax.ShapeDtypeStruct((t, hw), BF16)],
        compiler_params=_params(("parallel",)),
    )(dh1, z, z, pa, pb, w_o, w_pa, w_pb_p)


def _attn_bwd(q3, k3, v3, o3, do3, lse3, real_len, exchange=None):
    b, lp, hw = q3.shape
    heads = hw // HEAD_PAD
    tb = ATT_BLOCK
    nfull, tail_start, tail = _attn_blocks(real_len)
    done = tail_start + tail
    hpg = _heads_per_step(heads, ATT_BWD_HEADS)
    width = hpg * HEAD_PAD

    def body(q_ref, k_ref, v_ref, o_ref, do_ref, lse_ref, dq_ref, dk_ref, dv_ref, dqt_acc, lse_row, delta_row):
        group = pl.program_id(1)
        lse_t = jnp.transpose(lse_ref[0])
        head_of_row = lax.broadcasted_iota(jnp.int32, (HEAD_PAD, lp), 0)
        for hd in range(hpg):
            cols = slice(hd * HEAD_PAD, (hd + 1) * HEAD_PAD)
            lse_row[hd] = jnp.sum(jnp.where(head_of_row == group * hpg + hd, lse_t, 0.0), axis=0, keepdims=True)
            prod = do_ref[0, :, cols].astype(F32) * o_ref[0, :, cols].astype(F32)
            delta_row[hd] = jnp.sum(jnp.transpose(prod), axis=0, keepdims=True)
        dqt_acc[...] = jnp.zeros_like(dqt_acc)

        def kv_rows(c0, keys, whole_q_blocks_from):
            k_t = [jnp.transpose(k_ref[0, pl.ds(c0, keys), hd * HEAD_PAD:(hd + 1) * HEAD_PAD].astype(F32)).astype(BF16)
                   for hd in range(hpg)]

            def q_step(r0, rows, states, mask):
                out = []
                for hd, (dk, dv) in enumerate(states):
                    cols = slice(hd * HEAD_PAD, (hd + 1) * HEAD_PAD)
                    q = q_ref[0, pl.ds(r0, rows), cols]
                    do = do_ref[0, pl.ds(r0, rows), cols]
                    s_t = _dot_nt(k_ref[0, pl.ds(c0, keys), cols], q)
                    if mask is not None:
                        s_t = jnp.where(mask, s_t, MASK_VALUE)
                    p_t = jnp.exp(s_t - lse_row[hd, :, pl.ds(r0, rows)])
                    dp_t = _dot_nt(v_ref[0, pl.ds(c0, keys), cols], do)
                    ds_t = (p_t * (dp_t - delta_row[hd, :, pl.ds(r0, rows)])).astype(BF16)
                    dv = dv + _dot(p_t.astype(BF16), do)
                    dk = dk + _dot(ds_t, q)
                    dqt_acc[cols, pl.ds(r0, rows)] += _dot(k_t[hd], ds_t)
                    out.append((dk, dv))
                return tuple(out)

            zero = jnp.zeros((keys, HEAD_PAD), F32)
            key_pos = lax.broadcasted_iota(jnp.int32, (keys, keys), 0)
            query_pos = lax.broadcasted_iota(jnp.int32, (keys, keys), 1)
            states = q_step(c0, keys, tuple((zero, zero) for _ in range(hpg)), key_pos <= query_pos)
            if whole_q_blocks_from is not None:
                states = lax.fori_loop(whole_q_blocks_from, nfull,
                                       lambda i, st: q_step(pl.multiple_of(i * tb, tb), tb, st, None), states)
                states = q_step(tail_start, tail, states, None)
            for hd, (dk, dv) in enumerate(states):
                cols = slice(hd * HEAD_PAD, (hd + 1) * HEAD_PAD)
                dk_ref[0, pl.ds(c0, keys), cols] = dk.astype(BF16)
                dv_ref[0, pl.ds(c0, keys), cols] = dv.astype(BF16)

        def whole_block(j, carry):
            kv_rows(pl.multiple_of(j * tb, tb), tb, j + 1)
            return carry

        lax.fori_loop(0, nfull, whole_block, 0)
        kv_rows(tail_start, tail, None)
        if done < lp:
            dk_ref[0, done:lp, :] = jnp.zeros((lp - done, width), BF16)
            dv_ref[0, done:lp, :] = jnp.zeros((lp - done, width), BF16)
        for hd in range(hpg):
            cols = slice(hd * HEAD_PAD, (hd + 1) * HEAD_PAD)
            dq_ref[0, :, cols] = jnp.transpose(dqt_acc[cols, :]).astype(BF16)

    head_spec = pl.BlockSpec((1, lp, width), lambda bi, hi: (bi, 0, hi))
    out = jax.ShapeDtypeStruct((b, lp, hw), BF16)
    return _call_with_exchange(
        body, exchange, name="attn_bwd", grid=(b, heads // hpg),
        in_specs=[head_spec] * 5 + [pl.BlockSpec((1, lp, HEAD_PAD), lambda bi, hi: (bi, 0, 0))],
        out_specs=[head_spec] * 3,
        out_shape=[out, out, out],
        scratch_shapes=[pltpu.VMEM((width, lp), F32), pltpu.VMEM((hpg, 1, lp), F32), pltpu.VMEM((hpg, 1, lp), F32)],
        operands=(q3, k3, v3, o3, do3, lse3))


def _qkv_bwd(dq, dk, dv, z, g_q, g_kv, w_uq_p, w_kv_p, rope_c, rope_sa, rope_sb):
    t, hw = dq.shape
    heads = hw // HEAD_PAD
    tm = _token_tile(t, 512)

    def body(dq_ref, dk_ref, dv_ref, cq_ref, ckv_ref, gq_ref, gkv_ref, wq_ref, wkv_ref, c_ref, sa_ref, sb_ref,
             dqraw_ref, dkvraw_ref, cqn_ref, ckvn_ref, dcq_ref, dckv_ref, dkr_ref, dgq_ref, dgkv_ref):
        @pl.when(pl.program_id(0) == 0)
        def _():
            dgq_ref[...] = jnp.zeros_like(dgq_ref)
            dgkv_ref[...] = jnp.zeros_like(dgkv_ref)

        c, sa, sb = c_ref[...], sa_ref[...], sb_ref[...]
        dkr = jnp.zeros((tm, HEAD_PAD), F32)
        for hd in range(heads):
            cols = slice(hd * HEAD_PAD, (hd + 1) * HEAD_PAD)
            dqraw_ref[:, cols] = _rope_bwd(dq_ref[:, cols].astype(F32) * SM_SCALE, c, sa, sb).astype(BF16)
            dkvraw_ref[:, cols] = dk_ref[:, cols]
            dkr = dkr + dk_ref[:, cols].astype(F32)
        dkvraw_ref[:, hw:] = dv_ref[...]
        lane = lax.broadcasted_iota(jnp.int32, (tm, HEAD_PAD), 1)
        dkr_ref[...] = jnp.where((lane >= QK_NOPE) & (lane < QK_DIM), _rope_bwd(dkr, c, sa, sb), 0.0).astype(BF16)

        gq = gq_ref[...]
        cqn, xh, r = _rms_fwd(cq_ref[...].astype(F32), gq)
        cqn_ref[...] = cqn.astype(BF16)
        dx, dg_rows = _rms_bwd(_dot_nt(dqraw_ref[...], wq_ref[...]), xh, r, gq)
        dcq_ref[...] = dx.astype(BF16)
        dgq_ref[...] += jnp.sum(dg_rows, axis=0, keepdims=True)

        gkv = gkv_ref[...]
        ckvn, xh, r = _rms_fwd(ckv_ref[...].astype(F32), gkv)
        ckvn_ref[...] = ckvn.astype(BF16)
        dx, dg_rows = _rms_bwd(_dot_nt(dkvraw_ref[...], wkv_ref[...]), xh, r, gkv)
        dckv_ref[...] = dx.astype(BF16)
        dgkv_ref[...] += jnp.sum(dg_rows, axis=0, keepdims=True)

    def shape(width, dtype=BF16):
        return jax.ShapeDtypeStruct((t, width), dtype)

    return pl.pallas_call(
        body, name="qkv_bwd", grid=(t // tm,),
        in_specs=[_row_block(tm, hw)] * 3
        + [_row_block(tm, Q_RANK, POOL_WIDTH // Q_RANK), _row_block(tm, KV_RANK, (POOL_WIDTH + Q_RANK) // KV_RANK)]
        + [_vmem_spec()] * 4 + [_row_block(tm, HEAD_PAD)] * 3,
        out_specs=[_row_block(tm, hw), _row_block(tm, 2 * hw), _row_block(tm, Q_RANK), _row_block(tm, KV_RANK),
                   _row_block(tm, Q_RANK), _row_block(tm, KV_RANK), _row_block(tm, HEAD_PAD),
                   pl.BlockSpec((1, Q_RANK), lambda i: (0, 0)), pl.BlockSpec((1, KV_RANK), lambda i: (0, 0))],
        out_shape=[shape(hw), shape(2 * hw), shape(Q_RANK), shape(KV_RANK), shape(Q_RANK), shape(KV_RANK), shape(HEAD_PAD),
                   jax.ShapeDtypeStruct((1, Q_RANK), F32), jax.ShapeDtypeStruct((1, KV_RANK), F32)],
        compiler_params=_params(("arbitrary",)),
    )(dq, dk, dv, z, z, g_q, g_kv, w_uq_p, w_kv_p, rope_c, rope_sa, rope_sb)


def _pool_bwd(z3, da3, pool_w, pool_scale):
    b, lp, _ = z3.shape
    groups = len(POOL_WINDOWS)

    def body(u_ref, da_ref, pw_ref, sc_ref, du_ref, dpw_ref, dsc_ref):
        @pl.when(pl.program_id(0) == 0)
        def _():
            dpw_ref[...] = jnp.zeros_like(dpw_ref)
            dsc_ref[...] = jnp.zeros_like(dsc_ref)

        row = lax.broadcasted_iota(jnp.int32, (lp, POOL_GROUP), 0)
        pos = row.astype(F32)
        for gi, w in enumerate(POOL_WINDOWS):
            cols = slice(gi * POOL_GROUP, (gi + 1) * POOL_GROUP)
            count = jnp.minimum(pos + 1.0, float(w))
            u = u_ref[0, :, cols].astype(F32)
            y = (_window_sum(u, w, row, True) / count - u).astype(BF16)
            yw = _dot(y, pw_ref[gi])
            da = da_ref[0, :, cols].astype(F32)
            dsc_ref[:, cols] += jnp.sum(da * yw, axis=0, keepdims=True)
            dyw = (da * sc_ref[:, cols]).astype(BF16)
            dpw_ref[gi] += _dot_tn(y, dyw)
            dy = _dot_nt(dyw, pw_ref[gi])
            du_ref[0, :, cols] = (_window_sum(dy / count, w, row, False) - dy).astype(BF16)

    return pl.pallas_call(
        body, name="pool_bwd", grid=(b,),
        in_specs=[pl.BlockSpec((1, lp, POOL_WIDTH), lambda i: (i, 0, 0)), pl.BlockSpec((1, lp, POOL_WIDTH), lambda i: (i, 0, 0)),
                  _vmem_spec(), _vmem_spec()],
        out_specs=[pl.BlockSpec((1, lp, POOL_WIDTH), lambda i: (i, 0, 0)),
                   pl.BlockSpec((groups, POOL_GROUP, POOL_GROUP), lambda i: (0, 0, 0)),
                   pl.BlockSpec((1, POOL_WIDTH), lambda i: (0, 0))],
        out_shape=[jax.ShapeDtypeStruct((b, lp, POOL_WIDTH), BF16), jax.ShapeDtypeStruct((groups, POOL_GROUP, POOL_GROUP), F32),
                   jax.ShapeDtypeStruct((1, POOL_WIDTH), F32)],
        compiler_params=_params(("arbitrary",)),
    )(z3, da3, pool_w, pool_scale)


def _in_proj_bwd(h, dh1, du, dcq, dckv, dkr, dga, dgb, g, w_in_pt):
    t, d = h.shape
    nz = w_in_pt.shape[0]
    tm = _token_tile(t, 512)
    widths = (POOL_WIDTH, Q_RANK, KV_RANK, HEAD_PAD, d, d)

    def body(h_ref, dh1_ref, du_ref, dcq_ref, dckv_ref, dkr_ref, dga_ref, dgb_ref, g_ref, w_ref, dh_ref, hn_ref, dz_ref, dg_ref):
        @pl.when(pl.program_id(0) == 0)
        def _():
            dg_ref[...] = jnp.zeros_like(dg_ref)

        gain = g_ref[...]
        hn, xh, r = _rms_fwd(h_ref[...], gain)
        hn_ref[...] = hn.astype(BF16)
        dhn = jnp.zeros((tm, d), F32)
        start = 0
        for piece, width in zip((du_ref, dcq_ref, dckv_ref, dkr_ref, dga_ref, dgb_ref), widths):
            val = piece[...]
            dz_ref[:, start:start + width] = val
            dhn = dhn + _dot(val, w_ref[start:start + width, :])
            start += width
        dx, dg_rows = _rms_bwd(dhn, xh, r, gain)
        dh_ref[...] = dh1_ref[...] + dx
        dg_ref[...] += jnp.sum(dg_rows, axis=0, keepdims=True)

    return pl.pallas_call(
        body, name="in_proj_bwd", grid=(t // tm,),
        in_specs=[_row_block(tm, d), _row_block(tm, d)] + [_row_block(tm, w) for w in widths] + [_vmem_spec(), _vmem_spec()],
        out_specs=[_row_block(tm, d), _row_block(tm, d), _row_block(tm, nz), pl.BlockSpec((1, d), lambda i: (0, 0))],
        out_shape=[jax.ShapeDtypeStruct((t, d), F32), jax.ShapeDtypeStruct((t, d), BF16), jax.ShapeDtypeStruct((t, nz), BF16),
                   jax.ShapeDtypeStruct((1, d), F32)],
        compiler_params=_params(("arbitrary",)),
    )(h, dh1, du, dcq, dckv, dkr, dga, dgb, g, w_in_pt)


def _pad_heads(w, heads, width):
    k = w.shape[0]
    w = w.reshape(k, heads, width)
    return jnp.pad(w, ((0, 0), (0, 0), (0, HEAD_PAD - width))).reshape(k, heads * HEAD_PAD)


def _unpad_heads(w, heads, width):
    k = w.shape[0]
    return w.reshape(k, heads, HEAD_PAD)[:, :, :width].reshape(k, heads * width)


def _early_layouts(w, heads):
    o3, o4 = POOL_WIDTH + Q_RANK + KV_RANK, POOL_WIDTH + Q_RANK + KV_RANK + QK_ROPE
    w_in_t = w["w_in"]
    rope_rows = jnp.pad(w_in_t[o3:o4], ((QK_NOPE, HEAD_PAD - QK_DIM), (0, 0)))
    w_in_pt = jnp.concatenate([w_in_t[:o3], rope_rows, w_in_t[o4:]], axis=0)
    w_uq_p = _pad_heads(w["w_uq"], heads, QK_DIM)
    kv = w["w_ukv"].reshape(KV_RANK, heads, QK_NOPE + V_DIM)
    w_k = jnp.pad(kv[:, :, :QK_NOPE], ((0, 0), (0, 0), (0, HEAD_PAD - QK_NOPE))).reshape(KV_RANK, heads * HEAD_PAD)
    w_v = jnp.pad(kv[:, :, QK_NOPE:], ((0, 0), (0, 0), (0, HEAD_PAD - V_DIM))).reshape(KV_RANK, heads * HEAD_PAD)
    return dict(w_in_pt=w_in_pt, w_uq_p=w_uq_p, w_kv_p=jnp.concatenate([w_k, w_v], axis=1))


def _late_layouts(w, heads):
    d = w["w_pb"].shape[1]
    w_pb_p = jnp.pad(w["w_pb"].reshape(heads, V_DIM, d), ((0, 0), (0, HEAD_PAD - V_DIM), (0, 0))).reshape(heads * HEAD_PAD, d)
    return dict(w_pa=w["w_pa"], w_pb_p=w_pb_p, w_o=w["w_o"], w_gate_t=w["w_gate"], w_up_t=w["w_up"], w_down=w["w_down"])


def _early_grad_layouts(g, heads):
    o3 = POOL_WIDTH + Q_RANK + KV_RANK
    gin = g["w_in_pt"]
    w_in = jnp.concatenate([gin[:o3], gin[o3 + QK_NOPE:o3 + QK_DIM], gin[o3 + HEAD_PAD:]], axis=0)
    hw = heads * HEAD_PAD
    gk = g["w_kv_p"][:, :hw].reshape(KV_RANK, heads, HEAD_PAD)[:, :, :QK_NOPE]
    gv = g["w_kv_p"][:, hw:].reshape(KV_RANK, heads, HEAD_PAD)[:, :, :V_DIM]
    w_ukv = jnp.concatenate([gk, gv], axis=2).reshape(KV_RANK, heads * (QK_NOPE + V_DIM))
    return dict(w_in=w_in, w_uq=_unpad_heads(g["w_uq_p"], heads, QK_DIM), w_ukv=w_ukv)


def _late_grad_layouts(g, heads):
    d = g["w_pb_p"].shape[1]
    w_pb = g["w_pb_p"].reshape(heads, HEAD_PAD, d)[:, :V_DIM].reshape(heads * V_DIM, d)
    return dict(w_pa=g["w_pa"], w_pb=w_pb, w_o=g["w_o"], w_gate=g["w_gate_t"], w_up=g["w_up_t"], w_down=g["w_down"])


def _rope_tables(lp, b):
    inv = 1.0 / (ROPE_THETA ** (jnp.arange(0, QK_ROPE, 2, dtype=F32) / QK_ROPE))
    ang = jnp.arange(lp, dtype=F32)[:, None] * inv[None, :]
    cos, sin = jnp.cos(ang), jnp.sin(ang)
    half = QK_ROPE // 2
    ones = jnp.ones((lp, QK_NOPE), F32)
    zeros_lo = jnp.zeros((lp, QK_NOPE), F32)
    zeros_hi = jnp.zeros((lp, HEAD_PAD - QK_DIM), F32)
    zeros_half = jnp.zeros((lp, half), F32)
    c = jnp.concatenate([ones, cos, cos, zeros_hi], axis=1)
    sa = jnp.concatenate([zeros_lo, zeros_half, sin, zeros_hi], axis=1)
    sb = jnp.concatenate([zeros_lo, -sin, zeros_half, zeros_hi], axis=1)
    return tuple(jnp.tile(tab, (b, 1)) for tab in (c, sa, sb))


def _local_step(x, loss_target, meta_tokens, small, early_first, riding, early_first_codec):
    b, seq, d = x.shape
    depth = 2
    heads = early_first["w_uq"].shape[1] // QK_DIM
    core = lax.axis_index("c")
    chip = 2 * lax.axis_index("x") + lax.axis_index("y")
    real_len = N_META + seq
    lp = _round_up(real_len, SEQ_PAD)
    t = b * lp
    pad = lp - N_META - seq

    meta = jnp.broadcast_to(meta_tokens[None], (b, N_META, d))
    h = jnp.concatenate([meta, x, jnp.zeros((b, pad, d), F32)], axis=1).reshape(t, d)
    target = jnp.pad(loss_target, ((0, 0), (N_META, pad), (0, 0))).reshape(t, d)
    pos = jnp.arange(lp)
    valid = jnp.tile(((pos >= N_META) & (pos < N_META + seq)).astype(F32), b).reshape(t, 1)
    rope_c, rope_sa, rope_sb = _rope_tables(lp, b)

    layers = []
    for li in range(depth):
        lay = dict(pool_w=small["pool_w"][li].astype(BF16), pool_scale=small["pool_scale"][li][None])
        for n in ("norm_mix_g", "q_norm_g", "kv_norm_g", "norm_ffn_g"):
            lay[n] = small[n][li][None]
        layers.append(lay)
    layers[0].update(_early_layouts(early_first, heads))

    saved = []
    for li in range(depth):
        lay = layers[li]
        z = _in_proj_fwd(h, lay["norm_mix_g"], lay["w_in_pt"])
        a = _pool_fwd(z.reshape(b, lp, -1), lay["pool_w"], lay["pool_scale"]).reshape(t, POOL_WIDTH)
        q, k, v = _qkv_fwd(z, lay["q_norm_g"], lay["kv_norm_g"], lay["w_uq_p"], lay["w_kv_p"], rope_c, rope_sa, rope_sb)
        hw = q.shape[1]
        packed = riding[li]["packed"]
        o3, lse, others = _attn_fwd(q.reshape(b, lp, hw), k.reshape(b, lp, hw), v.reshape(b, lp, hw), real_len,
                                    _gather_exchange(packed))
        arrived = riding[li]["unpack"](lax.dynamic_update_index_in_dim(others, packed, chip, 0))
        lay.update(_late_layouts({n: arrived[n, li] for n in LATE_WEIGHTS}, heads))
        if li + 1 < depth:
            layers[li + 1].update(_early_layouts({n: arrived[n, li + 1] for n in EARLY_WEIGHTS}, heads))
        o = o3.reshape(t, hw)
        h1, pa, pb = _merge_fwd(h, z, a, o, lay["w_pa"], lay["w_pb_p"], lay["w_o"])
        h2, fa, fb = _ffn_fwd(h1, lay["norm_ffn_g"], lay["w_gate_t"], lay["w_up_t"], lay["w_down"])
        saved.append(dict(h=h, z=z, a=a, q=q, k=k, v=v, o=o, lse=lse, pa=pa, pb=pb, h1=h1, fa=fa, fb=fb))
        h = h2

    dh, loss, d_final = _loss_head(h, small["final_norm_g"][None], target, valid)

    g_small = {n: [] for n in SMALL_WEIGHTS if n != "final_norm_g"}
    early_grads, parts = {}, [None] * depth
    for li in reversed(range(depth)):
        lay, sv = layers[li], saved[li]
        hw = sv["q"].shape[1]
        dh1, hn_f, act, dfa, dfb, dg_ffn = _ffn_bwd(sv["h1"], dh, sv["fa"], sv["fb"], lay["norm_ffn_g"],
                                                     lay["w_gate_t"], lay["w_up_t"], lay["w_down"])
        gl = dict(w_gate_t=_weight_grad(dfa, hn_f, "grad_w_gate"), w_up_t=_weight_grad(dfb, hn_f, "grad_w_up"),
                  w_down=_weight_grad(act, dh, "grad_w_down"))
        merged, dpa, dpb, dga, dgb, da, do = _merge_bwd(dh1, sv["z"], sv["pa"], sv["pb"], lay["w_o"], lay["w_pa"], lay["w_pb_p"])
        gl["w_o"] = _weight_grad(merged, dh1, "grad_w_o")
        gl["w_pa"] = _weight_grad(sv["a"], dpa, "grad_w_pa")
        gl["w_pb_p"] = _weight_grad(sv["o"], dpb, "grad_w_pb")
        to_send = {(n, li): g for n, g in _late_grad_layouts(gl, heads).items()}
        if li + 1 < depth:
            to_send.update({(n, li + 1): g for n, g in early_grads[li + 1].items()})
        sending = riding[li]["pack_grads"](to_send)
        shape3 = (b, lp, hw)
        dq3, dk3, dv3, from_others = _attn_bwd(
            sv["q"].reshape(shape3), sv["k"].reshape(shape3), sv["v"].reshape(shape3), sv["o"].reshape(shape3),
            do.reshape(shape3), sv["lse"], real_len, _scatter_exchange(sending))
        own = lax.dynamic_index_in_dim(lax.dynamic_index_in_dim(sending, chip, 0, keepdims=False), core, 0, keepdims=False)
        parts[li] = lax.dynamic_update_index_in_dim(from_others, own, 2 * chip + core, 0)
        dqraw, dkvraw, cqn, ckvn, dcq, dckv, dkr, dg_q, dg_kv = _qkv_bwd(
            dq3.reshape(t, hw), dk3.reshape(t, hw), dv3.reshape(t, hw), sv["z"], lay["q_norm_g"], lay["kv_norm_g"],
            lay["w_uq_p"], lay["w_kv_p"], rope_c, rope_sa, rope_sb)
        gl["w_uq_p"] = _weight_grad(cqn, dqraw, "grad_w_uq")
        gl["w_kv_p"] = _weight_grad(ckvn, dkvraw, "grad_w_ukv")
        du3, dpool_w, dpool_scale = _pool_bwd(sv["z"].reshape(b, lp, -1), da.reshape(b, lp, POOL_WIDTH),
                                              lay["pool_w"], lay["pool_scale"])
        dh, hn_m, dz, dg_mix = _in_proj_bwd(sv["h"], dh1, du3.reshape(t, POOL_WIDTH), dcq, dckv, dkr, dga, dgb,
                                            lay["norm_mix_g"], lay["w_in_pt"])
        gl["w_in_pt"] = _weight_grad(dz, hn_m, "grad_w_in")
        early_grads[li] = _early_grad_layouts(gl, heads)
        for n, val in (("norm_mix_g", dg_mix[0]), ("pool_w", dpool_w), ("pool_scale", dpool_scale[0]), ("q_norm_g", dg_q[0]),
                       ("kv_norm_g", dg_kv[0]), ("norm_ffn_g", dg_ffn[0])):
            g_small[n].insert(0, val)

    dh3 = dh.reshape(b, lp, d)
    grad_x = dh3[:, N_META:N_META + seq]
    d_meta_rows = dh3[:, :N_META]
    g_small = {n: jnp.stack(v) for n, v in g_small.items()}
    g_small["final_norm_g"] = d_final[0]
    early_first_partial = early_first_codec["pack_grads"]({(n, 0): g for n, g in early_grads[0].items()})
    return loss, grad_x, d_meta_rows, g_small, early_first_partial, parts


def _mesh_place():
    x, y, c = lax.axis_index("x"), lax.axis_index("y"), lax.axis_index("c")
    others = [(1 - x, y), (x, 1 - y), (1 - x, 1 - y)]
    return x, y, c, 2 * x + y, others


def _remote(src, dst, send_sems, recv_sems, k, device):
    return pltpu.make_async_remote_copy(src_ref=src, dst_ref=dst, send_sem=send_sems.at[k], recv_sem=recv_sems.at[k],
                                        device_id=device, device_id_type=MESH)


def _gather_exchange(packed):
    def copy(p_ref, g_ref, send_sems, recv_sems, r, slot):
        _, _, c, _, others = _mesh_place()
        ox, oy = others[r]
        return _remote(p_ref, g_ref.at[slot], send_sems, recv_sems, r, (ox, oy, c))

    def start(p_ref, g_ref, send_sems, recv_sems):
        chip = _mesh_place()[3]
        for r in range(3):
            copy(p_ref, g_ref, send_sems, recv_sems, r, chip).start()

    def finish(p_ref, g_ref, send_sems, recv_sems):
        _, _, _, chip, others = _mesh_place()
        for r, (ox, oy) in enumerate(others):
            copy(p_ref, g_ref, send_sems, recv_sems, r, 2 * ox + oy).wait_recv()
        for r in range(3):
            copy(p_ref, g_ref, send_sems, recv_sems, r, chip).wait_send()

    return dict(name="gather", operand=packed, copies=3, start=start, finish=finish,
                out_shape=jax.ShapeDtypeStruct((N_CHIPS,) + packed.shape, packed.dtype))


def _scatter_exchange(parts):
    flips = [(dx, dy, dc) for dx in (0, 1) for dy in (0, 1) for dc in (0, 1)][1:]

    def copy(p_ref, got_ref, send_sems, recv_sems, k, arriving):
        x, y, c, _, _ = _mesh_place()
        dx, dy, dc = flips[k]
        tx, ty, tc = (1 - x if dx else x), (1 - y if dy else y), (1 - c if dc else c)
        slot = 4 * tx + 2 * ty + tc if arriving else 4 * x + 2 * y + c
        return _remote(p_ref.at[2 * tx + ty, tc], got_ref.at[slot], send_sems, recv_sems, k, (tx, ty, tc))

    def start(p_ref, got_ref, send_sems, recv_sems):
        for k in range(len(flips)):
            copy(p_ref, got_ref, send_sems, recv_sems, k, False).start()

    def finish(p_ref, got_ref, send_sems, recv_sems):
        for k in range(len(flips)):
            copy(p_ref, got_ref, send_sems, recv_sems, k, True).wait_recv()
        for k in range(len(flips)):
            copy(p_ref, got_ref, send_sems, recv_sems, k, False).wait_send()

    return dict(name="scatter", operand=parts, copies=len(flips), start=start, finish=finish,
                out_shape=jax.ShapeDtypeStruct((2 * N_CHIPS,) + parts.shape[2:], parts.dtype))


def _all_gather_shards(packed, meta_shard):
    _, rh, cols = packed.shape

    def body(p_ref, m_ref, g_ref, gm_ref, send_sems, recv_sems):
        x, y, c, chip, others = _mesh_place()
        sibling = (x, y, 1 - c)
        sends = []
        for r, (ox, oy) in enumerate(others):
            sends.append(_remote(p_ref.at[c], g_ref.at[chip, c], send_sems, recv_sems, r, (ox, oy, c)))
            sends.append(_remote(m_ref, gm_ref.at[chip], send_sems, recv_sems, 6 + r, (ox, oy, c)))
        for cp in sends:
            cp.start()
        for r, (ox, oy) in enumerate(others):
            src_chip = 2 * ox + oy
            _remote(p_ref.at[c], g_ref.at[src_chip, c], send_sems, recv_sems, r, (ox, oy, c)).wait_recv()
            passed = _remote(g_ref.at[src_chip, c], g_ref.at[src_chip, c], send_sems, recv_sems, 3 + r, sibling)
            passed.start()
            sends.append(passed)
        for r, (ox, oy) in enumerate(others):
            src_chip = 2 * ox + oy
            _remote(p_ref.at[c], g_ref.at[src_chip, 1 - c], send_sems, recv_sems, 3 + r, sibling).wait_recv()
            _remote(m_ref, gm_ref.at[src_chip], send_sems, recv_sems, 6 + r, (ox, oy, c)).wait_recv()
        for cp in sends:
            cp.wait_send()

    gathered, meta_all = pl.pallas_call(
        body, name="all_gather_shards",
        in_specs=[_any_spec(), _any_spec()], out_specs=[_any_spec(), _any_spec()],
        out_shape=[jax.ShapeDtypeStruct((N_CHIPS, 2, rh, cols), packed.dtype),
                   jax.ShapeDtypeStruct((N_CHIPS,) + meta_shard.shape, meta_shard.dtype)],
        scratch_shapes=[pltpu.SemaphoreType.DMA((9,)), pltpu.SemaphoreType.DMA((9,))],
    )(packed, meta_shard)
    chip = 2 * lax.axis_index("x") + lax.axis_index("y")
    return (lax.dynamic_update_index_in_dim(gathered, packed, chip, 0),
            lax.dynamic_update_index_in_dim(meta_all, meta_shard, chip, 0))


def _pair_exchange(give):
    def body(give_ref, got_ref, send_sems, recv_sems):
        x, y, c, _, _ = _mesh_place()
        cp = _remote(give_ref, got_ref, send_sems, recv_sems, 0, (x, y, 1 - c))
        cp.start()
        cp.wait()

    return pl.pallas_call(
        body, name="pair_exchange", in_specs=[_any_spec()], out_specs=_any_spec(),
        out_shape=jax.ShapeDtypeStruct(give.shape, give.dtype),
        scratch_shapes=[pltpu.SemaphoreType.DMA((1,)), pltpu.SemaphoreType.DMA((1,))],
    )(give)


def _chip_exchange(parts):
    def body(p_ref, got_ref, send_sems, recv_sems):
        _, _, c, chip, others = _mesh_place()
        sends = [_remote(p_ref.at[2 * ox + oy], got_ref.at[chip], send_sems, recv_sems, r, (ox, oy, c))
                 for r, (ox, oy) in enumerate(others)]
        for cp in sends:
            cp.start()
        for r, (ox, oy) in enumerate(others):
            _remote(p_ref.at[chip], got_ref.at[2 * ox + oy], send_sems, recv_sems, r, (ox, oy, c)).wait_recv()
        for cp in sends:
            cp.wait_send()

    got = pl.pallas_call(
        body, name="chip_exchange", in_specs=[_any_spec()], out_specs=_any_spec(),
        out_shape=jax.ShapeDtypeStruct(parts.shape, parts.dtype),
        scratch_shapes=[pltpu.SemaphoreType.DMA((3,)), pltpu.SemaphoreType.DMA((3,))],
    )(parts)
    chip = 2 * lax.axis_index("x") + lax.axis_index("y")
    own = lax.dynamic_index_in_dim(parts, chip, 0, keepdims=False)
    return lax.dynamic_update_index_in_dim(got, own, chip, 0)


def _pair_gather(half):
    def body(h_ref, out_ref, send_sems, recv_sems):
        x, y, c, _, _ = _mesh_place()
        cp = _remote(h_ref, out_ref.at[c], send_sems, recv_sems, 0, (x, y, 1 - c))
        cp.start()
        _remote(h_ref, out_ref.at[1 - c], send_sems, recv_sems, 0, (x, y, 1 - c)).wait_recv()
        cp.wait_send()

    both = pl.pallas_call(
        body, name="pair_gather", in_specs=[_any_spec()], out_specs=_any_spec(),
        out_shape=jax.ShapeDtypeStruct((2,) + half.shape, half.dtype),
        scratch_shapes=[pltpu.SemaphoreType.DMA((1,)), pltpu.SemaphoreType.DMA((1,))],
    )(half)
    return lax.dynamic_update_index_in_dim(both, half, lax.axis_index("c"), 0)


def _row_tile(rows, limit=PACK_TILE):
    if rows <= limit:
        return rows
    for tr in range(limit, 7, -8):
        if rows % tr == 0:
            return tr
    return rows


def _pair_add(keep, got):
    n, rh, cols = keep.shape
    tr = _row_tile(rh)

    def body(k_ref, g_ref, o_ref):
        o_ref[...] = (k_ref[...].astype(F32) + g_ref[...].astype(F32)).astype(BF16)

    spec = pl.BlockSpec((1, tr, cols), lambda j, i: (j, i, 0))
    return pl.pallas_call(
        body, name="pair_add", grid=(n, rh // tr), in_specs=[spec, spec], out_specs=spec,
        out_shape=jax.ShapeDtypeStruct(keep.shape, BF16),
        compiler_params=_params(("parallel", "parallel")),
    )(keep, got)


def _chip_sum(parts):
    n, rh, cols = parts.shape
    tr = _row_tile(rh)

    def body(p_ref, o_ref):
        total = p_ref[0].astype(F32)
        for k in range(1, n):
            total = total + p_ref[k].astype(F32)
        o_ref[...] = total

    return pl.pallas_call(
        body, name="chip_sum", grid=(rh // tr,),
        in_specs=[pl.BlockSpec((n, tr, cols), lambda i: (0, i, 0))], out_specs=pl.BlockSpec((tr, cols), lambda i: (i, 0)),
        out_shape=jax.ShapeDtypeStruct((rh, cols), F32),
        compiler_params=_params(("parallel",)),
    )(parts)


def _reduce_scatter(grads, c):
    keep = lax.dynamic_index_in_dim(grads, c, axis=1, keepdims=False)
    give = lax.dynamic_index_in_dim(grads, 1 - c, axis=1, keepdims=False)
    chip_partial = _pair_add(keep, _pair_exchange(give))
    return _pair_gather(_chip_sum(_chip_exchange(chip_partial)))


def _all_reduce_small(meta_rows, small):
    b, rm, cols = meta_rows.shape
    rows = rm + small.shape[0]

    def body(meta_ref, small_ref, out_ref, mine, pair_buf, chip_buf, send_sems, recv_sems):
        x, y, c, chip, others = _mesh_place()
        acc = meta_ref[0]
        for i in range(1, b):
            acc = acc + meta_ref[i]
        mine[0:rm, :] = acc
        mine[rm:rows, :] = small_ref[...]
        pair = _remote(mine, pair_buf, send_sems, recv_sems, 0, (x, y, 1 - c))
        pair.start()
        pair.wait()
        chip_buf[chip] = mine[...] + pair_buf[...]
        sends = [_remote(chip_buf.at[chip], chip_buf.at[chip], send_sems, recv_sems, 1 + r, (ox, oy, c))
                 for r, (ox, oy) in enumerate(others)]
        for cp in sends:
            cp.start()
        for r, (ox, oy) in enumerate(others):
            _remote(chip_buf.at[chip], chip_buf.at[2 * ox + oy], send_sems, recv_sems, 1 + r, (ox, oy, c)).wait_recv()
        for cp in sends:
            cp.wait_send()
        out_ref[...] = ((chip_buf[0] + chip_buf[1]) + chip_buf[2]) + chip_buf[3]

    return pl.pallas_call(
        body, name="all_reduce_small",
        in_specs=[_vmem_spec(), _vmem_spec()], out_specs=_vmem_spec(),
        out_shape=jax.ShapeDtypeStruct((rows, cols), F32),
        scratch_shapes=[pltpu.VMEM((rows, cols), F32), pltpu.VMEM((rows, cols), F32), pltpu.VMEM((N_CHIPS, rows, cols), F32),
                        pltpu.SemaphoreType.DMA((4,)), pltpu.SemaphoreType.DMA((4,))],
        compiler_params=pltpu.CompilerParams(vmem_limit_bytes=VMEM_LIMIT),
    )(meta_rows, small)


def _adamw(w, g, m, v):
    shape = w.shape
    cols = shape[-1]
    rows = w.size // cols
    tr = _row_tile(rows)

    def body(w_ref, g_ref, m_ref, v_ref, d_ref, m2_ref, v2_ref):
        grad = g_ref[...]
        m2 = ADAM_B1 * m_ref[...] + (1.0 - ADAM_B1) * grad
        v2 = ADAM_B2 * v_ref[...] + (1.0 - ADAM_B2) * jnp.square(grad)
        m_hat = m2 / (1.0 - ADAM_B1 ** ADAM_STEP)
        v_hat = v2 / (1.0 - ADAM_B2 ** ADAM_STEP)
        d_ref[...] = -ADAM_LR * (m_hat / (jnp.sqrt(v_hat) + ADAM_EPS) + ADAM_WD * w_ref[...])
        m2_ref[...] = m2
        v2_ref[...] = v2

    spec = pl.BlockSpec((tr, cols), lambda i: (i, 0))
    out = jax.ShapeDtypeStruct((rows, cols), F32)
    res = pl.pallas_call(
        body, name="adamw", grid=(rows // tr,), in_specs=[spec] * 4, out_specs=[spec] * 3, out_shape=[out] * 3,
        compiler_params=_params(("parallel",)),
    )(*(a.reshape(rows, cols) for a in (w, g, m, v)))
    return tuple(r.reshape(shape) for r in res)


def _pack_rows(arrays):
    flat = [a.reshape(-1, PACK_COLS) for a in arrays]
    counts = [f.shape[0] for f in flat]
    total = sum(counts)
    half = -(-total // 2)
    tiles = -(-half // PACK_TILE)
    padded = 2 * tiles * _round_up(-(-half // tiles), 16)
    if padded > total:
        flat.append(jnp.zeros((padded - total, PACK_COLS), flat[0].dtype))
    return jnp.concatenate(flat, axis=0), counts


def _unpack_rows(buffer, counts, shapes):
    out, start = [], 0
    for n, shape in zip(counts, shapes):
        out.append(buffer[..., start:start + n, :].reshape(buffer.shape[:-2] + tuple(shape)))
        start += n
    return out


def _group_codec(entries, weights):
    turned = [n in TRANSPOSED_WEIGHTS for n, _ in entries]
    shapes = [weights[n].shape[1:][::-1] if t else weights[n].shape[1:] for (n, _), t in zip(entries, turned)]
    by_rows = [t or SHARD_AXIS[n] == 1 for (n, _), t in zip(entries, turned)]
    packed, counts = _pack_rows([(weights[n][li].T if t else weights[n][li]).astype(BF16) for (n, li), t in zip(entries, turned)])
    rows = packed.shape[0]
    pad_rows = rows - sum(counts)

    def unpack(per_chip_packed):
        out = {}
        for entry, (s0, s1), rowwise, blk in zip(entries, shapes, by_rows, _unpack_rows(per_chip_packed, counts, shapes)):
            out[entry] = blk.reshape(N_CHIPS * s0, s1) if rowwise else jnp.transpose(blk, (1, 0, 2)).reshape(s0, N_CHIPS * s1)
        return out

    def pack_grads(whole):
        pieces = []
        for entry, (s0, s1), rowwise in zip(entries, shapes, by_rows):
            g = whole[entry]
            by_chip = g.reshape(N_CHIPS, s0, s1) if rowwise else jnp.transpose(g.reshape(s0, N_CHIPS, s1), (1, 0, 2))
            pieces.append(by_chip.reshape(N_CHIPS, -1, PACK_COLS))
        if pad_rows:
            pieces.append(jnp.zeros((N_CHIPS, pad_rows, PACK_COLS), BF16))
        return jnp.concatenate(pieces, axis=1).reshape(N_CHIPS, 2, rows // 2, PACK_COLS)

    def unpack_reduced(reduced):
        shards = _unpack_rows(reduced.reshape(rows, PACK_COLS), counts, shapes)
        return {entry: s.T if t else s for entry, s, t in zip(entries, shards, turned)}

    return dict(packed=packed, unpack=unpack, pack_grads=pack_grads, unpack_reduced=unpack_reduced)


def kernel(x, meta_tokens, norm_mix_g, w_in, pool_w, pool_scale, q_norm_g, kv_norm_g, w_uq, w_ukv, w_pa, w_pb, w_o, norm_ffn_g, w_gate, w_up, w_down, final_norm_g, loss_target, m_meta_tokens, m_norm_mix_g, m_w_in, m_pool_w, m_pool_scale, m_q_norm_g, m_kv_norm_g, m_w_uq, m_w_ukv, m_w_pa, m_w_pb, m_w_o, m_norm_ffn_g, m_w_gate, m_w_up, m_w_down, m_final_norm_g, v_meta_tokens, v_norm_mix_g, v_w_in, v_pool_w, v_pool_scale, v_q_norm_g, v_kv_norm_g, v_w_uq, v_w_ukv, v_w_pa, v_w_pb, v_w_o, v_norm_ffn_g, v_w_gate, v_w_up, v_w_down, v_final_norm_g):
    weights = dict(meta_tokens=meta_tokens, norm_mix_g=norm_mix_g, w_in=w_in, pool_w=pool_w, pool_scale=pool_scale,
                   q_norm_g=q_norm_g, kv_norm_g=kv_norm_g, w_uq=w_uq, w_ukv=w_ukv, w_pa=w_pa, w_pb=w_pb, w_o=w_o,
                   norm_ffn_g=norm_ffn_g, w_gate=w_gate, w_up=w_up, w_down=w_down, final_norm_g=final_norm_g)
    first = dict(meta_tokens=m_meta_tokens, norm_mix_g=m_norm_mix_g, w_in=m_w_in, pool_w=m_pool_w, pool_scale=m_pool_scale,
                 q_norm_g=m_q_norm_g, kv_norm_g=m_kv_norm_g, w_uq=m_w_uq, w_ukv=m_w_ukv, w_pa=m_w_pa, w_pb=m_w_pb, w_o=m_w_o,
                 norm_ffn_g=m_norm_ffn_g, w_gate=m_w_gate, w_up=m_w_up, w_down=m_w_down, final_norm_g=m_final_norm_g)
    second = dict(meta_tokens=v_meta_tokens, norm_mix_g=v_norm_mix_g, w_in=v_w_in, pool_w=v_pool_w, pool_scale=v_pool_scale,
                  q_norm_g=v_q_norm_g, kv_norm_g=v_kv_norm_g, w_uq=v_w_uq, w_ukv=v_w_ukv, w_pa=v_w_pa, w_pb=v_w_pb, w_o=v_w_o,
                  norm_ffn_g=v_norm_ffn_g, w_gate=v_w_gate, w_up=v_w_up, w_down=v_w_down, final_norm_g=v_final_norm_g)
    core = lax.axis_index("c")
    chip = 2 * lax.axis_index("x") + lax.axis_index("y")
    d = x.shape[-1]
    meta_cols = meta_tokens.shape[1]

    early_first = _group_codec([(n, 0) for n in EARLY_WEIGHTS], weights)
    riding = [_group_codec([(n, 0) for n in LATE_WEIGHTS] + [(n, 1) for n in EARLY_WEIGHTS], weights),
              _group_codec([(n, 1) for n in LATE_WEIGHTS], weights)]
    gathered, meta_all = _all_gather_shards(early_first["packed"].reshape(2, -1, PACK_COLS), meta_tokens)
    early_weights = early_first["unpack"](gathered.reshape(N_CHIPS, -1, PACK_COLS))
    meta_full = jnp.concatenate([meta_all[j] for j in range(N_CHIPS)], axis=1)
    small = {n: weights[n] for n in SMALL_WEIGHTS}

    loss, grad_x, d_meta_rows, g_small, early_partial, parts = _local_step(
        x, loss_target, meta_full, small, {n: early_weights[n, 0] for n in EARLY_WEIGHTS}, riding, early_first)

    shards = early_first["unpack_reduced"](_reduce_scatter(early_partial, core))
    for codec, from_all in zip(riding, parts):
        shards.update(codec["unpack_reduced"](_pair_gather(_chip_sum(from_all))))
    grads = {n: jnp.stack([shards[n, 0], shards[n, 1]]) for n in BIG_WEIGHTS}

    small_shapes = [weights[n].shape for n in SMALL_WEIGHTS]
    small_flat = jnp.concatenate([g_small[n].reshape(-1) for n in SMALL_WEIGHTS])
    small_len = small_flat.shape[0]
    small_rows = _round_up(-(-small_len // PACK_COLS), 8)
    small_pack = jnp.pad(small_flat, (0, small_rows * PACK_COLS - small_len)).reshape(small_rows, PACK_COLS)
    meta_rows = N_META * d // PACK_COLS
    summed = _all_reduce_small(d_meta_rows.reshape(-1, meta_rows, PACK_COLS), small_pack)
    grad_meta_full = summed[:meta_rows].reshape(N_META, d)
    grads["meta_tokens"] = lax.dynamic_slice_in_dim(grad_meta_full, chip * meta_cols, meta_cols, axis=1)
    small_sum = summed[meta_rows:].reshape(-1)
    start = 0
    for n, shape in zip(SMALL_WEIGHTS, small_shapes):
        size = 1
        for s in shape:
            size *= s
        grads[n] = small_sum[start:start + size].reshape(shape)
        start += size

    deltas, new_m, new_v = {}, {}, {}
    for n in WEIGHT_ORDER:
        deltas[n], new_m[n], new_v[n] = _adamw(weights[n], grads[n], first[n], second[n])

    total_loss = lax.psum(loss[0, 0], ("x", "y", "c"))
    return (total_loss, grad_x, *[grads[n] for n in WEIGHT_ORDER], *[deltas[n] for n in WEIGHT_ORDER],
            *[new_m[n] for n in WEIGHT_ORDER], *[new_v[n] for n in WEIGHT_ORDER])
```

```python
import functools

import jax
import jax.numpy as jnp
from jax import lax
from jax.experimental import pallas as pl
from jax.experimental.pallas import tpu as pltpu

F32 = jnp.float32
BF16 = jnp.bfloat16

N_META = 16
POOL_WINDOWS = (2, 4, 8, 16)
POOL_GROUP = 128
POOL_WIDTH = POOL_GROUP * len(POOL_WINDOWS)
QK_NOPE = 64
QK_ROPE = 32
V_DIM = 64
QK_DIM = QK_NOPE + QK_ROPE
Q_RANK = 256
KV_RANK = 128
HEAD_PAD = 128
SM_SCALE = QK_DIM ** -0.5
ROPE_THETA = 10000.0
NORM_EPS = 1e-6
MASK_VALUE = -1e30
Z_FIXED = POOL_WIDTH + Q_RANK + KV_RANK + HEAD_PAD

ADAM_LR = 0.001
ADAM_B1 = 0.9
ADAM_B2 = 0.999
ADAM_EPS = 1e-08
ADAM_WD = 0.01
ADAM_STEP = 10

N_CHIPS = 4
ATT_BLOCK = 256
SEQ_PAD = 128
ATT_Q_ROWS = 256
ATT_FWD_HEADS = 8
ATT_BWD_HEADS = 4
PACK_COLS = 1024
PACK_TILE = 512
VMEM_LIMIT = 60 * 1024 * 1024
MXU_DEPTH = 256
ACC_BYTES = 8 * 1024 * 1024

MESH = pl.DeviceIdType.MESH

BIG_WEIGHTS = ("w_in", "w_uq", "w_ukv", "w_pa", "w_pb", "w_o", "w_gate", "w_up", "w_down")
EARLY_WEIGHTS = ("w_in", "w_uq", "w_ukv")
LATE_WEIGHTS = ("w_pa", "w_pb", "w_o", "w_gate", "w_up", "w_down")
TRANSPOSED_WEIGHTS = ("w_in", "w_gate", "w_up")
SHARD_AXIS = {"w_in": 2, "w_uq": 2, "w_ukv": 2, "w_pa": 2, "w_pb": 1, "w_o": 1, "w_gate": 2, "w_up": 2, "w_down": 1}
SMALL_WEIGHTS = ("norm_mix_g", "pool_w", "pool_scale", "q_norm_g", "kv_norm_g", "norm_ffn_g", "final_norm_g")
WEIGHT_ORDER = ("meta_tokens", "norm_mix_g", "w_in", "pool_w", "pool_scale", "q_norm_g", "kv_norm_g", "w_uq", "w_ukv",
                "w_pa", "w_pb", "w_o", "norm_ffn_g", "w_gate", "w_up", "w_down", "final_norm_g")


def _round_up(n, m):
    return -(-n // m) * m


def _vmem_spec():
    return pl.BlockSpec(memory_space=pltpu.VMEM)


def _any_spec():
    return pl.BlockSpec(memory_space=pl.ANY)


def _row_block(tm, width, col_block=0):
    return pl.BlockSpec((tm, width), lambda i, cb=col_block: (i, cb))


def _params(sem, vmem=VMEM_LIMIT):
    return pltpu.CompilerParams(dimension_semantics=sem, vmem_limit_bytes=vmem)


def _token_tile(t, want):
    best = SEQ_PAD
    for tm in range(32, min(t, 2 * want) + 1, 32):
        if t % tm == 0 and abs(tm - want) < abs(best - want):
            best = tm
    return best


def _dot(a, b):
    return jnp.dot(a, b, preferred_element_type=F32)


def _dot_nt(a, b):
    return lax.dot_general(a, b, (((1,), (1,)), ((), ())), preferred_element_type=F32)


def _dot_tn(a, b):
    return lax.dot_general(a, b, (((0,), (0,)), ((), ())), preferred_element_type=F32)


def _rms_fwd(x, g):
    r = lax.rsqrt(jnp.mean(x * x, axis=-1, keepdims=True) + NORM_EPS)
    xh = x * r
    return xh * g, xh, r


def _rms_bwd(dy, xh, r, g):
    gdy = dy * g
    dx = r * (gdy - xh * jnp.mean(xh * gdy, axis=-1, keepdims=True))
    return dx, dy * xh


def _rope_fwd(x, c, sa, sb):
    return x * c + pltpu.roll(x, 16, 1) * sa + pltpu.roll(x, HEAD_PAD - 16, 1) * sb


def _rope_bwd(d, c, sa, sb):
    return d * c + pltpu.roll(d * sa, HEAD_PAD - 16, 1) + pltpu.roll(d * sb, 16, 1)


def _in_proj_fwd(h, g, w_in_pt):
    t, d = h.shape
    nz = w_in_pt.shape[0]
    tm = _token_tile(t, 512)

    def body(h_ref, g_ref, w_ref, z_ref):
        hn, _, _ = _rms_fwd(h_ref[...], g_ref[...])
        z_ref[...] = _dot_nt(hn.astype(BF16), w_ref[...]).astype(BF16)

    return pl.pallas_call(
        body, name="in_proj_fwd", grid=(t // tm,),
        in_specs=[_row_block(tm, d), _vmem_spec(), _vmem_spec()],
        out_specs=_row_block(tm, nz),
        out_shape=jax.ShapeDtypeStruct((t, nz), BF16),
        compiler_params=_params(("parallel",)),
    )(h, g, w_in_pt)


def _window_sum(x, w, row, forward):
    n = x.shape[0]
    s = x
    k = 1
    while k < w:
        if forward:
            s = s + jnp.where(row >= k, pltpu.roll(s, k, 0), 0.0)
        else:
            s = s + jnp.where(row < n - k, pltpu.roll(s, n - k, 0), 0.0)
        k *= 2
    return s


def _pool_fwd(z3, pool_w, pool_scale):
    b, lp, _ = z3.shape

    def body(u_ref, pw_ref, sc_ref, a_ref):
        row = lax.broadcasted_iota(jnp.int32, (lp, POOL_GROUP), 0)
        pos = row.astype(F32)
        for gi, w in enumerate(POOL_WINDOWS):
            cols = slice(gi * POOL_GROUP, (gi + 1) * POOL_GROUP)
            u = u_ref[0, :, cols].astype(F32)
            y = _window_sum(u, w, row, True) / jnp.minimum(pos + 1.0, float(w)) - u
            yw = _dot(y.astype(BF16), pw_ref[gi])
            a_ref[0, :, cols] = (yw * sc_ref[:, cols]).astype(BF16)

    return pl.pallas_call(
        body, name="pool_fwd", grid=(b,),
        in_specs=[pl.BlockSpec((1, lp, POOL_WIDTH), lambda i: (i, 0, 0)), _vmem_spec(), _vmem_spec()],
        out_specs=pl.BlockSpec((1, lp, POOL_WIDTH), lambda i: (i, 0, 0)),
        out_shape=jax.ShapeDtypeStruct((b, lp, POOL_WIDTH), BF16),
        compiler_params=_params(("parallel",)),
    )(z3, pool_w, pool_scale)


def _qkv_fwd(z, g_q, g_kv, w_uq_p, w_kv_p, rope_c, rope_sa, rope_sb):
    t = z.shape[0]
    hw = w_uq_p.shape[1]
    heads = hw // HEAD_PAD
    tm = _token_tile(t, 512)

    def body(cq_ref, ckv_ref, kr_ref, gq_ref, gkv_ref, wq_ref, wkv_ref, c_ref, sa_ref, sb_ref, q_ref, k_ref, v_ref):
        c, sa, sb = c_ref[...], sa_ref[...], sb_ref[...]
        cqn, _, _ = _rms_fwd(cq_ref[...].astype(F32), gq_ref[...])
        qraw = _dot(cqn.astype(BF16), wq_ref[...])
        ckvn, _, _ = _rms_fwd(ckv_ref[...].astype(F32), gkv_ref[...])
        kvraw = _dot(ckvn.astype(BF16), wkv_ref[...])
        kr = _rope_fwd(kr_ref[...].astype(F32), c, sa, sb)
        for hd in range(heads):
            cols = slice(hd * HEAD_PAD, (hd + 1) * HEAD_PAD)
            q_ref[:, cols] = (_rope_fwd(qraw[:, cols], c, sa, sb) * SM_SCALE).astype(BF16)
            k_ref[:, cols] = (kvraw[:, cols] + kr).astype(BF16)
        lane = lax.broadcasted_iota(jnp.int32, (tm, hw), 1)
        v_ref[...] = jnp.where((lane & (HEAD_PAD - 1)) == V_DIM, 1.0, kvraw[:, hw:]).astype(BF16)

    out = jax.ShapeDtypeStruct((t, hw), BF16)
    return pl.pallas_call(
        body, name="qkv_fwd", grid=(t // tm,),
        in_specs=[_row_block(tm, Q_RANK, POOL_WIDTH // Q_RANK),
                  _row_block(tm, KV_RANK, (POOL_WIDTH + Q_RANK) // KV_RANK),
                  _row_block(tm, HEAD_PAD, (POOL_WIDTH + Q_RANK + KV_RANK) // HEAD_PAD),
                  _vmem_spec(), _vmem_spec(), _vmem_spec(), _vmem_spec(),
                  _row_block(tm, HEAD_PAD), _row_block(tm, HEAD_PAD), _row_block(tm, HEAD_PAD)],
        out_specs=[_row_block(tm, hw)] * 3,
        out_shape=[out, out, out],
        compiler_params=_params(("parallel",)),
    )(z, z, z, g_q, g_kv, w_uq_p, w_kv_p, rope_c, rope_sa, rope_sb)


def _heads_per_step(heads, want):
    while heads % want:
        want //= 2
    return want


def _causal_mask(rows):
    row = lax.broadcasted_iota(jnp.int32, (rows, rows), 0)
    col = lax.broadcasted_iota(jnp.int32, (rows, rows), 1)
    return col <= row


def _attn_blocks(real_len):
    tail_start = (-(-real_len // ATT_BLOCK) - 1) * ATT_BLOCK
    return tail_start // ATT_BLOCK, tail_start, _round_up(real_len - tail_start, SEQ_PAD)


def _call_with_exchange(body, exchange, *, name, grid, in_specs, out_specs, out_shape, scratch_shapes, operands):
    if exchange is None:
        return pl.pallas_call(body, name=name, grid=grid, in_specs=in_specs, out_specs=out_specs, out_shape=out_shape,
                              scratch_shapes=scratch_shapes,
                              compiler_params=_params(("parallel",) + ("arbitrary",) * (len(grid) - 1)))(*operands)
    n_in, n_out, n_scratch = len(in_specs), len(out_specs), len(scratch_shapes)

    def riding(*refs):
        ins, src = refs[:n_in], refs[n_in]
        outs, dst = refs[n_in + 1:n_in + 1 + n_out], refs[n_in + 1 + n_out]
        scratch = refs[n_in + 2 + n_out:n_in + 2 + n_out + n_scratch]
        send_sems, recv_sems = refs[n_in + 2 + n_out + n_scratch:]
        steps = [pl.program_id(a) for a in range(len(grid))]

        last = functools.reduce(jnp.logical_and, [s == g - 1 for s, g in zip(steps, grid)])

        @pl.when(functools.reduce(jnp.logical_and, [s == 0 for s in steps]))
        def _():
            exchange["start"](src, dst, send_sems, recv_sems)

        if "pass_on" in exchange:
            @pl.when(last)
            def _():
                exchange["pass_on"](src, dst, send_sems, recv_sems)

        body(*ins, *outs, *scratch)

        @pl.when(last)
        def _():
            exchange["finish"](src, dst, send_sems, recv_sems)

    n = exchange["copies"]
    return pl.pallas_call(
        riding, name=name + "_" + exchange["name"], grid=grid,
        in_specs=list(in_specs) + [_any_spec()], out_specs=list(out_specs) + [_any_spec()],
        out_shape=list(out_shape) + [exchange["out_shape"]],
        scratch_shapes=list(scratch_shapes) + [pltpu.SemaphoreType.DMA((n,)), pltpu.SemaphoreType.DMA((n,))],
        compiler_params=_params(("arbitrary",) * len(grid)),
    )(*operands, exchange["operand"])


def _attn_fwd(q3, k3, v3, real_len, exchange=None):
    b, lp, hw = q3.shape
    heads = hw // HEAD_PAD
    tb = ATT_BLOCK
    nfull, tail_start, tail = _attn_blocks(real_len)
    done = tail_start + tail
    hpg = _heads_per_step(heads, ATT_FWD_HEADS)
    width = hpg * HEAD_PAD

    def body(q_ref, k_ref, v_ref, o_ref, lse_ref):
        group = pl.program_id(1)

        @pl.when(group == 0)
        def _():
            lse_ref[...] = jnp.zeros_like(lse_ref)

        def q_rows(r0, rows, whole_kv_blocks, back):
            def kv_step(c0, keys, states, mask):
                out = []
                for hd, (m, acc) in enumerate(states):
                    cols = slice(hd * HEAD_PAD, (hd + 1) * HEAD_PAD)
                    s = _dot_nt(q_ref[0, pl.ds(r0, rows), cols], k_ref[0, pl.ds(c0, keys), cols])
                    if mask is not None:
                        s = jnp.where(mask, s, MASK_VALUE)
                    m_new = jnp.maximum(m, jnp.max(s, axis=-1, keepdims=True))
                    p = jnp.exp((s - m_new).astype(BF16))
                    acc = jnp.exp(m - m_new) * acc + _dot(p, v_ref[0, pl.ds(c0, keys), cols])
                    out.append((m_new, acc))
                return tuple(out)

            init = tuple((jnp.full((rows, 1), MASK_VALUE, F32), jnp.zeros((rows, HEAD_PAD), F32)) for _ in range(hpg))
            states = lax.fori_loop(0, whole_kv_blocks, lambda j, st: kv_step(pl.multiple_of(j * tb, tb), tb, st, None), init)
            query = lax.broadcasted_iota(jnp.int32, (rows, back + rows), 0)
            key = lax.broadcasted_iota(jnp.int32, (rows, back + rows), 1)
            states = kv_step(pl.multiple_of(r0 - back, SEQ_PAD), back + rows, states, key <= query + back)
            lane = lax.broadcasted_iota(jnp.int32, (rows, HEAD_PAD), 1)
            lse_rows = lse_ref[0, pl.ds(r0, rows), :]
            for hd, (m, acc) in enumerate(states):
                l = jnp.sum(jnp.where(lane == V_DIM, acc, 0.0), axis=-1, keepdims=True)
                o_ref[0, pl.ds(r0, rows), hd * HEAD_PAD:(hd + 1) * HEAD_PAD] = (acc / l).astype(BF16)
                lse_rows = jnp.where(lane == group * hpg + hd, m + jnp.log(l), lse_rows)
            lse_ref[0, pl.ds(r0, rows), :] = lse_rows

        def whole_block(i, carry):
            for back in range(0, tb, ATT_Q_ROWS):
                q_rows(pl.multiple_of(i * tb + back, ATT_Q_ROWS), ATT_Q_ROWS, i, back)
            return carry

        lax.fori_loop(0, nfull, whole_block, 0)
        for back in range(0, tail, ATT_Q_ROWS):
            q_rows(tail_start + back, min(ATT_Q_ROWS, tail - back), nfull, back)
        if done < lp:
            o_ref[0, done:lp, :] = jnp.zeros((lp - done, width), BF16)

    head_spec = pl.BlockSpec((1, lp, width), lambda bi, hi: (bi, 0, hi))
    return _call_with_exchange(
        body, exchange, name="attn_fwd", grid=(b, heads // hpg),
        in_specs=[head_spec, head_spec, head_spec],
        out_specs=[head_spec, pl.BlockSpec((1, lp, HEAD_PAD), lambda bi, hi: (bi, 0, 0))],
        out_shape=[jax.ShapeDtypeStruct((b, lp, hw), BF16), jax.ShapeDtypeStruct((b, lp, HEAD_PAD), F32)],
        scratch_shapes=[], operands=(q3, k3, v3))


def _merge_fwd(h, z, a, o, w_pa, w_pb_p, w_o):
    t, d = h.shape
    hw = o.shape[1]
    tm = _token_tile(t, 512)
    gate_block = Z_FIXED // d

    def body(h_ref, ga_ref, gb_ref, a_ref, o_ref, wpa_ref, wpb_ref, wo_ref, h1_ref, pa_ref, pb_ref):
        pa = _dot(a_ref[...], wpa_ref[...])
        pb = _dot(o_ref[...], wpb_ref[...])
        merged = jax.nn.sigmoid(ga_ref[...].astype(F32)) * pa + jax.nn.sigmoid(gb_ref[...].astype(F32)) * pb
        h1_ref[...] = h_ref[...] + _dot(merged.astype(BF16), wo_ref[...])
        pa_ref[...] = pa.astype(BF16)
        pb_ref[...] = pb.astype(BF16)

    return pl.pallas_call(
        body, name="merge_fwd", grid=(t // tm,),
        in_specs=[_row_block(tm, d), _row_block(tm, d, gate_block), _row_block(tm, d, gate_block + 1),
                  _row_block(tm, POOL_WIDTH), _row_block(tm, hw), _vmem_spec(), _vmem_spec(), _vmem_spec()],
        out_specs=[_row_block(tm, d)] * 3,
        out_shape=[jax.ShapeDtypeStruct((t, d), F32), jax.ShapeDtypeStruct((t, d), BF16), jax.ShapeDtypeStruct((t, d), BF16)],
        compiler_params=_params(("parallel",)),
    )(h, z, z, a, o, w_pa, w_pb_p, w_o)


def _ffn_fwd(h, g, w_gate_t, w_up_t, w_down):
    t, d = h.shape
    f = w_gate_t.shape[0]
    tm = _token_tile(t, 256)

    def body(h_ref, g_ref, wg_ref, wu_ref, wd_ref, h2_ref, a_ref, b_ref):
        x = h_ref[...]
        hn, _, _ = _rms_fwd(x, g_ref[...])
        hn = hn.astype(BF16)
        ga = _dot_nt(hn, wg_ref[...])
        up = _dot_nt(hn, wu_ref[...])
        act = ga * jax.nn.sigmoid(ga) * up
        h2_ref[...] = x + _dot(act.astype(BF16), wd_ref[...])
        a_ref[...] = ga.astype(BF16)
        b_ref[...] = up.astype(BF16)

    return pl.pallas_call(
        body, name="ffn_fwd", grid=(t // tm,),
        in_specs=[_row_block(tm, d), _vmem_spec(), _vmem_spec(), _vmem_spec(), _vmem_spec()],
        out_specs=[_row_block(tm, d), _row_block(tm, f), _row_block(tm, f)],
        out_shape=[jax.ShapeDtypeStruct((t, d), F32), jax.ShapeDtypeStruct((t, f), BF16), jax.ShapeDtypeStruct((t, f), BF16)],
        compiler_params=_params(("parallel",)),
    )(h, g, w_gate_t, w_up_t, w_down)


def _loss_head(h, g, target, valid):
    t, d = h.shape
    tm = _token_tile(t, 512)

    def body(h_ref, g_ref, t_ref, valid_ref, dh_ref, loss_ref, dg_ref):
        @pl.when(pl.program_id(0) == 0)
        def _():
            loss_ref[...] = jnp.zeros_like(loss_ref)
            dg_ref[...] = jnp.zeros_like(dg_ref)

        gain = g_ref[...]
        y, xh, r = _rms_fwd(h_ref[...], gain)
        err = (y - t_ref[...]) * valid_ref[...]
        per_row = jnp.sum(err * err, axis=-1, keepdims=True) / d
        loss_ref[...] += 0.5 * jnp.sum(per_row, axis=0, keepdims=True)
        dx, dg_rows = _rms_bwd(err / d, xh, r, gain)
        dh_ref[...] = dx
        dg_ref[...] += jnp.sum(dg_rows, axis=0, keepdims=True)

    return pl.pallas_call(
        body, name="loss_head", grid=(t // tm,),
        in_specs=[_row_block(tm, d), _vmem_spec(), _row_block(tm, d), _row_block(tm, 1)],
        out_specs=[_row_block(tm, d), pl.BlockSpec((1, 1), lambda i: (0, 0)), pl.BlockSpec((1, d), lambda i: (0, 0))],
        out_shape=[jax.ShapeDtypeStruct((t, d), F32), jax.ShapeDtypeStruct((1, 1), F32), jax.ShapeDtypeStruct((1, d), F32)],
        compiler_params=_params(("arbitrary",)),
    )(h, g, target, valid)


def _weight_grad(x, y, name):
    t, k = x.shape
    n = y.shape[1]
    tm = _token_tile(t, 8 * MXU_DEPTH)
    tk, tn = k, n
    while tk * tn * 4 > ACC_BYTES and max(tk, tn) % 256 == 0:
        if tk > tn:
            tk //= 2
        else:
            tn //= 2
    steps = t // tm

    def body(x_ref, y_ref, o_ref, acc):
        @pl.when(pl.program_id(2) == 0)
        def _():
            acc[...] = jnp.zeros_like(acc)

        acc[...] += _dot_tn(x_ref[...].astype(BF16), y_ref[...].astype(BF16))

        @pl.when(pl.program_id(2) == steps - 1)
        def _():
            o_ref[...] = acc[...].astype(BF16)

    return pl.pallas_call(
        body, name=name, grid=(k // tk, n // tn, steps),
        in_specs=[pl.BlockSpec((tm, tk), lambda a, b, i: (i, a)), pl.BlockSpec((tm, tn), lambda a, b, i: (i, b))],
        out_specs=pl.BlockSpec((tk, tn), lambda a, b, i: (a, b)),
        out_shape=jax.ShapeDtypeStruct((k, n), BF16),
        scratch_shapes=[pltpu.VMEM((tk, tn), F32)],
        compiler_params=_params(("parallel", "parallel", "arbitrary")),
    )(x, y)


def _ffn_bwd(h, dh2, a, b, g, w_gate_t, w_up_t, w_down):
    t, d = h.shape
    f = a.shape[1]
    tm = _token_tile(t, 256)

    def body(h_ref, dh2_ref, a_ref, b_ref, g_ref, wg_ref, wu_ref, wd_ref, dh_ref, hn_ref, act_ref, da_ref, db_ref, dg_ref):
        @pl.when(pl.program_id(0) == 0)
        def _():
            dg_ref[...] = jnp.zeros_like(dg_ref)

        gain = g_ref[...]
        hn, xh, r = _rms_fwd(h_ref[...], gain)
        hn_ref[...] = hn.astype(BF16)
        dh2 = dh2_ref[...]
        dact = _dot_nt(dh2.astype(BF16), wd_ref[...])
        ga = a_ref[...].astype(F32)
        up = b_ref[...].astype(F32)
        sg = jax.nn.sigmoid(ga)
        silu = ga * sg
        act_ref[...] = (silu * up).astype(BF16)
        da = (dact * up * (sg * (1.0 + ga * (1.0 - sg)))).astype(BF16)
        db = (dact * silu).astype(BF16)
        da_ref[...] = da
        db_ref[...] = db
        dhn = _dot(da, wg_ref[...]) + _dot(db, wu_ref[...])
        dx, dg_rows = _rms_bwd(dhn, xh, r, gain)
        dh_ref[...] = dh2 + dx
        dg_ref[...] += jnp.sum(dg_rows, axis=0, keepdims=True)

    return pl.pallas_call(
        body, name="ffn_bwd", grid=(t // tm,),
        in_specs=[_row_block(tm, d), _row_block(tm, d), _row_block(tm, f), _row_block(tm, f),
                  _vmem_spec(), _vmem_spec(), _vmem_spec(), _vmem_spec()],
        out_specs=[_row_block(tm, d), _row_block(tm, d), _row_block(tm, f), _row_block(tm, f), _row_block(tm, f),
                   pl.BlockSpec((1, d), lambda i: (0, 0))],
        out_shape=[jax.ShapeDtypeStruct((t, d), F32), jax.ShapeDtypeStruct((t, d), BF16), jax.ShapeDtypeStruct((t, f), BF16),
                   jax.ShapeDtypeStruct((t, f), BF16), jax.ShapeDtypeStruct((t, f), BF16), jax.ShapeDtypeStruct((1, d), F32)],
        compiler_params=_params(("arbitrary",)),
    )(h, dh2, a, b, g, w_gate_t, w_up_t, w_down)


def _merge_bwd(dh1, z, pa, pb, w_o, w_pa, w_pb_p):
    t, d = dh1.shape
    hw = w_pb_p.shape[0]
    tm = _token_tile(t, 512)
    gate_block = Z_FIXED // d

    def body(dh_ref, ga_ref, gb_ref, pa_ref, pb_ref, wo_ref, wpa_ref, wpb_ref,
             mg_ref, dpa_ref, dpb_ref, dga_ref, dgb_ref, da_ref, do_ref):
        dm = _dot_nt(dh_ref[...].astype(BF16), wo_ref[...])
        sa = jax.nn.sigmoid(ga_ref[...].astype(F32))
        sb = jax.nn.sigmoid(gb_ref[...].astype(F32))
        pa = pa_ref[...].astype(F32)
        pb = pb_ref[...].astype(F32)
        mg_ref[...] = (sa * pa + sb * pb).astype(BF16)
        dpa = (dm * sa).astype(BF16)
        dpb = (dm * sb).astype(BF16)
        dpa_ref[...] = dpa
        dpb_ref[...] = dpb
        dga_ref[...] = (dm * pa * (sa * (1.0 - sa))).astype(BF16)
        dgb_ref[...] = (dm * pb * (sb * (1.0 - sb))).astype(BF16)
        da_ref[...] = _dot_nt(dpa, wpa_ref[...]).astype(BF16)
        do_ref[...] = _dot_nt(dpb, wpb_ref[...]).astype(BF16)

    wide = jax.ShapeDtypeStruct((t, d), BF16)
    return pl.pallas_call(
        body, name="merge_bwd", grid=(t // tm,),
        in_specs=[_row_block(tm, d), _row_block(tm, d, gate_block), _row_block(tm, d, gate_block + 1),
                  _row_block(tm, d), _row_block(tm, d), _vmem_spec(), _vmem_spec(), _vmem_spec()],
        out_specs=[_row_block(tm, d)] * 5 + [_row_block(tm, POOL_WIDTH), _row_block(tm, hw)],
        out_shape=[wide] * 5 + [jax.ShapeDtypeStruct((t, POOL_WIDTH), BF16), jax.ShapeDtypeStruct((t, hw), BF16)],
        compiler_params=_params(("parallel",)),
    )(dh1, z, z, pa, pb, w_o, w_pa, w_pb_p)


def _attn_bwd(q3, k3, v3, o3, do3, lse3, real_len, exchange=None):
    b, lp, hw = q3.shape
    heads = hw // HEAD_PAD
    tb = ATT_BLOCK
    nfull, tail_start, tail = _attn_blocks(real_len)
    done = tail_start + tail
    hpg = _heads_per_step(heads, ATT_BWD_HEADS)
    width = hpg * HEAD_PAD

    def body(q_ref, k_ref, v_ref, o_ref, do_ref, lse_ref, dq_ref, dk_ref, dv_ref, dqt_acc, lse_row, delta_row):
        group = pl.program_id(1)
        lse_t = jnp.transpose(lse_ref[0])
        head_of_row = lax.broadcasted_iota(jnp.int32, (HEAD_PAD, lp), 0)
        for hd in range(hpg):
            cols = slice(hd * HEAD_PAD, (hd + 1) * HEAD_PAD)
            lse_row[hd] = jnp.sum(jnp.where(head_of_row == group * hpg + hd, lse_t, 0.0), axis=0, keepdims=True)
            prod = do_ref[0, :, cols].astype(F32) * o_ref[0, :, cols].astype(F32)
            delta_row[hd] = jnp.sum(jnp.transpose(prod), axis=0, keepdims=True)
        dqt_acc[...] = jnp.zeros_like(dqt_acc)

        def kv_rows(c0, keys, whole_q_blocks_from):
            k_t = [jnp.transpose(k_ref[0, pl.ds(c0, keys), hd * HEAD_PAD:(hd + 1) * HEAD_PAD].astype(F32)).astype(BF16)
                   for hd in range(hpg)]

            def q_step(r0, rows, states, mask):
                out = []
                for hd, (dk, dv) in enumerate(states):
                    cols = slice(hd * HEAD_PAD, (hd + 1) * HEAD_PAD)
                    q = q_ref[0, pl.ds(r0, rows), cols]
                    do = do_ref[0, pl.ds(r0, rows), cols]
                    s_t = _dot_nt(k_ref[0, pl.ds(c0, keys), cols], q)
                    if mask is not None:
                        s_t = jnp.where(mask, s_t, MASK_VALUE)
                    p_t = jnp.exp(s_t - lse_row[hd, :, pl.ds(r0, rows)])
                    dp_t = _dot_nt(v_ref[0, pl.ds(c0, keys), cols], do)
                    ds_t = (p_t * (dp_t - delta_row[hd, :, pl.ds(r0, rows)])).astype(BF16)
                    dv = dv + _dot(p_t.astype(BF16), do)
                    dk = dk + _dot(ds_t, q)
                    dqt_acc[cols, pl.ds(r0, rows)] += _dot(k_t[hd], ds_t)
                    out.append((dk, dv))
                return tuple(out)

            zero = jnp.zeros((keys, HEAD_PAD), F32)
            key_pos = lax.broadcasted_iota(jnp.int32, (keys, keys), 0)
            query_pos = lax.broadcasted_iota(jnp.int32, (keys, keys), 1)
            states = q_step(c0, keys, tuple((zero, zero) for _ in range(hpg)), key_pos <= query_pos)
            if whole_q_blocks_from is not None:
                states = lax.fori_loop(whole_q_blocks_from, nfull,
                                       lambda i, st: q_step(pl.multiple_of(i * tb, tb), tb, st, None), states)
                states = q_step(tail_start, tail, states, None)
            for hd, (dk, dv) in enumerate(states):
                cols = slice(hd * HEAD_PAD, (hd + 1) * HEAD_PAD)
                dk_ref[0, pl.ds(c0, keys), cols] = dk.astype(BF16)
                dv_ref[0, pl.ds(c0, keys), cols] = dv.astype(BF16)

        def whole_block(j, carry):
            kv_rows(pl.multiple_of(j * tb, tb), tb, j + 1)
            return carry

        lax.fori_loop(0, nfull, whole_block, 0)
        kv_rows(tail_start, tail, None)
        if done < lp:
            dk_ref[0, done:lp, :] = jnp.zeros((lp - done, width), BF16)
            dv_ref[0, done:lp, :] = jnp.zeros((lp - done, width), BF16)
        for hd in range(hpg):
            cols = slice(hd * HEAD_PAD, (hd + 1) * HEAD_PAD)
            dq_ref[0, :, cols] = jnp.transpose(dqt_acc[cols, :]).astype(BF16)

    head_spec = pl.BlockSpec((1, lp, width), lambda bi, hi: (bi, 0, hi))
    out = jax.ShapeDtypeStruct((b, lp, hw), BF16)
    return _call_with_exchange(
        body, exchange, name="attn_bwd", grid=(b, heads // hpg),
        in_specs=[head_spec] * 5 + [pl.BlockSpec((1, lp, HEAD_PAD), lambda bi, hi: (bi, 0, 0))],
        out_specs=[head_spec] * 3,
        out_shape=[out, out, out],
        scratch_shapes=[pltpu.VMEM((width, lp), F32), pltpu.VMEM((hpg, 1, lp), F32), pltpu.VMEM((hpg, 1, lp), F32)],
        operands=(q3, k3, v3, o3, do3, lse3))


def _qkv_bwd(dq, dk, dv, z, g_q, g_kv, w_uq_p, w_kv_p, rope_c, rope_sa, rope_sb):
    t, hw = dq.shape
    heads = hw // HEAD_PAD
    tm = _token_tile(t, 512)

    def body(dq_ref, dk_ref, dv_ref, cq_ref, ckv_ref, gq_ref, gkv_ref, wq_ref, wkv_ref, c_ref, sa_ref, sb_ref,
             dqraw_ref, dkvraw_ref, cqn_ref, ckvn_ref, dcq_ref, dckv_ref, dkr_ref, dgq_ref, dgkv_ref):
        @pl.when(pl.program_id(0) == 0)
        def _():
            dgq_ref[...] = jnp.zeros_like(dgq_ref)
            dgkv_ref[...] = jnp.zeros_like(dgkv_ref)

        c, sa, sb = c_ref[...], sa_ref[...], sb_ref[...]
        dkr = jnp.zeros((tm, HEAD_PAD), F32)
        for hd in range(heads):
            cols = slice(hd * HEAD_PAD, (hd + 1) * HEAD_PAD)
            dqraw_ref[:, cols] = _rope_bwd(dq_ref[:, cols].astype(F32) * SM_SCALE, c, sa, sb).astype(BF16)
            dkvraw_ref[:, cols] = dk_ref[:, cols]
            dkr = dkr + dk_ref[:, cols].astype(F32)
        dkvraw_ref[:, hw:] = dv_ref[...]
        lane = lax.broadcasted_iota(jnp.int32, (tm, HEAD_PAD), 1)
        dkr_ref[...] = jnp.where((lane >= QK_NOPE) & (lane < QK_DIM), _rope_bwd(dkr, c, sa, sb), 0.0).astype(BF16)

        gq = gq_ref[...]
        cqn, xh, r = _rms_fwd(cq_ref[...].astype(F32), gq)
        cqn_ref[...] = cqn.astype(BF16)
        dx, dg_rows = _rms_bwd(_dot_nt(dqraw_ref[...], wq_ref[...]), xh, r, gq)
        dcq_ref[...] = dx.astype(BF16)
        dgq_ref[...] += jnp.sum(dg_rows, axis=0, keepdims=True)

        gkv = gkv_ref[...]
        ckvn, xh, r = _rms_fwd(ckv_ref[...].astype(F32), gkv)
        ckvn_ref[...] = ckvn.astype(BF16)
        dx, dg_rows = _rms_bwd(_dot_nt(dkvraw_ref[...], wkv_ref[...]), xh, r, gkv)
        dckv_ref[...] = dx.astype(BF16)
        dgkv_ref[...] += jnp.sum(dg_rows, axis=0, keepdims=True)

    def shape(width, dtype=BF16):
        return jax.ShapeDtypeStruct((t, width), dtype)

    return pl.pallas_call(
        body, name="qkv_bwd", grid=(t // tm,),
        in_specs=[_row_block(tm, hw)] * 3
        + [_row_block(tm, Q_RANK, POOL_WIDTH // Q_RANK), _row_block(tm, KV_RANK, (POOL_WIDTH + Q_RANK) // KV_RANK)]
        + [_vmem_spec()] * 4 + [_row_block(tm, HEAD_PAD)] * 3,
        out_specs=[_row_block(tm, hw), _row_block(tm, 2 * hw), _row_block(tm, Q_RANK), _row_block(tm, KV_RANK),
                   _row_block(tm, Q_RANK), _row_block(tm, KV_RANK), _row_block(tm, HEAD_PAD),
                   pl.BlockSpec((1, Q_RANK), lambda i: (0, 0)), pl.BlockSpec((1, KV_RANK), lambda i: (0, 0))],
        out_shape=[shape(hw), shape(2 * hw), shape(Q_RANK), shape(KV_RANK), shape(Q_RANK), shape(KV_RANK), shape(HEAD_PAD),
                   jax.ShapeDtypeStruct((1, Q_RANK), F32), jax.ShapeDtypeStruct((1, KV_RANK), F32)],
        compiler_params=_params(("arbitrary",)),
    )(dq, dk, dv, z, z, g_q, g_kv, w_uq_p, w_kv_p, rope_c, rope_sa, rope_sb)


def _pool_bwd(z3, da3, pool_w, pool_scale):
    b, lp, _ = z3.shape
    groups = len(POOL_WINDOWS)

    def body(u_ref, da_ref, pw_ref, sc_ref, du_ref, dpw_ref, dsc_ref):
        @pl.when(pl.program_id(0) == 0)
        def _():
            dpw_ref[...] = jnp.zeros_like(dpw_ref)
            dsc_ref[...] = jnp.zeros_like(dsc_ref)

        row = lax.broadcasted_iota(jnp.int32, (lp, POOL_GROUP), 0)
        pos = row.astype(F32)
        for gi, w in enumerate(POOL_WINDOWS):
            cols = slice(gi * POOL_GROUP, (gi + 1) * POOL_GROUP)
            count = jnp.minimum(pos + 1.0, float(w))
            u = u_ref[0, :, cols].astype(F32)
            y = (_window_sum(u, w, row, True) / count - u).astype(BF16)
            yw = _dot(y, pw_ref[gi])
            da = da_ref[0, :, cols].astype(F32)
            dsc_ref[:, cols] += jnp.sum(da * yw, axis=0, keepdims=True)
            dyw = (da * sc_ref[:, cols]).astype(BF16)
            dpw_ref[gi] += _dot_tn(y, dyw)
            dy = _dot_nt(dyw, pw_ref[gi])
            du_ref[0, :, cols] = (_window_sum(dy / count, w, row, False) - dy).astype(BF16)

    return pl.pallas_call(
        body, name="pool_bwd", grid=(b,),
        in_specs=[pl.BlockSpec((1, lp, POOL_WIDTH), lambda i: (i, 0, 0)), pl.BlockSpec((1, lp, POOL_WIDTH), lambda i: (i, 0, 0)),
                  _vmem_spec(), _vmem_spec()],
        out_specs=[pl.BlockSpec((1, lp, POOL_WIDTH), lambda i: (i, 0, 0)),
                   pl.BlockSpec((groups, POOL_GROUP, POOL_GROUP), lambda i: (0, 0, 0)),
                   pl.BlockSpec((1, POOL_WIDTH), lambda i: (0, 0))],
        out_shape=[jax.ShapeDtypeStruct((b, lp, POOL_WIDTH), BF16), jax.ShapeDtypeStruct((groups, POOL_GROUP, POOL_GROUP), F32),
                   jax.ShapeDtypeStruct((1, POOL_WIDTH), F32)],
        compiler_params=_params(("arbitrary",)),
    )(z3, da3, pool_w, pool_scale)


def _in_proj_bwd(h, dh1, du, dcq, dckv, dkr, dga, dgb, g, w_in_pt):
    t, d = h.shape
    nz = w_in_pt.shape[0]
    tm = _token_tile(t, 512)
    widths = (POOL_WIDTH, Q_RANK, KV_RANK, HEAD_PAD, d, d)

    def body(h_ref, dh1_ref, du_ref, dcq_ref, dckv_ref, dkr_ref, dga_ref, dgb_ref, g_ref, w_ref, dh_ref, hn_ref, dz_ref, dg_ref):
        @pl.when(pl.program_id(0) == 0)
        def _():
            dg_ref[...] = jnp.zeros_like(dg_ref)

        gain = g_ref[...]
        hn, xh, r = _rms_fwd(h_ref[...], gain)
        hn_ref[...] = hn.astype(BF16)
        dhn = jnp.zeros((tm, d), F32)
        start = 0
        for piece, width in zip((du_ref, dcq_ref, dckv_ref, dkr_ref, dga_ref, dgb_ref), widths):
            val = piece[...]
            dz_ref[:, start:start + width] = val
            dhn = dhn + _dot(val, w_ref[start:start + width, :])
            start += width
        dx, dg_rows = _rms_bwd(dhn, xh, r, gain)
        dh_ref[...] = dh1_ref[...] + dx
        dg_ref[...] += jnp.sum(dg_rows, axis=0, keepdims=True)

    return pl.pallas_call(
        body, name="in_proj_bwd", grid=(t // tm,),
        in_specs=[_row_block(tm, d), _row_block(tm, d)] + [_row_block(tm, w) for w in widths] + [_vmem_spec(), _vmem_spec()],
        out_specs=[_row_block(tm, d), _row_block(tm, d), _row_block(tm, nz), pl.BlockSpec((1, d), lambda i: (0, 0))],
        out_shape=[jax.ShapeDtypeStruct((t, d), F32), jax.ShapeDtypeStruct((t, d), BF16), jax.ShapeDtypeStruct((t, nz), BF16),
                   jax.ShapeDtypeStruct((1, d), F32)],
        compiler_params=_params(("arbitrary",)),
    )(h, dh1, du, dcq, dckv, dkr, dga, dgb, g, w_in_pt)


def _pad_heads(w, heads, width):
    k = w.shape[0]
    w = w.reshape(k, heads, width)
    return jnp.pad(w, ((0, 0), (0, 0), (0, HEAD_PAD - width))).reshape(k, heads * HEAD_PAD)


def _unpad_heads(w, heads, width):
    k = w.shape[0]
    return w.reshape(k, heads, HEAD_PAD)[:, :, :width].reshape(k, heads * width)


def _early_layouts(w, heads):
    o3, o4 = POOL_WIDTH + Q_RANK + KV_RANK, POOL_WIDTH + Q_RANK + KV_RANK + QK_ROPE
    w_in_t = w["w_in"]
    rope_rows = jnp.pad(w_in_t[o3:o4], ((QK_NOPE, HEAD_PAD - QK_DIM), (0, 0)))
    w_in_pt = jnp.concatenate([w_in_t[:o3], rope_rows, w_in_t[o4:]], axis=0)
    w_uq_p = _pad_heads(w["w_uq"], heads, QK_DIM)
    kv = w["w_ukv"].reshape(KV_RANK, heads, QK_NOPE + V_DIM)
    w_k = jnp.pad(kv[:, :, :QK_NOPE], ((0, 0), (0, 0), (0, HEAD_PAD - QK_NOPE))).reshape(KV_RANK, heads * HEAD_PAD)
    w_v = jnp.pad(kv[:, :, QK_NOPE:], ((0, 0), (0, 0), (0, HEAD_PAD - V_DIM))).reshape(KV_RANK, heads * HEAD_PAD)
    return dict(w_in_pt=w_in_pt, w_uq_p=w_uq_p, w_kv_p=jnp.concatenate([w_k, w_v], axis=1))


def _late_layouts(w, heads):
    d = w["w_pb"].shape[1]
    w_pb_p = jnp.pad(w["w_pb"].reshape(heads, V_DIM, d), ((0, 0), (0, HEAD_PAD - V_DIM), (0, 0))).reshape(heads * HEAD_PAD, d)
    return dict(w_pa=w["w_pa"], w_pb_p=w_pb_p, w_o=w["w_o"], w_gate_t=w["w_gate"], w_up_t=w["w_up"], w_down=w["w_down"])


def _early_grad_layouts(g, heads):
    o3 = POOL_WIDTH + Q_RANK + KV_RANK
    gin = g["w_in_pt"]
    w_in = jnp.concatenate([gin[:o3], gin[o3 + QK_NOPE:o3 + QK_DIM], gin[o3 + HEAD_PAD:]], axis=0)
    hw = heads * HEAD_PAD
    gk = g["w_kv_p"][:, :hw].reshape(KV_RANK, heads, HEAD_PAD)[:, :, :QK_NOPE]
    gv = g["w_kv_p"][:, hw:].reshape(KV_RANK, heads, HEAD_PAD)[:, :, :V_DIM]
    w_ukv = jnp.concatenate([gk, gv], axis=2).reshape(KV_RANK, heads * (QK_NOPE + V_DIM))
    return dict(w_in=w_in, w_uq=_unpad_heads(g["w_uq_p"], heads, QK_DIM), w_ukv=w_ukv)


def _late_grad_layouts(g, heads):
    d = g["w_pb_p"].shape[1]
    w_pb = g["w_pb_p"].reshape(heads, HEAD_PAD, d)[:, :V_DIM].reshape(heads * V_DIM, d)
    return dict(w_pa=g["w_pa"], w_pb=w_pb, w_o=g["w_o"], w_gate=g["w_gate_t"], w_up=g["w_up_t"], w_down=g["w_down"])


def _rope_tables(lp, b):
    inv = 1.0 / (ROPE_THETA ** (jnp.arange(0, QK_ROPE, 2, dtype=F32) / QK_ROPE))
    ang = jnp.arange(lp, dtype=F32)[:, None] * inv[None, :]
    cos, sin = jnp.cos(ang), jnp.sin(ang)
    half = QK_ROPE // 2
    ones = jnp.ones((lp, QK_NOPE), F32)
    zeros_lo = jnp.zeros((lp, QK_NOPE), F32)
    zeros_hi = jnp.zeros((lp, HEAD_PAD - QK_DIM), F32)
    zeros_half = jnp.zeros((lp, half), F32)
    c = jnp.concatenate([ones, cos, cos, zeros_hi], axis=1)
    sa = jnp.concatenate([zeros_lo, zeros_half, sin, zeros_hi], axis=1)
    sb = jnp.concatenate([zeros_lo, -sin, zeros_half, zeros_hi], axis=1)
    return tuple(jnp.tile(tab, (b, 1)) for tab in (c, sa, sb))


def _local_step(x, loss_target, meta_tokens, small, early_first, riding, early_first_codec):
    b, seq, d = x.shape
    depth = 2
    heads = early_first["w_uq"].shape[1] // QK_DIM
    core = lax.axis_index("c")
    chip = 2 * lax.axis_index("x") + lax.axis_index("y")
    real_len = N_META + seq
    lp = _round_up(real_len, SEQ_PAD)
    t = b * lp
    pad = lp - N_META - seq

    meta = jnp.broadcast_to(meta_tokens[None], (b, N_META, d))
    h = jnp.concatenate([meta, x, jnp.zeros((b, pad, d), F32)], axis=1).reshape(t, d)
    target = jnp.pad(loss_target, ((0, 0), (N_META, pad), (0, 0))).reshape(t, d)
    pos = jnp.arange(lp)
    valid = jnp.tile(((pos >= N_META) & (pos < N_META + seq)).astype(F32), b).reshape(t, 1)
    rope_c, rope_sa, rope_sb = _rope_tables(lp, b)

    layers = []
    for li in range(depth):
        lay = dict(pool_w=small["pool_w"][li].astype(BF16), pool_scale=small["pool_scale"][li][None])
        for n in ("norm_mix_g", "q_norm_g", "kv_norm_g", "norm_ffn_g"):
            lay[n] = small[n][li][None]
        layers.append(lay)
    layers[0].update(_early_layouts(early_first, heads))

    saved = []
    for li in range(depth):
        lay = layers[li]
        z = _in_proj_fwd(h, lay["norm_mix_g"], lay["w_in_pt"])
        a = _pool_fwd(z.reshape(b, lp, -1), lay["pool_w"], lay["pool_scale"]).reshape(t, POOL_WIDTH)
        q, k, v = _qkv_fwd(z, lay["q_norm_g"], lay["kv_norm_g"], lay["w_uq_p"], lay["w_kv_p"], rope_c, rope_sa, rope_sb)
        hw = q.shape[1]
        halves = riding[li]["packed"].reshape(2, -1, PACK_COLS)
        o3, lse, others = _attn_fwd(q.reshape(b, lp, hw), k.reshape(b, lp, hw), v.reshape(b, lp, hw), real_len,
                                    _gather_exchange(halves))
        arrived = riding[li]["unpack"](
            lax.dynamic_update_index_in_dim(others, halves, chip, 0).reshape(N_CHIPS, -1, PACK_COLS))
        lay.update(_late_layouts({n: arrived[n, li] for n in LATE_WEIGHTS}, heads))
        if li + 1 < depth:
            layers[li + 1].update(_early_layouts({n: arrived[n, li + 1] for n in EARLY_WEIGHTS}, heads))
        o = o3.reshape(t, hw)
        h1, pa, pb = _merge_fwd(h, z, a, o, lay["w_pa"], lay["w_pb_p"], lay["w_o"])
        h2, fa, fb = _ffn_fwd(h1, lay["norm_ffn_g"], lay["w_gate_t"], lay["w_up_t"], lay["w_down"])
        saved.append(dict(h=h, z=z, a=a, q=q, k=k, v=v, o=o, lse=lse, pa=pa, pb=pb, h1=h1, fa=fa, fb=fb))
        h = h2

    dh, loss, d_final = _loss_head(h, small["final_norm_g"][None], target, valid)

    g_small = {n: [] for n in SMALL_WEIGHTS if n != "final_norm_g"}
    early_grads, parts = {}, [None] * depth
    for li in reversed(range(depth)):
        lay, sv = layers[li], saved[li]
        hw = sv["q"].shape[1]
        dh1, hn_f, act, dfa, dfb, dg_ffn = _ffn_bwd(sv["h1"], dh, sv["fa"], sv["fb"], lay["norm_ffn_g"],
                                                     lay["w_gate_t"], lay["w_up_t"], lay["w_down"])
        gl = dict(w_gate_t=_weight_grad(dfa, hn_f, "grad_w_gate"), w_up_t=_weight_grad(dfb, hn_f, "grad_w_up"),
                  w_down=_weight_grad(act, dh, "grad_w_down"))
        merged, dpa, dpb, dga, dgb, da, do = _merge_bwd(dh1, sv["z"], sv["pa"], sv["pb"], lay["w_o"], lay["w_pa"], lay["w_pb_p"])
        gl["w_o"] = _weight_grad(merged, dh1, "grad_w_o")
        gl["w_pa"] = _weight_grad(sv["a"], dpa, "grad_w_pa")
        gl["w_pb_p"] = _weight_grad(sv["o"], dpb, "grad_w_pb")
        to_send = {(n, li): g for n, g in _late_grad_layouts(gl, heads).items()}
        if li + 1 < depth:
            to_send.update({(n, li + 1): g for n, g in early_grads[li + 1].items()})
        sending = riding[li]["pack_grads"](to_send)
        shape3 = (b, lp, hw)
        dq3, dk3, dv3, from_others = _attn_bwd(
            sv["q"].reshape(shape3), sv["k"].reshape(shape3), sv["v"].reshape(shape3), sv["o"].reshape(shape3),
            do.reshape(shape3), sv["lse"], real_len, _scatter_exchange(sending))
        own = lax.dynamic_index_in_dim(lax.dynamic_index_in_dim(sending, chip, 0, keepdims=False), core, 0, keepdims=False)
        parts[li] = lax.dynamic_update_index_in_dim(from_others, own, 2 * chip + core, 0)
        dqraw, dkvraw, cqn, ckvn, dcq, dckv, dkr, dg_q, dg_kv = _qkv_bwd(
            dq3.reshape(t, hw), dk3.reshape(t, hw), dv3.reshape(t, hw), sv["z"], lay["q_norm_g"], lay["kv_norm_g"],
            lay["w_uq_p"], lay["w_kv_p"], rope_c, rope_sa, rope_sb)
        gl["w_uq_p"] = _weight_grad(cqn, dqraw, "grad_w_uq")
        gl["w_kv_p"] = _weight_grad(ckvn, dkvraw, "grad_w_ukv")
        du3, dpool_w, dpool_scale = _pool_bwd(sv["z"].reshape(b, lp, -1), da.reshape(b, lp, POOL_WIDTH),
                                              lay["pool_w"], lay["pool_scale"])
        dh, hn_m, dz, dg_mix = _in_proj_bwd(sv["h"], dh1, du3.reshape(t, POOL_WIDTH), dcq, dckv, dkr, dga, dgb,
                                            lay["norm_mix_g"], lay["w_in_pt"])
        gl["w_in_pt"] = _weight_grad(dz, hn_m, "grad_w_in")
        early_grads[li] = _early_grad_layouts(gl, heads)
        for n, val in (("norm_mix_g", dg_mix[0]), ("pool_w", dpool_w), ("pool_scale", dpool_scale[0]), ("q_norm_g", dg_q[0]),
                       ("kv_norm_g", dg_kv[0]), ("norm_ffn_g", dg_ffn[0])):
            g_small[n].insert(0, val)

    dh3 = dh.reshape(b, lp, d)
    grad_x = dh3[:, N_META:N_META + seq]
    d_meta_rows = dh3[:, :N_META]
    g_small = {n: jnp.stack(v) for n, v in g_small.items()}
    g_small["final_norm_g"] = d_final[0]
    early_first_partial = early_first_codec["pack_grads"]({(n, 0): g for n, g in early_grads[0].items()})
    return loss, grad_x, d_meta_rows, g_small, early_first_partial, parts


def _mesh_place():
    x, y, c = lax.axis_index("x"), lax.axis_index("y"), lax.axis_index("c")
    others = [(1 - x, y), (x, 1 - y), (1 - x, 1 - y)]
    return x, y, c, 2 * x + y, others


def _remote(src, dst, send_sems, recv_sems, k, device):
    return pltpu.make_async_remote_copy(src_ref=src, dst_ref=dst, send_sem=send_sems.at[k], recv_sem=recv_sems.at[k],
                                        device_id=device, device_id_type=MESH)


def _gather_exchange(packed):
    def over_links(p_ref, g_ref, send_sems, recv_sems, r, slot):
        _, _, c, _, others = _mesh_place()
        ox, oy = others[r]
        return _remote(p_ref.at[c], g_ref.at[slot, c], send_sems, recv_sems, r, (ox, oy, c))

    def to_sibling(g_ref, send_sems, recv_sems, r, half):
        x, y, c, _, others = _mesh_place()
        ox, oy = others[r]
        block = g_ref.at[2 * ox + oy, c if half is None else half]
        return _remote(block, block, send_sems, recv_sems, 3 + r, (x, y, 1 - c))

    def start(p_ref, g_ref, send_sems, recv_sems):
        chip = _mesh_place()[3]
        for r in range(3):
            over_links(p_ref, g_ref, send_sems, recv_sems, r, chip).start()

    def pass_on(p_ref, g_ref, send_sems, recv_sems):
        others = _mesh_place()[4]
        for r, (ox, oy) in enumerate(others):
            over_links(p_ref, g_ref, send_sems, recv_sems, r, 2 * ox + oy).wait_recv()
            to_sibling(g_ref, send_sems, recv_sems, r, None).start()

    def finish(p_ref, g_ref, send_sems, recv_sems):
        _, _, c, chip, _ = _mesh_place()
        for r in range(3):
            to_sibling(g_ref, send_sems, recv_sems, r, 1 - c).wait_recv()
        for r in range(3):
            over_links(p_ref, g_ref, send_sems, recv_sems, r, chip).wait_send()
            to_sibling(g_ref, send_sems, recv_sems, r, None).wait_send()

    return dict(name="gather", operand=packed, copies=6, start=start, pass_on=pass_on, finish=finish,
                out_shape=jax.ShapeDtypeStruct((N_CHIPS,) + packed.shape, packed.dtype))


def _scatter_exchange(parts):
    flips = [(dx, dy, dc) for dx in (0, 1) for dy in (0, 1) for dc in (0, 1)][1:]

    def copy(p_ref, got_ref, send_sems, recv_sems, k, arriving):
        x, y, c, _, _ = _mesh_place()
        dx, dy, dc = flips[k]
        tx, ty, tc = (1 - x if dx else x), (1 - y if dy else y), (1 - c if dc else c)
        slot = 4 * tx + 2 * ty + tc if arriving else 4 * x + 2 * y + c
        return _remote(p_ref.at[2 * tx + ty, tc], got_ref.at[slot], send_sems, recv_sems, k, (tx, ty, tc))

    def start(p_ref, got_ref, send_sems, recv_sems):
        for k in range(len(flips)):
            copy(p_ref, got_ref, send_sems, recv_sems, k, False).start()

    def finish(p_ref, got_ref, send_sems, recv_sems):
        for k in range(len(flips)):
            copy(p_ref, got_ref, send_sems, recv_sems, k, True).wait_recv()
        for k in range(len(flips)):
            copy(p_ref, got_ref, send_sems, recv_sems, k, False).wait_send()

    return dict(name="scatter", operand=parts, copies=len(flips), start=start, finish=finish,
                out_shape=jax.ShapeDtypeStruct((2 * N_CHIPS,) + parts.shape[2:], parts.dtype))


def _all_gather_shards(packed, meta_shard):
    _, rh, cols = packed.shape

    def body(p_ref, m_ref, g_ref, gm_ref, send_sems, recv_sems):
        x, y, c, chip, others = _mesh_place()
        sibling = (x, y, 1 - c)
        sends = []
        for r, (ox, oy) in enumerate(others):
            sends.append(_remote(p_ref.at[c], g_ref.at[chip, c], send_sems, recv_sems, r, (ox, oy, c)))
            sends.append(_remote(m_ref, gm_ref.at[chip], send_sems, recv_sems, 6 + r, (ox, oy, c)))
        for cp in sends:
            cp.start()
        for r, (ox, oy) in enumerate(others):
            src_chip = 2 * ox + oy
            _remote(p_ref.at[c], g_ref.at[src_chip, c], send_sems, recv_sems, r, (ox, oy, c)).wait_recv()
            passed = _remote(g_ref.at[src_chip, c], g_ref.at[src_chip, c], send_sems, recv_sems, 3 + r, sibling)
            passed.start()
            sends.append(passed)
        for r, (ox, oy) in enumerate(others):
            src_chip = 2 * ox + oy
            _remote(p_ref.at[c], g_ref.at[src_chip, 1 - c], send_sems, recv_sems, 3 + r, sibling).wait_recv()
            _remote(m_ref, gm_ref.at[src_chip], send_sems, recv_sems, 6 + r, (ox, oy, c)).wait_recv()
        for cp in sends:
            cp.wait_send()

    gathered, meta_all = pl.pallas_call(
        body, name="all_gather_shards",
        in_specs=[_any_spec(), _any_spec()], out_specs=[_any_spec(), _any_spec()],
        out_shape=[jax.ShapeDtypeStruct((N_CHIPS, 2, rh, cols), packed.dtype),
                   jax.ShapeDtypeStruct((N_CHIPS,) + meta_shard.shape, meta_shard.dtype)],
        scratch_shapes=[pltpu.SemaphoreType.DMA((9,)), pltpu.SemaphoreType.DMA((9,))],
    )(packed, meta_shard)
    chip = 2 * lax.axis_index("x") + lax.axis_index("y")
    return (lax.dynamic_update_index_in_dim(gathered, packed, chip, 0),
            lax.dynamic_update_index_in_dim(meta_all, meta_shard, chip, 0))


def _pair_exchange(give):
    def body(give_ref, got_ref, send_sems, recv_sems):
        x, y, c, _, _ = _mesh_place()
        cp = _remote(give_ref, got_ref, send_sems, recv_sems, 0, (x, y, 1 - c))
        cp.start()
        cp.wait()

    return pl.pallas_call(
        body, name="pair_exchange", in_specs=[_any_spec()], out_specs=_any_spec(),
        out_shape=jax.ShapeDtypeStruct(give.shape, give.dtype),
        scratch_shapes=[pltpu.SemaphoreType.DMA((1,)), pltpu.SemaphoreType.DMA((1,))],
    )(give)


def _chip_exchange(parts):
    def body(p_ref, got_ref, send_sems, recv_sems):
        _, _, c, chip, others = _mesh_place()
        sends = [_remote(p_ref.at[2 * ox + oy], got_ref.at[chip], send_sems, recv_sems, r, (ox, oy, c))
                 for r, (ox, oy) in enumerate(others)]
        for cp in sends:
            cp.start()
        for r, (ox, oy) in enumerate(others):
            _remote(p_ref.at[chip], got_ref.at[2 * ox + oy], send_sems, recv_sems, r, (ox, oy, c)).wait_recv()
        for cp in sends:
            cp.wait_send()

    got = pl.pallas_call(
        body, name="chip_exchange", in_specs=[_any_spec()], out_specs=_any_spec(),
        out_shape=jax.ShapeDtypeStruct(parts.shape, parts.dtype),
        scratch_shapes=[pltpu.SemaphoreType.DMA((3,)), pltpu.SemaphoreType.DMA((3,))],
    )(parts)
    chip = 2 * lax.axis_index("x") + lax.axis_index("y")
    own = lax.dynamic_index_in_dim(parts, chip, 0, keepdims=False)
    return lax.dynamic_update_index_in_dim(got, own, chip, 0)


def _pair_gather(half):
    def body(h_ref, out_ref, send_sems, recv_sems):
        x, y, c, _, _ = _mesh_place()
        cp = _remote(h_ref, out_ref.at[c], send_sems, recv_sems, 0, (x, y, 1 - c))
        cp.start()
        _remote(h_ref, out_ref.at[1 - c], send_sems, recv_sems, 0, (x, y, 1 - c)).wait_recv()
        cp.wait_send()

    both = pl.pallas_call(
        body, name="pair_gather", in_specs=[_any_spec()], out_specs=_any_spec(),
        out_shape=jax.ShapeDtypeStruct((2,) + half.shape, half.dtype),
        scratch_shapes=[pltpu.SemaphoreType.DMA((1,)), pltpu.SemaphoreType.DMA((1,))],
    )(half)
    return lax.dynamic_update_index_in_dim(both, half, lax.axis_index("c"), 0)


def _row_tile(rows, limit=PACK_TILE):
    if rows <= limit:
        return rows
    for tr in range(limit, 7, -8):
        if rows % tr == 0:
            return tr
    return rows


def _pair_add(keep, got):
    n, rh, cols = keep.shape
    tr = _row_tile(rh)

    def body(k_ref, g_ref, o_ref):
        o_ref[...] = (k_ref[...].astype(F32) + g_ref[...].astype(F32)).astype(BF16)

    spec = pl.BlockSpec((1, tr, cols), lambda j, i: (j, i, 0))
    return pl.pallas_call(
        body, name="pair_add", grid=(n, rh // tr), in_specs=[spec, spec], out_specs=spec,
        out_shape=jax.ShapeDtypeStruct(keep.shape, BF16),
        compiler_params=_params(("parallel", "parallel")),
    )(keep, got)


def _chip_sum(parts):
    n, rh, cols = parts.shape
    tr = _row_tile(rh)

    def body(p_ref, o_ref):
        total = p_ref[0].astype(F32)
        for k in range(1, n):
            total = total + p_ref[k].astype(F32)
        o_ref[...] = total

    return pl.pallas_call(
        body, name="chip_sum", grid=(rh // tr,),
        in_specs=[pl.BlockSpec((n, tr, cols), lambda i: (0, i, 0))], out_specs=pl.BlockSpec((tr, cols), lambda i: (i, 0)),
        out_shape=jax.ShapeDtypeStruct((rh, cols), F32),
        compiler_params=_params(("parallel",)),
    )(parts)


def _reduce_scatter(grads, c):
    keep = lax.dynamic_index_in_dim(grads, c, axis=1, keepdims=False)
    give = lax.dynamic_index_in_dim(grads, 1 - c, axis=1, keepdims=False)
    chip_partial = _pair_add(keep, _pair_exchange(give))
    return _pair_gather(_chip_sum(_chip_exchange(chip_partial)))


def _all_reduce_small(meta_rows, small):
    b, rm, cols = meta_rows.shape
    rows = rm + small.shape[0]

    def body(meta_ref, small_ref, out_ref, mine, pair_buf, chip_buf, send_sems, recv_sems):
        x, y, c, chip, others = _mesh_place()
        acc = meta_ref[0]
        for i in range(1, b):
            acc = acc + meta_ref[i]
        mine[0:rm, :] = acc
        mine[rm:rows, :] = small_ref[...]
        pair = _remote(mine, pair_buf, send_sems, recv_sems, 0, (x, y, 1 - c))
        pair.start()
        pair.wait()
        chip_buf[chip] = mine[...] + pair_buf[...]
        sends = [_remote(chip_buf.at[chip], chip_buf.at[chip], send_sems, recv_sems, 1 + r, (ox, oy, c))
                 for r, (ox, oy) in enumerate(others)]
        for cp in sends:
            cp.start()
        for r, (ox, oy) in enumerate(others):
            _remote(chip_buf.at[chip], chip_buf.at[2 * ox + oy], send_sems, recv_sems, 1 + r, (ox, oy, c)).wait_recv()
        for cp in sends:
            cp.wait_send()
        out_ref[...] = ((chip_buf[0] + chip_buf[1]) + chip_buf[2]) + chip_buf[3]

    return pl.pallas_call(
        body, name="all_reduce_small",
        in_specs=[_vmem_spec(), _vmem_spec()], out_specs=_vmem_spec(),
        out_shape=jax.ShapeDtypeStruct((rows, cols), F32),
        scratch_shapes=[pltpu.VMEM((rows, cols), F32), pltpu.VMEM((rows, cols), F32), pltpu.VMEM((N_CHIPS, rows, cols), F32),
                        pltpu.SemaphoreType.DMA((4,)), pltpu.SemaphoreType.DMA((4,))],
        compiler_params=pltpu.CompilerParams(vmem_limit_bytes=VMEM_LIMIT),
    )(meta_rows, small)


def _adamw(w, g, m, v):
    shape = w.shape
    cols = shape[-1]
    rows = w.size // cols
    tr = _row_tile(rows)

    def body(w_ref, g_ref, m_ref, v_ref, d_ref, m2_ref, v2_ref):
        grad = g_ref[...]
        m2 = ADAM_B1 * m_ref[...] + (1.0 - ADAM_B1) * grad
        v2 = ADAM_B2 * v_ref[...] + (1.0 - ADAM_B2) * jnp.square(grad)
        m_hat = m2 / (1.0 - ADAM_B1 ** ADAM_STEP)
        v_hat = v2 / (1.0 - ADAM_B2 ** ADAM_STEP)
        d_ref[...] = -ADAM_LR * (m_hat / (jnp.sqrt(v_hat) + ADAM_EPS) + ADAM_WD * w_ref[...])
        m2_ref[...] = m2
        v2_ref[...] = v2

    spec = pl.BlockSpec((tr, cols), lambda i: (i, 0))
    out = jax.ShapeDtypeStruct((rows, cols), F32)
    res = pl.pallas_call(
        body, name="adamw", grid=(rows // tr,), in_specs=[spec] * 4, out_specs=[spec] * 3, out_shape=[out] * 3,
        compiler_params=_params(("parallel",)),
    )(*(a.reshape(rows, cols) for a in (w, g, m, v)))
    return tuple(r.reshape(shape) for r in res)


def _pack_rows(arrays):
    flat = [a.reshape(-1, PACK_COLS) for a in arrays]
    counts = [f.shape[0] for f in flat]
    total = sum(counts)
    half = -(-total // 2)
    tiles = -(-half // PACK_TILE)
    padded = 2 * tiles * _round_up(-(-half // tiles), 16)
    if padded > total:
        flat.append(jnp.zeros((padded - total, PACK_COLS), flat[0].dtype))
    return jnp.concatenate(flat, axis=0), counts


def _unpack_rows(buffer, counts, shapes):
    out, start = [], 0
    for n, shape in zip(counts, shapes):
        out.append(buffer[..., start:start + n, :].reshape(buffer.shape[:-2] + tuple(shape)))
        start += n
    return out


def _group_codec(entries, weights):
    turned = [n in TRANSPOSED_WEIGHTS for n, _ in entries]
    shapes = [weights[n].shape[1:][::-1] if t else weights[n].shape[1:] for (n, _), t in zip(entries, turned)]
    by_rows = [t or SHARD_AXIS[n] == 1 for (n, _), t in zip(entries, turned)]
    packed, counts = _pack_rows([(weights[n][li].T if t else weights[n][li]).astype(BF16) for (n, li), t in zip(entries, turned)])
    rows = packed.shape[0]
    pad_rows = rows - sum(counts)

    def unpack(per_chip_packed):
        out = {}
        for entry, (s0, s1), rowwise, blk in zip(entries, shapes, by_rows, _unpack_rows(per_chip_packed, counts, shapes)):
            out[entry] = blk.reshape(N_CHIPS * s0, s1) if rowwise else jnp.transpose(blk, (1, 0, 2)).reshape(s0, N_CHIPS * s1)
        return out

    def pack_grads(whole):
        pieces = []
        for entry, (s0, s1), rowwise in zip(entries, shapes, by_rows):
            g = whole[entry]
            by_chip = g.reshape(N_CHIPS, s0, s1) if rowwise else jnp.transpose(g.reshape(s0, N_CHIPS, s1), (1, 0, 2))
            pieces.append(by_chip.reshape(N_CHIPS, -1, PACK_COLS))
        if pad_rows:
            pieces.append(jnp.zeros((N_CHIPS, pad_rows, PACK_COLS), BF16))
        return jnp.concatenate(pieces, axis=1).reshape(N_CHIPS, 2, rows // 2, PACK_COLS)

    def unpack_reduced(reduced):
        shards = _unpack_rows(reduced.reshape(rows, PACK_COLS), counts, shapes)
        return {entry: s.T if t else s for entry, s, t in zip(entries, shards, turned)}

    return dict(packed=packed, unpack=unpack, pack_grads=pack_grads, unpack_reduced=unpack_reduced)


def kernel(x, meta_tokens, norm_mix_g, w_in, pool_w, pool_scale, q_norm_g, kv_norm_g, w_uq, w_ukv, w_pa, w_pb, w_o, norm_ffn_g, w_gate, w_up, w_down, final_norm_g, loss_target, m_meta_tokens, m_norm_mix_g, m_w_in, m_pool_w, m_pool_scale, m_q_norm_g, m_kv_norm_g, m_w_uq, m_w_ukv, m_w_pa, m_w_pb, m_w_o, m_norm_ffn_g, m_w_gate, m_w_up, m_w_down, m_final_norm_g, v_meta_tokens, v_norm_mix_g, v_w_in, v_pool_w, v_pool_scale, v_q_norm_g, v_kv_norm_g, v_w_uq, v_w_ukv, v_w_pa, v_w_pb, v_w_o, v_norm_ffn_g, v_w_gate, v_w_up, v_w_down, v_final_norm_g):
    weights = dict(meta_tokens=meta_tokens, norm_mix_g=norm_mix_g, w_in=w_in, pool_w=pool_w, pool_scale=pool_scale,
                   q_norm_g=q_norm_g, kv_norm_g=kv_norm_g, w_uq=w_uq, w_ukv=w_ukv, w_pa=w_pa, w_pb=w_pb, w_o=w_o,
                   norm_ffn_g=norm_ffn_g, w_gate=w_gate, w_up=w_up, w_down=w_down, final_norm_g=final_norm_g)
    first = dict(meta_tokens=m_meta_tokens, norm_mix_g=m_norm_mix_g, w_in=m_w_in, pool_w=m_pool_w, pool_scale=m_pool_scale,
                 q_norm_g=m_q_norm_g, kv_norm_g=m_kv_norm_g, w_uq=m_w_uq, w_ukv=m_w_ukv, w_pa=m_w_pa, w_pb=m_w_pb, w_o=m_w_o,
                 norm_ffn_g=m_norm_ffn_g, w_gate=m_w_gate, w_up=m_w_up, w_down=m_w_down, final_norm_g=m_final_norm_g)
    second = dict(meta_tokens=v_meta_tokens, norm_mix_g=v_norm_mix_g, w_in=v_w_in, pool_w=v_pool_w, pool_scale=v_pool_scale,
                  q_norm_g=v_q_norm_g, kv_norm_g=v_kv_norm_g, w_uq=v_w_uq, w_ukv=v_w_ukv, w_pa=v_w_pa, w_pb=v_w_pb, w_o=v_w_o,
                  norm_ffn_g=v_norm_ffn_g, w_gate=v_w_gate, w_up=v_w_up, w_down=v_w_down, final_norm_g=v_final_norm_g)
    core = lax.axis_index("c")
    chip = 2 * lax.axis_index("x") + lax.axis_index("y")
    d = x.shape[-1]
    meta_cols = meta_tokens.shape[1]

    early_first = _group_codec([(n, 0) for n in EARLY_WEIGHTS], weights)
    riding = [_group_codec([(n, 0) for n in LATE_WEIGHTS] + [(n, 1) for n in EARLY_WEIGHTS], weights),
              _group_codec([(n, 1) for n in LATE_WEIGHTS], weights)]
    gathered, meta_all = _all_gather_shards(early_first["packed"].reshape(2, -1, PACK_COLS), meta_tokens)
    early_weights = early_first["unpack"](gathered.reshape(N_CHIPS, -1, PACK_COLS))
    meta_full = jnp.concatenate([meta_all[j] for j in range(N_CHIPS)], axis=1)
    small = {n: weights[n] for n in SMALL_WEIGHTS}

    loss, grad_x, d_meta_rows, g_small, early_partial, parts = _local_step(
        x, loss_target, meta_full, small, {n: early_weights[n, 0] for n in EARLY_WEIGHTS}, riding, early_first)

    shards = early_first["unpack_reduced"](_reduce_scatter(early_partial, core))
    for codec, from_all in zip(riding, parts):
        shards.update(codec["unpack_reduced"](_pair_gather(_chip_sum(from_all))))
    grads = {n: jnp.stack([shards[n, 0], shards[n, 1]]) for n in BIG_WEIGHTS}

    small_shapes = [weights[n].shape for n in SMALL_WEIGHTS]
    small_flat = jnp.concatenate([g_small[n].reshape(-1) for n in SMALL_WEIGHTS])
    small_len = small_flat.shape[0]
    small_rows = _round_up(-(-small_len // PACK_COLS), 8)
    small_pack = jnp.pad(small_flat, (0, small_rows * PACK_COLS - small_len)).reshape(small_rows, PACK_COLS)
    meta_rows = N_META * d // PACK_COLS
    summed = _all_reduce_small(d_meta_rows.reshape(-1, meta_rows, PACK_COLS), small_pack)
    grad_meta_full = summed[:meta_rows].reshape(N_META, d)
    grads["meta_tokens"] = lax.dynamic_slice_in_dim(grad_meta_full, chip * meta_cols, meta_cols, axis=1)
    small_sum = summed[meta_rows:].reshape(-1)
    start = 0
    for n, shape in zip(SMALL_WEIGHTS, small_shapes):
        size = 1
        for s in shape:
            size *= s
        grads[n] = small_sum[start:start + size].reshape(shape)
        start += size

    deltas, new_m, new_v = {}, {}, {}
    for n in WEIGHT_ORDER:
        deltas[n], new_m[n], new_v[n] = _adamw(weights[n], grads[n], first[n], second[n])

    total_loss = lax.psum(loss[0, 0], ("x", "y", "c"))
    return (total_loss, grad_x, *[grads[n] for n in WEIGHT_ORDER], *[deltas[n] for n in WEIGHT_ORDER],
            *[new_m[n] for n in WEIGHT_ORDER], *[new_v[n] for n in WEIGHT_ORDER])
```

```python
import functools

import jax
import jax.numpy as jnp
from jax import lax
from jax.experimental import pallas as pl
from jax.experimental.pallas import tpu as pltpu

F32 = jnp.float32
BF16 = jnp.bfloat16

N_META = 16
POOL_WINDOWS = (2, 4, 8, 16)
POOL_GROUP = 128
POOL_WIDTH = POOL_GROUP * len(POOL_WINDOWS)
QK_NOPE = 64
QK_ROPE = 32
V_DIM = 64
QK_DIM = QK_NOPE + QK_ROPE
Q_RANK = 256
KV_RANK = 128
HEAD_PAD = 128
SM_SCALE = QK_DIM ** -0.5
ROPE_THETA = 10000.0
NORM_EPS = 1e-6
MASK_VALUE = -1e30
Z_FIXED = POOL_WIDTH + Q_RANK + KV_RANK + HEAD_PAD

ADAM_LR = 0.001
ADAM_B1 = 0.9
ADAM_B2 = 0.999
ADAM_EPS = 1e-08
ADAM_WD = 0.01
ADAM_STEP = 10

N_CHIPS = 4
ATT_BLOCK = 256
SEQ_PAD = 128
ATT_Q_ROWS = 256
ATT_FWD_HEADS = 8
ATT_BWD_HEADS = 4
PACK_COLS = 1024
PACK_TILE = 512
VMEM_LIMIT = 60 * 1024 * 1024
MXU_DEPTH = 256
ACC_BYTES = 8 * 1024 * 1024

MESH = pl.DeviceIdType.MESH

BIG_WEIGHTS = ("w_in", "w_uq", "w_ukv", "w_pa", "w_pb", "w_o", "w_gate", "w_up", "w_down")
EARLY_WEIGHTS = ("w_in", "w_uq", "w_ukv")
LATE_WEIGHTS = ("w_pa", "w_pb", "w_o", "w_gate", "w_up", "w_down")
TRANSPOSED_WEIGHTS = ("w_in", "w_gate", "w_up")
SHARD_AXIS = {"w_in": 2, "w_uq": 2, "w_ukv": 2, "w_pa": 2, "w_pb": 1, "w_o": 1, "w_gate": 2, "w_up": 2, "w_down": 1}
SMALL_WEIGHTS = ("norm_mix_g", "pool_w", "pool_scale", "q_norm_g", "kv_norm_g", "norm_ffn_g", "final_norm_g")
WEIGHT_ORDER = ("meta_tokens", "norm_mix_g", "w_in", "pool_w", "pool_scale", "q_norm_g", "kv_norm_g", "w_uq", "w_ukv",
                "w_pa", "w_pb", "w_o", "norm_ffn_g", "w_gate", "w_up", "w_down", "final_norm_g")


def _round_up(n, m):
    return -(-n // m) * m


def _vmem_spec():
    return pl.BlockSpec(memory_space=pltpu.VMEM)


def _any_spec():
    return pl.BlockSpec(memory_space=pl.ANY)


def _row_block(tm, width, col_block=0):
    return pl.BlockSpec((tm, width), lambda i, cb=col_block: (i, cb))


def _params(sem, vmem=VMEM_LIMIT):
    return pltpu.CompilerParams(dimension_semantics=sem, vmem_limit_bytes=vmem)


def _token_tile(t, want):
    best = SEQ_PAD
    for tm in range(32, min(t, 2 * want) + 1, 32):
        if t % tm == 0 and abs(tm - want) < abs(best - want):
            best = tm
    return best


def _dot(a, b):
    return jnp.dot(a, b, preferred_element_type=F32)


def _dot_nt(a, b):
    return lax.dot_general(a, b, (((1,), (1,)), ((), ())), preferred_element_type=F32)


def _dot_tn(a, b):
    return lax.dot_general(a, b, (((0,), (0,)), ((), ())), preferred_element_type=F32)


def _rms_fwd(x, g):
    r = lax.rsqrt(jnp.mean(x * x, axis=-1, keepdims=True) + NORM_EPS)
    xh = x * r
    return xh * g, xh, r


def _rms_bwd(dy, xh, r, g):
    gdy = dy * g
    dx = r * (gdy - xh * jnp.mean(xh * gdy, axis=-1, keepdims=True))
    return dx, dy * xh


def _rope_fwd(x, c, sa, sb):
    return x * c + pltpu.roll(x, 16, 1) * sa + pltpu.roll(x, HEAD_PAD - 16, 1) * sb


def _rope_bwd(d, c, sa, sb):
    return d * c + pltpu.roll(d * sa, HEAD_PAD - 16, 1) + pltpu.roll(d * sb, 16, 1)


def _in_proj_fwd(h, g, w_in_pt):
    t, d = h.shape
    nz = w_in_pt.shape[0]
    tm = _token_tile(t, 512)

    def body(h_ref, g_ref, w_ref, z_ref):
        hn, _, _ = _rms_fwd(h_ref[...], g_ref[...])
        z_ref[...] = _dot_nt(hn.astype(BF16), w_ref[...]).astype(BF16)

    return pl.pallas_call(
        body, name="in_proj_fwd", grid=(t // tm,),
        in_specs=[_row_block(tm, d), _vmem_spec(), _vmem_spec()],
        out_specs=_row_block(tm, nz),
        out_shape=jax.ShapeDtypeStruct((t, nz), BF16),
        compiler_params=_params(("parallel",)),
    )(h, g, w_in_pt)


def _window_sum(x, w, row, forward):
    n = x.shape[0]
    s = x
    k = 1
    while k < w:
        if forward:
            s = s + jnp.where(row >= k, pltpu.roll(s, k, 0), 0.0)
        else:
            s = s + jnp.where(row < n - k, pltpu.roll(s, n - k, 0), 0.0)
        k *= 2
    return s


def _pool_fwd(z3, pool_w, pool_scale):
    b, lp, _ = z3.shape

    def body(u_ref, pw_ref, sc_ref, a_ref):
        row = lax.broadcasted_iota(jnp.int32, (lp, POOL_GROUP), 0)
        pos = row.astype(F32)
        for gi, w in enumerate(POOL_WINDOWS):
            cols = slice(gi * POOL_GROUP, (gi + 1) * POOL_GROUP)
            u = u_ref[0, :, cols].astype(F32)
            y = _window_sum(u, w, row, True) / jnp.minimum(pos + 1.0, float(w)) - u
            yw = _dot(y.astype(BF16), pw_ref[gi])
            a_ref[0, :, cols] = (yw * sc_ref[:, cols]).astype(BF16)

    return pl.pallas_call(
        body, name="pool_fwd", grid=(b,),
        in_specs=[pl.BlockSpec((1, lp, POOL_WIDTH), lambda i: (i, 0, 0)), _vmem_spec(), _vmem_spec()],
        out_specs=pl.BlockSpec((1, lp, POOL_WIDTH), lambda i: (i, 0, 0)),
        out_shape=jax.ShapeDtypeStruct((b, lp, POOL_WIDTH), BF16),
        compiler_params=_params(("parallel",)),
    )(z3, pool_w, pool_scale)


def _qkv_fwd(z, g_q, g_kv, w_uq_p, w_kv_p, rope_c, rope_sa, rope_sb):
    t = z.shape[0]
    hw = w_uq_p.shape[1]
    heads = hw // HEAD_PAD
    tm = _token_tile(t, 512)

    def body(cq_ref, ckv_ref, kr_ref, gq_ref, gkv_ref, wq_ref, wkv_ref, c_ref, sa_ref, sb_ref, q_ref, k_ref, v_ref):
        c, sa, sb = c_ref[...], sa_ref[...], sb_ref[...]
        cqn, _, _ = _rms_fwd(cq_ref[...].astype(F32), gq_ref[...])
        qraw = _dot(cqn.astype(BF16), wq_ref[...])
        ckvn, _, _ = _rms_fwd(ckv_ref[...].astype(F32), gkv_ref[...])
        kvraw = _dot(ckvn.astype(BF16), wkv_ref[...])
        kr = _rope_fwd(kr_ref[...].astype(F32), c, sa, sb)
        for hd in range(heads):
            cols = slice(hd * HEAD_PAD, (hd + 1) * HEAD_PAD)
            q_ref[:, cols] = (_rope_fwd(qraw[:, cols], c, sa, sb) * SM_SCALE).astype(BF16)
            k_ref[:, cols] = (kvraw[:, cols] + kr).astype(BF16)
        lane = lax.broadcasted_iota(jnp.int32, (tm, hw), 1)
        v_ref[...] = jnp.where((lane & (HEAD_PAD - 1)) == V_DIM, 1.0, kvraw[:, hw:]).astype(BF16)

    out = jax.ShapeDtypeStruct((t, hw), BF16)
    return pl.pallas_call(
        body, name="qkv_fwd", grid=(t // tm,),
        in_specs=[_row_block(tm, Q_RANK, POOL_WIDTH // Q_RANK),
                  _row_block(tm, KV_RANK, (POOL_WIDTH + Q_RANK) // KV_RANK),
                  _row_block(tm, HEAD_PAD, (POOL_WIDTH + Q_RANK + KV_RANK) // HEAD_PAD),
                  _vmem_spec(), _vmem_spec(), _vmem_spec(), _vmem_spec(),
                  _row_block(tm, HEAD_PAD), _row_block(tm, HEAD_PAD), _row_block(tm, HEAD_PAD)],
        out_specs=[_row_block(tm, hw)] * 3,
        out_shape=[out, out, out],
        compiler_params=_params(("parallel",)),
    )(z, z, z, g_q, g_kv, w_uq_p, w_kv_p, rope_c, rope_sa, rope_sb)


def _heads_per_step(heads, want):
    while heads % want:
        want //= 2
    return want


def _causal_mask(rows):
    row = lax.broadcasted_iota(jnp.int32, (rows, rows), 0)
    col = lax.broadcasted_iota(jnp.int32, (rows, rows), 1)
    return col <= row


def _attn_blocks(real_len):
    tail_start = (-(-real_len // ATT_BLOCK) - 1) * ATT_BLOCK
    return tail_start // ATT_BLOCK, tail_start, _round_up(real_len - tail_start, SEQ_PAD)


def _call_with_exchange(body, exchange, *, name, grid, in_specs, out_specs, out_shape, scratch_shapes, operands):
    if exchange is None:
        return pl.pallas_call(body, name=name, grid=grid, in_specs=in_specs, out_specs=out_specs, out_shape=out_shape,
                              scratch_shapes=scratch_shapes,
                              compiler_params=_params(("parallel",) + ("arbitrary",) * (len(grid) - 1)))(*operands)
    n_in, n_out, n_scratch = len(in_specs), len(out_specs), len(scratch_shapes)

    def riding(*refs):
        ins, src = refs[:n_in], refs[n_in]
        outs, dst = refs[n_in + 1:n_in + 1 + n_out], refs[n_in + 1 + n_out]
        scratch = refs[n_in + 2 + n_out:n_in + 2 + n_out + n_scratch]
        send_sems, recv_sems = refs[n_in + 2 + n_out + n_scratch:]
        steps = [pl.program_id(a) for a in range(len(grid))]

        last = functools.reduce(jnp.logical_and, [s == g - 1 for s, g in zip(steps, grid)])

        @pl.when(functools.reduce(jnp.logical_and, [s == 0 for s in steps]))
        def _():
            exchange["start"](src, dst, send_sems, recv_sems)

        if "pass_on" in exchange:
            @pl.when(last)
            def _():
                exchange["pass_on"](src, dst, send_sems, recv_sems)

        body(*ins, *outs, *scratch)

        @pl.when(last)
        def _():
            exchange["finish"](src, dst, send_sems, recv_sems)

    n = exchange["copies"]
    return pl.pallas_call(
        riding, name=name + "_" + exchange["name"], grid=grid,
        in_specs=list(in_specs) + [_any_spec()], out_specs=list(out_specs) + [_any_spec()],
        out_shape=list(out_shape) + [exchange["out_shape"]],
        scratch_shapes=list(scratch_shapes) + [pltpu.SemaphoreType.DMA((n,)), pltpu.SemaphoreType.DMA((n,))],
        compiler_params=_params(("arbitrary",) * len(grid)),
    )(*operands, exchange["operand"])


def _attn_fwd(q3, k3, v3, real_len, exchange=None):
    b, lp, hw = q3.shape
    heads = hw // HEAD_PAD
    tb = ATT_BLOCK
    nfull, tail_start, tail = _attn_blocks(real_len)
    done = tail_start + tail
    hpg = _heads_per_step(heads, ATT_FWD_HEADS)
    width = hpg * HEAD_PAD

    def body(q_ref, k_ref, v_ref, o_ref, lse_ref):
        group = pl.program_id(1)

        @pl.when(group == 0)
        def _():
            lse_ref[...] = jnp.zeros_like(lse_ref)

        def q_rows(r0, rows, whole_kv_blocks, back):
            def kv_step(c0, keys, states, mask):
                out = []
                for hd, (m, acc) in enumerate(states):
                    cols = slice(hd * HEAD_PAD, (hd + 1) * HEAD_PAD)
                    s = _dot_nt(q_ref[0, pl.ds(r0, rows), cols], k_ref[0, pl.ds(c0, keys), cols])
                    if mask is not None:
                        s = jnp.where(mask, s, MASK_VALUE)
                    m_new = jnp.maximum(m, jnp.max(s, axis=-1, keepdims=True))
                    p = jnp.exp((s - m_new).astype(BF16))
                    acc = jnp.exp(m - m_new) * acc + _dot(p, v_ref[0, pl.ds(c0, keys), cols])
                    out.append((m_new, acc))
                return tuple(out)

            init = tuple((jnp.full((rows, 1), MASK_VALUE, F32), jnp.zeros((rows, HEAD_PAD), F32)) for _ in range(hpg))
            states = lax.fori_loop(0, whole_kv_blocks, lambda j, st: kv_step(pl.multiple_of(j * tb, tb), tb, st, None), init)
            query = lax.broadcasted_iota(jnp.int32, (rows, back + rows), 0)
            key = lax.broadcasted_iota(jnp.int32, (rows, back + rows), 1)
            states = kv_step(pl.multiple_of(r0 - back, SEQ_PAD), back + rows, states, key <= query + back)
            lane = lax.broadcasted_iota(jnp.int32, (rows, HEAD_PAD), 1)
            lse_rows = lse_ref[0, pl.ds(r0, rows), :]
            for hd, (m, acc) in enumerate(states):
                l = jnp.sum(jnp.where(lane == V_DIM, acc, 0.0), axis=-1, keepdims=True)
                o_ref[0, pl.ds(r0, rows), hd * HEAD_PAD:(hd + 1) * HEAD_PAD] = (acc / l).astype(BF16)
                lse_rows = jnp.where(lane == group * hpg + hd, m + jnp.log(l), lse_rows)
            lse_ref[0, pl.ds(r0, rows), :] = lse_rows

        def whole_block(i, carry):
            for back in range(0, tb, ATT_Q_ROWS):
                q_rows(pl.multiple_of(i * tb + back, ATT_Q_ROWS), ATT_Q_ROWS, i, back)
            return carry

        lax.fori_loop(0, nfull, whole_block, 0)
        for back in range(0, tail, ATT_Q_ROWS):
            q_rows(tail_start + back, min(ATT_Q_ROWS, tail - back), nfull, back)
        if done < lp:
            o_ref[0, done:lp, :] = jnp.zeros((lp - done, width), BF16)

    head_spec = pl.BlockSpec((1, lp, width), lambda bi, hi: (bi, 0, hi))
    return _call_with_exchange(
        body, exchange, name="attn_fwd", grid=(b, heads // hpg),
        in_specs=[head_spec, head_spec, head_spec],
        out_specs=[head_spec, pl.BlockSpec((1, lp, HEAD_PAD), lambda bi, hi: (bi, 0, 0))],
        out_shape=[jax.ShapeDtypeStruct((b, lp, hw), BF16), jax.ShapeDtypeStruct((b, lp, HEAD_PAD), F32)],
        scratch_shapes=[], operands=(q3, k3, v3))


def _merge_fwd(h, z, a, o, w_pa, w_pb_p, w_o):
    t, d = h.shape
    hw = o.shape[1]
    tm = _token_tile(t, 512)
    gate_block = Z_FIXED // d

    def body(h_ref, ga_ref, gb_ref, a_ref, o_ref, wpa_ref, wpb_ref, wo_ref, h1_ref, pa_ref, pb_ref):
        pa = _dot(a_ref[...], wpa_ref[...])
        pb = _dot(o_ref[...], wpb_ref[...])
        merged = jax.nn.sigmoid(ga_ref[...].astype(F32)) * pa + jax.nn.sigmoid(gb_ref[...].astype(F32)) * pb
        h1_ref[...] = h_ref[...] + _dot(merged.astype(BF16), wo_ref[...])
        pa_ref[...] = pa.astype(BF16)
        pb_ref[...] = pb.astype(BF16)

    return pl.pallas_call(
        body, name="merge_fwd", grid=(t // tm,),
        in_specs=[_row_block(tm, d), _row_block(tm, d, gate_block), _row_block(tm, d, gate_block + 1),
                  _row_block(tm, POOL_WIDTH), _row_block(tm, hw), _vmem_spec(), _vmem_spec(), _vmem_spec()],
        out_specs=[_row_block(tm, d)] * 3,
        out_shape=[jax.ShapeDtypeStruct((t, d), F32), jax.ShapeDtypeStruct((t, d), BF16), jax.ShapeDtypeStruct((t, d), BF16)],
        compiler_params=_params(("parallel",)),
    )(h, z, z, a, o, w_pa, w_pb_p, w_o)


def _ffn_fwd(h, g, w_gate_t, w_up_t, w_down):
    t, d = h.shape
    f = w_gate_t.shape[0]
    tm = _token_tile(t, 256)

    def body(h_ref, g_ref, wg_ref, wu_ref, wd_ref, h2_ref, a_ref, b_ref):
        x = h_ref[...]
        hn, _, _ = _rms_fwd(x, g_ref[...])
        hn = hn.astype(BF16)
        ga = _dot_nt(hn, wg_ref[...])
        up = _dot_nt(hn, wu_ref[...])
        act = ga * jax.nn.sigmoid(ga) * up
        h2_ref[...] = x + _dot(act.astype(BF16), wd_ref[...])
        a_ref[...] = ga.astype(BF16)
        b_ref[...] = up.astype(BF16)

    return pl.pallas_call(
        body, name="ffn_fwd", grid=(t // tm,),
        in_specs=[_row_block(tm, d), _vmem_spec(), _vmem_spec(), _vmem_spec(), _vmem_spec()],
        out_specs=[_row_block(tm, d), _row_block(tm, f), _row_block(tm, f)],
        out_shape=[jax.ShapeDtypeStruct((t, d), F32), jax.ShapeDtypeStruct((t, f), BF16), jax.ShapeDtypeStruct((t, f), BF16)],
        compiler_params=_params(("parallel",)),
    )(h, g, w_gate_t, w_up_t, w_down)


def _loss_head(h, g, target, valid):
    t, d = h.shape
    tm = _token_tile(t, 512)

    def body(h_ref, g_ref, t_ref, valid_ref, dh_ref, loss_ref, dg_ref):
        @pl.when(pl.program_id(0) == 0)
        def _():
            loss_ref[...] = jnp.zeros_like(loss_ref)
            dg_ref[...] = jnp.zeros_like(dg_ref)

        gain = g_ref[...]
        y, xh, r = _rms_fwd(h_ref[...], gain)
        err = (y - t_ref[...]) * valid_ref[...]
        per_row = jnp.sum(err * err, axis=-1, keepdims=True) / d
        loss_ref[...] += 0.5 * jnp.sum(per_row, axis=0, keepdims=True)
        dx, dg_rows = _rms_bwd(err / d, xh, r, gain)
        dh_ref[...] = dx
        dg_ref[...] += jnp.sum(dg_rows, axis=0, keepdims=True)

    return pl.pallas_call(
        body, name="loss_head", grid=(t // tm,),
        in_specs=[_row_block(tm, d), _vmem_spec(), _row_block(tm, d), _row_block(tm, 1)],
        out_specs=[_row_block(tm, d), pl.BlockSpec((1, 1), lambda i: (0, 0)), pl.BlockSpec((1, d), lambda i: (0, 0))],
        out_shape=[jax.ShapeDtypeStruct((t, d), F32), jax.ShapeDtypeStruct((1, 1), F32), jax.ShapeDtypeStruct((1, d), F32)],
        compiler_params=_params(("arbitrary",)),
    )(h, g, target, valid)


def _weight_grad(x, y, name):
    t, k = x.shape
    n = y.shape[1]
    tm = _token_tile(t, 8 * MXU_DEPTH)
    tk, tn = k, n
    while tk * tn * 4 > ACC_BYTES and max(tk, tn) % 256 == 0:
        if tk > tn:
            tk //= 2
        else:
            tn //= 2
    steps = t // tm

    def body(x_ref, y_ref, o_ref, acc):
        @pl.when(pl.program_id(2) == 0)
        def _():
            acc[...] = jnp.zeros_like(acc)

        acc[...] += _dot_tn(x_ref[...].astype(BF16), y_ref[...].astype(BF16))

        @pl.when(pl.program_id(2) == steps - 1)
        def _():
            o_ref[...] = acc[...].astype(BF16)

    return pl.pallas_call(
        body, name=name, grid=(k // tk, n // tn, steps),
        in_specs=[pl.BlockSpec((tm, tk), lambda a, b, i: (i, a)), pl.BlockSpec((tm, tn), lambda a, b, i: (i, b))],
        out_specs=pl.BlockSpec((tk, tn), lambda a, b, i: (a, b)),
        out_shape=jax.ShapeDtypeStruct((k, n), BF16),
        scratch_shapes=[pltpu.VMEM((tk, tn), F32)],
        compiler_params=_params(("parallel", "parallel", "arbitrary")),
    )(x, y)


def _ffn_bwd(h, dh2, a, b, g, w_gate_t, w_up_t, w_down):
    t, d = h.shape
    f = a.shape[1]
    tm = _token_tile(t, 256)

    def body(h_ref, dh2_ref, a_ref, b_ref, g_ref, wg_ref, wu_ref, wd_ref, dh_ref, hn_ref, act_ref, da_ref, db_ref, dg_ref):
        @pl.when(pl.program_id(0) == 0)
        def _():
            dg_ref[...] = jnp.zeros_like(dg_ref)

        gain = g_ref[...]
        hn, xh, r = _rms_fwd(h_ref[...], gain)
        hn_ref[...] = hn.astype(BF16)
        dh2 = dh2_ref[...]
        dact = _dot_nt(dh2.astype(BF16), wd_ref[...])
        ga = a_ref[...].astype(F32)
        up = b_ref[...].astype(F32)
        sg = jax.nn.sigmoid(ga)
        silu = ga * sg
        act_ref[...] = (silu * up).astype(BF16)
        da = (dact * up * (sg * (1.0 + ga * (1.0 - sg)))).astype(BF16)
        db = (dact * silu).astype(BF16)
        da_ref[...] = da
        db_ref[...] = db
        dhn = _dot(da, wg_ref[...]) + _dot(db, wu_ref[...])
        dx, dg_rows = _rms_bwd(dhn, xh, r, gain)
        dh_ref[...] = dh2 + dx
        dg_ref[...] += jnp.sum(dg_rows, axis=0, keepdims=True)

    return pl.pallas_call(
        body, name="ffn_bwd", grid=(t // tm,),
        in_specs=[_row_block(tm, d), _row_block(tm, d), _row_block(tm, f), _row_block(tm, f),
                  _vmem_spec(), _vmem_spec(), _vmem_spec(), _vmem_spec()],
        out_specs=[_row_block(tm, d), _row_block(tm, d), _row_block(tm, f), _row_block(tm, f), _row_block(tm, f),
                   pl.BlockSpec((1, d), lambda i: (0, 0))],
        out_shape=[jax.ShapeDtypeStruct((t, d), F32), jax.ShapeDtypeStruct((t, d), BF16), jax.ShapeDtypeStruct((t, f), BF16),
                   jax.ShapeDtypeStruct((t, f), BF16), jax.ShapeDtypeStruct((t, f), BF16), jax.ShapeDtypeStruct((1, d), F32)],
        compiler_params=_params(("arbitrary",)),
    )(h, dh2, a, b, g, w_gate_t, w_up_t, w_down)


def _merge_bwd(dh1, z, pa, pb, w_o, w_pa, w_pb_p):
    t, d = dh1.shape
    hw = w_pb_p.shape[0]
    tm = _token_tile(t, 512)
    gate_block = Z_FIXED // d

    def body(dh_ref, ga_ref, gb_ref, pa_ref, pb_ref, wo_ref, wpa_ref, wpb_ref,
             mg_ref, dpa_ref, dpb_ref, dga_ref, dgb_ref, da_ref, do_ref):
        dm = _dot_nt(dh_ref[...].astype(BF16), wo_ref[...])
        sa = jax.nn.sigmoid(ga_ref[...].astype(F32))
        sb = jax.nn.sigmoid(gb_ref[...].astype(F32))
        pa = pa_ref[...].astype(F32)
        pb = pb_ref[...].astype(F32)
        mg_ref[...] = (sa * pa + sb * pb).astype(BF16)
        dpa = (dm * sa).astype(BF16)
        dpb = (dm * sb).astype(BF16)
        dpa_ref[...] = dpa
        dpb_ref[...] = dpb
        dga_ref[...] = (dm * pa * (sa * (1.0 - sa))).astype(BF16)
        dgb_ref[...] = (dm * pb * (sb * (1.0 - sb))).astype(BF16)
        da_ref[...] = _dot_nt(dpa, wpa_ref[...]).astype(BF16)
        do_ref[...] = _dot_nt(dpb, wpb_ref[...]).astype(BF16)

    wide = jax.ShapeDtypeStruct((t, d), BF16)
    return pl.pallas_call(
        body, name="merge_bwd", grid=(t // tm,),
        in_specs=[_row_block(tm, d), _row_block(tm, d, gate_block), _row_block(tm, d, gate_block + 1),
                  _row_block(tm, d), _row_block(tm, d), _vmem_spec(), _vmem_spec(), _vmem_spec()],
        out_specs=[_row_block(tm, d)] * 5 + [_row_block(tm, POOL_WIDTH), _row_block(tm, hw)],
        out_shape=[wide] * 5 + [jax.ShapeDtypeStruct((t, POOL_WIDTH), BF16), jax.ShapeDtypeStruct((t, hw), BF16)],
        compiler_params=_params(("parallel",)),
    )(dh1, z, z, pa, pb, w_o, w_pa, w_pb_p)


def _attn_bwd(q3, k3, v3, o3, do3, lse3, real_len, exchange=None):
    b, lp, hw = q3.shape
    heads = hw // HEAD_PAD
    tb = ATT_BLOCK
    nfull, tail_start, tail = _attn_blocks(real_len)
    done = tail_start + tail
    hpg = _heads_per_step(heads, ATT_BWD_HEADS)
    width = hpg * HEAD_PAD

    def body(q_ref, k_ref, v_ref, o_ref, do_ref, lse_ref, dq_ref, dk_ref, dv_ref, dqt_acc, lse_row, delta_row):
        group = pl.program_id(1)
        lse_t = jnp.transpose(lse_ref[0])
        head_of_row = lax.broadcasted_iota(jnp.int32, (HEAD_PAD, lp), 0)
        for hd in range(hpg):
            cols = slice(hd * HEAD_PAD, (hd + 1) * HEAD_PAD)
            lse_row[hd] = jnp.sum(jnp.where(head_of_row == group * hpg + hd, lse_t, 0.0), axis=0, keepdims=True)
            prod = do_ref[0, :, cols].astype(F32) * o_ref[0, :, cols].astype(F32)
            delta_row[hd] = jnp.sum(jnp.transpose(prod), axis=0, keepdims=True)
        dqt_acc[...] = jnp.zeros_like(dqt_acc)

        def kv_rows(c0, keys, whole_q_blocks_from):
            k_t = [jnp.transpose(k_ref[0, pl.ds(c0, keys), hd * HEAD_PAD:(hd + 1) * HEAD_PAD].astype(F32)).astype(BF16)
                   for hd in range(hpg)]

            def q_step(r0, rows, states, mask):
                out = []
                for hd, (dk, dv) in enumerate(states):
                    cols = slice(hd * HEAD_PAD, (hd + 1) * HEAD_PAD)
                    q = q_ref[0, pl.ds(r0, rows), cols]
                    do = do_ref[0, pl.ds(r0, rows), cols]
                    s_t = _dot_nt(k_ref[0, pl.ds(c0, keys), cols], q)
                    if mask is not None:
                        s_t = jnp.where(mask, s_t, MASK_VALUE)
                    p_t = jnp.exp(s_t - lse_row[hd, :, pl.ds(r0, rows)])
                    dp_t = _dot_nt(v_ref[0, pl.ds(c0, keys), cols], do)
                    ds_t = (p_t * (dp_t - delta_row[hd, :, pl.ds(r0, rows)])).astype(BF16)
                    dv = dv + _dot(p_t.astype(BF16), do)
                    dk = dk + _dot(ds_t, q)
                    dqt_acc[cols, pl.ds(r0, rows)] += _dot(k_t[hd], ds_t)
                    out.append((dk, dv))
                return tuple(out)

            zero = jnp.zeros((keys, HEAD_PAD), F32)
            key_pos = lax.broadcasted_iota(jnp.int32, (keys, keys), 0)
            query_pos = lax.broadcasted_iota(jnp.int32, (keys, keys), 1)
            states = q_step(c0, keys, tuple((zero, zero) for _ in range(hpg)), key_pos <= query_pos)
            if whole_q_blocks_from is not None:
                states = lax.fori_loop(whole_q_blocks_from, nfull,
                                       lambda i, st: q_step(pl.multiple_of(i * tb, tb), tb, st, None), states)
                states = q_step(tail_start, tail, states, None)
            for hd, (dk, dv) in enumerate(states):
                cols = slice(hd * HEAD_PAD, (hd + 1) * HEAD_PAD)
                dk_ref[0, pl.ds(c0, keys), cols] = dk.astype(BF16)
                dv_ref[0, pl.ds(c0, keys), cols] = dv.astype(BF16)

        def whole_block(j, carry):
            kv_rows(pl.multiple_of(j * tb, tb), tb, j + 1)
            return carry

        lax.fori_loop(0, nfull, whole_block, 0)
        kv_rows(tail_start, tail, None)
        if done < lp:
            dk_ref[0, done:lp, :] = jnp.zeros((lp - done, width), BF16)
            dv_ref[0, done:lp, :] = jnp.zeros((lp - done, width), BF16)
        for hd in range(hpg):
            cols = slice(hd * HEAD_PAD, (hd + 1) * HEAD_PAD)
            dq_ref[0, :, cols] = jnp.transpose(dqt_acc[cols, :]).astype(BF16)

    head_spec = pl.BlockSpec((1, lp, width), lambda bi, hi: (bi, 0, hi))
    out = jax.ShapeDtypeStruct((b, lp, hw), BF16)
    return _call_with_exchange(
        body, exchange, name="attn_bwd", grid=(b, heads // hpg),
        in_specs=[head_spec] * 5 + [pl.BlockSpec((1, lp, HEAD_PAD), lambda bi, hi: (bi, 0, 0))],
        out_specs=[head_spec] * 3,
        out_shape=[out, out, out],
        scratch_shapes=[pltpu.VMEM((width, lp), F32), pltpu.VMEM((hpg, 1, lp), F32), pltpu.VMEM((hpg, 1, lp), F32)],
        operands=(q3, k3, v3, o3, do3, lse3))


def _qkv_bwd(dq, dk, dv, z, g_q, g_kv, w_uq_p, w_kv_p, rope_c, rope_sa, rope_sb):
    t, hw = dq.shape
    heads = hw // HEAD_PAD
    tm = _token_tile(t, 512)

    def body(dq_ref, dk_ref, dv_ref, cq_ref, ckv_ref, gq_ref, gkv_ref, wq_ref, wkv_ref, c_ref, sa_ref, sb_ref,
             dqraw_ref, dkvraw_ref, cqn_ref, ckvn_ref, dcq_ref, dckv_ref, dkr_ref, dgq_ref, dgkv_ref):
        @pl.when(pl.program_id(0) == 0)
        def _():
            dgq_ref[...] = jnp.zeros_like(dgq_ref)
            dgkv_ref[...] = jnp.zeros_like(dgkv_ref)

        c, sa, sb = c_ref[...], sa_ref[...], sb_ref[...]
        dkr = jnp.zeros((tm, HEAD_PAD), F32)
        for hd in range(heads):
            cols = slice(hd * HEAD_PAD, (hd + 1) * HEAD_PAD)
            dqraw_ref[:, cols] = _rope_bwd(dq_ref[:, cols].astype(F32) * SM_SCALE, c, sa, sb).astype(BF16)
            dkvraw_ref[:, cols] = dk_ref[:, cols]
            dkr = dkr + dk_ref[:, cols].astype(F32)
        dkvraw_ref[:, hw:] = dv_ref[...]
        lane = lax.broadcasted_iota(jnp.int32, (tm, HEAD_PAD), 1)
        dkr_ref[...] = jnp.where((lane >= QK_NOPE) & (lane < QK_DIM), _rope_bwd(dkr, c, sa, sb), 0.0).astype(BF16)

        gq = gq_ref[...]
        cqn, xh, r = _rms_fwd(cq_ref[...].astype(F32), gq)
        cqn_ref[...] = cqn.astype(BF16)
        dx, dg_rows = _rms_bwd(_dot_nt(dqraw_ref[...], wq_ref[...]), xh, r, gq)
        dcq_ref[...] = dx.astype(BF16)
        dgq_ref[...] += jnp.sum(dg_rows, axis=0, keepdims=True)

        gkv = gkv_ref[...]
        ckvn, xh, r = _rms_fwd(ckv_ref[...].astype(F32), gkv)
        ckvn_ref[...] = ckvn.astype(BF16)
        dx, dg_rows = _rms_bwd(_dot_nt(dkvraw_ref[...], wkv_ref[...]), xh, r, gkv)
        dckv_ref[...] = dx.astype(BF16)
        dgkv_ref[...] += jnp.sum(dg_rows, axis=0, keepdims=True)

    def shape(width, dtype=BF16):
        return jax.ShapeDtypeStruct((t, width), dtype)

    return pl.pallas_call(
        body, name="qkv_bwd", grid=(t // tm,),
        in_specs=[_row_block(tm, hw)] * 3
        + [_row_block(tm, Q_RANK, POOL_WIDTH // Q_RANK), _row_block(tm, KV_RANK, (POOL_WIDTH + Q_RANK) // KV_RANK)]
        + [_vmem_spec()] * 4 + [_row_block(tm, HEAD_PAD)] * 3,
        out_specs=[_row_block(tm, hw), _row_block(tm, 2 * hw), _row_block(tm, Q_RANK), _row_block(tm, KV_RANK),
                   _row_block(tm, Q_RANK), _row_block(tm, KV_RANK), _row_block(tm, HEAD_PAD),
                   pl.BlockSpec((1, Q_RANK), lambda i: (0, 0)), pl.BlockSpec((1, KV_RANK), lambda i: (0, 0))],
        out_shape=[shape(hw), shape(2 * hw), shape(Q_RANK), shape(KV_RANK), shape(Q_RANK), shape(KV_RANK), shape(HEAD_PAD),
                   jax.ShapeDtypeStruct((1, Q_RANK), F32), jax.ShapeDtypeStruct((1, KV_RANK), F32)],
        compiler_params=_params(("arbitrary",)),
    )(dq, dk, dv, z, z, g_q, g_kv, w_uq_p, w_kv_p, rope_c, rope_sa, rope_sb)


def _pool_bwd(z3, da3, pool_w, pool_scale):
    b, lp, _ = z3.shape
    groups = len(POOL_WINDOWS)

    def body(u_ref, da_ref, pw_ref, sc_ref, du_ref, dpw_ref, dsc_ref):
        @pl.when(pl.program_id(0) == 0)
        def _():
            dpw_ref[...] = jnp.zeros_like(dpw_ref)
            dsc_ref[...] = jnp.zeros_like(dsc_ref)

        row = lax.broadcasted_iota(jnp.int32, (lp, POOL_GROUP), 0)
        pos = row.astype(F32)
        for gi, w in enumerate(POOL_WINDOWS):
            cols = slice(gi * POOL_GROUP, (gi + 1) * POOL_GROUP)
            count = jnp.minimum(pos + 1.0, float(w))
            u = u_ref[0, :, cols].astype(F32)
            y = (_window_sum(u, w, row, True) / count - u).astype(BF16)
            yw = _dot(y, pw_ref[gi])
            da = da_ref[0, :, cols].astype(F32)
            dsc_ref[:, cols] += jnp.sum(da * yw, axis=0, keepdims=True)
            dyw = (da * sc_ref[:, cols]).astype(BF16)
            dpw_ref[gi] += _dot_tn(y, dyw)
            dy = _dot_nt(dyw, pw_ref[gi])
            du_ref[0, :, cols] = (_window_sum(dy / count, w, row, False) - dy).astype(BF16)

    return pl.pallas_call(
        body, name="pool_bwd", grid=(b,),
        in_specs=[pl.BlockSpec((1, lp, POOL_WIDTH), lambda i: (i, 0, 0)), pl.BlockSpec((1, lp, POOL_WIDTH), lambda i: (i, 0, 0)),
                  _vmem_spec(), _vmem_spec()],
        out_specs=[pl.BlockSpec((1, lp, POOL_WIDTH), lambda i: (i, 0, 0)),
                   pl.BlockSpec((groups, POOL_GROUP, POOL_GROUP), lambda i: (0, 0, 0)),
                   pl.BlockSpec((1, POOL_WIDTH), lambda i: (0, 0))],
        out_shape=[jax.ShapeDtypeStruct((b, lp, POOL_WIDTH), BF16), jax.ShapeDtypeStruct((groups, POOL_GROUP, POOL_GROUP), F32),
                   jax.ShapeDtypeStruct((1, POOL_WIDTH), F32)],
        compiler_params=_params(("arbitrary",)),
    )(z3, da3, pool_w, pool_scale)


def _in_proj_bwd(h, dh1, du, dcq, dckv, dkr, dga, dgb, g, w_in_pt):
    t, d = h.shape
    nz = w_in_pt.shape[0]
    tm = _token_tile(t, 512)
    widths = (POOL_WIDTH, Q_RANK, KV_RANK, HEAD_PAD, d, d)

    def body(h_ref, dh1_ref, du_ref, dcq_ref, dckv_ref, dkr_ref, dga_ref, dgb_ref, g_ref, w_ref, dh_ref, hn_ref, dz_ref, dg_ref):
        @pl.when(pl.program_id(0) == 0)
        def _():
            dg_ref[...] = jnp.zeros_like(dg_ref)

        gain = g_ref[...]
        hn, xh, r = _rms_fwd(h_ref[...], gain)
        hn_ref[...] = hn.astype(BF16)
        dhn = jnp.zeros((tm, d), F32)
        start = 0
        for piece, width in zip((du_ref, dcq_ref, dckv_ref, dkr_ref, dga_ref, dgb_ref), widths):
            val = piece[...]
            dz_ref[:, start:start + width] = val
            dhn = dhn + _dot(val, w_ref[start:start + width, :])
            start += width
        dx, dg_rows = _rms_bwd(dhn, xh, r, gain)
        dh_ref[...] = dh1_ref[...] + dx
        dg_ref[...] += jnp.sum(dg_rows, axis=0, keepdims=True)

    return pl.pallas_call(
        body, name="in_proj_bwd", grid=(t // tm,),
        in_specs=[_row_block(tm, d), _row_block(tm, d)] + [_row_block(tm, w) for w in widths] + [_vmem_spec(), _vmem_spec()],
        out_specs=[_row_block(tm, d), _row_block(tm, d), _row_block(tm, nz), pl.BlockSpec((1, d), lambda i: (0, 0))],
        out_shape=[jax.ShapeDtypeStruct((t, d), F32), jax.ShapeDtypeStruct((t, d), BF16), jax.ShapeDtypeStruct((t, nz), BF16),
                   jax.ShapeDtypeStruct((1, d), F32)],
        compiler_params=_params(("arbitrary",)),
    )(h, dh1, du, dcq, dckv, dkr, dga, dgb, g, w_in_pt)


def _pad_heads(w, heads, width):
    k = w.shape[0]
    w = w.reshape(k, heads, width)
    return jnp.pad(w, ((0, 0), (0, 0), (0, HEAD_PAD - width))).reshape(k, heads * HEAD_PAD)


def _unpad_heads(w, heads, width):
    k = w.shape[0]
    return w.reshape(k, heads, HEAD_PAD)[:, :, :width].reshape(k, heads * width)


def _early_layouts(w, heads):
    o3, o4 = POOL_WIDTH + Q_RANK + KV_RANK, POOL_WIDTH + Q_RANK + KV_RANK + QK_ROPE
    w_in_t = w["w_in"]
    rope_rows = jnp.pad(w_in_t[o3:o4], ((QK_NOPE, HEAD_PAD - QK_DIM), (0, 0)))
    w_in_pt = jnp.concatenate([w_in_t[:o3], rope_rows, w_in_t[o4:]], axis=0)
    w_uq_p = _pad_heads(w["w_uq"], heads, QK_DIM)
    kv = w["w_ukv"].reshape(KV_RANK, heads, QK_NOPE + V_DIM)
    w_k = jnp.pad(kv[:, :, :QK_NOPE], ((0, 0), (0, 0), (0, HEAD_PAD - QK_NOPE))).reshape(KV_RANK, heads * HEAD_PAD)
    w_v = jnp.pad(kv[:, :, QK_NOPE:], ((0, 0), (0, 0), (0, HEAD_PAD - V_DIM))).reshape(KV_RANK, heads * HEAD_PAD)
    return dict(w_in_pt=w_in_pt, w_uq_p=w_uq_p, w_kv_p=jnp.concatenate([w_k, w_v], axis=1))


def _late_layouts(w, heads):
    d = w["w_pb"].shape[1]
    w_pb_p = jnp.pad(w["w_pb"].reshape(heads, V_DIM, d), ((0, 0), (0, HEAD_PAD - V_DIM), (0, 0))).reshape(heads * HEAD_PAD, d)
    return dict(w_pa=w["w_pa"], w_pb_p=w_pb_p, w_o=w["w_o"], w_gate_t=w["w_gate"], w_up_t=w["w_up"], w_down=w["w_down"])


def _early_grad_layouts(g, heads):
    o3 = POOL_WIDTH + Q_RANK + KV_RANK
    gin = g["w_in_pt"]
    w_in = jnp.concatenate([gin[:o3], gin[o3 + QK_NOPE:o3 + QK_DIM], gin[o3 + HEAD_PAD:]], axis=0)
    hw = heads * HEAD_PAD
    gk = g["w_kv_p"][:, :hw].reshape(KV_RANK, heads, HEAD_PAD)[:, :, :QK_NOPE]
    gv = g["w_kv_p"][:, hw:].reshape(KV_RANK, heads, HEAD_PAD)[:, :, :V_DIM]
    w_ukv = jnp.concatenate([gk, gv], axis=2).reshape(KV_RANK, heads * (QK_NOPE + V_DIM))
    return dict(w_in=w_in, w_uq=_unpad_heads(g["w_uq_p"], heads, QK_DIM), w_ukv=w_ukv)


def _late_grad_layouts(g, heads):
    d = g["w_pb_p"].shape[1]
    w_pb = g["w_pb_p"].reshape(heads, HEAD_PAD, d)[:, :V_DIM].reshape(heads * V_DIM, d)
    return dict(w_pa=g["w_pa"], w_pb=w_pb, w_o=g["w_o"], w_gate=g["w_gate_t"], w_up=g["w_up_t"], w_down=g["w_down"])


def _rope_tables(lp, b):
    inv = 1.0 / (ROPE_THETA ** (jnp.arange(0, QK_ROPE, 2, dtype=F32) / QK_ROPE))
    ang = jnp.arange(lp, dtype=F32)[:, None] * inv[None, :]
    cos, sin = jnp.cos(ang), jnp.sin(ang)
    half = QK_ROPE // 2
    ones = jnp.ones((lp, QK_NOPE), F32)
    zeros_lo = jnp.zeros((lp, QK_NOPE), F32)
    zeros_hi = jnp.zeros((lp, HEAD_PAD - QK_DIM), F32)
    zeros_half = jnp.zeros((lp, half), F32)
    c = jnp.concatenate([ones, cos, cos, zeros_hi], axis=1)
    sa = jnp.concatenate([zeros_lo, zeros_half, sin, zeros_hi], axis=1)
    sb = jnp.concatenate([zeros_lo, -sin, zeros_half, zeros_hi], axis=1)
    return tuple(jnp.tile(tab, (b, 1)) for tab in (c, sa, sb))


def _local_step(x, loss_target, meta_tokens, small, early_first, riding, early_first_codec):
    b, seq, d = x.shape
    depth = 2
    heads = early_first["w_uq"].shape[1] // QK_DIM
    core = lax.axis_index("c")
    chip = 2 * lax.axis_index("x") + lax.axis_index("y")
    real_len = N_META + seq
    lp = _round_up(real_len, SEQ_PAD)
    t = b * lp
    pad = lp - N_META - seq

    meta = jnp.broadcast_to(meta_tokens[None], (b, N_META, d))
    h = jnp.concatenate([meta, x, jnp.zeros((b, pad, d), F32)], axis=1).reshape(t, d)
    target = jnp.pad(loss_target, ((0, 0), (N_META, pad), (0, 0))).reshape(t, d)
    pos = jnp.arange(lp)
    valid = jnp.tile(((pos >= N_META) & (pos < N_META + seq)).astype(F32), b).reshape(t, 1)
    rope_c, rope_sa, rope_sb = _rope_tables(lp, b)

    layers = []
    for li in range(depth):
        lay = dict(pool_w=small["pool_w"][li].astype(BF16), pool_scale=small["pool_scale"][li][None])
        for n in ("norm_mix_g", "q_norm_g", "kv_norm_g", "norm_ffn_g"):
            lay[n] = small[n][li][None]
        layers.append(lay)
    layers[0].update(_early_layouts(early_first, heads))

    saved = []
    for li in range(depth):
        lay = layers[li]
        z = _in_proj_fwd(h, lay["norm_mix_g"], lay["w_in_pt"])
        a = _pool_fwd(z.reshape(b, lp, -1), lay["pool_w"], lay["pool_scale"]).reshape(t, POOL_WIDTH)
        q, k, v = _qkv_fwd(z, lay["q_norm_g"], lay["kv_norm_g"], lay["w_uq_p"], lay["w_kv_p"], rope_c, rope_sa, rope_sb)
        hw = q.shape[1]
        halves = riding[li]["packed"].reshape(2, -1, PACK_COLS)
        o3, lse, others = _attn_fwd(q.reshape(b, lp, hw), k.reshape(b, lp, hw), v.reshape(b, lp, hw), real_len,
                                    _gather_exchange(halves))
        arrived = riding[li]["unpack"](
            lax.dynamic_update_index_in_dim(others, halves, chip, 0).reshape(N_CHIPS, -1, PACK_COLS))
        lay.update(_late_layouts({n: arrived[n, li] for n in LATE_WEIGHTS}, heads))
        if li + 1 < depth:
            layers[li + 1].update(_early_layouts({n: arrived[n, li + 1] for n in EARLY_WEIGHTS}, heads))
        o = o3.reshape(t, hw)
        h1, pa, pb = _merge_fwd(h, z, a, o, lay["w_pa"], lay["w_pb_p"], lay["w_o"])
        h2, fa, fb = _ffn_fwd(h1, lay["norm_ffn_g"], lay["w_gate_t"], lay["w_up_t"], lay["w_down"])
        saved.append(dict(h=h, z=z, a=a, q=q, k=k, v=v, o=o, lse=lse, pa=pa, pb=pb, h1=h1, fa=fa, fb=fb))
        h = h2

    dh, loss, d_final = _loss_head(h, small["final_norm_g"][None], target, valid)

    g_small = {n: [] for n in SMALL_WEIGHTS if n != "final_norm_g"}
    early_grads, parts = {}, [None] * depth
    for li in reversed(range(depth)):
        lay, sv = layers[li], saved[li]
        hw = sv["q"].shape[1]
        dh1, hn_f, act, dfa, dfb, dg_ffn = _ffn_bwd(sv["h1"], dh, sv["fa"], sv["fb"], lay["norm_ffn_g"],
                                                     lay["w_gate_t"], lay["w_up_t"], lay["w_down"])
        gl = dict(w_gate_t=_weight_grad(dfa, hn_f, "grad_w_gate"), w_up_t=_weight_grad(dfb, hn_f, "grad_w_up"),
                  w_down=_weight_grad(act, dh, "grad_w_down"))
        merged, dpa, dpb, dga, dgb, da, do = _merge_bwd(dh1, sv["z"], sv["pa"], sv["pb"], lay["w_o"], lay["w_pa"], lay["w_pb_p"])
        gl["w_o"] = _weight_grad(merged, dh1, "grad_w_o")
        gl["w_pa"] = _weight_grad(sv["a"], dpa, "grad_w_pa")
        gl["w_pb_p"] = _weight_grad(sv["o"], dpb, "grad_w_pb")
        to_send = {(n, li): g for n, g in _late_grad_layouts(gl, heads).items()}
        if li + 1 < depth:
            to_send.update({(n, li + 1): g for n, g in early_grads[li + 1].items()})
        sending = riding[li]["pack_grads"](to_send)
        shape3 = (b, lp, hw)
        dq3, dk3, dv3, from_others = _attn_bwd(
            sv["q"].reshape(shape3), sv["k"].reshape(shape3), sv["v"].reshape(shape3), sv["o"].reshape(shape3),
            do.reshape(shape3), sv["lse"], real_len, _scatter_exchange(sending))
        own = lax.dynamic_index_in_dim(lax.dynamic_index_in_dim(sending, chip, 0, keepdims=False), core, 0, keepdims=False)
        parts[li] = lax.dynamic_update_index_in_dim(from_others, own, 2 * chip + core, 0)
        dqraw, dkvraw, cqn, ckvn, dcq, dckv, dkr, dg_q, dg_kv = _qkv_bwd(
            dq3.reshape(t, hw), dk3.reshape(t, hw), dv3.reshape(t, hw), sv["z"], lay["q_norm_g"], lay["kv_norm_g"],
            lay["w_uq_p"], lay["w_kv_p"], rope_c, rope_sa, rope_sb)
        gl["w_uq_p"] = _weight_grad(cqn, dqraw, "grad_w_uq")
        gl["w_kv_p"] = _weight_grad(ckvn, dkvraw, "grad_w_ukv")
        du3, dpool_w, dpool_scale = _pool_bwd(sv["z"].reshape(b, lp, -1), da.reshape(b, lp, POOL_WIDTH),
                                              lay["pool_w"], lay["pool_scale"])
        dh, hn_m, dz, dg_mix = _in_proj_bwd(sv["h"], dh1, du3.reshape(t, POOL_WIDTH), dcq, dckv, dkr, dga, dgb,
                                            lay["norm_mix_g"], lay["w_in_pt"])
        gl["w_in_pt"] = _weight_grad(dz, hn_m, "grad_w_in")
        early_grads[li] = _early_grad_layouts(gl, heads)
        for n, val in (("norm_mix_g", dg_mix[0]), ("pool_w", dpool_w), ("pool_scale", dpool_scale[0]), ("q_norm_g", dg_q[0]),
                       ("kv_norm_g", dg_kv[0]), ("norm_ffn_g", dg_ffn[0])):
            g_small[n].insert(0, val)

    dh3 = dh.reshape(b, lp, d)
    grad_x = dh3[:, N_META:N_META + seq]
    d_meta_rows = dh3[:, :N_META]
    g_small = {n: jnp.stack(v) for n, v in g_small.items()}
    g_small["final_norm_g"] = d_final[0]
    early_first_partial = early_first_codec["pack_grads"]({(n, 0): g for n, g in early_grads[0].items()})
    return loss, grad_x, d_meta_rows, g_small, early_first_partial, parts


def _mesh_place():
    x, y, c = lax.axis_index("x"), lax.axis_index("y"), lax.axis_index("c")
    others = [(1 - x, y), (x, 1 - y), (1 - x, 1 - y)]
    return x, y, c, 2 * x + y, others


def _remote(src, dst, send_sems, recv_sems, k, device):
    return pltpu.make_async_remote_copy(src_ref=src, dst_ref=dst, send_sem=send_sems.at[k], recv_sem=recv_sems.at[k],
                                        device_id=device, device_id_type=MESH)


def _gather_exchange(packed):
    def over_links(p_ref, g_ref, send_sems, recv_sems, r, slot):
        _, _, c, _, others = _mesh_place()
        ox, oy = others[r]
        return _remote(p_ref.at[c], g_ref.at[slot, c], send_sems, recv_sems, r, (ox, oy, c))

    def to_sibling(g_ref, send_sems, recv_sems, r, half):
        x, y, c, _, others = _mesh_place()
        ox, oy = others[r]
        block = g_ref.at[2 * ox + oy, c if half is None else half]
        return _remote(block, block, send_sems, recv_sems, 3 + r, (x, y, 1 - c))

    def start(p_ref, g_ref, send_sems, recv_sems):
        chip = _mesh_place()[3]
        for r in range(3):
            over_links(p_ref, g_ref, send_sems, recv_sems, r, chip).start()

    def pass_on(p_ref, g_ref, send_sems, recv_sems):
        others = _mesh_place()[4]
        for r, (ox, oy) in enumerate(others):
            over_links(p_ref, g_ref, send_sems, recv_sems, r, 2 * ox + oy).wait_recv()
            to_sibling(g_ref, send_sems, recv_sems, r, None).start()

    def finish(p_ref, g_ref, send_sems, recv_sems):
        _, _, c, chip, _ = _mesh_place()
        for r in range(3):
            to_sibling(g_ref, send_sems, recv_sems, r, 1 - c).wait_recv()
        for r in range(3):
            over_links(p_ref, g_ref, send_sems, recv_sems, r, chip).wait_send()
            to_sibling(g_ref, send_sems, recv_sems, r, None).wait_send()

    return dict(name="gather", operand=packed, copies=6, start=start, pass_on=pass_on, finish=finish,
                out_shape=jax.ShapeDtypeStruct((N_CHIPS,) + packed.shape, packed.dtype))


def _scatter_exchange(parts):
    flips = [(dx, dy, dc) for dx in (0, 1) for dy in (0, 1) for dc in (0, 1)][1:]

    def copy(p_ref, got_ref, send_sems, recv_sems, k, arriving):
        x, y, c, _, _ = _mesh_place()
        dx, dy, dc = flips[k]
        tx, ty, tc = (1 - x if dx else x), (1 - y if dy else y), (1 - c if dc else c)
        slot = 4 * tx + 2 * ty + tc if arriving else 4 * x + 2 * y + c
        return _remote(p_ref.at[2 * tx + ty, tc], got_ref.at[slot], send_sems, recv_sems, k, (tx, ty, tc))

    def start(p_ref, got_ref, send_sems, recv_sems):
        for k in range(len(flips)):
            copy(p_ref, got_ref, send_sems, recv_sems, k, False).start()

    def finish(p_ref, got_ref, send_sems, recv_sems):
        for k in range(len(flips)):
            copy(p_ref, got_ref, send_sems, recv_sems, k, True).wait_recv()
        for k in range(len(flips)):
            copy(p_ref, got_ref, send_sems, recv_sems, k, False).wait_send()

    return dict(name="scatter", operand=parts, copies=len(flips), start=start, finish=finish,
                out_shape=jax.ShapeDtypeStruct((2 * N_CHIPS,) + parts.shape[2:], parts.dtype))


def _all_gather_shards(packed, meta_shard):
    _, rh, cols = packed.shape

    def body(p_ref, m_ref, g_ref, gm_ref, send_sems, recv_sems):
        x, y, c, chip, others = _mesh_place()
        sibling = (x, y, 1 - c)
        sends = []
        for r, (ox, oy) in enumerate(others):
            sends.append(_remote(p_ref.at[c], g_ref.at[chip, c], send_sems, recv_sems, r, (ox, oy, c)))
            sends.append(_remote(m_ref, gm_ref.at[chip], send_sems, recv_sems, 6 + r, (ox, oy, c)))
        for cp in sends:
            cp.start()
        for r, (ox, oy) in enumerate(others):
            src_chip = 2 * ox + oy
            _remote(p_ref.at[c], g_ref.at[src_chip, c], send_sems, recv_sems, r, (ox, oy, c)).wait_recv()
            passed = _remote(g_ref.at[src_chip, c], g_ref.at[src_chip, c], send_sems, recv_sems, 3 + r, sibling)
            passed.start()
            sends.append(passed)
        for r, (ox, oy) in enumerate(others):
            src_chip = 2 * ox + oy
            _remote(p_ref.at[c], g_ref.at[src_chip, 1 - c], send_sems, recv_sems, 3 + r, sibling).wait_recv()
            _remote(m_ref, gm_ref.at[src_chip], send_sems, recv_sems, 6 + r, (ox, oy, c)).wait_recv()
        for cp in sends:
            cp.wait_send()

    gathered, meta_all = pl.pallas_call(
        body, name="all_gather_shards",
        in_specs=[_any_spec(), _any_spec()], out_specs=[_any_spec(), _any_spec()],
        out_shape=[jax.ShapeDtypeStruct((N_CHIPS, 2, rh, cols), packed.dtype),
                   jax.ShapeDtypeStruct((N_CHIPS,) + meta_shard.shape, meta_shard.dtype)],
        scratch_shapes=[pltpu.SemaphoreType.DMA((9,)), pltpu.SemaphoreType.DMA((9,))],
    )(packed, meta_shard)
    chip = 2 * lax.axis_index("x") + lax.axis_index("y")
    return (lax.dynamic_update_index_in_dim(gathered, packed, chip, 0),
            lax.dynamic_update_index_in_dim(meta_all, meta_shard, chip, 0))


def _pair_exchange(give):
    def body(give_ref, got_ref, send_sems, recv_sems):
        x, y, c, _, _ = _mesh_place()
        cp = _remote(give_ref, got_ref, send_sems, recv_sems, 0, (x, y, 1 - c))
        cp.start()
        cp.wait()

    return pl.pallas_call(
        body, name="pair_exchange", in_specs=[_any_spec()], out_specs=_any_spec(),
        out_shape=jax.ShapeDtypeStruct(give.shape, give.dtype),
        scratch_shapes=[pltpu.SemaphoreType.DMA((1,)), pltpu.SemaphoreType.DMA((1,))],
    )(give)


def _chip_exchange(parts):
    def body(p_ref, got_ref, send_sems, recv_sems):
        _, _, c, chip, others = _mesh_place()
        sends = [_remote(p_ref.at[2 * ox + oy], got_ref.at[chip], send_sems, recv_sems, r, (ox, oy, c))
                 for r, (ox, oy) in enumerate(others)]
        for cp in sends:
            cp.start()
        for r, (ox, oy) in enumerate(others):
            _remote(p_ref.at[chip], got_ref.at[2 * ox + oy], send_sems, recv_sems, r, (ox, oy, c)).wait_recv()
        for cp in sends:
            cp.wait_send()

    got = pl.pallas_call(
        body, name="chip_exchange", in_specs=[_any_spec()], out_specs=_any_spec(),
        out_shape=jax.ShapeDtypeStruct(parts.shape, parts.dtype),
        scratch_shapes=[pltpu.SemaphoreType.DMA((3,)), pltpu.SemaphoreType.DMA((3,))],
    )(parts)
    chip = 2 * lax.axis_index("x") + lax.axis_index("y")
    own = lax.dynamic_index_in_dim(parts, chip, 0, keepdims=False)
    return lax.dynamic_update_index_in_dim(got, own, chip, 0)


def _pair_gather(half):
    def body(h_ref, out_ref, send_sems, recv_sems):
        x, y, c, _, _ = _mesh_place()
        cp = _remote(h_ref, out_ref.at[c], send_sems, recv_sems, 0, (x, y, 1 - c))
        cp.start()
        _remote(h_ref, out_ref.at[1 - c], send_sems, recv_sems, 0, (x, y, 1 - c)).wait_recv()
        cp.wait_send()

    both = pl.pallas_call(
        body, name="pair_gather", in_specs=[_any_spec()], out_specs=_any_spec(),
        out_shape=jax.ShapeDtypeStruct((2,) + half.shape, half.dtype),
        scratch_shapes=[pltpu.SemaphoreType.DMA((1,)), pltpu.SemaphoreType.DMA((1,))],
    )(half)
    return lax.dynamic_update_index_in_dim(both, half, lax.axis_index("c"), 0)


def _row_tile(rows, limit=PACK_TILE):
    if rows <= limit:
        return rows
    for tr in range(limit, 7, -8):
        if rows % tr == 0:
            return tr
    return rows


def _pair_add(keep, got):
    n, rh, cols = keep.shape
    tr = _row_tile(rh)

    def body(k_ref, g_ref, o_ref):
        o_ref[...] = (k_ref[...].astype(F32) + g_ref[...].astype(F32)).astype(BF16)

    spec = pl.BlockSpec((1, tr, cols), lambda j, i: (j, i, 0))
    return pl.pallas_call(
        body, name="pair_add", grid=(n, rh // tr), in_specs=[spec, spec], out_specs=spec,
        out_shape=jax.ShapeDtypeStruct(keep.shape, BF16),
        compiler_params=_params(("parallel", "parallel")),
    )(keep, got)


def _chip_sum(parts):
    n, rh, cols = parts.shape
    tr = _row_tile(rh)

    def body(p_ref, o_ref):
        total = p_ref[0].astype(F32)
        for k in range(1, n):
            total = total + p_ref[k].astype(F32)
        o_ref[...] = total

    return pl.pallas_call(
        body, name="chip_sum", grid=(rh // tr,),
        in_specs=[pl.BlockSpec((n, tr, cols), lambda i: (0, i, 0))], out_specs=pl.BlockSpec((tr, cols), lambda i: (i, 0)),
        out_shape=jax.ShapeDtypeStruct((rh, cols), F32),
        compiler_params=_params(("parallel",)),
    )(parts)


def _reduce_scatter(grads, c):
    keep = lax.dynamic_index_in_dim(grads, c, axis=1, keepdims=False)
    give = lax.dynamic_index_in_dim(grads, 1 - c, axis=1, keepdims=False)
    chip_partial = _pair_add(keep, _pair_exchange(give))
    return _pair_gather(_chip_sum(_chip_exchange(chip_partial)))


def _all_reduce_small(meta_rows, small):
    b, rm, cols = meta_rows.shape
    rows = rm + small.shape[0]

    def body(meta_ref, small_ref, out_ref, mine, pair_buf, chip_buf, send_sems, recv_sems):
        x, y, c, chip, others = _mesh_place()
        acc = meta_ref[0]
        for i in range(1, b):
            acc = acc + meta_ref[i]
        mine[0:rm, :] = acc
        mine[rm:rows, :] = small_ref[...]
        pair = _remote(mine, pair_buf, send_sems, recv_sems, 0, (x, y, 1 - c))
        pair.start()
        pair.wait()
        chip_buf[chip] = mine[...] + pair_buf[...]
        sends = [_remote(chip_buf.at[chip], chip_buf.at[chip], send_sems, recv_sems, 1 + r, (ox, oy, c))
                 for r, (ox, oy) in enumerate(others)]
        for cp in sends:
            cp.start()
        for r, (ox, oy) in enumerate(others):
            _remote(chip_buf.at[chip], chip_buf.at[2 * ox + oy], send_sems, recv_sems, 1 + r, (ox, oy, c)).wait_recv()
        for cp in sends:
            cp.wait_send()
        out_ref[...] = ((chip_buf[0] + chip_buf[1]) + chip_buf[2]) + chip_buf[3]

    return pl.pallas_call(
        body, name="all_reduce_small",
        in_specs=[_vmem_spec(), _vmem_spec()], out_specs=_vmem_spec(),
        out_shape=jax.ShapeDtypeStruct((rows, cols), F32),
        scratch_shapes=[pltpu.VMEM((rows, cols), F32), pltpu.VMEM((rows, cols), F32), pltpu.VMEM((N_CHIPS, rows, cols), F32),
                        pltpu.SemaphoreType.DMA((4,)), pltpu.SemaphoreType.DMA((4,))],
        compiler_params=pltpu.CompilerParams(vmem_limit_bytes=VMEM_LIMIT),
    )(meta_rows, small)


def _adamw(w, g, m, v):
    shape = w.shape
    cols = shape[-1]
    rows = w.size // cols
    tr = _row_tile(rows)

    def body(w_ref, g_ref, m_ref, v_ref, d_ref, m2_ref, v2_ref):
        grad = g_ref[...]
        m2 = ADAM_B1 * m_ref[...] + (1.0 - ADAM_B1) * grad
        v2 = ADAM_B2 * v_ref[...] + (1.0 - ADAM_B2) * jnp.square(grad)
        m_hat = m2 / (1.0 - ADAM_B1 ** ADAM_STEP)
        v_hat = v2 / (1.0 - ADAM_B2 ** ADAM_STEP)
        d_ref[...] = -ADAM_LR * (m_hat / (jnp.sqrt(v_hat) + ADAM_EPS) + ADAM_WD * w_ref[...])
        m2_ref[...] = m2
        v2_ref[...] = v2

    spec = pl.BlockSpec((tr, cols), lambda i: (i, 0))
    out = jax.ShapeDtypeStruct((rows, cols), F32)
    res = pl.pallas_call(
        body, name="adamw", grid=(rows // tr,), in_specs=[spec] * 4, out_specs=[spec] * 3, out_shape=[out] * 3,
        compiler_params=_params(("parallel",)),
    )(*(a.reshape(rows, cols) for a in (w, g, m, v)))
    return tuple(r.reshape(shape) for r in res)


def _adamw_turned(w, g_t, m, v):
    layers, rows, cols = w.shape
    tr = _row_tile(rows)

    def body(w_ref, gt_ref, m_ref, v_ref, g_ref, d_ref, m2_ref, v2_ref):
        grad = jnp.transpose(gt_ref[0])
        m2 = ADAM_B1 * m_ref[0] + (1.0 - ADAM_B1) * grad
        v2 = ADAM_B2 * v_ref[0] + (1.0 - ADAM_B2) * jnp.square(grad)
        m_hat = m2 / (1.0 - ADAM_B1 ** ADAM_STEP)
        v_hat = v2 / (1.0 - ADAM_B2 ** ADAM_STEP)
        g_ref[0] = grad
        d_ref[0] = -ADAM_LR * (m_hat / (jnp.sqrt(v_hat) + ADAM_EPS) + ADAM_WD * w_ref[0])
        m2_ref[0] = m2
        v2_ref[0] = v2

    spec = pl.BlockSpec((1, tr, cols), lambda li, i: (li, i, 0))
    out = jax.ShapeDtypeStruct(w.shape, F32)
    return pl.pallas_call(
        body, name="adamw_turned", grid=(layers, rows // tr),
        in_specs=[spec, pl.BlockSpec((1, cols, tr), lambda li, i: (li, 0, i)), spec, spec],
        out_specs=[spec] * 4, out_shape=[out] * 4,
        compiler_params=_params(("parallel", "parallel")),
    )(w, g_t, m, v)


def _pack_rows(arrays):
    flat = [a.reshape(-1, PACK_COLS) for a in arrays]
    counts = [f.shape[0] for f in flat]
    total = sum(counts)
    half = -(-total // 2)
    tiles = -(-half // PACK_TILE)
    padded = 2 * tiles * _round_up(-(-half // tiles), 16)
    if padded > total:
        flat.append(jnp.zeros((padded - total, PACK_COLS), flat[0].dtype))
    return jnp.concatenate(flat, axis=0), counts


def _unpack_rows(buffer, counts, shapes):
    out, start = [], 0
    for n, shape in zip(counts, shapes):
        out.append(buffer[..., start:start + n, :].reshape(buffer.shape[:-2] + tuple(shape)))
        start += n
    return out


def _group_codec(entries, weights):
    turned = [n in TRANSPOSED_WEIGHTS for n, _ in entries]
    shapes = [weights[n].shape[1:][::-1] if t else weights[n].shape[1:] for (n, _), t in zip(entries, turned)]
    by_rows = [t or SHARD_AXIS[n] == 1 for (n, _), t in zip(entries, turned)]
    packed, counts = _pack_rows([(weights[n][li].T if t else weights[n][li]).astype(BF16) for (n, li), t in zip(entries, turned)])
    rows = packed.shape[0]
    pad_rows = rows - sum(counts)

    def unpack(per_chip_packed):
        out = {}
        for entry, (s0, s1), rowwise, blk in zip(entries, shapes, by_rows, _unpack_rows(per_chip_packed, counts, shapes)):
            out[entry] = blk.reshape(N_CHIPS * s0, s1) if rowwise else jnp.transpose(blk, (1, 0, 2)).reshape(s0, N_CHIPS * s1)
        return out

    def pack_grads(whole):
        pieces = []
        for entry, (s0, s1), rowwise in zip(entries, shapes, by_rows):
            g = whole[entry]
            by_chip = g.reshape(N_CHIPS, s0, s1) if rowwise else jnp.transpose(g.reshape(s0, N_CHIPS, s1), (1, 0, 2))
            pieces.append(by_chip.reshape(N_CHIPS, -1, PACK_COLS))
        if pad_rows:
            pieces.append(jnp.zeros((N_CHIPS, pad_rows, PACK_COLS), BF16))
        return jnp.concatenate(pieces, axis=1).reshape(N_CHIPS, 2, rows // 2, PACK_COLS)

    def unpack_reduced(reduced):
        return dict(zip(entries, _unpack_rows(reduced.reshape(rows, PACK_COLS), counts, shapes)))

    return dict(packed=packed, unpack=unpack, pack_grads=pack_grads, unpack_reduced=unpack_reduced)


def kernel(x, meta_tokens, norm_mix_g, w_in, pool_w, pool_scale, q_norm_g, kv_norm_g, w_uq, w_ukv, w_pa, w_pb, w_o, norm_ffn_g, w_gate, w_up, w_down, final_norm_g, loss_target, m_meta_tokens, m_norm_mix_g, m_w_in, m_pool_w, m_pool_scale, m_q_norm_g, m_kv_norm_g, m_w_uq, m_w_ukv, m_w_pa, m_w_pb, m_w_o, m_norm_ffn_g, m_w_gate, m_w_up, m_w_down, m_final_norm_g, v_meta_tokens, v_norm_mix_g, v_w_in, v_pool_w, v_pool_scale, v_q_norm_g, v_kv_norm_g, v_w_uq, v_w_ukv, v_w_pa, v_w_pb, v_w_o, v_norm_ffn_g, v_w_gate, v_w_up, v_w_down, v_final_norm_g):
    weights = dict(meta_tokens=meta_tokens, norm_mix_g=norm_mix_g, w_in=w_in, pool_w=pool_w, pool_scale=pool_scale,
                   q_norm_g=q_norm_g, kv_norm_g=kv_norm_g, w_uq=w_uq, w_ukv=w_ukv, w_pa=w_pa, w_pb=w_pb, w_o=w_o,
                   norm_ffn_g=norm_ffn_g, w_gate=w_gate, w_up=w_up, w_down=w_down, final_norm_g=final_norm_g)
    first = dict(meta_tokens=m_meta_tokens, norm_mix_g=m_norm_mix_g, w_in=m_w_in, pool_w=m_pool_w, pool_scale=m_pool_scale,
                 q_norm_g=m_q_norm_g, kv_norm_g=m_kv_norm_g, w_uq=m_w_uq, w_ukv=m_w_ukv, w_pa=m_w_pa, w_pb=m_w_pb, w_o=m_w_o,
                 norm_ffn_g=m_norm_ffn_g, w_gate=m_w_gate, w_up=m_w_up, w_down=m_w_down, final_norm_g=m_final_norm_g)
    second = dict(meta_tokens=v_meta_tokens, norm_mix_g=v_norm_mix_g, w_in=v_w_in, pool_w=v_pool_w, pool_scale=v_pool_scale,
                  q_norm_g=v_q_norm_g, kv_norm_g=v_kv_norm_g, w_uq=v_w_uq, w_ukv=v_w_ukv, w_pa=v_w_pa, w_pb=v_w_pb, w_o=v_w_o,
                  norm_ffn_g=v_norm_ffn_g, w_gate=v_w_gate, w_up=v_w_up, w_down=v_w_down, final_norm_g=v_final_norm_g)
    core = lax.axis_index("c")
    chip = 2 * lax.axis_index("x") + lax.axis_index("y")
    d = x.shape[-1]
    meta_cols = meta_tokens.shape[1]

    early_first = _group_codec([(n, 0) for n in EARLY_WEIGHTS], weights)
    riding = [_group_codec([(n, 0) for n in LATE_WEIGHTS] + [(n, 1) for n in EARLY_WEIGHTS], weights),
              _group_codec([(n, 1) for n in LATE_WEIGHTS], weights)]
    gathered, meta_all = _all_gather_shards(early_first["packed"].reshape(2, -1, PACK_COLS), meta_tokens)
    early_weights = early_first["unpack"](gathered.reshape(N_CHIPS, -1, PACK_COLS))
    meta_full = jnp.concatenate([meta_all[j] for j in range(N_CHIPS)], axis=1)
    small = {n: weights[n] for n in SMALL_WEIGHTS}

    loss, grad_x, d_meta_rows, g_small, early_partial, parts = _local_step(
        x, loss_target, meta_full, small, {n: early_weights[n, 0] for n in EARLY_WEIGHTS}, riding, early_first)

    shards = early_first["unpack_reduced"](_reduce_scatter(early_partial, core))
    for codec, from_all in zip(riding, parts):
        shards.update(codec["unpack_reduced"](_pair_gather(_chip_sum(from_all))))
    grads = {n: jnp.stack([shards[n, 0], shards[n, 1]]) for n in BIG_WEIGHTS}

    small_shapes = [weights[n].shape for n in SMALL_WEIGHTS]
    small_flat = jnp.concatenate([g_small[n].reshape(-1) for n in SMALL_WEIGHTS])
    small_len = small_flat.shape[0]
    small_rows = _round_up(-(-small_len // PACK_COLS), 8)
    small_pack = jnp.pad(small_flat, (0, small_rows * PACK_COLS - small_len)).reshape(small_rows, PACK_COLS)
    meta_rows = N_META * d // PACK_COLS
    summed = _all_reduce_small(d_meta_rows.reshape(-1, meta_rows, PACK_COLS), small_pack)
    grad_meta_full = summed[:meta_rows].reshape(N_META, d)
    grads["meta_tokens"] = lax.dynamic_slice_in_dim(grad_meta_full, chip * meta_cols, meta_cols, axis=1)
    small_sum = summed[meta_rows:].reshape(-1)
    start = 0
    for n, shape in zip(SMALL_WEIGHTS, small_shapes):
        size = 1
        for s in shape:
            size *= s
        grads[n] = small_sum[start:start + size].reshape(shape)
        start += size

    deltas, new_m, new_v = {}, {}, {}
    for n in WEIGHT_ORDER:
        if n in TRANSPOSED_WEIGHTS:
            grads[n], deltas[n], new_m[n], new_v[n] = _adamw_turned(weights[n], grads[n], first[n], second[n])
        else:
            deltas[n], new_m[n], new_v[n] = _adamw(weights[n], grads[n], first[n], second[n])

    total_loss = lax.psum(loss[0, 0], ("x", "y", "c"))
    return (total_loss, grad_x, *[grads[n] for n in WEIGHT_ORDER], *[deltas[n] for n in WEIGHT_ORDER],
            *[new_m[n] for n in WEIGHT_ORDER], *[new_v[n] for n in WEIGHT_ORDER])
```

```python
import functools

import jax
import jax.numpy as jnp
from jax import lax
from jax.experimental import pallas as pl
from jax.experimental.pallas import tpu as pltpu

F32 = jnp.float32
BF16 = jnp.bfloat16

N_META = 16
POOL_WINDOWS = (2, 4, 8, 16)
POOL_GROUP = 128
POOL_WIDTH = POOL_GROUP * len(POOL_WINDOWS)
QK_NOPE = 64
QK_ROPE = 32
V_DIM = 64
QK_DIM = QK_NOPE + QK_ROPE
Q_RANK = 256
KV_RANK = 128
HEAD_PAD = 128
SM_SCALE = QK_DIM ** -0.5
ROPE_THETA = 10000.0
NORM_EPS = 1e-6
MASK_VALUE = -1e30
Z_FIXED = POOL_WIDTH + Q_RANK + KV_RANK + HEAD_PAD

ADAM_LR = 0.001
ADAM_B1 = 0.9
ADAM_B2 = 0.999
ADAM_EPS = 1e-08
ADAM_WD = 0.01
ADAM_STEP = 10

N_CHIPS = 4
ATT_BLOCK = 256
SEQ_PAD = 128
ATT_Q_ROWS = 256
ATT_FWD_HEADS = 8
ATT_BWD_HEADS = 4
PACK_COLS = 1024
PACK_TILE = 512
VMEM_LIMIT = 60 * 1024 * 1024
MXU_DEPTH = 256
ACC_BYTES = 8 * 1024 * 1024

MESH = pl.DeviceIdType.MESH

BIG_WEIGHTS = ("w_in", "w_uq", "w_ukv", "w_pa", "w_pb", "w_o", "w_gate", "w_up", "w_down")
EARLY_WEIGHTS = ("w_in", "w_uq", "w_ukv")
LATE_WEIGHTS = ("w_pa", "w_pb", "w_o", "w_gate", "w_up", "w_down")
TRANSPOSED_WEIGHTS = ("w_in", "w_gate", "w_up")
SHARD_AXIS = {"w_in": 2, "w_uq": 2, "w_ukv": 2, "w_pa": 2, "w_pb": 1, "w_o": 1, "w_gate": 2, "w_up": 2, "w_down": 1}
SMALL_WEIGHTS = ("norm_mix_g", "pool_w", "pool_scale", "q_norm_g", "kv_norm_g", "norm_ffn_g", "final_norm_g")
WEIGHT_ORDER = ("meta_tokens", "norm_mix_g", "w_in", "pool_w", "pool_scale", "q_norm_g", "kv_norm_g", "w_uq", "w_ukv",
                "w_pa", "w_pb", "w_o", "norm_ffn_g", "w_gate", "w_up", "w_down", "final_norm_g")


def _round_up(n, m):
    return -(-n // m) * m


def _vmem_spec():
    return pl.BlockSpec(memory_space=pltpu.VMEM)


def _any_spec():
    return pl.BlockSpec(memory_space=pl.ANY)


def _row_block(tm, width, col_block=0):
    return pl.BlockSpec((tm, width), lambda i, cb=col_block: (i, cb))


def _params(sem, vmem=VMEM_LIMIT):
    return pltpu.CompilerParams(dimension_semantics=sem, vmem_limit_bytes=vmem)


def _token_tile(t, want):
    best = SEQ_PAD
    for tm in range(32, min(t, 2 * want) + 1, 32):
        if t % tm == 0 and abs(tm - want) < abs(best - want):
            best = tm
    return best


def _dot(a, b):
    return jnp.dot(a, b, preferred_element_type=F32)


def _dot_nt(a, b):
    return lax.dot_general(a, b, (((1,), (1,)), ((), ())), preferred_element_type=F32)


def _dot_tn(a, b):
    return lax.dot_general(a, b, (((0,), (0,)), ((), ())), preferred_element_type=F32)


def _rms_fwd(x, g):
    r = lax.rsqrt(jnp.mean(x * x, axis=-1, keepdims=True) + NORM_EPS)
    xh = x * r
    return xh * g, xh, r


def _rms_bwd(dy, xh, r, g):
    gdy = dy * g
    dx = r * (gdy - xh * jnp.mean(xh * gdy, axis=-1, keepdims=True))
    return dx, dy * xh


def _rope_fwd(x, c, sa, sb):
    return x * c + pltpu.roll(x, 16, 1) * sa + pltpu.roll(x, HEAD_PAD - 16, 1) * sb


def _rope_bwd(d, c, sa, sb):
    return d * c + pltpu.roll(d * sa, HEAD_PAD - 16, 1) + pltpu.roll(d * sb, 16, 1)


def _in_proj_fwd(h, g, w_in_pt):
    t, d = h.shape
    nz = w_in_pt.shape[0]
    tm = _token_tile(t, 512)

    def body(h_ref, g_ref, w_ref, z_ref):
        hn, _, _ = _rms_fwd(h_ref[...], g_ref[...])
        z_ref[...] = _dot_nt(hn.astype(BF16), w_ref[...]).astype(BF16)

    return pl.pallas_call(
        body, name="in_proj_fwd", grid=(t // tm,),
        in_specs=[_row_block(tm, d), _vmem_spec(), _vmem_spec()],
        out_specs=_row_block(tm, nz),
        out_shape=jax.ShapeDtypeStruct((t, nz), BF16),
        compiler_params=_params(("parallel",)),
    )(h, g, w_in_pt)


def _window_sum(x, w, row, forward):
    n = x.shape[0]
    s = x
    k = 1
    while k < w:
        if forward:
            s = s + jnp.where(row >= k, pltpu.roll(s, k, 0), 0.0)
        else:
            s = s + jnp.where(row < n - k, pltpu.roll(s, n - k, 0), 0.0)
        k *= 2
    return s


def _pool_fwd(z3, pool_w, pool_scale):
    b, lp, _ = z3.shape

    def body(u_ref, pw_ref, sc_ref, a_ref):
        row = lax.broadcasted_iota(jnp.int32, (lp, POOL_GROUP), 0)
        pos = row.astype(F32)
        for gi, w in enumerate(POOL_WINDOWS):
            cols = slice(gi * POOL_GROUP, (gi + 1) * POOL_GROUP)
            u = u_ref[0, :, cols].astype(F32)
            y = _window_sum(u, w, row, True) / jnp.minimum(pos + 1.0, float(w)) - u
            yw = _dot(y.astype(BF16), pw_ref[gi])
            a_ref[0, :, cols] = (yw * sc_ref[:, cols]).astype(BF16)

    return pl.pallas_call(
        body, name="pool_fwd", grid=(b,),
        in_specs=[pl.BlockSpec((1, lp, POOL_WIDTH), lambda i: (i, 0, 0)), _vmem_spec(), _vmem_spec()],
        out_specs=pl.BlockSpec((1, lp, POOL_WIDTH), lambda i: (i, 0, 0)),
        out_shape=jax.ShapeDtypeStruct((b, lp, POOL_WIDTH), BF16),
        compiler_params=_params(("parallel",)),
    )(z3, pool_w, pool_scale)


def _qkv_fwd(z, g_q, g_kv, w_uq_p, w_kv_p, rope_c, rope_sa, rope_sb):
    t = z.shape[0]
    hw = w_uq_p.shape[1]
    heads = hw // HEAD_PAD
    tm = _token_tile(t, 512)

    def body(cq_ref, ckv_ref, kr_ref, gq_ref, gkv_ref, wq_ref, wkv_ref, c_ref, sa_ref, sb_ref, q_ref, k_ref, v_ref):
        c, sa, sb = c_ref[...], sa_ref[...], sb_ref[...]
        cqn, _, _ = _rms_fwd(cq_ref[...].astype(F32), gq_ref[...])
        qraw = _dot(cqn.astype(BF16), wq_ref[...])
        ckvn, _, _ = _rms_fwd(ckv_ref[...].astype(F32), gkv_ref[...])
        kvraw = _dot(ckvn.astype(BF16), wkv_ref[...])
        kr = _rope_fwd(kr_ref[...].astype(F32), c, sa, sb)
        for hd in range(heads):
            cols = slice(hd * HEAD_PAD, (hd + 1) * HEAD_PAD)
            q_ref[:, cols] = (_rope_fwd(qraw[:, cols], c, sa, sb) * SM_SCALE).astype(BF16)
            k_ref[:, cols] = (kvraw[:, cols] + kr).astype(BF16)
        lane = lax.broadcasted_iota(jnp.int32, (tm, hw), 1)
        v_ref[...] = jnp.where((lane & (HEAD_PAD - 1)) == V_DIM, 1.0, kvraw[:, hw:]).astype(BF16)

    out = jax.ShapeDtypeStruct((t, hw), BF16)
    return pl.pallas_call(
        body, name="qkv_fwd", grid=(t // tm,),
        in_specs=[_row_block(tm, Q_RANK, POOL_WIDTH // Q_RANK),
                  _row_block(tm, KV_RANK, (POOL_WIDTH + Q_RANK) // KV_RANK),
                  _row_block(tm, HEAD_PAD, (POOL_WIDTH + Q_RANK + KV_RANK) // HEAD_PAD),
                  _vmem_spec(), _vmem_spec(), _vmem_spec(), _vmem_spec(),
                  _row_block(tm, HEAD_PAD), _row_block(tm, HEAD_PAD), _row_block(tm, HEAD_PAD)],
        out_specs=[_row_block(tm, hw)] * 3,
        out_shape=[out, out, out],
        compiler_params=_params(("parallel",)),
    )(z, z, z, g_q, g_kv, w_uq_p, w_kv_p, rope_c, rope_sa, rope_sb)


def _heads_per_step(heads, want):
    while heads % want:
        want //= 2
    return want


def _causal_mask(rows):
    row = lax.broadcasted_iota(jnp.int32, (rows, rows), 0)
    col = lax.broadcasted_iota(jnp.int32, (rows, rows), 1)
    return col <= row


def _attn_blocks(real_len):
    tail_start = (-(-real_len // ATT_BLOCK) - 1) * ATT_BLOCK
    return tail_start // ATT_BLOCK, tail_start, _round_up(real_len - tail_start, SEQ_PAD)


def _call_with_exchange(body, exchange, *, name, grid, in_specs, out_specs, out_shape, scratch_shapes, operands):
    if exchange is None:
        return pl.pallas_call(body, name=name, grid=grid, in_specs=in_specs, out_specs=out_specs, out_shape=out_shape,
                              scratch_shapes=scratch_shapes,
                              compiler_params=_params(("parallel",) + ("arbitrary",) * (len(grid) - 1)))(*operands)
    n_in, n_out, n_scratch = len(in_specs), len(out_specs), len(scratch_shapes)

    def riding(*refs):
        ins, src = refs[:n_in], refs[n_in]
        outs, dst = refs[n_in + 1:n_in + 1 + n_out], refs[n_in + 1 + n_out]
        scratch = refs[n_in + 2 + n_out:n_in + 2 + n_out + n_scratch]
        send_sems, recv_sems = refs[n_in + 2 + n_out + n_scratch:]
        steps = [pl.program_id(a) for a in range(len(grid))]

        last = functools.reduce(jnp.logical_and, [s == g - 1 for s, g in zip(steps, grid)])

        @pl.when(functools.reduce(jnp.logical_and, [s == 0 for s in steps]))
        def _():
            exchange["start"](src, dst, send_sems, recv_sems)

        if "pass_on" in exchange:
            @pl.when(last)
            def _():
                exchange["pass_on"](src, dst, send_sems, recv_sems)

        body(*ins, *outs, *scratch)

        @pl.when(last)
        def _():
            exchange["finish"](src, dst, send_sems, recv_sems)

    n = exchange["copies"]
    return pl.pallas_call(
        riding, name=name + "_" + exchange["name"], grid=grid,
        in_specs=list(in_specs) + [_any_spec()], out_specs=list(out_specs) + [_any_spec()],
        out_shape=list(out_shape) + [exchange["out_shape"]],
        scratch_shapes=list(scratch_shapes) + [pltpu.SemaphoreType.DMA((n,)), pltpu.SemaphoreType.DMA((n,))],
        compiler_params=_params(("arbitrary",) * len(grid)),
    )(*operands, exchange["operand"])


def _attn_fwd(q3, k3, v3, real_len, exchange=None):
    b, lp, hw = q3.shape
    heads = hw // HEAD_PAD
    tb = ATT_BLOCK
    nfull, tail_start, tail = _attn_blocks(real_len)
    done = tail_start + tail
    hpg = _heads_per_step(heads, ATT_FWD_HEADS)
    width = hpg * HEAD_PAD

    def body(q_ref, k_ref, v_ref, o_ref, lse_ref):
        group = pl.program_id(1)

        @pl.when(group == 0)
        def _():
            lse_ref[...] = jnp.zeros_like(lse_ref)

        def q_rows(r0, rows, whole_kv_blocks, back):
            def kv_step(c0, keys, states, mask):
                out = []
                for hd, (m, acc) in enumerate(states):
                    cols = slice(hd * HEAD_PAD, (hd + 1) * HEAD_PAD)
                    s = _dot_nt(q_ref[0, pl.ds(r0, rows), cols], k_ref[0, pl.ds(c0, keys), cols])
                    if mask is not None:
                        s = jnp.where(mask, s, MASK_VALUE)
                    m_new = jnp.maximum(m, jnp.max(s, axis=-1, keepdims=True))
                    p = jnp.exp((s - m_new).astype(BF16))
                    acc = jnp.exp(m - m_new) * acc + _dot(p, v_ref[0, pl.ds(c0, keys), cols])
                    out.append((m_new, acc))
                return tuple(out)

            init = tuple((jnp.full((rows, 1), MASK_VALUE, F32), jnp.zeros((rows, HEAD_PAD), F32)) for _ in range(hpg))
            states = lax.fori_loop(0, whole_kv_blocks, lambda j, st: kv_step(pl.multiple_of(j * tb, tb), tb, st, None), init)
            query = lax.broadcasted_iota(jnp.int32, (rows, back + rows), 0)
            key = lax.broadcasted_iota(jnp.int32, (rows, back + rows), 1)
            states = kv_step(pl.multiple_of(r0 - back, SEQ_PAD), back + rows, states, key <= query + back)
            lane = lax.broadcasted_iota(jnp.int32, (rows, HEAD_PAD), 1)
            lse_rows = lse_ref[0, pl.ds(r0, rows), :]
            for hd, (m, acc) in enumerate(states):
                l = jnp.sum(jnp.where(lane == V_DIM, acc, 0.0), axis=-1, keepdims=True)
                o_ref[0, pl.ds(r0, rows), hd * HEAD_PAD:(hd + 1) * HEAD_PAD] = (acc / l).astype(BF16)
                lse_rows = jnp.where(lane == group * hpg + hd, m + jnp.log(l), lse_rows)
            lse_ref[0, pl.ds(r0, rows), :] = lse_rows

        def whole_block(i, carry):
            for back in range(0, tb, ATT_Q_ROWS):
                q_rows(pl.multiple_of(i * tb + back, ATT_Q_ROWS), ATT_Q_ROWS, i, back)
            return carry

        lax.fori_loop(0, nfull, whole_block, 0)
        for back in range(0, tail, ATT_Q_ROWS):
            q_rows(tail_start + back, min(ATT_Q_ROWS, tail - back), nfull, back)
        if done < lp:
            o_ref[0, done:lp, :] = jnp.zeros((lp - done, width), BF16)

    head_spec = pl.BlockSpec((1, lp, width), lambda bi, hi: (bi, 0, hi))
    return _call_with_exchange(
        body, exchange, name="attn_fwd", grid=(b, heads // hpg),
        in_specs=[head_spec, head_spec, head_spec],
        out_specs=[head_spec, pl.BlockSpec((1, lp, HEAD_PAD), lambda bi, hi: (bi, 0, 0))],
        out_shape=[jax.ShapeDtypeStruct((b, lp, hw), BF16), jax.ShapeDtypeStruct((b, lp, HEAD_PAD), F32)],
        scratch_shapes=[], operands=(q3, k3, v3))


def _merge_fwd(h, z, a, o, w_pa, w_pb_p, w_o):
    t, d = h.shape
    hw = o.shape[1]
    tm = _token_tile(t, 512)
    gate_block = Z_FIXED // d

    def body(h_ref, ga_ref, gb_ref, a_ref, o_ref, wpa_ref, wpb_ref, wo_ref, h1_ref, pa_ref, pb_ref):
        pa = _dot(a_ref[...], wpa_ref[...])
        pb = _dot(o_ref[...], wpb_ref[...])
        merged = jax.nn.sigmoid(ga_ref[...].astype(F32)) * pa + jax.nn.sigmoid(gb_ref[...].astype(F32)) * pb
        h1_ref[...] = h_ref[...] + _dot(merged.astype(BF16), wo_ref[...])
        pa_ref[...] = pa.astype(BF16)
        pb_ref[...] = pb.astype(BF16)

    return pl.pallas_call(
        body, name="merge_fwd", grid=(t // tm,),
        in_specs=[_row_block(tm, d), _row_block(tm, d, gate_block), _row_block(tm, d, gate_block + 1),
                  _row_block(tm, POOL_WIDTH), _row_block(tm, hw), _vmem_spec(), _vmem_spec(), _vmem_spec()],
        out_specs=[_row_block(tm, d)] * 3,
        out_shape=[jax.ShapeDtypeStruct((t, d), F32), jax.ShapeDtypeStruct((t, d), BF16), jax.ShapeDtypeStruct((t, d), BF16)],
        compiler_params=_params(("parallel",)),
    )(h, z, z, a, o, w_pa, w_pb_p, w_o)


def _ffn_fwd(h, g, w_gate_t, w_up_t, w_down):
    t, d = h.shape
    f = w_gate_t.shape[0]
    tm = _token_tile(t, 256)

    def body(h_ref, g_ref, wg_ref, wu_ref, wd_ref, h2_ref, a_ref, b_ref):
        x = h_ref[...]
        hn, _, _ = _rms_fwd(x, g_ref[...])
        hn = hn.astype(BF16)
        ga = _dot_nt(hn, wg_ref[...])
        up = _dot_nt(hn, wu_ref[...])
        act = ga * jax.nn.sigmoid(ga) * up
        h2_ref[...] = x + _dot(act.astype(BF16), wd_ref[...])
        a_ref[...] = ga.astype(BF16)
        b_ref[...] = up.astype(BF16)

    return pl.pallas_call(
        body, name="ffn_fwd", grid=(t // tm,),
        in_specs=[_row_block(tm, d), _vmem_spec(), _vmem_spec(), _vmem_spec(), _vmem_spec()],
        out_specs=[_row_block(tm, d), _row_block(tm, f), _row_block(tm, f)],
        out_shape=[jax.ShapeDtypeStruct((t, d), F32), jax.ShapeDtypeStruct((t, f), BF16), jax.ShapeDtypeStruct((t, f), BF16)],
        compiler_params=_params(("parallel",)),
    )(h, g, w_gate_t, w_up_t, w_down)


def _loss_head(h, g, target, valid):
    t, d = h.shape
    tm = _token_tile(t, 512)

    def body(h_ref, g_ref, t_ref, valid_ref, dh_ref, loss_ref, dg_ref):
        @pl.when(pl.program_id(0) == 0)
        def _():
            loss_ref[...] = jnp.zeros_like(loss_ref)
            dg_ref[...] = jnp.zeros_like(dg_ref)

        gain = g_ref[...]
        y, xh, r = _rms_fwd(h_ref[...], gain)
        err = (y - t_ref[...]) * valid_ref[...]
        per_row = jnp.sum(err * err, axis=-1, keepdims=True) / d
        loss_ref[...] += 0.5 * jnp.sum(per_row, axis=0, keepdims=True)
        dx, dg_rows = _rms_bwd(err / d, xh, r, gain)
        dh_ref[...] = dx
        dg_ref[...] += jnp.sum(dg_rows, axis=0, keepdims=True)

    return pl.pallas_call(
        body, name="loss_head", grid=(t // tm,),
        in_specs=[_row_block(tm, d), _vmem_spec(), _row_block(tm, d), _row_block(tm, 1)],
        out_specs=[_row_block(tm, d), pl.BlockSpec((1, 1), lambda i: (0, 0)), pl.BlockSpec((1, d), lambda i: (0, 0))],
        out_shape=[jax.ShapeDtypeStruct((t, d), F32), jax.ShapeDtypeStruct((1, 1), F32), jax.ShapeDtypeStruct((1, d), F32)],
        compiler_params=_params(("arbitrary",)),
    )(h, g, target, valid)


def _weight_grad(x, y, name):
    t, k = x.shape
    n = y.shape[1]
    tm = _token_tile(t, 8 * MXU_DEPTH)
    tk, tn = k, n
    while tk * tn * 4 > ACC_BYTES and max(tk, tn) % 256 == 0:
        if tk > tn:
            tk //= 2
        else:
            tn //= 2
    steps = t // tm

    def body(x_ref, y_ref, o_ref, acc):
        @pl.when(pl.program_id(2) == 0)
        def _():
            acc[...] = jnp.zeros_like(acc)

        acc[...] += _dot_tn(x_ref[...].astype(BF16), y_ref[...].astype(BF16))

        @pl.when(pl.program_id(2) == steps - 1)
        def _():
            o_ref[...] = acc[...].astype(BF16)

    return pl.pallas_call(
        body, name=name, grid=(k // tk, n // tn, steps),
        in_specs=[pl.BlockSpec((tm, tk), lambda a, b, i: (i, a)), pl.BlockSpec((tm, tn), lambda a, b, i: (i, b))],
        out_specs=pl.BlockSpec((tk, tn), lambda a, b, i: (a, b)),
        out_shape=jax.ShapeDtypeStruct((k, n), BF16),
        scratch_shapes=[pltpu.VMEM((tk, tn), F32)],
        compiler_params=_params(("parallel", "parallel", "arbitrary")),
    )(x, y)


def _ffn_bwd(h, dh2, a, b, g, w_gate_t, w_up_t, w_down):
    t, d = h.shape
    f = a.shape[1]
    tm = _token_tile(t, 256)

    def body(h_ref, dh2_ref, a_ref, b_ref, g_ref, wg_ref, wu_ref, wd_ref, dh_ref, hn_ref, act_ref, da_ref, db_ref, dg_ref):
        @pl.when(pl.program_id(0) == 0)
        def _():
            dg_ref[...] = jnp.zeros_like(dg_ref)

        gain = g_ref[...]
        hn, xh, r = _rms_fwd(h_ref[...], gain)
        hn_ref[...] = hn.astype(BF16)
        dh2 = dh2_ref[...]
        dact = _dot_nt(dh2.astype(BF16), wd_ref[...])
        ga = a_ref[...].astype(F32)
        up = b_ref[...].astype(F32)
        sg = jax.nn.sigmoid(ga)
        silu = ga * sg
        act_ref[...] = (silu * up).astype(BF16)
        da = (dact * up * (sg * (1.0 + ga * (1.0 - sg)))).astype(BF16)
        db = (dact * silu).astype(BF16)
        da_ref[...] = da
        db_ref[...] = db
        dhn = _dot(da, wg_ref[...]) + _dot(db, wu_ref[...])
        dx, dg_rows = _rms_bwd(dhn, xh, r, gain)
        dh_ref[...] = dh2 + dx
        dg_ref[...] += jnp.sum(dg_rows, axis=0, keepdims=True)

    return pl.pallas_call(
        body, name="ffn_bwd", grid=(t // tm,),
        in_specs=[_row_block(tm, d), _row_block(tm, d), _row_block(tm, f), _row_block(tm, f),
                  _vmem_spec(), _vmem_spec(), _vmem_spec(), _vmem_spec()],
        out_specs=[_row_block(tm, d), _row_block(tm, d), _row_block(tm, f), _row_block(tm, f), _row_block(tm, f),
                   pl.BlockSpec((1, d), lambda i: (0, 0))],
        out_shape=[jax.ShapeDtypeStruct((t, d), F32), jax.ShapeDtypeStruct((t, d), BF16), jax.ShapeDtypeStruct((t, f), BF16),
                   jax.ShapeDtypeStruct((t, f), BF16), jax.ShapeDtypeStruct((t, f), BF16), jax.ShapeDtypeStruct((1, d), F32)],
        compiler_params=_params(("arbitrary",)),
    )(h, dh2, a, b, g, w_gate_t, w_up_t, w_down)


def _merge_bwd(dh1, z, pa, pb, w_o, w_pa, w_pb_p):
    t, d = dh1.shape
    hw = w_pb_p.shape[0]
    tm = _token_tile(t, 512)
    gate_block = Z_FIXED // d

    def body(dh_ref, ga_ref, gb_ref, pa_ref, pb_ref, wo_ref, wpa_ref, wpb_ref,
             mg_ref, dpa_ref, dpb_ref, dga_ref, dgb_ref, da_ref, do_ref):
        dm = _dot_nt(dh_ref[...].astype(BF16), wo_ref[...])
        sa = jax.nn.sigmoid(ga_ref[...].astype(F32))
        sb = jax.nn.sigmoid(gb_ref[...].astype(F32))
        pa = pa_ref[...].astype(F32)
        pb = pb_ref[...].astype(F32)
        mg_ref[...] = (sa * pa + sb * pb).astype(BF16)
        dpa = (dm * sa).astype(BF16)
        dpb = (dm * sb).astype(BF16)
        dpa_ref[...] = dpa
        dpb_ref[...] = dpb
        dga_ref[...] = (dm * pa * (sa * (1.0 - sa))).astype(BF16)
        dgb_ref[...] = (dm * pb * (sb * (1.0 - sb))).astype(BF16)
        da_ref[...] = _dot_nt(dpa, wpa_ref[...]).astype(BF16)
        do_ref[...] = _dot_nt(dpb, wpb_ref[...]).astype(BF16)

    wide = jax.ShapeDtypeStruct((t, d), BF16)
    return pl.pallas_call(
        body, name="merge_bwd", grid=(t // tm,),
        in_specs=[_row_block(tm, d), _row_block(tm, d, gate_block), _row_block(tm, d, gate_block + 1),
                  _row_block(tm, d), _row_block(tm, d), _vmem_spec(), _vmem_spec(), _vmem_spec()],
        out_specs=[_row_block(tm, d)] * 5 + [_row_block(tm, POOL_WIDTH), _row_block(tm, hw)],
        out_shape=[wide] * 5 + [jax.ShapeDtypeStruct((t, POOL_WIDTH), BF16), jax.ShapeDtypeStruct((t, hw), BF16)],
        compiler_params=_params(("parallel",)),
    )(dh1, z, z, pa, pb, w_o, w_pa, w_pb_p)


def _attn_bwd(q3, k3, v3, o3, do3, lse3, real_len, exchange=None):
    b, lp, hw = q3.shape
    heads = hw // HEAD_PAD
    tb = ATT_BLOCK
    nfull, tail_start, tail = _attn_blocks(real_len)
    done = tail_start + tail
    hpg = _heads_per_step(heads, ATT_BWD_HEADS)
    width = hpg * HEAD_PAD

    def body(q_ref, k_ref, v_ref, o_ref, do_ref, lse_ref, dq_ref, dk_ref, dv_ref, dqt_acc, lse_row, delta_row):
        group = pl.program_id(1)
        lse_t = jnp.transpose(lse_ref[0])
        head_of_row = lax.broadcasted_iota(jnp.int32, (HEAD_PAD, lp), 0)
        for hd in range(hpg):
            cols = slice(hd * HEAD_PAD, (hd + 1) * HEAD_PAD)
            lse_row[hd] = jnp.sum(jnp.where(head_of_row == group * hpg + hd, lse_t, 0.0), axis=0, keepdims=True)
            prod = do_ref[0, :, cols].astype(F32) * o_ref[0, :, cols].astype(F32)
            delta_row[hd] = jnp.sum(jnp.transpose(prod), axis=0, keepdims=True)
        dqt_acc[...] = jnp.zeros_like(dqt_acc)

        def kv_rows(c0, keys, whole_q_blocks_from):
            k_t = [jnp.transpose(k_ref[0, pl.ds(c0, keys), hd * HEAD_PAD:(hd + 1) * HEAD_PAD].astype(F32)).astype(BF16)
                   for hd in range(hpg)]

            def q_step(r0, rows, states, mask):
                out = []
                for hd, (dk, dv) in enumerate(states):
                    cols = slice(hd * HEAD_PAD, (hd + 1) * HEAD_PAD)
                    q = q_ref[0, pl.ds(r0, rows), cols]
                    do = do_ref[0, pl.ds(r0, rows), cols]
                    s_t = _dot_nt(k_ref[0, pl.ds(c0, keys), cols], q)
                    if mask is not None:
                        s_t = jnp.where(mask, s_t, MASK_VALUE)
                    p_t = jnp.exp(s_t - lse_row[hd, :, pl.ds(r0, rows)])
                    dp_t = _dot_nt(v_ref[0, pl.ds(c0, keys), cols], do)
                    ds_t = (p_t * (dp_t - delta_row[hd, :, pl.ds(r0, rows)])).astype(BF16)
                    dv = dv + _dot(p_t.astype(BF16), do)
                    dk = dk + _dot(ds_t, q)
                    dqt_acc[cols, pl.ds(r0, rows)] += _dot(k_t[hd], ds_t)
                    out.append((dk, dv))
                return tuple(out)

            zero = jnp.zeros((keys, HEAD_PAD), F32)
            key_pos = lax.broadcasted_iota(jnp.int32, (keys, keys), 0)
            query_pos = lax.broadcasted_iota(jnp.int32, (keys, keys), 1)
            states = q_step(c0, keys, tuple((zero, zero) for _ in range(hpg)), key_pos <= query_pos)
            if whole_q_blocks_from is not None:
                states = lax.fori_loop(whole_q_blocks_from, nfull,
                                       lambda i, st: q_step(pl.multiple_of(i * tb, tb), tb, st, None), states)
                states = q_step(tail_start, tail, states, None)
            for hd, (dk, dv) in enumerate(states):
                cols = slice(hd * HEAD_PAD, (hd + 1) * HEAD_PAD)
                dk_ref[0, pl.ds(c0, keys), cols] = dk.astype(BF16)
                dv_ref[0, pl.ds(c0, keys), cols] = dv.astype(BF16)

        def whole_block(j, carry):
            kv_rows(pl.multiple_of(j * tb, tb), tb, j + 1)
            return carry

        lax.fori_loop(0, nfull, whole_block, 0)
        kv_rows(tail_start, tail, None)
        if done < lp:
            dk_ref[0, done:lp, :] = jnp.zeros((lp - done, width), BF16)
            dv_ref[0, done:lp, :] = jnp.zeros((lp - done, width), BF16)
        for hd in range(hpg):
            cols = slice(hd * HEAD_PAD, (hd + 1) * HEAD_PAD)
            dq_ref[0, :, cols] = jnp.transpose(dqt_acc[cols, :]).astype(BF16)

    head_spec = pl.BlockSpec((1, lp, width), lambda bi, hi: (bi, 0, hi))
    out = jax.ShapeDtypeStruct((b, lp, hw), BF16)
    return _call_with_exchange(
        body, exchange, name="attn_bwd", grid=(b, heads // hpg),
        in_specs=[head_spec] * 5 + [pl.BlockSpec((1, lp, HEAD_PAD), lambda bi, hi: (bi, 0, 0))],
        out_specs=[head_spec] * 3,
        out_shape=[out, out, out],
        scratch_shapes=[pltpu.VMEM((width, lp), F32), pltpu.VMEM((hpg, 1, lp), F32), pltpu.VMEM((hpg, 1, lp), F32)],
        operands=(q3, k3, v3, o3, do3, lse3))


def _qkv_bwd(dq, dk, dv, z, g_q, g_kv, w_uq_p, w_kv_p, rope_c, rope_sa, rope_sb):
    t, hw = dq.shape
    heads = hw // HEAD_PAD
    tm = _token_tile(t, 512)

    def body(dq_ref, dk_ref, dv_ref, cq_ref, ckv_ref, gq_ref, gkv_ref, wq_ref, wkv_ref, c_ref, sa_ref, sb_ref,
             dqraw_ref, dkvraw_ref, cqn_ref, ckvn_ref, dcq_ref, dckv_ref, dkr_ref, dgq_ref, dgkv_ref):
        @pl.when(pl.program_id(0) == 0)
        def _():
            dgq_ref[...] = jnp.zeros_like(dgq_ref)
            dgkv_ref[...] = jnp.zeros_like(dgkv_ref)

        c, sa, sb = c_ref[...], sa_ref[...], sb_ref[...]
        dkr = jnp.zeros((tm, HEAD_PAD), F32)
        for hd in range(heads):
            cols = slice(hd * HEAD_PAD, (hd + 1) * HEAD_PAD)
            dqraw_ref[:, cols] = _rope_bwd(dq_ref[:, cols].astype(F32) * SM_SCALE, c, sa, sb).astype(BF16)
            dkvraw_ref[:, cols] = dk_ref[:, cols]
            dkr = dkr + dk_ref[:, cols].astype(F32)
        dkvraw_ref[:, hw:] = dv_ref[...]
        lane = lax.broadcasted_iota(jnp.int32, (tm, HEAD_PAD), 1)
        dkr_ref[...] = jnp.where((lane >= QK_NOPE) & (lane < QK_DIM), _rope_bwd(dkr, c, sa, sb), 0.0).astype(BF16)

        gq = gq_ref[...]
        cqn, xh, r = _rms_fwd(cq_ref[...].astype(F32), gq)
        cqn_ref[...] = cqn.astype(BF16)
        dx, dg_rows = _rms_bwd(_dot_nt(dqraw_ref[...], wq_ref[...]), xh, r, gq)
        dcq_ref[...] = dx.astype(BF16)
        dgq_ref[...] += jnp.sum(dg_rows, axis=0, keepdims=True)

        gkv = gkv_ref[...]
        ckvn, xh, r = _rms_fwd(ckv_ref[...].astype(F32), gkv)
        ckvn_ref[...] = ckvn.astype(BF16)
        dx, dg_rows = _rms_bwd(_dot_nt(dkvraw_ref[...], wkv_ref[...]), xh, r, gkv)
        dckv_ref[...] = dx.astype(BF16)
        dgkv_ref[...] += jnp.sum(dg_rows, axis=0, keepdims=True)

    def shape(width, dtype=BF16):
        return jax.ShapeDtypeStruct((t, width), dtype)

    return pl.pallas_call(
        body, name="qkv_bwd", grid=(t // tm,),
        in_specs=[_row_block(tm, hw)] * 3
        + [_row_block(tm, Q_RANK, POOL_WIDTH // Q_RANK), _row_block(tm, KV_RANK, (POOL_WIDTH + Q_RANK) // KV_RANK)]
        + [_vmem_spec()] * 4 + [_row_block(tm, HEAD_PAD)] * 3,
        out_specs=[_row_block(tm, hw), _row_block(tm, 2 * hw), _row_block(tm, Q_RANK), _row_block(tm, KV_RANK),
                   _row_block(tm, Q_RANK), _row_block(tm, KV_RANK), _row_block(tm, HEAD_PAD),
                   pl.BlockSpec((1, Q_RANK), lambda i: (0, 0)), pl.BlockSpec((1, KV_RANK), lambda i: (0, 0))],
        out_shape=[shape(hw), shape(2 * hw), shape(Q_RANK), shape(KV_RANK), shape(Q_RANK), shape(KV_RANK), shape(HEAD_PAD),
                   jax.ShapeDtypeStruct((1, Q_RANK), F32), jax.ShapeDtypeStruct((1, KV_RANK), F32)],
        compiler_params=_params(("arbitrary",)),
    )(dq, dk, dv, z, z, g_q, g_kv, w_uq_p, w_kv_p, rope_c, rope_sa, rope_sb)


def _pool_bwd(z3, da3, pool_w, pool_scale):
    b, lp, _ = z3.shape
    groups = len(POOL_WINDOWS)

    def body(u_ref, da_ref, pw_ref, sc_ref, du_ref, dpw_ref, dsc_ref):
        @pl.when(pl.program_id(0) == 0)
        def _():
            dpw_ref[...] = jnp.zeros_like(dpw_ref)
            dsc_ref[...] = jnp.zeros_like(dsc_ref)

        row = lax.broadcasted_iota(jnp.int32, (lp, POOL_GROUP), 0)
        pos = row.astype(F32)
        for gi, w in enumerate(POOL_WINDOWS):
            cols = slice(gi * POOL_GROUP, (gi + 1) * POOL_GROUP)
            count = jnp.minimum(pos + 1.0, float(w))
            u = u_ref[0, :, cols].astype(F32)
            y = (_window_sum(u, w, row, True) / count - u).astype(BF16)
            yw = _dot(y, pw_ref[gi])
            da = da_ref[0, :, cols].astype(F32)
            dsc_ref[:, cols] += jnp.sum(da * yw, axis=0, keepdims=True)
            dyw = (da * sc_ref[:, cols]).astype(BF16)
            dpw_ref[gi] += _dot_tn(y, dyw)
            dy = _dot_nt(dyw, pw_ref[gi])
            du_ref[0, :, cols] = (_window_sum(dy / count, w, row, False) - dy).astype(BF16)

    return pl.pallas_call(
        body, name="pool_bwd", grid=(b,),
        in_specs=[pl.BlockSpec((1, lp, POOL_WIDTH), lambda i: (i, 0, 0)), pl.BlockSpec((1, lp, POOL_WIDTH), lambda i: (i, 0, 0)),
                  _vmem_spec(), _vmem_spec()],
        out_specs=[pl.BlockSpec((1, lp, POOL_WIDTH), lambda i: (i, 0, 0)),
                   pl.BlockSpec((groups, POOL_GROUP, POOL_GROUP), lambda i: (0, 0, 0)),
                   pl.BlockSpec((1, POOL_WIDTH), lambda i: (0, 0))],
        out_shape=[jax.ShapeDtypeStruct((b, lp, POOL_WIDTH), BF16), jax.ShapeDtypeStruct((groups, POOL_GROUP, POOL_GROUP), F32),
                   jax.ShapeDtypeStruct((1, POOL_WIDTH), F32)],
        compiler_params=_params(("arbitrary",)),
    )(z3, da3, pool_w, pool_scale)


def _in_proj_bwd(h, dh1, du, dcq, dckv, dkr, dga, dgb, g, w_in_pt):
    t, d = h.shape
    nz = w_in_pt.shape[0]
    tm = _token_tile(t, 512)
    widths = (POOL_WIDTH, Q_RANK, KV_RANK, HEAD_PAD, d, d)

    def body(h_ref, dh1_ref, du_ref, dcq_ref, dckv_ref, dkr_ref, dga_ref, dgb_ref, g_ref, w_ref, dh_ref, hn_ref, dz_ref, dg_ref):
        @pl.when(pl.program_id(0) == 0)
        def _():
            dg_ref[...] = jnp.zeros_like(dg_ref)

        gain = g_ref[...]
        hn, xh, r = _rms_fwd(h_ref[...], gain)
        hn_ref[...] = hn.astype(BF16)
        dhn = jnp.zeros((tm, d), F32)
        start = 0
        for piece, width in zip((du_ref, dcq_ref, dckv_ref, dkr_ref, dga_ref, dgb_ref), widths):
            val = piece[...]
            dz_ref[:, start:start + width] = val
            dhn = dhn + _dot(val, w_ref[start:start + width, :])
            start += width
        dx, dg_rows = _rms_bwd(dhn, xh, r, gain)
        dh_ref[...] = dh1_ref[...] + dx
        dg_ref[...] += jnp.sum(dg_rows, axis=0, keepdims=True)

    return pl.pallas_call(
        body, name="in_proj_bwd", grid=(t // tm,),
        in_specs=[_row_block(tm, d), _row_block(tm, d)] + [_row_block(tm, w) for w in widths] + [_vmem_spec(), _vmem_spec()],
        out_specs=[_row_block(tm, d), _row_block(tm, d), _row_block(tm, nz), pl.BlockSpec((1, d), lambda i: (0, 0))],
        out_shape=[jax.ShapeDtypeStruct((t, d), F32), jax.ShapeDtypeStruct((t, d), BF16), jax.ShapeDtypeStruct((t, nz), BF16),
                   jax.ShapeDtypeStruct((1, d), F32)],
        compiler_params=_params(("arbitrary",)),
    )(h, dh1, du, dcq, dckv, dkr, dga, dgb, g, w_in_pt)


def _pad_heads(w, heads, width):
    k = w.shape[0]
    w = w.reshape(k, heads, width)
    return jnp.pad(w, ((0, 0), (0, 0), (0, HEAD_PAD - width))).reshape(k, heads * HEAD_PAD)


def _unpad_heads(w, heads, width):
    k = w.shape[0]
    return w.reshape(k, heads, HEAD_PAD)[:, :, :width].reshape(k, heads * width)


def _early_layouts(w, heads):
    o3, o4 = POOL_WIDTH + Q_RANK + KV_RANK, POOL_WIDTH + Q_RANK + KV_RANK + QK_ROPE
    w_in_t = w["w_in"]
    rope_rows = jnp.pad(w_in_t[o3:o4], ((QK_NOPE, HEAD_PAD - QK_DIM), (0, 0)))
    w_in_pt = jnp.concatenate([w_in_t[:o3], rope_rows, w_in_t[o4:]], axis=0)
    w_uq_p = _pad_heads(w["w_uq"], heads, QK_DIM)
    kv = w["w_ukv"].reshape(KV_RANK, heads, QK_NOPE + V_DIM)
    w_k = jnp.pad(kv[:, :, :QK_NOPE], ((0, 0), (0, 0), (0, HEAD_PAD - QK_NOPE))).reshape(KV_RANK, heads * HEAD_PAD)
    w_v = jnp.pad(kv[:, :, QK_NOPE:], ((0, 0), (0, 0), (0, HEAD_PAD - V_DIM))).reshape(KV_RANK, heads * HEAD_PAD)
    return dict(w_in_pt=w_in_pt, w_uq_p=w_uq_p, w_kv_p=jnp.concatenate([w_k, w_v], axis=1))


def _late_layouts(w, heads):
    d = w["w_pb"].shape[1]
    w_pb_p = jnp.pad(w["w_pb"].reshape(heads, V_DIM, d), ((0, 0), (0, HEAD_PAD - V_DIM), (0, 0))).reshape(heads * HEAD_PAD, d)
    return dict(w_pa=w["w_pa"], w_pb_p=w_pb_p, w_o=w["w_o"], w_gate_t=w["w_gate"], w_up_t=w["w_up"], w_down=w["w_down"])


def _early_grad_layouts(g, heads):
    o3 = POOL_WIDTH + Q_RANK + KV_RANK
    gin = g["w_in_pt"]
    w_in = jnp.concatenate([gin[:o3], gin[o3 + QK_NOPE:o3 + QK_DIM], gin[o3 + HEAD_PAD:]], axis=0)
    hw = heads * HEAD_PAD
    gk = g["w_kv_p"][:, :hw].reshape(KV_RANK, heads, HEAD_PAD)[:, :, :QK_NOPE]
    gv = g["w_kv_p"][:, hw:].reshape(KV_RANK, heads, HEAD_PAD)[:, :, :V_DIM]
    w_ukv = jnp.concatenate([gk, gv], axis=2).reshape(KV_RANK, heads * (QK_NOPE + V_DIM))
    return dict(w_in=w_in, w_uq=_unpad_heads(g["w_uq_p"], heads, QK_DIM), w_ukv=w_ukv)


def _late_grad_layouts(g, heads):
    d = g["w_pb_p"].shape[1]
    w_pb = g["w_pb_p"].reshape(heads, HEAD_PAD, d)[:, :V_DIM].reshape(heads * V_DIM, d)
    return dict(w_pa=g["w_pa"], w_pb=w_pb, w_o=g["w_o"], w_gate=g["w_gate_t"], w_up=g["w_up_t"], w_down=g["w_down"])


def _rope_tables(lp, b):
    inv = 1.0 / (ROPE_THETA ** (jnp.arange(0, QK_ROPE, 2, dtype=F32) / QK_ROPE))
    ang = jnp.arange(lp, dtype=F32)[:, None] * inv[None, :]
    cos, sin = jnp.cos(ang), jnp.sin(ang)
    half = QK_ROPE // 2
    ones = jnp.ones((lp, QK_NOPE), F32)
    zeros_lo = jnp.zeros((lp, QK_NOPE), F32)
    zeros_hi = jnp.zeros((lp, HEAD_PAD - QK_DIM), F32)
    zeros_half = jnp.zeros((lp, half), F32)
    c = jnp.concatenate([ones, cos, cos, zeros_hi], axis=1)
    sa = jnp.concatenate([zeros_lo, zeros_half, sin, zeros_hi], axis=1)
    sb = jnp.concatenate([zeros_lo, -sin, zeros_half, zeros_hi], axis=1)
    return tuple(jnp.tile(tab, (b, 1)) for tab in (c, sa, sb))


def _local_step(x, loss_target, meta_tokens, small, early_first, riding, early_first_codec):
    b, seq, d = x.shape
    depth = 2
    heads = early_first["w_uq"].shape[1] // QK_DIM
    core = lax.axis_index("c")
    chip = 2 * lax.axis_index("x") + lax.axis_index("y")
    real_len = N_META + seq
    lp = _round_up(real_len, SEQ_PAD)
    t = b * lp
    pad = lp - N_META - seq

    meta = jnp.broadcast_to(meta_tokens[None], (b, N_META, d))
    h = jnp.concatenate([meta, x, jnp.zeros((b, pad, d), F32)], axis=1).reshape(t, d)
    target = jnp.pad(loss_target, ((0, 0), (N_META, pad), (0, 0))).reshape(t, d)
    pos = jnp.arange(lp)
    valid = jnp.tile(((pos >= N_META) & (pos < N_META + seq)).astype(F32), b).reshape(t, 1)
    rope_c, rope_sa, rope_sb = _rope_tables(lp, b)

    layers = []
    for li in range(depth):
        lay = dict(pool_w=small["pool_w"][li].astype(BF16), pool_scale=small["pool_scale"][li][None])
        for n in ("norm_mix_g", "q_norm_g", "kv_norm_g", "norm_ffn_g"):
            lay[n] = small[n][li][None]
        layers.append(lay)
    layers[0].update(_early_layouts(early_first, heads))

    saved = []
    for li in range(depth):
        lay = layers[li]
        z = _in_proj_fwd(h, lay["norm_mix_g"], lay["w_in_pt"])
        a = _pool_fwd(z.reshape(b, lp, -1), lay["pool_w"], lay["pool_scale"]).reshape(t, POOL_WIDTH)
        q, k, v = _qkv_fwd(z, lay["q_norm_g"], lay["kv_norm_g"], lay["w_uq_p"], lay["w_kv_p"], rope_c, rope_sa, rope_sb)
        hw = q.shape[1]
        halves = riding[li]["packed"].reshape(2, -1, PACK_COLS)
        o3, lse, others = _attn_fwd(q.reshape(b, lp, hw), k.reshape(b, lp, hw), v.reshape(b, lp, hw), real_len,
                                    _gather_exchange(halves))
        arrived = riding[li]["unpack"](
            lax.dynamic_update_index_in_dim(others, halves, chip, 0).reshape(N_CHIPS, -1, PACK_COLS))
        lay.update(_late_layouts({n: arrived[n, li] for n in LATE_WEIGHTS}, heads))
        if li + 1 < depth:
            layers[li + 1].update(_early_layouts({n: arrived[n, li + 1] for n in EARLY_WEIGHTS}, heads))
        o = o3.reshape(t, hw)
        h1, pa, pb = _merge_fwd(h, z, a, o, lay["w_pa"], lay["w_pb_p"], lay["w_o"])
        h2, fa, fb = _ffn_fwd(h1, lay["norm_ffn_g"], lay["w_gate_t"], lay["w_up_t"], lay["w_down"])
        saved.append(dict(h=h, z=z, a=a, q=q, k=k, v=v, o=o, lse=lse, pa=pa, pb=pb, h1=h1, fa=fa, fb=fb))
        h = h2

    dh, loss, d_final = _loss_head(h, small["final_norm_g"][None], target, valid)

    g_small = {n: [] for n in SMALL_WEIGHTS if n != "final_norm_g"}
    early_grads, parts = {}, [None] * depth
    for li in reversed(range(depth)):
        lay, sv = layers[li], saved[li]
        hw = sv["q"].shape[1]
        dh1, hn_f, act, dfa, dfb, dg_ffn = _ffn_bwd(sv["h1"], dh, sv["fa"], sv["fb"], lay["norm_ffn_g"],
                                                     lay["w_gate_t"], lay["w_up_t"], lay["w_down"])
        gl = dict(w_gate_t=_weight_grad(dfa, hn_f, "grad_w_gate"), w_up_t=_weight_grad(dfb, hn_f, "grad_w_up"),
                  w_down=_weight_grad(act, dh, "grad_w_down"))
        merged, dpa, dpb, dga, dgb, da, do = _merge_bwd(dh1, sv["z"], sv["pa"], sv["pb"], lay["w_o"], lay["w_pa"], lay["w_pb_p"])
        gl["w_o"] = _weight_grad(merged, dh1, "grad_w_o")
        gl["w_pa"] = _weight_grad(sv["a"], dpa, "grad_w_pa")
        gl["w_pb_p"] = _weight_grad(sv["o"], dpb, "grad_w_pb")
        to_send = {(n, li): g for n, g in _late_grad_layouts(gl, heads).items()}
        if li + 1 < depth:
            to_send.update({(n, li + 1): g for n, g in early_grads[li + 1].items()})
        sending = riding[li]["pack_grads"](to_send)
        shape3 = (b, lp, hw)
        dq3, dk3, dv3, from_others = _attn_bwd(
            sv["q"].reshape(shape3), sv["k"].reshape(shape3), sv["v"].reshape(shape3), sv["o"].reshape(shape3),
            do.reshape(shape3), sv["lse"], real_len, _scatter_exchange(sending))
        parts[li] = (from_others, sending)
        dqraw, dkvraw, cqn, ckvn, dcq, dckv, dkr, dg_q, dg_kv = _qkv_bwd(
            dq3.reshape(t, hw), dk3.reshape(t, hw), dv3.reshape(t, hw), sv["z"], lay["q_norm_g"], lay["kv_norm_g"],
            lay["w_uq_p"], lay["w_kv_p"], rope_c, rope_sa, rope_sb)
        gl["w_uq_p"] = _weight_grad(cqn, dqraw, "grad_w_uq")
        gl["w_kv_p"] = _weight_grad(ckvn, dkvraw, "grad_w_ukv")
        du3, dpool_w, dpool_scale = _pool_bwd(sv["z"].reshape(b, lp, -1), da.reshape(b, lp, POOL_WIDTH),
                                              lay["pool_w"], lay["pool_scale"])
        dh, hn_m, dz, dg_mix = _in_proj_bwd(sv["h"], dh1, du3.reshape(t, POOL_WIDTH), dcq, dckv, dkr, dga, dgb,
                                            lay["norm_mix_g"], lay["w_in_pt"])
        gl["w_in_pt"] = _weight_grad(dz, hn_m, "grad_w_in")
        early_grads[li] = _early_grad_layouts(gl, heads)
        for n, val in (("norm_mix_g", dg_mix[0]), ("pool_w", dpool_w), ("pool_scale", dpool_scale[0]), ("q_norm_g", dg_q[0]),
                       ("kv_norm_g", dg_kv[0]), ("norm_ffn_g", dg_ffn[0])):
            g_small[n].insert(0, val)

    dh3 = dh.reshape(b, lp, d)
    grad_x = dh3[:, N_META:N_META + seq]
    d_meta_rows = dh3[:, :N_META]
    g_small = {n: jnp.stack(v) for n, v in g_small.items()}
    g_small["final_norm_g"] = d_final[0]
    early_first_partial = early_first_codec["pack_grads"]({(n, 0): g for n, g in early_grads[0].items()})
    return loss, grad_x, d_meta_rows, g_small, early_first_partial, parts


def _mesh_place():
    x, y, c = lax.axis_index("x"), lax.axis_index("y"), lax.axis_index("c")
    others = [(1 - x, y), (x, 1 - y), (1 - x, 1 - y)]
    return x, y, c, 2 * x + y, others


def _remote(src, dst, send_sems, recv_sems, k, device):
    return pltpu.make_async_remote_copy(src_ref=src, dst_ref=dst, send_sem=send_sems.at[k], recv_sem=recv_sems.at[k],
                                        device_id=device, device_id_type=MESH)


def _gather_exchange(packed):
    def over_links(p_ref, g_ref, send_sems, recv_sems, r, slot):
        _, _, c, _, others = _mesh_place()
        ox, oy = others[r]
        return _remote(p_ref.at[c], g_ref.at[slot, c], send_sems, recv_sems, r, (ox, oy, c))

    def to_sibling(g_ref, send_sems, recv_sems, r, half):
        x, y, c, _, others = _mesh_place()
        ox, oy = others[r]
        block = g_ref.at[2 * ox + oy, c if half is None else half]
        return _remote(block, block, send_sems, recv_sems, 3 + r, (x, y, 1 - c))

    def start(p_ref, g_ref, send_sems, recv_sems):
        chip = _mesh_place()[3]
        for r in range(3):
            over_links(p_ref, g_ref, send_sems, recv_sems, r, chip).start()

    def pass_on(p_ref, g_ref, send_sems, recv_sems):
        others = _mesh_place()[4]
        for r, (ox, oy) in enumerate(others):
            over_links(p_ref, g_ref, send_sems, recv_sems, r, 2 * ox + oy).wait_recv()
            to_sibling(g_ref, send_sems, recv_sems, r, None).start()

    def finish(p_ref, g_ref, send_sems, recv_sems):
        _, _, c, chip, _ = _mesh_place()
        for r in range(3):
            to_sibling(g_ref, send_sems, recv_sems, r, 1 - c).wait_recv()
        for r in range(3):
            over_links(p_ref, g_ref, send_sems, recv_sems, r, chip).wait_send()
            to_sibling(g_ref, send_sems, recv_sems, r, None).wait_send()

    return dict(name="gather", operand=packed, copies=6, start=start, pass_on=pass_on, finish=finish,
                out_shape=jax.ShapeDtypeStruct((N_CHIPS,) + packed.shape, packed.dtype))


def _scatter_exchange(parts):
    flips = [(dx, dy, dc) for dx in (0, 1) for dy in (0, 1) for dc in (0, 1)][1:]

    def copy(p_ref, got_ref, send_sems, recv_sems, k, arriving):
        x, y, c, _, _ = _mesh_place()
        dx, dy, dc = flips[k]
        tx, ty, tc = (1 - x if dx else x), (1 - y if dy else y), (1 - c if dc else c)
        slot = 4 * tx + 2 * ty + tc if arriving else 4 * x + 2 * y + c
        return _remote(p_ref.at[2 * tx + ty, tc], got_ref.at[slot], send_sems, recv_sems, k, (tx, ty, tc))

    def start(p_ref, got_ref, send_sems, recv_sems):
        for k in range(len(flips)):
            copy(p_ref, got_ref, send_sems, recv_sems, k, False).start()

    def finish(p_ref, got_ref, send_sems, recv_sems):
        for k in range(len(flips)):
            copy(p_ref, got_ref, send_sems, recv_sems, k, True).wait_recv()
        for k in range(len(flips)):
            copy(p_ref, got_ref, send_sems, recv_sems, k, False).wait_send()

    return dict(name="scatter", operand=parts, copies=len(flips), start=start, finish=finish,
                out_shape=jax.ShapeDtypeStruct((2 * N_CHIPS,) + parts.shape[2:], parts.dtype))


def _all_gather_shards(packed, meta_shard):
    _, rh, cols = packed.shape

    def body(p_ref, m_ref, g_ref, gm_ref, send_sems, recv_sems):
        x, y, c, chip, others = _mesh_place()
        sibling = (x, y, 1 - c)
        sends = []
        for r, (ox, oy) in enumerate(others):
            sends.append(_remote(p_ref.at[c], g_ref.at[chip, c], send_sems, recv_sems, r, (ox, oy, c)))
            sends.append(_remote(m_ref, gm_ref.at[chip], send_sems, recv_sems, 6 + r, (ox, oy, c)))
        for cp in sends:
            cp.start()
        for r, (ox, oy) in enumerate(others):
            src_chip = 2 * ox + oy
            _remote(p_ref.at[c], g_ref.at[src_chip, c], send_sems, recv_sems, r, (ox, oy, c)).wait_recv()
            passed = _remote(g_ref.at[src_chip, c], g_ref.at[src_chip, c], send_sems, recv_sems, 3 + r, sibling)
            passed.start()
            sends.append(passed)
        for r, (ox, oy) in enumerate(others):
            src_chip = 2 * ox + oy
            _remote(p_ref.at[c], g_ref.at[src_chip, 1 - c], send_sems, recv_sems, 3 + r, sibling).wait_recv()
            _remote(m_ref, gm_ref.at[src_chip], send_sems, recv_sems, 6 + r, (ox, oy, c)).wait_recv()
        for cp in sends:
            cp.wait_send()

    gathered, meta_all = pl.pallas_call(
        body, name="all_gather_shards",
        in_specs=[_any_spec(), _any_spec()], out_specs=[_any_spec(), _any_spec()],
        out_shape=[jax.ShapeDtypeStruct((N_CHIPS, 2, rh, cols), packed.dtype),
                   jax.ShapeDtypeStruct((N_CHIPS,) + meta_shard.shape, meta_shard.dtype)],
        scratch_shapes=[pltpu.SemaphoreType.DMA((9,)), pltpu.SemaphoreType.DMA((9,))],
    )(packed, meta_shard)
    chip = 2 * lax.axis_index("x") + lax.axis_index("y")
    return (lax.dynamic_update_index_in_dim(gathered, packed, chip, 0),
            lax.dynamic_update_index_in_dim(meta_all, meta_shard, chip, 0))


def _pair_exchange(give):
    def body(give_ref, got_ref, send_sems, recv_sems):
        x, y, c, _, _ = _mesh_place()
        cp = _remote(give_ref, got_ref, send_sems, recv_sems, 0, (x, y, 1 - c))
        cp.start()
        cp.wait()

    return pl.pallas_call(
        body, name="pair_exchange", in_specs=[_any_spec()], out_specs=_any_spec(),
        out_shape=jax.ShapeDtypeStruct(give.shape, give.dtype),
        scratch_shapes=[pltpu.SemaphoreType.DMA((1,)), pltpu.SemaphoreType.DMA((1,))],
    )(give)


def _chip_exchange(parts):
    def body(p_ref, got_ref, send_sems, recv_sems):
        _, _, c, chip, others = _mesh_place()
        sends = [_remote(p_ref.at[2 * ox + oy], got_ref.at[chip], send_sems, recv_sems, r, (ox, oy, c))
                 for r, (ox, oy) in enumerate(others)]
        for cp in sends:
            cp.start()
        for r, (ox, oy) in enumerate(others):
            _remote(p_ref.at[chip], got_ref.at[2 * ox + oy], send_sems, recv_sems, r, (ox, oy, c)).wait_recv()
        for cp in sends:
            cp.wait_send()

    got = pl.pallas_call(
        body, name="chip_exchange", in_specs=[_any_spec()], out_specs=_any_spec(),
        out_shape=jax.ShapeDtypeStruct(parts.shape, parts.dtype),
        scratch_shapes=[pltpu.SemaphoreType.DMA((3,)), pltpu.SemaphoreType.DMA((3,))],
    )(parts)
    chip = 2 * lax.axis_index("x") + lax.axis_index("y")
    own = lax.dynamic_index_in_dim(parts, chip, 0, keepdims=False)
    return lax.dynamic_update_index_in_dim(got, own, chip, 0)


def _pair_gather(half):
    def body(h_ref, out_ref, send_sems, recv_sems):
        x, y, c, _, _ = _mesh_place()
        cp = _remote(h_ref, out_ref.at[c], send_sems, recv_sems, 0, (x, y, 1 - c))
        cp.start()
        _remote(h_ref, out_ref.at[1 - c], send_sems, recv_sems, 0, (x, y, 1 - c)).wait_recv()
        cp.wait_send()

    both = pl.pallas_call(
        body, name="pair_gather", in_specs=[_any_spec()], out_specs=_any_spec(),
        out_shape=jax.ShapeDtypeStruct((2,) + half.shape, half.dtype),
        scratch_shapes=[pltpu.SemaphoreType.DMA((1,)), pltpu.SemaphoreType.DMA((1,))],
    )(half)
    return lax.dynamic_update_index_in_dim(both, half, lax.axis_index("c"), 0)


def _row_tile(rows, limit=PACK_TILE):
    if rows <= limit:
        return rows
    for tr in range(limit, 7, -8):
        if rows % tr == 0:
            return tr
    return rows


def _pair_add(keep, got):
    n, rh, cols = keep.shape
    tr = _row_tile(rh)

    def body(k_ref, g_ref, o_ref):
        o_ref[...] = (k_ref[...].astype(F32) + g_ref[...].astype(F32)).astype(BF16)

    spec = pl.BlockSpec((1, tr, cols), lambda j, i: (j, i, 0))
    return pl.pallas_call(
        body, name="pair_add", grid=(n, rh // tr), in_specs=[spec, spec], out_specs=spec,
        out_shape=jax.ShapeDtypeStruct(keep.shape, BF16),
        compiler_params=_params(("parallel", "parallel")),
    )(keep, got)


def _chip_sum(parts):
    n, rh, cols = parts.shape
    tr = _row_tile(rh)

    def body(p_ref, o_ref):
        total = p_ref[0].astype(F32)
        for k in range(1, n):
            total = total + p_ref[k].astype(F32)
        o_ref[...] = total

    return pl.pallas_call(
        body, name="chip_sum", grid=(rh // tr,),
        in_specs=[pl.BlockSpec((n, tr, cols), lambda i: (0, i, 0))], out_specs=pl.BlockSpec((tr, cols), lambda i: (i, 0)),
        out_shape=jax.ShapeDtypeStruct((rh, cols), F32),
        compiler_params=_params(("parallel",)),
    )(parts)


def _device_sum(from_others, sending):
    n, rh, cols = from_others.shape
    tr = _row_tile(rh)
    x, y, c = lax.axis_index("x"), lax.axis_index("y"), lax.axis_index("c")
    place = jnp.stack([2 * x + y, c, 4 * x + 2 * y + c]).astype(jnp.int32)

    def body(place_ref, p_ref, own_ref, o_ref):
        me = place_ref[2]
        own = own_ref[0, 0].astype(F32)
        total = jnp.where(me == 0, own, p_ref[0].astype(F32))
        for k in range(1, n):
            total = total + jnp.where(me == k, own, p_ref[k].astype(F32))
        o_ref[...] = total

    return pl.pallas_call(
        body, name="device_sum",
        grid_spec=pltpu.PrefetchScalarGridSpec(
            num_scalar_prefetch=1, grid=(rh // tr,),
            in_specs=[pl.BlockSpec((n, tr, cols), lambda i, place_ref: (0, i, 0)),
                      pl.BlockSpec((1, 1, tr, cols), lambda i, place_ref: (place_ref[0], place_ref[1], i, 0))],
            out_specs=pl.BlockSpec((tr, cols), lambda i, place_ref: (i, 0))),
        out_shape=jax.ShapeDtypeStruct((rh, cols), F32),
        compiler_params=_params(("parallel",)),
    )(place, from_others, sending)


def _reduce_scatter(grads, c):
    keep = lax.dynamic_index_in_dim(grads, c, axis=1, keepdims=False)
    give = lax.dynamic_index_in_dim(grads, 1 - c, axis=1, keepdims=False)
    chip_partial = _pair_add(keep, _pair_exchange(give))
    return _pair_gather(_chip_sum(_chip_exchange(chip_partial)))


def _all_reduce_small(meta_rows, small):
    b, rm, cols = meta_rows.shape
    rows = rm + small.shape[0]

    def body(meta_ref, small_ref, out_ref, mine, pair_buf, chip_buf, send_sems, recv_sems):
        x, y, c, chip, others = _mesh_place()
        acc = meta_ref[0]
        for i in range(1, b):
            acc = acc + meta_ref[i]
        mine[0:rm, :] = acc
        mine[rm:rows, :] = small_ref[...]
        pair = _remote(mine, pair_buf, send_sems, recv_sems, 0, (x, y, 1 - c))
        pair.start()
        pair.wait()
        chip_buf[chip] = mine[...] + pair_buf[...]
        sends = [_remote(chip_buf.at[chip], chip_buf.at[chip], send_sems, recv_sems, 1 + r, (ox, oy, c))
                 for r, (ox, oy) in enumerate(others)]
        for cp in sends:
            cp.start()
        for r, (ox, oy) in enumerate(others):
            _remote(chip_buf.at[chip], chip_buf.at[2 * ox + oy], send_sems, recv_sems, 1 + r, (ox, oy, c)).wait_recv()
        for cp in sends:
            cp.wait_send()
        out_ref[...] = ((chip_buf[0] + chip_buf[1]) + chip_buf[2]) + chip_buf[3]

    return pl.pallas_call(
        body, name="all_reduce_small",
        in_specs=[_vmem_spec(), _vmem_spec()], out_specs=_vmem_spec(),
        out_shape=jax.ShapeDtypeStruct((rows, cols), F32),
        scratch_shapes=[pltpu.VMEM((rows, cols), F32), pltpu.VMEM((rows, cols), F32), pltpu.VMEM((N_CHIPS, rows, cols), F32),
                        pltpu.SemaphoreType.DMA((4,)), pltpu.SemaphoreType.DMA((4,))],
        compiler_params=pltpu.CompilerParams(vmem_limit_bytes=VMEM_LIMIT),
    )(meta_rows, small)


def _adamw(w, g, m, v):
    shape = w.shape
    cols = shape[-1]
    rows = w.size // cols
    tr = _row_tile(rows)

    def body(w_ref, g_ref, m_ref, v_ref, d_ref, m2_ref, v2_ref):
        grad = g_ref[...]
        m2 = ADAM_B1 * m_ref[...] + (1.0 - ADAM_B1) * grad
        v2 = ADAM_B2 * v_ref[...] + (1.0 - ADAM_B2) * jnp.square(grad)
        m_hat = m2 / (1.0 - ADAM_B1 ** ADAM_STEP)
        v_hat = v2 / (1.0 - ADAM_B2 ** ADAM_STEP)
        d_ref[...] = -ADAM_LR * (m_hat / (jnp.sqrt(v_hat) + ADAM_EPS) + ADAM_WD * w_ref[...])
        m2_ref[...] = m2
        v2_ref[...] = v2

    spec = pl.BlockSpec((tr, cols), lambda i: (i, 0))
    out = jax.ShapeDtypeStruct((rows, cols), F32)
    res = pl.pallas_call(
        body, name="adamw", grid=(rows // tr,), in_specs=[spec] * 4, out_specs=[spec] * 3, out_shape=[out] * 3,
        compiler_params=_params(("parallel",)),
    )(*(a.reshape(rows, cols) for a in (w, g, m, v)))
    return tuple(r.reshape(shape) for r in res)


def _pack_rows(arrays):
    flat = [a.reshape(-1, PACK_COLS) for a in arrays]
    counts = [f.shape[0] for f in flat]
    total = sum(counts)
    half = -(-total // 2)
    tiles = -(-half // PACK_TILE)
    padded = 2 * tiles * _round_up(-(-half // tiles), 16)
    if padded > total:
        flat.append(jnp.zeros((padded - total, PACK_COLS), flat[0].dtype))
    return jnp.concatenate(flat, axis=0), counts


def _unpack_rows(buffer, counts, shapes):
    out, start = [], 0
    for n, shape in zip(counts, shapes):
        out.append(buffer[..., start:start + n, :].reshape(buffer.shape[:-2] + tuple(shape)))
        start += n
    return out


def _group_codec(entries, weights):
    turned = [n in TRANSPOSED_WEIGHTS for n, _ in entries]
    shapes = [weights[n].shape[1:][::-1] if t else weights[n].shape[1:] for (n, _), t in zip(entries, turned)]
    by_rows = [t or SHARD_AXIS[n] == 1 for (n, _), t in zip(entries, turned)]
    packed, counts = _pack_rows([(weights[n][li].T if t else weights[n][li]).astype(BF16) for (n, li), t in zip(entries, turned)])
    rows = packed.shape[0]
    pad_rows = rows - sum(counts)

    def unpack(per_chip_packed):
        out = {}
        for entry, (s0, s1), rowwise, blk in zip(entries, shapes, by_rows, _unpack_rows(per_chip_packed, counts, shapes)):
            out[entry] = blk.reshape(N_CHIPS * s0, s1) if rowwise else jnp.transpose(blk, (1, 0, 2)).reshape(s0, N_CHIPS * s1)
        return out

    def pack_grads(whole):
        pieces = []
        for entry, (s0, s1), rowwise in zip(entries, shapes, by_rows):
            g = whole[entry]
            by_chip = g.reshape(N_CHIPS, s0, s1) if rowwise else jnp.transpose(g.reshape(s0, N_CHIPS, s1), (1, 0, 2))
            pieces.append(by_chip.reshape(N_CHIPS, -1, PACK_COLS))
        if pad_rows:
            pieces.append(jnp.zeros((N_CHIPS, pad_rows, PACK_COLS), BF16))
        return jnp.concatenate(pieces, axis=1).reshape(N_CHIPS, 2, rows // 2, PACK_COLS)

    def unpack_reduced(reduced):
        shards = _unpack_rows(reduced.reshape(rows, PACK_COLS), counts, shapes)
        return {entry: s.T if t else s for entry, s, t in zip(entries, shards, turned)}

    return dict(packed=packed, unpack=unpack, pack_grads=pack_grads, unpack_reduced=unpack_reduced)


def kernel(x, meta_tokens, norm_mix_g, w_in, pool_w, pool_scale, q_norm_g, kv_norm_g, w_uq, w_ukv, w_pa, w_pb, w_o, norm_ffn_g, w_gate, w_up, w_down, final_norm_g, loss_target, m_meta_tokens, m_norm_mix_g, m_w_in, m_pool_w, m_pool_scale, m_q_norm_g, m_kv_norm_g, m_w_uq, m_w_ukv, m_w_pa, m_w_pb, m_w_o, m_norm_ffn_g, m_w_gate, m_w_up, m_w_down, m_final_norm_g, v_meta_tokens, v_norm_mix_g, v_w_in, v_pool_w, v_pool_scale, v_q_norm_g, v_kv_norm_g, v_w_uq, v_w_ukv, v_w_pa, v_w_pb, v_w_o, v_norm_ffn_g, v_w_gate, v_w_up, v_w_down, v_final_norm_g):
    weights = dict(meta_tokens=meta_tokens, norm_mix_g=norm_mix_g, w_in=w_in, pool_w=pool_w, pool_scale=pool_scale,
                   q_norm_g=q_norm_g, kv_norm_g=kv_norm_g, w_uq=w_uq, w_ukv=w_ukv, w_pa=w_pa, w_pb=w_pb, w_o=w_o,
                   norm_ffn_g=norm_ffn_g, w_gate=w_gate, w_up=w_up, w_down=w_down, final_norm_g=final_norm_g)
    first = dict(meta_tokens=m_meta_tokens, norm_mix_g=m_norm_mix_g, w_in=m_w_in, pool_w=m_pool_w, pool_scale=m_pool_scale,
                 q_norm_g=m_q_norm_g, kv_norm_g=m_kv_norm_g, w_uq=m_w_uq, w_ukv=m_w_ukv, w_pa=m_w_pa, w_pb=m_w_pb, w_o=m_w_o,
                 norm_ffn_g=m_norm_ffn_g, w_gate=m_w_gate, w_up=m_w_up, w_down=m_w_down, final_norm_g=m_final_norm_g)
    second = dict(meta_tokens=v_meta_tokens, norm_mix_g=v_norm_mix_g, w_in=v_w_in, pool_w=v_pool_w, pool_scale=v_pool_scale,
                  q_norm_g=v_q_norm_g, kv_norm_g=v_kv_norm_g, w_uq=v_w_uq, w_ukv=v_w_ukv, w_pa=v_w_pa, w_pb=v_w_pb, w_o=v_w_o,
                  norm_ffn_g=v_norm_ffn_g, w_gate=v_w_gate, w_up=v_w_up, w_down=v_w_down, final_norm_g=v_final_norm_g)
    core = lax.axis_index("c")
    chip = 2 * lax.axis_index("x") + lax.axis_index("y")
    d = x.shape[-1]
    meta_cols = meta_tokens.shape[1]

    early_first = _group_codec([(n, 0) for n in EARLY_WEIGHTS], weights)
    riding = [_group_codec([(n, 0) for n in LATE_WEIGHTS] + [(n, 1) for n in EARLY_WEIGHTS], weights),
              _group_codec([(n, 1) for n in LATE_WEIGHTS], weights)]
    gathered, meta_all = _all_gather_shards(early_first["packed"].reshape(2, -1, PACK_COLS), meta_tokens)
    early_weights = early_first["unpack"](gathered.reshape(N_CHIPS, -1, PACK_COLS))
    meta_full = jnp.concatenate([meta_all[j] for j in range(N_CHIPS)], axis=1)
    small = {n: weights[n] for n in SMALL_WEIGHTS}

    loss, grad_x, d_meta_rows, g_small, early_partial, parts = _local_step(
        x, loss_target, meta_full, small, {n: early_weights[n, 0] for n in EARLY_WEIGHTS}, riding, early_first)

    shards = early_first["unpack_reduced"](_reduce_scatter(early_partial, core))
    for codec, (from_others, sending) in zip(riding, parts):
        shards.update(codec["unpack_reduced"](_pair_gather(_device_sum(from_others, sending))))
    grads = {n: jnp.stack([shards[n, 0], shards[n, 1]]) for n in BIG_WEIGHTS}

    small_shapes = [weights[n].shape for n in SMALL_WEIGHTS]
    small_flat = jnp.concatenate([g_small[n].reshape(-1) for n in SMALL_WEIGHTS])
    small_len = small_flat.shape[0]
    small_rows = _round_up(-(-small_len // PACK_COLS), 8)
    small_pack = jnp.pad(small_flat, (0, small_rows * PACK_COLS - small_len)).reshape(small_rows, PACK_COLS)
    meta_rows = N_META * d // PACK_COLS
    summed = _all_reduce_small(d_meta_rows.reshape(-1, meta_rows, PACK_COLS), small_pack)
    grad_meta_full = summed[:meta_rows].reshape(N_META, d)
    grads["meta_tokens"] = lax.dynamic_slice_in_dim(grad_meta_full, chip * meta_cols, meta_cols, axis=1)
    small_sum = summed[meta_rows:].reshape(-1)
    start = 0
    for n, shape in zip(SMALL_WEIGHTS, small_shapes):
        size = 1
        for s in shape:
            size *= s
        grads[n] = small_sum[start:start + size].reshape(shape)
        start += size

    deltas, new_m, new_v = {}, {}, {}
    for n in WEIGHT_ORDER:
        deltas[n], new_m[n], new_v[n] = _adamw(weights[n], grads[n], first[n], second[n])

    total_loss = lax.psum(loss[0, 0], ("x", "y", "c"))
    return (total_loss, grad_x, *[grads[n] for n in WEIGHT_ORDER], *[deltas[n] for n in WEIGHT_ORDER],
            *[new_m[n] for n in WEIGHT_ORDER], *[new_v[n] for n in WEIGHT_ORDER])
```

```python
import functools

import jax
import jax.numpy as jnp
from jax import lax
from jax.experimental import pallas as pl
from jax.experimental.pallas import tpu as pltpu

F32 = jnp.float32
BF16 = jnp.bfloat16

N_META = 16
POOL_WINDOWS = (2, 4, 8, 16)
POOL_GROUP = 128
POOL_WIDTH = POOL_GROUP * len(POOL_WINDOWS)
QK_NOPE = 64
QK_ROPE = 32
V_DIM = 64
QK_DIM = QK_NOPE + QK_ROPE
Q_RANK = 256
KV_RANK = 128
HEAD_PAD = 128
SM_SCALE = QK_DIM ** -0.5
ROPE_THETA = 10000.0
NORM_EPS = 1e-6
MASK_VALUE = -1e30
Z_FIXED = POOL_WIDTH + Q_RANK + KV_RANK + HEAD_PAD

ADAM_LR = 0.001
ADAM_B1 = 0.9
ADAM_B2 = 0.999
ADAM_EPS = 1e-08
ADAM_WD = 0.01
ADAM_STEP = 10

N_CHIPS = 4
ATT_BLOCK = 256
SEQ_PAD = 128
ATT_Q_ROWS = 256
ATT_FWD_HEADS = 8
ATT_BWD_HEADS = 4
PACK_COLS = 1024
PACK_TILE = 512
VMEM_LIMIT = 60 * 1024 * 1024
MXU_DEPTH = 256
ACC_BYTES = 8 * 1024 * 1024

MESH = pl.DeviceIdType.MESH

BIG_WEIGHTS = ("w_in", "w_uq", "w_ukv", "w_pa", "w_pb", "w_o", "w_gate", "w_up", "w_down")
EARLY_WEIGHTS = ("w_in", "w_uq", "w_ukv")
LATE_WEIGHTS = ("w_pa", "w_pb", "w_o", "w_gate", "w_up", "w_down")
TRANSPOSED_WEIGHTS = ("w_in", "w_gate", "w_up")
SHARD_AXIS = {"w_in": 2, "w_uq": 2, "w_ukv": 2, "w_pa": 2, "w_pb": 1, "w_o": 1, "w_gate": 2, "w_up": 2, "w_down": 1}
SMALL_WEIGHTS = ("norm_mix_g", "pool_w", "pool_scale", "q_norm_g", "kv_norm_g", "norm_ffn_g", "final_norm_g")
WEIGHT_ORDER = ("meta_tokens", "norm_mix_g", "w_in", "pool_w", "pool_scale", "q_norm_g", "kv_norm_g", "w_uq", "w_ukv",
                "w_pa", "w_pb", "w_o", "norm_ffn_g", "w_gate", "w_up", "w_down", "final_norm_g")


def _round_up(n, m):
    return -(-n // m) * m


def _vmem_spec():
    return pl.BlockSpec(memory_space=pltpu.VMEM)


def _any_spec():
    return pl.BlockSpec(memory_space=pl.ANY)


def _row_block(tm, width, col_block=0):
    return pl.BlockSpec((tm, width), lambda i, cb=col_block: (i, cb))


def _params(sem, vmem=VMEM_LIMIT):
    return pltpu.CompilerParams(dimension_semantics=sem, vmem_limit_bytes=vmem)


def _token_tile(t, want):
    best = SEQ_PAD
    for tm in range(32, min(t, 2 * want) + 1, 32):
        if t % tm == 0 and abs(tm - want) < abs(best - want):
            best = tm
    return best


def _dot(a, b):
    return jnp.dot(a, b, preferred_element_type=F32)


def _dot_nt(a, b):
    return lax.dot_general(a, b, (((1,), (1,)), ((), ())), preferred_element_type=F32)


def _dot_tn(a, b):
    return lax.dot_general(a, b, (((0,), (0,)), ((), ())), preferred_element_type=F32)


def _rms_fwd(x, g):
    r = lax.rsqrt(jnp.mean(x * x, axis=-1, keepdims=True) + NORM_EPS)
    xh = x * r
    return xh * g, xh, r


def _rms_bwd(dy, xh, r, g):
    gdy = dy * g
    dx = r * (gdy - xh * jnp.mean(xh * gdy, axis=-1, keepdims=True))
    return dx, dy * xh


def _rope_fwd(x, c, sa, sb):
    return x * c + pltpu.roll(x, 16, 1) * sa + pltpu.roll(x, HEAD_PAD - 16, 1) * sb


def _rope_bwd(d, c, sa, sb):
    return d * c + pltpu.roll(d * sa, HEAD_PAD - 16, 1) + pltpu.roll(d * sb, 16, 1)


def _in_proj_fwd(h, g, w_in_pt):
    t, d = h.shape
    nz = w_in_pt.shape[0]
    tm = _token_tile(t, 512)

    def body(h_ref, g_ref, w_ref, z_ref):
        hn, _, _ = _rms_fwd(h_ref[...], g_ref[...])
        z_ref[...] = _dot_nt(hn.astype(BF16), w_ref[...]).astype(BF16)

    return pl.pallas_call(
        body, name="in_proj_fwd", grid=(t // tm,),
        in_specs=[_row_block(tm, d), _vmem_spec(), _vmem_spec()],
        out_specs=_row_block(tm, nz),
        out_shape=jax.ShapeDtypeStruct((t, nz), BF16),
        compiler_params=_params(("parallel",)),
    )(h, g, w_in_pt)


def _window_sum(x, w, row, forward):
    n = x.shape[0]
    s = x
    k = 1
    while k < w:
        if forward:
            s = s + jnp.where(row >= k, pltpu.roll(s, k, 0), 0.0)
        else:
            s = s + jnp.where(row < n - k, pltpu.roll(s, n - k, 0), 0.0)
        k *= 2
    return s


def _pool_fwd(z3, pool_w, pool_scale):
    b, lp, _ = z3.shape

    def body(u_ref, pw_ref, sc_ref, a_ref):
        row = lax.broadcasted_iota(jnp.int32, (lp, POOL_GROUP), 0)
        pos = row.astype(F32)
        for gi, w in enumerate(POOL_WINDOWS):
            cols = slice(gi * POOL_GROUP, (gi + 1) * POOL_GROUP)
            u = u_ref[0, :, cols].astype(F32)
            y = _window_sum(u, w, row, True) / jnp.minimum(pos + 1.0, float(w)) - u
            yw = _dot(y.astype(BF16), pw_ref[gi])
            a_ref[0, :, cols] = (yw * sc_ref[:, cols]).astype(BF16)

    return pl.pallas_call(
        body, name="pool_fwd", grid=(b,),
        in_specs=[pl.BlockSpec((1, lp, POOL_WIDTH), lambda i: (i, 0, 0)), _vmem_spec(), _vmem_spec()],
        out_specs=pl.BlockSpec((1, lp, POOL_WIDTH), lambda i: (i, 0, 0)),
        out_shape=jax.ShapeDtypeStruct((b, lp, POOL_WIDTH), BF16),
        compiler_params=_params(("parallel",)),
    )(z3, pool_w, pool_scale)


def _qkv_fwd(z, g_q, g_kv, w_uq_p, w_kv_p, rope_c, rope_sa, rope_sb):
    t = z.shape[0]
    hw = w_uq_p.shape[1]
    heads = hw // HEAD_PAD
    tm = _token_tile(t, 512)

    def body(cq_ref, ckv_ref, kr_ref, gq_ref, gkv_ref, wq_ref, wkv_ref, c_ref, sa_ref, sb_ref, q_ref, k_ref, v_ref):
        c, sa, sb = c_ref[...], sa_ref[...], sb_ref[...]
        cqn, _, _ = _rms_fwd(cq_ref[...].astype(F32), gq_ref[...])
        qraw = _dot(cqn.astype(BF16), wq_ref[...])
        ckvn, _, _ = _rms_fwd(ckv_ref[...].astype(F32), gkv_ref[...])
        kvraw = _dot(ckvn.astype(BF16), wkv_ref[...])
        kr = _rope_fwd(kr_ref[...].astype(F32), c, sa, sb)
        for hd in range(heads):
            cols = slice(hd * HEAD_PAD, (hd + 1) * HEAD_PAD)
            q_ref[:, cols] = (_rope_fwd(qraw[:, cols], c, sa, sb) * SM_SCALE).astype(BF16)
            k_ref[:, cols] = (kvraw[:, cols] + kr).astype(BF16)
        lane = lax.broadcasted_iota(jnp.int32, (tm, hw), 1)
        v_ref[...] = jnp.where((lane & (HEAD_PAD - 1)) == V_DIM, 1.0, kvraw[:, hw:]).astype(BF16)

    out = jax.ShapeDtypeStruct((t, hw), BF16)
    return pl.pallas_call(
        body, name="qkv_fwd", grid=(t // tm,),
        in_specs=[_row_block(tm, Q_RANK, POOL_WIDTH // Q_RANK),
                  _row_block(tm, KV_RANK, (POOL_WIDTH + Q_RANK) // KV_RANK),
                  _row_block(tm, HEAD_PAD, (POOL_WIDTH + Q_RANK + KV_RANK) // HEAD_PAD),
                  _vmem_spec(), _vmem_spec(), _vmem_spec(), _vmem_spec(),
                  _row_block(tm, HEAD_PAD), _row_block(tm, HEAD_PAD), _row_block(tm, HEAD_PAD)],
        out_specs=[_row_block(tm, hw)] * 3,
        out_shape=[out, out, out],
        compiler_params=_params(("parallel",)),
    )(z, z, z, g_q, g_kv, w_uq_p, w_kv_p, rope_c, rope_sa, rope_sb)


def _heads_per_step(heads, want):
    while heads % want:
        want //= 2
    return want


def _causal_mask(rows):
    row = lax.broadcasted_iota(jnp.int32, (rows, rows), 0)
    col = lax.broadcasted_iota(jnp.int32, (rows, rows), 1)
    return col <= row


def _attn_blocks(real_len, granule=SEQ_PAD):
    tail_start = (-(-real_len // ATT_BLOCK) - 1) * ATT_BLOCK
    return tail_start // ATT_BLOCK, tail_start, _round_up(real_len - tail_start, granule)


def _call_with_exchange(body, exchange, *, name, grid, in_specs, out_specs, out_shape, scratch_shapes, operands):
    if exchange is None:
        return pl.pallas_call(body, name=name, grid=grid, in_specs=in_specs, out_specs=out_specs, out_shape=out_shape,
                              scratch_shapes=scratch_shapes,
                              compiler_params=_params(("parallel",) + ("arbitrary",) * (len(grid) - 1)))(*operands)
    n_in, n_out, n_scratch = len(in_specs), len(out_specs), len(scratch_shapes)

    def riding(*refs):
        ins, src = refs[:n_in], refs[n_in]
        outs, dst = refs[n_in + 1:n_in + 1 + n_out], refs[n_in + 1 + n_out]
        scratch = refs[n_in + 2 + n_out:n_in + 2 + n_out + n_scratch]
        send_sems, recv_sems = refs[n_in + 2 + n_out + n_scratch:]
        steps = [pl.program_id(a) for a in range(len(grid))]

        last = functools.reduce(jnp.logical_and, [s == g - 1 for s, g in zip(steps, grid)])

        @pl.when(functools.reduce(jnp.logical_and, [s == 0 for s in steps]))
        def _():
            exchange["start"](src, dst, send_sems, recv_sems)

        if "pass_on" in exchange:
            @pl.when(last)
            def _():
                exchange["pass_on"](src, dst, send_sems, recv_sems)

        body(*ins, *outs, *scratch)

        @pl.when(last)
        def _():
            exchange["finish"](src, dst, send_sems, recv_sems)

    n = exchange["copies"]
    return pl.pallas_call(
        riding, name=name + "_" + exchange["name"], grid=grid,
        in_specs=list(in_specs) + [_any_spec()], out_specs=list(out_specs) + [_any_spec()],
        out_shape=list(out_shape) + [exchange["out_shape"]],
        scratch_shapes=list(scratch_shapes) + [pltpu.SemaphoreType.DMA((n,)), pltpu.SemaphoreType.DMA((n,))],
        compiler_params=_params(("arbitrary",) * len(grid)),
    )(*operands, exchange["operand"])


def _attn_fwd(q3, k3, v3, real_len, exchange=None):
    b, lp, hw = q3.shape
    heads = hw // HEAD_PAD
    tb = ATT_BLOCK
    nfull, tail_start, tail = _attn_blocks(real_len, 32)
    done = tail_start + tail
    hpg = _heads_per_step(heads, ATT_FWD_HEADS)
    width = hpg * HEAD_PAD

    def body(q_ref, k_ref, v_ref, o_ref, lse_ref):
        group = pl.program_id(1)

        @pl.when(group == 0)
        def _():
            lse_ref[...] = jnp.zeros_like(lse_ref)

        def q_rows(r0, rows, whole_kv_blocks, back):
            def kv_step(c0, keys, states, mask):
                out = []
                for hd, (m, acc) in enumerate(states):
                    cols = slice(hd * HEAD_PAD, (hd + 1) * HEAD_PAD)
                    s = _dot_nt(q_ref[0, pl.ds(r0, rows), cols], k_ref[0, pl.ds(c0, keys), cols])
                    if mask is not None:
                        s = jnp.where(mask, s, MASK_VALUE)
                    m_new = jnp.maximum(m, jnp.max(s, axis=-1, keepdims=True))
                    p = jnp.exp((s - m_new).astype(BF16))
                    acc = jnp.exp(m - m_new) * acc + _dot(p, v_ref[0, pl.ds(c0, keys), cols])
                    out.append((m_new, acc))
                return tuple(out)

            init = tuple((jnp.full((rows, 1), MASK_VALUE, F32), jnp.zeros((rows, HEAD_PAD), F32)) for _ in range(hpg))
            states = lax.fori_loop(0, whole_kv_blocks, lambda j, st: kv_step(pl.multiple_of(j * tb, tb), tb, st, None), init)
            query = lax.broadcasted_iota(jnp.int32, (rows, back + rows), 0)
            key = lax.broadcasted_iota(jnp.int32, (rows, back + rows), 1)
            states = kv_step(pl.multiple_of(r0 - back, SEQ_PAD), back + rows, states, key <= query + back)
            lane = lax.broadcasted_iota(jnp.int32, (rows, HEAD_PAD), 1)
            lse_rows = lse_ref[0, pl.ds(r0, rows), :]
            for hd, (m, acc) in enumerate(states):
                l = jnp.sum(jnp.where(lane == V_DIM, acc, 0.0), axis=-1, keepdims=True)
                o_ref[0, pl.ds(r0, rows), hd * HEAD_PAD:(hd + 1) * HEAD_PAD] = (acc / l).astype(BF16)
                lse_rows = jnp.where(lane == group * hpg + hd, m + jnp.log(l), lse_rows)
            lse_ref[0, pl.ds(r0, rows), :] = lse_rows

        def whole_block(i, carry):
            for back in range(0, tb, ATT_Q_ROWS):
                q_rows(pl.multiple_of(i * tb + back, ATT_Q_ROWS), ATT_Q_ROWS, i, back)
            return carry

        lax.fori_loop(0, nfull, whole_block, 0)
        for back in range(0, tail, ATT_Q_ROWS):
            q_rows(tail_start + back, min(ATT_Q_ROWS, tail - back), nfull, back)
        if done < lp:
            o_ref[0, done:lp, :] = jnp.zeros((lp - done, width), BF16)

    head_spec = pl.BlockSpec((1, lp, width), lambda bi, hi: (bi, 0, hi))
    return _call_with_exchange(
        body, exchange, name="attn_fwd", grid=(b, heads // hpg),
        in_specs=[head_spec, head_spec, head_spec],
        out_specs=[head_spec, pl.BlockSpec((1, lp, HEAD_PAD), lambda bi, hi: (bi, 0, 0))],
        out_shape=[jax.ShapeDtypeStruct((b, lp, hw), BF16), jax.ShapeDtypeStruct((b, lp, HEAD_PAD), F32)],
        scratch_shapes=[], operands=(q3, k3, v3))


def _merge_fwd(h, z, a, o, w_pa, w_pb_p, w_o):
    t, d = h.shape
    hw = o.shape[1]
    tm = _token_tile(t, 512)
    gate_block = Z_FIXED // d

    def body(h_ref, ga_ref, gb_ref, a_ref, o_ref, wpa_ref, wpb_ref, wo_ref, h1_ref, pa_ref, pb_ref):
        pa = _dot(a_ref[...], wpa_ref[...])
        pb = _dot(o_ref[...], wpb_ref[...])
        merged = jax.nn.sigmoid(ga_ref[...].astype(F32)) * pa + jax.nn.sigmoid(gb_ref[...].astype(F32)) * pb
        h1_ref[...] = h_ref[...] + _dot(merged.astype(BF16), wo_ref[...])
        pa_ref[...] = pa.astype(BF16)
        pb_ref[...] = pb.astype(BF16)

    return pl.pallas_call(
        body, name="merge_fwd", grid=(t // tm,),
        in_specs=[_row_block(tm, d), _row_block(tm, d, gate_block), _row_block(tm, d, gate_block + 1),
                  _row_block(tm, POOL_WIDTH), _row_block(tm, hw), _vmem_spec(), _vmem_spec(), _vmem_spec()],
        out_specs=[_row_block(tm, d)] * 3,
        out_shape=[jax.ShapeDtypeStruct((t, d), F32), jax.ShapeDtypeStruct((t, d), BF16), jax.ShapeDtypeStruct((t, d), BF16)],
        compiler_params=_params(("parallel",)),
    )(h, z, z, a, o, w_pa, w_pb_p, w_o)


def _ffn_fwd(h, g, w_gate_t, w_up_t, w_down):
    t, d = h.shape
    f = w_gate_t.shape[0]
    tm = _token_tile(t, 256)

    def body(h_ref, g_ref, wg_ref, wu_ref, wd_ref, h2_ref, a_ref, b_ref):
        x = h_ref[...]
        hn, _, _ = _rms_fwd(x, g_ref[...])
        hn = hn.astype(BF16)
        ga = _dot_nt(hn, wg_ref[...])
        up = _dot_nt(hn, wu_ref[...])
        act = ga * jax.nn.sigmoid(ga) * up
        h2_ref[...] = x + _dot(act.astype(BF16), wd_ref[...])
        a_ref[...] = ga.astype(BF16)
        b_ref[...] = up.astype(BF16)

    return pl.pallas_call(
        body, name="ffn_fwd", grid=(t // tm,),
        in_specs=[_row_block(tm, d), _vmem_spec(), _vmem_spec(), _vmem_spec(), _vmem_spec()],
        out_specs=[_row_block(tm, d), _row_block(tm, f), _row_block(tm, f)],
        out_shape=[jax.ShapeDtypeStruct((t, d), F32), jax.ShapeDtypeStruct((t, f), BF16), jax.ShapeDtypeStruct((t, f), BF16)],
        compiler_params=_params(("parallel",)),
    )(h, g, w_gate_t, w_up_t, w_down)


def _loss_head(h, g, target, valid):
    t, d = h.shape
    tm = _token_tile(t, 512)

    def body(h_ref, g_ref, t_ref, valid_ref, dh_ref, loss_ref, dg_ref):
        @pl.when(pl.program_id(0) == 0)
        def _():
            loss_ref[...] = jnp.zeros_like(loss_ref)
            dg_ref[...] = jnp.zeros_like(dg_ref)

        gain = g_ref[...]
        y, xh, r = _rms_fwd(h_ref[...], gain)
        err = (y - t_ref[...]) * valid_ref[...]
        per_row = jnp.sum(err * err, axis=-1, keepdims=True) / d
        loss_ref[...] += 0.5 * jnp.sum(per_row, axis=0, keepdims=True)
        dx, dg_rows = _rms_bwd(err / d, xh, r, gain)
        dh_ref[...] = dx
        dg_ref[...] += jnp.sum(dg_rows, axis=0, keepdims=True)

    return pl.pallas_call(
        body, name="loss_head", grid=(t // tm,),
        in_specs=[_row_block(tm, d), _vmem_spec(), _row_block(tm, d), _row_block(tm, 1)],
        out_specs=[_row_block(tm, d), pl.BlockSpec((1, 1), lambda i: (0, 0)), pl.BlockSpec((1, d), lambda i: (0, 0))],
        out_shape=[jax.ShapeDtypeStruct((t, d), F32), jax.ShapeDtypeStruct((1, 1), F32), jax.ShapeDtypeStruct((1, d), F32)],
        compiler_params=_params(("arbitrary",)),
    )(h, g, target, valid)


def _weight_grad(x, y, name):
    t, k = x.shape
    n = y.shape[1]
    tm = _token_tile(t, 8 * MXU_DEPTH)
    tk, tn = k, n
    while tk * tn * 4 > ACC_BYTES and max(tk, tn) % 256 == 0:
        if tk > tn:
            tk //= 2
        else:
            tn //= 2
    steps = t // tm

    def body(x_ref, y_ref, o_ref, acc):
        @pl.when(pl.program_id(2) == 0)
        def _():
            acc[...] = jnp.zeros_like(acc)

        acc[...] += _dot_tn(x_ref[...].astype(BF16), y_ref[...].astype(BF16))

        @pl.when(pl.program_id(2) == steps - 1)
        def _():
            o_ref[...] = acc[...].astype(BF16)

    return pl.pallas_call(
        body, name=name, grid=(k // tk, n // tn, steps),
        in_specs=[pl.BlockSpec((tm, tk), lambda a, b, i: (i, a)), pl.BlockSpec((tm, tn), lambda a, b, i: (i, b))],
        out_specs=pl.BlockSpec((tk, tn), lambda a, b, i: (a, b)),
        out_shape=jax.ShapeDtypeStruct((k, n), BF16),
        scratch_shapes=[pltpu.VMEM((tk, tn), F32)],
        compiler_params=_params(("parallel", "parallel", "arbitrary")),
    )(x, y)


def _ffn_bwd(h, dh2, a, b, g, w_gate_t, w_up_t, w_down):
    t, d = h.shape
    f = a.shape[1]
    tm = _token_tile(t, 256)

    def body(h_ref, dh2_ref, a_ref, b_ref, g_ref, wg_ref, wu_ref, wd_ref, dh_ref, hn_ref, act_ref, da_ref, db_ref, dg_ref):
        @pl.when(pl.program_id(0) == 0)
        def _():
            dg_ref[...] = jnp.zeros_like(dg_ref)

        gain = g_ref[...]
        hn, xh, r = _rms_fwd(h_ref[...], gain)
        hn_ref[...] = hn.astype(BF16)
        dh2 = dh2_ref[...]
        dact = _dot_nt(dh2.astype(BF16), wd_ref[...])
        ga = a_ref[...].astype(F32)
        up = b_ref[...].astype(F32)
        sg = jax.nn.sigmoid(ga)
        silu = ga * sg
        act_ref[...] = (silu * up).astype(BF16)
        da = (dact * up * (sg * (1.0 + ga * (1.0 - sg)))).astype(BF16)
        db = (dact * silu).astype(BF16)
        da_ref[...] = da
        db_ref[...] = db
        dhn = _dot(da, wg_ref[...]) + _dot(db, wu_ref[...])
        dx, dg_rows = _rms_bwd(dhn, xh, r, gain)
        dh_ref[...] = dh2 + dx
        dg_ref[...] += jnp.sum(dg_rows, axis=0, keepdims=True)

    return pl.pallas_call(
        body, name="ffn_bwd", grid=(t // tm,),
        in_specs=[_row_block(tm, d), _row_block(tm, d), _row_block(tm, f), _row_block(tm, f),
                  _vmem_spec(), _vmem_spec(), _vmem_spec(), _vmem_spec()],
        out_specs=[_row_block(tm, d), _row_block(tm, d), _row_block(tm, f), _row_block(tm, f), _row_block(tm, f),
                   pl.BlockSpec((1, d), lambda i: (0, 0))],
        out_shape=[jax.ShapeDtypeStruct((t, d), F32), jax.ShapeDtypeStruct((t, d), BF16), jax.ShapeDtypeStruct((t, f), BF16),
                   jax.ShapeDtypeStruct((t, f), BF16), jax.ShapeDtypeStruct((t, f), BF16), jax.ShapeDtypeStruct((1, d), F32)],
        compiler_params=_params(("arbitrary",)),
    )(h, dh2, a, b, g, w_gate_t, w_up_t, w_down)


def _merge_bwd(dh1, z, pa, pb, w_o, w_pa, w_pb_p):
    t, d = dh1.shape
    hw = w_pb_p.shape[0]
    tm = _token_tile(t, 512)
    gate_block = Z_FIXED // d

    def body(dh_ref, ga_ref, gb_ref, pa_ref, pb_ref, wo_ref, wpa_ref, wpb_ref,
             mg_ref, dpa_ref, dpb_ref, dga_ref, dgb_ref, da_ref, do_ref):
        dm = _dot_nt(dh_ref[...].astype(BF16), wo_ref[...])
        sa = jax.nn.sigmoid(ga_ref[...].astype(F32))
        sb = jax.nn.sigmoid(gb_ref[...].astype(F32))
        pa = pa_ref[...].astype(F32)
        pb = pb_ref[...].astype(F32)
        mg_ref[...] = (sa * pa + sb * pb).astype(BF16)
        dpa = (dm * sa).astype(BF16)
        dpb = (dm * sb).astype(BF16)
        dpa_ref[...] = dpa
        dpb_ref[...] = dpb
        dga_ref[...] = (dm * pa * (sa * (1.0 - sa))).astype(BF16)
        dgb_ref[...] = (dm * pb * (sb * (1.0 - sb))).astype(BF16)
        da_ref[...] = _dot_nt(dpa, wpa_ref[...]).astype(BF16)
        do_ref[...] = _dot_nt(dpb, wpb_ref[...]).astype(BF16)

    wide = jax.ShapeDtypeStruct((t, d), BF16)
    return pl.pallas_call(
        body, name="merge_bwd", grid=(t // tm,),
        in_specs=[_row_block(tm, d), _row_block(tm, d, gate_block), _row_block(tm, d, gate_block + 1),
                  _row_block(tm, d), _row_block(tm, d), _vmem_spec(), _vmem_spec(), _vmem_spec()],
        out_specs=[_row_block(tm, d)] * 5 + [_row_block(tm, POOL_WIDTH), _row_block(tm, hw)],
        out_shape=[wide] * 5 + [jax.ShapeDtypeStruct((t, POOL_WIDTH), BF16), jax.ShapeDtypeStruct((t, hw), BF16)],
        compiler_params=_params(("parallel",)),
    )(dh1, z, z, pa, pb, w_o, w_pa, w_pb_p)


def _attn_bwd(q3, k3, v3, o3, do3, lse3, real_len, exchange=None):
    b, lp, hw = q3.shape
    heads = hw // HEAD_PAD
    tb = ATT_BLOCK
    nfull, tail_start, tail = _attn_blocks(real_len)
    done = tail_start + tail
    hpg = _heads_per_step(heads, ATT_BWD_HEADS)
    width = hpg * HEAD_PAD

    def body(q_ref, k_ref, v_ref, o_ref, do_ref, lse_ref, dq_ref, dk_ref, dv_ref, dqt_acc, lse_row, delta_row):
        group = pl.program_id(1)
        lse_t = jnp.transpose(lse_ref[0])
        head_of_row = lax.broadcasted_iota(jnp.int32, (HEAD_PAD, lp), 0)
        for hd in range(hpg):
            cols = slice(hd * HEAD_PAD, (hd + 1) * HEAD_PAD)
            lse_row[hd] = jnp.sum(jnp.where(head_of_row == group * hpg + hd, lse_t, 0.0), axis=0, keepdims=True)
            prod = do_ref[0, :, cols].astype(F32) * o_ref[0, :, cols].astype(F32)
            delta_row[hd] = jnp.sum(jnp.transpose(prod), axis=0, keepdims=True)
        dqt_acc[...] = jnp.zeros_like(dqt_acc)

        def kv_rows(c0, keys, whole_q_blocks_from):
            k_t = [jnp.transpose(k_ref[0, pl.ds(c0, keys), hd * HEAD_PAD:(hd + 1) * HEAD_PAD].astype(F32)).astype(BF16)
                   for hd in range(hpg)]

            def q_step(r0, rows, states, mask):
                out = []
                for hd, (dk, dv) in enumerate(states):
                    cols = slice(hd * HEAD_PAD, (hd + 1) * HEAD_PAD)
                    q = q_ref[0, pl.ds(r0, rows), cols]
                    do = do_ref[0, pl.ds(r0, rows), cols]
                    s_t = _dot_nt(k_ref[0, pl.ds(c0, keys), cols], q)
                    if mask is not None:
                        s_t = jnp.where(mask, s_t, MASK_VALUE)
                    p_t = jnp.exp(s_t - lse_row[hd, :, pl.ds(r0, rows)])
                    dp_t = _dot_nt(v_ref[0, pl.ds(c0, keys), cols], do)
                    ds_t = (p_t * (dp_t - delta_row[hd, :, pl.ds(r0, rows)])).astype(BF16)
                    dv = dv + _dot(p_t.astype(BF16), do)
                    dk = dk + _dot(ds_t, q)
                    dqt_acc[cols, pl.ds(r0, rows)] += _dot(k_t[hd], ds_t)
                    out.append((dk, dv))
                return tuple(out)

            zero = jnp.zeros((keys, HEAD_PAD), F32)
            key_pos = lax.broadcasted_iota(jnp.int32, (keys, keys), 0)
            query_pos = lax.broadcasted_iota(jnp.int32, (keys, keys), 1)
            states = q_step(c0, keys, tuple((zero, zero) for _ in range(hpg)), key_pos <= query_pos)
            if whole_q_blocks_from is not None:
                states = lax.fori_loop(whole_q_blocks_from, nfull,
                                       lambda i, st: q_step(pl.multiple_of(i * tb, tb), tb, st, None), states)
                states = q_step(tail_start, tail, states, None)
            for hd, (dk, dv) in enumerate(states):
                cols = slice(hd * HEAD_PAD, (hd + 1) * HEAD_PAD)
                dk_ref[0, pl.ds(c0, keys), cols] = dk.astype(BF16)
                dv_ref[0, pl.ds(c0, keys), cols] = dv.astype(BF16)

        def whole_block(j, carry):
            kv_rows(pl.multiple_of(j * tb, tb), tb, j + 1)
            return carry

        lax.fori_loop(0, nfull, whole_block, 0)
        kv_rows(tail_start, tail, None)
        if done < lp:
            dk_ref[0, done:lp, :] = jnp.zeros((lp - done, width), BF16)
            dv_ref[0, done:lp, :] = jnp.zeros((lp - done, width), BF16)
        for hd in range(hpg):
            cols = slice(hd * HEAD_PAD, (hd + 1) * HEAD_PAD)
            dq_ref[0, :, cols] = jnp.transpose(dqt_acc[cols, :]).astype(BF16)

    head_spec = pl.BlockSpec((1, lp, width), lambda bi, hi: (bi, 0, hi))
    out = jax.ShapeDtypeStruct((b, lp, hw), BF16)
    return _call_with_exchange(
        body, exchange, name="attn_bwd", grid=(b, heads // hpg),
        in_specs=[head_spec] * 5 + [pl.BlockSpec((1, lp, HEAD_PAD), lambda bi, hi: (bi, 0, 0))],
        out_specs=[head_spec] * 3,
        out_shape=[out, out, out],
        scratch_shapes=[pltpu.VMEM((width, lp), F32), pltpu.VMEM((hpg, 1, lp), F32), pltpu.VMEM((hpg, 1, lp), F32)],
        operands=(q3, k3, v3, o3, do3, lse3))


def _qkv_bwd(dq, dk, dv, z, g_q, g_kv, w_uq_p, w_kv_p, rope_c, rope_sa, rope_sb):
    t, hw = dq.shape
    heads = hw // HEAD_PAD
    tm = _token_tile(t, 512)

    def body(dq_ref, dk_ref, dv_ref, cq_ref, ckv_ref, gq_ref, gkv_ref, wq_ref, wkv_ref, c_ref, sa_ref, sb_ref,
             dqraw_ref, dkvraw_ref, cqn_ref, ckvn_ref, dcq_ref, dckv_ref, dkr_ref, dgq_ref, dgkv_ref):
        @pl.when(pl.program_id(0) == 0)
        def _():
            dgq_ref[...] = jnp.zeros_like(dgq_ref)
            dgkv_ref[...] = jnp.zeros_like(dgkv_ref)

        c, sa, sb = c_ref[...], sa_ref[...], sb_ref[...]
        dkr = jnp.zeros((tm, HEAD_PAD), F32)
        for hd in range(heads):
            cols = slice(hd * HEAD_PAD, (hd + 1) * HEAD_PAD)
            dqraw_ref[:, cols] = _rope_bwd(dq_ref[:, cols].astype(F32) * SM_SCALE, c, sa, sb).astype(BF16)
            dkvraw_ref[:, cols] = dk_ref[:, cols]
            dkr = dkr + dk_ref[:, cols].astype(F32)
        dkvraw_ref[:, hw:] = dv_ref[...]
        lane = lax.broadcasted_iota(jnp.int32, (tm, HEAD_PAD), 1)
        dkr_ref[...] = jnp.where((lane >= QK_NOPE) & (lane < QK_DIM), _rope_bwd(dkr, c, sa, sb), 0.0).astype(BF16)

        gq = gq_ref[...]
        cqn, xh, r = _rms_fwd(cq_ref[...].astype(F32), gq)
        cqn_ref[...] = cqn.astype(BF16)
        dx, dg_rows = _rms_bwd(_dot_nt(dqraw_ref[...], wq_ref[...]), xh, r, gq)
        dcq_ref[...] = dx.astype(BF16)
        dgq_ref[...] += jnp.sum(dg_rows, axis=0, keepdims=True)

        gkv = gkv_ref[...]
        ckvn, xh, r = _rms_fwd(ckv_ref[...].astype(F32), gkv)
        ckvn_ref[...] = ckvn.astype(BF16)
        dx, dg_rows = _rms_bwd(_dot_nt(dkvraw_ref[...], wkv_ref[...]), xh, r, gkv)
        dckv_ref[...] = dx.astype(BF16)
        dgkv_ref[...] += jnp.sum(dg_rows, axis=0, keepdims=True)

    def shape(width, dtype=BF16):
        return jax.ShapeDtypeStruct((t, width), dtype)

    return pl.pallas_call(
        body, name="qkv_bwd", grid=(t // tm,),
        in_specs=[_row_block(tm, hw)] * 3
        + [_row_block(tm, Q_RANK, POOL_WIDTH // Q_RANK), _row_block(tm, KV_RANK, (POOL_WIDTH + Q_RANK) // KV_RANK)]
        + [_vmem_spec()] * 4 + [_row_block(tm, HEAD_PAD)] * 3,
        out_specs=[_row_block(tm, hw), _row_block(tm, 2 * hw), _row_block(tm, Q_RANK), _row_block(tm, KV_RANK),
                   _row_block(tm, Q_RANK), _row_block(tm, KV_RANK), _row_block(tm, HEAD_PAD),
                   pl.BlockSpec((1, Q_RANK), lambda i: (0, 0)), pl.BlockSpec((1, KV_RANK), lambda i: (0, 0))],
        out_shape=[shape(hw), shape(2 * hw), shape(Q_RANK), shape(KV_RANK), shape(Q_RANK), shape(KV_RANK), shape(HEAD_PAD),
                   jax.ShapeDtypeStruct((1, Q_RANK), F32), jax.ShapeDtypeStruct((1, KV_RANK), F32)],
        compiler_params=_params(("arbitrary",)),
    )(dq, dk, dv, z, z, g_q, g_kv, w_uq_p, w_kv_p, rope_c, rope_sa, rope_sb)


def _pool_bwd(z3, da3, pool_w, pool_scale):
    b, lp, _ = z3.shape
    groups = len(POOL_WINDOWS)

    def body(u_ref, da_ref, pw_ref, sc_ref, du_ref, dpw_ref, dsc_ref):
        @pl.when(pl.program_id(0) == 0)
        def _():
            dpw_ref[...] = jnp.zeros_like(dpw_ref)
            dsc_ref[...] = jnp.zeros_like(dsc_ref)

        row = lax.broadcasted_iota(jnp.int32, (lp, POOL_GROUP), 0)
        pos = row.astype(F32)
        for gi, w in enumerate(POOL_WINDOWS):
            cols = slice(gi * POOL_GROUP, (gi + 1) * POOL_GROUP)
            count = jnp.minimum(pos + 1.0, float(w))
            u = u_ref[0, :, cols].astype(F32)
            y = (_window_sum(u, w, row, True) / count - u).astype(BF16)
            yw = _dot(y, pw_ref[gi])
            da = da_ref[0, :, cols].astype(F32)
            dsc_ref[:, cols] += jnp.sum(da * yw, axis=0, keepdims=True)
            dyw = (da * sc_ref[:, cols]).astype(BF16)
            dpw_ref[gi] += _dot_tn(y, dyw)
            dy = _dot_nt(dyw, pw_ref[gi])
            du_ref[0, :, cols] = (_window_sum(dy / count, w, row, False) - dy).astype(BF16)

    return pl.pallas_call(
        body, name="pool_bwd", grid=(b,),
        in_specs=[pl.BlockSpec((1, lp, POOL_WIDTH), lambda i: (i, 0, 0)), pl.BlockSpec((1, lp, POOL_WIDTH), lambda i: (i, 0, 0)),
                  _vmem_spec(), _vmem_spec()],
        out_specs=[pl.BlockSpec((1, lp, POOL_WIDTH), lambda i: (i, 0, 0)),
                   pl.BlockSpec((groups, POOL_GROUP, POOL_GROUP), lambda i: (0, 0, 0)),
                   pl.BlockSpec((1, POOL_WIDTH), lambda i: (0, 0))],
        out_shape=[jax.ShapeDtypeStruct((b, lp, POOL_WIDTH), BF16), jax.ShapeDtypeStruct((groups, POOL_GROUP, POOL_GROUP), F32),
                   jax.ShapeDtypeStruct((1, POOL_WIDTH), F32)],
        compiler_params=_params(("arbitrary",)),
    )(z3, da3, pool_w, pool_scale)


def _in_proj_bwd(h, dh1, du, dcq, dckv, dkr, dga, dgb, g, w_in_pt):
    t, d = h.shape
    nz = w_in_pt.shape[0]
    tm = _token_tile(t, 512)
    widths = (POOL_WIDTH, Q_RANK, KV_RANK, HEAD_PAD, d, d)

    def body(h_ref, dh1_ref, du_ref, dcq_ref, dckv_ref, dkr_ref, dga_ref, dgb_ref, g_ref, w_ref, dh_ref, hn_ref, dz_ref, dg_ref):
        @pl.when(pl.program_id(0) == 0)
        def _():
            dg_ref[...] = jnp.zeros_like(dg_ref)

        gain = g_ref[...]
        hn, xh, r = _rms_fwd(h_ref[...], gain)
        hn_ref[...] = hn.astype(BF16)
        dhn = jnp.zeros((tm, d), F32)
        start = 0
        for piece, width in zip((du_ref, dcq_ref, dckv_ref, dkr_ref, dga_ref, dgb_ref), widths):
            val = piece[...]
            dz_ref[:, start:start + width] = val
            dhn = dhn + _dot(val, w_ref[start:start + width, :])
            start += width
        dx, dg_rows = _rms_bwd(dhn, xh, r, gain)
        dh_ref[...] = dh1_ref[...] + dx
        dg_ref[...] += jnp.sum(dg_rows, axis=0, keepdims=True)

    return pl.pallas_call(
        body, name="in_proj_bwd", grid=(t // tm,),
        in_specs=[_row_block(tm, d), _row_block(tm, d)] + [_row_block(tm, w) for w in widths] + [_vmem_spec(), _vmem_spec()],
        out_specs=[_row_block(tm, d), _row_block(tm, d), _row_block(tm, nz), pl.BlockSpec((1, d), lambda i: (0, 0))],
        out_shape=[jax.ShapeDtypeStruct((t, d), F32), jax.ShapeDtypeStruct((t, d), BF16), jax.ShapeDtypeStruct((t, nz), BF16),
                   jax.ShapeDtypeStruct((1, d), F32)],
        compiler_params=_params(("arbitrary",)),
    )(h, dh1, du, dcq, dckv, dkr, dga, dgb, g, w_in_pt)


def _pad_heads(w, heads, width):
    k = w.shape[0]
    w = w.reshape(k, heads, width)
    return jnp.pad(w, ((0, 0), (0, 0), (0, HEAD_PAD - width))).reshape(k, heads * HEAD_PAD)


def _unpad_heads(w, heads, width):
    k = w.shape[0]
    return w.reshape(k, heads, HEAD_PAD)[:, :, :width].reshape(k, heads * width)


def _early_layouts(w, heads):
    o3, o4 = POOL_WIDTH + Q_RANK + KV_RANK, POOL_WIDTH + Q_RANK + KV_RANK + QK_ROPE
    w_in_t = w["w_in"]
    rope_rows = jnp.pad(w_in_t[o3:o4], ((QK_NOPE, HEAD_PAD - QK_DIM), (0, 0)))
    w_in_pt = jnp.concatenate([w_in_t[:o3], rope_rows, w_in_t[o4:]], axis=0)
    w_uq_p = _pad_heads(w["w_uq"], heads, QK_DIM)
    kv = w["w_ukv"].reshape(KV_RANK, heads, QK_NOPE + V_DIM)
    w_k = jnp.pad(kv[:, :, :QK_NOPE], ((0, 0), (0, 0), (0, HEAD_PAD - QK_NOPE))).reshape(KV_RANK, heads * HEAD_PAD)
    w_v = jnp.pad(kv[:, :, QK_NOPE:], ((0, 0), (0, 0), (0, HEAD_PAD - V_DIM))).reshape(KV_RANK, heads * HEAD_PAD)
    return dict(w_in_pt=w_in_pt, w_uq_p=w_uq_p, w_kv_p=jnp.concatenate([w_k, w_v], axis=1))


def _late_layouts(w, heads):
    d = w["w_pb"].shape[1]
    w_pb_p = jnp.pad(w["w_pb"].reshape(heads, V_DIM, d), ((0, 0), (0, HEAD_PAD - V_DIM), (0, 0))).reshape(heads * HEAD_PAD, d)
    return dict(w_pa=w["w_pa"], w_pb_p=w_pb_p, w_o=w["w_o"], w_gate_t=w["w_gate"], w_up_t=w["w_up"], w_down=w["w_down"])


def _early_grad_layouts(g, heads):
    o3 = POOL_WIDTH + Q_RANK + KV_RANK
    gin = g["w_in_pt"]
    w_in = jnp.concatenate([gin[:o3], gin[o3 + QK_NOPE:o3 + QK_DIM], gin[o3 + HEAD_PAD:]], axis=0)
    hw = heads * HEAD_PAD
    gk = g["w_kv_p"][:, :hw].reshape(KV_RANK, heads, HEAD_PAD)[:, :, :QK_NOPE]
    gv = g["w_kv_p"][:, hw:].reshape(KV_RANK, heads, HEAD_PAD)[:, :, :V_DIM]
    w_ukv = jnp.concatenate([gk, gv], axis=2).reshape(KV_RANK, heads * (QK_NOPE + V_DIM))
    return dict(w_in=w_in, w_uq=_unpad_heads(g["w_uq_p"], heads, QK_DIM), w_ukv=w_ukv)


def _late_grad_layouts(g, heads):
    d = g["w_pb_p"].shape[1]
    w_pb = g["w_pb_p"].reshape(heads, HEAD_PAD, d)[:, :V_DIM].reshape(heads * V_DIM, d)
    return dict(w_pa=g["w_pa"], w_pb=w_pb, w_o=g["w_o"], w_gate=g["w_gate_t"], w_up=g["w_up_t"], w_down=g["w_down"])


def _rope_tables(lp, b):
    inv = 1.0 / (ROPE_THETA ** (jnp.arange(0, QK_ROPE, 2, dtype=F32) / QK_ROPE))
    ang = jnp.arange(lp, dtype=F32)[:, None] * inv[None, :]
    cos, sin = jnp.cos(ang), jnp.sin(ang)
    half = QK_ROPE // 2
    ones = jnp.ones((lp, QK_NOPE), F32)
    zeros_lo = jnp.zeros((lp, QK_NOPE), F32)
    zeros_hi = jnp.zeros((lp, HEAD_PAD - QK_DIM), F32)
    zeros_half = jnp.zeros((lp, half), F32)
    c = jnp.concatenate([ones, cos, cos, zeros_hi], axis=1)
    sa = jnp.concatenate([zeros_lo, zeros_half, sin, zeros_hi], axis=1)
    sb = jnp.concatenate([zeros_lo, -sin, zeros_half, zeros_hi], axis=1)
    return tuple(jnp.tile(tab, (b, 1)) for tab in (c, sa, sb))


def _local_step(x, loss_target, meta_tokens, small, early_first, riding, early_first_codec):
    b, seq, d = x.shape
    depth = 2
    heads = early_first["w_uq"].shape[1] // QK_DIM
    core = lax.axis_index("c")
    chip = 2 * lax.axis_index("x") + lax.axis_index("y")
    real_len = N_META + seq
    lp = _round_up(real_len, SEQ_PAD)
    t = b * lp
    pad = lp - N_META - seq

    meta = jnp.broadcast_to(meta_tokens[None], (b, N_META, d))
    h = jnp.concatenate([meta, x, jnp.zeros((b, pad, d), F32)], axis=1).reshape(t, d)
    target = jnp.pad(loss_target, ((0, 0), (N_META, pad), (0, 0))).reshape(t, d)
    pos = jnp.arange(lp)
    valid = jnp.tile(((pos >= N_META) & (pos < N_META + seq)).astype(F32), b).reshape(t, 1)
    rope_c, rope_sa, rope_sb = _rope_tables(lp, b)

    layers = []
    for li in range(depth):
        lay = dict(pool_w=small["pool_w"][li].astype(BF16), pool_scale=small["pool_scale"][li][None])
        for n in ("norm_mix_g", "q_norm_g", "kv_norm_g", "norm_ffn_g"):
            lay[n] = small[n][li][None]
        layers.append(lay)
    layers[0].update(_early_layouts(early_first, heads))

    saved = []
    for li in range(depth):
        lay = layers[li]
        z = _in_proj_fwd(h, lay["norm_mix_g"], lay["w_in_pt"])
        a = _pool_fwd(z.reshape(b, lp, -1), lay["pool_w"], lay["pool_scale"]).reshape(t, POOL_WIDTH)
        q, k, v = _qkv_fwd(z, lay["q_norm_g"], lay["kv_norm_g"], lay["w_uq_p"], lay["w_kv_p"], rope_c, rope_sa, rope_sb)
        hw = q.shape[1]
        halves = riding[li]["packed"].reshape(2, -1, PACK_COLS)
        o3, lse, others = _attn_fwd(q.reshape(b, lp, hw), k.reshape(b, lp, hw), v.reshape(b, lp, hw), real_len,
                                    _gather_exchange(halves))
        arrived = riding[li]["unpack"](
            lax.dynamic_update_index_in_dim(others, halves, chip, 0).reshape(N_CHIPS, -1, PACK_COLS))
        lay.update(_late_layouts({n: arrived[n, li] for n in LATE_WEIGHTS}, heads))
        if li + 1 < depth:
            layers[li + 1].update(_early_layouts({n: arrived[n, li + 1] for n in EARLY_WEIGHTS}, heads))
        o = o3.reshape(t, hw)
        h1, pa, pb = _merge_fwd(h, z, a, o, lay["w_pa"], lay["w_pb_p"], lay["w_o"])
        h2, fa, fb = _ffn_fwd(h1, lay["norm_ffn_g"], lay["w_gate_t"], lay["w_up_t"], lay["w_down"])
        saved.append(dict(h=h, z=z, a=a, q=q, k=k, v=v, o=o, lse=lse, pa=pa, pb=pb, h1=h1, fa=fa, fb=fb))
        h = h2

    dh, loss, d_final = _loss_head(h, small["final_norm_g"][None], target, valid)

    g_small = {n: [] for n in SMALL_WEIGHTS if n != "final_norm_g"}
    early_grads, parts = {}, [None] * depth
    for li in reversed(range(depth)):
        lay, sv = layers[li], saved[li]
        hw = sv["q"].shape[1]
        dh1, hn_f, act, dfa, dfb, dg_ffn = _ffn_bwd(sv["h1"], dh, sv["fa"], sv["fb"], lay["norm_ffn_g"],
                                                     lay["w_gate_t"], lay["w_up_t"], lay["w_down"])
        gl = dict(w_gate_t=_weight_grad(dfa, hn_f, "grad_w_gate"), w_up_t=_weight_grad(dfb, hn_f, "grad_w_up"),
                  w_down=_weight_grad(act, dh, "grad_w_down"))
        merged, dpa, dpb, dga, dgb, da, do = _merge_bwd(dh1, sv["z"], sv["pa"], sv["pb"], lay["w_o"], lay["w_pa"], lay["w_pb_p"])
        gl["w_o"] = _weight_grad(merged, dh1, "grad_w_o")
        gl["w_pa"] = _weight_grad(sv["a"], dpa, "grad_w_pa")
        gl["w_pb_p"] = _weight_grad(sv["o"], dpb, "grad_w_pb")
        to_send = {(n, li): g for n, g in _late_grad_layouts(gl, heads).items()}
        if li + 1 < depth:
            to_send.update({(n, li + 1): g for n, g in early_grads[li + 1].items()})
        sending = riding[li]["pack_grads"](to_send)
        shape3 = (b, lp, hw)
        dq3, dk3, dv3, from_others = _attn_bwd(
            sv["q"].reshape(shape3), sv["k"].reshape(shape3), sv["v"].reshape(shape3), sv["o"].reshape(shape3),
            do.reshape(shape3), sv["lse"], real_len, _scatter_exchange(sending))
        own = lax.dynamic_index_in_dim(lax.dynamic_index_in_dim(sending, chip, 0, keepdims=False), core, 0, keepdims=False)
        parts[li] = lax.dynamic_update_index_in_dim(from_others, own, 2 * chip + core, 0)
        dqraw, dkvraw, cqn, ckvn, dcq, dckv, dkr, dg_q, dg_kv = _qkv_bwd(
            dq3.reshape(t, hw), dk3.reshape(t, hw), dv3.reshape(t, hw), sv["z"], lay["q_norm_g"], lay["kv_norm_g"],
            lay["w_uq_p"], lay["w_kv_p"], rope_c, rope_sa, rope_sb)
        gl["w_uq_p"] = _weight_grad(cqn, dqraw, "grad_w_uq")
        gl["w_kv_p"] = _weight_grad(ckvn, dkvraw, "grad_w_ukv")
        du3, dpool_w, dpool_scale = _pool_bwd(sv["z"].reshape(b, lp, -1), da.reshape(b, lp, POOL_WIDTH),
                                              lay["pool_w"], lay["pool_scale"])
        dh, hn_m, dz, dg_mix = _in_proj_bwd(sv["h"], dh1, du3.reshape(t, POOL_WIDTH), dcq, dckv, dkr, dga, dgb,
                                            lay["norm_mix_g"], lay["w_in_pt"])
        gl["w_in_pt"] = _weight_grad(dz, hn_m, "grad_w_in")
        early_grads[li] = _early_grad_layouts(gl, heads)
        for n, val in (("norm_mix_g", dg_mix[0]), ("pool_w", dpool_w), ("pool_scale", dpool_scale[0]), ("q_norm_g", dg_q[0]),
                       ("kv_norm_g", dg_kv[0]), ("norm_ffn_g", dg_ffn[0])):
            g_small[n].insert(0, val)

    dh3 = dh.reshape(b, lp, d)
    grad_x = dh3[:, N_META:N_META + seq]
    d_meta_rows = dh3[:, :N_META]
    g_small = {n: jnp.stack(v) for n, v in g_small.items()}
    g_small["final_norm_g"] = d_final[0]
    early_first_partial = early_first_codec["pack_grads"]({(n, 0): g for n, g in early_grads[0].items()})
    return loss, grad_x, d_meta_rows, g_small, early_first_partial, parts


def _mesh_place():
    x, y, c = lax.axis_index("x"), lax.axis_index("y"), lax.axis_index("c")
    others = [(1 - x, y), (x, 1 - y), (1 - x, 1 - y)]
    return x, y, c, 2 * x + y, others


def _remote(src, dst, send_sems, recv_sems, k, device):
    return pltpu.make_async_remote_copy(src_ref=src, dst_ref=dst, send_sem=send_sems.at[k], recv_sem=recv_sems.at[k],
                                        device_id=device, device_id_type=MESH)


def _gather_exchange(packed):
    def over_links(p_ref, g_ref, send_sems, recv_sems, r, slot):
        _, _, c, _, others = _mesh_place()
        ox, oy = others[r]
        return _remote(p_ref.at[c], g_ref.at[slot, c], send_sems, recv_sems, r, (ox, oy, c))

    def to_sibling(g_ref, send_sems, recv_sems, r, half):
        x, y, c, _, others = _mesh_place()
        ox, oy = others[r]
        block = g_ref.at[2 * ox + oy, c if half is None else half]
        return _remote(block, block, send_sems, recv_sems, 3 + r, (x, y, 1 - c))

    def start(p_ref, g_ref, send_sems, recv_sems):
        chip = _mesh_place()[3]
        for r in range(3):
            over_links(p_ref, g_ref, send_sems, recv_sems, r, chip).start()

    def pass_on(p_ref, g_ref, send_sems, recv_sems):
        others = _mesh_place()[4]
        for r, (ox, oy) in enumerate(others):
            over_links(p_ref, g_ref, send_sems, recv_sems, r, 2 * ox + oy).wait_recv()
            to_sibling(g_ref, send_sems, recv_sems, r, None).start()

    def finish(p_ref, g_ref, send_sems, recv_sems):
        _, _, c, chip, _ = _mesh_place()
        for r in range(3):
            to_sibling(g_ref, send_sems, recv_sems, r, 1 - c).wait_recv()
        for r in range(3):
            over_links(p_ref, g_ref, send_sems, recv_sems, r, chip).wait_send()
            to_sibling(g_ref, send_sems, recv_sems, r, None).wait_send()

    return dict(name="gather", operand=packed, copies=6, start=start, pass_on=pass_on, finish=finish,
                out_shape=jax.ShapeDtypeStruct((N_CHIPS,) + packed.shape, packed.dtype))


def _scatter_exchange(parts):
    flips = [(dx, dy, dc) for dx in (0, 1) for dy in (0, 1) for dc in (0, 1)][1:]

    def copy(p_ref, got_ref, send_sems, recv_sems, k, arriving):
        x, y, c, _, _ = _mesh_place()
        dx, dy, dc = flips[k]
        tx, ty, tc = (1 - x if dx else x), (1 - y if dy else y), (1 - c if dc else c)
        slot = 4 * tx + 2 * ty + tc if arriving else 4 * x + 2 * y + c
        return _remote(p_ref.at[2 * tx + ty, tc], got_ref.at[slot], send_sems, recv_sems, k, (tx, ty, tc))

    def start(p_ref, got_ref, send_sems, recv_sems):
        for k in range(len(flips)):
            copy(p_ref, got_ref, send_sems, recv_sems, k, False).start()

    def finish(p_ref, got_ref, send_sems, recv_sems):
        for k in range(len(flips)):
            copy(p_ref, got_ref, send_sems, recv_sems, k, True).wait_recv()
        for k in range(len(flips)):
            copy(p_ref, got_ref, send_sems, recv_sems, k, False).wait_send()

    return dict(name="scatter", operand=parts, copies=len(flips), start=start, finish=finish,
                out_shape=jax.ShapeDtypeStruct((2 * N_CHIPS,) + parts.shape[2:], parts.dtype))


def _all_gather_shards(packed, meta_shard):
    _, rh, cols = packed.shape

    def body(p_ref, m_ref, g_ref, gm_ref, send_sems, recv_sems):
        x, y, c, chip, others = _mesh_place()
        sibling = (x, y, 1 - c)
        sends = []
        for r, (ox, oy) in enumerate(others):
            sends.append(_remote(p_ref.at[c], g_ref.at[chip, c], send_sems, recv_sems, r, (ox, oy, c)))
            sends.append(_remote(m_ref, gm_ref.at[chip], send_sems, recv_sems, 6 + r, (ox, oy, c)))
        for cp in sends:
            cp.start()
        for r, (ox, oy) in enumerate(others):
            src_chip = 2 * ox + oy
            _remote(p_ref.at[c], g_ref.at[src_chip, c], send_sems, recv_sems, r, (ox, oy, c)).wait_recv()
            passed = _remote(g_ref.at[src_chip, c], g_ref.at[src_chip, c], send_sems, recv_sems, 3 + r, sibling)
            passed.start()
            sends.append(passed)
        for r, (ox, oy) in enumerate(others):
            src_chip = 2 * ox + oy
            _remote(p_ref.at[c], g_ref.at[src_chip, 1 - c], send_sems, recv_sems, 3 + r, sibling).wait_recv()
            _remote(m_ref, gm_ref.at[src_chip], send_sems, recv_sems, 6 + r, (ox, oy, c)).wait_recv()
        for cp in sends:
            cp.wait_send()

    gathered, meta_all = pl.pallas_call(
        body, name="all_gather_shards",
        in_specs=[_any_spec(), _any_spec()], out_specs=[_any_spec(), _any_spec()],
        out_shape=[jax.ShapeDtypeStruct((N_CHIPS, 2, rh, cols), packed.dtype),
                   jax.ShapeDtypeStruct((N_CHIPS,) + meta_shard.shape, meta_shard.dtype)],
        scratch_shapes=[pltpu.SemaphoreType.DMA((9,)), pltpu.SemaphoreType.DMA((9,))],
    )(packed, meta_shard)
    chip = 2 * lax.axis_index("x") + lax.axis_index("y")
    return (lax.dynamic_update_index_in_dim(gathered, packed, chip, 0),
            lax.dynamic_update_index_in_dim(meta_all, meta_shard, chip, 0))


def _pair_exchange(give):
    def body(give_ref, got_ref, send_sems, recv_sems):
        x, y, c, _, _ = _mesh_place()
        cp = _remote(give_ref, got_ref, send_sems, recv_sems, 0, (x, y, 1 - c))
        cp.start()
        cp.wait()

    return pl.pallas_call(
        body, name="pair_exchange", in_specs=[_any_spec()], out_specs=_any_spec(),
        out_shape=jax.ShapeDtypeStruct(give.shape, give.dtype),
        scratch_shapes=[pltpu.SemaphoreType.DMA((1,)), pltpu.SemaphoreType.DMA((1,))],
    )(give)


def _chip_exchange(parts):
    def body(p_ref, got_ref, send_sems, recv_sems):
        _, _, c, chip, others = _mesh_place()
        sends = [_remote(p_ref.at[2 * ox + oy], got_ref.at[chip], send_sems, recv_sems, r, (ox, oy, c))
                 for r, (ox, oy) in enumerate(others)]
        for cp in sends:
            cp.start()
        for r, (ox, oy) in enumerate(others):
            _remote(p_ref.at[chip], got_ref.at[2 * ox + oy], send_sems, recv_sems, r, (ox, oy, c)).wait_recv()
        for cp in sends:
            cp.wait_send()

    got = pl.pallas_call(
        body, name="chip_exchange", in_specs=[_any_spec()], out_specs=_any_spec(),
        out_shape=jax.ShapeDtypeStruct(parts.shape, parts.dtype),
        scratch_shapes=[pltpu.SemaphoreType.DMA((3,)), pltpu.SemaphoreType.DMA((3,))],
    )(parts)
    chip = 2 * lax.axis_index("x") + lax.axis_index("y")
    own = lax.dynamic_index_in_dim(parts, chip, 0, keepdims=False)
    return lax.dynamic_update_index_in_dim(got, own, chip, 0)


def _pair_gather(half):
    def body(h_ref, out_ref, send_sems, recv_sems):
        x, y, c, _, _ = _mesh_place()
        cp = _remote(h_ref, out_ref.at[c], send_sems, recv_sems, 0, (x, y, 1 - c))
        cp.start()
        _remote(h_ref, out_ref.at[1 - c], send_sems, recv_sems, 0, (x, y, 1 - c)).wait_recv()
        cp.wait_send()

    both = pl.pallas_call(
        body, name="pair_gather", in_specs=[_any_spec()], out_specs=_any_spec(),
        out_shape=jax.ShapeDtypeStruct((2,) + half.shape, half.dtype),
        scratch_shapes=[pltpu.SemaphoreType.DMA((1,)), pltpu.SemaphoreType.DMA((1,))],
    )(half)
    return lax.dynamic_update_index_in_dim(both, half, lax.axis_index("c"), 0)


def _row_tile(rows, limit=PACK_TILE):
    if rows <= limit:
        return rows
    for tr in range(limit, 7, -8):
        if rows % tr == 0:
            return tr
    return rows


def _pair_add(keep, got):
    n, rh, cols = keep.shape
    tr = _row_tile(rh)

    def body(k_ref, g_ref, o_ref):
        o_ref[...] = (k_ref[...].astype(F32) + g_ref[...].astype(F32)).astype(BF16)

    spec = pl.BlockSpec((1, tr, cols), lambda j, i: (j, i, 0))
    return pl.pallas_call(
        body, name="pair_add", grid=(n, rh // tr), in_specs=[spec, spec], out_specs=spec,
        out_shape=jax.ShapeDtypeStruct(keep.shape, BF16),
        compiler_params=_params(("parallel", "parallel")),
    )(keep, got)


def _chip_sum(parts):
    n, rh, cols = parts.shape
    tr = _row_tile(rh)

    def body(p_ref, o_ref):
        total = p_ref[0].astype(F32)
        for k in range(1, n):
            total = total + p_ref[k].astype(F32)
        o_ref[...] = total

    return pl.pallas_call(
        body, name="chip_sum", grid=(rh // tr,),
        in_specs=[pl.BlockSpec((n, tr, cols), lambda i: (0, i, 0))], out_specs=pl.BlockSpec((tr, cols), lambda i: (i, 0)),
        out_shape=jax.ShapeDtypeStruct((rh, cols), F32),
        compiler_params=_params(("parallel",)),
    )(parts)


def _reduce_scatter(grads, c):
    keep = lax.dynamic_index_in_dim(grads, c, axis=1, keepdims=False)
    give = lax.dynamic_index_in_dim(grads, 1 - c, axis=1, keepdims=False)
    chip_partial = _pair_add(keep, _pair_exchange(give))
    return _pair_gather(_chip_sum(_chip_exchange(chip_partial)))


def _all_reduce_small(meta_rows, small):
    b, rm, cols = meta_rows.shape
    rows = rm + small.shape[0]

    def body(meta_ref, small_ref, out_ref, mine, pair_buf, chip_buf, send_sems, recv_sems):
        x, y, c, chip, others = _mesh_place()
        acc = meta_ref[0]
        for i in range(1, b):
            acc = acc + meta_ref[i]
        mine[0:rm, :] = acc
        mine[rm:rows, :] = small_ref[...]
        pair = _remote(mine, pair_buf, send_sems, recv_sems, 0, (x, y, 1 - c))
        pair.start()
        pair.wait()
        chip_buf[chip] = mine[...] + pair_buf[...]
        sends = [_remote(chip_buf.at[chip], chip_buf.at[chip], send_sems, recv_sems, 1 + r, (ox, oy, c))
                 for r, (ox, oy) in enumerate(others)]
        for cp in sends:
            cp.start()
        for r, (ox, oy) in enumerate(others):
            _remote(chip_buf.at[chip], chip_buf.at[2 * ox + oy], send_sems, recv_sems, 1 + r, (ox, oy, c)).wait_recv()
        for cp in sends:
            cp.wait_send()
        out_ref[...] = ((chip_buf[0] + chip_buf[1]) + chip_buf[2]) + chip_buf[3]

    return pl.pallas_call(
        body, name="all_reduce_small",
        in_specs=[_vmem_spec(), _vmem_spec()], out_specs=_vmem_spec(),
        out_shape=jax.ShapeDtypeStruct((rows, cols), F32),
        scratch_shapes=[pltpu.VMEM((rows, cols), F32), pltpu.VMEM((rows, cols), F32), pltpu.VMEM((N_CHIPS, rows, cols), F32),
                        pltpu.SemaphoreType.DMA((4,)), pltpu.SemaphoreType.DMA((4,))],
        compiler_params=pltpu.CompilerParams(vmem_limit_bytes=VMEM_LIMIT),
    )(meta_rows, small)


def _adamw(w, g, m, v):
    shape = w.shape
    cols = shape[-1]
    rows = w.size // cols
    tr = _row_tile(rows)

    def body(w_ref, g_ref, m_ref, v_ref, d_ref, m2_ref, v2_ref):
        grad = g_ref[...]
        m2 = ADAM_B1 * m_ref[...] + (1.0 - ADAM_B1) * grad
        v2 = ADAM_B2 * v_ref[...] + (1.0 - ADAM_B2) * jnp.square(grad)
        m_hat = m2 / (1.0 - ADAM_B1 ** ADAM_STEP)
        v_hat = v2 / (1.0 - ADAM_B2 ** ADAM_STEP)
        d_ref[...] = -ADAM_LR * (m_hat / (jnp.sqrt(v_hat) + ADAM_EPS) + ADAM_WD * w_ref[...])
        m2_ref[...] = m2
        v2_ref[...] = v2

    spec = pl.BlockSpec((tr, cols), lambda i: (i, 0))
    out = jax.ShapeDtypeStruct((rows, cols), F32)
    res = pl.pallas_call(
        body, name="adamw", grid=(rows // tr,), in_specs=[spec] * 4, out_specs=[spec] * 3, out_shape=[out] * 3,
        compiler_params=_params(("parallel",)),
    )(*(a.reshape(rows, cols) for a in (w, g, m, v)))
    return tuple(r.reshape(shape) for r in res)


def _pack_rows(arrays):
    flat = [a.reshape(-1, PACK_COLS) for a in arrays]
    counts = [f.shape[0] for f in flat]
    total = sum(counts)
    half = -(-total // 2)
    tiles = -(-half // PACK_TILE)
    padded = 2 * tiles * _round_up(-(-half // tiles), 16)
    if padded > total:
        flat.append(jnp.zeros((padded - total, PACK_COLS), flat[0].dtype))
    return jnp.concatenate(flat, axis=0), counts


def _unpack_rows(buffer, counts, shapes):
    out, start = [], 0
    for n, shape in zip(counts, shapes):
        out.append(buffer[..., start:start + n, :].reshape(buffer.shape[:-2] + tuple(shape)))
        start += n
    return out


def _group_codec(entries, weights):
    turned = [n in TRANSPOSED_WEIGHTS for n, _ in entries]
    shapes = [weights[n].shape[1:][::-1] if t else weights[n].shape[1:] for (n, _), t in zip(entries, turned)]
    by_rows = [t or SHARD_AXIS[n] == 1 for (n, _), t in zip(entries, turned)]
    packed, counts = _pack_rows([(weights[n][li].T if t else weights[n][li]).astype(BF16) for (n, li), t in zip(entries, turned)])
    rows = packed.shape[0]
    pad_rows = rows - sum(counts)

    def unpack(per_chip_packed):
        out = {}
        for entry, (s0, s1), rowwise, blk in zip(entries, shapes, by_rows, _unpack_rows(per_chip_packed, counts, shapes)):
            out[entry] = blk.reshape(N_CHIPS * s0, s1) if rowwise else jnp.transpose(blk, (1, 0, 2)).reshape(s0, N_CHIPS * s1)
        return out

    def pack_grads(whole):
        pieces = []
        for entry, (s0, s1), rowwise in zip(entries, shapes, by_rows):
            g = whole[entry]
            by_chip = g.reshape(N_CHIPS, s0, s1) if rowwise else jnp.transpose(g.reshape(s0, N_CHIPS, s1), (1, 0, 2))
            pieces.append(by_chip.reshape(N_CHIPS, -1, PACK_COLS))
        if pad_rows:
            pieces.append(jnp.zeros((N_CHIPS, pad_rows, PACK_COLS), BF16))
        return jnp.concatenate(pieces, axis=1).reshape(N_CHIPS, 2, rows // 2, PACK_COLS)

    def unpack_reduced(reduced):
        shards = _unpack_rows(reduced.reshape(rows, PACK_COLS), counts, shapes)
        return {entry: s.T if t else s for entry, s, t in zip(entries, shards, turned)}

    return dict(packed=packed, unpack=unpack, pack_grads=pack_grads, unpack_reduced=unpack_reduced)


def kernel(x, meta_tokens, norm_mix_g, w_in, pool_w, pool_scale, q_norm_g, kv_norm_g, w_uq, w_ukv, w_pa, w_pb, w_o, norm_ffn_g, w_gate, w_up, w_down, final_norm_g, loss_target, m_meta_tokens, m_norm_mix_g, m_w_in, m_pool_w, m_pool_scale, m_q_norm_g, m_kv_norm_g, m_w_uq, m_w_ukv, m_w_pa, m_w_pb, m_w_o, m_norm_ffn_g, m_w_gate, m_w_up, m_w_down, m_final_norm_g, v_meta_tokens, v_norm_mix_g, v_w_in, v_pool_w, v_pool_scale, v_q_norm_g, v_kv_norm_g, v_w_uq, v_w_ukv, v_w_pa, v_w_pb, v_w_o, v_norm_ffn_g, v_w_gate, v_w_up, v_w_down, v_final_norm_g):
    weights = dict(meta_tokens=meta_tokens, norm_mix_g=norm_mix_g, w_in=w_in, pool_w=pool_w, pool_scale=pool_scale,
                   q_norm_g=q_norm_g, kv_norm_g=kv_norm_g, w_uq=w_uq, w_ukv=w_ukv, w_pa=w_pa, w_pb=w_pb, w_o=w_o,
                   norm_ffn_g=norm_ffn_g, w_gate=w_gate, w_up=w_up, w_down=w_down, final_norm_g=final_norm_g)
    first = dict(meta_tokens=m_meta_tokens, norm_mix_g=m_norm_mix_g, w_in=m_w_in, pool_w=m_pool_w, pool_scale=m_pool_scale,
                 q_norm_g=m_q_norm_g, kv_norm_g=m_kv_norm_g, w_uq=m_w_uq, w_ukv=m_w_ukv, w_pa=m_w_pa, w_pb=m_w_pb, w_o=m_w_o,
                 norm_ffn_g=m_norm_ffn_g, w_gate=m_w_gate, w_up=m_w_up, w_down=m_w_down, final_norm_g=m_final_norm_g)
    second = dict(meta_tokens=v_meta_tokens, norm_mix_g=v_norm_mix_g, w_in=v_w_in, pool_w=v_pool_w, pool_scale=v_pool_scale,
                  q_norm_g=v_q_norm_g, kv_norm_g=v_kv_norm_g, w_uq=v_w_uq, w_ukv=v_w_ukv, w_pa=v_w_pa, w_pb=v_w_pb, w_o=v_w_o,
                  norm_ffn_g=v_norm_ffn_g, w_gate=v_w_gate, w_up=v_w_up, w_down=v_w_down, final_norm_g=v_final_norm_g)
    core = lax.axis_index("c")
    chip = 2 * lax.axis_index("x") + lax.axis_index("y")
    d = x.shape[-1]
    meta_cols = meta_tokens.shape[1]

    early_first = _group_codec([(n, 0) for n in EARLY_WEIGHTS], weights)
    riding = [_group_codec([(n, 0) for n in LATE_WEIGHTS] + [(n, 1) for n in EARLY_WEIGHTS], weights),
              _group_codec([(n, 1) for n in LATE_WEIGHTS], weights)]
    gathered, meta_all = _all_gather_shards(early_first["packed"].reshape(2, -1, PACK_COLS), meta_tokens)
    early_weights = early_first["unpack"](gathered.reshape(N_CHIPS, -1, PACK_COLS))
    meta_full = jnp.concatenate([meta_all[j] for j in range(N_CHIPS)], axis=1)
    small = {n: weights[n] for n in SMALL_WEIGHTS}

    loss, grad_x, d_meta_rows, g_small, early_partial, parts = _local_step(
        x, loss_target, meta_full, small, {n: early_weights[n, 0] for n in EARLY_WEIGHTS}, riding, early_first)

    shards = early_first["unpack_reduced"](_reduce_scatter(early_partial, core))
    for codec, from_all in zip(riding, parts):
        shards.update(codec["unpack_reduced"](_pair_gather(_chip_sum(from_all))))
    grads = {n: jnp.stack([shards[n, 0], shards[n, 1]]) for n in BIG_WEIGHTS}

    small_shapes = [weights[n].shape for n in SMALL_WEIGHTS]
    small_flat = jnp.concatenate([g_small[n].reshape(-1) for n in SMALL_WEIGHTS])
    small_len = small_flat.shape[0]
    small_rows = _round_up(-(-small_len // PACK_COLS), 8)
    small_pack = jnp.pad(small_flat, (0, small_rows * PACK_COLS - small_len)).reshape(small_rows, PACK_COLS)
    meta_rows = N_META * d // PACK_COLS
    summed = _all_reduce_small(d_meta_rows.reshape(-1, meta_rows, PACK_COLS), small_pack)
    grad_meta_full = summed[:meta_rows].reshape(N_META, d)
    grads["meta_tokens"] = lax.dynamic_slice_in_dim(grad_meta_full, chip * meta_cols, meta_cols, axis=1)
    small_sum = summed[meta_rows:].reshape(-1)
    start = 0
    for n, shape in zip(SMALL_WEIGHTS, small_shapes):
        size = 1
        for s in shape:
            size *= s
        grads[n] = small_sum[start:start + size].reshape(shape)
        start += size

    deltas, new_m, new_v = {}, {}, {}
    for n in WEIGHT_ORDER:
        deltas[n], new_m[n], new_v[n] = _adamw(weights[n], grads[n], first[n], second[n])

    total_loss = lax.psum(loss[0, 0], ("x", "y", "c"))
    return (total_loss, grad_x, *[grads[n] for n in WEIGHT_ORDER], *[deltas[n] for n in WEIGHT_ORDER],
            *[new_m[n] for n in WEIGHT_ORDER], *[new_v[n] for n in WEIGHT_ORDER])
```
